```python
import math
import jax, jax.numpy as jnp
from jax import lax
import numpy as np

D_MODEL = 1024
BATCH = 8
SEQ = 8192
DEPTH = 2

CHUNK = 64
N_LEFT_CHUNKS = 8
BAND_CHUNKS = N_LEFT_CHUNKS + 1
N_HEADS = 16
HEAD_DIM = D_MODEL // N_HEADS
MAX_REL = 2 * CHUNK
N_REL = 2 * MAX_REL + 1
CONV_WIDTH = 3
D_FF = ((8 * D_MODEL // 3 + 255) // 256) * 256
N_A = DEPTH // 2
N_B = DEPTH - N_A
EPS = 1e-6

kernel_name = "yoco_shortconv_chunkattn_sandwich_adaln"


def rms_norm(x, g):
    xf = x.astype(jnp.float32)
    y = xf * lax.rsqrt(jnp.mean(xf * xf, axis=-1, keepdims=True) + EPS)
    return (y * g.astype(jnp.float32)).astype(x.dtype)


def modulate(h, shift, scale):
    return h * (1.0 + scale[:, None, :]) + shift[:, None, :]


def short_conv_mixer(h, w_in, conv_k, w_out):
    S = h.shape[1]
    bcx = h @ w_in
    b_gate, c_gate, xin = jnp.split(bcx, 3, axis=-1)
    z = c_gate * xin
    zp = jnp.pad(z, ((0, 0), (CONV_WIDTH - 1, 0), (0, 0)))
    conv = sum(conv_k[k] * zp[:, k:k + S] for k in range(CONV_WIDTH))
    return (b_gate * conv) @ w_out


def gather_band(t):
    Bsz, S = t.shape[0], t.shape[1]
    nc = S // CHUNK
    tc = t.reshape(Bsz, nc, CHUNK, N_HEADS, HEAD_DIM)
    tp = jnp.pad(tc, ((0, 0), (N_LEFT_CHUNKS, 0), (0, 0), (0, 0), (0, 0)))
    idx = jnp.arange(nc)[:, None] + jnp.arange(BAND_CHUNKS)[None, :]
    band = tp[:, idx]
    return band.reshape(Bsz, nc, BAND_CHUNKS * CHUNK, N_HEADS, HEAD_DIM)


def chunk_band_attention(h, k_band, v_band, w_q, w_o, rel_bias):
    Bsz, S, _ = h.shape
    nc = S // CHUNK
    q = (h @ w_q).reshape(Bsz, nc, CHUNK, N_HEADS, HEAD_DIM)
    scores = jnp.einsum('bnqhd,bnkhd->bhnqk', q, k_band).astype(jnp.float32)
    scores = scores * (HEAD_DIM ** -0.5)
    a = jnp.arange(CHUNK)[:, None]
    kk = jnp.arange(BAND_CHUNKS * CHUNK)[None, :]
    j, b = kk // CHUNK, kk % CHUNK
    rel = (N_LEFT_CHUNKS - j) * CHUNK + a - b
    rel_idx = jnp.clip(rel, -MAX_REL, MAX_REL) + MAX_REL
    bias = rel_bias.astype(jnp.float32)[:, rel_idx]
    scores = scores + bias[None, :, None]
    key_chunk = jnp.arange(nc)[:, None] + (jnp.arange(BAND_CHUNKS * CHUNK)[None, :] // CHUNK) - N_LEFT_CHUNKS
    valid = key_chunk >= 0
    scores = jnp.where(valid[None, None, :, None, :], scores, jnp.finfo(jnp.float32).min)
    p = jax.nn.softmax(scores, axis=-1).astype(v_band.dtype)
    o = jnp.einsum('bhnqk,bnkhd->bnqhd', p, v_band)
    return o.reshape(Bsz, S, D_MODEL) @ w_o


def swiglu(h, w_in, w_out):
    gu = h @ w_in
    g, u = jnp.split(gu, 2, axis=-1)
    return (jax.nn.silu(g) * u) @ w_out


def _fwd_setup_inputs(seed: int = 0) -> dict:
    key = jax.random.key(seed)
    ks = jax.random.split(key, 20)
    nrm = lambda k, shape, fan: jax.random.normal(k, shape, jnp.float32) * (fan ** -0.5)
    D = D_MODEL
    return {
        "x": jax.random.normal(ks[0], (BATCH, SEQ, D), jnp.float32),
        "c": jax.random.normal(ks[1], (BATCH, D), jnp.float32),
        "mod_w": nrm(ks[2], (DEPTH, D, 6 * D), D) * 0.3,
        "mod_b": 0.05 * jax.random.normal(ks[3], (DEPTH, 6 * D), jnp.float32),
        "norm_g": 1.0 + 0.05 * jax.random.normal(ks[4], (DEPTH, 4, D), jnp.float32),
        "ffn_w_in": nrm(ks[5], (DEPTH, D, 2 * D_FF), D),
        "ffn_w_out": nrm(ks[6], (DEPTH, D_FF, D), D_FF),
        "conv_w_in": nrm(ks[7], (N_A, D, 3 * D), D),
        "conv_k": nrm(ks[8], (N_A, CONV_WIDTH, D), CONV_WIDTH),
        "conv_w_out": nrm(ks[9], (N_A, D, D), D),
        "kv_mod_w": nrm(ks[10], (D, 2 * D), D) * 0.3,
        "kv_mod_b": 0.05 * jax.random.normal(ks[11], (2 * D,), jnp.float32),
        "kv_norm_g": 1.0 + 0.05 * jax.random.normal(ks[12], (D,), jnp.float32),
        "w_kv": nrm(ks[13], (D, 2 * D), D),
        "attn_w_q": nrm(ks[14], (N_B, D, D), D),
        "attn_w_o": nrm(ks[15], (N_B, D, D), D),
        "rel_bias": 0.5 * jax.random.normal(ks[16], (N_B, N_HEADS, N_REL), jnp.float32),
    }


def _fwd_reference(x, c, mod_w, mod_b, norm_g, ffn_w_in, ffn_w_out, conv_w_in, conv_k,
              conv_w_out, kv_mod_w, kv_mod_b, kv_norm_g, w_kv, attn_w_q, attn_w_o,
              rel_bias):
    Bsz, S, _ = x.shape
    silu_c = jax.nn.silu(c)
    k_band = None
    v_band = None
    for layer in range(DEPTH):
        mod = silu_c @ mod_w[layer] + mod_b[layer]
        sh1, sc1, g1, sh2, sc2, g2 = jnp.split(mod, 6, axis=-1)
        h = modulate(rms_norm(x, norm_g[layer, 0]), sh1, sc1)
        if layer < N_A:
            y = short_conv_mixer(h, conv_w_in[layer], conv_k[layer], conv_w_out[layer])
        else:
            if layer == N_A:
                kv_sh, kv_sc = jnp.split(silu_c @ kv_mod_w + kv_mod_b, 2, axis=-1)
                hkv = modulate(rms_norm(x, kv_norm_g), kv_sh, kv_sc)
                k, v = jnp.split(hkv @ w_kv, 2, axis=-1)
                k_band = gather_band(k.reshape(Bsz, S, N_HEADS, HEAD_DIM))
                v_band = gather_band(v.reshape(Bsz, S, N_HEADS, HEAD_DIM))
            bi = layer - N_A
            y = chunk_band_attention(h, k_band, v_band, attn_w_q[bi], attn_w_o[bi], rel_bias[bi])
        x = x + g1[:, None, :] * rms_norm(y, norm_g[layer, 1])
        h = modulate(rms_norm(x, norm_g[layer, 2]), sh2, sc2)
        y = swiglu(h, ffn_w_in[layer], ffn_w_out[layer])
        x = x + g2[:, None, :] * rms_norm(y, norm_g[layer, 3])
    return x


import jax as _jax
import jax.numpy as _jnp

TWIN_FORMAT = 'train_step'
FWD_PARAMS = ['x', 'c', 'mod_w', 'mod_b', 'norm_g', 'ffn_w_in', 'ffn_w_out', 'conv_w_in', 'conv_k', 'conv_w_out', 'kv_mod_w', 'kv_mod_b', 'kv_norm_g', 'w_kv', 'attn_w_q', 'attn_w_o', 'rel_bias']
TWIN_WEIGHTS = ['mod_w', 'mod_b', 'norm_g', 'ffn_w_in', 'ffn_w_out', 'conv_w_in', 'conv_k', 'conv_w_out', 'kv_mod_w', 'kv_mod_b', 'kv_norm_g', 'w_kv', 'attn_w_q', 'attn_w_o', 'rel_bias']
TWIN_DIFF_INPUT = 'x'
TWIN_INPUTS = ['x', 'c', 'mod_w', 'mod_b', 'norm_g', 'ffn_w_in', 'ffn_w_out', 'conv_w_in', 'conv_k', 'conv_w_out', 'kv_mod_w', 'kv_mod_b', 'kv_norm_g', 'w_kv', 'attn_w_q', 'attn_w_o', 'rel_bias', 'loss_target', 'm_mod_w', 'm_mod_b', 'm_norm_g', 'm_ffn_w_in', 'm_ffn_w_out', 'm_conv_w_in', 'm_conv_k', 'm_conv_w_out', 'm_kv_mod_w', 'm_kv_mod_b', 'm_kv_norm_g', 'm_w_kv', 'm_attn_w_q', 'm_attn_w_o', 'm_rel_bias', 'v_mod_w', 'v_mod_b', 'v_norm_g', 'v_ffn_w_in', 'v_ffn_w_out', 'v_conv_w_in', 'v_conv_k', 'v_conv_w_out', 'v_kv_mod_w', 'v_kv_mod_b', 'v_kv_norm_g', 'v_w_kv', 'v_attn_w_q', 'v_attn_w_o', 'v_rel_bias']
TWIN_OUTPUTS = ['loss', 'grad_x', 'grad_mod_w', 'grad_mod_b', 'grad_norm_g', 'grad_ffn_w_in', 'grad_ffn_w_out', 'grad_conv_w_in', 'grad_conv_k', 'grad_conv_w_out', 'grad_kv_mod_w', 'grad_kv_mod_b', 'grad_kv_norm_g', 'grad_w_kv', 'grad_attn_w_q', 'grad_attn_w_o', 'grad_rel_bias', 'delta_mod_w', 'delta_mod_b', 'delta_norm_g', 'delta_ffn_w_in', 'delta_ffn_w_out', 'delta_conv_w_in', 'delta_conv_k', 'delta_conv_w_out', 'delta_kv_mod_w', 'delta_kv_mod_b', 'delta_kv_norm_g', 'delta_w_kv', 'delta_attn_w_q', 'delta_attn_w_o', 'delta_rel_bias', 'new_m_mod_w', 'new_m_mod_b', 'new_m_norm_g', 'new_m_ffn_w_in', 'new_m_ffn_w_out', 'new_m_conv_w_in', 'new_m_conv_k', 'new_m_conv_w_out', 'new_m_kv_mod_w', 'new_m_kv_mod_b', 'new_m_kv_norm_g', 'new_m_w_kv', 'new_m_attn_w_q', 'new_m_attn_w_o', 'new_m_rel_bias', 'new_v_mod_w', 'new_v_mod_b', 'new_v_norm_g', 'new_v_ffn_w_in', 'new_v_ffn_w_out', 'new_v_conv_w_in', 'new_v_conv_k', 'new_v_conv_w_out', 'new_v_kv_mod_w', 'new_v_kv_mod_b', 'new_v_kv_norm_g', 'new_v_w_kv', 'new_v_attn_w_q', 'new_v_attn_w_o', 'new_v_rel_bias']
TWIN_LEAF_KINDS = {'loss': 'loss', 'grad_x': 'grad_x', 'grad_mod_w': 'grad_w', 'grad_mod_b': 'grad_w', 'grad_norm_g': 'grad_w', 'grad_ffn_w_in': 'grad_w', 'grad_ffn_w_out': 'grad_w', 'grad_conv_w_in': 'grad_w', 'grad_conv_k': 'grad_w', 'grad_conv_w_out': 'grad_w', 'grad_kv_mod_w': 'grad_w', 'grad_kv_mod_b': 'grad_w', 'grad_kv_norm_g': 'grad_w', 'grad_w_kv': 'grad_w', 'grad_attn_w_q': 'grad_w', 'grad_attn_w_o': 'grad_w', 'grad_rel_bias': 'grad_w', 'delta_mod_w': 'delta_w', 'delta_mod_b': 'delta_w', 'delta_norm_g': 'delta_w', 'delta_ffn_w_in': 'delta_w', 'delta_ffn_w_out': 'delta_w', 'delta_conv_w_in': 'delta_w', 'delta_conv_k': 'delta_w', 'delta_conv_w_out': 'delta_w', 'delta_kv_mod_w': 'delta_w', 'delta_kv_mod_b': 'delta_w', 'delta_kv_norm_g': 'delta_w', 'delta_w_kv': 'delta_w', 'delta_attn_w_q': 'delta_w', 'delta_attn_w_o': 'delta_w', 'delta_rel_bias': 'delta_w', 'new_m_mod_w': 'new_m', 'new_m_mod_b': 'new_m', 'new_m_norm_g': 'new_m', 'new_m_ffn_w_in': 'new_m', 'new_m_ffn_w_out': 'new_m', 'new_m_conv_w_in': 'new_m', 'new_m_conv_k': 'new_m', 'new_m_conv_w_out': 'new_m', 'new_m_kv_mod_w': 'new_m', 'new_m_kv_mod_b': 'new_m', 'new_m_kv_norm_g': 'new_m', 'new_m_w_kv': 'new_m', 'new_m_attn_w_q': 'new_m', 'new_m_attn_w_o': 'new_m', 'new_m_rel_bias': 'new_m', 'new_v_mod_w': 'new_v', 'new_v_mod_b': 'new_v', 'new_v_norm_g': 'new_v', 'new_v_ffn_w_in': 'new_v', 'new_v_ffn_w_out': 'new_v', 'new_v_conv_w_in': 'new_v', 'new_v_conv_k': 'new_v', 'new_v_conv_w_out': 'new_v', 'new_v_kv_mod_w': 'new_v', 'new_v_kv_mod_b': 'new_v', 'new_v_kv_norm_g': 'new_v', 'new_v_w_kv': 'new_v', 'new_v_attn_w_q': 'new_v', 'new_v_attn_w_o': 'new_v', 'new_v_rel_bias': 'new_v'}


def _forward(args):
    return _fwd_reference(*[args[k] for k in FWD_PARAMS])


def _output_shape():
    def fwd():
        inp = _fwd_setup_inputs(0)
        return _fwd_reference(*[inp[k] for k in FWD_PARAMS])
    out = _jax.eval_shape(fwd)
    return out.shape, out.dtype

N_MICROBATCH = 1
ADAM_LR = 0.001
ADAM_B1 = 0.9
ADAM_B2 = 0.999
ADAM_EPS = 1e-08
ADAM_WD = 0.01
ADAM_STEP = 10
PER_EXAMPLE_BATCH_AXIS = {'x': 0, 'c': 0, 'loss_target': 0}
SHARED_INPUTS = []
_WEIGHT_DTYPES = {'mod_w': _jnp.float32, 'mod_b': _jnp.float32, 'norm_g': _jnp.float32, 'ffn_w_in': _jnp.float32, 'ffn_w_out': _jnp.float32, 'conv_w_in': _jnp.float32, 'conv_k': _jnp.float32, 'conv_w_out': _jnp.float32, 'kv_mod_w': _jnp.float32, 'kv_mod_b': _jnp.float32, 'kv_norm_g': _jnp.float32, 'w_kv': _jnp.float32, 'attn_w_q': _jnp.float32, 'attn_w_o': _jnp.float32, 'rel_bias': _jnp.float32}
MOMENT_SCALE = {'mod_w': 1.792848e+00, 'mod_b': 4.001962e+00, 'norm_g': 1.959490e+00, 'ffn_w_in': 5.405599e-02, 'ffn_w_out': 9.944594e-02, 'conv_w_in': 7.700781e-02, 'conv_k': 8.190271e-02, 'conv_w_out': 8.152901e-02, 'kv_mod_w': 1.810374e+00, 'kv_mod_b': 3.143528e+00, 'kv_norm_g': 1.087837e-01, 'w_kv': 6.535645e-01, 'attn_w_q': 4.902918e-02, 'attn_w_o': 8.675612e-01, 'rel_bias': 2.216366e-02}


def _to_microbatches(a, axis):
    t = _jnp.moveaxis(a, axis, 0)
    t = t.reshape((N_MICROBATCH, t.shape[0] // N_MICROBATCH) + t.shape[1:])
    return _jnp.moveaxis(t, 1, axis + 1)


def setup_inputs(seed: int = 0) -> dict:
    inp = _fwd_setup_inputs(seed)
    key = _jax.random.fold_in(_jax.random.key(seed), 7919)
    shape, _ = _output_shape()
    out = dict(inp)
    out["loss_target"] = _jax.random.normal(_jax.random.fold_in(key, 0), shape, _jnp.float32)
    for i, name in enumerate(TWIN_WEIGHTS):
        w = inp[name].astype(_jnp.float32)
        if MOMENT_SCALE is None:
            s = _jnp.sqrt(_jnp.mean(_jnp.square(w)) + 1e-30)
        else:
            s = MOMENT_SCALE[name]
        km, kv = _jax.random.split(_jax.random.fold_in(key, i + 1))
        out[name] = w
        out["m_" + name] = s * _jax.random.normal(km, w.shape, _jnp.float32)
        out["v_" + name] = (s * s) * _jax.random.uniform(kv, w.shape, _jnp.float32, 0.5, 1.5)
    if N_MICROBATCH > 1:
        for name, axis in PER_EXAMPLE_BATCH_AXIS.items():
            out[name] = _to_microbatches(out[name], axis)
    return {'x': out['x'], 'c': out['c'], 'mod_w': out['mod_w'], 'mod_b': out['mod_b'], 'norm_g': out['norm_g'], 'ffn_w_in': out['ffn_w_in'], 'ffn_w_out': out['ffn_w_out'], 'conv_w_in': out['conv_w_in'], 'conv_k': out['conv_k'], 'conv_w_out': out['conv_w_out'], 'kv_mod_w': out['kv_mod_w'], 'kv_mod_b': out['kv_mod_b'], 'kv_norm_g': out['kv_norm_g'], 'w_kv': out['w_kv'], 'attn_w_q': out['attn_w_q'], 'attn_w_o': out['attn_w_o'], 'rel_bias': out['rel_bias'], 'loss_target': out['loss_target'], 'm_mod_w': out['m_mod_w'], 'm_mod_b': out['m_mod_b'], 'm_norm_g': out['m_norm_g'], 'm_ffn_w_in': out['m_ffn_w_in'], 'm_ffn_w_out': out['m_ffn_w_out'], 'm_conv_w_in': out['m_conv_w_in'], 'm_conv_k': out['m_conv_k'], 'm_conv_w_out': out['m_conv_w_out'], 'm_kv_mod_w': out['m_kv_mod_w'], 'm_kv_mod_b': out['m_kv_mod_b'], 'm_kv_norm_g': out['m_kv_norm_g'], 'm_w_kv': out['m_w_kv'], 'm_attn_w_q': out['m_attn_w_q'], 'm_attn_w_o': out['m_attn_w_o'], 'm_rel_bias': out['m_rel_bias'], 'v_mod_w': out['v_mod_w'], 'v_mod_b': out['v_mod_b'], 'v_norm_g': out['v_norm_g'], 'v_ffn_w_in': out['v_ffn_w_in'], 'v_ffn_w_out': out['v_ffn_w_out'], 'v_conv_w_in': out['v_conv_w_in'], 'v_conv_k': out['v_conv_k'], 'v_conv_w_out': out['v_conv_w_out'], 'v_kv_mod_w': out['v_kv_mod_w'], 'v_kv_mod_b': out['v_kv_mod_b'], 'v_kv_norm_g': out['v_kv_norm_g'], 'v_w_kv': out['v_w_kv'], 'v_attn_w_q': out['v_attn_w_q'], 'v_attn_w_o': out['v_attn_w_o'], 'v_rel_bias': out['v_rel_bias']}


def _loss(weights, diff, rest, loss_target):
    with _jax.named_scope("forward"):
        args = {**rest, TWIN_DIFF_INPUT: diff, **{k: w.astype(_WEIGHT_DTYPES[k]) for k, w in weights.items()}}
        y = _forward(args)
    with _jax.named_scope("loss_head"):
        err = _jnp.square(y.astype(_jnp.float32) - loss_target)
        return 0.5 * _jnp.sum(_jnp.mean(err, axis=-1)) if err.ndim else 0.5 * err


def _adamw(w, g, m, v):
    m = ADAM_B1 * m + (1.0 - ADAM_B1) * g
    v = ADAM_B2 * v + (1.0 - ADAM_B2) * _jnp.square(g)
    m_hat = m / (1.0 - ADAM_B1 ** ADAM_STEP)
    v_hat = v / (1.0 - ADAM_B2 ** ADAM_STEP)
    delta = -ADAM_LR * (m_hat / (_jnp.sqrt(v_hat) + ADAM_EPS) + ADAM_WD * w)
    return delta, m, v


def reference(x, c, mod_w, mod_b, norm_g, ffn_w_in, ffn_w_out, conv_w_in, conv_k, conv_w_out, kv_mod_w, kv_mod_b, kv_norm_g, w_kv, attn_w_q, attn_w_o, rel_bias, loss_target, m_mod_w, m_mod_b, m_norm_g, m_ffn_w_in, m_ffn_w_out, m_conv_w_in, m_conv_k, m_conv_w_out, m_kv_mod_w, m_kv_mod_b, m_kv_norm_g, m_w_kv, m_attn_w_q, m_attn_w_o, m_rel_bias, v_mod_w, v_mod_b, v_norm_g, v_ffn_w_in, v_ffn_w_out, v_conv_w_in, v_conv_k, v_conv_w_out, v_kv_mod_w, v_kv_mod_b, v_kv_norm_g, v_w_kv, v_attn_w_q, v_attn_w_o, v_rel_bias):
    given = dict(x=x, c=c, mod_w=mod_w, mod_b=mod_b, norm_g=norm_g, ffn_w_in=ffn_w_in, ffn_w_out=ffn_w_out, conv_w_in=conv_w_in, conv_k=conv_k, conv_w_out=conv_w_out, kv_mod_w=kv_mod_w, kv_mod_b=kv_mod_b, kv_norm_g=kv_norm_g, w_kv=w_kv, attn_w_q=attn_w_q, attn_w_o=attn_w_o, rel_bias=rel_bias, loss_target=loss_target, m_mod_w=m_mod_w, m_mod_b=m_mod_b, m_norm_g=m_norm_g, m_ffn_w_in=m_ffn_w_in, m_ffn_w_out=m_ffn_w_out, m_conv_w_in=m_conv_w_in, m_conv_k=m_conv_k, m_conv_w_out=m_conv_w_out, m_kv_mod_w=m_kv_mod_w, m_kv_mod_b=m_kv_mod_b, m_kv_norm_g=m_kv_norm_g, m_w_kv=m_w_kv, m_attn_w_q=m_attn_w_q, m_attn_w_o=m_attn_w_o, m_rel_bias=m_rel_bias, v_mod_w=v_mod_w, v_mod_b=v_mod_b, v_norm_g=v_norm_g, v_ffn_w_in=v_ffn_w_in, v_ffn_w_out=v_ffn_w_out, v_conv_w_in=v_conv_w_in, v_conv_k=v_conv_k, v_conv_w_out=v_conv_w_out, v_kv_mod_w=v_kv_mod_w, v_kv_mod_b=v_kv_mod_b, v_kv_norm_g=v_kv_norm_g, v_w_kv=v_w_kv, v_attn_w_q=v_attn_w_q, v_attn_w_o=v_attn_w_o, v_rel_bias=v_rel_bias)
    weights = {n: given[n] for n in TWIN_WEIGHTS}
    shared = {n: given[n] for n in SHARED_INPUTS}
    per_example = {n: given[n] for n in ['x', 'c']}
    grad_fn = _jax.value_and_grad(_loss, argnums=(0, 1))

    def one_microbatch(ex, loss_target):
        ex = dict(ex)
        diff = ex.pop(TWIN_DIFF_INPUT)
        return grad_fn(weights, diff, {**shared, **ex}, loss_target)

    if N_MICROBATCH == 1:
        loss, (grad_w, grad_x) = one_microbatch(per_example, given["loss_target"])
    else:
        def body(carry, xs):
            loss_sum, grad_sum = carry
            l_k, (gw_k, gx_k) = one_microbatch(xs[0], xs[1])
            with _jax.named_scope("update"):
                return (loss_sum + l_k, _jax.tree.map(_jnp.add, grad_sum, gw_k)), gx_k

        init = (_jnp.zeros((), _jnp.float32), _jax.tree.map(_jnp.zeros_like, weights))
        (loss, grad_w), grad_x = _jax.lax.scan(body, init, (per_example, given["loss_target"]))
    with _jax.named_scope("update"):
        delta_w, new_m, new_v = {}, {}, {}
        for n in TWIN_WEIGHTS:
            delta_w[n], new_m[n], new_v[n] = _adamw(weights[n], grad_w[n], given["m_" + n], given["v_" + n])
    return (loss, grad_x, *[grad_w[n] for n in TWIN_WEIGHTS], *[delta_w[n] for n in TWIN_WEIGHTS],
            *[new_m[n] for n in TWIN_WEIGHTS], *[new_v[n] for n in TWIN_WEIGHTS])
```

```python
import functools

import jax
import jax.numpy as jnp
from jax import lax
from jax.experimental import pallas as pl
from jax.experimental.pallas import tpu as pltpu

F32 = jnp.float32
BF16 = jnp.bfloat16

EPS = 1e-6
CHUNK = 64
HEAD_DIM = 64
N_LEFT = 8
LANES = 128
QB = 4 * CHUNK
KW = QB + N_LEFT * CHUNK
NEG = -1e30
N_DEV = 8

ADAM_LR = 0.001
ADAM_B1 = 0.9
ADAM_B2 = 0.999
ADAM_EPS = 1e-08
ADAM_WD = 0.01
ADAM_STEP = 10

VMEM_BIG = 56 * 1024 * 1024

NT = (((1,), (1,)), ((), ()))
TN = (((0,), (0,)), ((), ()))

R_W1, R_SH1, R_P1, R_W2, R_SH2, R_P2 = range(6)
R_KV = 12


def _params(vmem):
    return pltpu.CompilerParams(vmem_limit_bytes=vmem)


def _row_tile(rows, cap):
    for t in range(min(cap, rows) // 16 * 16, 0, -16):
        if rows % t == 0:
            return t
    return rows


def _rows(tm, cols):
    return pl.BlockSpec((tm, cols), lambda i: (i, 0))


def _const(shape):
    nd = len(shape)
    return pl.BlockSpec(shape, lambda *_: (0,) * nd, pipeline_mode=pl.Buffered(1))


def _rs(x):
    return lax.rsqrt(jnp.mean(x * x, axis=-1, keepdims=True) + EPS)


def _norm_bwd(d, n, r):
    return r * (d - n * jnp.mean(d * n, axis=-1, keepdims=True))


def _colsum(a):
    return jnp.sum(a, axis=0, keepdims=True)


def _sigmoid(g):
    return 1.0 / (1.0 + jnp.exp(-g))


def _exchange(arrays, modes, name):
    n = len(arrays)
    out_shape = []
    for a, mode in zip(arrays, modes):
        shp = (N_DEV,) + a.shape if mode == "gather" else a.shape
        out_shape.append(jax.ShapeDtypeStruct(shp, a.dtype))

    def body(*refs):
        ins, outs = refs[:n], refs[n:2 * n]
        send_sems, recv_sems, local_sems = refs[2 * n:]
        x, y, c = lax.axis_index("x"), lax.axis_index("y"), lax.axis_index("c")
        me = 4 * x + 2 * y + c
        local, sends, recvs = [], [], []
        for a in range(n):
            own = ins[a] if modes[a] == "gather" else ins[a].at[me]
            cp = pltpu.make_async_copy(own, outs[a].at[me], local_sems.at[a])
            cp.start()
            local.append(cp)
        for k in range(1, N_DEV):
            px = 1 - x if k & 4 else x
            py = 1 - y if k & 2 else y
            pc = 1 - c if k & 1 else c
            peer = 4 * px + 2 * py + pc
            for a in range(n):
                src = ins[a] if modes[a] == "gather" else ins[a].at[peer]
                sem = a * (N_DEV - 1) + k - 1
                cp = pltpu.make_async_remote_copy(
                    src_ref=src, dst_ref=outs[a].at[me],
                    send_sem=send_sems.at[sem], recv_sem=recv_sems.at[sem],
                    device_id=(px, py, pc), device_id_type=pl.DeviceIdType.MESH)
                cp.start()
                sends.append(cp)
                recvs.append(pltpu.make_async_remote_copy(
                    src_ref=src, dst_ref=outs[a].at[peer],
                    send_sem=send_sems.at[sem], recv_sem=recv_sems.at[sem],
                    device_id=(px, py, pc), device_id_type=pl.DeviceIdType.MESH))
        for cp in recvs:
            cp.wait_recv()
        for cp in sends:
            cp.wait_send()
        for cp in local:
            cp.wait()

    any_spec = pl.BlockSpec(memory_space=pl.ANY)
    return pl.pallas_call(
        body, name=name,
        out_shape=tuple(out_shape),
        in_specs=[any_spec] * n,
        out_specs=tuple([any_spec] * n),
        scratch_shapes=[
            pltpu.SemaphoreType.DMA((n * (N_DEV - 1),)),
            pltpu.SemaphoreType.DMA((n * (N_DEV - 1),)),
            pltpu.SemaphoreType.DMA((n,)),
        ],
    )(*arrays)


def _mod_fwd(c_all, mod_w, kv_mod_w):
    nl, d, mw = mod_w.shape
    kw = kv_mod_w.shape[1]

    def body(c_ref, mw_ref, kw_ref, o_ref, sc_ref):
        cc = c_ref[...]
        sc = (cc * _sigmoid(cc)).astype(BF16)
        sc_ref[...] = sc
        for l in range(nl):
            o_ref[:, l * mw:(l + 1) * mw] = jnp.dot(sc, mw_ref[l].astype(BF16), preferred_element_type=F32)
        o_ref[:, nl * mw:nl * mw + kw] = jnp.dot(sc, kw_ref[...].astype(BF16), preferred_element_type=F32)

    return pl.pallas_call(
        body, name="mod_fwd",
        out_shape=(jax.ShapeDtypeStruct((c_all.shape[0], nl * mw + kw), F32),
                   jax.ShapeDtypeStruct(c_all.shape, BF16)),
        compiler_params=_params(VMEM_BIG),
    )(c_all, mod_w, kv_mod_w)


def _vec_prep(modrow, modb, kvrow, kvb, ng, kvg):
    d = ng.shape[1]

    def body(mr_ref, mb_ref, kr_ref, kb_ref, ng_ref, kvg_ref, t_ref, m_ref):
        t_ref[...] = jnp.zeros_like(t_ref)
        m_ref[...] = jnp.zeros_like(m_ref)
        for l in range(2):
            mod = mr_ref[l] + mb_ref[l]
            m_ref[6 * l:6 * l + 6, :] = mod
            g = ng_ref[4 * l:4 * l + 4, :]
            t_ref[6 * l + R_W1:6 * l + R_W1 + 1, :] = g[0:1] * (1.0 + mod[1:2])
            t_ref[6 * l + R_SH1:6 * l + R_SH1 + 1, :] = mod[0:1]
            t_ref[6 * l + R_P1:6 * l + R_P1 + 1, :] = mod[2:3] * g[1:2]
            t_ref[6 * l + R_W2:6 * l + R_W2 + 1, :] = g[2:3] * (1.0 + mod[4:5])
            t_ref[6 * l + R_SH2:6 * l + R_SH2 + 1, :] = mod[3:4]
            t_ref[6 * l + R_P2:6 * l + R_P2 + 1, :] = mod[5:6] * g[3:4]
        kv = kr_ref[...] + kb_ref[...]
        m_ref[R_KV:R_KV + 2, :] = kv
        t_ref[R_KV:R_KV + 1, :] = kvg_ref[...] * (1.0 + kv[1:2])
        t_ref[R_KV + 1:R_KV + 2, :] = kv[0:1]

    return pl.pallas_call(
        body, name="vec_prep",
        out_shape=(jax.ShapeDtypeStruct((16, d), F32), jax.ShapeDtypeStruct((16, d), F32)),
    )(modrow, modb, kvrow, kvb, ng, kvg)


def _vec_bwd(sums_c, sums_f0, sums_q, sums_o, sums_f1, mt, ng, kvg):
    d = ng.shape[1]

    def body(sc_ref, sf0_ref, sq_ref, so_ref, sf1_ref, m_ref, ng_ref, kvg_ref, dm_ref, dng_ref, dkvg_ref, g_ref):
        g_ref[...] = jnp.zeros_like(g_ref)
        g_ref[0:3, :] = sc_ref[0:3, :]
        g_ref[3:6, :] = sf0_ref[3:6, :]
        g_ref[6:8, :] = sq_ref[0:2, :]
        g_ref[8:9, :] = so_ref[2:3, :]
        g_ref[9:12, :] = sf1_ref[3:6, :]
        g_ref[R_KV:R_KV + 2, :] = sq_ref[2:4, :]
        dm_ref[...] = jnp.zeros_like(dm_ref)
        dkvg_ref[...] = jnp.zeros_like(dkvg_ref)
        for l in range(2):
            g = ng_ref[4 * l:4 * l + 4, :]
            mod = m_ref[6 * l:6 * l + 6, :]
            s = g_ref[6 * l:6 * l + 6, :]
            dm_ref[6 * l + 0:6 * l + 1, :] = s[1:2]
            dm_ref[6 * l + 1:6 * l + 2, :] = s[0:1] * g[0:1]
            dm_ref[6 * l + 2:6 * l + 3, :] = s[2:3] * g[1:2]
            dm_ref[6 * l + 3:6 * l + 4, :] = s[4:5]
            dm_ref[6 * l + 4:6 * l + 5, :] = s[3:4] * g[2:3]
            dm_ref[6 * l + 5:6 * l + 6, :] = s[5:6] * g[3:4]
            dng_ref[4 * l + 0:4 * l + 1, :] = s[0:1] * (1.0 + mod[1:2])
            dng_ref[4 * l + 1:4 * l + 2, :] = s[2:3] * mod[2:3]
            dng_ref[4 * l + 2:4 * l + 3, :] = s[3:4] * (1.0 + mod[4:5])
            dng_ref[4 * l + 3:4 * l + 4, :] = s[5:6] * mod[5:6]
        dm_ref[R_KV:R_KV + 1, :] = g_ref[R_KV + 1:R_KV + 2, :]
        dm_ref[R_KV + 1:R_KV + 2, :] = g_ref[R_KV:R_KV + 1, :] * kvg_ref[...]
        dkvg_ref[0:1, :] = g_ref[R_KV:R_KV + 1, :] * (1.0 + m_ref[R_KV + 1:R_KV + 2, :])

    return pl.pallas_call(
        body, name="vec_bwd",
        out_shape=(jax.ShapeDtypeStruct((16, d), F32), jax.ShapeDtypeStruct((8, d), F32),
                   jax.ShapeDtypeStruct((8, d), F32)),
        scratch_shapes=[pltpu.VMEM((16, d), F32)],
    )(sums_c, sums_f0, sums_q, sums_o, sums_f1, mt, ng, kvg)


def _rel_index(nrel):
    width = KW + QB
    e = lax.broadcasted_iota(jnp.int32, (nrel, width), 1)
    r = lax.broadcasted_iota(jnp.int32, (nrel, width), 0)
    max_rel = (nrel - 1) // 2
    idx = jnp.clip(KW - e, -max_rel, max_rel) + max_rel
    return (idx == r).astype(F32)


def _band_valid():
    row = lax.broadcasted_iota(jnp.int32, (QB, KW), 0) // CHUNK
    col = lax.broadcasted_iota(jnp.int32, (QB, KW), 1) // CHUNK
    j = col - row
    return (j >= 0) & (j <= N_LEFT)


def _bias_fwd(rel_bias):
    nh, nrel = rel_bias.shape
    width = KW + QB

    def body(rb_ref, o_ref):
        onehot = _rel_index(nrel)
        gr = jnp.dot(rb_ref[...], onehot, preferred_element_type=F32, precision=lax.Precision.HIGHEST)
        valid = _band_valid()
        for h in range(nh):
            xrow = jnp.broadcast_to(gr[h:h + 1, :], (QB, width))
            rolled = pltpu.roll(xrow, 0, 1, stride=1, stride_axis=0)
            o_ref[h] = jnp.where(valid, rolled[:, QB:], NEG)

    return pl.pallas_call(
        body, name="bias_fwd",
        out_shape=jax.ShapeDtypeStruct((nh, QB, KW), F32),
        compiler_params=_params(VMEM_BIG),
    )(rel_bias)


def _bias_bwd(dbias, nrel):
    nh = dbias.shape[0]
    width = KW + QB

    def body(db_ref, o_ref, diag_ref):
        onehot = _rel_index(nrel)
        valid = _band_valid()
        rr = lax.broadcasted_iota(jnp.int32, (QB, QB), 0)
        cc = lax.broadcasted_iota(jnp.int32, (QB, QB), 1)
        flip = (rr + cc == QB - 1).astype(F32)
        for h in range(nh):
            rev = jnp.dot(flip, jnp.where(valid, db_ref[h], 0.0), preferred_element_type=F32,
                          precision=lax.Precision.HIGHEST)
            w = jnp.concatenate([jnp.zeros((QB, QB), F32), rev], axis=1)
            back = pltpu.roll(w, width - (QB - 1), 1, stride=1, stride_axis=0)
            diag_ref[h:h + 1, :] = _colsum(back)
        o_ref[...] = lax.dot_general(diag_ref[...], onehot, NT, preferred_element_type=F32,
                                     precision=lax.Precision.HIGHEST)

    return pl.pallas_call(
        body, name="bias_bwd",
        out_shape=jax.ShapeDtypeStruct((nh, nrel), F32),
        scratch_shapes=[pltpu.VMEM((nh, width), F32)],
        compiler_params=_params(VMEM_BIG),
    )(dbias)


def _conv_fwd(x, tab, ck, wci, wco, tm):
    s, d = x.shape
    nsh, _, cw = wci.shape

    def body(x_ref, t_ref, ck_ref, wci_ref, wco_ref, x1_ref, h_ref, bcx_ref, u_ref, y_ref, carry):
        @pl.when(pl.program_id(0) == 0)
        def _():
            carry[...] = jnp.zeros_like(carry)

        xv = x_ref[...]
        hb = ((xv * _rs(xv)) * t_ref[R_W1:R_W1 + 1, :] + t_ref[R_SH1:R_SH1 + 1, :]).astype(BF16)
        h_ref[...] = hb
        for j in range(nsh):
            bcx_ref[:, j * cw:(j + 1) * cw] = jnp.dot(hb, wci_ref[j], preferred_element_type=F32)
        bg, cg, xi = bcx_ref[:, 0:d], bcx_ref[:, d:2 * d], bcx_ref[:, 2 * d:3 * d]
        z = cg * xi
        row = lax.broadcasted_iota(jnp.int32, z.shape, 0)
        c1, c2 = carry[7:8, :], carry[6:7, :]
        z1 = jnp.where(row == 0, c1, pltpu.roll(z, 1, 0))
        z2 = jnp.where(row == 0, c2, jnp.where(row == 1, c1, pltpu.roll(z, 2, 0)))
        carry[...] = z[tm - 8:tm, :]
        conv = ck_ref[0:1, :] * z2 + ck_ref[1:2, :] * z1 + ck_ref[2:3, :] * z
        ub = (bg * conv).astype(BF16)
        u_ref[...] = ub
        yv = jnp.dot(ub, wco_ref[...], preferred_element_type=F32)
        y_ref[...] = yv
        x1_ref[...] = xv + (yv * _rs(yv)) * t_ref[R_P1:R_P1 + 1, :]

    return pl.pallas_call(
        body, name="conv_fwd", grid=(s // tm,),
        in_specs=[_rows(tm, d), _const(tab.shape), _const(ck.shape), _const(wci.shape), _const(wco.shape)],
        out_specs=(_rows(tm, d), _rows(tm, d), _rows(tm, 3 * d), _rows(tm, d), _rows(tm, d)),
        out_shape=(jax.ShapeDtypeStruct((s, d), F32), jax.ShapeDtypeStruct((s, d), BF16),
                   jax.ShapeDtypeStruct((s, 3 * d), F32), jax.ShapeDtypeStruct((s, d), BF16),
                   jax.ShapeDtypeStruct((s, d), F32)),
        scratch_shapes=[pltpu.VMEM((8, d), F32)],
        compiler_params=_params(VMEM_BIG),
    )(x, tab, ck, wci, wco)


def _ffn_fwd(x, tab, base, wfi, wfo, tgt, tm, name):
    s, d = x.shape
    nsh, _, fw = wfi.shape
    nh = nsh // 2
    with_loss = tgt is not None

    def body(*refs):
        if with_loss:
            x_ref, t_ref, wfi_ref, wfo_ref, tgt_ref, xo_ref, h_ref, gu_ref, a_ref, y_ref, loss_ref = refs
        else:
            x_ref, t_ref, wfi_ref, wfo_ref, xo_ref, h_ref, gu_ref, a_ref, y_ref = refs
        xv = x_ref[...]
        hb = ((xv * _rs(xv)) * t_ref[base + R_W2:base + R_W2 + 1, :]
              + t_ref[base + R_SH2:base + R_SH2 + 1, :]).astype(BF16)
        h_ref[...] = hb
        acc = jnp.zeros((tm, d), F32)
        for j in range(nh):
            g = jnp.dot(hb, wfi_ref[j], preferred_element_type=F32)
            u = jnp.dot(hb, wfi_ref[j + nh], preferred_element_type=F32)
            gu_ref[j] = g
            gu_ref[j + nh] = u
            ab = ((g * _sigmoid(g)) * u).astype(BF16)
            a_ref[j] = ab
            acc = acc + jnp.dot(ab, wfo_ref[j], preferred_element_type=F32)
        y_ref[...] = acc
        xo = xv + (acc * _rs(acc)) * t_ref[base + R_P2:base + R_P2 + 1, :]
        if with_loss:
            @pl.when(pl.program_id(0) == 0)
            def _():
                loss_ref[...] = jnp.zeros_like(loss_ref)

            err = xo - tgt_ref[...]
            xo_ref[...] = err * (1.0 / d)
            loss_ref[...] += jnp.sum(err * err)
        else:
            xo_ref[...] = xo

    in_specs = [_rows(tm, d), _const(tab.shape), _const(wfi.shape), _const(wfo.shape)]
    args = [x, tab, wfi, wfo]
    out_specs = [_rows(tm, d), _rows(tm, d), pl.BlockSpec((nsh, tm, fw), lambda i: (0, i, 0)),
                 pl.BlockSpec((nh, tm, fw), lambda i: (0, i, 0)), _rows(tm, d)]
    out_shape = [jax.ShapeDtypeStruct((s, d), F32), jax.ShapeDtypeStruct((s, d), BF16),
                 jax.ShapeDtypeStruct((nsh, s, fw), F32), jax.ShapeDtypeStruct((nh, s, fw), BF16),
                 jax.ShapeDtypeStruct((s, d), F32)]
    if with_loss:
        in_specs.append(_rows(tm, d))
        args.append(tgt)
        out_specs.append(pl.BlockSpec((8, LANES), lambda i: (0, 0)))
        out_shape.append(jax.ShapeDtypeStruct((8, LANES), F32))
    return pl.pallas_call(
        body, name=name, grid=(s // tm,), in_specs=in_specs, out_specs=tuple(out_specs),
        out_shape=tuple(out_shape), compiler_params=_params(VMEM_BIG),
    )(*args)


def _qkv_fwd(x, tab, wq, wkv, tm):
    s, d = x.shape
    nsh, _, kw = wkv.shape
    nh = nsh // 2
    base = 6

    def body(x_ref, t_ref, wq_ref, wkv_ref, hkv_ref, h1_ref, q_ref, k_ref, v_ref):
        xv = x_ref[...]
        n = xv * _rs(xv)
        hkv = (n * t_ref[R_KV:R_KV + 1, :] + t_ref[R_KV + 1:R_KV + 2, :]).astype(BF16)
        h1 = (n * t_ref[base + R_W1:base + R_W1 + 1, :] + t_ref[base + R_SH1:base + R_SH1 + 1, :]).astype(BF16)
        hkv_ref[...] = hkv
        h1_ref[...] = h1
        q_ref[...] = (jnp.dot(h1, wq_ref[...], preferred_element_type=F32) * (HEAD_DIM ** -0.5)).astype(BF16)
        for j in range(nh):
            k_ref[:, j * kw:(j + 1) * kw] = jnp.dot(hkv, wkv_ref[j], preferred_element_type=F32).astype(BF16)
            v_ref[:, j * kw:(j + 1) * kw] = jnp.dot(hkv, wkv_ref[j + nh], preferred_element_type=F32).astype(BF16)

    act = jax.ShapeDtypeStruct((s, d), BF16)
    return pl.pallas_call(
        body, name="qkv_fwd", grid=(s // tm,),
        in_specs=[_rows(tm, d), _const(tab.shape), _const(wq.shape), _const(wkv.shape)],
        out_specs=tuple([_rows(tm, d)] * 5), out_shape=(act,) * 5,
        compiler_params=_params(VMEM_BIG),
    )(x, tab, wq, wkv)


def _window_specs():
    return [pl.BlockSpec((QB, LANES), (lambda p, b, w=w: (jnp.maximum(b - 2 + w, 0), p))) for w in range(3)]


def _key_valid(b):
    col = lax.broadcasted_iota(jnp.int32, (QB, KW), 1) // CHUNK
    return (b * (QB // CHUNK) - N_LEFT + col) >= 0


def _head_masks():
    lane = lax.broadcasted_iota(jnp.int32, (1, LANES), 1)
    return [(lane // HEAD_DIM == hh) for hh in range(LANES // HEAD_DIM)]


def _attn_fwd(q, k, v, bias):
    s, d = q.shape
    npair, nb = d // LANES, s // QB
    hpp = LANES // HEAD_DIM

    def body(q_ref, k0, k1, k2, v0, v1, v2, bias_ref, o_ref, lse_ref):
        b = pl.program_id(1)
        qv = q_ref[...]
        kwin = jnp.concatenate([k0[...], k1[...], k2[...]], axis=0)
        vwin = jnp.concatenate([v0[...], v1[...], v2[...]], axis=0)
        valid = _key_valid(b)
        masks = _head_masks()
        o = jnp.zeros((QB, LANES), F32)
        lse = jnp.zeros((QB, LANES), F32)
        for hh in range(hpp):
            qm = jnp.where(masks[hh], qv, jnp.zeros_like(qv))
            vm = jnp.where(masks[hh], vwin, jnp.zeros_like(vwin))
            sc = lax.dot_general(qm, kwin, NT, preferred_element_type=F32) + bias_ref[hh]
            sc = jnp.where(valid, sc, NEG)
            m = jnp.max(sc, axis=-1, keepdims=True)
            p = jnp.exp(sc - m)
            l = jnp.sum(p, axis=-1, keepdims=True)
            o = o + jnp.dot(p.astype(BF16), vm, preferred_element_type=F32) * (1.0 / l)
            lse = jnp.where(masks[hh], m + jnp.log(l), lse)
        o_ref[...] = o.astype(BF16)
        lse_ref[...] = lse

    blk = pl.BlockSpec((QB, LANES), lambda p, b: (b, p))
    return pl.pallas_call(
        body, name="attn_fwd", grid=(npair, nb),
        in_specs=[blk] + _window_specs() + _window_specs()
                 + [pl.BlockSpec((hpp, QB, KW), lambda p, b: (p, 0, 0))],
        out_specs=(blk, blk),
        out_shape=(jax.ShapeDtypeStruct((s, d), BF16), jax.ShapeDtypeStruct((s, d), F32)),
        compiler_params=_params(VMEM_BIG),
    )(q, k, k, k, v, v, v, bias)


def _attn_out_fwd(o, x, tab, wo, tm):
    s, d = x.shape
    base = 6

    def body(o_ref, x_ref, t_ref, wo_ref, x3_ref, y_ref):
        yv = jnp.dot(o_ref[...], wo_ref[...], preferred_element_type=F32)
        y_ref[...] = yv
        x3_ref[...] = x_ref[...] + (yv * _rs(yv)) * t_ref[base + R_P1:base + R_P1 + 1, :]

    return pl.pallas_call(
        body, name="attn_out_fwd", grid=(s // tm,),
        in_specs=[_rows(tm, d), _rows(tm, d), _const(tab.shape), _const(wo.shape)],
        out_specs=(_rows(tm, d), _rows(tm, d)),
        out_shape=(jax.ShapeDtypeStruct((s, d), F32), jax.ShapeDtypeStruct((s, d), F32)),
        compiler_params=_params(VMEM_BIG),
    )(o, x, tab, wo)


def _ffn_bwd(dxo, x, y, gu, tab, base, wfi, wfo, tm, name):
    s, d = x.shape
    nsh, _, fw = wfi.shape
    nh = nsh // 2

    def body(dxo_ref, x_ref, y_ref, gu_ref, t_ref, wfi_ref, wfo_ref, dx_ref, dyb_ref, dgu_ref, sums_ref):
        @pl.when(pl.program_id(0) == 0)
        def _():
            sums_ref[...] = jnp.zeros_like(sums_ref)

        dxo_v = dxo_ref[...]
        yv = y_ref[...]
        ry = _rs(yv)
        ny = yv * ry
        sums_ref[R_P2:R_P2 + 1, :] += _colsum(dxo_v * ny)
        dyb = _norm_bwd(dxo_v * t_ref[base + R_P2:base + R_P2 + 1, :], ny, ry).astype(BF16)
        dyb_ref[...] = dyb
        dh = jnp.zeros((tm, d), F32)
        for j in range(nh):
            da = lax.dot_general(dyb, wfo_ref[j], NT, preferred_element_type=F32)
            g, u = gu_ref[j], gu_ref[j + nh]
            sg = _sigmoid(g)
            dg = (da * u * sg * (1.0 + g * (1.0 - sg))).astype(BF16)
            du = (da * g * sg).astype(BF16)
            dgu_ref[j] = dg
            dgu_ref[j + nh] = du
            dh = dh + lax.dot_general(dg, wfi_ref[j], NT, preferred_element_type=F32)
            dh = dh + lax.dot_general(du, wfi_ref[j + nh], NT, preferred_element_type=F32)
        xv = x_ref[...]
        r = _rs(xv)
        n = xv * r
        sums_ref[R_SH2:R_SH2 + 1, :] += _colsum(dh)
        sums_ref[R_W2:R_W2 + 1, :] += _colsum(dh * n)
        dx_ref[...] = dxo_v + _norm_bwd(dh * t_ref[base + R_W2:base + R_W2 + 1, :], n, r)

    return pl.pallas_call(
        body, name=name, grid=(s // tm,),
        in_specs=[_rows(tm, d), _rows(tm, d), _rows(tm, d), pl.BlockSpec((nsh, tm, fw), lambda i: (0, i, 0)),
                  _const(tab.shape), _const(wfi.shape), _const(wfo.shape)],
        out_specs=(_rows(tm, d), _rows(tm, d), pl.BlockSpec((nsh, tm, fw), lambda i: (0, i, 0)),
                   pl.BlockSpec((8, d), lambda i: (0, 0))),
        out_shape=(jax.ShapeDtypeStruct((s, d), F32), jax.ShapeDtypeStruct((s, d), BF16),
                   jax.ShapeDtypeStruct((nsh, s, fw), BF16), jax.ShapeDtypeStruct((8, d), F32)),
        compiler_params=_params(VMEM_BIG),
    )(dxo, x, y, gu, tab, wfi, wfo)


def _attn_out_bwd(dx, y, tab, wo, tm):
    s, d = y.shape
    base = 6

    def body(dx_ref, y_ref, t_ref, wo_ref, dyb_ref, do_ref, sums_ref):
        @pl.when(pl.program_id(0) == 0)
        def _():
            sums_ref[...] = jnp.zeros_like(sums_ref)

        dxv = dx_ref[...]
        yv = y_ref[...]
        ry = _rs(yv)
        ny = yv * ry
        sums_ref[R_P1:R_P1 + 1, :] += _colsum(dxv * ny)
        dyb = _norm_bwd(dxv * t_ref[base + R_P1:base + R_P1 + 1, :], ny, ry).astype(BF16)
        dyb_ref[...] = dyb
        do_ref[...] = lax.dot_general(dyb, wo_ref[...], NT, preferred_element_type=F32).astype(BF16)

    return pl.pallas_call(
        body, name="attn_out_bwd", grid=(s // tm,),
        in_specs=[_rows(tm, d), _rows(tm, d), _const(tab.shape), _const(wo.shape)],
        out_specs=(_rows(tm, d), _rows(tm, d), pl.BlockSpec((8, d), lambda i: (0, 0))),
        out_shape=(jax.ShapeDtypeStruct((s, d), BF16), jax.ShapeDtypeStruct((s, d), BF16),
                   jax.ShapeDtypeStruct((8, d), F32)),
        compiler_params=_params(VMEM_BIG),
    )(dx, y, tab, wo)


def _attn_bwd(q, k, v, do, lse, bias):
    s, d = q.shape
    npair, nb = d // LANES, s // QB
    hpp = LANES // HEAD_DIM

    def body(q_ref, k0, k1, k2, v0, v1, v2, do_ref, lse_ref, bias_ref, dq_ref, dk_ref, dv_ref, db_ref):
        b = pl.program_id(1)

        @pl.when(b == 0)
        def _():
            dk_ref[...] = jnp.zeros_like(dk_ref)
            dv_ref[...] = jnp.zeros_like(dv_ref)
            db_ref[...] = jnp.zeros_like(db_ref)

        qv = q_ref[...]
        dov = do_ref[...]
        lsev = lse_ref[...]
        kwin = jnp.concatenate([k0[...], k1[...], k2[...]], axis=0)
        vwin = jnp.concatenate([v0[...], v1[...], v2[...]], axis=0)
        valid = _key_valid(b)
        masks = _head_masks()
        dq = jnp.zeros((QB, LANES), F32)
        dkw = jnp.zeros((KW, LANES), F32)
        dvw = jnp.zeros((KW, LANES), F32)
        for hh in range(hpp):
            qm = jnp.where(masks[hh], qv, jnp.zeros_like(qv))
            dom = jnp.where(masks[hh], dov, jnp.zeros_like(dov))
            km = jnp.where(masks[hh], kwin, jnp.zeros_like(kwin))
            lse_h = jnp.max(jnp.where(masks[hh], lsev, NEG), axis=-1, keepdims=True)
            sc = lax.dot_general(qm, kwin, NT, preferred_element_type=F32) + bias_ref[hh]
            sc = jnp.where(valid, sc, NEG)
            p = jnp.exp(sc - lse_h)
            dp = lax.dot_general(dom, vwin, NT, preferred_element_type=F32)
            ds = p * (dp - jnp.sum(dp * p, axis=-1, keepdims=True))
            db_ref[hh] += ds
            dsb = ds.astype(BF16)
            dq = dq + jnp.dot(dsb, km, preferred_element_type=F32)
            dkw = dkw + lax.dot_general(dsb, qm, TN, preferred_element_type=F32)
            dvw = dvw + lax.dot_general(p.astype(BF16), dom, TN, preferred_element_type=F32)
        dq_ref[...] = (dq * (HEAD_DIM ** -0.5)).astype(BF16)
        for w in range(3):
            start = pl.multiple_of(jnp.maximum(b - 2 + w, 0) * QB, QB)
            dk_ref[pl.ds(start, QB), :] += dkw[w * QB:(w + 1) * QB, :]
            dv_ref[pl.ds(start, QB), :] += dvw[w * QB:(w + 1) * QB, :]

    blk = pl.BlockSpec((QB, LANES), lambda p, b: (b, p))
    col = pl.BlockSpec((s, LANES), lambda p, b: (0, p))
    pair = pl.BlockSpec((hpp, QB, KW), lambda p, b: (p, 0, 0))
    return pl.pallas_call(
        body, name="attn_bwd", grid=(npair, nb),
        in_specs=[blk] + _window_specs() + _window_specs() + [blk, blk, pair],
        out_specs=(blk, col, col, pair),
        out_shape=(jax.ShapeDtypeStruct((s, d), BF16), jax.ShapeDtypeStruct((s, d), F32),
                   jax.ShapeDtypeStruct((s, d), F32), jax.ShapeDtypeStruct(bias.shape, F32)),
        compiler_params=_params(VMEM_BIG),
    )(q, k, k, k, v, v, v, do, lse, bias)


def _qkv_bwd(dres, dq, dk, dv, x, tab, wq, wkv, tm):
    s, d = x.shape
    nsh, _, kw = wkv.shape
    nh = nsh // 2
    base = 6

    def body(dres_ref, dq_ref, dk_ref, dv_ref, x_ref, t_ref, wq_ref, wkv_ref, dx_ref, dkv_ref, sums_ref):
        @pl.when(pl.program_id(0) == 0)
        def _():
            sums_ref[...] = jnp.zeros_like(sums_ref)

        dh1 = lax.dot_general(dq_ref[...], wq_ref[...], NT, preferred_element_type=F32)
        dkv_ref[:, 0:d] = dk_ref[...].astype(BF16)
        dkv_ref[:, d:2 * d] = dv_ref[...].astype(BF16)
        dhkv = jnp.zeros((tm, d), F32)
        for j in range(nsh):
            dhkv = dhkv + lax.dot_general(dkv_ref[:, j * kw:(j + 1) * kw], wkv_ref[j], NT,
                                          preferred_element_type=F32)
        xv = x_ref[...]
        r = _rs(xv)
        n = xv * r
        sums_ref[0:1, :] += _colsum(dh1 * n)
        sums_ref[1:2, :] += _colsum(dh1)
        sums_ref[2:3, :] += _colsum(dhkv * n)
        sums_ref[3:4, :] += _colsum(dhkv)
        dn = dh1 * t_ref[base + R_W1:base + R_W1 + 1, :] + dhkv * t_ref[R_KV:R_KV + 1, :]
        dx_ref[...] = dres_ref[...] + _norm_bwd(dn, n, r)

    return pl.pallas_call(
        body, name="qkv_bwd", grid=(s // tm,),
        in_specs=[_rows(tm, d)] * 5 + [_const(tab.shape), _const(wq.shape), _const(wkv.shape)],
        out_specs=(_rows(tm, d), _rows(tm, 2 * d), pl.BlockSpec((8, d), lambda i: (0, 0))),
        out_shape=(jax.ShapeDtypeStruct((s, d), F32), jax.ShapeDtypeStruct((s, 2 * d), BF16),
                   jax.ShapeDtypeStruct((8, d), F32)),
        compiler_params=_params(VMEM_BIG),
    )(dres, dq, dk, dv, x, tab, wq, wkv)


def _conv_bwd(dx1, x, y, bcx, tab, ck, wci, wco, tm):
    s, d = x.shape
    nsh, _, cw = wci.shape
    nt = s // tm

    def rev(i):
        return (nt - 1 - i, 0)

    def halo(i):
        return (jnp.maximum((nt - 1 - i) * (tm // 8) - 1, 0), 0)

    def body(dx_ref, x_ref, y_ref, bcx_ref, halo_ref, t_ref, ck_ref, wci_ref, wco_ref,
             dx0_ref, dyb_ref, dbcx_ref, sums_ref, dck_ref, carry):
        i = pl.program_id(0)

        @pl.when(i == 0)
        def _():
            sums_ref[...] = jnp.zeros_like(sums_ref)
            dck_ref[...] = jnp.zeros_like(dck_ref)
            carry[...] = jnp.zeros_like(carry)

        dxv = dx_ref[...]
        yv = y_ref[...]
        ry = _rs(yv)
        ny = yv * ry
        sums_ref[R_P1:R_P1 + 1, :] += _colsum(dxv * ny)
        dyb = _norm_bwd(dxv * t_ref[R_P1:R_P1 + 1, :], ny, ry).astype(BF16)
        dyb_ref[...] = dyb
        du = lax.dot_general(dyb, wco_ref[...], NT, preferred_element_type=F32)
        bg, cg, xi = bcx_ref[:, 0:d], bcx_ref[:, d:2 * d], bcx_ref[:, 2 * d:3 * d]
        z = cg * xi
        zp = halo_ref[:, d:2 * d] * halo_ref[:, 2 * d:3 * d]
        zp = jnp.where(i == nt - 1, jnp.zeros_like(zp), zp)
        row = lax.broadcasted_iota(jnp.int32, z.shape, 0)
        c1, c2 = zp[7:8, :], zp[6:7, :]
        z1 = jnp.where(row == 0, c1, pltpu.roll(z, 1, 0))
        z2 = jnp.where(row == 0, c2, jnp.where(row == 1, c1, pltpu.roll(z, 2, 0)))
        k0, k1, k2 = ck_ref[0:1, :], ck_ref[1:2, :], ck_ref[2:3, :]
        conv = k0 * z2 + k1 * z1 + k2 * z
        dconv = du * bg
        dck_ref[0:1, :] += _colsum(dconv * z2)
        dck_ref[1:2, :] += _colsum(dconv * z1)
        dck_ref[2:3, :] += _colsum(dconv * z)
        n1, n2 = carry[0:1, :], carry[1:2, :]
        d1 = jnp.where(row == tm - 1, n1, pltpu.roll(dconv, tm - 1, 0))
        d2 = jnp.where(row == tm - 1, n2, jnp.where(row == tm - 2, n1, pltpu.roll(dconv, tm - 2, 0)))
        carry[...] = dconv[0:8, :]
        dz = k2 * dconv + k1 * d1 + k0 * d2
        dbcx_ref[:, 0:d] = (du * conv).astype(BF16)
        dbcx_ref[:, d:2 * d] = (dz * xi).astype(BF16)
        dbcx_ref[:, 2 * d:3 * d] = (dz * cg).astype(BF16)
        dh = jnp.zeros((tm, d), F32)
        for j in range(nsh):
            dh = dh + lax.dot_general(dbcx_ref[:, j * cw:(j + 1) * cw], wci_ref[j], NT,
                                      preferred_element_type=F32)
        xv = x_ref[...]
        r = _rs(xv)
        n = xv * r
        sums_ref[R_W1:R_W1 + 1, :] += _colsum(dh * n)
        sums_ref[R_SH1:R_SH1 + 1, :] += _colsum(dh)
        dx0_ref[...] = dxv + _norm_bwd(dh * t_ref[R_W1:R_W1 + 1, :], n, r)

    rrow = lambda cols: pl.BlockSpec((tm, cols), rev)
    acc = pl.BlockSpec((8, d), lambda i: (0, 0))
    return pl.pallas_call(
        body, name="conv_bwd", grid=(nt,),
        in_specs=[rrow(d), rrow(d), rrow(d), rrow(3 * d), pl.BlockSpec((8, 3 * d), halo),
                  _const(tab.shape), _const(ck.shape), _const(wci.shape), _const(wco.shape)],
        out_specs=(rrow(d), rrow(d), rrow(3 * d), acc, acc),
        out_shape=(jax.ShapeDtypeStruct((s, d), F32), jax.ShapeDtypeStruct((s, d), BF16),
                   jax.ShapeDtypeStruct((s, 3 * d), BF16), jax.ShapeDtypeStruct((8, d), F32),
                   jax.ShapeDtypeStruct((8, d), F32)),
        scratch_shapes=[pltpu.VMEM((8, d), F32)],
        compiler_params=_params(VMEM_BIG),
    )(dx1, x, y, bcx, bcx, tab, ck, wci, wco)


def _wgrad(a, b, nblk, a_spec, b_spec, m, n, tk, name):
    s = a.shape[-2]
    nk = s // tk

    def body(a_ref, b_ref, o_ref, acc):
        kk = pl.program_id(1)

        @pl.when(kk == 0)
        def _():
            acc[...] = jnp.zeros_like(acc)

        acc[...] += lax.dot_general(a_ref[...], b_ref[...], TN, preferred_element_type=F32)

        @pl.when(kk == nk - 1)
        def _():
            o_ref[...] = acc[...].astype(BF16)

    return pl.pallas_call(
        body, name=name, grid=(nblk, nk),
        in_specs=[a_spec, b_spec],
        out_specs=pl.BlockSpec((None, m, n), lambda j, kk: (j, 0, 0)),
        out_shape=jax.ShapeDtypeStruct((nblk, m, n), BF16),
        scratch_shapes=[pltpu.VMEM((m, n), F32)],
        compiler_params=_params(VMEM_BIG),
    )(a, b)


def _wgrad_cols(a, b, nblk, tk, name):
    m, n = a.shape[1], b.shape[1] // nblk
    return _wgrad(a, b, nblk, pl.BlockSpec((tk, m), lambda j, kk: (kk, 0)),
                  pl.BlockSpec((tk, n), lambda j, kk: (kk, j)), m, n, tk, name)


def _wgrad_bstack(a, b, tk, name):
    nblk, _, n = b.shape
    m = a.shape[1]
    return _wgrad(a, b, nblk, pl.BlockSpec((tk, m), lambda j, kk: (kk, 0)),
                  pl.BlockSpec((None, tk, n), lambda j, kk: (j, kk, 0)), m, n, tk, name)


def _wgrad_astack(a, b, tk, name):
    nblk, _, m = a.shape
    n = b.shape[1]
    return _wgrad(a, b, nblk, pl.BlockSpec((None, tk, m), lambda j, kk: (j, kk, 0)),
                  pl.BlockSpec((tk, n), lambda j, kk: (kk, 0)), m, n, tk, name)


def _adamw_math(w, g, m, v):
    m = ADAM_B1 * m + (1.0 - ADAM_B1) * g
    v = ADAM_B2 * v + (1.0 - ADAM_B2) * (g * g)
    m_hat = m / (1.0 - ADAM_B1 ** ADAM_STEP)
    v_hat = v / (1.0 - ADAM_B2 ** ADAM_STEP)
    delta = -ADAM_LR * (m_hat / (jnp.sqrt(v_hat) + ADAM_EPS) + ADAM_WD * w)
    return delta, m, v


def _adamw_reduce(parts, w, m, v, tr, name):
    r, c = w.shape
    tr = _row_tile(r, tr)

    def body(p_ref, w_ref, m_ref, v_ref, g_ref, d_ref, mo_ref, vo_ref):
        g = p_ref[0].astype(F32)
        for i in range(1, N_DEV):
            g = g + p_ref[i].astype(F32)
        g_ref[...] = g
        d_ref[...], mo_ref[...], vo_ref[...] = _adamw_math(w_ref[...], g, m_ref[...], v_ref[...])

    blk = pl.BlockSpec((tr, c), lambda i: (i, 0))
    out = jax.ShapeDtypeStruct((r, c), F32)
    return pl.pallas_call(
        body, name=name, grid=(r // tr,),
        in_specs=[pl.BlockSpec((N_DEV, tr, c), lambda i: (0, i, 0)), blk, blk, blk],
        out_specs=(blk,) * 4, out_shape=(out,) * 4,
        compiler_params=_params(VMEM_BIG),
    )(parts, w, m, v)


def _adamw_outer(sct, dm, w, m, v, tr, name):
    nl, d, c = w.shape

    def body(s_ref, dm_ref, w_ref, m_ref, v_ref, g_ref, d_ref, mo_ref, vo_ref):
        g = jnp.dot(s_ref[...], dm_ref[...], preferred_element_type=F32)
        g_ref[...] = g
        d_ref[...], mo_ref[...], vo_ref[...] = _adamw_math(w_ref[...], g, m_ref[...], v_ref[...])

    blk = pl.BlockSpec((None, tr, c), lambda l, i: (l, i, 0))
    out = jax.ShapeDtypeStruct((nl, d, c), F32)
    return pl.pallas_call(
        body, name=name, grid=(nl, d // tr),
        in_specs=[pl.BlockSpec((tr, N_DEV), lambda l, i: (i, 0)),
                  pl.BlockSpec((None, N_DEV, c), lambda l, i: (l, 0, 0)), blk, blk, blk],
        out_specs=(blk,) * 4, out_shape=(out,) * 4,
        compiler_params=_params(VMEM_BIG),
    )(sct, dm, w, m, v)


def _pad_rows(a, rows):
    return jnp.concatenate([a, jnp.zeros((rows - a.shape[0],) + a.shape[1:], a.dtype)], axis=0)


def kernel(x, c, mod_w, mod_b, norm_g, ffn_w_in, ffn_w_out, conv_w_in, conv_k, conv_w_out, kv_mod_w, kv_mod_b, kv_norm_g, w_kv, attn_w_q, attn_w_o, rel_bias, loss_target, m_mod_w, m_mod_b, m_norm_g, m_ffn_w_in, m_ffn_w_out, m_conv_w_in, m_conv_k, m_conv_w_out, m_kv_mod_w, m_kv_mod_b, m_kv_norm_g, m_w_kv, m_attn_w_q, m_attn_w_o, m_rel_bias, v_mod_w, v_mod_b, v_norm_g, v_ffn_w_in, v_ffn_w_out, v_conv_w_in, v_conv_k, v_conv_w_out, v_kv_mod_w, v_kv_mod_b, v_kv_norm_g, v_w_kv, v_attn_w_q, v_attn_w_o, v_rel_bias):
    s, d = x.shape[1], x.shape[2]
    dq = d // LANES
    dsh = d // N_DEV
    nl = mod_w.shape[0]
    mw = mod_w.shape[2]
    kmw = kv_mod_w.shape[1]
    fw = ffn_w_in.shape[2]
    nh, nrel = rel_bias.shape[1], rel_bias.shape[2]
    tm = min(256, s)
    tk = min(512, s)
    me = 4 * lax.axis_index("x") + 2 * lax.axis_index("y") + lax.axis_index("c")

    x0 = x[0]
    tgt = loss_target[0]

    small1 = jnp.concatenate([c.reshape(dq, LANES), norm_g.reshape(dq, LANES),
                              _pad_rows(conv_k[0], 8).reshape(dq, LANES)], axis=0)
    big = [ffn_w_in[0], ffn_w_in[1], ffn_w_out[0], ffn_w_out[1], conv_w_in[0], conv_w_out[0], w_kv,
           attn_w_q[0], attn_w_o[0]]
    big = [a.astype(BF16) for a in big]
    gathered = _exchange(big + [small1], ["gather"] * 10, "gather_weights")
    sm = gathered[9]
    c_all = sm[:, 0:dq].reshape(N_DEV, d)
    ng_full = jnp.transpose(sm[:, dq:2 * dq].reshape(N_DEV, 8, dsh), (1, 0, 2)).reshape(8, d)
    ck_full = jnp.transpose(sm[:, 2 * dq:3 * dq].reshape(N_DEV, 8, dsh), (1, 0, 2)).reshape(8, d)
    wfi = [gathered[0], gathered[1]]
    wfo = [gathered[2].reshape(N_DEV // 2, -1, d), gathered[3].reshape(N_DEV // 2, -1, d)]
    wci = gathered[4]
    wco = gathered[5].reshape(d, d)
    wkv = gathered[6]
    wq = gathered[7].reshape(d, d)
    wo = gathered[8].reshape(d, d)

    modcols, silu_c = _mod_fwd(c_all, mod_w, kv_mod_w)
    (modall,) = _exchange([modcols], ["gather"], "gather_mod")
    mine = lax.dynamic_index_in_dim(modall, me, axis=1, keepdims=False)
    modrow = jnp.stack([mine[:, l * mw:(l + 1) * mw].reshape(6, d) for l in range(nl)])
    kvrow = mine[:, nl * mw:nl * mw + kmw].reshape(2, d)
    tab, modval = _vec_prep(modrow, mod_b.reshape(nl, 6, d), kvrow, kv_mod_b.reshape(2, d), ng_full,
                            kv_norm_g.reshape(1, d))
    bias = _bias_fwd(rel_bias[0])

    x1, h1a, bcx, ua, ya = _conv_fwd(x0, tab, ck_full, wci, wco, tm)
    x2, h2a, gua, aa, y2a = _ffn_fwd(x1, tab, 0, wfi[0], wfo[0], None, tm, "ffn_fwd0")
    hkv, h1b, q, k, v = _qkv_fwd(x2, tab, wq, wkv, tm)
    o, lse = _attn_fwd(q, k, v, bias)
    x3, yb = _attn_out_fwd(o, x2, tab, wo, tm)
    dx4, h2b, gub, ab, y2b, loss_acc = _ffn_fwd(x3, tab, 6, wfi[1], wfo[1], tgt, tm, "ffn_fwd1")
    loss = lax.psum(loss_acc[0, 0] * (0.5 / d), ("x", "y", "c"))

    dx3, dy2b, dgub, sums_f1 = _ffn_bwd(dx4, x3, y2b, gub, tab, 6, wfi[1], wfo[1], tm, "ffn_bwd1")
    dyb, do, sums_o = _attn_out_bwd(dx3, yb, tab, wo, tm)
    dqb, dk, dv, dbias = _attn_bwd(q, k, v, do, lse, bias)
    dx2, dkvb, sums_q = _qkv_bwd(dx3, dqb, dk, dv, x2, tab, wq, wkv, tm)
    dx1, dy2a, dgua, sums_f0 = _ffn_bwd(dx2, x1, y2a, gua, tab, 0, wfi[0], wfo[0], tm, "ffn_bwd0")
    dx0, dya, dbcx, sums_c, dck = _conv_bwd(dx1, x0, ya, bcx, tab, ck_full, wci, wco, tm)

    g_wfi0 = _wgrad_bstack(h2a, dgua, tk, "wgrad_ffn_in0")
    g_wfi1 = _wgrad_bstack(h2b, dgub, tk, "wgrad_ffn_in1")
    g_wfo0 = _wgrad_astack(aa, dy2a, tk, "wgrad_ffn_out0").reshape(N_DEV, -1, d)
    g_wfo1 = _wgrad_astack(ab, dy2b, tk, "wgrad_ffn_out1").reshape(N_DEV, -1, d)
    g_wci = _wgrad_cols(h1a, dbcx, N_DEV, tk, "wgrad_conv_in")
    g_wco = _wgrad_cols(ua, dya, 1, tk, "wgrad_conv_out").reshape(N_DEV, dsh, d)
    g_wkv = _wgrad_cols(hkv, dkvb, N_DEV, tk, "wgrad_kv")
    g_wq = _wgrad_cols(h1b, dqb, 1, tk, "wgrad_q").reshape(N_DEV, dsh, d)
    g_wo = _wgrad_cols(o, dyb, 1, tk, "wgrad_o").reshape(N_DEV, dsh, d)
    drel = _bias_bwd(dbias, nrel)

    dmod, dng, dkvg = _vec_bwd(sums_c, sums_f0, sums_q, sums_o, sums_f1, modval, ng_full, kv_norm_g.reshape(1, d))

    relw = -(-nrel // LANES) * LANES
    drel_p = jnp.concatenate([drel, jnp.zeros((nh, relw - nrel), F32)], axis=1)
    small3 = jnp.concatenate([dmod.reshape(16 * dq, LANES), dng.reshape(8 * dq, LANES), dkvg.reshape(8 * dq, LANES),
                              dck.reshape(8 * dq, LANES), drel_p.reshape(nh * relw // LANES, LANES)], axis=0)
    parts = _exchange([g_wfi0, g_wfi1, g_wfo0, g_wfo1, g_wci, g_wco, g_wkv, g_wq, g_wo, small3],
                      ["scatter"] * 9 + ["gather"], "exchange_grads")
    sm = parts[9]
    o1, o2, o3, o4 = 16 * dq, 24 * dq, 32 * dq, 40 * dq
    dmod_all = sm[:, 0:o1].reshape(N_DEV, 16, d)
    mine_cols = lambda a: lax.dynamic_slice_in_dim(a, me * dsh, dsh, axis=2)
    dng_parts = mine_cols(sm[:, o1:o2].reshape(N_DEV, 8, d))
    dkvg_parts = sm[:, o2:o3].reshape(N_DEV, 8, d)[:, 0:1]
    dck_parts = mine_cols(sm[:, o3:o4].reshape(N_DEV, 8, d))[:, 0:3]
    drel_parts = sm[:, o4:].reshape(N_DEV, nh, relw)[:, :, 0:nrel]

    def update(parts8, w, m, v, name, tr=256):
        shp = w.shape
        w2, m2, v2 = (a.reshape(-1, shp[-1]) for a in (w, m, v))
        outs = _adamw_reduce(parts8.reshape(N_DEV, -1, shp[-1]), w2, m2, v2, tr, name)
        return [a.reshape(shp) for a in outs]

    u_ffn_in = update(jnp.stack([parts[0], parts[1]], axis=1), ffn_w_in, m_ffn_w_in, v_ffn_w_in, "adamw_ffn_in")
    u_ffn_out = update(jnp.stack([parts[2], parts[3]], axis=1), ffn_w_out, m_ffn_w_out, v_ffn_w_out, "adamw_ffn_out")
    u_conv_in = update(parts[4], conv_w_in, m_conv_w_in, v_conv_w_in, "adamw_conv_in")
    u_conv_out = update(parts[5], conv_w_out, m_conv_w_out, v_conv_w_out, "adamw_conv_out")
    u_w_kv = update(parts[6], w_kv, m_w_kv, v_w_kv, "adamw_w_kv")
    u_w_q = update(parts[7], attn_w_q, m_attn_w_q, v_attn_w_q, "adamw_w_q")
    u_w_o = update(parts[8], attn_w_o, m_attn_w_o, v_attn_w_o, "adamw_w_o")

    sct = jnp.transpose(silu_c)
    dm_mod = jnp.stack([lax.dynamic_slice_in_dim(dmod_all[:, 6 * l:6 * l + 6].reshape(N_DEV, 6 * d), me * mw, mw, axis=1)
                        for l in range(nl)]).astype(BF16)
    dm_kv = lax.dynamic_slice_in_dim(dmod_all[:, R_KV:R_KV + 2].reshape(N_DEV, 2 * d), me * kmw, kmw, axis=1)
    u_mod_w = _adamw_outer(sct, dm_mod, mod_w, m_mod_w, v_mod_w, min(256, d), "adamw_mod_w")
    u_kv_mod_w = [a[0] for a in _adamw_outer(sct, dm_kv.astype(BF16)[None], kv_mod_w[None], m_kv_mod_w[None],
                                             v_kv_mod_w[None], min(256, d), "adamw_kv_mod_w")]

    modb_parts = jnp.stack([dmod_all[:, 6 * l:6 * l + 6].reshape(N_DEV, 6 * d) for l in range(nl)], axis=1)
    u_mod_b = update(modb_parts, mod_b, m_mod_b, v_mod_b, "adamw_mod_b")
    u_norm_g = update(dng_parts, norm_g.reshape(8, dsh), m_norm_g.reshape(8, dsh), v_norm_g.reshape(8, dsh), "adamw_norm_g")
    u_norm_g = [a.reshape(norm_g.shape) for a in u_norm_g]
    u_conv_k = update(dck_parts, conv_k, m_conv_k, v_conv_k, "adamw_conv_k")
    kvb_parts = dmod_all[:, R_KV:R_KV + 2].reshape(N_DEV, 1, 2 * d)
    u_kv_mod_b = [a.reshape(kv_mod_b.shape) for a in update(kvb_parts, kv_mod_b.reshape(1, -1), m_kv_mod_b.reshape(1, -1),
                                                            v_kv_mod_b.reshape(1, -1), "adamw_kv_mod_b")]
    u_kv_norm_g = [a.reshape(kv_norm_g.shape) for a in update(dkvg_parts, kv_norm_g.reshape(1, -1), m_kv_norm_g.reshape(1, -1),
                                                              v_kv_norm_g.reshape(1, -1), "adamw_kv_norm_g")]
    u_rel = update(drel_parts, rel_bias, m_rel_bias, v_rel_bias, "adamw_rel_bias")

    ups = [u_mod_w, u_mod_b, u_norm_g, u_ffn_in, u_ffn_out, u_conv_in, u_conv_k, u_conv_out, u_kv_mod_w, u_kv_mod_b,
           u_kv_norm_g, u_w_kv, u_w_q, u_w_o, u_rel]
    return (loss, dx0[None], *[u[0] for u in ups], *[u[1] for u in ups], *[u[2] for u in ups], *[u[3] for u in ups])
```

```python
import functools

import jax
import jax.numpy as jnp
from jax import lax
from jax.experimental import pallas as pl
from jax.experimental.pallas import tpu as pltpu

F32 = jnp.float32
BF16 = jnp.bfloat16

EPS = 1e-6
CHUNK = 64
HEAD_DIM = 64
N_LEFT = 8
LANES = 128
QB = 4 * CHUNK
KW = QB + N_LEFT * CHUNK
NEG = -1e30
N_DEV = 8

ADAM_LR = 0.001
ADAM_B1 = 0.9
ADAM_B2 = 0.999
ADAM_EPS = 1e-08
ADAM_WD = 0.01
ADAM_STEP = 10

VMEM_BIG = 56 * 1024 * 1024

NT = (((1,), (1,)), ((), ()))
TN = (((0,), (0,)), ((), ()))

R_W1, R_SH1, R_P1, R_W2, R_SH2, R_P2 = range(6)
R_KV = 12


def _params(vmem):
    return pltpu.CompilerParams(vmem_limit_bytes=vmem)


def _row_tile(rows, cap):
    for t in range(min(cap, rows) // 16 * 16, 0, -16):
        if rows % t == 0:
            return t
    return rows


def _rows(tm, cols):
    return pl.BlockSpec((tm, cols), lambda i: (i, 0))


def _const(shape):
    nd = len(shape)
    return pl.BlockSpec(shape, lambda *_: (0,) * nd, pipeline_mode=pl.Buffered(1))


def _rs(x):
    return lax.rsqrt(jnp.mean(x * x, axis=-1, keepdims=True) + EPS)


def _norm_bwd(d, n, r):
    return r * (d - n * jnp.mean(d * n, axis=-1, keepdims=True))


def _colsum(a):
    return jnp.sum(a, axis=0, keepdims=True)


def _sigmoid(g):
    return 1.0 / (1.0 + jnp.exp(-g))


def _exchange(arrays, modes, name):
    n = len(arrays)
    out_shape = []
    for a, mode in zip(arrays, modes):
        shp = (N_DEV,) + a.shape if mode == "gather" else a.shape
        out_shape.append(jax.ShapeDtypeStruct(shp, a.dtype))

    def body(*refs):
        ins, outs = refs[:n], refs[n:2 * n]
        send_sems, recv_sems, local_sems = refs[2 * n:]
        x, y, c = lax.axis_index("x"), lax.axis_index("y"), lax.axis_index("c")
        me = 4 * x + 2 * y + c
        local, sends, recvs = [], [], []
        for a in range(n):
            own = ins[a] if modes[a] == "gather" else ins[a].at[me]
            cp = pltpu.make_async_copy(own, outs[a].at[me], local_sems.at[a])
            cp.start()
            local.append(cp)
        for k in range(1, N_DEV):
            px = 1 - x if k & 4 else x
            py = 1 - y if k & 2 else y
            pc = 1 - c if k & 1 else c
            peer = 4 * px + 2 * py + pc
            for a in range(n):
                src = ins[a] if modes[a] == "gather" else ins[a].at[peer]
                sem = a * (N_DEV - 1) + k - 1
                cp = pltpu.make_async_remote_copy(
                    src_ref=src, dst_ref=outs[a].at[me],
                    send_sem=send_sems.at[sem], recv_sem=recv_sems.at[sem],
                    device_id=(px, py, pc), device_id_type=pl.DeviceIdType.MESH)
                cp.start()
                sends.append(cp)
                recvs.append(pltpu.make_async_remote_copy(
                    src_ref=src, dst_ref=outs[a].at[peer],
                    send_sem=send_sems.at[sem], recv_sem=recv_sems.at[sem],
                    device_id=(px, py, pc), device_id_type=pl.DeviceIdType.MESH))
        for cp in recvs:
            cp.wait_recv()
        for cp in sends:
            cp.wait_send()
        for cp in local:
            cp.wait()

    any_spec = pl.BlockSpec(memory_space=pl.ANY)
    return pl.pallas_call(
        body, name=name,
        out_shape=tuple(out_shape),
        in_specs=[any_spec] * n,
        out_specs=tuple([any_spec] * n),
        scratch_shapes=[
            pltpu.SemaphoreType.DMA((n * (N_DEV - 1),)),
            pltpu.SemaphoreType.DMA((n * (N_DEV - 1),)),
            pltpu.SemaphoreType.DMA((n,)),
        ],
    )(*arrays)


def _peers(x, y, c):
    out = []
    for k in range(1, N_DEV):
        px = 1 - x if k & 4 else x
        py = 1 - y if k & 2 else y
        pc = 1 - c if k & 1 else c
        out.append((k - 1, (px, py, pc), 4 * px + 2 * py + pc))
    return out


def _land_shape(a, mode):
    return (N_DEV,) + a.shape if mode == "gather" else a.shape


_HBM = pl.BlockSpec(memory_space=pltpu.HBM)
_SEM = pl.BlockSpec(memory_space=pltpu.SEMAPHORE)
_EFFECT = pltpu.SideEffectType.DATAFLOW_SIDE_EFFECTING


def _xstart(groups, name):
    flat = [(a, m) for arrays, modes in groups for a, m in zip(arrays, modes)]
    n, ngr = len(flat), len(groups)
    sizes = [len(arrays) for arrays, _ in groups]
    npeer = N_DEV - 1

    def body(*refs):
        ins, lands = refs[:n], refs[n:2 * n]
        sems = refs[2 * n:2 * n + 2 * ngr]
        token = refs[2 * n + 2 * ngr + 2 * n]
        local_sems = refs[2 * n + 2 * ngr + 2 * n + 1]
        x, y, c = lax.axis_index("x"), lax.axis_index("y"), lax.axis_index("c")
        me = 4 * x + 2 * y + c
        local = []
        a = 0
        for g in range(ngr):
            for j in range(sizes[g]):
                mode = flat[a][1]
                own = ins[a] if mode == "gather" else ins[a].at[me]
                cp = pltpu.make_async_copy(own, lands[a].at[me], local_sems.at[a])
                cp.start()
                local.append(cp)
                for slot, peer, pidx in _peers(x, y, c):
                    pltpu.make_async_remote_copy(
                        src_ref=ins[a] if mode == "gather" else ins[a].at[pidx], dst_ref=lands[a].at[me],
                        send_sem=sems[2 * g].at[j * npeer + slot], recv_sem=sems[2 * g + 1].at[j * npeer + slot],
                        device_id=peer, device_id_type=pl.DeviceIdType.MESH).start()
                a += 1
        for cp in local:
            cp.wait()
        token[...] = jnp.zeros_like(token)

    out_shape, out_specs = [], []
    for sz in sizes:
        out_shape += [pltpu.SemaphoreType.DMA((sz * npeer,)), pltpu.SemaphoreType.DMA((sz * npeer,))]
        out_specs += [_SEM, _SEM]
    out_shape += [pltpu.HBM(a.shape, a.dtype) for a, _ in flat]
    out_shape += [pltpu.HBM(_land_shape(a, m), a.dtype) for a, m in flat]
    out_specs += [_HBM] * (2 * n)
    out_shape.append(jax.ShapeDtypeStruct((8, LANES), F32))
    out_specs.append(pl.BlockSpec(memory_space=pltpu.VMEM))
    args = [pltpu.with_memory_space_constraint(a, pltpu.HBM) for a, _ in flat]
    args += [pltpu.with_memory_space_constraint(lax.empty(_land_shape(a, m), a.dtype), pltpu.HBM) for a, m in flat]
    res = pl.pallas_call(
        body, name=name, out_shape=tuple(out_shape), in_specs=[_HBM] * (2 * n), out_specs=tuple(out_specs),
        input_output_aliases={i: 2 * ngr + i for i in range(2 * n)},
        scratch_shapes=[pltpu.SemaphoreType.DMA((n,))],
        compiler_params=pltpu.CompilerParams(has_side_effects=_EFFECT),
    )(*args)
    handles, a = [], 0
    for g, sz in enumerate(sizes):
        handles.append((res[2 * g], res[2 * g + 1], list(res[2 * ngr + a:2 * ngr + a + sz]),
                        list(res[2 * ngr + n + a:2 * ngr + n + a + sz]), list(groups[g][1])))
        a += sz
    return handles, res[-1]


def _xwait(handle, after, name):
    send_sems, recv_sems, srcs, lands, modes = handle
    m = len(srcs)
    npeer = N_DEV - 1

    def body(*refs):
        ins, lnd = refs[:m], refs[m:2 * m]
        ssem, rsem = refs[2 * m], refs[2 * m + 1]
        x, y, c = lax.axis_index("x"), lax.axis_index("y"), lax.axis_index("c")
        for j in range(m):
            for slot, peer, pidx in _peers(x, y, c):
                cp = pltpu.make_async_remote_copy(
                    src_ref=ins[j] if modes[j] == "gather" else ins[j].at[pidx], dst_ref=lnd[j].at[pidx],
                    send_sem=ssem.at[j * npeer + slot], recv_sem=rsem.at[j * npeer + slot],
                    device_id=peer, device_id_type=pl.DeviceIdType.MESH)
                cp.wait_send()
                cp.wait_recv()

    res = pl.pallas_call(
        body, name=name,
        out_shape=tuple([pltpu.HBM(a.shape, a.dtype) for a in srcs] + [pltpu.HBM(a.shape, a.dtype) for a in lands]),
        in_specs=[_HBM] * (2 * m) + [_SEM, _SEM, pl.BlockSpec(memory_space=pl.ANY)],
        out_specs=tuple([_HBM] * (2 * m)),
        input_output_aliases={i: i for i in range(2 * m)},
        compiler_params=pltpu.CompilerParams(has_side_effects=_EFFECT),
    )(*srcs, *lands, send_sems, recv_sems, after)
    return list(res[m:])


def _mod_fwd(c_all, mod_w, kv_mod_w):
    nl, d, mw = mod_w.shape
    kw = kv_mod_w.shape[1]

    def body(c_ref, mw_ref, kw_ref, o_ref, sc_ref):
        cc = c_ref[...]
        sc = (cc * _sigmoid(cc)).astype(BF16)
        sc_ref[...] = sc
        for l in range(nl):
            o_ref[:, l * mw:(l + 1) * mw] = jnp.dot(sc, mw_ref[l].astype(BF16), preferred_element_type=F32)
        o_ref[:, nl * mw:nl * mw + kw] = jnp.dot(sc, kw_ref[...].astype(BF16), preferred_element_type=F32)

    return pl.pallas_call(
        body, name="mod_fwd",
        out_shape=(jax.ShapeDtypeStruct((c_all.shape[0], nl * mw + kw), F32),
                   jax.ShapeDtypeStruct(c_all.shape, BF16)),
        compiler_params=_params(VMEM_BIG),
    )(c_all, mod_w, kv_mod_w)


def _vec_prep(modrow, modb, kvrow, kvb, ng, kvg):
    d = ng.shape[1]

    def body(mr_ref, mb_ref, kr_ref, kb_ref, ng_ref, kvg_ref, t_ref, m_ref):
        t_ref[...] = jnp.zeros_like(t_ref)
        m_ref[...] = jnp.zeros_like(m_ref)
        for l in range(2):
            mod = mr_ref[l] + mb_ref[l]
            m_ref[6 * l:6 * l + 6, :] = mod
            g = ng_ref[4 * l:4 * l + 4, :]
            t_ref[6 * l + R_W1:6 * l + R_W1 + 1, :] = g[0:1] * (1.0 + mod[1:2])
            t_ref[6 * l + R_SH1:6 * l + R_SH1 + 1, :] = mod[0:1]
            t_ref[6 * l + R_P1:6 * l + R_P1 + 1, :] = mod[2:3] * g[1:2]
            t_ref[6 * l + R_W2:6 * l + R_W2 + 1, :] = g[2:3] * (1.0 + mod[4:5])
            t_ref[6 * l + R_SH2:6 * l + R_SH2 + 1, :] = mod[3:4]
            t_ref[6 * l + R_P2:6 * l + R_P2 + 1, :] = mod[5:6] * g[3:4]
        kv = kr_ref[...] + kb_ref[...]
        m_ref[R_KV:R_KV + 2, :] = kv
        t_ref[R_KV:R_KV + 1, :] = kvg_ref[...] * (1.0 + kv[1:2])
        t_ref[R_KV + 1:R_KV + 2, :] = kv[0:1]

    return pl.pallas_call(
        body, name="vec_prep",
        out_shape=(jax.ShapeDtypeStruct((16, d), F32), jax.ShapeDtypeStruct((16, d), F32)),
    )(modrow, modb, kvrow, kvb, ng, kvg)


def _vec_bwd(sums_c, sums_f0, sums_q, sums_o, sums_f1, mt, ng, kvg):
    d = ng.shape[1]

    def body(sc_ref, sf0_ref, sq_ref, so_ref, sf1_ref, m_ref, ng_ref, kvg_ref, dm_ref, dng_ref, dkvg_ref, g_ref):
        g_ref[...] = jnp.zeros_like(g_ref)
        g_ref[0:3, :] = sc_ref[0:3, :]
        g_ref[3:6, :] = sf0_ref[3:6, :]
        g_ref[6:8, :] = sq_ref[0:2, :]
        g_ref[8:9, :] = so_ref[2:3, :]
        g_ref[9:12, :] = sf1_ref[3:6, :]
        g_ref[R_KV:R_KV + 2, :] = sq_ref[2:4, :]
        dm_ref[...] = jnp.zeros_like(dm_ref)
        dkvg_ref[...] = jnp.zeros_like(dkvg_ref)
        for l in range(2):
            g = ng_ref[4 * l:4 * l + 4, :]
            mod = m_ref[6 * l:6 * l + 6, :]
            s = g_ref[6 * l:6 * l + 6, :]
            dm_ref[6 * l + 0:6 * l + 1, :] = s[1:2]
            dm_ref[6 * l + 1:6 * l + 2, :] = s[0:1] * g[0:1]
            dm_ref[6 * l + 2:6 * l + 3, :] = s[2:3] * g[1:2]
            dm_ref[6 * l + 3:6 * l + 4, :] = s[4:5]
            dm_ref[6 * l + 4:6 * l + 5, :] = s[3:4] * g[2:3]
            dm_ref[6 * l + 5:6 * l + 6, :] = s[5:6] * g[3:4]
            dng_ref[4 * l + 0:4 * l + 1, :] = s[0:1] * (1.0 + mod[1:2])
            dng_ref[4 * l + 1:4 * l + 2, :] = s[2:3] * mod[2:3]
            dng_ref[4 * l + 2:4 * l + 3, :] = s[3:4] * (1.0 + mod[4:5])
            dng_ref[4 * l + 3:4 * l + 4, :] = s[5:6] * mod[5:6]
        dm_ref[R_KV:R_KV + 1, :] = g_ref[R_KV + 1:R_KV + 2, :]
        dm_ref[R_KV + 1:R_KV + 2, :] = g_ref[R_KV:R_KV + 1, :] * kvg_ref[...]
        dkvg_ref[0:1, :] = g_ref[R_KV:R_KV + 1, :] * (1.0 + m_ref[R_KV + 1:R_KV + 2, :])

    return pl.pallas_call(
        body, name="vec_bwd",
        out_shape=(jax.ShapeDtypeStruct((16, d), F32), jax.ShapeDtypeStruct((8, d), F32),
                   jax.ShapeDtypeStruct((8, d), F32)),
        scratch_shapes=[pltpu.VMEM((16, d), F32)],
    )(sums_c, sums_f0, sums_q, sums_o, sums_f1, mt, ng, kvg)


def _rel_index(nrel):
    width = KW + QB
    e = lax.broadcasted_iota(jnp.int32, (nrel, width), 1)
    r = lax.broadcasted_iota(jnp.int32, (nrel, width), 0)
    max_rel = (nrel - 1) // 2
    idx = jnp.clip(KW - e, -max_rel, max_rel) + max_rel
    return (idx == r).astype(F32)


def _band_valid():
    row = lax.broadcasted_iota(jnp.int32, (QB, KW), 0) // CHUNK
    col = lax.broadcasted_iota(jnp.int32, (QB, KW), 1) // CHUNK
    j = col - row
    return (j >= 0) & (j <= N_LEFT)


def _bias_fwd(rel_bias):
    nh, nrel = rel_bias.shape
    width = KW + QB

    def body(rb_ref, o_ref):
        onehot = _rel_index(nrel)
        gr = jnp.dot(rb_ref[...], onehot, preferred_element_type=F32, precision=lax.Precision.HIGHEST)
        valid = _band_valid()
        for h in range(nh):
            xrow = jnp.broadcast_to(gr[h:h + 1, :], (QB, width))
            rolled = pltpu.roll(xrow, 0, 1, stride=1, stride_axis=0)
            o_ref[h] = jnp.where(valid, rolled[:, QB:], NEG)

    return pl.pallas_call(
        body, name="bias_fwd",
        out_shape=jax.ShapeDtypeStruct((nh, QB, KW), F32),
        compiler_params=_params(VMEM_BIG),
    )(rel_bias)


def _bias_bwd(dbias, nrel):
    nh = dbias.shape[0]
    width = KW + QB

    def body(db_ref, o_ref, diag_ref):
        onehot = _rel_index(nrel)
        valid = _band_valid()
        rr = lax.broadcasted_iota(jnp.int32, (QB, QB), 0)
        cc = lax.broadcasted_iota(jnp.int32, (QB, QB), 1)
        flip = (rr + cc == QB - 1).astype(F32)
        for h in range(nh):
            rev = jnp.dot(flip, jnp.where(valid, db_ref[h], 0.0), preferred_element_type=F32,
                          precision=lax.Precision.HIGHEST)
            w = jnp.concatenate([jnp.zeros((QB, QB), F32), rev], axis=1)
            back = pltpu.roll(w, width - (QB - 1), 1, stride=1, stride_axis=0)
            diag_ref[h:h + 1, :] = _colsum(back)
        o_ref[...] = lax.dot_general(diag_ref[...], onehot, NT, preferred_element_type=F32,
                                     precision=lax.Precision.HIGHEST)

    return pl.pallas_call(
        body, name="bias_bwd",
        out_shape=jax.ShapeDtypeStruct((nh, nrel), F32),
        scratch_shapes=[pltpu.VMEM((nh, width), F32)],
        compiler_params=_params(VMEM_BIG),
    )(dbias)


def _conv_fwd(x, tab, ck, wci, wco, tm):
    s, d = x.shape
    nsh, _, cw = wci.shape

    def body(x_ref, t_ref, ck_ref, wci_ref, wco_ref, x1_ref, h_ref, bcx_ref, u_ref, y_ref, carry):
        @pl.when(pl.program_id(0) == 0)
        def _():
            carry[...] = jnp.zeros_like(carry)

        xv = x_ref[...]
        hb = ((xv * _rs(xv)) * t_ref[R_W1:R_W1 + 1, :] + t_ref[R_SH1:R_SH1 + 1, :]).astype(BF16)
        h_ref[...] = hb
        for j in range(nsh):
            bcx_ref[:, j * cw:(j + 1) * cw] = jnp.dot(hb, wci_ref[j], preferred_element_type=F32)
        bg, cg, xi = bcx_ref[:, 0:d], bcx_ref[:, d:2 * d], bcx_ref[:, 2 * d:3 * d]
        z = cg * xi
        row = lax.broadcasted_iota(jnp.int32, z.shape, 0)
        c1, c2 = carry[7:8, :], carry[6:7, :]
        z1 = jnp.where(row == 0, c1, pltpu.roll(z, 1, 0))
        z2 = jnp.where(row == 0, c2, jnp.where(row == 1, c1, pltpu.roll(z, 2, 0)))
        carry[...] = z[tm - 8:tm, :]
        conv = ck_ref[0:1, :] * z2 + ck_ref[1:2, :] * z1 + ck_ref[2:3, :] * z
        ub = (bg * conv).astype(BF16)
        u_ref[...] = ub
        yv = jnp.dot(ub, wco_ref[...], preferred_element_type=F32)
        y_ref[...] = yv
        x1_ref[...] = xv + (yv * _rs(yv)) * t_ref[R_P1:R_P1 + 1, :]

    return pl.pallas_call(
        body, name="conv_fwd", grid=(s // tm,),
        in_specs=[_rows(tm, d), _const(tab.shape), _const(ck.shape), _const(wci.shape), _const(wco.shape)],
        out_specs=(_rows(tm, d), _rows(tm, d), _rows(tm, 3 * d), _rows(tm, d), _rows(tm, d)),
        out_shape=(jax.ShapeDtypeStruct((s, d), F32), jax.ShapeDtypeStruct((s, d), BF16),
                   jax.ShapeDtypeStruct((s, 3 * d), F32), jax.ShapeDtypeStruct((s, d), BF16),
                   jax.ShapeDtypeStruct((s, d), F32)),
        scratch_shapes=[pltpu.VMEM((8, d), F32)],
        compiler_params=_params(VMEM_BIG),
    )(x, tab, ck, wci, wco)


def _ffn_fwd(x, tab, base, wfi, wfo, tgt, tm, name):
    s, d = x.shape
    nsh, _, fw = wfi.shape
    nh = nsh // 2
    with_loss = tgt is not None

    def body(*refs):
        if with_loss:
            x_ref, t_ref, wfi_ref, wfo_ref, tgt_ref, xo_ref, h_ref, gu_ref, a_ref, y_ref, loss_ref = refs
        else:
            x_ref, t_ref, wfi_ref, wfo_ref, xo_ref, h_ref, gu_ref, a_ref, y_ref = refs
        xv = x_ref[...]
        hb = ((xv * _rs(xv)) * t_ref[base + R_W2:base + R_W2 + 1, :]
              + t_ref[base + R_SH2:base + R_SH2 + 1, :]).astype(BF16)
        h_ref[...] = hb
        acc = jnp.zeros((tm, d), F32)
        for j in range(nh):
            g = jnp.dot(hb, wfi_ref[j], preferred_element_type=F32)
            u = jnp.dot(hb, wfi_ref[j + nh], preferred_element_type=F32)
            gu_ref[j] = g
            gu_ref[j + nh] = u
            ab = ((g * _sigmoid(g)) * u).astype(BF16)
            a_ref[j] = ab
            acc = acc + jnp.dot(ab, wfo_ref[j], preferred_element_type=F32)
        y_ref[...] = acc
        xo = xv + (acc * _rs(acc)) * t_ref[base + R_P2:base + R_P2 + 1, :]
        if with_loss:
            @pl.when(pl.program_id(0) == 0)
            def _():
                loss_ref[...] = jnp.zeros_like(loss_ref)

            err = xo - tgt_ref[...]
            xo_ref[...] = err * (1.0 / d)
            loss_ref[...] += jnp.sum(err * err)
        else:
            xo_ref[...] = xo

    in_specs = [_rows(tm, d), _const(tab.shape), _const(wfi.shape), _const(wfo.shape)]
    args = [x, tab, wfi, wfo]
    out_specs = [_rows(tm, d), _rows(tm, d), pl.BlockSpec((nsh, tm, fw), lambda i: (0, i, 0)),
                 pl.BlockSpec((nh, tm, fw), lambda i: (0, i, 0)), _rows(tm, d)]
    out_shape = [jax.ShapeDtypeStruct((s, d), F32), jax.ShapeDtypeStruct((s, d), BF16),
                 jax.ShapeDtypeStruct((nsh, s, fw), F32), jax.ShapeDtypeStruct((nh, s, fw), BF16),
                 jax.ShapeDtypeStruct((s, d), F32)]
    if with_loss:
        in_specs.append(_rows(tm, d))
        args.append(tgt)
        out_specs.append(pl.BlockSpec((8, LANES), lambda i: (0, 0)))
        out_shape.append(jax.ShapeDtypeStruct((8, LANES), F32))
    return pl.pallas_call(
        body, name=name, grid=(s // tm,), in_specs=in_specs, out_specs=tuple(out_specs),
        out_shape=tuple(out_shape), compiler_params=_params(VMEM_BIG),
    )(*args)


def _qkv_fwd(x, tab, wq, wkv, tm):
    s, d = x.shape
    nsh, _, kw = wkv.shape
    nh = nsh // 2
    base = 6

    def body(x_ref, t_ref, wq_ref, wkv_ref, hkv_ref, h1_ref, q_ref, k_ref, v_ref):
        xv = x_ref[...]
        n = xv * _rs(xv)
        hkv = (n * t_ref[R_KV:R_KV + 1, :] + t_ref[R_KV + 1:R_KV + 2, :]).astype(BF16)
        h1 = (n * t_ref[base + R_W1:base + R_W1 + 1, :] + t_ref[base + R_SH1:base + R_SH1 + 1, :]).astype(BF16)
        hkv_ref[...] = hkv
        h1_ref[...] = h1
        q_ref[...] = (jnp.dot(h1, wq_ref[...], preferred_element_type=F32) * (HEAD_DIM ** -0.5)).astype(BF16)
        for j in range(nh):
            k_ref[:, j * kw:(j + 1) * kw] = jnp.dot(hkv, wkv_ref[j], preferred_element_type=F32).astype(BF16)
            v_ref[:, j * kw:(j + 1) * kw] = jnp.dot(hkv, wkv_ref[j + nh], preferred_element_type=F32).astype(BF16)

    act = jax.ShapeDtypeStruct((s, d), BF16)
    return pl.pallas_call(
        body, name="qkv_fwd", grid=(s // tm,),
        in_specs=[_rows(tm, d), _const(tab.shape), _const(wq.shape), _const(wkv.shape)],
        out_specs=tuple([_rows(tm, d)] * 5), out_shape=(act,) * 5,
        compiler_params=_params(VMEM_BIG),
    )(x, tab, wq, wkv)


def _window_specs():
    return [pl.BlockSpec((QB, LANES), (lambda p, b, w=w: (jnp.maximum(b - 2 + w, 0), p))) for w in range(3)]


def _key_valid(b):
    col = lax.broadcasted_iota(jnp.int32, (QB, KW), 1) // CHUNK
    return (b * (QB // CHUNK) - N_LEFT + col) >= 0


def _head_masks():
    lane = lax.broadcasted_iota(jnp.int32, (1, LANES), 1)
    return [(lane // HEAD_DIM == hh) for hh in range(LANES // HEAD_DIM)]


def _attn_fwd(q, k, v, bias):
    s, d = q.shape
    npair, nb = d // LANES, s // QB
    hpp = LANES // HEAD_DIM

    def body(q_ref, k0, k1, k2, v0, v1, v2, bias_ref, o_ref, lse_ref):
        b = pl.program_id(1)
        qv = q_ref[...]
        kwin = jnp.concatenate([k0[...], k1[...], k2[...]], axis=0)
        vwin = jnp.concatenate([v0[...], v1[...], v2[...]], axis=0)
        valid = _key_valid(b)
        masks = _head_masks()
        o = jnp.zeros((QB, LANES), F32)
        lse = jnp.zeros((QB, LANES), F32)
        for hh in range(hpp):
            qm = jnp.where(masks[hh], qv, jnp.zeros_like(qv))
            vm = jnp.where(masks[hh], vwin, jnp.zeros_like(vwin))
            sc = lax.dot_general(qm, kwin, NT, preferred_element_type=F32) + bias_ref[hh]
            sc = jnp.where(valid, sc, NEG)
            m = jnp.max(sc, axis=-1, keepdims=True)
            p = jnp.exp(sc - m)
            l = jnp.sum(p, axis=-1, keepdims=True)
            o = o + jnp.dot(p.astype(BF16), vm, preferred_element_type=F32) * (1.0 / l)
            lse = jnp.where(masks[hh], m + jnp.log(l), lse)
        o_ref[...] = o.astype(BF16)
        lse_ref[...] = lse

    blk = pl.BlockSpec((QB, LANES), lambda p, b: (b, p))
    return pl.pallas_call(
        body, name="attn_fwd", grid=(npair, nb),
        in_specs=[blk] + _window_specs() + _window_specs()
                 + [pl.BlockSpec((hpp, QB, KW), lambda p, b: (p, 0, 0))],
        out_specs=(blk, blk),
        out_shape=(jax.ShapeDtypeStruct((s, d), BF16), jax.ShapeDtypeStruct((s, d), F32)),
        compiler_params=_params(VMEM_BIG),
    )(q, k, k, k, v, v, v, bias)


def _attn_out_fwd(o, x, tab, wo, tm):
    s, d = x.shape
    base = 6

    def body(o_ref, x_ref, t_ref, wo_ref, x3_ref, y_ref):
        yv = jnp.dot(o_ref[...], wo_ref[...], preferred_element_type=F32)
        y_ref[...] = yv
        x3_ref[...] = x_ref[...] + (yv * _rs(yv)) * t_ref[base + R_P1:base + R_P1 + 1, :]

    return pl.pallas_call(
        body, name="attn_out_fwd", grid=(s // tm,),
        in_specs=[_rows(tm, d), _rows(tm, d), _const(tab.shape), _const(wo.shape)],
        out_specs=(_rows(tm, d), _rows(tm, d)),
        out_shape=(jax.ShapeDtypeStruct((s, d), F32), jax.ShapeDtypeStruct((s, d), F32)),
        compiler_params=_params(VMEM_BIG),
    )(o, x, tab, wo)


def _ffn_bwd(dxo, x, y, gu, tab, base, wfi, wfo, tm, name):
    s, d = x.shape
    nsh, _, fw = wfi.shape
    nh = nsh // 2

    def body(dxo_ref, x_ref, y_ref, gu_ref, t_ref, wfi_ref, wfo_ref, dx_ref, dyb_ref, dgu_ref, sums_ref):
        @pl.when(pl.program_id(0) == 0)
        def _():
            sums_ref[...] = jnp.zeros_like(sums_ref)

        dxo_v = dxo_ref[...]
        yv = y_ref[...]
        ry = _rs(yv)
        ny = yv * ry
        sums_ref[R_P2:R_P2 + 1, :] += _colsum(dxo_v * ny)
        dyb = _norm_bwd(dxo_v * t_ref[base + R_P2:base + R_P2 + 1, :], ny, ry).astype(BF16)
        dyb_ref[...] = dyb
        dh = jnp.zeros((tm, d), F32)
        for j in range(nh):
            da = lax.dot_general(dyb, wfo_ref[j], NT, preferred_element_type=F32)
            g, u = gu_ref[j], gu_ref[j + nh]
            sg = _sigmoid(g)
            dg = (da * u * sg * (1.0 + g * (1.0 - sg))).astype(BF16)
            du = (da * g * sg).astype(BF16)
            dgu_ref[j] = dg
            dgu_ref[j + nh] = du
            dh = dh + lax.dot_general(dg, wfi_ref[j], NT, preferred_element_type=F32)
            dh = dh + lax.dot_general(du, wfi_ref[j + nh], NT, preferred_element_type=F32)
        xv = x_ref[...]
        r = _rs(xv)
        n = xv * r
        sums_ref[R_SH2:R_SH2 + 1, :] += _colsum(dh)
        sums_ref[R_W2:R_W2 + 1, :] += _colsum(dh * n)
        dx_ref[...] = dxo_v + _norm_bwd(dh * t_ref[base + R_W2:base + R_W2 + 1, :], n, r)

    return pl.pallas_call(
        body, name=name, grid=(s // tm,),
        in_specs=[_rows(tm, d), _rows(tm, d), _rows(tm, d), pl.BlockSpec((nsh, tm, fw), lambda i: (0, i, 0)),
                  _const(tab.shape), _const(wfi.shape), _const(wfo.shape)],
        out_specs=(_rows(tm, d), _rows(tm, d), pl.BlockSpec((nsh, tm, fw), lambda i: (0, i, 0)),
                   pl.BlockSpec((8, d), lambda i: (0, 0))),
        out_shape=(jax.ShapeDtypeStruct((s, d), F32), jax.ShapeDtypeStruct((s, d), BF16),
                   jax.ShapeDtypeStruct((nsh, s, fw), BF16), jax.ShapeDtypeStruct((8, d), F32)),
        compiler_params=_params(VMEM_BIG),
    )(dxo, x, y, gu, tab, wfi, wfo)


def _attn_out_bwd(dx, y, tab, wo, tm):
    s, d = y.shape
    base = 6

    def body(dx_ref, y_ref, t_ref, wo_ref, dyb_ref, do_ref, sums_ref):
        @pl.when(pl.program_id(0) == 0)
        def _():
            sums_ref[...] = jnp.zeros_like(sums_ref)

        dxv = dx_ref[...]
        yv = y_ref[...]
        ry = _rs(yv)
        ny = yv * ry
        sums_ref[R_P1:R_P1 + 1, :] += _colsum(dxv * ny)
        dyb = _norm_bwd(dxv * t_ref[base + R_P1:base + R_P1 + 1, :], ny, ry).astype(BF16)
        dyb_ref[...] = dyb
        do_ref[...] = lax.dot_general(dyb, wo_ref[...], NT, preferred_element_type=F32).astype(BF16)

    return pl.pallas_call(
        body, name="attn_out_bwd", grid=(s // tm,),
        in_specs=[_rows(tm, d), _rows(tm, d), _const(tab.shape), _const(wo.shape)],
        out_specs=(_rows(tm, d), _rows(tm, d), pl.BlockSpec((8, d), lambda i: (0, 0))),
        out_shape=(jax.ShapeDtypeStruct((s, d), BF16), jax.ShapeDtypeStruct((s, d), BF16),
                   jax.ShapeDtypeStruct((8, d), F32)),
        compiler_params=_params(VMEM_BIG),
    )(dx, y, tab, wo)


def _attn_bwd(q, k, v, do, lse, bias):
    s, d = q.shape
    npair, nb = d // LANES, s // QB
    hpp = LANES // HEAD_DIM

    def body(q_ref, k0, k1, k2, v0, v1, v2, do_ref, lse_ref, bias_ref, dq_ref, dk_ref, dv_ref, db_ref):
        b = pl.program_id(1)

        @pl.when(b == 0)
        def _():
            dk_ref[...] = jnp.zeros_like(dk_ref)
            dv_ref[...] = jnp.zeros_like(dv_ref)
            db_ref[...] = jnp.zeros_like(db_ref)

        qv = q_ref[...]
        dov = do_ref[...]
        lsev = lse_ref[...]
        kwin = jnp.concatenate([k0[...], k1[...], k2[...]], axis=0)
        vwin = jnp.concatenate([v0[...], v1[...], v2[...]], axis=0)
        valid = _key_valid(b)
        masks = _head_masks()
        dq = jnp.zeros((QB, LANES), F32)
        dkw = jnp.zeros((KW, LANES), F32)
        dvw = jnp.zeros((KW, LANES), F32)
        for hh in range(hpp):
            qm = jnp.where(masks[hh], qv, jnp.zeros_like(qv))
            dom = jnp.where(masks[hh], dov, jnp.zeros_like(dov))
            km = jnp.where(masks[hh], kwin, jnp.zeros_like(kwin))
            lse_h = jnp.max(jnp.where(masks[hh], lsev, NEG), axis=-1, keepdims=True)
            sc = lax.dot_general(qm, kwin, NT, preferred_element_type=F32) + bias_ref[hh]
            sc = jnp.where(valid, sc, NEG)
            p = jnp.exp(sc - lse_h)
            dp = lax.dot_general(dom, vwin, NT, preferred_element_type=F32)
            ds = p * (dp - jnp.sum(dp * p, axis=-1, keepdims=True))
            db_ref[hh] += ds
            dsb = ds.astype(BF16)
            dq = dq + jnp.dot(dsb, km, preferred_element_type=F32)
            dkw = dkw + lax.dot_general(dsb, qm, TN, preferred_element_type=F32)
            dvw = dvw + lax.dot_general(p.astype(BF16), dom, TN, preferred_element_type=F32)
        dq_ref[...] = (dq * (HEAD_DIM ** -0.5)).astype(BF16)
        for w in range(3):
            start = pl.multiple_of(jnp.maximum(b - 2 + w, 0) * QB, QB)
            dk_ref[pl.ds(start, QB), :] += dkw[w * QB:(w + 1) * QB, :]
            dv_ref[pl.ds(start, QB), :] += dvw[w * QB:(w + 1) * QB, :]

    blk = pl.BlockSpec((QB, LANES), lambda p, b: (b, p))
    col = pl.BlockSpec((s, LANES), lambda p, b: (0, p))
    pair = pl.BlockSpec((hpp, QB, KW), lambda p, b: (p, 0, 0))
    return pl.pallas_call(
        body, name="attn_bwd", grid=(npair, nb),
        in_specs=[blk] + _window_specs() + _window_specs() + [blk, blk, pair],
        out_specs=(blk, col, col, pair),
        out_shape=(jax.ShapeDtypeStruct((s, d), BF16), jax.ShapeDtypeStruct((s, d), F32),
                   jax.ShapeDtypeStruct((s, d), F32), jax.ShapeDtypeStruct(bias.shape, F32)),
        compiler_params=_params(VMEM_BIG),
    )(q, k, k, k, v, v, v, do, lse, bias)


def _qkv_bwd(dres, dq, dk, dv, x, tab, wq, wkv, tm):
    s, d = x.shape
    nsh, _, kw = wkv.shape
    nh = nsh // 2
    base = 6

    def body(dres_ref, dq_ref, dk_ref, dv_ref, x_ref, t_ref, wq_ref, wkv_ref, dx_ref, dkv_ref, sums_ref):
        @pl.when(pl.program_id(0) == 0)
        def _():
            sums_ref[...] = jnp.zeros_like(sums_ref)

        dh1 = lax.dot_general(dq_ref[...], wq_ref[...], NT, preferred_element_type=F32)
        dkv_ref[:, 0:d] = dk_ref[...].astype(BF16)
        dkv_ref[:, d:2 * d] = dv_ref[...].astype(BF16)
        dhkv = jnp.zeros((tm, d), F32)
        for j in range(nsh):
            dhkv = dhkv + lax.dot_general(dkv_ref[:, j * kw:(j + 1) * kw], wkv_ref[j], NT,
                                          preferred_element_type=F32)
        xv = x_ref[...]
        r = _rs(xv)
        n = xv * r
        sums_ref[0:1, :] += _colsum(dh1 * n)
        sums_ref[1:2, :] += _colsum(dh1)
        sums_ref[2:3, :] += _colsum(dhkv * n)
        sums_ref[3:4, :] += _colsum(dhkv)
        dn = dh1 * t_ref[base + R_W1:base + R_W1 + 1, :] + dhkv * t_ref[R_KV:R_KV + 1, :]
        dx_ref[...] = dres_ref[...] + _norm_bwd(dn, n, r)

    return pl.pallas_call(
        body, name="qkv_bwd", grid=(s // tm,),
        in_specs=[_rows(tm, d)] * 5 + [_const(tab.shape), _const(wq.shape), _const(wkv.shape)],
        out_specs=(_rows(tm, d), _rows(tm, 2 * d), pl.BlockSpec((8, d), lambda i: (0, 0))),
        out_shape=(jax.ShapeDtypeStruct((s, d), F32), jax.ShapeDtypeStruct((s, 2 * d), BF16),
                   jax.ShapeDtypeStruct((8, d), F32)),
        compiler_params=_params(VMEM_BIG),
    )(dres, dq, dk, dv, x, tab, wq, wkv)


def _conv_bwd(dx1, x, y, bcx, tab, ck, wci, wco, tm):
    s, d = x.shape
    nsh, _, cw = wci.shape
    nt = s // tm

    def rev(i):
        return (nt - 1 - i, 0)

    def halo(i):
        return (jnp.maximum((nt - 1 - i) * (tm // 8) - 1, 0), 0)

    def body(dx_ref, x_ref, y_ref, bcx_ref, halo_ref, t_ref, ck_ref, wci_ref, wco_ref,
             dx0_ref, dyb_ref, dbcx_ref, sums_ref, dck_ref, carry):
        i = pl.program_id(0)

        @pl.when(i == 0)
        def _():
            sums_ref[...] = jnp.zeros_like(sums_ref)
            dck_ref[...] = jnp.zeros_like(dck_ref)
            carry[...] = jnp.zeros_like(carry)

        dxv = dx_ref[...]
        yv = y_ref[...]
        ry = _rs(yv)
        ny = yv * ry
        sums_ref[R_P1:R_P1 + 1, :] += _colsum(dxv * ny)
        dyb = _norm_bwd(dxv * t_ref[R_P1:R_P1 + 1, :], ny, ry).astype(BF16)
        dyb_ref[...] = dyb
        du = lax.dot_general(dyb, wco_ref[...], NT, preferred_element_type=F32)
        bg, cg, xi = bcx_ref[:, 0:d], bcx_ref[:, d:2 * d], bcx_ref[:, 2 * d:3 * d]
        z = cg * xi
        zp = halo_ref[:, d:2 * d] * halo_ref[:, 2 * d:3 * d]
        zp = jnp.where(i == nt - 1, jnp.zeros_like(zp), zp)
        row = lax.broadcasted_iota(jnp.int32, z.shape, 0)
        c1, c2 = zp[7:8, :], zp[6:7, :]
        z1 = jnp.where(row == 0, c1, pltpu.roll(z, 1, 0))
        z2 = jnp.where(row == 0, c2, jnp.where(row == 1, c1, pltpu.roll(z, 2, 0)))
        k0, k1, k2 = ck_ref[0:1, :], ck_ref[1:2, :], ck_ref[2:3, :]
        conv = k0 * z2 + k1 * z1 + k2 * z
        dconv = du * bg
        dck_ref[0:1, :] += _colsum(dconv * z2)
        dck_ref[1:2, :] += _colsum(dconv * z1)
        dck_ref[2:3, :] += _colsum(dconv * z)
        n1, n2 = carry[0:1, :], carry[1:2, :]
        d1 = jnp.where(row == tm - 1, n1, pltpu.roll(dconv, tm - 1, 0))
        d2 = jnp.where(row == tm - 1, n2, jnp.where(row == tm - 2, n1, pltpu.roll(dconv, tm - 2, 0)))
        carry[...] = dconv[0:8, :]
        dz = k2 * dconv + k1 * d1 + k0 * d2
        dbcx_ref[:, 0:d] = (du * conv).astype(BF16)
        dbcx_ref[:, d:2 * d] = (dz * xi).astype(BF16)
        dbcx_ref[:, 2 * d:3 * d] = (dz * cg).astype(BF16)
        dh = jnp.zeros((tm, d), F32)
        for j in range(nsh):
            dh = dh + lax.dot_general(dbcx_ref[:, j * cw:(j + 1) * cw], wci_ref[j], NT,
                                      preferred_element_type=F32)
        xv = x_ref[...]
        r = _rs(xv)
        n = xv * r
        sums_ref[R_W1:R_W1 + 1, :] += _colsum(dh * n)
        sums_ref[R_SH1:R_SH1 + 1, :] += _colsum(dh)
        dx0_ref[...] = dxv + _norm_bwd(dh * t_ref[R_W1:R_W1 + 1, :], n, r)

    rrow = lambda cols: pl.BlockSpec((tm, cols), rev)
    acc = pl.BlockSpec((8, d), lambda i: (0, 0))
    return pl.pallas_call(
        body, name="conv_bwd", grid=(nt,),
        in_specs=[rrow(d), rrow(d), rrow(d), rrow(3 * d), pl.BlockSpec((8, 3 * d), halo),
                  _const(tab.shape), _const(ck.shape), _const(wci.shape), _const(wco.shape)],
        out_specs=(rrow(d), rrow(d), rrow(3 * d), acc, acc),
        out_shape=(jax.ShapeDtypeStruct((s, d), F32), jax.ShapeDtypeStruct((s, d), BF16),
                   jax.ShapeDtypeStruct((s, 3 * d), BF16), jax.ShapeDtypeStruct((8, d), F32),
                   jax.ShapeDtypeStruct((8, d), F32)),
        scratch_shapes=[pltpu.VMEM((8, d), F32)],
        compiler_params=_params(VMEM_BIG),
    )(dx1, x, y, bcx, bcx, tab, ck, wci, wco)


def _wgrad(a, b, nblk, a_spec, b_spec, m, n, tk, name):
    s = a.shape[-2]
    nk = s // tk

    def body(a_ref, b_ref, o_ref, acc):
        kk = pl.program_id(1)

        @pl.when(kk == 0)
        def _():
            acc[...] = jnp.zeros_like(acc)

        acc[...] += lax.dot_general(a_ref[...], b_ref[...], TN, preferred_element_type=F32)

        @pl.when(kk == nk - 1)
        def _():
            o_ref[...] = acc[...].astype(BF16)

    return pl.pallas_call(
        body, name=name, grid=(nblk, nk),
        in_specs=[a_spec, b_spec],
        out_specs=pl.BlockSpec((None, m, n), lambda j, kk: (j, 0, 0)),
        out_shape=jax.ShapeDtypeStruct((nblk, m, n), BF16),
        scratch_shapes=[pltpu.VMEM((m, n), F32)],
        compiler_params=_params(VMEM_BIG),
    )(a, b)


def _wgrad_cols(a, b, nblk, tk, name):
    m, n = a.shape[1], b.shape[1] // nblk
    return _wgrad(a, b, nblk, pl.BlockSpec((tk, m), lambda j, kk: (kk, 0)),
                  pl.BlockSpec((tk, n), lambda j, kk: (kk, j)), m, n, tk, name)


def _wgrad_bstack(a, b, tk, name):
    nblk, _, n = b.shape
    m = a.shape[1]
    return _wgrad(a, b, nblk, pl.BlockSpec((tk, m), lambda j, kk: (kk, 0)),
                  pl.BlockSpec((None, tk, n), lambda j, kk: (j, kk, 0)), m, n, tk, name)


def _wgrad_astack(a, b, tk, name):
    nblk, _, m = a.shape
    n = b.shape[1]
    return _wgrad(a, b, nblk, pl.BlockSpec((None, tk, m), lambda j, kk: (j, kk, 0)),
                  pl.BlockSpec((tk, n), lambda j, kk: (kk, 0)), m, n, tk, name)


def _adamw_math(w, g, m, v):
    m = ADAM_B1 * m + (1.0 - ADAM_B1) * g
    v = ADAM_B2 * v + (1.0 - ADAM_B2) * (g * g)
    m_hat = m / (1.0 - ADAM_B1 ** ADAM_STEP)
    v_hat = v / (1.0 - ADAM_B2 ** ADAM_STEP)
    delta = -ADAM_LR * (m_hat / (jnp.sqrt(v_hat) + ADAM_EPS) + ADAM_WD * w)
    return delta, m, v


def _adamw_reduce(parts, w, m, v, tr, name):
    nl, r, c = w.shape
    tr = _row_tile(r, tr)

    def body(*refs):
        p_refs = refs[:nl]
        w_ref, m_ref, v_ref, g_ref, d_ref, mo_ref, vo_ref = refs[nl:]
        layer = pl.program_id(0)

        def partial(i):
            val = p_refs[0][i].astype(F32)
            for q in range(1, nl):
                val = jnp.where(layer == q, p_refs[q][i].astype(F32), val)
            return val

        g = partial(0)
        for i in range(1, N_DEV):
            g = g + partial(i)
        g_ref[...] = g
        d_ref[...], mo_ref[...], vo_ref[...] = _adamw_math(w_ref[...], g, m_ref[...], v_ref[...])

    blk = pl.BlockSpec((None, tr, c), lambda l, i: (l, i, 0))
    out = jax.ShapeDtypeStruct((nl, r, c), F32)
    p_specs = [pl.BlockSpec((N_DEV, tr, c), (lambda l, i, q=q: (0, jnp.where(l == q, i, 0), 0))) for q in range(nl)]
    return pl.pallas_call(
        body, name=name, grid=(nl, r // tr),
        in_specs=p_specs + [blk, blk, blk],
        out_specs=(blk,) * 4, out_shape=(out,) * 4,
        compiler_params=_params(VMEM_BIG),
    )(*parts, w, m, v)


def _adamw_outer(sct, dm, w, m, v, tr, name):
    nl, d, c = w.shape

    def body(s_ref, dm_ref, w_ref, m_ref, v_ref, g_ref, d_ref, mo_ref, vo_ref):
        g = jnp.dot(s_ref[...], dm_ref[...], preferred_element_type=F32)
        g_ref[...] = g
        d_ref[...], mo_ref[...], vo_ref[...] = _adamw_math(w_ref[...], g, m_ref[...], v_ref[...])

    blk = pl.BlockSpec((None, tr, c), lambda l, i: (l, i, 0))
    out = jax.ShapeDtypeStruct((nl, d, c), F32)
    return pl.pallas_call(
        body, name=name, grid=(nl, d // tr),
        in_specs=[pl.BlockSpec((tr, N_DEV), lambda l, i: (i, 0)),
                  pl.BlockSpec((None, N_DEV, c), lambda l, i: (l, 0, 0)), blk, blk, blk],
        out_specs=(blk,) * 4, out_shape=(out,) * 4,
        compiler_params=_params(VMEM_BIG),
    )(sct, dm, w, m, v)


def _pad_rows(a, rows):
    return jnp.concatenate([a, jnp.zeros((rows - a.shape[0],) + a.shape[1:], a.dtype)], axis=0)


def kernel(x, c, mod_w, mod_b, norm_g, ffn_w_in, ffn_w_out, conv_w_in, conv_k, conv_w_out, kv_mod_w, kv_mod_b, kv_norm_g, w_kv, attn_w_q, attn_w_o, rel_bias, loss_target, m_mod_w, m_mod_b, m_norm_g, m_ffn_w_in, m_ffn_w_out, m_conv_w_in, m_conv_k, m_conv_w_out, m_kv_mod_w, m_kv_mod_b, m_kv_norm_g, m_w_kv, m_attn_w_q, m_attn_w_o, m_rel_bias, v_mod_w, v_mod_b, v_norm_g, v_ffn_w_in, v_ffn_w_out, v_conv_w_in, v_conv_k, v_conv_w_out, v_kv_mod_w, v_kv_mod_b, v_kv_norm_g, v_w_kv, v_attn_w_q, v_attn_w_o, v_rel_bias):
    s, d = x.shape[1], x.shape[2]
    dq = d // LANES
    dsh = d // N_DEV
    nl = mod_w.shape[0]
    mw = mod_w.shape[2]
    kmw = kv_mod_w.shape[1]
    fw = ffn_w_in.shape[2]
    nh, nrel = rel_bias.shape[1], rel_bias.shape[2]
    tm = min(256, s)
    tk = min(512, s)
    me = 4 * lax.axis_index("x") + 2 * lax.axis_index("y") + lax.axis_index("c")

    x0 = x[0]
    tgt = loss_target[0]

    small1 = jnp.concatenate([c.reshape(dq, LANES), norm_g.reshape(dq, LANES),
                              _pad_rows(conv_k[0], 8).reshape(dq, LANES)], axis=0)
    cast = lambda *ws: [a.astype(BF16) for a in ws]
    gath = lambda ws: (ws, ["gather"] * len(ws))
    (h_conv, h_ffn0, h_attn, h_ffn1), token = _xstart(
        [gath(cast(conv_w_in[0], conv_w_out[0])), gath(cast(ffn_w_in[0], ffn_w_out[0])),
         gath(cast(w_kv, attn_w_q[0], attn_w_o[0])), gath(cast(ffn_w_in[1], ffn_w_out[1]))], "gather_start")
    (sm,) = _exchange([small1 + token[0, 0]], ["gather"], "gather_small")
    c_all = sm[:, 0:dq].reshape(N_DEV, d)
    ng_full = jnp.transpose(sm[:, dq:2 * dq].reshape(N_DEV, 8, dsh), (1, 0, 2)).reshape(8, d)
    ck_full = jnp.transpose(sm[:, 2 * dq:3 * dq].reshape(N_DEV, 8, dsh), (1, 0, 2)).reshape(8, d)

    modcols, silu_c = _mod_fwd(c_all, mod_w, kv_mod_w)
    (modall,) = _exchange([modcols], ["gather"], "gather_mod")
    mine = lax.dynamic_index_in_dim(modall, me, axis=1, keepdims=False)
    modrow = jnp.stack([mine[:, l * mw:(l + 1) * mw].reshape(6, d) for l in range(nl)])
    kvrow = mine[:, nl * mw:nl * mw + kmw].reshape(2, d)
    tab, modval = _vec_prep(modrow, mod_b.reshape(nl, 6, d), kvrow, kv_mod_b.reshape(2, d), ng_full,
                            kv_norm_g.reshape(1, d))
    bias = _bias_fwd(rel_bias[0])

    wci, wco = _xwait(h_conv, bias, "gather_wait_conv")
    wco = wco.reshape(d, d)
    x1, h1a, bcx, ua, ya = _conv_fwd(x0, tab, ck_full, wci, wco, tm)
    wfi0, wfo0 = _xwait(h_ffn0, x1, "gather_wait_ffn0")
    wfo0 = wfo0.reshape(N_DEV // 2, -1, d)
    x2, h2a, gua, aa, y2a = _ffn_fwd(x1, tab, 0, wfi0, wfo0, None, tm, "ffn_fwd0")
    wkv, wq, wo = _xwait(h_attn, x2, "gather_wait_attn")
    wq, wo = wq.reshape(d, d), wo.reshape(d, d)
    hkv, h1b, q, k, v = _qkv_fwd(x2, tab, wq, wkv, tm)
    o, lse = _attn_fwd(q, k, v, bias)
    x3, yb = _attn_out_fwd(o, x2, tab, wo, tm)
    wfi1, wfo1 = _xwait(h_ffn1, x3, "gather_wait_ffn1")
    wfo1 = wfo1.reshape(N_DEV // 2, -1, d)
    dx4, h2b, gub, ab, y2b, loss_acc = _ffn_fwd(x3, tab, 6, wfi1, wfo1, tgt, tm, "ffn_fwd1")
    loss = lax.psum(loss_acc[0, 0] * (0.5 / d), ("x", "y", "c"))

    scat = lambda ws: [(ws, ["scatter"] * len(ws))]
    dx3, dy2b, dgub, sums_f1 = _ffn_bwd(dx4, x3, y2b, gub, tab, 6, wfi1, wfo1, tm, "ffn_bwd1")
    g_wfi1 = _wgrad_bstack(h2b, dgub, tk, "wgrad_ffn_in1")
    g_wfo1 = _wgrad_astack(ab, dy2b, tk, "wgrad_ffn_out1").reshape(N_DEV, -1, d)
    (h_g1,), token = _xstart(scat([g_wfi1, g_wfo1]), "grads_start_ffn1")
    tab = tab + token[0, 0]
    dyb, do, sums_o = _attn_out_bwd(dx3, yb, tab, wo, tm)
    g_wo = _wgrad_cols(o, dyb, 1, tk, "wgrad_o").reshape(N_DEV, dsh, d)
    dqb, dk, dv, dbias = _attn_bwd(q, k, v, do, lse, bias)
    g_wq = _wgrad_cols(h1b, dqb, 1, tk, "wgrad_q").reshape(N_DEV, dsh, d)
    dx2, dkvb, sums_q = _qkv_bwd(dx3, dqb, dk, dv, x2, tab, wq, wkv, tm)
    g_wkv = _wgrad_cols(hkv, dkvb, N_DEV, tk, "wgrad_kv")
    (h_g2,), token = _xstart(scat([g_wkv, g_wq, g_wo]), "grads_start_attn")
    tab = tab + token[0, 0]
    dx1, dy2a, dgua, sums_f0 = _ffn_bwd(dx2, x1, y2a, gua, tab, 0, wfi0, wfo0, tm, "ffn_bwd0")
    g_wfi0 = _wgrad_bstack(h2a, dgua, tk, "wgrad_ffn_in0")
    g_wfo0 = _wgrad_astack(aa, dy2a, tk, "wgrad_ffn_out0").reshape(N_DEV, -1, d)
    (h_g3,), token = _xstart(scat([g_wfi0, g_wfo0]), "grads_start_ffn0")
    tab = tab + token[0, 0]
    dx0, dya, dbcx, sums_c, dck = _conv_bwd(dx1, x0, ya, bcx, tab, ck_full, wci, wco, tm)
    g_wci = _wgrad_cols(h1a, dbcx, N_DEV, tk, "wgrad_conv_in")
    g_wco = _wgrad_cols(ua, dya, 1, tk, "wgrad_conv_out").reshape(N_DEV, dsh, d)
    (h_g4,), token = _xstart(scat([g_wci, g_wco]), "grads_start_conv")

    drel = _bias_bwd(dbias, nrel)
    dmod, dng, dkvg = _vec_bwd(sums_c, sums_f0, sums_q, sums_o, sums_f1, modval, ng_full, kv_norm_g.reshape(1, d))

    relw = -(-nrel // LANES) * LANES
    drel_p = jnp.concatenate([drel, jnp.zeros((nh, relw - nrel), F32)], axis=1)
    small3 = jnp.concatenate([dmod.reshape(16 * dq, LANES), dng.reshape(8 * dq, LANES), dkvg.reshape(8 * dq, LANES),
                              dck.reshape(8 * dq, LANES), drel_p.reshape(nh * relw // LANES, LANES)], axis=0)
    (sm,) = _exchange([small3 + token[0, 0]], ["gather"], "gather_small_grads")
    o1, o2, o3, o4 = 16 * dq, 24 * dq, 32 * dq, 40 * dq
    dmod_all = sm[:, 0:o1].reshape(N_DEV, 16, d)
    mine_cols = lambda a: lax.dynamic_slice_in_dim(a, me * dsh, dsh, axis=2)
    dng_parts = mine_cols(sm[:, o1:o2].reshape(N_DEV, 8, d))
    dkvg_parts = sm[:, o2:o3].reshape(N_DEV, 8, d)[:, 0:1]
    dck_parts = mine_cols(sm[:, o3:o4].reshape(N_DEV, 8, d))[:, 0:3]
    drel_parts = sm[:, o4:].reshape(N_DEV, nh, relw)[:, :, 0:nrel]

    def update(parts, w, m, v, name, layers=1):
        shp = w.shape
        w3, m3, v3 = (a.reshape(layers, -1, shp[-1]) for a in (w, m, v))
        outs = _adamw_reduce([p.reshape(N_DEV, -1, shp[-1]) for p in parts], w3, m3, v3, 256, name)
        return [a.reshape(shp) for a in outs]

    p_wfi1, p_wfo1 = _xwait(h_g1, sm, "grads_wait_ffn1")
    p_wfi0, p_wfo0 = _xwait(h_g3, p_wfi1, "grads_wait_ffn0")
    u_ffn_in = update([p_wfi0, p_wfi1], ffn_w_in, m_ffn_w_in, v_ffn_w_in, "adamw_ffn_in", 2)
    u_ffn_out = update([p_wfo0, p_wfo1], ffn_w_out, m_ffn_w_out, v_ffn_w_out, "adamw_ffn_out", 2)
    p_wkv, p_wq, p_wo = _xwait(h_g2, u_ffn_out[0], "grads_wait_attn")
    u_w_kv = update([p_wkv], w_kv, m_w_kv, v_w_kv, "adamw_w_kv")
    u_w_q = update([p_wq], attn_w_q, m_attn_w_q, v_attn_w_q, "adamw_w_q")
    u_w_o = update([p_wo], attn_w_o, m_attn_w_o, v_attn_w_o, "adamw_w_o")

    sct = jnp.transpose(silu_c)
    dm_mod = jnp.stack([lax.dynamic_slice_in_dim(dmod_all[:, 6 * l:6 * l + 6].reshape(N_DEV, 6 * d), me * mw, mw, axis=1)
                        for l in range(nl)]).astype(BF16)
    dm_kv = lax.dynamic_slice_in_dim(dmod_all[:, R_KV:R_KV + 2].reshape(N_DEV, 2 * d), me * kmw, kmw, axis=1)
    u_mod_w = _adamw_outer(sct, dm_mod, mod_w, m_mod_w, v_mod_w, min(256, d), "adamw_mod_w")
    u_kv_mod_w = [a[0] for a in _adamw_outer(sct, dm_kv.astype(BF16)[None], kv_mod_w[None], m_kv_mod_w[None],
                                             v_kv_mod_w[None], min(256, d), "adamw_kv_mod_w")]

    modb_parts = jnp.stack([dmod_all[:, 6 * l:6 * l + 6].reshape(N_DEV, 6 * d) for l in range(nl)], axis=1)
    u_mod_b = update([modb_parts], mod_b, m_mod_b, v_mod_b, "adamw_mod_b")
    u_norm_g = update([dng_parts], norm_g.reshape(8, dsh), m_norm_g.reshape(8, dsh), v_norm_g.reshape(8, dsh), "adamw_norm_g")
    u_norm_g = [a.reshape(norm_g.shape) for a in u_norm_g]
    u_conv_k = update([dck_parts], conv_k, m_conv_k, v_conv_k, "adamw_conv_k")
    kvb_parts = dmod_all[:, R_KV:R_KV + 2].reshape(N_DEV, 1, 2 * d)
    u_kv_mod_b = [a.reshape(kv_mod_b.shape) for a in update([kvb_parts], kv_mod_b.reshape(1, -1), m_kv_mod_b.reshape(1, -1),
                                                            v_kv_mod_b.reshape(1, -1), "adamw_kv_mod_b")]
    u_kv_norm_g = [a.reshape(kv_norm_g.shape) for a in update([dkvg_parts], kv_norm_g.reshape(1, -1), m_kv_norm_g.reshape(1, -1),
                                                              v_kv_norm_g.reshape(1, -1), "adamw_kv_norm_g")]
    u_rel = update([drel_parts], rel_bias, m_rel_bias, v_rel_bias, "adamw_rel_bias")

    p_wci, p_wco = _xwait(h_g4, u_mod_w[0], "grads_wait_conv")
    u_conv_in = update([p_wci], conv_w_in, m_conv_w_in, v_conv_w_in, "adamw_conv_in")
    u_conv_out = update([p_wco], conv_w_out, m_conv_w_out, v_conv_w_out, "adamw_conv_out")

    ups = [u_mod_w, u_mod_b, u_norm_g, u_ffn_in, u_ffn_out, u_conv_in, u_conv_k, u_conv_out, u_kv_mod_w, u_kv_mod_b,
           u_kv_norm_g, u_w_kv, u_w_q, u_w_o, u_rel]
    return (loss, dx0[None], *[u[0] for u in ups], *[u[1] for u in ups], *[u[2] for u in ups], *[u[3] for u in ups])
```

```python
import functools

import jax
import jax.numpy as jnp
from jax import lax
from jax.experimental import pallas as pl
from jax.experimental.pallas import tpu as pltpu

F32 = jnp.float32
BF16 = jnp.bfloat16

EPS = 1e-6
CHUNK = 64
HEAD_DIM = 64
N_LEFT = 8
LANES = 128
QB = 4 * CHUNK
KW = QB + N_LEFT * CHUNK
NEG = -1e30
N_DEV = 8

ADAM_LR = 0.001
ADAM_B1 = 0.9
ADAM_B2 = 0.999
ADAM_EPS = 1e-08
ADAM_WD = 0.01
ADAM_STEP = 10

VMEM_BIG = 56 * 1024 * 1024

NT = (((1,), (1,)), ((), ()))
TN = (((0,), (0,)), ((), ()))

R_W1, R_SH1, R_P1, R_W2, R_SH2, R_P2 = range(6)
R_KV = 12


def _params(vmem):
    return pltpu.CompilerParams(vmem_limit_bytes=vmem)


def _row_tile(rows, cap):
    for t in range(min(cap, rows) // 16 * 16, 0, -16):
        if rows % t == 0:
            return t
    return rows


def _rows(tm, cols):
    return pl.BlockSpec((tm, cols), lambda i: (i, 0))


def _const(shape):
    nd = len(shape)
    return pl.BlockSpec(shape, lambda *_: (0,) * nd, pipeline_mode=pl.Buffered(1))


def _rs(x):
    return lax.rsqrt(jnp.mean(x * x, axis=-1, keepdims=True) + EPS)


def _norm_bwd(d, n, r):
    return r * (d - n * jnp.mean(d * n, axis=-1, keepdims=True))


def _colsum(a):
    return jnp.sum(a, axis=0, keepdims=True)


def _sigmoid(g):
    return 1.0 / (1.0 + jnp.exp(-g))


def _exchange(arrays, modes, name):
    n = len(arrays)
    out_shape = []
    for a, mode in zip(arrays, modes):
        shp = (N_DEV,) + a.shape if mode == "gather" else a.shape
        out_shape.append(jax.ShapeDtypeStruct(shp, a.dtype))

    def body(*refs):
        ins, outs = refs[:n], refs[n:2 * n]
        send_sems, recv_sems, local_sems = refs[2 * n:]
        x, y, c = lax.axis_index("x"), lax.axis_index("y"), lax.axis_index("c")
        me = 4 * x + 2 * y + c
        local, sends, recvs = [], [], []
        for a in range(n):
            own = ins[a] if modes[a] == "gather" else ins[a].at[me]
            cp = pltpu.make_async_copy(own, outs[a].at[me], local_sems.at[a])
            cp.start()
            local.append(cp)
        for k in range(1, N_DEV):
            px = 1 - x if k & 4 else x
            py = 1 - y if k & 2 else y
            pc = 1 - c if k & 1 else c
            peer = 4 * px + 2 * py + pc
            for a in range(n):
                src = ins[a] if modes[a] == "gather" else ins[a].at[peer]
                sem = a * (N_DEV - 1) + k - 1
                cp = pltpu.make_async_remote_copy(
                    src_ref=src, dst_ref=outs[a].at[me],
                    send_sem=send_sems.at[sem], recv_sem=recv_sems.at[sem],
                    device_id=(px, py, pc), device_id_type=pl.DeviceIdType.MESH)
                cp.start()
                sends.append(cp)
                recvs.append(pltpu.make_async_remote_copy(
                    src_ref=src, dst_ref=outs[a].at[peer],
                    send_sem=send_sems.at[sem], recv_sem=recv_sems.at[sem],
                    device_id=(px, py, pc), device_id_type=pl.DeviceIdType.MESH))
        for cp in recvs:
            cp.wait_recv()
        for cp in sends:
            cp.wait_send()
        for cp in local:
            cp.wait()

    any_spec = pl.BlockSpec(memory_space=pl.ANY)
    return pl.pallas_call(
        body, name=name,
        out_shape=tuple(out_shape),
        in_specs=[any_spec] * n,
        out_specs=tuple([any_spec] * n),
        scratch_shapes=[
            pltpu.SemaphoreType.DMA((n * (N_DEV - 1),)),
            pltpu.SemaphoreType.DMA((n * (N_DEV - 1),)),
            pltpu.SemaphoreType.DMA((n,)),
        ],
    )(*arrays)


def _peers(x, y, c):
    out = []
    for k in range(1, N_DEV):
        px = 1 - x if k & 4 else x
        py = 1 - y if k & 2 else y
        pc = 1 - c if k & 1 else c
        out.append((k - 1, (px, py, pc), 4 * px + 2 * py + pc))
    return out


def _land_shape(a, mode):
    return (N_DEV,) + a.shape if mode == "gather" else a.shape


_HBM = pl.BlockSpec(memory_space=pltpu.HBM)
_SEM = pl.BlockSpec(memory_space=pltpu.SEMAPHORE)
_EFFECT = pltpu.SideEffectType.DATAFLOW_SIDE_EFFECTING


def _xstart(groups, after, name):
    flat = [(a, m) for arrays, modes in groups for a, m in zip(arrays, modes)]
    n, ngr = len(flat), len(groups)
    sizes = [len(arrays) for arrays, _ in groups]
    npeer = N_DEV - 1

    def body(*refs):
        ins, lands = refs[:n], refs[n:2 * n]
        outs = refs[2 * n + 1:]
        sems = outs[:2 * ngr]
        token = outs[2 * ngr + 2 * n]
        local_sems = outs[2 * ngr + 2 * n + 1]
        x, y, c = lax.axis_index("x"), lax.axis_index("y"), lax.axis_index("c")
        me = 4 * x + 2 * y + c
        local = []
        for a in range(n):
            own = ins[a] if flat[a][1] == "gather" else ins[a].at[me]
            cp = pltpu.make_async_copy(own, lands[a].at[me], local_sems.at[a])
            cp.start()
            local.append(cp)
        a = 0
        for g in range(ngr):
            for j in range(sizes[g]):
                mode = flat[a][1]
                for slot, peer, pidx in _peers(x, y, c):
                    pltpu.make_async_remote_copy(
                        src_ref=ins[a] if mode == "gather" else ins[a].at[pidx], dst_ref=lands[a].at[me],
                        send_sem=sems[2 * g].at[j * npeer + slot], recv_sem=sems[2 * g + 1].at[j * npeer + slot],
                        device_id=peer, device_id_type=pl.DeviceIdType.MESH).start()
                a += 1
        for cp in local:
            cp.wait()
        token[...] = jnp.zeros_like(token)

    out_shape, out_specs = [], []
    for sz in sizes:
        out_shape += [pltpu.SemaphoreType.DMA((sz * npeer,)), pltpu.SemaphoreType.DMA((sz * npeer,))]
        out_specs += [_SEM, _SEM]
    out_shape += [pltpu.HBM(a.shape, a.dtype) for a, _ in flat]
    out_shape += [pltpu.HBM(_land_shape(a, m), a.dtype) for a, m in flat]
    out_specs += [_HBM] * (2 * n)
    out_shape.append(jax.ShapeDtypeStruct((8, LANES), F32))
    out_specs.append(pl.BlockSpec(memory_space=pltpu.VMEM))
    args = [pltpu.with_memory_space_constraint(a, pltpu.HBM) for a, _ in flat]
    args += [pltpu.with_memory_space_constraint(lax.empty(_land_shape(a, m), a.dtype), pltpu.HBM) for a, m in flat]
    res = pl.pallas_call(
        body, name=name, out_shape=tuple(out_shape),
        in_specs=[_HBM] * (2 * n) + [pl.BlockSpec(memory_space=pl.ANY)], out_specs=tuple(out_specs),
        input_output_aliases={i: 2 * ngr + i for i in range(2 * n)},
        scratch_shapes=[pltpu.SemaphoreType.DMA((n,))],
        compiler_params=pltpu.CompilerParams(has_side_effects=_EFFECT),
    )(*args, after)
    handles, a = [], 0
    for g, sz in enumerate(sizes):
        handles.append((res[2 * g], res[2 * g + 1], list(res[2 * ngr + a:2 * ngr + a + sz]),
                        list(res[2 * ngr + n + a:2 * ngr + n + a + sz]), list(groups[g][1])))
        a += sz
    return handles, res[-1]


def _xwait(handle, after, name):
    send_sems, recv_sems, srcs, lands, modes = handle
    m = len(srcs)
    npeer = N_DEV - 1

    def body(*refs):
        ins, lnd = refs[:m], refs[m:2 * m]
        ssem, rsem = refs[2 * m], refs[2 * m + 1]
        x, y, c = lax.axis_index("x"), lax.axis_index("y"), lax.axis_index("c")
        for j in range(m):
            for slot, peer, pidx in _peers(x, y, c):
                cp = pltpu.make_async_remote_copy(
                    src_ref=ins[j] if modes[j] == "gather" else ins[j].at[pidx], dst_ref=lnd[j].at[pidx],
                    send_sem=ssem.at[j * npeer + slot], recv_sem=rsem.at[j * npeer + slot],
                    device_id=peer, device_id_type=pl.DeviceIdType.MESH)
                cp.wait_send()
                cp.wait_recv()

    res = pl.pallas_call(
        body, name=name,
        out_shape=tuple([pltpu.HBM(a.shape, a.dtype) for a in srcs] + [pltpu.HBM(a.shape, a.dtype) for a in lands]),
        in_specs=[_HBM] * (2 * m) + [_SEM, _SEM, pl.BlockSpec(memory_space=pl.ANY)],
        out_specs=tuple([_HBM] * (2 * m)),
        input_output_aliases={i: i for i in range(2 * m)},
        compiler_params=pltpu.CompilerParams(has_side_effects=_EFFECT),
    )(*srcs, *lands, send_sems, recv_sems, after)
    return list(res[m:])


def _mod_fwd(c_all, mod_w, kv_mod_w):
    nl, d, mw = mod_w.shape
    kw = kv_mod_w.shape[1]

    def body(c_ref, mw_ref, kw_ref, o_ref, sc_ref):
        cc = c_ref[...]
        sc = (cc * _sigmoid(cc)).astype(BF16)
        sc_ref[...] = sc
        for l in range(nl):
            o_ref[:, l * mw:(l + 1) * mw] = jnp.dot(sc, mw_ref[l].astype(BF16), preferred_element_type=F32)
        o_ref[:, nl * mw:nl * mw + kw] = jnp.dot(sc, kw_ref[...].astype(BF16), preferred_element_type=F32)

    return pl.pallas_call(
        body, name="mod_fwd",
        out_shape=(jax.ShapeDtypeStruct((c_all.shape[0], nl * mw + kw), F32),
                   jax.ShapeDtypeStruct(c_all.shape, BF16)),
        compiler_params=_params(VMEM_BIG),
    )(c_all, mod_w, kv_mod_w)


def _vec_prep(modrow, modb, kvrow, kvb, ng, kvg):
    d = ng.shape[1]

    def body(mr_ref, mb_ref, kr_ref, kb_ref, ng_ref, kvg_ref, t_ref, m_ref):
        t_ref[...] = jnp.zeros_like(t_ref)
        m_ref[...] = jnp.zeros_like(m_ref)
        for l in range(2):
            mod = mr_ref[l] + mb_ref[l]
            m_ref[6 * l:6 * l + 6, :] = mod
            g = ng_ref[4 * l:4 * l + 4, :]
            t_ref[6 * l + R_W1:6 * l + R_W1 + 1, :] = g[0:1] * (1.0 + mod[1:2])
            t_ref[6 * l + R_SH1:6 * l + R_SH1 + 1, :] = mod[0:1]
            t_ref[6 * l + R_P1:6 * l + R_P1 + 1, :] = mod[2:3] * g[1:2]
            t_ref[6 * l + R_W2:6 * l + R_W2 + 1, :] = g[2:3] * (1.0 + mod[4:5])
            t_ref[6 * l + R_SH2:6 * l + R_SH2 + 1, :] = mod[3:4]
            t_ref[6 * l + R_P2:6 * l + R_P2 + 1, :] = mod[5:6] * g[3:4]
        kv = kr_ref[...] + kb_ref[...]
        m_ref[R_KV:R_KV + 2, :] = kv
        t_ref[R_KV:R_KV + 1, :] = kvg_ref[...] * (1.0 + kv[1:2])
        t_ref[R_KV + 1:R_KV + 2, :] = kv[0:1]

    return pl.pallas_call(
        body, name="vec_prep",
        out_shape=(jax.ShapeDtypeStruct((16, d), F32), jax.ShapeDtypeStruct((16, d), F32)),
    )(modrow, modb, kvrow, kvb, ng, kvg)


def _vec_bwd(sums_c, sums_f0, sums_q, sums_o, sums_f1, mt, ng, kvg):
    d = ng.shape[1]

    def body(sc_ref, sf0_ref, sq_ref, so_ref, sf1_ref, m_ref, ng_ref, kvg_ref, dm_ref, dng_ref, dkvg_ref, g_ref):
        g_ref[...] = jnp.zeros_like(g_ref)
        g_ref[0:3, :] = sc_ref[0:3, :]
        g_ref[3:6, :] = sf0_ref[3:6, :]
        g_ref[6:8, :] = sq_ref[0:2, :]
        g_ref[8:9, :] = so_ref[2:3, :]
        g_ref[9:12, :] = sf1_ref[3:6, :]
        g_ref[R_KV:R_KV + 2, :] = sq_ref[2:4, :]
        dm_ref[...] = jnp.zeros_like(dm_ref)
        dkvg_ref[...] = jnp.zeros_like(dkvg_ref)
        for l in range(2):
            g = ng_ref[4 * l:4 * l + 4, :]
            mod = m_ref[6 * l:6 * l + 6, :]
            s = g_ref[6 * l:6 * l + 6, :]
            dm_ref[6 * l + 0:6 * l + 1, :] = s[1:2]
            dm_ref[6 * l + 1:6 * l + 2, :] = s[0:1] * g[0:1]
            dm_ref[6 * l + 2:6 * l + 3, :] = s[2:3] * g[1:2]
            dm_ref[6 * l + 3:6 * l + 4, :] = s[4:5]
            dm_ref[6 * l + 4:6 * l + 5, :] = s[3:4] * g[2:3]
            dm_ref[6 * l + 5:6 * l + 6, :] = s[5:6] * g[3:4]
            dng_ref[4 * l + 0:4 * l + 1, :] = s[0:1] * (1.0 + mod[1:2])
            dng_ref[4 * l + 1:4 * l + 2, :] = s[2:3] * mod[2:3]
            dng_ref[4 * l + 2:4 * l + 3, :] = s[3:4] * (1.0 + mod[4:5])
            dng_ref[4 * l + 3:4 * l + 4, :] = s[5:6] * mod[5:6]
        dm_ref[R_KV:R_KV + 1, :] = g_ref[R_KV + 1:R_KV + 2, :]
        dm_ref[R_KV + 1:R_KV + 2, :] = g_ref[R_KV:R_KV + 1, :] * kvg_ref[...]
        dkvg_ref[0:1, :] = g_ref[R_KV:R_KV + 1, :] * (1.0 + m_ref[R_KV + 1:R_KV + 2, :])

    return pl.pallas_call(
        body, name="vec_bwd",
        out_shape=(jax.ShapeDtypeStruct((16, d), F32), jax.ShapeDtypeStruct((8, d), F32),
                   jax.ShapeDtypeStruct((8, d), F32)),
        scratch_shapes=[pltpu.VMEM((16, d), F32)],
    )(sums_c, sums_f0, sums_q, sums_o, sums_f1, mt, ng, kvg)


def _rel_index(nrel):
    width = KW + QB
    e = lax.broadcasted_iota(jnp.int32, (nrel, width), 1)
    r = lax.broadcasted_iota(jnp.int32, (nrel, width), 0)
    max_rel = (nrel - 1) // 2
    idx = jnp.clip(KW - e, -max_rel, max_rel) + max_rel
    return (idx == r).astype(F32)


def _band_valid():
    row = lax.broadcasted_iota(jnp.int32, (QB, KW), 0) // CHUNK
    col = lax.broadcasted_iota(jnp.int32, (QB, KW), 1) // CHUNK
    j = col - row
    return (j >= 0) & (j <= N_LEFT)


def _bias_fwd(rel_bias):
    nh, nrel = rel_bias.shape
    width = KW + QB

    def body(rb_ref, o_ref):
        onehot = _rel_index(nrel)
        gr = jnp.dot(rb_ref[...], onehot, preferred_element_type=F32, precision=lax.Precision.HIGHEST)
        valid = _band_valid()
        for h in range(nh):
            xrow = jnp.broadcast_to(gr[h:h + 1, :], (QB, width))
            rolled = pltpu.roll(xrow, 0, 1, stride=1, stride_axis=0)
            o_ref[h] = jnp.where(valid, rolled[:, QB:], NEG)

    return pl.pallas_call(
        body, name="bias_fwd",
        out_shape=jax.ShapeDtypeStruct((nh, QB, KW), F32),
        compiler_params=_params(VMEM_BIG),
    )(rel_bias)


def _bias_bwd(dbias, nrel):
    nh = dbias.shape[0]
    width = KW + QB

    def body(db_ref, o_ref, diag_ref):
        onehot = _rel_index(nrel)
        valid = _band_valid()
        rr = lax.broadcasted_iota(jnp.int32, (QB, QB), 0)
        cc = lax.broadcasted_iota(jnp.int32, (QB, QB), 1)
        flip = (rr + cc == QB - 1).astype(F32)
        for h in range(nh):
            rev = jnp.dot(flip, jnp.where(valid, db_ref[h], 0.0), preferred_element_type=F32,
                          precision=lax.Precision.HIGHEST)
            w = jnp.concatenate([jnp.zeros((QB, QB), F32), rev], axis=1)
            back = pltpu.roll(w, width - (QB - 1), 1, stride=1, stride_axis=0)
            diag_ref[h:h + 1, :] = _colsum(back)
        o_ref[...] = lax.dot_general(diag_ref[...], onehot, NT, preferred_element_type=F32,
                                     precision=lax.Precision.HIGHEST)

    return pl.pallas_call(
        body, name="bias_bwd",
        out_shape=jax.ShapeDtypeStruct((nh, nrel), F32),
        scratch_shapes=[pltpu.VMEM((nh, width), F32)],
        compiler_params=_params(VMEM_BIG),
    )(dbias)


def _conv_fwd(x, tab, ck, wci, wco, tm):
    s, d = x.shape
    nsh, _, cw = wci.shape

    def body(x_ref, t_ref, ck_ref, wci_ref, wco_ref, x1_ref, h_ref, bcx_ref, u_ref, y_ref, carry):
        @pl.when(pl.program_id(0) == 0)
        def _():
            carry[...] = jnp.zeros_like(carry)

        xv = x_ref[...]
        hb = ((xv * _rs(xv)) * t_ref[R_W1:R_W1 + 1, :] + t_ref[R_SH1:R_SH1 + 1, :]).astype(BF16)
        h_ref[...] = hb
        for j in range(nsh):
            bcx_ref[:, j * cw:(j + 1) * cw] = jnp.dot(hb, wci_ref[j], preferred_element_type=F32)
        bg, cg, xi = bcx_ref[:, 0:d], bcx_ref[:, d:2 * d], bcx_ref[:, 2 * d:3 * d]
        z = cg * xi
        row = lax.broadcasted_iota(jnp.int32, z.shape, 0)
        c1, c2 = carry[7:8, :], carry[6:7, :]
        z1 = jnp.where(row == 0, c1, pltpu.roll(z, 1, 0))
        z2 = jnp.where(row == 0, c2, jnp.where(row == 1, c1, pltpu.roll(z, 2, 0)))
        carry[...] = z[tm - 8:tm, :]
        conv = ck_ref[0:1, :] * z2 + ck_ref[1:2, :] * z1 + ck_ref[2:3, :] * z
        ub = (bg * conv).astype(BF16)
        u_ref[...] = ub
        yv = jnp.dot(ub, wco_ref[...], preferred_element_type=F32)
        y_ref[...] = yv
        x1_ref[...] = xv + (yv * _rs(yv)) * t_ref[R_P1:R_P1 + 1, :]

    return pl.pallas_call(
        body, name="conv_fwd", grid=(s // tm,),
        in_specs=[_rows(tm, d), _const(tab.shape), _const(ck.shape), _const(wci.shape), _const(wco.shape)],
        out_specs=(_rows(tm, d), _rows(tm, d), _rows(tm, 3 * d), _rows(tm, d), _rows(tm, d)),
        out_shape=(jax.ShapeDtypeStruct((s, d), F32), jax.ShapeDtypeStruct((s, d), BF16),
                   jax.ShapeDtypeStruct((s, 3 * d), F32), jax.ShapeDtypeStruct((s, d), BF16),
                   jax.ShapeDtypeStruct((s, d), F32)),
        scratch_shapes=[pltpu.VMEM((8, d), F32)],
        compiler_params=_params(VMEM_BIG),
    )(x, tab, ck, wci, wco)


def _ffn_fwd(x, tab, base, wfi, wfo, tgt, tm, name):
    s, d = x.shape
    nsh, _, fw = wfi.shape
    nh = nsh // 2
    with_loss = tgt is not None

    def body(*refs):
        if with_loss:
            x_ref, t_ref, wfi_ref, wfo_ref, tgt_ref, xo_ref, h_ref, gu_ref, a_ref, y_ref, loss_ref = refs
        else:
            x_ref, t_ref, wfi_ref, wfo_ref, xo_ref, h_ref, gu_ref, a_ref, y_ref = refs
        xv = x_ref[...]
        hb = ((xv * _rs(xv)) * t_ref[base + R_W2:base + R_W2 + 1, :]
              + t_ref[base + R_SH2:base + R_SH2 + 1, :]).astype(BF16)
        h_ref[...] = hb
        acc = jnp.zeros((tm, d), F32)
        for j in range(nh):
            g = jnp.dot(hb, wfi_ref[j], preferred_element_type=F32)
            u = jnp.dot(hb, wfi_ref[j + nh], preferred_element_type=F32)
            gu_ref[j] = g
            gu_ref[j + nh] = u
            ab = ((g * _sigmoid(g)) * u).astype(BF16)
            a_ref[j] = ab
            acc = acc + jnp.dot(ab, wfo_ref[j], preferred_element_type=F32)
        y_ref[...] = acc
        xo = xv + (acc * _rs(acc)) * t_ref[base + R_P2:base + R_P2 + 1, :]
        if with_loss:
            @pl.when(pl.program_id(0) == 0)
            def _():
                loss_ref[...] = jnp.zeros_like(loss_ref)

            err = xo - tgt_ref[...]
            xo_ref[...] = err * (1.0 / d)
            loss_ref[...] += jnp.sum(err * err)
        else:
            xo_ref[...] = xo

    in_specs = [_rows(tm, d), _const(tab.shape), _const(wfi.shape), _const(wfo.shape)]
    args = [x, tab, wfi, wfo]
    out_specs = [_rows(tm, d), _rows(tm, d), pl.BlockSpec((nsh, tm, fw), lambda i: (0, i, 0)),
                 pl.BlockSpec((nh, tm, fw), lambda i: (0, i, 0)), _rows(tm, d)]
    out_shape = [jax.ShapeDtypeStruct((s, d), F32), jax.ShapeDtypeStruct((s, d), BF16),
                 jax.ShapeDtypeStruct((nsh, s, fw), F32), jax.ShapeDtypeStruct((nh, s, fw), BF16),
                 jax.ShapeDtypeStruct((s, d), F32)]
    if with_loss:
        in_specs.append(_rows(tm, d))
        args.append(tgt)
        out_specs.append(pl.BlockSpec((8, LANES), lambda i: (0, 0)))
        out_shape.append(jax.ShapeDtypeStruct((8, LANES), F32))
    return pl.pallas_call(
        body, name=name, grid=(s // tm,), in_specs=in_specs, out_specs=tuple(out_specs),
        out_shape=tuple(out_shape), compiler_params=_params(VMEM_BIG),
    )(*args)


def _qkv_fwd(x, tab, wq, wkv, tm):
    s, d = x.shape
    nsh, _, kw = wkv.shape
    nh = nsh // 2
    base = 6

    def body(x_ref, t_ref, wq_ref, wkv_ref, hkv_ref, h1_ref, q_ref, k_ref, v_ref):
        xv = x_ref[...]
        n = xv * _rs(xv)
        hkv = (n * t_ref[R_KV:R_KV + 1, :] + t_ref[R_KV + 1:R_KV + 2, :]).astype(BF16)
        h1 = (n * t_ref[base + R_W1:base + R_W1 + 1, :] + t_ref[base + R_SH1:base + R_SH1 + 1, :]).astype(BF16)
        hkv_ref[...] = hkv
        h1_ref[...] = h1
        q_ref[...] = (jnp.dot(h1, wq_ref[...], preferred_element_type=F32) * (HEAD_DIM ** -0.5)).astype(BF16)
        for j in range(nh):
            k_ref[:, j * kw:(j + 1) * kw] = jnp.dot(hkv, wkv_ref[j], preferred_element_type=F32).astype(BF16)
            v_ref[:, j * kw:(j + 1) * kw] = jnp.dot(hkv, wkv_ref[j + nh], preferred_element_type=F32).astype(BF16)

    act = jax.ShapeDtypeStruct((s, d), BF16)
    return pl.pallas_call(
        body, name="qkv_fwd", grid=(s // tm,),
        in_specs=[_rows(tm, d), _const(tab.shape), _const(wq.shape), _const(wkv.shape)],
        out_specs=tuple([_rows(tm, d)] * 5), out_shape=(act,) * 5,
        compiler_params=_params(VMEM_BIG),
    )(x, tab, wq, wkv)


def _window_specs():
    return [pl.BlockSpec((QB, LANES), (lambda p, b, w=w: (jnp.maximum(b - 2 + w, 0), p))) for w in range(3)]


def _key_valid(b):
    col = lax.broadcasted_iota(jnp.int32, (QB, KW), 1) // CHUNK
    return (b * (QB // CHUNK) - N_LEFT + col) >= 0


def _head_masks():
    lane = lax.broadcasted_iota(jnp.int32, (1, LANES), 1)
    return [(lane // HEAD_DIM == hh) for hh in range(LANES // HEAD_DIM)]


def _attn_fwd(q, k, v, bias):
    s, d = q.shape
    npair, nb = d // LANES, s // QB
    hpp = LANES // HEAD_DIM

    def body(q_ref, k0, k1, k2, v0, v1, v2, bias_ref, o_ref, lse_ref):
        b = pl.program_id(1)
        qv = q_ref[...]
        kwin = jnp.concatenate([k0[...], k1[...], k2[...]], axis=0)
        vwin = jnp.concatenate([v0[...], v1[...], v2[...]], axis=0)
        valid = _key_valid(b)
        masks = _head_masks()
        o = jnp.zeros((QB, LANES), F32)
        lse = jnp.zeros((QB, LANES), F32)
        for hh in range(hpp):
            qm = jnp.where(masks[hh], qv, jnp.zeros_like(qv))
            vm = jnp.where(masks[hh], vwin, jnp.zeros_like(vwin))
            sc = lax.dot_general(qm, kwin, NT, preferred_element_type=F32) + bias_ref[hh]
            sc = jnp.where(valid, sc, NEG)
            m = jnp.max(sc, axis=-1, keepdims=True)
            p = jnp.exp(sc - m)
            l = jnp.sum(p, axis=-1, keepdims=True)
            o = o + jnp.dot(p.astype(BF16), vm, preferred_element_type=F32) * (1.0 / l)
            lse = jnp.where(masks[hh], m + jnp.log(l), lse)
        o_ref[...] = o.astype(BF16)
        lse_ref[...] = lse

    blk = pl.BlockSpec((QB, LANES), lambda p, b: (b, p))
    return pl.pallas_call(
        body, name="attn_fwd", grid=(npair, nb),
        in_specs=[blk] + _window_specs() + _window_specs()
                 + [pl.BlockSpec((hpp, QB, KW), lambda p, b: (p, 0, 0))],
        out_specs=(blk, blk),
        out_shape=(jax.ShapeDtypeStruct((s, d), BF16), jax.ShapeDtypeStruct((s, d), F32)),
        compiler_params=_params(VMEM_BIG),
    )(q, k, k, k, v, v, v, bias)


def _attn_out_fwd(o, x, tab, wo, tm):
    s, d = x.shape
    base = 6

    def body(o_ref, x_ref, t_ref, wo_ref, x3_ref, y_ref):
        yv = jnp.dot(o_ref[...], wo_ref[...], preferred_element_type=F32)
        y_ref[...] = yv
        x3_ref[...] = x_ref[...] + (yv * _rs(yv)) * t_ref[base + R_P1:base + R_P1 + 1, :]

    return pl.pallas_call(
        body, name="attn_out_fwd", grid=(s // tm,),
        in_specs=[_rows(tm, d), _rows(tm, d), _const(tab.shape), _const(wo.shape)],
        out_specs=(_rows(tm, d), _rows(tm, d)),
        out_shape=(jax.ShapeDtypeStruct((s, d), F32), jax.ShapeDtypeStruct((s, d), F32)),
        compiler_params=_params(VMEM_BIG),
    )(o, x, tab, wo)


def _ffn_bwd(dxo, x, y, gu, tab, base, wfi, wfo, tm, name):
    s, d = x.shape
    nsh, _, fw = wfi.shape
    nh = nsh // 2

    def body(dxo_ref, x_ref, y_ref, gu_ref, t_ref, wfi_ref, wfo_ref, dx_ref, dyb_ref, dgu_ref, sums_ref):
        @pl.when(pl.program_id(0) == 0)
        def _():
            sums_ref[...] = jnp.zeros_like(sums_ref)

        dxo_v = dxo_ref[...]
        yv = y_ref[...]
        ry = _rs(yv)
        ny = yv * ry
        sums_ref[R_P2:R_P2 + 1, :] += _colsum(dxo_v * ny)
        dyb = _norm_bwd(dxo_v * t_ref[base + R_P2:base + R_P2 + 1, :], ny, ry).astype(BF16)
        dyb_ref[...] = dyb
        dh = jnp.zeros((tm, d), F32)
        for j in range(nh):
            da = lax.dot_general(dyb, wfo_ref[j], NT, preferred_element_type=F32)
            g, u = gu_ref[j], gu_ref[j + nh]
            sg = _sigmoid(g)
            dg = (da * u * sg * (1.0 + g * (1.0 - sg))).astype(BF16)
            du = (da * g * sg).astype(BF16)
            dgu_ref[j] = dg
            dgu_ref[j + nh] = du
            dh = dh + lax.dot_general(dg, wfi_ref[j], NT, preferred_element_type=F32)
            dh = dh + lax.dot_general(du, wfi_ref[j + nh], NT, preferred_element_type=F32)
        xv = x_ref[...]
        r = _rs(xv)
        n = xv * r
        sums_ref[R_SH2:R_SH2 + 1, :] += _colsum(dh)
        sums_ref[R_W2:R_W2 + 1, :] += _colsum(dh * n)
        dx_ref[...] = dxo_v + _norm_bwd(dh * t_ref[base + R_W2:base + R_W2 + 1, :], n, r)

    return pl.pallas_call(
        body, name=name, grid=(s // tm,),
        in_specs=[_rows(tm, d), _rows(tm, d), _rows(tm, d), pl.BlockSpec((nsh, tm, fw), lambda i: (0, i, 0)),
                  _const(tab.shape), _const(wfi.shape), _const(wfo.shape)],
        out_specs=(_rows(tm, d), _rows(tm, d), pl.BlockSpec((nsh, tm, fw), lambda i: (0, i, 0)),
                   pl.BlockSpec((8, d), lambda i: (0, 0))),
        out_shape=(jax.ShapeDtypeStruct((s, d), F32), jax.ShapeDtypeStruct((s, d), BF16),
                   jax.ShapeDtypeStruct((nsh, s, fw), BF16), jax.ShapeDtypeStruct((8, d), F32)),
        compiler_params=_params(VMEM_BIG),
    )(dxo, x, y, gu, tab, wfi, wfo)


def _attn_out_bwd(dx, y, tab, wo, tm):
    s, d = y.shape
    base = 6

    def body(dx_ref, y_ref, t_ref, wo_ref, dyb_ref, do_ref, sums_ref):
        @pl.when(pl.program_id(0) == 0)
        def _():
            sums_ref[...] = jnp.zeros_like(sums_ref)

        dxv = dx_ref[...]
        yv = y_ref[...]
        ry = _rs(yv)
        ny = yv * ry
        sums_ref[R_P1:R_P1 + 1, :] += _colsum(dxv * ny)
        dyb = _norm_bwd(dxv * t_ref[base + R_P1:base + R_P1 + 1, :], ny, ry).astype(BF16)
        dyb_ref[...] = dyb
        do_ref[...] = lax.dot_general(dyb, wo_ref[...], NT, preferred_element_type=F32).astype(BF16)

    return pl.pallas_call(
        body, name="attn_out_bwd", grid=(s // tm,),
        in_specs=[_rows(tm, d), _rows(tm, d), _const(tab.shape), _const(wo.shape)],
        out_specs=(_rows(tm, d), _rows(tm, d), pl.BlockSpec((8, d), lambda i: (0, 0))),
        out_shape=(jax.ShapeDtypeStruct((s, d), BF16), jax.ShapeDtypeStruct((s, d), BF16),
                   jax.ShapeDtypeStruct((8, d), F32)),
        compiler_params=_params(VMEM_BIG),
    )(dx, y, tab, wo)


def _attn_bwd(q, k, v, do, lse, bias):
    s, d = q.shape
    npair, nb = d // LANES, s // QB
    hpp = LANES // HEAD_DIM

    def body(q_ref, k0, k1, k2, v0, v1, v2, do_ref, lse_ref, bias_ref, dq_ref, dk_ref, dv_ref, db_ref):
        b = pl.program_id(1)

        @pl.when(b == 0)
        def _():
            dk_ref[...] = jnp.zeros_like(dk_ref)
            dv_ref[...] = jnp.zeros_like(dv_ref)
            db_ref[...] = jnp.zeros_like(db_ref)

        qv = q_ref[...]
        dov = do_ref[...]
        lsev = lse_ref[...]
        kwin = jnp.concatenate([k0[...], k1[...], k2[...]], axis=0)
        vwin = jnp.concatenate([v0[...], v1[...], v2[...]], axis=0)
        valid = _key_valid(b)
        masks = _head_masks()
        dq = jnp.zeros((QB, LANES), F32)
        dkw = jnp.zeros((KW, LANES), F32)
        dvw = jnp.zeros((KW, LANES), F32)
        for hh in range(hpp):
            qm = jnp.where(masks[hh], qv, jnp.zeros_like(qv))
            dom = jnp.where(masks[hh], dov, jnp.zeros_like(dov))
            km = jnp.where(masks[hh], kwin, jnp.zeros_like(kwin))
            lse_h = jnp.max(jnp.where(masks[hh], lsev, NEG), axis=-1, keepdims=True)
            sc = lax.dot_general(qm, kwin, NT, preferred_element_type=F32) + bias_ref[hh]
            sc = jnp.where(valid, sc, NEG)
            p = jnp.exp(sc - lse_h)
            dp = lax.dot_general(dom, vwin, NT, preferred_element_type=F32)
            ds = p * (dp - jnp.sum(dp * p, axis=-1, keepdims=True))
            db_ref[hh] += ds
            dsb = ds.astype(BF16)
            dq = dq + jnp.dot(dsb, km, preferred_element_type=F32)
            dkw = dkw + lax.dot_general(dsb, qm, TN, preferred_element_type=F32)
            dvw = dvw + lax.dot_general(p.astype(BF16), dom, TN, preferred_element_type=F32)
        dq_ref[...] = (dq * (HEAD_DIM ** -0.5)).astype(BF16)
        for w in range(3):
            start = pl.multiple_of(jnp.maximum(b - 2 + w, 0) * QB, QB)
            dk_ref[pl.ds(start, QB), :] += dkw[w * QB:(w + 1) * QB, :]
            dv_ref[pl.ds(start, QB), :] += dvw[w * QB:(w + 1) * QB, :]

    blk = pl.BlockSpec((QB, LANES), lambda p, b: (b, p))
    col = pl.BlockSpec((s, LANES), lambda p, b: (0, p))
    pair = pl.BlockSpec((hpp, QB, KW), lambda p, b: (p, 0, 0))
    return pl.pallas_call(
        body, name="attn_bwd", grid=(npair, nb),
        in_specs=[blk] + _window_specs() + _window_specs() + [blk, blk, pair],
        out_specs=(blk, col, col, pair),
        out_shape=(jax.ShapeDtypeStruct((s, d), BF16), jax.ShapeDtypeStruct((s, d), F32),
                   jax.ShapeDtypeStruct((s, d), F32), jax.ShapeDtypeStruct(bias.shape, F32)),
        compiler_params=_params(VMEM_BIG),
    )(q, k, k, k, v, v, v, do, lse, bias)


def _qkv_bwd(dres, dq, dk, dv, x, tab, wq, wkv, tm):
    s, d = x.shape
    nsh, _, kw = wkv.shape
    nh = nsh // 2
    base = 6

    def body(dres_ref, dq_ref, dk_ref, dv_ref, x_ref, t_ref, wq_ref, wkv_ref, dx_ref, dkv_ref, sums_ref):
        @pl.when(pl.program_id(0) == 0)
        def _():
            sums_ref[...] = jnp.zeros_like(sums_ref)

        dh1 = lax.dot_general(dq_ref[...], wq_ref[...], NT, preferred_element_type=F32)
        dkv_ref[:, 0:d] = dk_ref[...].astype(BF16)
        dkv_ref[:, d:2 * d] = dv_ref[...].astype(BF16)
        dhkv = jnp.zeros((tm, d), F32)
        for j in range(nsh):
            dhkv = dhkv + lax.dot_general(dkv_ref[:, j * kw:(j + 1) * kw], wkv_ref[j], NT,
                                          preferred_element_type=F32)
        xv = x_ref[...]
        r = _rs(xv)
        n = xv * r
        sums_ref[0:1, :] += _colsum(dh1 * n)
        sums_ref[1:2, :] += _colsum(dh1)
        sums_ref[2:3, :] += _colsum(dhkv * n)
        sums_ref[3:4, :] += _colsum(dhkv)
        dn = dh1 * t_ref[base + R_W1:base + R_W1 + 1, :] + dhkv * t_ref[R_KV:R_KV + 1, :]
        dx_ref[...] = dres_ref[...] + _norm_bwd(dn, n, r)

    return pl.pallas_call(
        body, name="qkv_bwd", grid=(s // tm,),
        in_specs=[_rows(tm, d)] * 5 + [_const(tab.shape), _const(wq.shape), _const(wkv.shape)],
        out_specs=(_rows(tm, d), _rows(tm, 2 * d), pl.BlockSpec((8, d), lambda i: (0, 0))),
        out_shape=(jax.ShapeDtypeStruct((s, d), F32), jax.ShapeDtypeStruct((s, 2 * d), BF16),
                   jax.ShapeDtypeStruct((8, d), F32)),
        compiler_params=_params(VMEM_BIG),
    )(dres, dq, dk, dv, x, tab, wq, wkv)


def _conv_bwd(dx1, x, y, bcx, tab, ck, wci, wco, tm):
    s, d = x.shape
    nsh, _, cw = wci.shape
    nt = s // tm

    def rev(i):
        return (nt - 1 - i, 0)

    def halo(i):
        return (jnp.maximum((nt - 1 - i) * (tm // 8) - 1, 0), 0)

    def body(dx_ref, x_ref, y_ref, bcx_ref, halo_ref, t_ref, ck_ref, wci_ref, wco_ref,
             dx0_ref, dyb_ref, dbcx_ref, sums_ref, dck_ref, carry):
        i = pl.program_id(0)

        @pl.when(i == 0)
        def _():
            sums_ref[...] = jnp.zeros_like(sums_ref)
            dck_ref[...] = jnp.zeros_like(dck_ref)
            carry[...] = jnp.zeros_like(carry)

        dxv = dx_ref[...]
        yv = y_ref[...]
        ry = _rs(yv)
        ny = yv * ry
        sums_ref[R_P1:R_P1 + 1, :] += _colsum(dxv * ny)
        dyb = _norm_bwd(dxv * t_ref[R_P1:R_P1 + 1, :], ny, ry).astype(BF16)
        dyb_ref[...] = dyb
        du = lax.dot_general(dyb, wco_ref[...], NT, preferred_element_type=F32)
        bg, cg, xi = bcx_ref[:, 0:d], bcx_ref[:, d:2 * d], bcx_ref[:, 2 * d:3 * d]
        z = cg * xi
        zp = halo_ref[:, d:2 * d] * halo_ref[:, 2 * d:3 * d]
        zp = jnp.where(i == nt - 1, jnp.zeros_like(zp), zp)
        row = lax.broadcasted_iota(jnp.int32, z.shape, 0)
        c1, c2 = zp[7:8, :], zp[6:7, :]
        z1 = jnp.where(row == 0, c1, pltpu.roll(z, 1, 0))
        z2 = jnp.where(row == 0, c2, jnp.where(row == 1, c1, pltpu.roll(z, 2, 0)))
        k0, k1, k2 = ck_ref[0:1, :], ck_ref[1:2, :], ck_ref[2:3, :]
        conv = k0 * z2 + k1 * z1 + k2 * z
        dconv = du * bg
        dck_ref[0:1, :] += _colsum(dconv * z2)
        dck_ref[1:2, :] += _colsum(dconv * z1)
        dck_ref[2:3, :] += _colsum(dconv * z)
        n1, n2 = carry[0:1, :], carry[1:2, :]
        d1 = jnp.where(row == tm - 1, n1, pltpu.roll(dconv, tm - 1, 0))
        d2 = jnp.where(row == tm - 1, n2, jnp.where(row == tm - 2, n1, pltpu.roll(dconv, tm - 2, 0)))
        carry[...] = dconv[0:8, :]
        dz = k2 * dconv + k1 * d1 + k0 * d2
        dbcx_ref[:, 0:d] = (du * conv).astype(BF16)
        dbcx_ref[:, d:2 * d] = (dz * xi).astype(BF16)
        dbcx_ref[:, 2 * d:3 * d] = (dz * cg).astype(BF16)
        dh = jnp.zeros((tm, d), F32)
        for j in range(nsh):
            dh = dh + lax.dot_general(dbcx_ref[:, j * cw:(j + 1) * cw], wci_ref[j], NT,
                                      preferred_element_type=F32)
        xv = x_ref[...]
        r = _rs(xv)
        n = xv * r
        sums_ref[R_W1:R_W1 + 1, :] += _colsum(dh * n)
        sums_ref[R_SH1:R_SH1 + 1, :] += _colsum(dh)
        dx0_ref[...] = dxv + _norm_bwd(dh * t_ref[R_W1:R_W1 + 1, :], n, r)

    rrow = lambda cols: pl.BlockSpec((tm, cols), rev)
    acc = pl.BlockSpec((8, d), lambda i: (0, 0))
    return pl.pallas_call(
        body, name="conv_bwd", grid=(nt,),
        in_specs=[rrow(d), rrow(d), rrow(d), rrow(3 * d), pl.BlockSpec((8, 3 * d), halo),
                  _const(tab.shape), _const(ck.shape), _const(wci.shape), _const(wco.shape)],
        out_specs=(rrow(d), rrow(d), rrow(3 * d), acc, acc),
        out_shape=(jax.ShapeDtypeStruct((s, d), F32), jax.ShapeDtypeStruct((s, d), BF16),
                   jax.ShapeDtypeStruct((s, 3 * d), BF16), jax.ShapeDtypeStruct((8, d), F32),
                   jax.ShapeDtypeStruct((8, d), F32)),
        scratch_shapes=[pltpu.VMEM((8, d), F32)],
        compiler_params=_params(VMEM_BIG),
    )(dx1, x, y, bcx, bcx, tab, ck, wci, wco)


def _wgrad(a, b, nblk, a_spec, b_spec, m, n, tk, name):
    s = a.shape[-2]
    nk = s // tk

    def body(a_ref, b_ref, o_ref, acc):
        kk = pl.program_id(1)

        @pl.when(kk == 0)
        def _():
            acc[...] = jnp.zeros_like(acc)

        acc[...] += lax.dot_general(a_ref[...], b_ref[...], TN, preferred_element_type=F32)

        @pl.when(kk == nk - 1)
        def _():
            o_ref[...] = acc[...].astype(BF16)

    return pl.pallas_call(
        body, name=name, grid=(nblk, nk),
        in_specs=[a_spec, b_spec],
        out_specs=pl.BlockSpec((None, m, n), lambda j, kk: (j, 0, 0)),
        out_shape=jax.ShapeDtypeStruct((nblk, m, n), BF16),
        scratch_shapes=[pltpu.VMEM((m, n), F32)],
        compiler_params=_params(VMEM_BIG),
    )(a, b)


def _wgrad_cols(a, b, nblk, tk, name):
    m, n = a.shape[1], b.shape[1] // nblk
    return _wgrad(a, b, nblk, pl.BlockSpec((tk, m), lambda j, kk: (kk, 0)),
                  pl.BlockSpec((tk, n), lambda j, kk: (kk, j)), m, n, tk, name)


def _wgrad_bstack(a, b, tk, name):
    nblk, _, n = b.shape
    m = a.shape[1]
    return _wgrad(a, b, nblk, pl.BlockSpec((tk, m), lambda j, kk: (kk, 0)),
                  pl.BlockSpec((None, tk, n), lambda j, kk: (j, kk, 0)), m, n, tk, name)


def _wgrad_astack(a, b, tk, name):
    nblk, _, m = a.shape
    n = b.shape[1]
    return _wgrad(a, b, nblk, pl.BlockSpec((None, tk, m), lambda j, kk: (j, kk, 0)),
                  pl.BlockSpec((tk, n), lambda j, kk: (kk, 0)), m, n, tk, name)


def _adamw_math(w, g, m, v):
    m = ADAM_B1 * m + (1.0 - ADAM_B1) * g
    v = ADAM_B2 * v + (1.0 - ADAM_B2) * (g * g)
    m_hat = m / (1.0 - ADAM_B1 ** ADAM_STEP)
    v_hat = v / (1.0 - ADAM_B2 ** ADAM_STEP)
    delta = -ADAM_LR * (m_hat / (jnp.sqrt(v_hat) + ADAM_EPS) + ADAM_WD * w)
    return delta, m, v


def _adamw_reduce(parts, w, m, v, tr, name):
    nl, r, c = w.shape
    tr = _row_tile(r, tr)

    def body(*refs):
        p_refs = refs[:nl]
        w_ref, m_ref, v_ref, g_ref, d_ref, mo_ref, vo_ref = refs[nl:]
        layer = pl.program_id(0)

        def partial(i):
            val = p_refs[0][i].astype(F32)
            for q in range(1, nl):
                val = jnp.where(layer == q, p_refs[q][i].astype(F32), val)
            return val

        g = partial(0)
        for i in range(1, N_DEV):
            g = g + partial(i)
        g_ref[...] = g
        d_ref[...], mo_ref[...], vo_ref[...] = _adamw_math(w_ref[...], g, m_ref[...], v_ref[...])

    blk = pl.BlockSpec((None, tr, c), lambda l, i: (l, i, 0))
    out = jax.ShapeDtypeStruct((nl, r, c), F32)
    p_specs = [pl.BlockSpec((N_DEV, tr, c), (lambda l, i, q=q: (0, jnp.where(l == q, i, 0), 0))) for q in range(nl)]
    return pl.pallas_call(
        body, name=name, grid=(nl, r // tr),
        in_specs=p_specs + [blk, blk, blk],
        out_specs=(blk,) * 4, out_shape=(out,) * 4,
        compiler_params=_params(VMEM_BIG),
    )(*parts, w, m, v)


def _adamw_outer(sct, dm, w, m, v, tr, name):
    nl, d, c = w.shape

    def body(s_ref, dm_ref, w_ref, m_ref, v_ref, g_ref, d_ref, mo_ref, vo_ref):
        g = jnp.dot(s_ref[...], dm_ref[...], preferred_element_type=F32)
        g_ref[...] = g
        d_ref[...], mo_ref[...], vo_ref[...] = _adamw_math(w_ref[...], g, m_ref[...], v_ref[...])

    blk = pl.BlockSpec((None, tr, c), lambda l, i: (l, i, 0))
    out = jax.ShapeDtypeStruct((nl, d, c), F32)
    return pl.pallas_call(
        body, name=name, grid=(nl, d // tr),
        in_specs=[pl.BlockSpec((tr, N_DEV), lambda l, i: (i, 0)),
                  pl.BlockSpec((None, N_DEV, c), lambda l, i: (l, 0, 0)), blk, blk, blk],
        out_specs=(blk,) * 4, out_shape=(out,) * 4,
        compiler_params=_params(VMEM_BIG),
    )(sct, dm, w, m, v)


def _pad_rows(a, rows):
    return jnp.concatenate([a, jnp.zeros((rows - a.shape[0],) + a.shape[1:], a.dtype)], axis=0)


def kernel(x, c, mod_w, mod_b, norm_g, ffn_w_in, ffn_w_out, conv_w_in, conv_k, conv_w_out, kv_mod_w, kv_mod_b, kv_norm_g, w_kv, attn_w_q, attn_w_o, rel_bias, loss_target, m_mod_w, m_mod_b, m_norm_g, m_ffn_w_in, m_ffn_w_out, m_conv_w_in, m_conv_k, m_conv_w_out, m_kv_mod_w, m_kv_mod_b, m_kv_norm_g, m_w_kv, m_attn_w_q, m_attn_w_o, m_rel_bias, v_mod_w, v_mod_b, v_norm_g, v_ffn_w_in, v_ffn_w_out, v_conv_w_in, v_conv_k, v_conv_w_out, v_kv_mod_w, v_kv_mod_b, v_kv_norm_g, v_w_kv, v_attn_w_q, v_attn_w_o, v_rel_bias):
    s, d = x.shape[1], x.shape[2]
    dq = d // LANES
    dsh = d // N_DEV
    nl = mod_w.shape[0]
    mw = mod_w.shape[2]
    kmw = kv_mod_w.shape[1]
    fw = ffn_w_in.shape[2]
    nh, nrel = rel_bias.shape[1], rel_bias.shape[2]
    tm = min(256, s)
    tk = min(512, s)
    me = 4 * lax.axis_index("x") + 2 * lax.axis_index("y") + lax.axis_index("c")

    x0 = x[0]
    tgt = loss_target[0]

    small1 = jnp.concatenate([c.reshape(dq, LANES), norm_g.reshape(dq, LANES),
                              _pad_rows(conv_k[0], 8).reshape(dq, LANES)], axis=0)
    (sm,) = _exchange([small1], ["gather"], "gather_small")
    c_all = sm[:, 0:dq].reshape(N_DEV, d)
    ng_full = jnp.transpose(sm[:, dq:2 * dq].reshape(N_DEV, 8, dsh), (1, 0, 2)).reshape(8, d)
    ck_full = jnp.transpose(sm[:, 2 * dq:3 * dq].reshape(N_DEV, 8, dsh), (1, 0, 2)).reshape(8, d)

    modcols, silu_c = _mod_fwd(c_all, mod_w, kv_mod_w)
    (modall,) = _exchange([modcols], ["gather"], "gather_mod")

    cast = lambda *ws: [a.astype(BF16) for a in ws]
    gath = lambda ws: (ws, ["gather"] * len(ws))
    (h_conv, h_ffn0, h_attn, h_ffn1), token = _xstart(
        [gath(cast(conv_w_in[0], conv_w_out[0])), gath(cast(ffn_w_in[0], ffn_w_out[0])),
         gath(cast(w_kv, attn_w_q[0], attn_w_o[0])), gath(cast(ffn_w_in[1], ffn_w_out[1]))],
        modall, "gather_start")
    modall = modall + token[0, 0]
    mine = lax.dynamic_index_in_dim(modall, me, axis=1, keepdims=False)
    modrow = jnp.stack([mine[:, l * mw:(l + 1) * mw].reshape(6, d) for l in range(nl)])
    kvrow = mine[:, nl * mw:nl * mw + kmw].reshape(2, d)
    tab, modval = _vec_prep(modrow, mod_b.reshape(nl, 6, d), kvrow, kv_mod_b.reshape(2, d), ng_full,
                            kv_norm_g.reshape(1, d))
    bias = _bias_fwd(rel_bias[0])

    wci, wco = _xwait(h_conv, bias, "gather_wait_conv")
    wco = wco.reshape(d, d)
    x1, h1a, bcx, ua, ya = _conv_fwd(x0, tab, ck_full, wci, wco, tm)
    wfi0, wfo0 = _xwait(h_ffn0, x1, "gather_wait_ffn0")
    wfo0 = wfo0.reshape(N_DEV // 2, -1, d)
    x2, h2a, gua, aa, y2a = _ffn_fwd(x1, tab, 0, wfi0, wfo0, None, tm, "ffn_fwd0")
    wkv, wq, wo = _xwait(h_attn, x2, "gather_wait_attn")
    wq, wo = wq.reshape(d, d), wo.reshape(d, d)
    hkv, h1b, q, k, v = _qkv_fwd(x2, tab, wq, wkv, tm)
    o, lse = _attn_fwd(q, k, v, bias)
    x3, yb = _attn_out_fwd(o, x2, tab, wo, tm)
    wfi1, wfo1 = _xwait(h_ffn1, x3, "gather_wait_ffn1")
    wfo1 = wfo1.reshape(N_DEV // 2, -1, d)
    dx4, h2b, gub, ab, y2b, loss_acc = _ffn_fwd(x3, tab, 6, wfi1, wfo1, tgt, tm, "ffn_fwd1")
    loss = lax.psum(loss_acc[0, 0] * (0.5 / d), ("x", "y", "c"))

    scat = lambda ws: [(ws, ["scatter"] * len(ws))]
    dx3, dy2b, dgub, sums_f1 = _ffn_bwd(dx4, x3, y2b, gub, tab, 6, wfi1, wfo1, tm, "ffn_bwd1")
    g_wfi1 = _wgrad_bstack(h2b, dgub, tk, "wgrad_ffn_in1")
    g_wfo1 = _wgrad_astack(ab, dy2b, tk, "wgrad_ffn_out1").reshape(N_DEV, -1, d)
    (h_g1,), token = _xstart(scat([g_wfi1, g_wfo1]), dx3, "grads_start_ffn1")
    tab = tab + token[0, 0]
    dyb, do, sums_o = _attn_out_bwd(dx3, yb, tab, wo, tm)
    g_wo = _wgrad_cols(o, dyb, 1, tk, "wgrad_o").reshape(N_DEV, dsh, d)
    dqb, dk, dv, dbias = _attn_bwd(q, k, v, do, lse, bias)
    g_wq = _wgrad_cols(h1b, dqb, 1, tk, "wgrad_q").reshape(N_DEV, dsh, d)
    dx2, dkvb, sums_q = _qkv_bwd(dx3, dqb, dk, dv, x2, tab, wq, wkv, tm)
    g_wkv = _wgrad_cols(hkv, dkvb, N_DEV, tk, "wgrad_kv")
    (h_g2,), token = _xstart(scat([g_wkv, g_wq, g_wo]), dx2, "grads_start_attn")
    tab = tab + token[0, 0]
    dx1, dy2a, dgua, sums_f0 = _ffn_bwd(dx2, x1, y2a, gua, tab, 0, wfi0, wfo0, tm, "ffn_bwd0")
    g_wfi0 = _wgrad_bstack(h2a, dgua, tk, "wgrad_ffn_in0")
    g_wfo0 = _wgrad_astack(aa, dy2a, tk, "wgrad_ffn_out0").reshape(N_DEV, -1, d)
    (h_g3,), token = _xstart(scat([g_wfi0, g_wfo0]), dx1, "grads_start_ffn0")
    tab = tab + token[0, 0]
    dx0, dya, dbcx, sums_c, dck = _conv_bwd(dx1, x0, ya, bcx, tab, ck_full, wci, wco, tm)
    drel = _bias_bwd(dbias, nrel)
    dmod, dng, dkvg = _vec_bwd(sums_c, sums_f0, sums_q, sums_o, sums_f1, modval, ng_full, kv_norm_g.reshape(1, d))

    relw = -(-nrel // LANES) * LANES
    drel_p = jnp.concatenate([drel, jnp.zeros((nh, relw - nrel), F32)], axis=1)
    small3 = jnp.concatenate([dmod.reshape(16 * dq, LANES), dng.reshape(8 * dq, LANES), dkvg.reshape(8 * dq, LANES),
                              dck.reshape(8 * dq, LANES), drel_p.reshape(nh * relw // LANES, LANES)], axis=0)
    (sm,) = _exchange([small3], ["gather"], "gather_small_grads")
    g_wci = _wgrad_cols(h1a, dbcx, N_DEV, tk, "wgrad_conv_in")
    g_wco = _wgrad_cols(ua, dya, 1, tk, "wgrad_conv_out").reshape(N_DEV, dsh, d)
    (h_g4,), token = _xstart(scat([g_wci, g_wco]), sm, "grads_start_conv")
    o1, o2, o3, o4 = 16 * dq, 24 * dq, 32 * dq, 40 * dq
    dmod_all = sm[:, 0:o1].reshape(N_DEV, 16, d)
    mine_cols = lambda a: lax.dynamic_slice_in_dim(a, me * dsh, dsh, axis=2)
    dng_parts = mine_cols(sm[:, o1:o2].reshape(N_DEV, 8, d))
    dkvg_parts = sm[:, o2:o3].reshape(N_DEV, 8, d)[:, 0:1]
    dck_parts = mine_cols(sm[:, o3:o4].reshape(N_DEV, 8, d))[:, 0:3]
    drel_parts = sm[:, o4:].reshape(N_DEV, nh, relw)[:, :, 0:nrel]

    def update(parts, w, m, v, name, layers=1):
        shp = w.shape
        w3, m3, v3 = (a.reshape(layers, -1, shp[-1]) for a in (w, m, v))
        outs = _adamw_reduce([p.reshape(N_DEV, -1, shp[-1]) for p in parts], w3, m3, v3, 256, name)
        return [a.reshape(shp) for a in outs]

    p_wfi1, p_wfo1 = _xwait(h_g1, sm, "grads_wait_ffn1")
    p_wfi0, p_wfo0 = _xwait(h_g3, p_wfi1, "grads_wait_ffn0")
    u_ffn_in = update([p_wfi0, p_wfi1], ffn_w_in, m_ffn_w_in, v_ffn_w_in, "adamw_ffn_in", 2)
    u_ffn_out = update([p_wfo0, p_wfo1], ffn_w_out, m_ffn_w_out, v_ffn_w_out, "adamw_ffn_out", 2)
    p_wkv, p_wq, p_wo = _xwait(h_g2, u_ffn_out[0], "grads_wait_attn")
    u_w_kv = update([p_wkv], w_kv, m_w_kv, v_w_kv, "adamw_w_kv")
    u_w_q = update([p_wq], attn_w_q, m_attn_w_q, v_attn_w_q, "adamw_w_q")
    u_w_o = update([p_wo], attn_w_o, m_attn_w_o, v_attn_w_o, "adamw_w_o")

    sct = jnp.transpose(silu_c)
    dm_mod = jnp.stack([lax.dynamic_slice_in_dim(dmod_all[:, 6 * l:6 * l + 6].reshape(N_DEV, 6 * d), me * mw, mw, axis=1)
                        for l in range(nl)]).astype(BF16)
    dm_kv = lax.dynamic_slice_in_dim(dmod_all[:, R_KV:R_KV + 2].reshape(N_DEV, 2 * d), me * kmw, kmw, axis=1)
    u_mod_w = _adamw_outer(sct, dm_mod, mod_w, m_mod_w, v_mod_w, min(256, d), "adamw_mod_w")
    u_kv_mod_w = [a[0] for a in _adamw_outer(sct, dm_kv.astype(BF16)[None], kv_mod_w[None], m_kv_mod_w[None],
                                             v_kv_mod_w[None], min(256, d), "adamw_kv_mod_w")]

    modb_parts = jnp.stack([dmod_all[:, 6 * l:6 * l + 6].reshape(N_DEV, 6 * d) for l in range(nl)], axis=1)
    u_mod_b = update([modb_parts], mod_b, m_mod_b, v_mod_b, "adamw_mod_b")
    u_norm_g = update([dng_parts], norm_g.reshape(8, dsh), m_norm_g.reshape(8, dsh), v_norm_g.reshape(8, dsh), "adamw_norm_g")
    u_norm_g = [a.reshape(norm_g.shape) for a in u_norm_g]
    u_conv_k = update([dck_parts], conv_k, m_conv_k, v_conv_k, "adamw_conv_k")
    kvb_parts = dmod_all[:, R_KV:R_KV + 2].reshape(N_DEV, 1, 2 * d)
    u_kv_mod_b = [a.reshape(kv_mod_b.shape) for a in update([kvb_parts], kv_mod_b.reshape(1, -1), m_kv_mod_b.reshape(1, -1),
                                                            v_kv_mod_b.reshape(1, -1), "adamw_kv_mod_b")]
    u_kv_norm_g = [a.reshape(kv_norm_g.shape) for a in update([dkvg_parts], kv_norm_g.reshape(1, -1), m_kv_norm_g.reshape(1, -1),
                                                              v_kv_norm_g.reshape(1, -1), "adamw_kv_norm_g")]
    u_rel = update([drel_parts], rel_bias, m_rel_bias, v_rel_bias, "adamw_rel_bias")

    p_wci, p_wco = _xwait(h_g4, u_mod_w[0], "grads_wait_conv")
    u_conv_in = update([p_wci], conv_w_in, m_conv_w_in, v_conv_w_in, "adamw_conv_in")
    u_conv_out = update([p_wco], conv_w_out, m_conv_w_out, v_conv_w_out, "adamw_conv_out")

    ups = [u_mod_w, u_mod_b, u_norm_g, u_ffn_in, u_ffn_out, u_conv_in, u_conv_k, u_conv_out, u_kv_mod_w, u_kv_mod_b,
           u_kv_norm_g, u_w_kv, u_w_q, u_w_o, u_rel]
    return (loss, dx0[None], *[u[0] for u in ups], *[u[1] for u in ups], *[u[2] for u in ups], *[u[3] for u in ups])
```

```python
import functools

import jax
import jax.numpy as jnp
from jax import lax
from jax.experimental import pallas as pl
from jax.experimental.pallas import tpu as pltpu

F32 = jnp.float32
BF16 = jnp.bfloat16

EPS = 1e-6
CHUNK = 64
HEAD_DIM = 64
N_LEFT = 8
LANES = 128
QB = 4 * CHUNK
KW = QB + N_LEFT * CHUNK
NEG = -1e30
N_DEV = 8

ADAM_LR = 0.001
ADAM_B1 = 0.9
ADAM_B2 = 0.999
ADAM_EPS = 1e-08
ADAM_WD = 0.01
ADAM_STEP = 10

VMEM_BIG = 56 * 1024 * 1024

NT = (((1,), (1,)), ((), ()))
TN = (((0,), (0,)), ((), ()))

R_W1, R_SH1, R_P1, R_W2, R_SH2, R_P2 = range(6)
R_KV = 12


def _params(vmem):
    return pltpu.CompilerParams(vmem_limit_bytes=vmem)


def _row_tile(rows, cap):
    for t in range(min(cap, rows) // 16 * 16, 0, -16):
        if rows % t == 0:
            return t
    return rows


def _rows(tm, cols):
    return pl.BlockSpec((tm, cols), lambda i: (i, 0))


def _const(shape):
    nd = len(shape)
    return pl.BlockSpec(shape, lambda *_: (0,) * nd, pipeline_mode=pl.Buffered(1))


def _rs(x):
    return lax.rsqrt(jnp.mean(x * x, axis=-1, keepdims=True) + EPS)


def _norm_bwd(d, n, r):
    return r * (d - n * jnp.mean(d * n, axis=-1, keepdims=True))


def _colsum(a):
    return jnp.sum(a, axis=0, keepdims=True)


def _sigmoid(g):
    return 1.0 / (1.0 + jnp.exp(-g))


def _exchange(arrays, modes, name):
    n = len(arrays)
    out_shape = []
    for a, mode in zip(arrays, modes):
        shp = (N_DEV,) + a.shape if mode == "gather" else a.shape
        out_shape.append(jax.ShapeDtypeStruct(shp, a.dtype))

    def body(*refs):
        ins, outs = refs[:n], refs[n:2 * n]
        send_sems, recv_sems, local_sems = refs[2 * n:]
        x, y, c = lax.axis_index("x"), lax.axis_index("y"), lax.axis_index("c")
        me = 4 * x + 2 * y + c
        local, sends, recvs = [], [], []
        for a in range(n):
            own = ins[a] if modes[a] == "gather" else ins[a].at[me]
            cp = pltpu.make_async_copy(own, outs[a].at[me], local_sems.at[a])
            cp.start()
            local.append(cp)
        for k in range(1, N_DEV):
            px = 1 - x if k & 4 else x
            py = 1 - y if k & 2 else y
            pc = 1 - c if k & 1 else c
            peer = 4 * px + 2 * py + pc
            for a in range(n):
                src = ins[a] if modes[a] == "gather" else ins[a].at[peer]
                sem = a * (N_DEV - 1) + k - 1
                cp = pltpu.make_async_remote_copy(
                    src_ref=src, dst_ref=outs[a].at[me],
                    send_sem=send_sems.at[sem], recv_sem=recv_sems.at[sem],
                    device_id=(px, py, pc), device_id_type=pl.DeviceIdType.MESH)
                cp.start()
                sends.append(cp)
                recvs.append(pltpu.make_async_remote_copy(
                    src_ref=src, dst_ref=outs[a].at[peer],
                    send_sem=send_sems.at[sem], recv_sem=recv_sems.at[sem],
                    device_id=(px, py, pc), device_id_type=pl.DeviceIdType.MESH))
        for cp in recvs:
            cp.wait_recv()
        for cp in sends:
            cp.wait_send()
        for cp in local:
            cp.wait()

    any_spec = pl.BlockSpec(memory_space=pl.ANY)
    return pl.pallas_call(
        body, name=name,
        out_shape=tuple(out_shape),
        in_specs=[any_spec] * n,
        out_specs=tuple([any_spec] * n),
        scratch_shapes=[
            pltpu.SemaphoreType.DMA((n * (N_DEV - 1),)),
            pltpu.SemaphoreType.DMA((n * (N_DEV - 1),)),
            pltpu.SemaphoreType.DMA((n,)),
        ],
    )(*arrays)


def _peers(x, y, c):
    out = []
    for k in range(1, N_DEV):
        px = 1 - x if k & 4 else x
        py = 1 - y if k & 2 else y
        pc = 1 - c if k & 1 else c
        out.append((k - 1, (px, py, pc), 4 * px + 2 * py + pc))
    return out


def _land_shape(a, mode):
    return (N_DEV,) + a.shape if mode == "gather" else a.shape


_HBM = pl.BlockSpec(memory_space=pltpu.HBM)
_SEM = pl.BlockSpec(memory_space=pltpu.SEMAPHORE)
_EFFECT = pltpu.SideEffectType.DATAFLOW_SIDE_EFFECTING


def _xstart(groups, after, name):
    flat = [(a, m) for arrays, modes in groups for a, m in zip(arrays, modes)]
    n, ngr = len(flat), len(groups)
    sizes = [len(arrays) for arrays, _ in groups]
    npeer = N_DEV - 1

    def body(*refs):
        ins, lands = refs[:n], refs[n:2 * n]
        outs = refs[2 * n + 1:]
        sems = outs[:2 * ngr]
        token = outs[2 * ngr + 2 * n]
        local_sems = outs[2 * ngr + 2 * n + 1]
        stage = outs[2 * ngr + 2 * n + 2:]
        x, y, c = lax.axis_index("x"), lax.axis_index("y"), lax.axis_index("c")
        me = 4 * x + 2 * y + c
        loads, stores = [], []
        for a in range(n):
            own = ins[a] if flat[a][1] == "gather" else ins[a].at[me]
            loads.append(pltpu.make_async_copy(own, stage[a], local_sems.at[a]))
            stores.append(pltpu.make_async_copy(stage[a], lands[a].at[me], local_sems.at[a]))
            loads[a].start()
        for a in range(n):
            loads[a].wait()
            stores[a].start()
        a = 0
        for g in range(ngr):
            for j in range(sizes[g]):
                mode = flat[a][1]
                for slot, peer, pidx in _peers(x, y, c):
                    pltpu.make_async_remote_copy(
                        src_ref=ins[a] if mode == "gather" else ins[a].at[pidx], dst_ref=lands[a].at[me],
                        send_sem=sems[2 * g].at[j * npeer + slot], recv_sem=sems[2 * g + 1].at[j * npeer + slot],
                        device_id=peer, device_id_type=pl.DeviceIdType.MESH).start()
                a += 1
        for cp in stores:
            cp.wait()
        token[...] = jnp.zeros_like(token)

    out_shape, out_specs = [], []
    for sz in sizes:
        out_shape += [pltpu.SemaphoreType.DMA((sz * npeer,)), pltpu.SemaphoreType.DMA((sz * npeer,))]
        out_specs += [_SEM, _SEM]
    out_shape += [pltpu.HBM(a.shape, a.dtype) for a, _ in flat]
    out_shape += [pltpu.HBM(_land_shape(a, m), a.dtype) for a, m in flat]
    out_specs += [_HBM] * (2 * n)
    out_shape.append(jax.ShapeDtypeStruct((8, LANES), F32))
    out_specs.append(pl.BlockSpec(memory_space=pltpu.VMEM))
    args = [pltpu.with_memory_space_constraint(a, pltpu.HBM) for a, _ in flat]
    args += [pltpu.with_memory_space_constraint(lax.empty(_land_shape(a, m), a.dtype), pltpu.HBM) for a, m in flat]
    res = pl.pallas_call(
        body, name=name, out_shape=tuple(out_shape),
        in_specs=[_HBM] * (2 * n) + [pl.BlockSpec(memory_space=pl.ANY)], out_specs=tuple(out_specs),
        input_output_aliases={i: 2 * ngr + i for i in range(2 * n)},
        scratch_shapes=[pltpu.SemaphoreType.DMA((n,))]
                       + [pltpu.VMEM(a.shape if m == "gather" else a.shape[1:], a.dtype) for a, m in flat],
        compiler_params=pltpu.CompilerParams(has_side_effects=_EFFECT, vmem_limit_bytes=VMEM_BIG),
    )(*args, after)
    handles, a = [], 0
    for g, sz in enumerate(sizes):
        handles.append((res[2 * g], res[2 * g + 1], list(res[2 * ngr + a:2 * ngr + a + sz]),
                        list(res[2 * ngr + n + a:2 * ngr + n + a + sz]), list(groups[g][1])))
        a += sz
    return handles, res[-1]


def _xwait(handle, after, name):
    send_sems, recv_sems, srcs, lands, modes = handle
    m = len(srcs)
    npeer = N_DEV - 1

    def body(*refs):
        ins, lnd = refs[:m], refs[m:2 * m]
        ssem, rsem = refs[2 * m], refs[2 * m + 1]
        x, y, c = lax.axis_index("x"), lax.axis_index("y"), lax.axis_index("c")
        for j in range(m):
            for slot, peer, pidx in _peers(x, y, c):
                cp = pltpu.make_async_remote_copy(
                    src_ref=ins[j] if modes[j] == "gather" else ins[j].at[pidx], dst_ref=lnd[j].at[pidx],
                    send_sem=ssem.at[j * npeer + slot], recv_sem=rsem.at[j * npeer + slot],
                    device_id=peer, device_id_type=pl.DeviceIdType.MESH)
                cp.wait_send()
                cp.wait_recv()

    res = pl.pallas_call(
        body, name=name,
        out_shape=tuple([pltpu.HBM(a.shape, a.dtype) for a in srcs] + [pltpu.HBM(a.shape, a.dtype) for a in lands]),
        in_specs=[_HBM] * (2 * m) + [_SEM, _SEM, pl.BlockSpec(memory_space=pl.ANY)],
        out_specs=tuple([_HBM] * (2 * m)),
        input_output_aliases={i: i for i in range(2 * m)},
        compiler_params=pltpu.CompilerParams(has_side_effects=_EFFECT),
    )(*srcs, *lands, send_sems, recv_sems, after)
    return list(res[m:])


def _mod_fwd(c_all, mod_w, kv_mod_w):
    nl, d, mw = mod_w.shape
    kw = kv_mod_w.shape[1]

    def body(c_ref, mw_ref, kw_ref, o_ref, sc_ref):
        cc = c_ref[...]
        sc = (cc * _sigmoid(cc)).astype(BF16)
        sc_ref[...] = sc
        for l in range(nl):
            o_ref[:, l * mw:(l + 1) * mw] = jnp.dot(sc, mw_ref[l].astype(BF16), preferred_element_type=F32)
        o_ref[:, nl * mw:nl * mw + kw] = jnp.dot(sc, kw_ref[...].astype(BF16), preferred_element_type=F32)

    return pl.pallas_call(
        body, name="mod_fwd",
        out_shape=(jax.ShapeDtypeStruct((c_all.shape[0], nl * mw + kw), F32),
                   jax.ShapeDtypeStruct(c_all.shape, BF16)),
        compiler_params=_params(VMEM_BIG),
    )(c_all, mod_w, kv_mod_w)


def _vec_prep(modrow, modb, kvrow, kvb, ng, kvg):
    d = ng.shape[1]

    def body(mr_ref, mb_ref, kr_ref, kb_ref, ng_ref, kvg_ref, t_ref, m_ref):
        t_ref[...] = jnp.zeros_like(t_ref)
        m_ref[...] = jnp.zeros_like(m_ref)
        for l in range(2):
            mod = mr_ref[l] + mb_ref[l]
            m_ref[6 * l:6 * l + 6, :] = mod
            g = ng_ref[4 * l:4 * l + 4, :]
            t_ref[6 * l + R_W1:6 * l + R_W1 + 1, :] = g[0:1] * (1.0 + mod[1:2])
            t_ref[6 * l + R_SH1:6 * l + R_SH1 + 1, :] = mod[0:1]
            t_ref[6 * l + R_P1:6 * l + R_P1 + 1, :] = mod[2:3] * g[1:2]
            t_ref[6 * l + R_W2:6 * l + R_W2 + 1, :] = g[2:3] * (1.0 + mod[4:5])
            t_ref[6 * l + R_SH2:6 * l + R_SH2 + 1, :] = mod[3:4]
            t_ref[6 * l + R_P2:6 * l + R_P2 + 1, :] = mod[5:6] * g[3:4]
        kv = kr_ref[...] + kb_ref[...]
        m_ref[R_KV:R_KV + 2, :] = kv
        t_ref[R_KV:R_KV + 1, :] = kvg_ref[...] * (1.0 + kv[1:2])
        t_ref[R_KV + 1:R_KV + 2, :] = kv[0:1]

    return pl.pallas_call(
        body, name="vec_prep",
        out_shape=(jax.ShapeDtypeStruct((16, d), F32), jax.ShapeDtypeStruct((16, d), F32)),
    )(modrow, modb, kvrow, kvb, ng, kvg)


def _vec_bwd(sums_c, sums_f0, sums_q, sums_o, sums_f1, mt, ng, kvg):
    d = ng.shape[1]

    def body(sc_ref, sf0_ref, sq_ref, so_ref, sf1_ref, m_ref, ng_ref, kvg_ref, dm_ref, dng_ref, dkvg_ref, g_ref):
        g_ref[...] = jnp.zeros_like(g_ref)
        g_ref[0:3, :] = sc_ref[0:3, :]
        g_ref[3:6, :] = sf0_ref[3:6, :]
        g_ref[6:8, :] = sq_ref[0:2, :]
        g_ref[8:9, :] = so_ref[2:3, :]
        g_ref[9:12, :] = sf1_ref[3:6, :]
        g_ref[R_KV:R_KV + 2, :] = sq_ref[2:4, :]
        dm_ref[...] = jnp.zeros_like(dm_ref)
        dkvg_ref[...] = jnp.zeros_like(dkvg_ref)
        for l in range(2):
            g = ng_ref[4 * l:4 * l + 4, :]
            mod = m_ref[6 * l:6 * l + 6, :]
            s = g_ref[6 * l:6 * l + 6, :]
            dm_ref[6 * l + 0:6 * l + 1, :] = s[1:2]
            dm_ref[6 * l + 1:6 * l + 2, :] = s[0:1] * g[0:1]
            dm_ref[6 * l + 2:6 * l + 3, :] = s[2:3] * g[1:2]
            dm_ref[6 * l + 3:6 * l + 4, :] = s[4:5]
            dm_ref[6 * l + 4:6 * l + 5, :] = s[3:4] * g[2:3]
            dm_ref[6 * l + 5:6 * l + 6, :] = s[5:6] * g[3:4]
            dng_ref[4 * l + 0:4 * l + 1, :] = s[0:1] * (1.0 + mod[1:2])
            dng_ref[4 * l + 1:4 * l + 2, :] = s[2:3] * mod[2:3]
            dng_ref[4 * l + 2:4 * l + 3, :] = s[3:4] * (1.0 + mod[4:5])
            dng_ref[4 * l + 3:4 * l + 4, :] = s[5:6] * mod[5:6]
        dm_ref[R_KV:R_KV + 1, :] = g_ref[R_KV + 1:R_KV + 2, :]
        dm_ref[R_KV + 1:R_KV + 2, :] = g_ref[R_KV:R_KV + 1, :] * kvg_ref[...]
        dkvg_ref[0:1, :] = g_ref[R_KV:R_KV + 1, :] * (1.0 + m_ref[R_KV + 1:R_KV + 2, :])

    return pl.pallas_call(
        body, name="vec_bwd",
        out_shape=(jax.ShapeDtypeStruct((16, d), F32), jax.ShapeDtypeStruct((8, d), F32),
                   jax.ShapeDtypeStruct((8, d), F32)),
        scratch_shapes=[pltpu.VMEM((16, d), F32)],
    )(sums_c, sums_f0, sums_q, sums_o, sums_f1, mt, ng, kvg)


def _rel_index(nrel):
    width = KW + QB
    e = lax.broadcasted_iota(jnp.int32, (nrel, width), 1)
    r = lax.broadcasted_iota(jnp.int32, (nrel, width), 0)
    max_rel = (nrel - 1) // 2
    idx = jnp.clip(KW - e, -max_rel, max_rel) + max_rel
    return (idx == r).astype(F32)


def _band_valid():
    row = lax.broadcasted_iota(jnp.int32, (QB, KW), 0) // CHUNK
    col = lax.broadcasted_iota(jnp.int32, (QB, KW), 1) // CHUNK
    j = col - row
    return (j >= 0) & (j <= N_LEFT)


def _bias_fwd(rel_bias):
    nh, nrel = rel_bias.shape
    width = KW + QB

    def body(rb_ref, o_ref):
        onehot = _rel_index(nrel)
        gr = jnp.dot(rb_ref[...], onehot, preferred_element_type=F32, precision=lax.Precision.HIGHEST)
        valid = _band_valid()
        for h in range(nh):
            xrow = jnp.broadcast_to(gr[h:h + 1, :], (QB, width))
            rolled = pltpu.roll(xrow, 0, 1, stride=1, stride_axis=0)
            o_ref[h] = jnp.where(valid, rolled[:, QB:], NEG)

    return pl.pallas_call(
        body, name="bias_fwd",
        out_shape=jax.ShapeDtypeStruct((nh, QB, KW), F32),
        compiler_params=_params(VMEM_BIG),
    )(rel_bias)


def _bias_bwd(dbias, nrel):
    nh = dbias.shape[0]
    width = KW + QB

    def body(db_ref, o_ref, diag_ref):
        onehot = _rel_index(nrel)
        valid = _band_valid()
        rr = lax.broadcasted_iota(jnp.int32, (QB, QB), 0)
        cc = lax.broadcasted_iota(jnp.int32, (QB, QB), 1)
        flip = (rr + cc == QB - 1).astype(F32)
        for h in range(nh):
            rev = jnp.dot(flip, jnp.where(valid, db_ref[h], 0.0), preferred_element_type=F32,
                          precision=lax.Precision.HIGHEST)
            w = jnp.concatenate([jnp.zeros((QB, QB), F32), rev], axis=1)
            back = pltpu.roll(w, width - (QB - 1), 1, stride=1, stride_axis=0)
            diag_ref[h:h + 1, :] = _colsum(back)
        o_ref[...] = lax.dot_general(diag_ref[...], onehot, NT, preferred_element_type=F32,
                                     precision=lax.Precision.HIGHEST)

    return pl.pallas_call(
        body, name="bias_bwd",
        out_shape=jax.ShapeDtypeStruct((nh, nrel), F32),
        scratch_shapes=[pltpu.VMEM((nh, width), F32)],
        compiler_params=_params(VMEM_BIG),
    )(dbias)


def _conv_fwd(x, tab, ck, wci, wco, tm):
    s, d = x.shape
    nsh, _, cw = wci.shape

    def body(x_ref, t_ref, ck_ref, wci_ref, wco_ref, x1_ref, h_ref, bcx_ref, u_ref, y_ref, carry):
        @pl.when(pl.program_id(0) == 0)
        def _():
            carry[...] = jnp.zeros_like(carry)

        xv = x_ref[...]
        hb = ((xv * _rs(xv)) * t_ref[R_W1:R_W1 + 1, :] + t_ref[R_SH1:R_SH1 + 1, :]).astype(BF16)
        h_ref[...] = hb
        for j in range(nsh):
            bcx_ref[:, j * cw:(j + 1) * cw] = jnp.dot(hb, wci_ref[j], preferred_element_type=F32)
        bg, cg, xi = bcx_ref[:, 0:d], bcx_ref[:, d:2 * d], bcx_ref[:, 2 * d:3 * d]
        z = cg * xi
        row = lax.broadcasted_iota(jnp.int32, z.shape, 0)
        c1, c2 = carry[7:8, :], carry[6:7, :]
        z1 = jnp.where(row == 0, c1, pltpu.roll(z, 1, 0))
        z2 = jnp.where(row == 0, c2, jnp.where(row == 1, c1, pltpu.roll(z, 2, 0)))
        carry[...] = z[tm - 8:tm, :]
        conv = ck_ref[0:1, :] * z2 + ck_ref[1:2, :] * z1 + ck_ref[2:3, :] * z
        ub = (bg * conv).astype(BF16)
        u_ref[...] = ub
        yv = jnp.dot(ub, wco_ref[...], preferred_element_type=F32)
        y_ref[...] = yv
        x1_ref[...] = xv + (yv * _rs(yv)) * t_ref[R_P1:R_P1 + 1, :]

    return pl.pallas_call(
        body, name="conv_fwd", grid=(s // tm,),
        in_specs=[_rows(tm, d), _const(tab.shape), _const(ck.shape), _const(wci.shape), _const(wco.shape)],
        out_specs=(_rows(tm, d), _rows(tm, d), _rows(tm, 3 * d), _rows(tm, d), _rows(tm, d)),
        out_shape=(jax.ShapeDtypeStruct((s, d), F32), jax.ShapeDtypeStruct((s, d), BF16),
                   jax.ShapeDtypeStruct((s, 3 * d), F32), jax.ShapeDtypeStruct((s, d), BF16),
                   jax.ShapeDtypeStruct((s, d), F32)),
        scratch_shapes=[pltpu.VMEM((8, d), F32)],
        compiler_params=_params(VMEM_BIG),
    )(x, tab, ck, wci, wco)


def _ffn_fwd(x, tab, base, wfi, wfo, tgt, tm, name):
    s, d = x.shape
    nsh, _, fw = wfi.shape
    nh = nsh // 2
    with_loss = tgt is not None

    def body(*refs):
        if with_loss:
            x_ref, t_ref, wfi_ref, wfo_ref, tgt_ref, xo_ref, h_ref, gu_ref, a_ref, y_ref, loss_ref = refs
        else:
            x_ref, t_ref, wfi_ref, wfo_ref, xo_ref, h_ref, gu_ref, a_ref, y_ref = refs
        xv = x_ref[...]
        hb = ((xv * _rs(xv)) * t_ref[base + R_W2:base + R_W2 + 1, :]
              + t_ref[base + R_SH2:base + R_SH2 + 1, :]).astype(BF16)
        h_ref[...] = hb
        acc = jnp.zeros((tm, d), F32)
        for j in range(nh):
            g = jnp.dot(hb, wfi_ref[j], preferred_element_type=F32)
            u = jnp.dot(hb, wfi_ref[j + nh], preferred_element_type=F32)
            gu_ref[j] = g
            gu_ref[j + nh] = u
            ab = ((g * _sigmoid(g)) * u).astype(BF16)
            a_ref[j] = ab
            acc = acc + jnp.dot(ab, wfo_ref[j], preferred_element_type=F32)
        y_ref[...] = acc
        xo = xv + (acc * _rs(acc)) * t_ref[base + R_P2:base + R_P2 + 1, :]
        if with_loss:
            @pl.when(pl.program_id(0) == 0)
            def _():
                loss_ref[...] = jnp.zeros_like(loss_ref)

            err = xo - tgt_ref[...]
            xo_ref[...] = err * (1.0 / d)
            loss_ref[...] += jnp.sum(err * err)
        else:
            xo_ref[...] = xo

    in_specs = [_rows(tm, d), _const(tab.shape), _const(wfi.shape), _const(wfo.shape)]
    args = [x, tab, wfi, wfo]
    out_specs = [_rows(tm, d), _rows(tm, d), pl.BlockSpec((nsh, tm, fw), lambda i: (0, i, 0)),
                 pl.BlockSpec((nh, tm, fw), lambda i: (0, i, 0)), _rows(tm, d)]
    out_shape = [jax.ShapeDtypeStruct((s, d), F32), jax.ShapeDtypeStruct((s, d), BF16),
                 jax.ShapeDtypeStruct((nsh, s, fw), F32), jax.ShapeDtypeStruct((nh, s, fw), BF16),
                 jax.ShapeDtypeStruct((s, d), F32)]
    if with_loss:
        in_specs.append(_rows(tm, d))
        args.append(tgt)
        out_specs.append(pl.BlockSpec((8, LANES), lambda i: (0, 0)))
        out_shape.append(jax.ShapeDtypeStruct((8, LANES), F32))
    return pl.pallas_call(
        body, name=name, grid=(s // tm,), in_specs=in_specs, out_specs=tuple(out_specs),
        out_shape=tuple(out_shape), compiler_params=_params(VMEM_BIG),
    )(*args)


def _qkv_fwd(x, tab, wq, wkv, tm):
    s, d = x.shape
    nsh, _, kw = wkv.shape
    nh = nsh // 2
    base = 6

    def body(x_ref, t_ref, wq_ref, wkv_ref, hkv_ref, h1_ref, q_ref, k_ref, v_ref):
        xv = x_ref[...]
        n = xv * _rs(xv)
        hkv = (n * t_ref[R_KV:R_KV + 1, :] + t_ref[R_KV + 1:R_KV + 2, :]).astype(BF16)
        h1 = (n * t_ref[base + R_W1:base + R_W1 + 1, :] + t_ref[base + R_SH1:base + R_SH1 + 1, :]).astype(BF16)
        hkv_ref[...] = hkv
        h1_ref[...] = h1
        q_ref[...] = (jnp.dot(h1, wq_ref[...], preferred_element_type=F32) * (HEAD_DIM ** -0.5)).astype(BF16)
        for j in range(nh):
            k_ref[:, j * kw:(j + 1) * kw] = jnp.dot(hkv, wkv_ref[j], preferred_element_type=F32).astype(BF16)
            v_ref[:, j * kw:(j + 1) * kw] = jnp.dot(hkv, wkv_ref[j + nh], preferred_element_type=F32).astype(BF16)

    act = jax.ShapeDtypeStruct((s, d), BF16)
    return pl.pallas_call(
        body, name="qkv_fwd", grid=(s // tm,),
        in_specs=[_rows(tm, d), _const(tab.shape), _const(wq.shape), _const(wkv.shape)],
        out_specs=tuple([_rows(tm, d)] * 5), out_shape=(act,) * 5,
        compiler_params=_params(VMEM_BIG),
    )(x, tab, wq, wkv)


def _window_specs():
    return [pl.BlockSpec((QB, LANES), (lambda p, b, w=w: (jnp.maximum(b - 2 + w, 0), p))) for w in range(3)]


def _key_valid(b):
    col = lax.broadcasted_iota(jnp.int32, (QB, KW), 1) // CHUNK
    return (b * (QB // CHUNK) - N_LEFT + col) >= 0


def _head_masks():
    lane = lax.broadcasted_iota(jnp.int32, (1, LANES), 1)
    return [(lane // HEAD_DIM == hh) for hh in range(LANES // HEAD_DIM)]


def _attn_fwd(q, k, v, bias):
    s, d = q.shape
    npair, nb = d // LANES, s // QB
    hpp = LANES // HEAD_DIM

    def body(q_ref, k0, k1, k2, v0, v1, v2, bias_ref, o_ref, lse_ref):
        b = pl.program_id(1)
        qv = q_ref[...]
        kwin = jnp.concatenate([k0[...], k1[...], k2[...]], axis=0)
        vwin = jnp.concatenate([v0[...], v1[...], v2[...]], axis=0)
        valid = _key_valid(b)
        masks = _head_masks()
        o = jnp.zeros((QB, LANES), F32)
        lse = jnp.zeros((QB, LANES), F32)
        for hh in range(hpp):
            qm = jnp.where(masks[hh], qv, jnp.zeros_like(qv))
            vm = jnp.where(masks[hh], vwin, jnp.zeros_like(vwin))
            sc = lax.dot_general(qm, kwin, NT, preferred_element_type=F32) + bias_ref[hh]
            sc = jnp.where(valid, sc, NEG)
            m = jnp.max(sc, axis=-1, keepdims=True)
            p = jnp.exp(sc - m)
            l = jnp.sum(p, axis=-1, keepdims=True)
            o = o + jnp.dot(p.astype(BF16), vm, preferred_element_type=F32) * (1.0 / l)
            lse = jnp.where(masks[hh], m + jnp.log(l), lse)
        o_ref[...] = o.astype(BF16)
        lse_ref[...] = lse

    blk = pl.BlockSpec((QB, LANES), lambda p, b: (b, p))
    return pl.pallas_call(
        body, name="attn_fwd", grid=(npair, nb),
        in_specs=[blk] + _window_specs() + _window_specs()
                 + [pl.BlockSpec((hpp, QB, KW), lambda p, b: (p, 0, 0))],
        out_specs=(blk, blk),
        out_shape=(jax.ShapeDtypeStruct((s, d), BF16), jax.ShapeDtypeStruct((s, d), F32)),
        compiler_params=_params(VMEM_BIG),
    )(q, k, k, k, v, v, v, bias)


def _attn_out_fwd(o, x, tab, wo, tm):
    s, d = x.shape
    base = 6

    def body(o_ref, x_ref, t_ref, wo_ref, x3_ref, y_ref):
        yv = jnp.dot(o_ref[...], wo_ref[...], preferred_element_type=F32)
        y_ref[...] = yv
        x3_ref[...] = x_ref[...] + (yv * _rs(yv)) * t_ref[base + R_P1:base + R_P1 + 1, :]

    return pl.pallas_call(
        body, name="attn_out_fwd", grid=(s // tm,),
        in_specs=[_rows(tm, d), _rows(tm, d), _const(tab.shape), _const(wo.shape)],
        out_specs=(_rows(tm, d), _rows(tm, d)),
        out_shape=(jax.ShapeDtypeStruct((s, d), F32), jax.ShapeDtypeStruct((s, d), F32)),
        compiler_params=_params(VMEM_BIG),
    )(o, x, tab, wo)


def _ffn_bwd(dxo, x, y, gu, tab, base, wfi, wfo, tm, name):
    s, d = x.shape
    nsh, _, fw = wfi.shape
    nh = nsh // 2

    def body(dxo_ref, x_ref, y_ref, gu_ref, t_ref, wfi_ref, wfo_ref, dx_ref, dyb_ref, dgu_ref, sums_ref):
        @pl.when(pl.program_id(0) == 0)
        def _():
            sums_ref[...] = jnp.zeros_like(sums_ref)

        dxo_v = dxo_ref[...]
        yv = y_ref[...]
        ry = _rs(yv)
        ny = yv * ry
        sums_ref[R_P2:R_P2 + 1, :] += _colsum(dxo_v * ny)
        dyb = _norm_bwd(dxo_v * t_ref[base + R_P2:base + R_P2 + 1, :], ny, ry).astype(BF16)
        dyb_ref[...] = dyb
        dh = jnp.zeros((tm, d), F32)
        for j in range(nh):
            da = lax.dot_general(dyb, wfo_ref[j], NT, preferred_element_type=F32)
            g, u = gu_ref[j], gu_ref[j + nh]
            sg = _sigmoid(g)
            dg = (da * u * sg * (1.0 + g * (1.0 - sg))).astype(BF16)
            du = (da * g * sg).astype(BF16)
            dgu_ref[j] = dg
            dgu_ref[j + nh] = du
            dh = dh + lax.dot_general(dg, wfi_ref[j], NT, preferred_element_type=F32)
            dh = dh + lax.dot_general(du, wfi_ref[j + nh], NT, preferred_element_type=F32)
        xv = x_ref[...]
        r = _rs(xv)
        n = xv * r
        sums_ref[R_SH2:R_SH2 + 1, :] += _colsum(dh)
        sums_ref[R_W2:R_W2 + 1, :] += _colsum(dh * n)
        dx_ref[...] = dxo_v + _norm_bwd(dh * t_ref[base + R_W2:base + R_W2 + 1, :], n, r)

    return pl.pallas_call(
        body, name=name, grid=(s // tm,),
        in_specs=[_rows(tm, d), _rows(tm, d), _rows(tm, d), pl.BlockSpec((nsh, tm, fw), lambda i: (0, i, 0)),
                  _const(tab.shape), _const(wfi.shape), _const(wfo.shape)],
        out_specs=(_rows(tm, d), _rows(tm, d), pl.BlockSpec((nsh, tm, fw), lambda i: (0, i, 0)),
                   pl.BlockSpec((8, d), lambda i: (0, 0))),
        out_shape=(jax.ShapeDtypeStruct((s, d), F32), jax.ShapeDtypeStruct((s, d), BF16),
                   jax.ShapeDtypeStruct((nsh, s, fw), BF16), jax.ShapeDtypeStruct((8, d), F32)),
        compiler_params=_params(VMEM_BIG),
    )(dxo, x, y, gu, tab, wfi, wfo)


def _attn_out_bwd(dx, y, tab, wo, tm):
    s, d = y.shape
    base = 6

    def body(dx_ref, y_ref, t_ref, wo_ref, dyb_ref, do_ref, sums_ref):
        @pl.when(pl.program_id(0) == 0)
        def _():
            sums_ref[...] = jnp.zeros_like(sums_ref)

        dxv = dx_ref[...]
        yv = y_ref[...]
        ry = _rs(yv)
        ny = yv * ry
        sums_ref[R_P1:R_P1 + 1, :] += _colsum(dxv * ny)
        dyb = _norm_bwd(dxv * t_ref[base + R_P1:base + R_P1 + 1, :], ny, ry).astype(BF16)
        dyb_ref[...] = dyb
        do_ref[...] = lax.dot_general(dyb, wo_ref[...], NT, preferred_element_type=F32).astype(BF16)

    return pl.pallas_call(
        body, name="attn_out_bwd", grid=(s // tm,),
        in_specs=[_rows(tm, d), _rows(tm, d), _const(tab.shape), _const(wo.shape)],
        out_specs=(_rows(tm, d), _rows(tm, d), pl.BlockSpec((8, d), lambda i: (0, 0))),
        out_shape=(jax.ShapeDtypeStruct((s, d), BF16), jax.ShapeDtypeStruct((s, d), BF16),
                   jax.ShapeDtypeStruct((8, d), F32)),
        compiler_params=_params(VMEM_BIG),
    )(dx, y, tab, wo)


def _attn_bwd(q, k, v, do, lse, bias):
    s, d = q.shape
    npair, nb = d // LANES, s // QB
    hpp = LANES // HEAD_DIM

    def body(q_ref, k0, k1, k2, v0, v1, v2, do_ref, lse_ref, bias_ref, dq_ref, dk_ref, dv_ref, db_ref):
        b = pl.program_id(1)

        @pl.when(b == 0)
        def _():
            dk_ref[...] = jnp.zeros_like(dk_ref)
            dv_ref[...] = jnp.zeros_like(dv_ref)
            db_ref[...] = jnp.zeros_like(db_ref)

        qv = q_ref[...]
        dov = do_ref[...]
        lsev = lse_ref[...]
        kwin = jnp.concatenate([k0[...], k1[...], k2[...]], axis=0)
        vwin = jnp.concatenate([v0[...], v1[...], v2[...]], axis=0)
        valid = _key_valid(b)
        masks = _head_masks()
        dq = jnp.zeros((QB, LANES), F32)
        dkw = jnp.zeros((KW, LANES), F32)
        dvw = jnp.zeros((KW, LANES), F32)
        for hh in range(hpp):
            qm = jnp.where(masks[hh], qv, jnp.zeros_like(qv))
            dom = jnp.where(masks[hh], dov, jnp.zeros_like(dov))
            km = jnp.where(masks[hh], kwin, jnp.zeros_like(kwin))
            lse_h = jnp.max(jnp.where(masks[hh], lsev, NEG), axis=-1, keepdims=True)
            sc = lax.dot_general(qm, kwin, NT, preferred_element_type=F32) + bias_ref[hh]
            sc = jnp.where(valid, sc, NEG)
            p = jnp.exp(sc - lse_h)
            dp = lax.dot_general(dom, vwin, NT, preferred_element_type=F32)
            ds = p * (dp - jnp.sum(dp * p, axis=-1, keepdims=True))
            db_ref[hh] += ds
            dsb = ds.astype(BF16)
            dq = dq + jnp.dot(dsb, km, preferred_element_type=F32)
            dkw = dkw + lax.dot_general(dsb, qm, TN, preferred_element_type=F32)
            dvw = dvw + lax.dot_general(p.astype(BF16), dom, TN, preferred_element_type=F32)
        dq_ref[...] = (dq * (HEAD_DIM ** -0.5)).astype(BF16)
        for w in range(3):
            start = pl.multiple_of(jnp.maximum(b - 2 + w, 0) * QB, QB)
            dk_ref[pl.ds(start, QB), :] += dkw[w * QB:(w + 1) * QB, :]
            dv_ref[pl.ds(start, QB), :] += dvw[w * QB:(w + 1) * QB, :]

    blk = pl.BlockSpec((QB, LANES), lambda p, b: (b, p))
    col = pl.BlockSpec((s, LANES), lambda p, b: (0, p))
    pair = pl.BlockSpec((hpp, QB, KW), lambda p, b: (p, 0, 0))
    return pl.pallas_call(
        body, name="attn_bwd", grid=(npair, nb),
        in_specs=[blk] + _window_specs() + _window_specs() + [blk, blk, pair],
        out_specs=(blk, col, col, pair),
        out_shape=(jax.ShapeDtypeStruct((s, d), BF16), jax.ShapeDtypeStruct((s, d), F32),
                   jax.ShapeDtypeStruct((s, d), F32), jax.ShapeDtypeStruct(bias.shape, F32)),
        compiler_params=_params(VMEM_BIG),
    )(q, k, k, k, v, v, v, do, lse, bias)


def _qkv_bwd(dres, dq, dk, dv, x, tab, wq, wkv, tm):
    s, d = x.shape
    nsh, _, kw = wkv.shape
    nh = nsh // 2
    base = 6

    def body(dres_ref, dq_ref, dk_ref, dv_ref, x_ref, t_ref, wq_ref, wkv_ref, dx_ref, dkv_ref, sums_ref):
        @pl.when(pl.program_id(0) == 0)
        def _():
            sums_ref[...] = jnp.zeros_like(sums_ref)

        dh1 = lax.dot_general(dq_ref[...], wq_ref[...], NT, preferred_element_type=F32)
        dkv_ref[:, 0:d] = dk_ref[...].astype(BF16)
        dkv_ref[:, d:2 * d] = dv_ref[...].astype(BF16)
        dhkv = jnp.zeros((tm, d), F32)
        for j in range(nsh):
            dhkv = dhkv + lax.dot_general(dkv_ref[:, j * kw:(j + 1) * kw], wkv_ref[j], NT,
                                          preferred_element_type=F32)
        xv = x_ref[...]
        r = _rs(xv)
        n = xv * r
        sums_ref[0:1, :] += _colsum(dh1 * n)
        sums_ref[1:2, :] += _colsum(dh1)
        sums_ref[2:3, :] += _colsum(dhkv * n)
        sums_ref[3:4, :] += _colsum(dhkv)
        dn = dh1 * t_ref[base + R_W1:base + R_W1 + 1, :] + dhkv * t_ref[R_KV:R_KV + 1, :]
        dx_ref[...] = dres_ref[...] + _norm_bwd(dn, n, r)

    return pl.pallas_call(
        body, name="qkv_bwd", grid=(s // tm,),
        in_specs=[_rows(tm, d)] * 5 + [_const(tab.shape), _const(wq.shape), _const(wkv.shape)],
        out_specs=(_rows(tm, d), _rows(tm, 2 * d), pl.BlockSpec((8, d), lambda i: (0, 0))),
        out_shape=(jax.ShapeDtypeStruct((s, d), F32), jax.ShapeDtypeStruct((s, 2 * d), BF16),
                   jax.ShapeDtypeStruct((8, d), F32)),
        compiler_params=_params(VMEM_BIG),
    )(dres, dq, dk, dv, x, tab, wq, wkv)


def _conv_bwd(dx1, x, y, bcx, tab, ck, wci, wco, tm):
    s, d = x.shape
    nsh, _, cw = wci.shape
    nt = s // tm

    def rev(i):
        return (nt - 1 - i, 0)

    def halo(i):
        return (jnp.maximum((nt - 1 - i) * (tm // 8) - 1, 0), 0)

    def body(dx_ref, x_ref, y_ref, bcx_ref, halo_ref, t_ref, ck_ref, wci_ref, wco_ref,
             dx0_ref, dyb_ref, dbcx_ref, sums_ref, dck_ref, carry):
        i = pl.program_id(0)

        @pl.when(i == 0)
        def _():
            sums_ref[...] = jnp.zeros_like(sums_ref)
            dck_ref[...] = jnp.zeros_like(dck_ref)
            carry[...] = jnp.zeros_like(carry)

        dxv = dx_ref[...]
        yv = y_ref[...]
        ry = _rs(yv)
        ny = yv * ry
        sums_ref[R_P1:R_P1 + 1, :] += _colsum(dxv * ny)
        dyb = _norm_bwd(dxv * t_ref[R_P1:R_P1 + 1, :], ny, ry).astype(BF16)
        dyb_ref[...] = dyb
        du = lax.dot_general(dyb, wco_ref[...], NT, preferred_element_type=F32)
        bg, cg, xi = bcx_ref[:, 0:d], bcx_ref[:, d:2 * d], bcx_ref[:, 2 * d:3 * d]
        z = cg * xi
        zp = halo_ref[:, d:2 * d] * halo_ref[:, 2 * d:3 * d]
        zp = jnp.where(i == nt - 1, jnp.zeros_like(zp), zp)
        row = lax.broadcasted_iota(jnp.int32, z.shape, 0)
        c1, c2 = zp[7:8, :], zp[6:7, :]
        z1 = jnp.where(row == 0, c1, pltpu.roll(z, 1, 0))
        z2 = jnp.where(row == 0, c2, jnp.where(row == 1, c1, pltpu.roll(z, 2, 0)))
        k0, k1, k2 = ck_ref[0:1, :], ck_ref[1:2, :], ck_ref[2:3, :]
        conv = k0 * z2 + k1 * z1 + k2 * z
        dconv = du * bg
        dck_ref[0:1, :] += _colsum(dconv * z2)
        dck_ref[1:2, :] += _colsum(dconv * z1)
        dck_ref[2:3, :] += _colsum(dconv * z)
        n1, n2 = carry[0:1, :], carry[1:2, :]
        d1 = jnp.where(row == tm - 1, n1, pltpu.roll(dconv, tm - 1, 0))
        d2 = jnp.where(row == tm - 1, n2, jnp.where(row == tm - 2, n1, pltpu.roll(dconv, tm - 2, 0)))
        carry[...] = dconv[0:8, :]
        dz = k2 * dconv + k1 * d1 + k0 * d2
        dbcx_ref[:, 0:d] = (du * conv).astype(BF16)
        dbcx_ref[:, d:2 * d] = (dz * xi).astype(BF16)
        dbcx_ref[:, 2 * d:3 * d] = (dz * cg).astype(BF16)
        dh = jnp.zeros((tm, d), F32)
        for j in range(nsh):
            dh = dh + lax.dot_general(dbcx_ref[:, j * cw:(j + 1) * cw], wci_ref[j], NT,
                                      preferred_element_type=F32)
        xv = x_ref[...]
        r = _rs(xv)
        n = xv * r
        sums_ref[R_W1:R_W1 + 1, :] += _colsum(dh * n)
        sums_ref[R_SH1:R_SH1 + 1, :] += _colsum(dh)
        dx0_ref[...] = dxv + _norm_bwd(dh * t_ref[R_W1:R_W1 + 1, :], n, r)

    rrow = lambda cols: pl.BlockSpec((tm, cols), rev)
    acc = pl.BlockSpec((8, d), lambda i: (0, 0))
    return pl.pallas_call(
        body, name="conv_bwd", grid=(nt,),
        in_specs=[rrow(d), rrow(d), rrow(d), rrow(3 * d), pl.BlockSpec((8, 3 * d), halo),
                  _const(tab.shape), _const(ck.shape), _const(wci.shape), _const(wco.shape)],
        out_specs=(rrow(d), rrow(d), rrow(3 * d), acc, acc),
        out_shape=(jax.ShapeDtypeStruct((s, d), F32), jax.ShapeDtypeStruct((s, d), BF16),
                   jax.ShapeDtypeStruct((s, 3 * d), BF16), jax.ShapeDtypeStruct((8, d), F32),
                   jax.ShapeDtypeStruct((8, d), F32)),
        scratch_shapes=[pltpu.VMEM((8, d), F32)],
        compiler_params=_params(VMEM_BIG),
    )(dx1, x, y, bcx, bcx, tab, ck, wci, wco)


def _wgrad(a, b, nblk, a_spec, b_spec, m, n, tk, name):
    s = a.shape[-2]
    nk = s // tk

    def body(a_ref, b_ref, o_ref, acc):
        kk = pl.program_id(1)

        @pl.when(kk == 0)
        def _():
            acc[...] = jnp.zeros_like(acc)

        acc[...] += lax.dot_general(a_ref[...], b_ref[...], TN, preferred_element_type=F32)

        @pl.when(kk == nk - 1)
        def _():
            o_ref[...] = acc[...].astype(BF16)

    return pl.pallas_call(
        body, name=name, grid=(nblk, nk),
        in_specs=[a_spec, b_spec],
        out_specs=pl.BlockSpec((None, m, n), lambda j, kk: (j, 0, 0)),
        out_shape=jax.ShapeDtypeStruct((nblk, m, n), BF16),
        scratch_shapes=[pltpu.VMEM((m, n), F32)],
        compiler_params=_params(VMEM_BIG),
    )(a, b)


def _wgrad_cols(a, b, nblk, tk, name):
    m, n = a.shape[1], b.shape[1] // nblk
    return _wgrad(a, b, nblk, pl.BlockSpec((tk, m), lambda j, kk: (kk, 0)),
                  pl.BlockSpec((tk, n), lambda j, kk: (kk, j)), m, n, tk, name)


def _wgrad_bstack(a, b, tk, name):
    nblk, _, n = b.shape
    m = a.shape[1]
    return _wgrad(a, b, nblk, pl.BlockSpec((tk, m), lambda j, kk: (kk, 0)),
                  pl.BlockSpec((None, tk, n), lambda j, kk: (j, kk, 0)), m, n, tk, name)


def _wgrad_astack(a, b, tk, name):
    nblk, _, m = a.shape
    n = b.shape[1]
    return _wgrad(a, b, nblk, pl.BlockSpec((None, tk, m), lambda j, kk: (j, kk, 0)),
                  pl.BlockSpec((tk, n), lambda j, kk: (kk, 0)), m, n, tk, name)


def _adamw_math(w, g, m, v):
    m = ADAM_B1 * m + (1.0 - ADAM_B1) * g
    v = ADAM_B2 * v + (1.0 - ADAM_B2) * (g * g)
    m_hat = m / (1.0 - ADAM_B1 ** ADAM_STEP)
    v_hat = v / (1.0 - ADAM_B2 ** ADAM_STEP)
    delta = -ADAM_LR * (m_hat / (jnp.sqrt(v_hat) + ADAM_EPS) + ADAM_WD * w)
    return delta, m, v


def _adamw_reduce(parts, w, m, v, tr, name):
    nl, r, c = w.shape
    tr = _row_tile(r, tr)

    def body(*refs):
        p_refs = refs[:nl]
        w_ref, m_ref, v_ref, g_ref, d_ref, mo_ref, vo_ref = refs[nl:]
        layer = pl.program_id(0)

        def partial(i):
            val = p_refs[0][i].astype(F32)
            for q in range(1, nl):
                val = jnp.where(layer == q, p_refs[q][i].astype(F32), val)
            return val

        g = partial(0)
        for i in range(1, N_DEV):
            g = g + partial(i)
        g_ref[...] = g
        d_ref[...], mo_ref[...], vo_ref[...] = _adamw_math(w_ref[...], g, m_ref[...], v_ref[...])

    blk = pl.BlockSpec((None, tr, c), lambda l, i: (l, i, 0))
    out = jax.ShapeDtypeStruct((nl, r, c), F32)
    p_specs = [pl.BlockSpec((N_DEV, tr, c), (lambda l, i, q=q: (0, jnp.where(l == q, i, 0), 0))) for q in range(nl)]
    return pl.pallas_call(
        body, name=name, grid=(nl, r // tr),
        in_specs=p_specs + [blk, blk, blk],
        out_specs=(blk,) * 4, out_shape=(out,) * 4,
        compiler_params=_params(VMEM_BIG),
    )(*parts, w, m, v)


def _adamw_outer(sct, dm, w, m, v, tr, name):
    nl, d, c = w.shape

    def body(s_ref, dm_ref, w_ref, m_ref, v_ref, g_ref, d_ref, mo_ref, vo_ref):
        g = jnp.dot(s_ref[...], dm_ref[...], preferred_element_type=F32)
        g_ref[...] = g
        d_ref[...], mo_ref[...], vo_ref[...] = _adamw_math(w_ref[...], g, m_ref[...], v_ref[...])

    blk = pl.BlockSpec((None, tr, c), lambda l, i: (l, i, 0))
    out = jax.ShapeDtypeStruct((nl, d, c), F32)
    return pl.pallas_call(
        body, name=name, grid=(nl, d // tr),
        in_specs=[pl.BlockSpec((tr, N_DEV), lambda l, i: (i, 0)),
                  pl.BlockSpec((None, N_DEV, c), lambda l, i: (l, 0, 0)), blk, blk, blk],
        out_specs=(blk,) * 4, out_shape=(out,) * 4,
        compiler_params=_params(VMEM_BIG),
    )(sct, dm, w, m, v)


def _pad_rows(a, rows):
    return jnp.concatenate([a, jnp.zeros((rows - a.shape[0],) + a.shape[1:], a.dtype)], axis=0)


def kernel(x, c, mod_w, mod_b, norm_g, ffn_w_in, ffn_w_out, conv_w_in, conv_k, conv_w_out, kv_mod_w, kv_mod_b, kv_norm_g, w_kv, attn_w_q, attn_w_o, rel_bias, loss_target, m_mod_w, m_mod_b, m_norm_g, m_ffn_w_in, m_ffn_w_out, m_conv_w_in, m_conv_k, m_conv_w_out, m_kv_mod_w, m_kv_mod_b, m_kv_norm_g, m_w_kv, m_attn_w_q, m_attn_w_o, m_rel_bias, v_mod_w, v_mod_b, v_norm_g, v_ffn_w_in, v_ffn_w_out, v_conv_w_in, v_conv_k, v_conv_w_out, v_kv_mod_w, v_kv_mod_b, v_kv_norm_g, v_w_kv, v_attn_w_q, v_attn_w_o, v_rel_bias):
    s, d = x.shape[1], x.shape[2]
    dq = d // LANES
    dsh = d // N_DEV
    nl = mod_w.shape[0]
    mw = mod_w.shape[2]
    kmw = kv_mod_w.shape[1]
    fw = ffn_w_in.shape[2]
    nh, nrel = rel_bias.shape[1], rel_bias.shape[2]
    tm = min(256, s)
    tk = min(512, s)
    me = 4 * lax.axis_index("x") + 2 * lax.axis_index("y") + lax.axis_index("c")

    x0 = x[0]
    tgt = loss_target[0]

    small1 = jnp.concatenate([c.reshape(dq, LANES), norm_g.reshape(dq, LANES),
                              _pad_rows(conv_k[0], 8).reshape(dq, LANES)], axis=0)
    (sm,) = _exchange([small1], ["gather"], "gather_small")
    c_all = sm[:, 0:dq].reshape(N_DEV, d)
    ng_full = jnp.transpose(sm[:, dq:2 * dq].reshape(N_DEV, 8, dsh), (1, 0, 2)).reshape(8, d)
    ck_full = jnp.transpose(sm[:, 2 * dq:3 * dq].reshape(N_DEV, 8, dsh), (1, 0, 2)).reshape(8, d)

    modcols, silu_c = _mod_fwd(c_all, mod_w, kv_mod_w)
    (modall,) = _exchange([modcols], ["gather"], "gather_mod")

    cast = lambda *ws: [a.astype(BF16) for a in ws]
    gath = lambda ws: (ws, ["gather"] * len(ws))
    (h_conv, h_ffn0, h_attn, h_ffn1), token = _xstart(
        [gath(cast(conv_w_in[0], conv_w_out[0])), gath(cast(ffn_w_in[0], ffn_w_out[0])),
         gath(cast(w_kv, attn_w_q[0], attn_w_o[0])), gath(cast(ffn_w_in[1], ffn_w_out[1]))],
        modall, "gather_start")
    modall = modall + token[0, 0]
    mine = lax.dynamic_index_in_dim(modall, me, axis=1, keepdims=False)
    modrow = jnp.stack([mine[:, l * mw:(l + 1) * mw].reshape(6, d) for l in range(nl)])
    kvrow = mine[:, nl * mw:nl * mw + kmw].reshape(2, d)
    tab, modval = _vec_prep(modrow, mod_b.reshape(nl, 6, d), kvrow, kv_mod_b.reshape(2, d), ng_full,
                            kv_norm_g.reshape(1, d))
    bias = _bias_fwd(rel_bias[0])

    wci, wco = _xwait(h_conv, bias, "gather_wait_conv")
    wco = wco.reshape(d, d)
    x1, h1a, bcx, ua, ya = _conv_fwd(x0, tab, ck_full, wci, wco, tm)
    wfi0, wfo0 = _xwait(h_ffn0, x1, "gather_wait_ffn0")
    wfo0 = wfo0.reshape(N_DEV // 2, -1, d)
    x2, h2a, gua, aa, y2a = _ffn_fwd(x1, tab, 0, wfi0, wfo0, None, tm, "ffn_fwd0")
    wkv, wq, wo = _xwait(h_attn, x2, "gather_wait_attn")
    wq, wo = wq.reshape(d, d), wo.reshape(d, d)
    hkv, h1b, q, k, v = _qkv_fwd(x2, tab, wq, wkv, tm)
    o, lse = _attn_fwd(q, k, v, bias)
    x3, yb = _attn_out_fwd(o, x2, tab, wo, tm)
    wfi1, wfo1 = _xwait(h_ffn1, x3, "gather_wait_ffn1")
    wfo1 = wfo1.reshape(N_DEV // 2, -1, d)
    dx4, h2b, gub, ab, y2b, loss_acc = _ffn_fwd(x3, tab, 6, wfi1, wfo1, tgt, tm, "ffn_fwd1")
    loss = lax.psum(loss_acc[0, 0] * (0.5 / d), ("x", "y", "c"))

    scat = lambda ws: [(ws, ["scatter"] * len(ws))]
    dx3, dy2b, dgub, sums_f1 = _ffn_bwd(dx4, x3, y2b, gub, tab, 6, wfi1, wfo1, tm, "ffn_bwd1")
    g_wfi1 = _wgrad_bstack(h2b, dgub, tk, "wgrad_ffn_in1")
    g_wfo1 = _wgrad_astack(ab, dy2b, tk, "wgrad_ffn_out1").reshape(N_DEV, -1, d)
    (h_g1,), token = _xstart(scat([g_wfi1, g_wfo1]), dx3, "grads_start_ffn1")
    tab = tab + token[0, 0]
    dyb, do, sums_o = _attn_out_bwd(dx3, yb, tab, wo, tm)
    g_wo = _wgrad_cols(o, dyb, 1, tk, "wgrad_o").reshape(N_DEV, dsh, d)
    dqb, dk, dv, dbias = _attn_bwd(q, k, v, do, lse, bias)
    g_wq = _wgrad_cols(h1b, dqb, 1, tk, "wgrad_q").reshape(N_DEV, dsh, d)
    dx2, dkvb, sums_q = _qkv_bwd(dx3, dqb, dk, dv, x2, tab, wq, wkv, tm)
    g_wkv = _wgrad_cols(hkv, dkvb, N_DEV, tk, "wgrad_kv")
    (h_g2,), token = _xstart(scat([g_wkv, g_wq, g_wo]), dx2, "grads_start_attn")
    tab = tab + token[0, 0]
    dx1, dy2a, dgua, sums_f0 = _ffn_bwd(dx2, x1, y2a, gua, tab, 0, wfi0, wfo0, tm, "ffn_bwd0")
    g_wfi0 = _wgrad_bstack(h2a, dgua, tk, "wgrad_ffn_in0")
    g_wfo0 = _wgrad_astack(aa, dy2a, tk, "wgrad_ffn_out0").reshape(N_DEV, -1, d)
    (h_g3,), token = _xstart(scat([g_wfi0, g_wfo0]), dx1, "grads_start_ffn0")
    tab = tab + token[0, 0]
    dx0, dya, dbcx, sums_c, dck = _conv_bwd(dx1, x0, ya, bcx, tab, ck_full, wci, wco, tm)
    drel = _bias_bwd(dbias, nrel)
    dmod, dng, dkvg = _vec_bwd(sums_c, sums_f0, sums_q, sums_o, sums_f1, modval, ng_full, kv_norm_g.reshape(1, d))

    relw = -(-nrel // LANES) * LANES
    drel_p = jnp.concatenate([drel, jnp.zeros((nh, relw - nrel), F32)], axis=1)
    small3 = jnp.concatenate([dmod.reshape(16 * dq, LANES), dng.reshape(8 * dq, LANES), dkvg.reshape(8 * dq, LANES),
                              dck.reshape(8 * dq, LANES), drel_p.reshape(nh * relw // LANES, LANES)], axis=0)
    (sm,) = _exchange([small3], ["gather"], "gather_small_grads")
    g_wci = _wgrad_cols(h1a, dbcx, N_DEV, tk, "wgrad_conv_in")
    g_wco = _wgrad_cols(ua, dya, 1, tk, "wgrad_conv_out").reshape(N_DEV, dsh, d)
    (h_g4,), token = _xstart(scat([g_wci, g_wco]), sm, "grads_start_conv")
    o1, o2, o3, o4 = 16 * dq, 24 * dq, 32 * dq, 40 * dq
    dmod_all = sm[:, 0:o1].reshape(N_DEV, 16, d)
    mine_cols = lambda a: lax.dynamic_slice_in_dim(a, me * dsh, dsh, axis=2)
    dng_parts = mine_cols(sm[:, o1:o2].reshape(N_DEV, 8, d))
    dkvg_parts = sm[:, o2:o3].reshape(N_DEV, 8, d)[:, 0:1]
    dck_parts = mine_cols(sm[:, o3:o4].reshape(N_DEV, 8, d))[:, 0:3]
    drel_parts = sm[:, o4:].reshape(N_DEV, nh, relw)[:, :, 0:nrel]

    def update(parts, w, m, v, name, layers=1):
        shp = w.shape
        w3, m3, v3 = (a.reshape(layers, -1, shp[-1]) for a in (w, m, v))
        outs = _adamw_reduce([p.reshape(N_DEV, -1, shp[-1]) for p in parts], w3, m3, v3, 256, name)
        return [a.reshape(shp) for a in outs]

    p_wfi1, p_wfo1 = _xwait(h_g1, sm, "grads_wait_ffn1")
    p_wfi0, p_wfo0 = _xwait(h_g3, p_wfi1, "grads_wait_ffn0")
    u_ffn_in = update([p_wfi0, p_wfi1], ffn_w_in, m_ffn_w_in, v_ffn_w_in, "adamw_ffn_in", 2)
    u_ffn_out = update([p_wfo0, p_wfo1], ffn_w_out, m_ffn_w_out, v_ffn_w_out, "adamw_ffn_out", 2)
    p_wkv, p_wq, p_wo = _xwait(h_g2, u_ffn_out[0], "grads_wait_attn")
    u_w_kv = update([p_wkv], w_kv, m_w_kv, v_w_kv, "adamw_w_kv")
    u_w_q = update([p_wq], attn_w_q, m_attn_w_q, v_attn_w_q, "adamw_w_q")
    u_w_o = update([p_wo], attn_w_o, m_attn_w_o, v_attn_w_o, "adamw_w_o")

    sct = jnp.transpose(silu_c)
    dm_mod = jnp.stack([lax.dynamic_slice_in_dim(dmod_all[:, 6 * l:6 * l + 6].reshape(N_DEV, 6 * d), me * mw, mw, axis=1)
                        for l in range(nl)]).astype(BF16)
    dm_kv = lax.dynamic_slice_in_dim(dmod_all[:, R_KV:R_KV + 2].reshape(N_DEV, 2 * d), me * kmw, kmw, axis=1)
    u_mod_w = _adamw_outer(sct, dm_mod, mod_w, m_mod_w, v_mod_w, min(256, d), "adamw_mod_w")
    u_kv_mod_w = [a[0] for a in _adamw_outer(sct, dm_kv.astype(BF16)[None], kv_mod_w[None], m_kv_mod_w[None],
                                             v_kv_mod_w[None], min(256, d), "adamw_kv_mod_w")]

    modb_parts = jnp.stack([dmod_all[:, 6 * l:6 * l + 6].reshape(N_DEV, 6 * d) for l in range(nl)], axis=1)
    u_mod_b = update([modb_parts], mod_b, m_mod_b, v_mod_b, "adamw_mod_b")
    u_norm_g = update([dng_parts], norm_g.reshape(8, dsh), m_norm_g.reshape(8, dsh), v_norm_g.reshape(8, dsh), "adamw_norm_g")
    u_norm_g = [a.reshape(norm_g.shape) for a in u_norm_g]
    u_conv_k = update([dck_parts], conv_k, m_conv_k, v_conv_k, "adamw_conv_k")
    kvb_parts = dmod_all[:, R_KV:R_KV + 2].reshape(N_DEV, 1, 2 * d)
    u_kv_mod_b = [a.reshape(kv_mod_b.shape) for a in update([kvb_parts], kv_mod_b.reshape(1, -1), m_kv_mod_b.reshape(1, -1),
                                                            v_kv_mod_b.reshape(1, -1), "adamw_kv_mod_b")]
    u_kv_norm_g = [a.reshape(kv_norm_g.shape) for a in update([dkvg_parts], kv_norm_g.reshape(1, -1), m_kv_norm_g.reshape(1, -1),
                                                              v_kv_norm_g.reshape(1, -1), "adamw_kv_norm_g")]
    u_rel = update([drel_parts], rel_bias, m_rel_bias, v_rel_bias, "adamw_rel_bias")

    p_wci, p_wco = _xwait(h_g4, u_mod_w[0], "grads_wait_conv")
    u_conv_in = update([p_wci], conv_w_in, m_conv_w_in, v_conv_w_in, "adamw_conv_in")
    u_conv_out = update([p_wco], conv_w_out, m_conv_w_out, v_conv_w_out, "adamw_conv_out")

    ups = [u_mod_w, u_mod_b, u_norm_g, u_ffn_in, u_ffn_out, u_conv_in, u_conv_k, u_conv_out, u_kv_mod_w, u_kv_mod_b,
           u_kv_norm_g, u_w_kv, u_w_q, u_w_o, u_rel]
    return (loss, dx0[None], *[u[0] for u in ups], *[u[1] for u in ups], *[u[2] for u in ups], *[u[3] for u in ups])
```

```python
import functools

import jax
import jax.numpy as jnp
from jax import lax
from jax.experimental import pallas as pl
from jax.experimental.pallas import tpu as pltpu

F32 = jnp.float32
BF16 = jnp.bfloat16

EPS = 1e-6
CHUNK = 64
HEAD_DIM = 64
N_LEFT = 8
LANES = 128
QB = 4 * CHUNK
KW = QB + N_LEFT * CHUNK
NEG = -1e30
N_DEV = 8

ADAM_LR = 0.001
ADAM_B1 = 0.9
ADAM_B2 = 0.999
ADAM_EPS = 1e-08
ADAM_WD = 0.01
ADAM_STEP = 10

VMEM_BIG = 56 * 1024 * 1024

NT = (((1,), (1,)), ((), ()))
TN = (((0,), (0,)), ((), ()))

R_W1, R_SH1, R_P1, R_W2, R_SH2, R_P2 = range(6)
R_KV = 12


def _params(vmem):
    return pltpu.CompilerParams(vmem_limit_bytes=vmem)


def _row_tile(rows, cap):
    for t in range(min(cap, rows) // 16 * 16, 0, -16):
        if rows % t == 0:
            return t
    return rows


def _rows(tm, cols):
    return pl.BlockSpec((tm, cols), lambda i: (i, 0))


def _const(shape):
    nd = len(shape)
    return pl.BlockSpec(shape, lambda *_: (0,) * nd, pipeline_mode=pl.Buffered(1))


def _rs(x):
    return lax.rsqrt(jnp.mean(x * x, axis=-1, keepdims=True) + EPS)


def _norm_bwd(d, n, r):
    return r * (d - n * jnp.mean(d * n, axis=-1, keepdims=True))


def _colsum(a):
    return jnp.sum(a, axis=0, keepdims=True)


def _sigmoid(g):
    return 1.0 / (1.0 + jnp.exp(-g))


def _exchange(arrays, modes, name):
    n = len(arrays)
    out_shape = []
    for a, mode in zip(arrays, modes):
        shp = (N_DEV,) + a.shape if mode == "gather" else a.shape
        out_shape.append(jax.ShapeDtypeStruct(shp, a.dtype))

    def body(*refs):
        ins, outs = refs[:n], refs[n:2 * n]
        send_sems, recv_sems, local_sems = refs[2 * n:]
        x, y, c = lax.axis_index("x"), lax.axis_index("y"), lax.axis_index("c")
        me = 4 * x + 2 * y + c
        local, sends, recvs = [], [], []
        for a in range(n):
            own = ins[a] if modes[a] == "gather" else ins[a].at[me]
            cp = pltpu.make_async_copy(own, outs[a].at[me], local_sems.at[a])
            cp.start()
            local.append(cp)
        for k in range(1, N_DEV):
            px = 1 - x if k & 4 else x
            py = 1 - y if k & 2 else y
            pc = 1 - c if k & 1 else c
            peer = 4 * px + 2 * py + pc
            for a in range(n):
                src = ins[a] if modes[a] == "gather" else ins[a].at[peer]
                sem = a * (N_DEV - 1) + k - 1
                cp = pltpu.make_async_remote_copy(
                    src_ref=src, dst_ref=outs[a].at[me],
                    send_sem=send_sems.at[sem], recv_sem=recv_sems.at[sem],
                    device_id=(px, py, pc), device_id_type=pl.DeviceIdType.MESH)
                cp.start()
                sends.append(cp)
                recvs.append(pltpu.make_async_remote_copy(
                    src_ref=src, dst_ref=outs[a].at[peer],
                    send_sem=send_sems.at[sem], recv_sem=recv_sems.at[sem],
                    device_id=(px, py, pc), device_id_type=pl.DeviceIdType.MESH))
        for cp in recvs:
            cp.wait_recv()
        for cp in sends:
            cp.wait_send()
        for cp in local:
            cp.wait()

    any_spec = pl.BlockSpec(memory_space=pl.ANY)
    return pl.pallas_call(
        body, name=name,
        out_shape=tuple(out_shape),
        in_specs=[any_spec] * n,
        out_specs=tuple([any_spec] * n),
        scratch_shapes=[
            pltpu.SemaphoreType.DMA((n * (N_DEV - 1),)),
            pltpu.SemaphoreType.DMA((n * (N_DEV - 1),)),
            pltpu.SemaphoreType.DMA((n,)),
        ],
    )(*arrays)


def _peers(x, y, c):
    out = []
    for k in range(1, N_DEV):
        px = 1 - x if k & 4 else x
        py = 1 - y if k & 2 else y
        pc = 1 - c if k & 1 else c
        out.append((k - 1, (px, py, pc), 4 * px + 2 * py + pc))
    return out


def _land_shape(a, mode):
    return (N_DEV,) + a.shape if mode == "gather" else a.shape


_HBM = pl.BlockSpec(memory_space=pltpu.HBM)
_SEM = pl.BlockSpec(memory_space=pltpu.SEMAPHORE)
_EFFECT = pltpu.SideEffectType.DATAFLOW_SIDE_EFFECTING


def _xstart(groups, after, name):
    flat = [(a, m) for arrays, modes in groups for a, m in zip(arrays, modes)]
    n, ngr = len(flat), len(groups)
    sizes = [len(arrays) for arrays, _ in groups]
    npeer = N_DEV - 1

    def body(*refs):
        ins, lands = refs[:n], refs[n:2 * n]
        outs = refs[2 * n + 1:]
        sems = outs[:2 * ngr]
        token = outs[2 * ngr + 2 * n]
        local_sems = outs[2 * ngr + 2 * n + 1]
        stage = outs[2 * ngr + 2 * n + 2:]
        x, y, c = lax.axis_index("x"), lax.axis_index("y"), lax.axis_index("c")
        me = 4 * x + 2 * y + c
        loads, stores = [], []
        for a in range(n):
            own = ins[a] if flat[a][1] == "gather" else ins[a].at[me]
            loads.append(pltpu.make_async_copy(own, stage[a], local_sems.at[a]))
            stores.append(pltpu.make_async_copy(stage[a], lands[a].at[me], local_sems.at[a]))
            loads[a].start()
        for a in range(n):
            loads[a].wait()
            stores[a].start()
        a = 0
        for g in range(ngr):
            for j in range(sizes[g]):
                mode = flat[a][1]
                for slot, peer, pidx in _peers(x, y, c):
                    pltpu.make_async_remote_copy(
                        src_ref=ins[a] if mode == "gather" else ins[a].at[pidx], dst_ref=lands[a].at[me],
                        send_sem=sems[2 * g].at[j * npeer + slot], recv_sem=sems[2 * g + 1].at[j * npeer + slot],
                        device_id=peer, device_id_type=pl.DeviceIdType.MESH).start()
                a += 1
        for cp in stores:
            cp.wait()
        token[...] = jnp.zeros_like(token)

    out_shape, out_specs = [], []
    for sz in sizes:
        out_shape += [pltpu.SemaphoreType.DMA((sz * npeer,)), pltpu.SemaphoreType.DMA((sz * npeer,))]
        out_specs += [_SEM, _SEM]
    out_shape += [pltpu.HBM(a.shape, a.dtype) for a, _ in flat]
    out_shape += [pltpu.HBM(_land_shape(a, m), a.dtype) for a, m in flat]
    out_specs += [_HBM] * (2 * n)
    out_shape.append(jax.ShapeDtypeStruct((8, LANES), F32))
    out_specs.append(pl.BlockSpec(memory_space=pltpu.VMEM))
    args = [pltpu.with_memory_space_constraint(a, pltpu.HBM) for a, _ in flat]
    args += [pltpu.with_memory_space_constraint(lax.empty(_land_shape(a, m), a.dtype), pltpu.HBM) for a, m in flat]
    res = pl.pallas_call(
        body, name=name, out_shape=tuple(out_shape),
        in_specs=[_HBM] * (2 * n) + [pl.BlockSpec(memory_space=pl.ANY)], out_specs=tuple(out_specs),
        input_output_aliases={i: 2 * ngr + i for i in range(2 * n)},
        scratch_shapes=[pltpu.SemaphoreType.DMA((n,))]
                       + [pltpu.VMEM(a.shape if m == "gather" else a.shape[1:], a.dtype) for a, m in flat],
        compiler_params=pltpu.CompilerParams(has_side_effects=_EFFECT, vmem_limit_bytes=VMEM_BIG),
    )(*args, after)
    handles, a = [], 0
    for g, sz in enumerate(sizes):
        handles.append((res[2 * g], res[2 * g + 1], list(res[2 * ngr + a:2 * ngr + a + sz]),
                        list(res[2 * ngr + n + a:2 * ngr + n + a + sz]), list(groups[g][1])))
        a += sz
    return handles, res[-1]


def _xwait(handle, after, name):
    send_sems, recv_sems, srcs, lands, modes = handle
    m = len(srcs)
    npeer = N_DEV - 1

    def body(*refs):
        ins, lnd = refs[:m], refs[m:2 * m]
        ssem, rsem = refs[2 * m], refs[2 * m + 1]
        x, y, c = lax.axis_index("x"), lax.axis_index("y"), lax.axis_index("c")
        for j in range(m):
            for slot, peer, pidx in _peers(x, y, c):
                cp = pltpu.make_async_remote_copy(
                    src_ref=ins[j] if modes[j] == "gather" else ins[j].at[pidx], dst_ref=lnd[j].at[pidx],
                    send_sem=ssem.at[j * npeer + slot], recv_sem=rsem.at[j * npeer + slot],
                    device_id=peer, device_id_type=pl.DeviceIdType.MESH)
                cp.wait_send()
                cp.wait_recv()

    res = pl.pallas_call(
        body, name=name,
        out_shape=tuple([pltpu.HBM(a.shape, a.dtype) for a in srcs] + [pltpu.HBM(a.shape, a.dtype) for a in lands]),
        in_specs=[_HBM] * (2 * m) + [_SEM, _SEM, pl.BlockSpec(memory_space=pl.ANY)],
        out_specs=tuple([_HBM] * (2 * m)),
        input_output_aliases={i: i for i in range(2 * m)},
        compiler_params=pltpu.CompilerParams(has_side_effects=_EFFECT),
    )(*srcs, *lands, send_sems, recv_sems, after)
    return list(res[m:])


def _mod_fwd(c_all, mod_w, kv_mod_w):
    nl, d, mw = mod_w.shape
    kw = kv_mod_w.shape[1]

    def body(c_ref, mw_ref, kw_ref, o_ref, sc_ref):
        cc = c_ref[...]
        sc = (cc * _sigmoid(cc)).astype(BF16)
        sc_ref[...] = sc
        for l in range(nl):
            o_ref[:, l * mw:(l + 1) * mw] = jnp.dot(sc, mw_ref[l].astype(BF16), preferred_element_type=F32)
        o_ref[:, nl * mw:nl * mw + kw] = jnp.dot(sc, kw_ref[...].astype(BF16), preferred_element_type=F32)

    return pl.pallas_call(
        body, name="mod_fwd",
        out_shape=(jax.ShapeDtypeStruct((c_all.shape[0], nl * mw + kw), F32),
                   jax.ShapeDtypeStruct(c_all.shape, BF16)),
        compiler_params=_params(VMEM_BIG),
    )(c_all, mod_w, kv_mod_w)


def _vec_prep(modrow, modb, kvrow, kvb, ng, kvg):
    d = ng.shape[1]

    def body(mr_ref, mb_ref, kr_ref, kb_ref, ng_ref, kvg_ref, t_ref, m_ref):
        t_ref[...] = jnp.zeros_like(t_ref)
        m_ref[...] = jnp.zeros_like(m_ref)
        for l in range(2):
            mod = mr_ref[l] + mb_ref[l]
            m_ref[6 * l:6 * l + 6, :] = mod
            g = ng_ref[4 * l:4 * l + 4, :]
            t_ref[6 * l + R_W1:6 * l + R_W1 + 1, :] = g[0:1] * (1.0 + mod[1:2])
            t_ref[6 * l + R_SH1:6 * l + R_SH1 + 1, :] = mod[0:1]
            t_ref[6 * l + R_P1:6 * l + R_P1 + 1, :] = mod[2:3] * g[1:2]
            t_ref[6 * l + R_W2:6 * l + R_W2 + 1, :] = g[2:3] * (1.0 + mod[4:5])
            t_ref[6 * l + R_SH2:6 * l + R_SH2 + 1, :] = mod[3:4]
            t_ref[6 * l + R_P2:6 * l + R_P2 + 1, :] = mod[5:6] * g[3:4]
        kv = kr_ref[...] + kb_ref[...]
        m_ref[R_KV:R_KV + 2, :] = kv
        t_ref[R_KV:R_KV + 1, :] = kvg_ref[...] * (1.0 + kv[1:2])
        t_ref[R_KV + 1:R_KV + 2, :] = kv[0:1]

    return pl.pallas_call(
        body, name="vec_prep",
        out_shape=(jax.ShapeDtypeStruct((16, d), F32), jax.ShapeDtypeStruct((16, d), F32)),
    )(modrow, modb, kvrow, kvb, ng, kvg)


def _vec_bwd(sums_c, sums_f0, sums_q, sums_o, sums_f1, mt, ng, kvg):
    d = ng.shape[1]

    def body(sc_ref, sf0_ref, sq_ref, so_ref, sf1_ref, m_ref, ng_ref, kvg_ref, dm_ref, dng_ref, dkvg_ref, g_ref):
        g_ref[...] = jnp.zeros_like(g_ref)
        g_ref[0:3, :] = sc_ref[0:3, :]
        g_ref[3:6, :] = sf0_ref[3:6, :]
        g_ref[6:8, :] = sq_ref[0:2, :]
        g_ref[8:9, :] = so_ref[2:3, :]
        g_ref[9:12, :] = sf1_ref[3:6, :]
        g_ref[R_KV:R_KV + 2, :] = sq_ref[2:4, :]
        dm_ref[...] = jnp.zeros_like(dm_ref)
        dkvg_ref[...] = jnp.zeros_like(dkvg_ref)
        for l in range(2):
            g = ng_ref[4 * l:4 * l + 4, :]
            mod = m_ref[6 * l:6 * l + 6, :]
            s = g_ref[6 * l:6 * l + 6, :]
            dm_ref[6 * l + 0:6 * l + 1, :] = s[1:2]
            dm_ref[6 * l + 1:6 * l + 2, :] = s[0:1] * g[0:1]
            dm_ref[6 * l + 2:6 * l + 3, :] = s[2:3] * g[1:2]
            dm_ref[6 * l + 3:6 * l + 4, :] = s[4:5]
            dm_ref[6 * l + 4:6 * l + 5, :] = s[3:4] * g[2:3]
            dm_ref[6 * l + 5:6 * l + 6, :] = s[5:6] * g[3:4]
            dng_ref[4 * l + 0:4 * l + 1, :] = s[0:1] * (1.0 + mod[1:2])
            dng_ref[4 * l + 1:4 * l + 2, :] = s[2:3] * mod[2:3]
            dng_ref[4 * l + 2:4 * l + 3, :] = s[3:4] * (1.0 + mod[4:5])
            dng_ref[4 * l + 3:4 * l + 4, :] = s[5:6] * mod[5:6]
        dm_ref[R_KV:R_KV + 1, :] = g_ref[R_KV + 1:R_KV + 2, :]
        dm_ref[R_KV + 1:R_KV + 2, :] = g_ref[R_KV:R_KV + 1, :] * kvg_ref[...]
        dkvg_ref[0:1, :] = g_ref[R_KV:R_KV + 1, :] * (1.0 + m_ref[R_KV + 1:R_KV + 2, :])

    return pl.pallas_call(
        body, name="vec_bwd",
        out_shape=(jax.ShapeDtypeStruct((16, d), F32), jax.ShapeDtypeStruct((8, d), F32),
                   jax.ShapeDtypeStruct((8, d), F32)),
        scratch_shapes=[pltpu.VMEM((16, d), F32)],
    )(sums_c, sums_f0, sums_q, sums_o, sums_f1, mt, ng, kvg)


def _rel_index(nrel):
    width = KW + QB
    e = lax.broadcasted_iota(jnp.int32, (nrel, width), 1)
    r = lax.broadcasted_iota(jnp.int32, (nrel, width), 0)
    max_rel = (nrel - 1) // 2
    idx = jnp.clip(KW - e, -max_rel, max_rel) + max_rel
    return (idx == r).astype(F32)


def _band_valid():
    row = lax.broadcasted_iota(jnp.int32, (QB, KW), 0) // CHUNK
    col = lax.broadcasted_iota(jnp.int32, (QB, KW), 1) // CHUNK
    j = col - row
    return (j >= 0) & (j <= N_LEFT)


def _bias_fwd(rel_bias):
    nh, nrel = rel_bias.shape
    width = KW + QB

    def body(rb_ref, o_ref):
        onehot = _rel_index(nrel)
        gr = jnp.dot(rb_ref[...], onehot, preferred_element_type=F32, precision=lax.Precision.HIGHEST)
        valid = _band_valid()
        for h in range(nh):
            xrow = jnp.broadcast_to(gr[h:h + 1, :], (QB, width))
            rolled = pltpu.roll(xrow, 0, 1, stride=1, stride_axis=0)
            o_ref[h] = jnp.where(valid, rolled[:, QB:], NEG)

    return pl.pallas_call(
        body, name="bias_fwd",
        out_shape=jax.ShapeDtypeStruct((nh, QB, KW), F32),
        compiler_params=_params(VMEM_BIG),
    )(rel_bias)


def _bias_bwd(dbias, nrel):
    nh = dbias.shape[0]
    width = KW + QB

    def body(db_ref, o_ref, diag_ref):
        onehot = _rel_index(nrel)
        valid = _band_valid()
        rr = lax.broadcasted_iota(jnp.int32, (QB, QB), 0)
        cc = lax.broadcasted_iota(jnp.int32, (QB, QB), 1)
        flip = (rr + cc == QB - 1).astype(F32)
        for h in range(nh):
            rev = jnp.dot(flip, jnp.where(valid, db_ref[h], 0.0), preferred_element_type=F32,
                          precision=lax.Precision.HIGHEST)
            w = jnp.concatenate([jnp.zeros((QB, QB), F32), rev], axis=1)
            back = pltpu.roll(w, width - (QB - 1), 1, stride=1, stride_axis=0)
            diag_ref[h:h + 1, :] = _colsum(back)
        o_ref[...] = lax.dot_general(diag_ref[...], onehot, NT, preferred_element_type=F32,
                                     precision=lax.Precision.HIGHEST)

    return pl.pallas_call(
        body, name="bias_bwd",
        out_shape=jax.ShapeDtypeStruct((nh, nrel), F32),
        scratch_shapes=[pltpu.VMEM((nh, width), F32)],
        compiler_params=_params(VMEM_BIG),
    )(dbias)


def _conv_fwd(x, tab, ck, wci, wco, tm):
    s, d = x.shape
    nsh, _, cw = wci.shape

    def body(x_ref, t_ref, ck_ref, wci_ref, wco_ref, x1_ref, h_ref, bcx_ref, u_ref, y_ref, carry):
        @pl.when(pl.program_id(0) == 0)
        def _():
            carry[...] = jnp.zeros_like(carry)

        xv = x_ref[...]
        hb = ((xv * _rs(xv)) * t_ref[R_W1:R_W1 + 1, :] + t_ref[R_SH1:R_SH1 + 1, :]).astype(BF16)
        h_ref[...] = hb
        for j in range(nsh):
            bcx_ref[:, j * cw:(j + 1) * cw] = jnp.dot(hb, wci_ref[j], preferred_element_type=F32)
        bg, cg, xi = bcx_ref[:, 0:d], bcx_ref[:, d:2 * d], bcx_ref[:, 2 * d:3 * d]
        z = cg * xi
        row = lax.broadcasted_iota(jnp.int32, z.shape, 0)
        c1, c2 = carry[7:8, :], carry[6:7, :]
        z1 = jnp.where(row == 0, c1, pltpu.roll(z, 1, 0))
        z2 = jnp.where(row == 0, c2, jnp.where(row == 1, c1, pltpu.roll(z, 2, 0)))
        carry[...] = z[tm - 8:tm, :]
        conv = ck_ref[0:1, :] * z2 + ck_ref[1:2, :] * z1 + ck_ref[2:3, :] * z
        ub = (bg * conv).astype(BF16)
        u_ref[...] = ub
        yv = jnp.dot(ub, wco_ref[...], preferred_element_type=F32)
        y_ref[...] = yv
        x1_ref[...] = xv + (yv * _rs(yv)) * t_ref[R_P1:R_P1 + 1, :]

    return pl.pallas_call(
        body, name="conv_fwd", grid=(s // tm,),
        in_specs=[_rows(tm, d), _const(tab.shape), _const(ck.shape), _const(wci.shape), _const(wco.shape)],
        out_specs=(_rows(tm, d), _rows(tm, d), _rows(tm, 3 * d), _rows(tm, d), _rows(tm, d)),
        out_shape=(jax.ShapeDtypeStruct((s, d), F32), jax.ShapeDtypeStruct((s, d), BF16),
                   jax.ShapeDtypeStruct((s, 3 * d), F32), jax.ShapeDtypeStruct((s, d), BF16),
                   jax.ShapeDtypeStruct((s, d), F32)),
        scratch_shapes=[pltpu.VMEM((8, d), F32)],
        compiler_params=_params(VMEM_BIG),
    )(x, tab, ck, wci, wco)


def _ffn_fwd(x, tab, base, wfi, wfo, tgt, tm, name):
    s, d = x.shape
    nsh, _, fw = wfi.shape
    nh = nsh // 2
    with_loss = tgt is not None

    def body(*refs):
        if with_loss:
            x_ref, t_ref, wfi_ref, wfo_ref, tgt_ref, xo_ref, h_ref, gu_ref, a_ref, y_ref, loss_ref = refs
        else:
            x_ref, t_ref, wfi_ref, wfo_ref, xo_ref, h_ref, gu_ref, a_ref, y_ref = refs
        xv = x_ref[...]
        hb = ((xv * _rs(xv)) * t_ref[base + R_W2:base + R_W2 + 1, :]
              + t_ref[base + R_SH2:base + R_SH2 + 1, :]).astype(BF16)
        h_ref[...] = hb
        acc = jnp.zeros((tm, d), F32)
        for j in range(nh):
            g = jnp.dot(hb, wfi_ref[j], preferred_element_type=F32)
            u = jnp.dot(hb, wfi_ref[j + nh], preferred_element_type=F32)
            gu_ref[j] = g
            gu_ref[j + nh] = u
            ab = ((g * _sigmoid(g)) * u).astype(BF16)
            a_ref[j] = ab
            acc = acc + jnp.dot(ab, wfo_ref[j], preferred_element_type=F32)
        y_ref[...] = acc
        xo = xv + (acc * _rs(acc)) * t_ref[base + R_P2:base + R_P2 + 1, :]
        if with_loss:
            @pl.when(pl.program_id(0) == 0)
            def _():
                loss_ref[...] = jnp.zeros_like(loss_ref)

            err = xo - tgt_ref[...]
            xo_ref[...] = err * (1.0 / d)
            loss_ref[...] += jnp.sum(err * err)
        else:
            xo_ref[...] = xo

    in_specs = [_rows(tm, d), _const(tab.shape), _const(wfi.shape), _const(wfo.shape)]
    args = [x, tab, wfi, wfo]
    out_specs = [_rows(tm, d), _rows(tm, d), pl.BlockSpec((nsh, tm, fw), lambda i: (0, i, 0)),
                 pl.BlockSpec((nh, tm, fw), lambda i: (0, i, 0)), _rows(tm, d)]
    out_shape = [jax.ShapeDtypeStruct((s, d), F32), jax.ShapeDtypeStruct((s, d), BF16),
                 jax.ShapeDtypeStruct((nsh, s, fw), F32), jax.ShapeDtypeStruct((nh, s, fw), BF16),
                 jax.ShapeDtypeStruct((s, d), F32)]
    if with_loss:
        in_specs.append(_rows(tm, d))
        args.append(tgt)
        out_specs.append(pl.BlockSpec((8, LANES), lambda i: (0, 0)))
        out_shape.append(jax.ShapeDtypeStruct((8, LANES), F32))
    return pl.pallas_call(
        body, name=name, grid=(s // tm,), in_specs=in_specs, out_specs=tuple(out_specs),
        out_shape=tuple(out_shape), compiler_params=_params(VMEM_BIG),
    )(*args)


def _qkv_fwd(x, tab, wq, wkv, tm):
    s, d = x.shape
    nsh, _, kw = wkv.shape
    nh = nsh // 2
    base = 6

    def body(x_ref, t_ref, wq_ref, wkv_ref, hkv_ref, h1_ref, q_ref, k_ref, v_ref):
        xv = x_ref[...]
        n = xv * _rs(xv)
        hkv = (n * t_ref[R_KV:R_KV + 1, :] + t_ref[R_KV + 1:R_KV + 2, :]).astype(BF16)
        h1 = (n * t_ref[base + R_W1:base + R_W1 + 1, :] + t_ref[base + R_SH1:base + R_SH1 + 1, :]).astype(BF16)
        hkv_ref[...] = hkv
        h1_ref[...] = h1
        q_ref[...] = (jnp.dot(h1, wq_ref[...], preferred_element_type=F32) * (HEAD_DIM ** -0.5)).astype(BF16)
        for j in range(nh):
            k_ref[:, j * kw:(j + 1) * kw] = jnp.dot(hkv, wkv_ref[j], preferred_element_type=F32).astype(BF16)
            v_ref[:, j * kw:(j + 1) * kw] = jnp.dot(hkv, wkv_ref[j + nh], preferred_element_type=F32).astype(BF16)

    act = jax.ShapeDtypeStruct((s, d), BF16)
    return pl.pallas_call(
        body, name="qkv_fwd", grid=(s // tm,),
        in_specs=[_rows(tm, d), _const(tab.shape), _const(wq.shape), _const(wkv.shape)],
        out_specs=tuple([_rows(tm, d)] * 5), out_shape=(act,) * 5,
        compiler_params=_params(VMEM_BIG),
    )(x, tab, wq, wkv)


def _window_specs():
    return [pl.BlockSpec((QB, LANES), (lambda p, b, w=w: (jnp.maximum(b - 2 + w, 0), p))) for w in range(3)]


def _key_valid(b):
    col = lax.broadcasted_iota(jnp.int32, (QB, KW), 1) // CHUNK
    return (b * (QB // CHUNK) - N_LEFT + col) >= 0


def _head_masks():
    lane = lax.broadcasted_iota(jnp.int32, (1, LANES), 1)
    return [(lane // HEAD_DIM == hh) for hh in range(LANES // HEAD_DIM)]


def _attn_fwd(q, k, v, bias):
    s, d = q.shape
    npair, nb = d // LANES, s // QB
    hpp = LANES // HEAD_DIM

    def body(q_ref, k0, k1, k2, v0, v1, v2, bias_ref, o_ref, lse_ref):
        b = pl.program_id(1)
        qv = q_ref[...]
        kwin = jnp.concatenate([k0[...], k1[...], k2[...]], axis=0)
        vwin = jnp.concatenate([v0[...], v1[...], v2[...]], axis=0)
        valid = _key_valid(b)
        masks = _head_masks()
        o = jnp.zeros((QB, LANES), F32)
        lse = jnp.zeros((QB, LANES), F32)
        for hh in range(hpp):
            qm = jnp.where(masks[hh], qv, jnp.zeros_like(qv))
            vm = jnp.where(masks[hh], vwin, jnp.zeros_like(vwin))
            sc = lax.dot_general(qm, kwin, NT, preferred_element_type=F32) + bias_ref[hh]
            sc = jnp.where(valid, sc, NEG)
            m = jnp.max(sc, axis=-1, keepdims=True)
            p = jnp.exp(sc - m)
            l = jnp.sum(p, axis=-1, keepdims=True)
            o = o + jnp.dot(p.astype(BF16), vm, preferred_element_type=F32) * (1.0 / l)
            lse = jnp.where(masks[hh], m + jnp.log(l), lse)
        o_ref[...] = o.astype(BF16)
        lse_ref[...] = lse

    blk = pl.BlockSpec((QB, LANES), lambda p, b: (b, p))
    return pl.pallas_call(
        body, name="attn_fwd", grid=(npair, nb),
        in_specs=[blk] + _window_specs() + _window_specs()
                 + [pl.BlockSpec((hpp, QB, KW), lambda p, b: (p, 0, 0))],
        out_specs=(blk, blk),
        out_shape=(jax.ShapeDtypeStruct((s, d), BF16), jax.ShapeDtypeStruct((s, d), F32)),
        compiler_params=_params(VMEM_BIG),
    )(q, k, k, k, v, v, v, bias)


def _attn_out_fwd(o, x, tab, wo, tm):
    s, d = x.shape
    base = 6

    def body(o_ref, x_ref, t_ref, wo_ref, x3_ref, y_ref):
        yv = jnp.dot(o_ref[...], wo_ref[...], preferred_element_type=F32)
        y_ref[...] = yv
        x3_ref[...] = x_ref[...] + (yv * _rs(yv)) * t_ref[base + R_P1:base + R_P1 + 1, :]

    return pl.pallas_call(
        body, name="attn_out_fwd", grid=(s // tm,),
        in_specs=[_rows(tm, d), _rows(tm, d), _const(tab.shape), _const(wo.shape)],
        out_specs=(_rows(tm, d), _rows(tm, d)),
        out_shape=(jax.ShapeDtypeStruct((s, d), F32), jax.ShapeDtypeStruct((s, d), F32)),
        compiler_params=_params(VMEM_BIG),
    )(o, x, tab, wo)


def _ffn_bwd(dxo, x, y, gu, tab, base, wfi, wfo, tm, name):
    s, d = x.shape
    nsh, _, fw = wfi.shape
    nh = nsh // 2

    def body(dxo_ref, x_ref, y_ref, gu_ref, t_ref, wfi_ref, wfo_ref, dx_ref, dyb_ref, dgu_ref, sums_ref):
        @pl.when(pl.program_id(0) == 0)
        def _():
            sums_ref[...] = jnp.zeros_like(sums_ref)

        dxo_v = dxo_ref[...]
        yv = y_ref[...]
        ry = _rs(yv)
        ny = yv * ry
        sums_ref[R_P2:R_P2 + 1, :] += _colsum(dxo_v * ny)
        dyb = _norm_bwd(dxo_v * t_ref[base + R_P2:base + R_P2 + 1, :], ny, ry).astype(BF16)
        dyb_ref[...] = dyb
        dh = jnp.zeros((tm, d), F32)
        for j in range(nh):
            da = lax.dot_general(dyb, wfo_ref[j], NT, preferred_element_type=F32)
            g, u = gu_ref[j], gu_ref[j + nh]
            sg = _sigmoid(g)
            dg = (da * u * sg * (1.0 + g * (1.0 - sg))).astype(BF16)
            du = (da * g * sg).astype(BF16)
            dgu_ref[j] = dg
            dgu_ref[j + nh] = du
            dh = dh + lax.dot_general(dg, wfi_ref[j], NT, preferred_element_type=F32)
            dh = dh + lax.dot_general(du, wfi_ref[j + nh], NT, preferred_element_type=F32)
        xv = x_ref[...]
        r = _rs(xv)
        n = xv * r
        sums_ref[R_SH2:R_SH2 + 1, :] += _colsum(dh)
        sums_ref[R_W2:R_W2 + 1, :] += _colsum(dh * n)
        dx_ref[...] = dxo_v + _norm_bwd(dh * t_ref[base + R_W2:base + R_W2 + 1, :], n, r)

    return pl.pallas_call(
        body, name=name, grid=(s // tm,),
        in_specs=[_rows(tm, d), _rows(tm, d), _rows(tm, d), pl.BlockSpec((nsh, tm, fw), lambda i: (0, i, 0)),
                  _const(tab.shape), _const(wfi.shape), _const(wfo.shape)],
        out_specs=(_rows(tm, d), _rows(tm, d), pl.BlockSpec((nsh, tm, fw), lambda i: (0, i, 0)),
                   pl.BlockSpec((8, d), lambda i: (0, 0))),
        out_shape=(jax.ShapeDtypeStruct((s, d), F32), jax.ShapeDtypeStruct((s, d), BF16),
                   jax.ShapeDtypeStruct((nsh, s, fw), BF16), jax.ShapeDtypeStruct((8, d), F32)),
        compiler_params=_params(VMEM_BIG),
    )(dxo, x, y, gu, tab, wfi, wfo)


def _attn_out_bwd(dx, y, tab, wo, tm):
    s, d = y.shape
    base = 6

    def body(dx_ref, y_ref, t_ref, wo_ref, dyb_ref, do_ref, sums_ref):
        @pl.when(pl.program_id(0) == 0)
        def _():
            sums_ref[...] = jnp.zeros_like(sums_ref)

        dxv = dx_ref[...]
        yv = y_ref[...]
        ry = _rs(yv)
        ny = yv * ry
        sums_ref[R_P1:R_P1 + 1, :] += _colsum(dxv * ny)
        dyb = _norm_bwd(dxv * t_ref[base + R_P1:base + R_P1 + 1, :], ny, ry).astype(BF16)
        dyb_ref[...] = dyb
        do_ref[...] = lax.dot_general(dyb, wo_ref[...], NT, preferred_element_type=F32).astype(BF16)

    return pl.pallas_call(
        body, name="attn_out_bwd", grid=(s // tm,),
        in_specs=[_rows(tm, d), _rows(tm, d), _const(tab.shape), _const(wo.shape)],
        out_specs=(_rows(tm, d), _rows(tm, d), pl.BlockSpec((8, d), lambda i: (0, 0))),
        out_shape=(jax.ShapeDtypeStruct((s, d), BF16), jax.ShapeDtypeStruct((s, d), BF16),
                   jax.ShapeDtypeStruct((8, d), F32)),
        compiler_params=_params(VMEM_BIG),
    )(dx, y, tab, wo)


def _attn_bwd(q, k, v, do, lse, bias):
    s, d = q.shape
    npair, nb = d // LANES, s // QB
    hpp = LANES // HEAD_DIM

    def body(q_ref, k0, k1, k2, v0, v1, v2, do_ref, lse_ref, bias_ref, dq_ref, dk_ref, dv_ref, db_ref):
        b = pl.program_id(1)

        @pl.when(b == 0)
        def _():
            dk_ref[...] = jnp.zeros_like(dk_ref)
            dv_ref[...] = jnp.zeros_like(dv_ref)
            db_ref[...] = jnp.zeros_like(db_ref)

        qv = q_ref[...]
        dov = do_ref[...]
        lsev = lse_ref[...]
        kwin = jnp.concatenate([k0[...], k1[...], k2[...]], axis=0)
        vwin = jnp.concatenate([v0[...], v1[...], v2[...]], axis=0)
        valid = _key_valid(b)
        masks = _head_masks()
        dq = jnp.zeros((QB, LANES), F32)
        dkw = jnp.zeros((KW, LANES), F32)
        dvw = jnp.zeros((KW, LANES), F32)
        for hh in range(hpp):
            qm = jnp.where(masks[hh], qv, jnp.zeros_like(qv))
            dom = jnp.where(masks[hh], dov, jnp.zeros_like(dov))
            km = jnp.where(masks[hh], kwin, jnp.zeros_like(kwin))
            lse_h = jnp.max(jnp.where(masks[hh], lsev, NEG), axis=-1, keepdims=True)
            sc = lax.dot_general(qm, kwin, NT, preferred_element_type=F32) + bias_ref[hh]
            sc = jnp.where(valid, sc, NEG)
            p = jnp.exp(sc - lse_h)
            dp = lax.dot_general(dom, vwin, NT, preferred_element_type=F32)
            ds = p * (dp - jnp.sum(dp * p, axis=-1, keepdims=True))
            db_ref[hh] += ds
            dsb = ds.astype(BF16)
            dq = dq + jnp.dot(dsb, km, preferred_element_type=F32)
            dkw = dkw + lax.dot_general(dsb, qm, TN, preferred_element_type=F32)
            dvw = dvw + lax.dot_general(p.astype(BF16), dom, TN, preferred_element_type=F32)
        dq_ref[...] = (dq * (HEAD_DIM ** -0.5)).astype(BF16)
        for w in range(3):
            start = pl.multiple_of(jnp.maximum(b - 2 + w, 0) * QB, QB)
            dk_ref[pl.ds(start, QB), :] += dkw[w * QB:(w + 1) * QB, :]
            dv_ref[pl.ds(start, QB), :] += dvw[w * QB:(w + 1) * QB, :]

    blk = pl.BlockSpec((QB, LANES), lambda p, b: (b, p))
    col = pl.BlockSpec((s, LANES), lambda p, b: (0, p))
    pair = pl.BlockSpec((hpp, QB, KW), lambda p, b: (p, 0, 0))
    return pl.pallas_call(
        body, name="attn_bwd", grid=(npair, nb),
        in_specs=[blk] + _window_specs() + _window_specs() + [blk, blk, pair],
        out_specs=(blk, col, col, pair),
        out_shape=(jax.ShapeDtypeStruct((s, d), BF16), jax.ShapeDtypeStruct((s, d), F32),
                   jax.ShapeDtypeStruct((s, d), F32), jax.ShapeDtypeStruct(bias.shape, F32)),
        compiler_params=_params(VMEM_BIG),
    )(q, k, k, k, v, v, v, do, lse, bias)


def _qkv_bwd(dres, dq, dk, dv, x, tab, wq, wkv, tm):
    s, d = x.shape
    nsh, _, kw = wkv.shape
    nh = nsh // 2
    base = 6

    def body(dres_ref, dq_ref, dk_ref, dv_ref, x_ref, t_ref, wq_ref, wkv_ref, dx_ref, dkv_ref, sums_ref):
        @pl.when(pl.program_id(0) == 0)
        def _():
            sums_ref[...] = jnp.zeros_like(sums_ref)

        dh1 = lax.dot_general(dq_ref[...], wq_ref[...], NT, preferred_element_type=F32)
        dkv_ref[:, 0:d] = dk_ref[...].astype(BF16)
        dkv_ref[:, d:2 * d] = dv_ref[...].astype(BF16)
        dhkv = jnp.zeros((tm, d), F32)
        for j in range(nsh):
            dhkv = dhkv + lax.dot_general(dkv_ref[:, j * kw:(j + 1) * kw], wkv_ref[j], NT,
                                          preferred_element_type=F32)
        xv = x_ref[...]
        r = _rs(xv)
        n = xv * r
        sums_ref[0:1, :] += _colsum(dh1 * n)
        sums_ref[1:2, :] += _colsum(dh1)
        sums_ref[2:3, :] += _colsum(dhkv * n)
        sums_ref[3:4, :] += _colsum(dhkv)
        dn = dh1 * t_ref[base + R_W1:base + R_W1 + 1, :] + dhkv * t_ref[R_KV:R_KV + 1, :]
        dx_ref[...] = dres_ref[...] + _norm_bwd(dn, n, r)

    return pl.pallas_call(
        body, name="qkv_bwd", grid=(s // tm,),
        in_specs=[_rows(tm, d)] * 5 + [_const(tab.shape), _const(wq.shape), _const(wkv.shape)],
        out_specs=(_rows(tm, d), _rows(tm, 2 * d), pl.BlockSpec((8, d), lambda i: (0, 0))),
        out_shape=(jax.ShapeDtypeStruct((s, d), F32), jax.ShapeDtypeStruct((s, 2 * d), BF16),
                   jax.ShapeDtypeStruct((8, d), F32)),
        compiler_params=_params(VMEM_BIG),
    )(dres, dq, dk, dv, x, tab, wq, wkv)


def _conv_bwd(dx1, x, y, bcx, tab, ck, wci, wco, tm):
    s, d = x.shape
    nsh, _, cw = wci.shape
    nt = s // tm

    def rev(i):
        return (nt - 1 - i, 0)

    def halo(i):
        return (jnp.maximum((nt - 1 - i) * (tm // 8) - 1, 0), 0)

    def body(dx_ref, x_ref, y_ref, bcx_ref, halo_ref, t_ref, ck_ref, wci_ref, wco_ref,
             dx0_ref, dyb_ref, dbcx_ref, sums_ref, dck_ref, carry):
        i = pl.program_id(0)

        @pl.when(i == 0)
        def _():
            sums_ref[...] = jnp.zeros_like(sums_ref)
            dck_ref[...] = jnp.zeros_like(dck_ref)
            carry[...] = jnp.zeros_like(carry)

        dxv = dx_ref[...]
        yv = y_ref[...]
        ry = _rs(yv)
        ny = yv * ry
        sums_ref[R_P1:R_P1 + 1, :] += _colsum(dxv * ny)
        dyb = _norm_bwd(dxv * t_ref[R_P1:R_P1 + 1, :], ny, ry).astype(BF16)
        dyb_ref[...] = dyb
        du = lax.dot_general(dyb, wco_ref[...], NT, preferred_element_type=F32)
        bg, cg, xi = bcx_ref[:, 0:d], bcx_ref[:, d:2 * d], bcx_ref[:, 2 * d:3 * d]
        z = cg * xi
        zp = halo_ref[:, d:2 * d] * halo_ref[:, 2 * d:3 * d]
        zp = jnp.where(i == nt - 1, jnp.zeros_like(zp), zp)
        row = lax.broadcasted_iota(jnp.int32, z.shape, 0)
        c1, c2 = zp[7:8, :], zp[6:7, :]
        z1 = jnp.where(row == 0, c1, pltpu.roll(z, 1, 0))
        z2 = jnp.where(row == 0, c2, jnp.where(row == 1, c1, pltpu.roll(z, 2, 0)))
        k0, k1, k2 = ck_ref[0:1, :], ck_ref[1:2, :], ck_ref[2:3, :]
        conv = k0 * z2 + k1 * z1 + k2 * z
        dconv = du * bg
        dck_ref[0:1, :] += _colsum(dconv * z2)
        dck_ref[1:2, :] += _colsum(dconv * z1)
        dck_ref[2:3, :] += _colsum(dconv * z)
        n1, n2 = carry[0:1, :], carry[1:2, :]
        d1 = jnp.where(row == tm - 1, n1, pltpu.roll(dconv, tm - 1, 0))
        d2 = jnp.where(row == tm - 1, n2, jnp.where(row == tm - 2, n1, pltpu.roll(dconv, tm - 2, 0)))
        carry[...] = dconv[0:8, :]
        dz = k2 * dconv + k1 * d1 + k0 * d2
        dbcx_ref[:, 0:d] = (du * conv).astype(BF16)
        dbcx_ref[:, d:2 * d] = (dz * xi).astype(BF16)
        dbcx_ref[:, 2 * d:3 * d] = (dz * cg).astype(BF16)
        dh = jnp.zeros((tm, d), F32)
        for j in range(nsh):
            dh = dh + lax.dot_general(dbcx_ref[:, j * cw:(j + 1) * cw], wci_ref[j], NT,
                                      preferred_element_type=F32)
        xv = x_ref[...]
        r = _rs(xv)
        n = xv * r
        sums_ref[R_W1:R_W1 + 1, :] += _colsum(dh * n)
        sums_ref[R_SH1:R_SH1 + 1, :] += _colsum(dh)
        dx0_ref[...] = dxv + _norm_bwd(dh * t_ref[R_W1:R_W1 + 1, :], n, r)

    rrow = lambda cols: pl.BlockSpec((tm, cols), rev)
    acc = pl.BlockSpec((8, d), lambda i: (0, 0))
    return pl.pallas_call(
        body, name="conv_bwd", grid=(nt,),
        in_specs=[rrow(d), rrow(d), rrow(d), rrow(3 * d), pl.BlockSpec((8, 3 * d), halo),
                  _const(tab.shape), _const(ck.shape), _const(wci.shape), _const(wco.shape)],
        out_specs=(rrow(d), rrow(d), rrow(3 * d), acc, acc),
        out_shape=(jax.ShapeDtypeStruct((s, d), F32), jax.ShapeDtypeStruct((s, d), BF16),
                   jax.ShapeDtypeStruct((s, 3 * d), BF16), jax.ShapeDtypeStruct((8, d), F32),
                   jax.ShapeDtypeStruct((8, d), F32)),
        scratch_shapes=[pltpu.VMEM((8, d), F32)],
        compiler_params=_params(VMEM_BIG),
    )(dx1, x, y, bcx, bcx, tab, ck, wci, wco)


def _wgrad(a, b, nblk, a_spec, b_spec, m, n, tk, name):
    s = a.shape[-2]
    nk = s // tk

    def body(a_ref, b_ref, o_ref, acc):
        kk = pl.program_id(1)

        @pl.when(kk == 0)
        def _():
            acc[...] = jnp.zeros_like(acc)

        acc[...] += lax.dot_general(a_ref[...], b_ref[...], TN, preferred_element_type=F32)

        @pl.when(kk == nk - 1)
        def _():
            o_ref[...] = acc[...].astype(BF16)

    return pl.pallas_call(
        body, name=name, grid=(nblk, nk),
        in_specs=[a_spec, b_spec],
        out_specs=pl.BlockSpec((None, m, n), lambda j, kk: (j, 0, 0)),
        out_shape=jax.ShapeDtypeStruct((nblk, m, n), BF16),
        scratch_shapes=[pltpu.VMEM((m, n), F32)],
        compiler_params=_params(VMEM_BIG),
    )(a, b)


def _wgrad_cols(a, b, nblk, tk, name):
    m, n = a.shape[1], b.shape[1] // nblk
    return _wgrad(a, b, nblk, pl.BlockSpec((tk, m), lambda j, kk: (kk, 0)),
                  pl.BlockSpec((tk, n), lambda j, kk: (kk, j)), m, n, tk, name)


def _wgrad_bstack(a, b, tk, name):
    nblk, _, n = b.shape
    m = a.shape[1]
    return _wgrad(a, b, nblk, pl.BlockSpec((tk, m), lambda j, kk: (kk, 0)),
                  pl.BlockSpec((None, tk, n), lambda j, kk: (j, kk, 0)), m, n, tk, name)


def _wgrad_wide(a, b, nblk, tk, name):
    s, m = a.shape
    n = b.shape[1] // nblk
    nk = s // tk

    def body(a_ref, b_ref, o_ref, acc):
        kk = pl.program_id(0)

        @pl.when(kk == 0)
        def _():
            acc[...] = jnp.zeros_like(acc)

        acc[...] += jnp.dot(a_ref[...].T, b_ref[...], preferred_element_type=F32)

        @pl.when(kk == nk - 1)
        def _():
            for j in range(nblk):
                o_ref[j] = acc[:, j * n:(j + 1) * n].astype(BF16)

    return pl.pallas_call(
        body, name=name, grid=(nk,),
        in_specs=[pl.BlockSpec((tk, m), lambda kk: (kk, 0)), pl.BlockSpec((tk, nblk * n), lambda kk: (kk, 0))],
        out_specs=pl.BlockSpec((nblk, m, n), lambda kk: (0, 0, 0)),
        out_shape=jax.ShapeDtypeStruct((nblk, m, n), BF16),
        scratch_shapes=[pltpu.VMEM((m, nblk * n), F32)],
        compiler_params=_params(VMEM_BIG),
    )(a, b)


def _wgrad_wide_stack(a, b, nsplit, tk, name):
    nblk, s, n = b.shape
    m = a.shape[1]
    per = nblk // nsplit
    nk = s // tk

    def body(a_ref, b_ref, o_ref, acc):
        kk = pl.program_id(1)

        @pl.when(kk == 0)
        def _():
            acc[...] = jnp.zeros_like(acc)

        at = a_ref[...].T
        for j in range(per):
            acc[j] += jnp.dot(at, b_ref[j], preferred_element_type=F32)

        @pl.when(kk == nk - 1)
        def _():
            o_ref[...] = acc[...].astype(BF16)

    return pl.pallas_call(
        body, name=name, grid=(nsplit, nk),
        in_specs=[pl.BlockSpec((tk, m), lambda h, kk: (kk, 0)), pl.BlockSpec((per, tk, n), lambda h, kk: (h, kk, 0))],
        out_specs=pl.BlockSpec((per, m, n), lambda h, kk: (h, 0, 0)),
        out_shape=jax.ShapeDtypeStruct((nblk, m, n), BF16),
        scratch_shapes=[pltpu.VMEM((per, m, n), F32)],
        compiler_params=_params(VMEM_BIG),
    )(a, b)


def _wgrad_astack(a, b, tk, name):
    nblk, _, m = a.shape
    n = b.shape[1]
    return _wgrad(a, b, nblk, pl.BlockSpec((None, tk, m), lambda j, kk: (j, kk, 0)),
                  pl.BlockSpec((tk, n), lambda j, kk: (kk, 0)), m, n, tk, name)


def _adamw_math(w, g, m, v):
    m = ADAM_B1 * m + (1.0 - ADAM_B1) * g
    v = ADAM_B2 * v + (1.0 - ADAM_B2) * (g * g)
    m_hat = m / (1.0 - ADAM_B1 ** ADAM_STEP)
    v_hat = v / (1.0 - ADAM_B2 ** ADAM_STEP)
    delta = -ADAM_LR * (m_hat / (jnp.sqrt(v_hat) + ADAM_EPS) + ADAM_WD * w)
    return delta, m, v


def _adamw_reduce(parts, w, m, v, tr, name):
    nl, r, c = w.shape
    tr = _row_tile(r, tr)

    def body(*refs):
        p_refs = refs[:nl]
        w_ref, m_ref, v_ref, g_ref, d_ref, mo_ref, vo_ref = refs[nl:]
        layer = pl.program_id(0)

        def partial(i):
            val = p_refs[0][i].astype(F32)
            for q in range(1, nl):
                val = jnp.where(layer == q, p_refs[q][i].astype(F32), val)
            return val

        g = partial(0)
        for i in range(1, N_DEV):
            g = g + partial(i)
        g_ref[...] = g
        d_ref[...], mo_ref[...], vo_ref[...] = _adamw_math(w_ref[...], g, m_ref[...], v_ref[...])

    blk = pl.BlockSpec((None, tr, c), lambda l, i: (l, i, 0))
    out = jax.ShapeDtypeStruct((nl, r, c), F32)
    p_specs = [pl.BlockSpec((N_DEV, tr, c), (lambda l, i, q=q: (0, jnp.where(l == q, i, 0), 0))) for q in range(nl)]
    return pl.pallas_call(
        body, name=name, grid=(nl, r // tr),
        in_specs=p_specs + [blk, blk, blk],
        out_specs=(blk,) * 4, out_shape=(out,) * 4,
        compiler_params=_params(VMEM_BIG),
    )(*parts, w, m, v)


def _adamw_outer(sct, dm, w, m, v, tr, name):
    nl, d, c = w.shape

    def body(s_ref, dm_ref, w_ref, m_ref, v_ref, g_ref, d_ref, mo_ref, vo_ref):
        g = jnp.dot(s_ref[...], dm_ref[...], preferred_element_type=F32)
        g_ref[...] = g
        d_ref[...], mo_ref[...], vo_ref[...] = _adamw_math(w_ref[...], g, m_ref[...], v_ref[...])

    blk = pl.BlockSpec((None, tr, c), lambda l, i: (l, i, 0))
    out = jax.ShapeDtypeStruct((nl, d, c), F32)
    return pl.pallas_call(
        body, name=name, grid=(nl, d // tr),
        in_specs=[pl.BlockSpec((tr, N_DEV), lambda l, i: (i, 0)),
                  pl.BlockSpec((None, N_DEV, c), lambda l, i: (l, 0, 0)), blk, blk, blk],
        out_specs=(blk,) * 4, out_shape=(out,) * 4,
        compiler_params=_params(VMEM_BIG),
    )(sct, dm, w, m, v)


def _pad_rows(a, rows):
    return jnp.concatenate([a, jnp.zeros((rows - a.shape[0],) + a.shape[1:], a.dtype)], axis=0)


def kernel(x, c, mod_w, mod_b, norm_g, ffn_w_in, ffn_w_out, conv_w_in, conv_k, conv_w_out, kv_mod_w, kv_mod_b, kv_norm_g, w_kv, attn_w_q, attn_w_o, rel_bias, loss_target, m_mod_w, m_mod_b, m_norm_g, m_ffn_w_in, m_ffn_w_out, m_conv_w_in, m_conv_k, m_conv_w_out, m_kv_mod_w, m_kv_mod_b, m_kv_norm_g, m_w_kv, m_attn_w_q, m_attn_w_o, m_rel_bias, v_mod_w, v_mod_b, v_norm_g, v_ffn_w_in, v_ffn_w_out, v_conv_w_in, v_conv_k, v_conv_w_out, v_kv_mod_w, v_kv_mod_b, v_kv_norm_g, v_w_kv, v_attn_w_q, v_attn_w_o, v_rel_bias):
    s, d = x.shape[1], x.shape[2]
    dq = d // LANES
    dsh = d // N_DEV
    nl = mod_w.shape[0]
    mw = mod_w.shape[2]
    kmw = kv_mod_w.shape[1]
    fw = ffn_w_in.shape[2]
    nh, nrel = rel_bias.shape[1], rel_bias.shape[2]
    tm = min(256, s)
    tk = min(1024, s)
    tk2 = min(2048, s)
    me = 4 * lax.axis_index("x") + 2 * lax.axis_index("y") + lax.axis_index("c")

    x0 = x[0]
    tgt = loss_target[0]

    small1 = jnp.concatenate([c.reshape(dq, LANES), norm_g.reshape(dq, LANES),
                              _pad_rows(conv_k[0], 8).reshape(dq, LANES)], axis=0)
    (sm,) = _exchange([small1], ["gather"], "gather_small")
    c_all = sm[:, 0:dq].reshape(N_DEV, d)
    ng_full = jnp.transpose(sm[:, dq:2 * dq].reshape(N_DEV, 8, dsh), (1, 0, 2)).reshape(8, d)
    ck_full = jnp.transpose(sm[:, 2 * dq:3 * dq].reshape(N_DEV, 8, dsh), (1, 0, 2)).reshape(8, d)

    modcols, silu_c = _mod_fwd(c_all, mod_w, kv_mod_w)
    (modall,) = _exchange([modcols], ["gather"], "gather_mod")

    cast = lambda *ws: [a.astype(BF16) for a in ws]
    gath = lambda ws: (ws, ["gather"] * len(ws))
    (h_conv, h_ffn0, h_attn, h_ffn1), token = _xstart(
        [gath(cast(conv_w_in[0], conv_w_out[0])), gath(cast(ffn_w_in[0], ffn_w_out[0])),
         gath(cast(w_kv, attn_w_q[0], attn_w_o[0])), gath(cast(ffn_w_in[1], ffn_w_out[1]))],
        modall, "gather_start")
    modall = modall + token[0, 0]
    mine = lax.dynamic_index_in_dim(modall, me, axis=1, keepdims=False)
    modrow = jnp.stack([mine[:, l * mw:(l + 1) * mw].reshape(6, d) for l in range(nl)])
    kvrow = mine[:, nl * mw:nl * mw + kmw].reshape(2, d)
    tab, modval = _vec_prep(modrow, mod_b.reshape(nl, 6, d), kvrow, kv_mod_b.reshape(2, d), ng_full,
                            kv_norm_g.reshape(1, d))
    bias = _bias_fwd(rel_bias[0])

    wci, wco = _xwait(h_conv, bias, "gather_wait_conv")
    wco = wco.reshape(d, d)
    x1, h1a, bcx, ua, ya = _conv_fwd(x0, tab, ck_full, wci, wco, tm)
    wfi0, wfo0 = _xwait(h_ffn0, x1, "gather_wait_ffn0")
    wfo0 = wfo0.reshape(N_DEV // 2, -1, d)
    x2, h2a, gua, aa, y2a = _ffn_fwd(x1, tab, 0, wfi0, wfo0, None, tm, "ffn_fwd0")
    wkv, wq, wo = _xwait(h_attn, x2, "gather_wait_attn")
    wq, wo = wq.reshape(d, d), wo.reshape(d, d)
    hkv, h1b, q, k, v = _qkv_fwd(x2, tab, wq, wkv, tm)
    o, lse = _attn_fwd(q, k, v, bias)
    x3, yb = _attn_out_fwd(o, x2, tab, wo, tm)
    wfi1, wfo1 = _xwait(h_ffn1, x3, "gather_wait_ffn1")
    wfo1 = wfo1.reshape(N_DEV // 2, -1, d)
    dx4, h2b, gub, ab, y2b, loss_acc = _ffn_fwd(x3, tab, 6, wfi1, wfo1, tgt, tm, "ffn_fwd1")
    loss = lax.psum(loss_acc[0, 0] * (0.5 / d), ("x", "y", "c"))

    scat = lambda ws: [(ws, ["scatter"] * len(ws))]
    dx3, dy2b, dgub, sums_f1 = _ffn_bwd(dx4, x3, y2b, gub, tab, 6, wfi1, wfo1, tm, "ffn_bwd1")
    g_wfi1 = _wgrad_wide_stack(h2b, dgub, 2, tk, "wgrad_ffn_in1")
    g_wfo1 = _wgrad_astack(ab, dy2b, tk2, "wgrad_ffn_out1").reshape(N_DEV, -1, d)
    (h_g1,), token = _xstart(scat([g_wfi1, g_wfo1]), dx3, "grads_start_ffn1")
    tab = tab + token[0, 0]
    dyb, do, sums_o = _attn_out_bwd(dx3, yb, tab, wo, tm)
    g_wo = _wgrad_wide(o, dyb, 1, tk, "wgrad_o").reshape(N_DEV, dsh, d)
    dqb, dk, dv, dbias = _attn_bwd(q, k, v, do, lse, bias)
    g_wq = _wgrad_wide(h1b, dqb, 1, tk, "wgrad_q").reshape(N_DEV, dsh, d)
    dx2, dkvb, sums_q = _qkv_bwd(dx3, dqb, dk, dv, x2, tab, wq, wkv, tm)
    g_wkv = _wgrad_wide(hkv, dkvb, N_DEV, tk, "wgrad_kv")
    (h_g2,), token = _xstart(scat([g_wkv, g_wq, g_wo]), dx2, "grads_start_attn")
    tab = tab + token[0, 0]
    dx1, dy2a, dgua, sums_f0 = _ffn_bwd(dx2, x1, y2a, gua, tab, 0, wfi0, wfo0, tm, "ffn_bwd0")
    g_wfi0 = _wgrad_wide_stack(h2a, dgua, 2, tk, "wgrad_ffn_in0")
    g_wfo0 = _wgrad_astack(aa, dy2a, tk2, "wgrad_ffn_out0").reshape(N_DEV, -1, d)
    (h_g3,), token = _xstart(scat([g_wfi0, g_wfo0]), dx1, "grads_start_ffn0")
    tab = tab + token[0, 0]
    dx0, dya, dbcx, sums_c, dck = _conv_bwd(dx1, x0, ya, bcx, tab, ck_full, wci, wco, tm)
    drel = _bias_bwd(dbias, nrel)
    dmod, dng, dkvg = _vec_bwd(sums_c, sums_f0, sums_q, sums_o, sums_f1, modval, ng_full, kv_norm_g.reshape(1, d))

    relw = -(-nrel // LANES) * LANES
    drel_p = jnp.concatenate([drel, jnp.zeros((nh, relw - nrel), F32)], axis=1)
    small3 = jnp.concatenate([dmod.reshape(16 * dq, LANES), dng.reshape(8 * dq, LANES), dkvg.reshape(8 * dq, LANES),
                              dck.reshape(8 * dq, LANES), drel_p.reshape(nh * relw // LANES, LANES)], axis=0)
    (sm,) = _exchange([small3], ["gather"], "gather_small_grads")
    g_wci = _wgrad_wide(h1a, dbcx, N_DEV, tk, "wgrad_conv_in")
    g_wco = _wgrad_wide(ua, dya, 1, tk, "wgrad_conv_out").reshape(N_DEV, dsh, d)
    (h_g4,), token = _xstart(scat([g_wci, g_wco]), sm, "grads_start_conv")
    o1, o2, o3, o4 = 16 * dq, 24 * dq, 32 * dq, 40 * dq
    dmod_all = sm[:, 0:o1].reshape(N_DEV, 16, d)
    mine_cols = lambda a: lax.dynamic_slice_in_dim(a, me * dsh, dsh, axis=2)
    dng_parts = mine_cols(sm[:, o1:o2].reshape(N_DEV, 8, d))
    dkvg_parts = sm[:, o2:o3].reshape(N_DEV, 8, d)[:, 0:1]
    dck_parts = mine_cols(sm[:, o3:o4].reshape(N_DEV, 8, d))[:, 0:3]
    drel_parts = sm[:, o4:].reshape(N_DEV, nh, relw)[:, :, 0:nrel]

    def update(parts, w, m, v, name, layers=1):
        shp = w.shape
        w3, m3, v3 = (a.reshape(layers, -1, shp[-1]) for a in (w, m, v))
        outs = _adamw_reduce([p.reshape(N_DEV, -1, shp[-1]) for p in parts], w3, m3, v3, 256, name)
        return [a.reshape(shp) for a in outs]

    p_wfi1, p_wfo1 = _xwait(h_g1, sm, "grads_wait_ffn1")
    p_wfi0, p_wfo0 = _xwait(h_g3, p_wfi1, "grads_wait_ffn0")
    u_ffn_in = update([p_wfi0, p_wfi1], ffn_w_in, m_ffn_w_in, v_ffn_w_in, "adamw_ffn_in", 2)
    u_ffn_out = update([p_wfo0, p_wfo1], ffn_w_out, m_ffn_w_out, v_ffn_w_out, "adamw_ffn_out", 2)
    p_wkv, p_wq, p_wo = _xwait(h_g2, u_ffn_out[0], "grads_wait_attn")
    u_w_kv = update([p_wkv], w_kv, m_w_kv, v_w_kv, "adamw_w_kv")
    u_w_q = update([p_wq], attn_w_q, m_attn_w_q, v_attn_w_q, "adamw_w_q")
    u_w_o = update([p_wo], attn_w_o, m_attn_w_o, v_attn_w_o, "adamw_w_o")

    sct = jnp.transpose(silu_c)
    dm_mod = jnp.stack([lax.dynamic_slice_in_dim(dmod_all[:, 6 * l:6 * l + 6].reshape(N_DEV, 6 * d), me * mw, mw, axis=1)
                        for l in range(nl)]).astype(BF16)
    dm_kv = lax.dynamic_slice_in_dim(dmod_all[:, R_KV:R_KV + 2].reshape(N_DEV, 2 * d), me * kmw, kmw, axis=1)
    u_mod_w = _adamw_outer(sct, dm_mod, mod_w, m_mod_w, v_mod_w, min(256, d), "adamw_mod_w")
    u_kv_mod_w = [a[0] for a in _adamw_outer(sct, dm_kv.astype(BF16)[None], kv_mod_w[None], m_kv_mod_w[None],
                                             v_kv_mod_w[None], min(256, d), "adamw_kv_mod_w")]

    modb_parts = jnp.stack([dmod_all[:, 6 * l:6 * l + 6].reshape(N_DEV, 6 * d) for l in range(nl)], axis=1)
    u_mod_b = update([modb_parts], mod_b, m_mod_b, v_mod_b, "adamw_mod_b")
    u_norm_g = update([dng_parts], norm_g.reshape(8, dsh), m_norm_g.reshape(8, dsh), v_norm_g.reshape(8, dsh), "adamw_norm_g")
    u_norm_g = [a.reshape(norm_g.shape) for a in u_norm_g]
    u_conv_k = update([dck_parts], conv_k, m_conv_k, v_conv_k, "adamw_conv_k")
    kvb_parts = dmod_all[:, R_KV:R_KV + 2].reshape(N_DEV, 1, 2 * d)
    u_kv_mod_b = [a.reshape(kv_mod_b.shape) for a in update([kvb_parts], kv_mod_b.reshape(1, -1), m_kv_mod_b.reshape(1, -1),
                                                            v_kv_mod_b.reshape(1, -1), "adamw_kv_mod_b")]
    u_kv_norm_g = [a.reshape(kv_norm_g.shape) for a in update([dkvg_parts], kv_norm_g.reshape(1, -1), m_kv_norm_g.reshape(1, -1),
                                                              v_kv_norm_g.reshape(1, -1), "adamw_kv_norm_g")]
    u_rel = update([drel_parts], rel_bias, m_rel_bias, v_rel_bias, "adamw_rel_bias")

    p_wci, p_wco = _xwait(h_g4, u_mod_w[0], "grads_wait_conv")
    u_conv_in = update([p_wci], conv_w_in, m_conv_w_in, v_conv_w_in, "adamw_conv_in")
    u_conv_out = update([p_wco], conv_w_out, m_conv_w_out, v_conv_w_out, "adamw_conv_out")

    ups = [u_mod_w, u_mod_b, u_norm_g, u_ffn_in, u_ffn_out, u_conv_in, u_conv_k, u_conv_out, u_kv_mod_w, u_kv_mod_b,
           u_kv_norm_g, u_w_kv, u_w_q, u_w_o, u_rel]
    return (loss, dx0[None], *[u[0] for u in ups], *[u[1] for u in ups], *[u[2] for u in ups], *[u[3] for u in ups])
```

```python
import jax
import jax.numpy as jnp
from jax import lax
from jax.experimental import pallas as pl
from jax.experimental.pallas import tpu as pltpu

F32 = jnp.float32
BF16 = jnp.bfloat16

EPS = 1e-6
CHUNK = 64
HEAD_DIM = 64
N_LEFT = 8
LANES = 128
QB = 4 * CHUNK
KW = QB + N_LEFT * CHUNK
NEG = -1e30
N_DEV = 8

ADAM_LR = 0.001
ADAM_B1 = 0.9
ADAM_B2 = 0.999
ADAM_EPS = 1e-08
ADAM_WD = 0.01
ADAM_STEP = 10

VMEM_BIG = 56 * 1024 * 1024

NT = (((1,), (1,)), ((), ()))
TN = (((0,), (0,)), ((), ()))

R_W1, R_SH1, R_P1, R_W2, R_SH2, R_P2 = range(6)
R_KV = 12


def _params(vmem):
    return pltpu.CompilerParams(vmem_limit_bytes=vmem)


def _row_tile(rows, cap):
    for t in range(min(cap, rows) // 16 * 16, 0, -16):
        if rows % t == 0:
            return t
    return rows


def _rows(tm, cols):
    return pl.BlockSpec((tm, cols), lambda i: (i, 0))


def _const(shape):
    nd = len(shape)
    return pl.BlockSpec(shape, lambda *_: (0,) * nd, pipeline_mode=pl.Buffered(1))


def _rs(x):
    return lax.rsqrt(jnp.mean(x * x, axis=-1, keepdims=True) + EPS)


def _norm_bwd(d, n, r):
    return r * (d - n * jnp.mean(d * n, axis=-1, keepdims=True))


def _colsum(a):
    return jnp.sum(a, axis=0, keepdims=True)


def _sigmoid(g):
    return 1.0 / (1.0 + jnp.exp(-g))


def _exchange(arrays, modes, name):
    n = len(arrays)
    out_shape = []
    for a, mode in zip(arrays, modes):
        shp = (N_DEV,) + a.shape if mode == "gather" else a.shape
        out_shape.append(jax.ShapeDtypeStruct(shp, a.dtype))

    def body(*refs):
        ins, outs = refs[:n], refs[n:2 * n]
        send_sems, recv_sems, local_sems = refs[2 * n:]
        x, y, c = lax.axis_index("x"), lax.axis_index("y"), lax.axis_index("c")
        me = 4 * x + 2 * y + c
        local, sends, recvs = [], [], []
        for a in range(n):
            own = ins[a] if modes[a] == "gather" else ins[a].at[me]
            cp = pltpu.make_async_copy(own, outs[a].at[me], local_sems.at[a])
            cp.start()
            local.append(cp)
        for k in range(1, N_DEV):
            px = 1 - x if k & 4 else x
            py = 1 - y if k & 2 else y
            pc = 1 - c if k & 1 else c
            peer = 4 * px + 2 * py + pc
            for a in range(n):
                src = ins[a] if modes[a] == "gather" else ins[a].at[peer]
                sem = a * (N_DEV - 1) + k - 1
                cp = pltpu.make_async_remote_copy(
                    src_ref=src, dst_ref=outs[a].at[me],
                    send_sem=send_sems.at[sem], recv_sem=recv_sems.at[sem],
                    device_id=(px, py, pc), device_id_type=pl.DeviceIdType.MESH)
                cp.start()
                sends.append(cp)
                recvs.append(pltpu.make_async_remote_copy(
                    src_ref=src, dst_ref=outs[a].at[peer],
                    send_sem=send_sems.at[sem], recv_sem=recv_sems.at[sem],
                    device_id=(px, py, pc), device_id_type=pl.DeviceIdType.MESH))
        for cp in recvs:
            cp.wait_recv()
        for cp in sends:
            cp.wait_send()
        for cp in local:
            cp.wait()

    any_spec = pl.BlockSpec(memory_space=pl.ANY)
    return pl.pallas_call(
        body, name=name,
        out_shape=tuple(out_shape),
        in_specs=[any_spec] * n,
        out_specs=tuple([any_spec] * n),
        scratch_shapes=[
            pltpu.SemaphoreType.DMA((n * (N_DEV - 1),)),
            pltpu.SemaphoreType.DMA((n * (N_DEV - 1),)),
            pltpu.SemaphoreType.DMA((n,)),
        ],
    )(*arrays)


def _peers(x, y, c):
    out = []
    for k in range(1, N_DEV):
        px = 1 - x if k & 4 else x
        py = 1 - y if k & 2 else y
        pc = 1 - c if k & 1 else c
        out.append((k - 1, (px, py, pc), 4 * px + 2 * py + pc))
    return out


def _land_shape(a, mode):
    return (N_DEV,) + a.shape if mode == "gather" else a.shape


_HBM = pl.BlockSpec(memory_space=pltpu.HBM)
_SEM = pl.BlockSpec(memory_space=pltpu.SEMAPHORE)
_EFFECT = pltpu.SideEffectType.DATAFLOW_SIDE_EFFECTING


def _xstart(groups, after, name):
    flat = [(a, m) for arrays, modes in groups for a, m in zip(arrays, modes)]
    n, ngr = len(flat), len(groups)
    sizes = [len(arrays) for arrays, _ in groups]
    npeer = N_DEV - 1

    def body(*refs):
        ins, lands = refs[:n], refs[n:2 * n]
        outs = refs[2 * n + 1:]
        sems = outs[:2 * ngr]
        token = outs[2 * ngr + 2 * n]
        local_sems = outs[2 * ngr + 2 * n + 1]
        stage = outs[2 * ngr + 2 * n + 2:]
        x, y, c = lax.axis_index("x"), lax.axis_index("y"), lax.axis_index("c")
        me = 4 * x + 2 * y + c
        loads, stores = [], []
        for a in range(n):
            own = ins[a] if flat[a][1] == "gather" else ins[a].at[me]
            loads.append(pltpu.make_async_copy(own, stage[a], local_sems.at[a]))
            stores.append(pltpu.make_async_copy(stage[a], lands[a].at[me], local_sems.at[a]))
            loads[a].start()
        for a in range(n):
            loads[a].wait()
            stores[a].start()
        a = 0
        for g in range(ngr):
            for j in range(sizes[g]):
                mode = flat[a][1]
                for slot, peer, pidx in _peers(x, y, c):
                    pltpu.make_async_remote_copy(
                        src_ref=ins[a] if mode == "gather" else ins[a].at[pidx], dst_ref=lands[a].at[me],
                        send_sem=sems[2 * g].at[j * npeer + slot], recv_sem=sems[2 * g + 1].at[j * npeer + slot],
                        device_id=peer, device_id_type=pl.DeviceIdType.MESH).start()
                a += 1
        for cp in stores:
            cp.wait()
        token[...] = jnp.zeros_like(token)

    out_shape, out_specs = [], []
    for sz in sizes:
        out_shape += [pltpu.SemaphoreType.DMA((sz * npeer,)), pltpu.SemaphoreType.DMA((sz * npeer,))]
        out_specs += [_SEM, _SEM]
    out_shape += [pltpu.HBM(a.shape, a.dtype) for a, _ in flat]
    out_shape += [pltpu.HBM(_land_shape(a, m), a.dtype) for a, m in flat]
    out_specs += [_HBM] * (2 * n)
    out_shape.append(jax.ShapeDtypeStruct((8, LANES), F32))
    out_specs.append(pl.BlockSpec(memory_space=pltpu.VMEM))
    args = [pltpu.with_memory_space_constraint(a, pltpu.HBM) for a, _ in flat]
    args += [pltpu.with_memory_space_constraint(lax.empty(_land_shape(a, m), a.dtype), pltpu.HBM) for a, m in flat]
    res = pl.pallas_call(
        body, name=name, out_shape=tuple(out_shape),
        in_specs=[_HBM] * (2 * n) + [pl.BlockSpec(memory_space=pl.ANY)], out_specs=tuple(out_specs),
        input_output_aliases={i: 2 * ngr + i for i in range(2 * n)},
        scratch_shapes=[pltpu.SemaphoreType.DMA((n,))]
                       + [pltpu.VMEM(a.shape if m == "gather" else a.shape[1:], a.dtype) for a, m in flat],
        compiler_params=pltpu.CompilerParams(has_side_effects=_EFFECT, vmem_limit_bytes=VMEM_BIG),
    )(*args, after)
    handles, a = [], 0
    for g, sz in enumerate(sizes):
        handles.append((res[2 * g], res[2 * g + 1], list(res[2 * ngr + a:2 * ngr + a + sz]),
                        list(res[2 * ngr + n + a:2 * ngr + n + a + sz]), list(groups[g][1])))
        a += sz
    return handles, res[-1]


def _xwait(handle, after, name):
    send_sems, recv_sems, srcs, lands, modes = handle
    m = len(srcs)
    npeer = N_DEV - 1
    after = list(after)

    def body(*refs):
        ins, lnd = refs[:m], refs[m:2 * m]
        ssem, rsem = refs[2 * m], refs[2 * m + 1]
        x, y, c = lax.axis_index("x"), lax.axis_index("y"), lax.axis_index("c")
        for j in range(m):
            for slot, peer, pidx in _peers(x, y, c):
                cp = pltpu.make_async_remote_copy(
                    src_ref=ins[j] if modes[j] == "gather" else ins[j].at[pidx], dst_ref=lnd[j].at[pidx],
                    send_sem=ssem.at[j * npeer + slot], recv_sem=rsem.at[j * npeer + slot],
                    device_id=peer, device_id_type=pl.DeviceIdType.MESH)
                cp.wait_send()
                cp.wait_recv()

    res = pl.pallas_call(
        body, name=name,
        out_shape=tuple([pltpu.HBM(a.shape, a.dtype) for a in srcs] + [pltpu.HBM(a.shape, a.dtype) for a in lands]),
        in_specs=[_HBM] * (2 * m) + [_SEM, _SEM] + [pl.BlockSpec(memory_space=pl.ANY)] * len(after),
        out_specs=tuple([_HBM] * (2 * m)),
        input_output_aliases={i: i for i in range(2 * m)},
        compiler_params=pltpu.CompilerParams(has_side_effects=_EFFECT),
    )(*srcs, *lands, send_sems, recv_sems, *after)
    return list(res[m:])


def _mod_fwd(c_all, mod_w, kv_mod_w):
    nl, d, mw = mod_w.shape
    kw = kv_mod_w.shape[1]

    def body(c_ref, mw_ref, kw_ref, o_ref, sc_ref):
        cc = c_ref[...]
        sc = (cc * _sigmoid(cc)).astype(BF16)
        sc_ref[...] = sc
        for l in range(nl):
            o_ref[:, l * mw:(l + 1) * mw] = jnp.dot(sc, mw_ref[l].astype(BF16), preferred_element_type=F32)
        o_ref[:, nl * mw:nl * mw + kw] = jnp.dot(sc, kw_ref[...].astype(BF16), preferred_element_type=F32)

    return pl.pallas_call(
        body, name="mod_fwd",
        out_shape=(jax.ShapeDtypeStruct((c_all.shape[0], nl * mw + kw), F32),
                   jax.ShapeDtypeStruct(c_all.shape, BF16)),
        compiler_params=_params(VMEM_BIG),
    )(c_all, mod_w, kv_mod_w)


def _vec_prep(modrow, modb, kvrow, kvb, ng, kvg):
    d = ng.shape[1]

    def body(mr_ref, mb_ref, kr_ref, kb_ref, ng_ref, kvg_ref, t_ref, m_ref):
        t_ref[...] = jnp.zeros_like(t_ref)
        m_ref[...] = jnp.zeros_like(m_ref)
        for l in range(2):
            mod = mr_ref[l] + mb_ref[l]
            m_ref[6 * l:6 * l + 6, :] = mod
            g = ng_ref[4 * l:4 * l + 4, :]
            t_ref[6 * l + R_W1:6 * l + R_W1 + 1, :] = g[0:1] * (1.0 + mod[1:2])
            t_ref[6 * l + R_SH1:6 * l + R_SH1 + 1, :] = mod[0:1]
            t_ref[6 * l + R_P1:6 * l + R_P1 + 1, :] = mod[2:3] * g[1:2]
            t_ref[6 * l + R_W2:6 * l + R_W2 + 1, :] = g[2:3] * (1.0 + mod[4:5])
            t_ref[6 * l + R_SH2:6 * l + R_SH2 + 1, :] = mod[3:4]
            t_ref[6 * l + R_P2:6 * l + R_P2 + 1, :] = mod[5:6] * g[3:4]
        kv = kr_ref[...] + kb_ref[...]
        m_ref[R_KV:R_KV + 2, :] = kv
        t_ref[R_KV:R_KV + 1, :] = kvg_ref[...] * (1.0 + kv[1:2])
        t_ref[R_KV + 1:R_KV + 2, :] = kv[0:1]

    return pl.pallas_call(
        body, name="vec_prep",
        out_shape=(jax.ShapeDtypeStruct((16, d), F32), jax.ShapeDtypeStruct((16, d), F32)),
    )(modrow, modb, kvrow, kvb, ng, kvg)


def _vec_bwd(sums_c, sums_f0, sums_q, sums_o, sums_f1, mt, ng, kvg):
    d = ng.shape[1]

    def body(sc_ref, sf0_ref, sq_ref, so_ref, sf1_ref, m_ref, ng_ref, kvg_ref, dm_ref, dng_ref, dkvg_ref, g_ref):
        g_ref[...] = jnp.zeros_like(g_ref)
        g_ref[0:3, :] = sc_ref[0:3, :]
        g_ref[3:6, :] = sf0_ref[3:6, :]
        g_ref[6:8, :] = sq_ref[0:2, :]
        g_ref[8:9, :] = so_ref[2:3, :]
        g_ref[9:12, :] = sf1_ref[3:6, :]
        g_ref[R_KV:R_KV + 2, :] = sq_ref[2:4, :]
        dm_ref[...] = jnp.zeros_like(dm_ref)
        dkvg_ref[...] = jnp.zeros_like(dkvg_ref)
        for l in range(2):
            g = ng_ref[4 * l:4 * l + 4, :]
            mod = m_ref[6 * l:6 * l + 6, :]
            s = g_ref[6 * l:6 * l + 6, :]
            dm_ref[6 * l + 0:6 * l + 1, :] = s[1:2]
            dm_ref[6 * l + 1:6 * l + 2, :] = s[0:1] * g[0:1]
            dm_ref[6 * l + 2:6 * l + 3, :] = s[2:3] * g[1:2]
            dm_ref[6 * l + 3:6 * l + 4, :] = s[4:5]
            dm_ref[6 * l + 4:6 * l + 5, :] = s[3:4] * g[2:3]
            dm_ref[6 * l + 5:6 * l + 6, :] = s[5:6] * g[3:4]
            dng_ref[4 * l + 0:4 * l + 1, :] = s[0:1] * (1.0 + mod[1:2])
            dng_ref[4 * l + 1:4 * l + 2, :] = s[2:3] * mod[2:3]
            dng_ref[4 * l + 2:4 * l + 3, :] = s[3:4] * (1.0 + mod[4:5])
            dng_ref[4 * l + 3:4 * l + 4, :] = s[5:6] * mod[5:6]
        dm_ref[R_KV:R_KV + 1, :] = g_ref[R_KV + 1:R_KV + 2, :]
        dm_ref[R_KV + 1:R_KV + 2, :] = g_ref[R_KV:R_KV + 1, :] * kvg_ref[...]
        dkvg_ref[0:1, :] = g_ref[R_KV:R_KV + 1, :] * (1.0 + m_ref[R_KV + 1:R_KV + 2, :])

    return pl.pallas_call(
        body, name="vec_bwd",
        out_shape=(jax.ShapeDtypeStruct((16, d), F32), jax.ShapeDtypeStruct((8, d), F32),
                   jax.ShapeDtypeStruct((8, d), F32)),
        scratch_shapes=[pltpu.VMEM((16, d), F32)],
    )(sums_c, sums_f0, sums_q, sums_o, sums_f1, mt, ng, kvg)


def _rel_index(nrel):
    width = KW + QB
    e = lax.broadcasted_iota(jnp.int32, (nrel, width), 1)
    r = lax.broadcasted_iota(jnp.int32, (nrel, width), 0)
    max_rel = (nrel - 1) // 2
    idx = jnp.clip(KW - e, -max_rel, max_rel) + max_rel
    return (idx == r).astype(F32)


def _band_valid():
    row = lax.broadcasted_iota(jnp.int32, (QB, KW), 0) // CHUNK
    col = lax.broadcasted_iota(jnp.int32, (QB, KW), 1) // CHUNK
    j = col - row
    return (j >= 0) & (j <= N_LEFT)


def _bias_fwd(rel_bias):
    nh, nrel = rel_bias.shape
    width = KW + QB

    def body(rb_ref, o_ref):
        onehot = _rel_index(nrel)
        gr = jnp.dot(rb_ref[...], onehot, preferred_element_type=F32, precision=lax.Precision.HIGHEST)
        valid = _band_valid()
        for h in range(nh):
            xrow = jnp.broadcast_to(gr[h:h + 1, :], (QB, width))
            rolled = pltpu.roll(xrow, 0, 1, stride=1, stride_axis=0)
            o_ref[h] = jnp.where(valid, rolled[:, QB:], NEG)

    return pl.pallas_call(
        body, name="bias_fwd",
        out_shape=jax.ShapeDtypeStruct((nh, QB, KW), F32),
        compiler_params=_params(VMEM_BIG),
    )(rel_bias)


def _bias_bwd(dbias, nrel):
    nh = dbias.shape[0]
    width = KW + QB

    def body(db_ref, o_ref, diag_ref):
        onehot = _rel_index(nrel)
        valid = _band_valid()
        rr = lax.broadcasted_iota(jnp.int32, (QB, QB), 0)
        cc = lax.broadcasted_iota(jnp.int32, (QB, QB), 1)
        flip = (rr + cc == QB - 1).astype(F32)
        for h in range(nh):
            rev = jnp.dot(flip, jnp.where(valid, db_ref[h], 0.0), preferred_element_type=F32,
                          precision=lax.Precision.HIGHEST)
            w = jnp.concatenate([jnp.zeros((QB, QB), F32), rev], axis=1)
            back = pltpu.roll(w, width - (QB - 1), 1, stride=1, stride_axis=0)
            diag_ref[h:h + 1, :] = _colsum(back)
        o_ref[...] = lax.dot_general(diag_ref[...], onehot, NT, preferred_element_type=F32,
                                     precision=lax.Precision.HIGHEST)

    return pl.pallas_call(
        body, name="bias_bwd",
        out_shape=jax.ShapeDtypeStruct((nh, nrel), F32),
        scratch_shapes=[pltpu.VMEM((nh, width), F32)],
        compiler_params=_params(VMEM_BIG),
    )(dbias)


def _conv_fwd(x, tab, ck, wci, wco, tm):
    s, d = x.shape
    nsh, _, cw = wci.shape

    def body(x_ref, t_ref, ck_ref, wci_ref, wco_ref, x1_ref, h_ref, bcx_ref, u_ref, y_ref, carry):
        @pl.when(pl.program_id(0) == 0)
        def _():
            carry[...] = jnp.zeros_like(carry)

        xv = x_ref[...]
        hb = ((xv * _rs(xv)) * t_ref[R_W1:R_W1 + 1, :] + t_ref[R_SH1:R_SH1 + 1, :]).astype(BF16)
        h_ref[...] = hb
        for j in range(nsh):
            bcx_ref[:, j * cw:(j + 1) * cw] = jnp.dot(hb, wci_ref[j], preferred_element_type=F32)
        bg, cg, xi = bcx_ref[:, 0:d], bcx_ref[:, d:2 * d], bcx_ref[:, 2 * d:3 * d]
        z = cg * xi
        row = lax.broadcasted_iota(jnp.int32, z.shape, 0)
        c1, c2 = carry[7:8, :], carry[6:7, :]
        z1 = jnp.where(row == 0, c1, pltpu.roll(z, 1, 0))
        z2 = jnp.where(row == 0, c2, jnp.where(row == 1, c1, pltpu.roll(z, 2, 0)))
        carry[...] = z[tm - 8:tm, :]
        conv = ck_ref[0:1, :] * z2 + ck_ref[1:2, :] * z1 + ck_ref[2:3, :] * z
        ub = (bg * conv).astype(BF16)
        u_ref[...] = ub
        yv = jnp.dot(ub, wco_ref[...], preferred_element_type=F32)
        y_ref[...] = yv
        x1_ref[...] = xv + (yv * _rs(yv)) * t_ref[R_P1:R_P1 + 1, :]

    return pl.pallas_call(
        body, name="conv_fwd", grid=(s // tm,),
        in_specs=[_rows(tm, d), _const(tab.shape), _const(ck.shape), _const(wci.shape), _const(wco.shape)],
        out_specs=(_rows(tm, d), _rows(tm, d), _rows(tm, 3 * d), _rows(tm, d), _rows(tm, d)),
        out_shape=(jax.ShapeDtypeStruct((s, d), F32), jax.ShapeDtypeStruct((s, d), BF16),
                   jax.ShapeDtypeStruct((s, 3 * d), F32), jax.ShapeDtypeStruct((s, d), BF16),
                   jax.ShapeDtypeStruct((s, d), F32)),
        scratch_shapes=[pltpu.VMEM((8, d), F32)],
        compiler_params=_params(VMEM_BIG),
    )(x, tab, ck, wci, wco)


def _ffn_fwd(x, tab, base, wfi, wfo, tgt, tm, name):
    s, d = x.shape
    nsh, fw, _ = wfi.shape
    nh = nsh // 2
    with_loss = tgt is not None

    def body(*refs):
        if with_loss:
            x_ref, t_ref, wfi_ref, wfo_ref, tgt_ref, xo_ref, h_ref, gu_ref, y_ref, loss_ref = refs
        else:
            x_ref, t_ref, wfi_ref, wfo_ref, xo_ref, h_ref, gu_ref, y_ref = refs
        xv = x_ref[...]
        hb = ((xv * _rs(xv)) * t_ref[base + R_W2:base + R_W2 + 1, :]
              + t_ref[base + R_SH2:base + R_SH2 + 1, :]).astype(BF16)
        h_ref[...] = hb
        acc = jnp.zeros((tm, d), F32)
        for j in range(nh):
            g = lax.dot_general(hb, wfi_ref[j], NT, preferred_element_type=F32)
            u = lax.dot_general(hb, wfi_ref[j + nh], NT, preferred_element_type=F32)
            gu_ref[j] = g.astype(BF16)
            gu_ref[j + nh] = u.astype(BF16)
            ab = ((g * _sigmoid(g)) * u).astype(BF16)
            acc = acc + jnp.dot(ab, wfo_ref[j], preferred_element_type=F32)
        y_ref[...] = acc
        xo = xv + (acc * _rs(acc)) * t_ref[base + R_P2:base + R_P2 + 1, :]
        if with_loss:
            @pl.when(pl.program_id(0) == 0)
            def _():
                loss_ref[...] = jnp.zeros_like(loss_ref)

            err = xo - tgt_ref[...]
            xo_ref[...] = err * (1.0 / d)
            loss_ref[...] += jnp.sum(err * err)
        else:
            xo_ref[...] = xo

    in_specs = [_rows(tm, d), _const(tab.shape), _const(wfi.shape), _const(wfo.shape)]
    args = [x, tab, wfi, wfo]
    out_specs = [_rows(tm, d), _rows(tm, d), pl.BlockSpec((nsh, tm, fw), lambda i: (0, i, 0)), _rows(tm, d)]
    out_shape = [jax.ShapeDtypeStruct((s, d), F32), jax.ShapeDtypeStruct((s, d), BF16),
                 jax.ShapeDtypeStruct((nsh, s, fw), BF16), jax.ShapeDtypeStruct((s, d), F32)]
    if with_loss:
        in_specs.append(_rows(tm, d))
        args.append(tgt)
        out_specs.append(pl.BlockSpec((8, LANES), lambda i: (0, 0)))
        out_shape.append(jax.ShapeDtypeStruct((8, LANES), F32))
    return pl.pallas_call(
        body, name=name, grid=(s // tm,), in_specs=in_specs, out_specs=tuple(out_specs),
        out_shape=tuple(out_shape), compiler_params=_params(VMEM_BIG),
    )(*args)


def _qkv_fwd(x, tab, wq, wkv, tm):
    s, d = x.shape
    nsh, _, kw = wkv.shape
    nh = nsh // 2
    base = 6

    def body(x_ref, t_ref, wq_ref, wkv_ref, hkv_ref, h1_ref, q_ref, k_ref, v_ref):
        xv = x_ref[...]
        n = xv * _rs(xv)
        hkv = (n * t_ref[R_KV:R_KV + 1, :] + t_ref[R_KV + 1:R_KV + 2, :]).astype(BF16)
        h1 = (n * t_ref[base + R_W1:base + R_W1 + 1, :] + t_ref[base + R_SH1:base + R_SH1 + 1, :]).astype(BF16)
        hkv_ref[...] = hkv
        h1_ref[...] = h1
        q_ref[...] = (jnp.dot(h1, wq_ref[...], preferred_element_type=F32) * (HEAD_DIM ** -0.5)).astype(BF16)
        for j in range(nh):
            k_ref[:, j * kw:(j + 1) * kw] = jnp.dot(hkv, wkv_ref[j], preferred_element_type=F32).astype(BF16)
            v_ref[:, j * kw:(j + 1) * kw] = jnp.dot(hkv, wkv_ref[j + nh], preferred_element_type=F32).astype(BF16)

    act = jax.ShapeDtypeStruct((s, d), BF16)
    return pl.pallas_call(
        body, name="qkv_fwd", grid=(s // tm,),
        in_specs=[_rows(tm, d), _const(tab.shape), _const(wq.shape), _const(wkv.shape)],
        out_specs=tuple([_rows(tm, d)] * 5), out_shape=(act,) * 5,
        compiler_params=_params(VMEM_BIG),
    )(x, tab, wq, wkv)


def _window_specs():
    return [pl.BlockSpec((QB, LANES), (lambda p, b, w=w: (jnp.maximum(b - 2 + w, 0), p))) for w in range(3)]


def _key_valid(b):
    col = lax.broadcasted_iota(jnp.int32, (QB, KW), 1) // CHUNK
    return (b * (QB // CHUNK) - N_LEFT + col) >= 0


def _head_masks():
    lane = lax.broadcasted_iota(jnp.int32, (1, LANES), 1)
    return [(lane // HEAD_DIM == hh) for hh in range(LANES // HEAD_DIM)]


def _attn_fwd(q, k, v, bias):
    s, d = q.shape
    npair, nb = d // LANES, s // QB
    hpp = LANES // HEAD_DIM

    def body(q_ref, k0, k1, k2, v0, v1, v2, bias_ref, o_ref, lse_ref):
        b = pl.program_id(1)
        qv = q_ref[...]
        kwin = jnp.concatenate([k0[...], k1[...], k2[...]], axis=0)
        vwin = jnp.concatenate([v0[...], v1[...], v2[...]], axis=0)
        valid = _key_valid(b)
        masks = _head_masks()
        o = jnp.zeros((QB, LANES), F32)
        lse = jnp.zeros((QB, LANES), F32)
        for hh in range(hpp):
            qm = jnp.where(masks[hh], qv, jnp.zeros_like(qv))
            vm = jnp.where(masks[hh], vwin, jnp.zeros_like(vwin))
            sc = lax.dot_general(qm, kwin, NT, preferred_element_type=F32) + bias_ref[hh]
            sc = jnp.where(valid, sc, NEG)
            m = jnp.max(sc, axis=-1, keepdims=True)
            p = jnp.exp(sc - m)
            l = jnp.sum(p, axis=-1, keepdims=True)
            o = o + jnp.dot(p.astype(BF16), vm, preferred_element_type=F32) * (1.0 / l)
            lse = jnp.where(masks[hh], m + jnp.log(l), lse)
        o_ref[...] = o.astype(BF16)
        lse_ref[...] = lse

    blk = pl.BlockSpec((QB, LANES), lambda p, b: (b, p))
    return pl.pallas_call(
        body, name="attn_fwd", grid=(npair, nb),
        in_specs=[blk] + _window_specs() + _window_specs()
                 + [pl.BlockSpec((hpp, QB, KW), lambda p, b: (p, 0, 0))],
        out_specs=(blk, blk),
        out_shape=(jax.ShapeDtypeStruct((s, d), BF16), jax.ShapeDtypeStruct((s, d), F32)),
        compiler_params=_params(VMEM_BIG),
    )(q, k, k, k, v, v, v, bias)


def _attn_out_fwd(o, x, tab, wo, tm):
    s, d = x.shape
    base = 6

    def body(o_ref, x_ref, t_ref, wo_ref, x3_ref, y_ref):
        yv = jnp.dot(o_ref[...], wo_ref[...], preferred_element_type=F32)
        y_ref[...] = yv
        x3_ref[...] = x_ref[...] + (yv * _rs(yv)) * t_ref[base + R_P1:base + R_P1 + 1, :]

    return pl.pallas_call(
        body, name="attn_out_fwd", grid=(s // tm,),
        in_specs=[_rows(tm, d), _rows(tm, d), _const(tab.shape), _const(wo.shape)],
        out_specs=(_rows(tm, d), _rows(tm, d)),
        out_shape=(jax.ShapeDtypeStruct((s, d), F32), jax.ShapeDtypeStruct((s, d), F32)),
        compiler_params=_params(VMEM_BIG),
    )(o, x, tab, wo)


def _ffn_bwd(dxo, x, y, gu, tab, base, wfi, wfo, tm, name):
    s, d = x.shape
    nsh, fw, _ = wfi.shape
    nh = nsh // 2

    def body(dxo_ref, x_ref, y_ref, gu_ref, t_ref, wfi_ref, wfo_ref, dx_ref, dyb_ref, dgu_ref, a_ref, sums_ref):
        @pl.when(pl.program_id(0) == 0)
        def _():
            sums_ref[...] = jnp.zeros_like(sums_ref)

        dxo_v = dxo_ref[...]
        yv = y_ref[...]
        ry = _rs(yv)
        ny = yv * ry
        sums_ref[R_P2:R_P2 + 1, :] += _colsum(dxo_v * ny)
        dyb = _norm_bwd(dxo_v * t_ref[base + R_P2:base + R_P2 + 1, :], ny, ry).astype(BF16)
        dyb_ref[...] = dyb
        dh = jnp.zeros((tm, d), F32)
        for j in range(nh):
            da = lax.dot_general(dyb, wfo_ref[j], NT, preferred_element_type=F32)
            g, u = gu_ref[j].astype(F32), gu_ref[j + nh].astype(F32)
            sg = _sigmoid(g)
            gs = g * sg
            a_ref[j] = (gs * u).astype(BF16)
            dg = (da * u * sg * (1.0 + g * (1.0 - sg))).astype(BF16)
            du = (da * gs).astype(BF16)
            dgu_ref[j] = dg
            dgu_ref[j + nh] = du
            dh = dh + jnp.dot(dg, wfi_ref[j], preferred_element_type=F32)
            dh = dh + jnp.dot(du, wfi_ref[j + nh], preferred_element_type=F32)
        xv = x_ref[...]
        r = _rs(xv)
        n = xv * r
        sums_ref[R_SH2:R_SH2 + 1, :] += _colsum(dh)
        sums_ref[R_W2:R_W2 + 1, :] += _colsum(dh * n)
        dx_ref[...] = dxo_v + _norm_bwd(dh * t_ref[base + R_W2:base + R_W2 + 1, :], n, r)

    stack = lambda n: pl.BlockSpec((n, tm, fw), lambda i: (0, i, 0))
    return pl.pallas_call(
        body, name=name, grid=(s // tm,),
        in_specs=[_rows(tm, d), _rows(tm, d), _rows(tm, d), stack(nsh),
                  _const(tab.shape), _const(wfi.shape), _const(wfo.shape)],
        out_specs=(_rows(tm, d), _rows(tm, d), stack(nsh), stack(nh), pl.BlockSpec((8, d), lambda i: (0, 0))),
        out_shape=(jax.ShapeDtypeStruct((s, d), F32), jax.ShapeDtypeStruct((s, d), BF16),
                   jax.ShapeDtypeStruct((nsh, s, fw), BF16), jax.ShapeDtypeStruct((nh, s, fw), BF16),
                   jax.ShapeDtypeStruct((8, d), F32)),
        compiler_params=_params(VMEM_BIG),
    )(dxo, x, y, gu, tab, wfi, wfo)


def _attn_out_bwd(dx, y, tab, wo, tm):
    s, d = y.shape
    base = 6

    def body(dx_ref, y_ref, t_ref, wo_ref, dyb_ref, do_ref, sums_ref):
        @pl.when(pl.program_id(0) == 0)
        def _():
            sums_ref[...] = jnp.zeros_like(sums_ref)

        dxv = dx_ref[...]
        yv = y_ref[...]
        ry = _rs(yv)
        ny = yv * ry
        sums_ref[R_P1:R_P1 + 1, :] += _colsum(dxv * ny)
        dyb = _norm_bwd(dxv * t_ref[base + R_P1:base + R_P1 + 1, :], ny, ry).astype(BF16)
        dyb_ref[...] = dyb
        do_ref[...] = lax.dot_general(dyb, wo_ref[...], NT, preferred_element_type=F32).astype(BF16)

    return pl.pallas_call(
        body, name="attn_out_bwd", grid=(s // tm,),
        in_specs=[_rows(tm, d), _rows(tm, d), _const(tab.shape), _const(wo.shape)],
        out_specs=(_rows(tm, d), _rows(tm, d), pl.BlockSpec((8, d), lambda i: (0, 0))),
        out_shape=(jax.ShapeDtypeStruct((s, d), BF16), jax.ShapeDtypeStruct((s, d), BF16),
                   jax.ShapeDtypeStruct((8, d), F32)),
        compiler_params=_params(VMEM_BIG),
    )(dx, y, tab, wo)


def _attn_bwd(q, k, v, do, lse, bias):
    s, d = q.shape
    npair, nb = d // LANES, s // QB
    hpp = LANES // HEAD_DIM

    def body(q_ref, k0, k1, k2, v0, v1, v2, do_ref, lse_ref, bias_ref, dq_ref, dk_ref, dv_ref, db_ref):
        b = pl.program_id(1)

        @pl.when(b == 0)
        def _():
            dk_ref[...] = jnp.zeros_like(dk_ref)
            dv_ref[...] = jnp.zeros_like(dv_ref)
            db_ref[...] = jnp.zeros_like(db_ref)

        qv = q_ref[...]
        dov = do_ref[...]
        lsev = lse_ref[...]
        kwin = jnp.concatenate([k0[...], k1[...], k2[...]], axis=0)
        vwin = jnp.concatenate([v0[...], v1[...], v2[...]], axis=0)
        valid = _key_valid(b)
        masks = _head_masks()
        dq = jnp.zeros((QB, LANES), F32)
        dkw = jnp.zeros((KW, LANES), F32)
        dvw = jnp.zeros((KW, LANES), F32)
        for hh in range(hpp):
            qm = jnp.where(masks[hh], qv, jnp.zeros_like(qv))
            dom = jnp.where(masks[hh], dov, jnp.zeros_like(dov))
            km = jnp.where(masks[hh], kwin, jnp.zeros_like(kwin))
            lse_h = jnp.max(jnp.where(masks[hh], lsev, NEG), axis=-1, keepdims=True)
            sc = lax.dot_general(qm, kwin, NT, preferred_element_type=F32) + bias_ref[hh]
            sc = jnp.where(valid, sc, NEG)
            p = jnp.exp(sc - lse_h)
            dp = lax.dot_general(dom, vwin, NT, preferred_element_type=F32)
            ds = p * (dp - jnp.sum(dp * p, axis=-1, keepdims=True))
            db_ref[hh] += ds
            dsb = ds.astype(BF16)
            dq = dq + jnp.dot(dsb, km, preferred_element_type=F32)
            dkw = dkw + lax.dot_general(dsb, qm, TN, preferred_element_type=F32)
            dvw = dvw + lax.dot_general(p.astype(BF16), dom, TN, preferred_element_type=F32)
        dq_ref[...] = (dq * (HEAD_DIM ** -0.5)).astype(BF16)
        for w in range(3):
            start = pl.multiple_of(jnp.maximum(b - 2 + w, 0) * QB, QB)
            dk_ref[pl.ds(start, QB), :] += dkw[w * QB:(w + 1) * QB, :]
            dv_ref[pl.ds(start, QB), :] += dvw[w * QB:(w + 1) * QB, :]

    blk = pl.BlockSpec((QB, LANES), lambda p, b: (b, p))
    col = pl.BlockSpec((s, LANES), lambda p, b: (0, p))
    pair = pl.BlockSpec((hpp, QB, KW), lambda p, b: (p, 0, 0))
    return pl.pallas_call(
        body, name="attn_bwd", grid=(npair, nb),
        in_specs=[blk] + _window_specs() + _window_specs() + [blk, blk, pair],
        out_specs=(blk, col, col, pair),
        out_shape=(jax.ShapeDtypeStruct((s, d), BF16), jax.ShapeDtypeStruct((s, d), F32),
                   jax.ShapeDtypeStruct((s, d), F32), jax.ShapeDtypeStruct(bias.shape, F32)),
        compiler_params=_params(VMEM_BIG),
    )(q, k, k, k, v, v, v, do, lse, bias)


def _qkv_bwd(dres, dq, dk, dv, x, tab, wq, wkv, tm):
    s, d = x.shape
    nsh, _, kw = wkv.shape
    nh = nsh // 2
    base = 6

    def body(dres_ref, dq_ref, dk_ref, dv_ref, x_ref, t_ref, wq_ref, wkv_ref, dx_ref, dkv_ref, sums_ref):
        @pl.when(pl.program_id(0) == 0)
        def _():
            sums_ref[...] = jnp.zeros_like(sums_ref)

        dh1 = lax.dot_general(dq_ref[...], wq_ref[...], NT, preferred_element_type=F32)
        dkv_ref[:, 0:d] = dk_ref[...].astype(BF16)
        dkv_ref[:, d:2 * d] = dv_ref[...].astype(BF16)
        dhkv = jnp.zeros((tm, d), F32)
        for j in range(nsh):
            dhkv = dhkv + lax.dot_general(dkv_ref[:, j * kw:(j + 1) * kw], wkv_ref[j], NT,
                                          preferred_element_type=F32)
        xv = x_ref[...]
        r = _rs(xv)
        n = xv * r
        sums_ref[0:1, :] += _colsum(dh1 * n)
        sums_ref[1:2, :] += _colsum(dh1)
        sums_ref[2:3, :] += _colsum(dhkv * n)
        sums_ref[3:4, :] += _colsum(dhkv)
        dn = dh1 * t_ref[base + R_W1:base + R_W1 + 1, :] + dhkv * t_ref[R_KV:R_KV + 1, :]
        dx_ref[...] = dres_ref[...] + _norm_bwd(dn, n, r)

    return pl.pallas_call(
        body, name="qkv_bwd", grid=(s // tm,),
        in_specs=[_rows(tm, d)] * 5 + [_const(tab.shape), _const(wq.shape), _const(wkv.shape)],
        out_specs=(_rows(tm, d), _rows(tm, 2 * d), pl.BlockSpec((8, d), lambda i: (0, 0))),
        out_shape=(jax.ShapeDtypeStruct((s, d), F32), jax.ShapeDtypeStruct((s, 2 * d), BF16),
                   jax.ShapeDtypeStruct((8, d), F32)),
        compiler_params=_params(VMEM_BIG),
    )(dres, dq, dk, dv, x, tab, wq, wkv)


def _conv_bwd(dx1, x, y, bcx, tab, ck, wci, wco, tm):
    s, d = x.shape
    nsh, _, cw = wci.shape
    nt = s // tm

    def rev(i):
        return (nt - 1 - i, 0)

    def halo(i):
        return (jnp.maximum((nt - 1 - i) * (tm // 8) - 1, 0), 0)

    def body(dx_ref, x_ref, y_ref, bcx_ref, halo_ref, t_ref, ck_ref, wci_ref, wco_ref,
             dx0_ref, dyb_ref, dbcx_ref, sums_ref, dck_ref, carry):
        i = pl.program_id(0)

        @pl.when(i == 0)
        def _():
            sums_ref[...] = jnp.zeros_like(sums_ref)
            dck_ref[...] = jnp.zeros_like(dck_ref)
            carry[...] = jnp.zeros_like(carry)

        dxv = dx_ref[...]
        yv = y_ref[...]
        ry = _rs(yv)
        ny = yv * ry
        sums_ref[R_P1:R_P1 + 1, :] += _colsum(dxv * ny)
        dyb = _norm_bwd(dxv * t_ref[R_P1:R_P1 + 1, :], ny, ry).astype(BF16)
        dyb_ref[...] = dyb
        du = lax.dot_general(dyb, wco_ref[...], NT, preferred_element_type=F32)
        bg, cg, xi = bcx_ref[:, 0:d], bcx_ref[:, d:2 * d], bcx_ref[:, 2 * d:3 * d]
        z = cg * xi
        zp = halo_ref[:, d:2 * d] * halo_ref[:, 2 * d:3 * d]
        zp = jnp.where(i == nt - 1, jnp.zeros_like(zp), zp)
        row = lax.broadcasted_iota(jnp.int32, z.shape, 0)
        c1, c2 = zp[7:8, :], zp[6:7, :]
        z1 = jnp.where(row == 0, c1, pltpu.roll(z, 1, 0))
        z2 = jnp.where(row == 0, c2, jnp.where(row == 1, c1, pltpu.roll(z, 2, 0)))
        k0, k1, k2 = ck_ref[0:1, :], ck_ref[1:2, :], ck_ref[2:3, :]
        conv = k0 * z2 + k1 * z1 + k2 * z
        dconv = du * bg
        dck_ref[0:1, :] += _colsum(dconv * z2)
        dck_ref[1:2, :] += _colsum(dconv * z1)
        dck_ref[2:3, :] += _colsum(dconv * z)
        n1, n2 = carry[0:1, :], carry[1:2, :]
        d1 = jnp.where(row == tm - 1, n1, pltpu.roll(dconv, tm - 1, 0))
        d2 = jnp.where(row == tm - 1, n2, jnp.where(row == tm - 2, n1, pltpu.roll(dconv, tm - 2, 0)))
        carry[...] = dconv[0:8, :]
        dz = k2 * dconv + k1 * d1 + k0 * d2
        dbcx_ref[:, 0:d] = (du * conv).astype(BF16)
        dbcx_ref[:, d:2 * d] = (dz * xi).astype(BF16)
        dbcx_ref[:, 2 * d:3 * d] = (dz * cg).astype(BF16)
        dh = jnp.zeros((tm, d), F32)
        for j in range(nsh):
            dh = dh + lax.dot_general(dbcx_ref[:, j * cw:(j + 1) * cw], wci_ref[j], NT,
                                      preferred_element_type=F32)
        xv = x_ref[...]
        r = _rs(xv)
        n = xv * r
        sums_ref[R_W1:R_W1 + 1, :] += _colsum(dh * n)
        sums_ref[R_SH1:R_SH1 + 1, :] += _colsum(dh)
        dx0_ref[...] = dxv + _norm_bwd(dh * t_ref[R_W1:R_W1 + 1, :], n, r)

    rrow = lambda cols: pl.BlockSpec((tm, cols), rev)
    acc = pl.BlockSpec((8, d), lambda i: (0, 0))
    return pl.pallas_call(
        body, name="conv_bwd", grid=(nt,),
        in_specs=[rrow(d), rrow(d), rrow(d), rrow(3 * d), pl.BlockSpec((8, 3 * d), halo),
                  _const(tab.shape), _const(ck.shape), _const(wci.shape), _const(wco.shape)],
        out_specs=(rrow(d), rrow(d), rrow(3 * d), acc, acc),
        out_shape=(jax.ShapeDtypeStruct((s, d), F32), jax.ShapeDtypeStruct((s, d), BF16),
                   jax.ShapeDtypeStruct((s, 3 * d), BF16), jax.ShapeDtypeStruct((8, d), F32),
                   jax.ShapeDtypeStruct((8, d), F32)),
        scratch_shapes=[pltpu.VMEM((8, d), F32)],
        compiler_params=_params(VMEM_BIG),
    )(dx1, x, y, bcx, bcx, tab, ck, wci, wco)


def _wgrad(a, b, nblk, a_spec, b_spec, m, n, tk, name):
    s = a.shape[-2]
    nk = s // tk

    def body(a_ref, b_ref, o_ref, acc):
        kk = pl.program_id(1)

        @pl.when(kk == 0)
        def _():
            acc[...] = jnp.zeros_like(acc)

        acc[...] += lax.dot_general(a_ref[...], b_ref[...], TN, preferred_element_type=F32)

        @pl.when(kk == nk - 1)
        def _():
            o_ref[...] = acc[...].astype(BF16)

    return pl.pallas_call(
        body, name=name, grid=(nblk, nk),
        in_specs=[a_spec, b_spec],
        out_specs=pl.BlockSpec((None, m, n), lambda j, kk: (j, 0, 0)),
        out_shape=jax.ShapeDtypeStruct((nblk, m, n), BF16),
        scratch_shapes=[pltpu.VMEM((m, n), F32)],
        compiler_params=_params(VMEM_BIG),
    )(a, b)


def _wgrad_wide(a, b, nblk, tk, name):
    s, m = a.shape
    n = b.shape[1] // nblk
    nk = s // tk

    def body(a_ref, b_ref, o_ref, acc):
        kk = pl.program_id(0)

        @pl.when(kk == 0)
        def _():
            acc[...] = jnp.zeros_like(acc)

        acc[...] += jnp.dot(a_ref[...].T, b_ref[...], preferred_element_type=F32)

        @pl.when(kk == nk - 1)
        def _():
            for j in range(nblk):
                o_ref[j] = acc[:, j * n:(j + 1) * n].astype(BF16)

    return pl.pallas_call(
        body, name=name, grid=(nk,),
        in_specs=[pl.BlockSpec((tk, m), lambda kk: (kk, 0)), pl.BlockSpec((tk, nblk * n), lambda kk: (kk, 0))],
        out_specs=pl.BlockSpec((nblk, m, n), lambda kk: (0, 0, 0)),
        out_shape=jax.ShapeDtypeStruct((nblk, m, n), BF16),
        scratch_shapes=[pltpu.VMEM((m, nblk * n), F32)],
        compiler_params=_params(VMEM_BIG),
    )(a, b)


def _wgrad_astack(a, b, tk, name):
    nblk, _, m = a.shape
    n = b.shape[1]
    return _wgrad(a, b, nblk, pl.BlockSpec((None, tk, m), lambda j, kk: (j, kk, 0)),
                  pl.BlockSpec((tk, n), lambda j, kk: (kk, 0)), m, n, tk, name)


def _adamw_math(w, g, m, v):
    m = ADAM_B1 * m + (1.0 - ADAM_B1) * g
    v = ADAM_B2 * v + (1.0 - ADAM_B2) * (g * g)
    m_hat = m / (1.0 - ADAM_B1 ** ADAM_STEP)
    v_hat = v / (1.0 - ADAM_B2 ** ADAM_STEP)
    delta = -ADAM_LR * (m_hat / (jnp.sqrt(v_hat) + ADAM_EPS) + ADAM_WD * w)
    return delta, m, v


def _adamw_reduce(parts, w, m, v, tr, name):
    nl, r, c = w.shape
    tr = _row_tile(r, tr)

    def body(*refs):
        p_refs = refs[:nl]
        w_ref, m_ref, v_ref, g_ref, d_ref, mo_ref, vo_ref = refs[nl:]
        layer = pl.program_id(0)

        def partial(i):
            val = p_refs[0][i].astype(F32)
            for q in range(1, nl):
                val = jnp.where(layer == q, p_refs[q][i].astype(F32), val)
            return val

        g = partial(0)
        for i in range(1, N_DEV):
            g = g + partial(i)
        g_ref[...] = g
        d_ref[...], mo_ref[...], vo_ref[...] = _adamw_math(w_ref[...], g, m_ref[...], v_ref[...])

    blk = pl.BlockSpec((None, tr, c), lambda l, i: (l, i, 0))
    out = jax.ShapeDtypeStruct((nl, r, c), F32)
    p_specs = [pl.BlockSpec((N_DEV, tr, c), (lambda l, i, q=q: (0, jnp.where(l == q, i, 0), 0))) for q in range(nl)]
    return pl.pallas_call(
        body, name=name, grid=(nl, r // tr),
        in_specs=p_specs + [blk, blk, blk],
        out_specs=(blk,) * 4, out_shape=(out,) * 4,
        compiler_params=_params(VMEM_BIG),
    )(*parts, w, m, v)


def _adamw_outer(sct, dm, w, m, v, tr, name):
    nl, d, c = w.shape

    def body(s_ref, dm_ref, w_ref, m_ref, v_ref, g_ref, d_ref, mo_ref, vo_ref):
        g = jnp.dot(s_ref[...], dm_ref[...], preferred_element_type=F32)
        g_ref[...] = g
        d_ref[...], mo_ref[...], vo_ref[...] = _adamw_math(w_ref[...], g, m_ref[...], v_ref[...])

    blk = pl.BlockSpec((None, tr, c), lambda l, i: (l, i, 0))
    out = jax.ShapeDtypeStruct((nl, d, c), F32)
    return pl.pallas_call(
        body, name=name, grid=(nl, d // tr),
        in_specs=[pl.BlockSpec((tr, N_DEV), lambda l, i: (i, 0)),
                  pl.BlockSpec((None, N_DEV, c), lambda l, i: (l, 0, 0)), blk, blk, blk],
        out_specs=(blk,) * 4, out_shape=(out,) * 4,
        compiler_params=_params(VMEM_BIG),
    )(sct, dm, w, m, v)


def _pad_rows(a, rows):
    return jnp.concatenate([a, jnp.zeros((rows - a.shape[0],) + a.shape[1:], a.dtype)], axis=0)


def kernel(x, c, mod_w, mod_b, norm_g, ffn_w_in, ffn_w_out, conv_w_in, conv_k, conv_w_out, kv_mod_w, kv_mod_b, kv_norm_g, w_kv, attn_w_q, attn_w_o, rel_bias, loss_target, m_mod_w, m_mod_b, m_norm_g, m_ffn_w_in, m_ffn_w_out, m_conv_w_in, m_conv_k, m_conv_w_out, m_kv_mod_w, m_kv_mod_b, m_kv_norm_g, m_w_kv, m_attn_w_q, m_attn_w_o, m_rel_bias, v_mod_w, v_mod_b, v_norm_g, v_ffn_w_in, v_ffn_w_out, v_conv_w_in, v_conv_k, v_conv_w_out, v_kv_mod_w, v_kv_mod_b, v_kv_norm_g, v_w_kv, v_attn_w_q, v_attn_w_o, v_rel_bias):
    s, d = x.shape[1], x.shape[2]
    dq = d // LANES
    dsh = d // N_DEV
    nl = mod_w.shape[0]
    mw = mod_w.shape[2]
    kmw = kv_mod_w.shape[1]
    fw = ffn_w_in.shape[2]
    nh, nrel = rel_bias.shape[1], rel_bias.shape[2]
    tm = min(256, s)
    tm2 = min(512, s)
    tk = min(1024, s)
    tk2 = min(2048, s)
    me = 4 * lax.axis_index("x") + 2 * lax.axis_index("y") + lax.axis_index("c")

    x0 = x[0]
    tgt = loss_target[0]

    small1 = jnp.concatenate([c.reshape(dq, LANES), norm_g.reshape(dq, LANES),
                              _pad_rows(conv_k[0], 8).reshape(dq, LANES)], axis=0)
    (sm,) = _exchange([small1], ["gather"], "gather_small")
    c_all = sm[:, 0:dq].reshape(N_DEV, d)
    ng_full = jnp.transpose(sm[:, dq:2 * dq].reshape(N_DEV, 8, dsh), (1, 0, 2)).reshape(8, d)
    ck_full = jnp.transpose(sm[:, 2 * dq:3 * dq].reshape(N_DEV, 8, dsh), (1, 0, 2)).reshape(8, d)

    modcols, silu_c = _mod_fwd(c_all, mod_w, kv_mod_w)
    (modall,) = _exchange([modcols], ["gather"], "gather_mod")

    cast = lambda *ws: [a.astype(BF16) for a in ws]
    gath = lambda ws: (ws, ["gather"] * len(ws))
    (h_conv, h_ffn0, h_attn, h_ffn1), token = _xstart(
        [gath(cast(conv_w_in[0], conv_w_out[0])), gath(cast(jnp.swapaxes(ffn_w_in[0], 0, 1), ffn_w_out[0])),
         gath(cast(w_kv, attn_w_q[0], attn_w_o[0])), gath(cast(jnp.swapaxes(ffn_w_in[1], 0, 1), ffn_w_out[1]))],
        modall, "gather_start")
    modall = modall + token[0, 0]
    mine = lax.dynamic_index_in_dim(modall, me, axis=1, keepdims=False)
    modrow = jnp.stack([mine[:, l * mw:(l + 1) * mw].reshape(6, d) for l in range(nl)])
    kvrow = mine[:, nl * mw:nl * mw + kmw].reshape(2, d)
    tab, modval = _vec_prep(modrow, mod_b.reshape(nl, 6, d), kvrow, kv_mod_b.reshape(2, d), ng_full,
                            kv_norm_g.reshape(1, d))
    bias = _bias_fwd(rel_bias[0])

    wci, wco = _xwait(h_conv, [bias], "gather_wait_conv")
    wco = wco.reshape(d, d)
    x1, h1a, bcx, ua, ya = _conv_fwd(x0, tab, ck_full, wci, wco, tm2)
    wfi0, wfo0 = _xwait(h_ffn0, [x1], "gather_wait_ffn0")
    wfo0 = wfo0.reshape(N_DEV // 2, -1, d)
    x2, h2a, gua, y2a = _ffn_fwd(x1, tab, 0, wfi0, wfo0, None, tm2, "ffn_fwd0")
    wkv, wq, wo = _xwait(h_attn, [x2], "gather_wait_attn")
    wq, wo = wq.reshape(d, d), wo.reshape(d, d)
    hkv, h1b, q, k, v = _qkv_fwd(x2, tab, wq, wkv, tm2)
    o, lse = _attn_fwd(q, k, v, bias)
    x3, yb = _attn_out_fwd(o, x2, tab, wo, tm2)
    wfi1, wfo1 = _xwait(h_ffn1, [x3], "gather_wait_ffn1")
    wfo1 = wfo1.reshape(N_DEV // 2, -1, d)
    dx4, h2b, gub, y2b, loss_acc = _ffn_fwd(x3, tab, 6, wfi1, wfo1, tgt, tm2, "ffn_fwd1")

    scat = lambda ws: [(ws, ["scatter"] * len(ws))]
    dx3, dy2b, dgub, ab, sums_f1 = _ffn_bwd(dx4, x3, y2b, gub, tab, 6, wfi1, wfo1, tm, "ffn_bwd1")
    g_wfi1 = _wgrad_astack(dgub, h2b, tk2, "wgrad_ffn_in1")
    g_wfo1 = _wgrad_astack(ab, dy2b, tk2, "wgrad_ffn_out1").reshape(N_DEV, -1, d)
    (h_g1,), token = _xstart(scat([g_wfi1, g_wfo1]), dx3, "grads_start_ffn1")
    tab = tab + token[0, 0]
    dyb, do, sums_o = _attn_out_bwd(dx3, yb, tab, wo, tm2)
    g_wo = _wgrad_wide(o, dyb, 1, tk, "wgrad_o").reshape(N_DEV, dsh, d)
    dqb, dk, dv, dbias = _attn_bwd(q, k, v, do, lse, bias)
    g_wq = _wgrad_wide(h1b, dqb, 1, tk, "wgrad_q").reshape(N_DEV, dsh, d)
    dx2, dkvb, sums_q = _qkv_bwd(dx3, dqb, dk, dv, x2, tab, wq, wkv, tm2)
    g_wkv = _wgrad_wide(hkv, dkvb, N_DEV, tk, "wgrad_kv")
    (h_g2,), token = _xstart(scat([g_wkv, g_wq, g_wo]), dx2, "grads_start_attn")
    tab = tab + token[0, 0]
    dx1, dy2a, dgua, aa, sums_f0 = _ffn_bwd(dx2, x1, y2a, gua, tab, 0, wfi0, wfo0, tm, "ffn_bwd0")
    g_wfi0 = _wgrad_astack(dgua, h2a, tk2, "wgrad_ffn_in0")
    g_wfo0 = _wgrad_astack(aa, dy2a, tk2, "wgrad_ffn_out0").reshape(N_DEV, -1, d)
    (h_g3,), token = _xstart(scat([g_wfi0, g_wfo0]), dx1, "grads_start_ffn0")
    tab = tab + token[0, 0]
    dx0, dya, dbcx, sums_c, dck = _conv_bwd(dx1, x0, ya, bcx, tab, ck_full, wci, wco, tm2)
    drel = _bias_bwd(dbias, nrel)
    dmod, dng, dkvg = _vec_bwd(sums_c, sums_f0, sums_q, sums_o, sums_f1, modval, ng_full, kv_norm_g.reshape(1, d))

    relw = -(-nrel // LANES) * LANES
    drel_p = jnp.concatenate([drel, jnp.zeros((nh, relw - nrel), F32)], axis=1)
    small3 = jnp.concatenate([dmod.reshape(16 * dq, LANES), dng.reshape(8 * dq, LANES), dkvg.reshape(8 * dq, LANES),
                              dck.reshape(8 * dq, LANES), loss_acc,
                              drel_p.reshape(nh * relw // LANES, LANES)], axis=0)
    (sm,) = _exchange([small3], ["gather"], "gather_small_grads")
    g_wci = _wgrad_wide(h1a, dbcx, N_DEV, tk, "wgrad_conv_in")
    g_wco = _wgrad_wide(ua, dya, 1, tk, "wgrad_conv_out").reshape(N_DEV, dsh, d)
    (h_g4,), token = _xstart(scat([g_wci, g_wco]), sm, "grads_start_conv")
    sm = sm + token[0, 0]
    o1, o2, o3, o4, o5 = 16 * dq, 24 * dq, 32 * dq, 40 * dq, 40 * dq + 8
    loss = jnp.sum(sm[:, o4, 0]) * (0.5 / d)
    dmod_all = sm[:, 0:o1].reshape(N_DEV, 16, d)
    mine_cols = lambda a: lax.dynamic_slice_in_dim(a, me * dsh, dsh, axis=2)
    dng_parts = mine_cols(sm[:, o1:o2].reshape(N_DEV, 8, d))
    dkvg_parts = sm[:, o2:o3].reshape(N_DEV, 8, d)[:, 0:1]
    dck_parts = mine_cols(sm[:, o3:o4].reshape(N_DEV, 8, d))[:, 0:3]
    drel_parts = sm[:, o5:].reshape(N_DEV, nh, relw)[:, :, 0:nrel]

    def update(parts, w, m, v, name, layers=1):
        shp = w.shape
        w3, m3, v3 = (a.reshape(layers, -1, shp[-1]) for a in (w, m, v))
        outs = _adamw_reduce([p.reshape(N_DEV, -1, shp[-1]) for p in parts], w3, m3, v3, 256, name)
        return [a.reshape(shp) for a in outs]

    p_wfi1, p_wfo1 = _xwait(h_g1, [sm], "grads_wait_ffn1")
    p_wfi0, p_wfo0 = _xwait(h_g3, [p_wfi1], "grads_wait_ffn0")
    tr = lambda a: jnp.swapaxes(a, 1, 2)
    u_ffn_in = [tr(a) for a in update([p_wfi0, p_wfi1], tr(ffn_w_in), tr(m_ffn_w_in), tr(v_ffn_w_in),
                                      "adamw_ffn_in", 2)]
    u_ffn_out = update([p_wfo0, p_wfo1], ffn_w_out, m_ffn_w_out, v_ffn_w_out, "adamw_ffn_out", 2)
    p_wkv, p_wq, p_wo = _xwait(h_g2, [u_ffn_out[0]], "grads_wait_attn")
    u_w_kv = update([p_wkv], w_kv, m_w_kv, v_w_kv, "adamw_w_kv")
    u_w_q = update([p_wq], attn_w_q, m_attn_w_q, v_attn_w_q, "adamw_w_q")
    u_w_o = update([p_wo], attn_w_o, m_attn_w_o, v_attn_w_o, "adamw_w_o")

    sct = jnp.transpose(silu_c)
    dm_mod = jnp.stack([lax.dynamic_slice_in_dim(dmod_all[:, 6 * l:6 * l + 6].reshape(N_DEV, 6 * d), me * mw, mw, axis=1)
                        for l in range(nl)]).astype(BF16)
    dm_kv = lax.dynamic_slice_in_dim(dmod_all[:, R_KV:R_KV + 2].reshape(N_DEV, 2 * d), me * kmw, kmw, axis=1)
    u_mod_w = _adamw_outer(sct, dm_mod, mod_w, m_mod_w, v_mod_w, min(256, d), "adamw_mod_w")
    u_kv_mod_w = [a[0] for a in _adamw_outer(sct, dm_kv.astype(BF16)[None], kv_mod_w[None], m_kv_mod_w[None],
                                             v_kv_mod_w[None], min(256, d), "adamw_kv_mod_w")]

    modb_parts = jnp.stack([dmod_all[:, 6 * l:6 * l + 6].reshape(N_DEV, 6 * d) for l in range(nl)], axis=1)
    u_mod_b = update([modb_parts], mod_b, m_mod_b, v_mod_b, "adamw_mod_b")
    u_norm_g = update([dng_parts], norm_g.reshape(8, dsh), m_norm_g.reshape(8, dsh), v_norm_g.reshape(8, dsh), "adamw_norm_g")
    u_norm_g = [a.reshape(norm_g.shape) for a in u_norm_g]
    u_conv_k = update([dck_parts], conv_k, m_conv_k, v_conv_k, "adamw_conv_k")
    kvb_parts = dmod_all[:, R_KV:R_KV + 2].reshape(N_DEV, 1, 2 * d)
    u_kv_mod_b = [a.reshape(kv_mod_b.shape) for a in update([kvb_parts], kv_mod_b.reshape(1, -1), m_kv_mod_b.reshape(1, -1),
                                                            v_kv_mod_b.reshape(1, -1), "adamw_kv_mod_b")]
    u_kv_norm_g = [a.reshape(kv_norm_g.shape) for a in update([dkvg_parts], kv_norm_g.reshape(1, -1), m_kv_norm_g.reshape(1, -1),
                                                              v_kv_norm_g.reshape(1, -1), "adamw_kv_norm_g")]
    u_rel = update([drel_parts], rel_bias, m_rel_bias, v_rel_bias, "adamw_rel_bias")

    others = [u_ffn_in, u_ffn_out, u_w_kv, u_w_q, u_w_o, u_mod_w, u_kv_mod_w, u_mod_b, u_norm_g, u_conv_k, u_kv_mod_b,
              u_kv_norm_g, u_rel]
    p_wci, p_wco = _xwait(h_g4, [u[3] for u in others], "grads_wait_conv")
    u_conv_in = update([p_wci], conv_w_in, m_conv_w_in, v_conv_w_in, "adamw_conv_in")
    u_conv_out = update([p_wco], conv_w_out, m_conv_w_out, v_conv_w_out, "adamw_conv_out")

    ups = [u_mod_w, u_mod_b, u_norm_g, u_ffn_in, u_ffn_out, u_conv_in, u_conv_k, u_conv_out, u_kv_mod_w, u_kv_mod_b,
           u_kv_norm_g, u_w_kv, u_w_q, u_w_o, u_rel]
    return (loss, dx0[None], *[u[0] for u in ups], *[u[1] for u in ups], *[u[2] for u in ups], *[u[3] for u in ups])
```

```python
import jax
import jax.numpy as jnp
from jax import lax
from jax.experimental import pallas as pl
from jax.experimental.pallas import tpu as pltpu

F32 = jnp.float32
BF16 = jnp.bfloat16

EPS = 1e-6
CHUNK = 64
HEAD_DIM = 64
N_LEFT = 8
LANES = 128
QB = 4 * CHUNK
KW = QB + N_LEFT * CHUNK
BIAS_VARIANTS = N_LEFT * CHUNK // QB + 1
NEG = -1e30
N_DEV = 8

ADAM_LR = 0.001
ADAM_B1 = 0.9
ADAM_B2 = 0.999
ADAM_EPS = 1e-08
ADAM_WD = 0.01
ADAM_STEP = 10

VMEM_BIG = 56 * 1024 * 1024

NT = (((1,), (1,)), ((), ()))
TN = (((0,), (0,)), ((), ()))

R_W1, R_SH1, R_P1, R_W2, R_SH2, R_P2 = range(6)
R_KV = 12


def _params(vmem):
    return pltpu.CompilerParams(vmem_limit_bytes=vmem)


def _row_tile(rows, cap):
    for t in range(min(cap, rows) // 16 * 16, 0, -16):
        if rows % t == 0:
            return t
    return rows


def _rows(tm, cols):
    return pl.BlockSpec((tm, cols), lambda i: (i, 0))


def _const(shape):
    nd = len(shape)
    return pl.BlockSpec(shape, lambda *_: (0,) * nd, pipeline_mode=pl.Buffered(1))


def _rs(x):
    return lax.rsqrt(jnp.mean(x * x, axis=-1, keepdims=True) + EPS)


def _norm_bwd(d, n, r):
    return r * (d - n * jnp.mean(d * n, axis=-1, keepdims=True))


def _colsum(a):
    return jnp.sum(a, axis=0, keepdims=True)


def _sigmoid(g):
    return 1.0 / (1.0 + jnp.exp(-g))


def _exchange(arrays, modes, name):
    n = len(arrays)
    out_shape = []
    for a, mode in zip(arrays, modes):
        shp = (N_DEV,) + a.shape if mode == "gather" else a.shape
        out_shape.append(jax.ShapeDtypeStruct(shp, a.dtype))

    def body(*refs):
        ins, outs = refs[:n], refs[n:2 * n]
        send_sems, recv_sems, local_sems = refs[2 * n:]
        x, y, c = lax.axis_index("x"), lax.axis_index("y"), lax.axis_index("c")
        me = 4 * x + 2 * y + c
        local, sends, recvs = [], [], []
        for a in range(n):
            own = ins[a] if modes[a] == "gather" else ins[a].at[me]
            cp = pltpu.make_async_copy(own, outs[a].at[me], local_sems.at[a])
            cp.start()
            local.append(cp)
        for k in range(1, N_DEV):
            px = 1 - x if k & 4 else x
            py = 1 - y if k & 2 else y
            pc = 1 - c if k & 1 else c
            peer = 4 * px + 2 * py + pc
            for a in range(n):
                src = ins[a] if modes[a] == "gather" else ins[a].at[peer]
                sem = a * (N_DEV - 1) + k - 1
                cp = pltpu.make_async_remote_copy(
                    src_ref=src, dst_ref=outs[a].at[me],
                    send_sem=send_sems.at[sem], recv_sem=recv_sems.at[sem],
                    device_id=(px, py, pc), device_id_type=pl.DeviceIdType.MESH)
                cp.start()
                sends.append(cp)
                recvs.append(pltpu.make_async_remote_copy(
                    src_ref=src, dst_ref=outs[a].at[peer],
                    send_sem=send_sems.at[sem], recv_sem=recv_sems.at[sem],
                    device_id=(px, py, pc), device_id_type=pl.DeviceIdType.MESH))
        for cp in recvs:
            cp.wait_recv()
        for cp in sends:
            cp.wait_send()
        for cp in local:
            cp.wait()

    any_spec = pl.BlockSpec(memory_space=pl.ANY)
    return pl.pallas_call(
        body, name=name,
        out_shape=tuple(out_shape),
        in_specs=[any_spec] * n,
        out_specs=tuple([any_spec] * n),
        scratch_shapes=[
            pltpu.SemaphoreType.DMA((n * (N_DEV - 1),)),
            pltpu.SemaphoreType.DMA((n * (N_DEV - 1),)),
            pltpu.SemaphoreType.DMA((n,)),
        ],
    )(*arrays)


def _peers(x, y, c):
    out = []
    for k in range(1, N_DEV):
        px = 1 - x if k & 4 else x
        py = 1 - y if k & 2 else y
        pc = 1 - c if k & 1 else c
        out.append((k - 1, (px, py, pc), 4 * px + 2 * py + pc))
    return out


def _land_shape(a, mode):
    return (N_DEV,) + a.shape if mode == "gather" else a.shape


_HBM = pl.BlockSpec(memory_space=pltpu.HBM)
_SEM = pl.BlockSpec(memory_space=pltpu.SEMAPHORE)
_EFFECT = pltpu.SideEffectType.DATAFLOW_SIDE_EFFECTING


def _xstart(groups, after, name):
    flat = [(a, m) for arrays, modes in groups for a, m in zip(arrays, modes)]
    n, ngr = len(flat), len(groups)
    sizes = [len(arrays) for arrays, _ in groups]
    npeer = N_DEV - 1

    def body(*refs):
        ins, lands = refs[:n], refs[n:2 * n]
        outs = refs[2 * n + 1:]
        sems = outs[:2 * ngr]
        token = outs[2 * ngr + 2 * n]
        local_sems = outs[2 * ngr + 2 * n + 1]
        stage = outs[2 * ngr + 2 * n + 2:]
        x, y, c = lax.axis_index("x"), lax.axis_index("y"), lax.axis_index("c")
        me = 4 * x + 2 * y + c
        loads, stores = [], []
        for a in range(n):
            own = ins[a] if flat[a][1] == "gather" else ins[a].at[me]
            loads.append(pltpu.make_async_copy(own, stage[a], local_sems.at[a]))
            stores.append(pltpu.make_async_copy(stage[a], lands[a].at[me], local_sems.at[a]))
            loads[a].start()
        for a in range(n):
            loads[a].wait()
            stores[a].start()
        a = 0
        for g in range(ngr):
            for j in range(sizes[g]):
                mode = flat[a][1]
                for slot, peer, pidx in _peers(x, y, c):
                    pltpu.make_async_remote_copy(
                        src_ref=ins[a] if mode == "gather" else ins[a].at[pidx], dst_ref=lands[a].at[me],
                        send_sem=sems[2 * g].at[j * npeer + slot], recv_sem=sems[2 * g + 1].at[j * npeer + slot],
                        device_id=peer, device_id_type=pl.DeviceIdType.MESH).start()
                a += 1
        for cp in stores:
            cp.wait()
        token[...] = jnp.zeros_like(token)

    out_shape, out_specs = [], []
    for sz in sizes:
        out_shape += [pltpu.SemaphoreType.DMA((sz * npeer,)), pltpu.SemaphoreType.DMA((sz * npeer,))]
        out_specs += [_SEM, _SEM]
    out_shape += [pltpu.HBM(a.shape, a.dtype) for a, _ in flat]
    out_shape += [pltpu.HBM(_land_shape(a, m), a.dtype) for a, m in flat]
    out_specs += [_HBM] * (2 * n)
    out_shape.append(jax.ShapeDtypeStruct((8, LANES), F32))
    out_specs.append(pl.BlockSpec(memory_space=pltpu.VMEM))
    args = [pltpu.with_memory_space_constraint(a, pltpu.HBM) for a, _ in flat]
    args += [pltpu.with_memory_space_constraint(lax.empty(_land_shape(a, m), a.dtype), pltpu.HBM) for a, m in flat]
    res = pl.pallas_call(
        body, name=name, out_shape=tuple(out_shape),
        in_specs=[_HBM] * (2 * n) + [pl.BlockSpec(memory_space=pl.ANY)], out_specs=tuple(out_specs),
        input_output_aliases={i: 2 * ngr + i for i in range(2 * n)},
        scratch_shapes=[pltpu.SemaphoreType.DMA((n,))]
                       + [pltpu.VMEM(a.shape if m == "gather" else a.shape[1:], a.dtype) for a, m in flat],
        compiler_params=pltpu.CompilerParams(has_side_effects=_EFFECT, vmem_limit_bytes=VMEM_BIG),
    )(*args, after)
    handles, a = [], 0
    for g, sz in enumerate(sizes):
        handles.append((res[2 * g], res[2 * g + 1], list(res[2 * ngr + a:2 * ngr + a + sz]),
                        list(res[2 * ngr + n + a:2 * ngr + n + a + sz]), list(groups[g][1])))
        a += sz
    return handles, res[-1]


def _xwait(handle, after, name):
    send_sems, recv_sems, srcs, lands, modes = handle
    m = len(srcs)
    npeer = N_DEV - 1
    after = list(after)

    def body(*refs):
        ins, lnd = refs[:m], refs[m:2 * m]
        ssem, rsem = refs[2 * m], refs[2 * m + 1]
        x, y, c = lax.axis_index("x"), lax.axis_index("y"), lax.axis_index("c")
        for j in range(m):
            for slot, peer, pidx in _peers(x, y, c):
                cp = pltpu.make_async_remote_copy(
                    src_ref=ins[j] if modes[j] == "gather" else ins[j].at[pidx], dst_ref=lnd[j].at[pidx],
                    send_sem=ssem.at[j * npeer + slot], recv_sem=rsem.at[j * npeer + slot],
                    device_id=peer, device_id_type=pl.DeviceIdType.MESH)
                cp.wait_send()
                cp.wait_recv()

    res = pl.pallas_call(
        body, name=name,
        out_shape=tuple([pltpu.HBM(a.shape, a.dtype) for a in srcs] + [pltpu.HBM(a.shape, a.dtype) for a in lands]),
        in_specs=[_HBM] * (2 * m) + [_SEM, _SEM] + [pl.BlockSpec(memory_space=pl.ANY)] * len(after),
        out_specs=tuple([_HBM] * (2 * m)),
        input_output_aliases={i: i for i in range(2 * m)},
        compiler_params=pltpu.CompilerParams(has_side_effects=_EFFECT),
    )(*srcs, *lands, send_sems, recv_sems, *after)
    return list(res[m:])


def _mod_fwd(c_all, mod_w, kv_mod_w):
    nl, d, mw = mod_w.shape
    kw = kv_mod_w.shape[1]

    def body(c_ref, mw_ref, kw_ref, o_ref, sc_ref):
        cc = c_ref[...]
        sc = (cc * _sigmoid(cc)).astype(BF16)
        sc_ref[...] = sc
        for l in range(nl):
            o_ref[:, l * mw:(l + 1) * mw] = jnp.dot(sc, mw_ref[l].astype(BF16), preferred_element_type=F32)
        o_ref[:, nl * mw:nl * mw + kw] = jnp.dot(sc, kw_ref[...].astype(BF16), preferred_element_type=F32)

    return pl.pallas_call(
        body, name="mod_fwd",
        out_shape=(jax.ShapeDtypeStruct((c_all.shape[0], nl * mw + kw), F32),
                   jax.ShapeDtypeStruct(c_all.shape, BF16)),
        compiler_params=_params(VMEM_BIG),
    )(c_all, mod_w, kv_mod_w)


def _vec_prep(modrow, modb, kvrow, kvb, ng, kvg):
    d = ng.shape[1]

    def body(mr_ref, mb_ref, kr_ref, kb_ref, ng_ref, kvg_ref, t_ref, m_ref):
        t_ref[...] = jnp.zeros_like(t_ref)
        m_ref[...] = jnp.zeros_like(m_ref)
        for l in range(2):
            mod = mr_ref[l] + mb_ref[l]
            m_ref[6 * l:6 * l + 6, :] = mod
            g = ng_ref[4 * l:4 * l + 4, :]
            t_ref[6 * l + R_W1:6 * l + R_W1 + 1, :] = g[0:1] * (1.0 + mod[1:2])
            t_ref[6 * l + R_SH1:6 * l + R_SH1 + 1, :] = mod[0:1]
            t_ref[6 * l + R_P1:6 * l + R_P1 + 1, :] = mod[2:3] * g[1:2]
            t_ref[6 * l + R_W2:6 * l + R_W2 + 1, :] = g[2:3] * (1.0 + mod[4:5])
            t_ref[6 * l + R_SH2:6 * l + R_SH2 + 1, :] = mod[3:4]
            t_ref[6 * l + R_P2:6 * l + R_P2 + 1, :] = mod[5:6] * g[3:4]
        kv = kr_ref[...] + kb_ref[...]
        m_ref[R_KV:R_KV + 2, :] = kv
        t_ref[R_KV:R_KV + 1, :] = kvg_ref[...] * (1.0 + kv[1:2])
        t_ref[R_KV + 1:R_KV + 2, :] = kv[0:1]

    return pl.pallas_call(
        body, name="vec_prep",
        out_shape=(jax.ShapeDtypeStruct((16, d), F32), jax.ShapeDtypeStruct((16, d), F32)),
    )(modrow, modb, kvrow, kvb, ng, kvg)


def _vec_bwd(sums_c, sums_f0, sums_q, sums_o, sums_f1, mt, ng, kvg):
    d = ng.shape[1]

    def body(sc_ref, sf0_ref, sq_ref, so_ref, sf1_ref, m_ref, ng_ref, kvg_ref, dm_ref, dng_ref, dkvg_ref, g_ref):
        g_ref[...] = jnp.zeros_like(g_ref)
        g_ref[0:3, :] = sc_ref[0:3, :]
        g_ref[3:6, :] = sf0_ref[3:6, :]
        g_ref[6:8, :] = sq_ref[0:2, :]
        g_ref[8:9, :] = so_ref[2:3, :]
        g_ref[9:12, :] = sf1_ref[3:6, :]
        g_ref[R_KV:R_KV + 2, :] = sq_ref[2:4, :]
        dm_ref[...] = jnp.zeros_like(dm_ref)
        dkvg_ref[...] = jnp.zeros_like(dkvg_ref)
        for l in range(2):
            g = ng_ref[4 * l:4 * l + 4, :]
            mod = m_ref[6 * l:6 * l + 6, :]
            s = g_ref[6 * l:6 * l + 6, :]
            dm_ref[6 * l + 0:6 * l + 1, :] = s[1:2]
            dm_ref[6 * l + 1:6 * l + 2, :] = s[0:1] * g[0:1]
            dm_ref[6 * l + 2:6 * l + 3, :] = s[2:3] * g[1:2]
            dm_ref[6 * l + 3:6 * l + 4, :] = s[4:5]
            dm_ref[6 * l + 4:6 * l + 5, :] = s[3:4] * g[2:3]
            dm_ref[6 * l + 5:6 * l + 6, :] = s[5:6] * g[3:4]
            dng_ref[4 * l + 0:4 * l + 1, :] = s[0:1] * (1.0 + mod[1:2])
            dng_ref[4 * l + 1:4 * l + 2, :] = s[2:3] * mod[2:3]
            dng_ref[4 * l + 2:4 * l + 3, :] = s[3:4] * (1.0 + mod[4:5])
            dng_ref[4 * l + 3:4 * l + 4, :] = s[5:6] * mod[5:6]
        dm_ref[R_KV:R_KV + 1, :] = g_ref[R_KV + 1:R_KV + 2, :]
        dm_ref[R_KV + 1:R_KV + 2, :] = g_ref[R_KV:R_KV + 1, :] * kvg_ref[...]
        dkvg_ref[0:1, :] = g_ref[R_KV:R_KV + 1, :] * (1.0 + m_ref[R_KV + 1:R_KV + 2, :])

    return pl.pallas_call(
        body, name="vec_bwd",
        out_shape=(jax.ShapeDtypeStruct((16, d), F32), jax.ShapeDtypeStruct((8, d), F32),
                   jax.ShapeDtypeStruct((8, d), F32)),
        scratch_shapes=[pltpu.VMEM((16, d), F32)],
    )(sums_c, sums_f0, sums_q, sums_o, sums_f1, mt, ng, kvg)


def _rel_index(nrel):
    width = KW + QB
    e = lax.broadcasted_iota(jnp.int32, (nrel, width), 1)
    r = lax.broadcasted_iota(jnp.int32, (nrel, width), 0)
    max_rel = (nrel - 1) // 2
    idx = jnp.clip(KW - e, -max_rel, max_rel) + max_rel
    return (idx == r).astype(F32)


def _band_valid():
    row = lax.broadcasted_iota(jnp.int32, (QB, KW), 0) // CHUNK
    col = lax.broadcasted_iota(jnp.int32, (QB, KW), 1) // CHUNK
    j = col - row
    return (j >= 0) & (j <= N_LEFT)


def _bias_fwd(rel_bias):
    nh, nrel = rel_bias.shape
    width = KW + QB

    def body(rb_ref, o_ref):
        onehot = _rel_index(nrel)
        gr = jnp.dot(rb_ref[...], onehot, preferred_element_type=F32, precision=lax.Precision.HIGHEST)
        valid = _band_valid() & _key_valid(pl.program_id(0))
        for h in range(nh):
            xrow = jnp.broadcast_to(gr[h:h + 1, :], (QB, width))
            rolled = pltpu.roll(xrow, 0, 1, stride=1, stride_axis=0)
            o_ref[h] = jnp.where(valid, rolled[:, QB:], NEG)

    return pl.pallas_call(
        body, name="bias_fwd", grid=(BIAS_VARIANTS,),
        in_specs=[pl.BlockSpec(rel_bias.shape, lambda v: (0, 0))],
        out_specs=pl.BlockSpec((None, nh, QB, KW), lambda v: (v, 0, 0, 0)),
        out_shape=jax.ShapeDtypeStruct((BIAS_VARIANTS, nh, QB, KW), F32),
        compiler_params=_params(VMEM_BIG),
    )(rel_bias)


def _bias_bwd(dbias, nrel):
    nh = dbias.shape[0]
    width = KW + QB

    def body(db_ref, o_ref, diag_ref):
        onehot = _rel_index(nrel)
        valid = _band_valid()
        rr = lax.broadcasted_iota(jnp.int32, (QB, QB), 0)
        cc = lax.broadcasted_iota(jnp.int32, (QB, QB), 1)
        flip = (rr + cc == QB - 1).astype(F32)
        for h in range(nh):
            rev = jnp.dot(flip, jnp.where(valid, db_ref[h], 0.0), preferred_element_type=F32,
                          precision=lax.Precision.HIGHEST)
            w = jnp.concatenate([jnp.zeros((QB, QB), F32), rev], axis=1)
            back = pltpu.roll(w, width - (QB - 1), 1, stride=1, stride_axis=0)
            diag_ref[h:h + 1, :] = _colsum(back)
        o_ref[...] = lax.dot_general(diag_ref[...], onehot, NT, preferred_element_type=F32,
                                     precision=lax.Precision.HIGHEST)

    return pl.pallas_call(
        body, name="bias_bwd",
        out_shape=jax.ShapeDtypeStruct((nh, nrel), F32),
        scratch_shapes=[pltpu.VMEM((nh, width), F32)],
        compiler_params=_params(VMEM_BIG),
    )(dbias)


def _conv_fwd(x, tab, ck, wci, wco, tm):
    s, d = x.shape
    nsh, _, cw = wci.shape

    def body(x_ref, t_ref, ck_ref, wci_ref, wco_ref, x1_ref, h_ref, bcx_ref, u_ref, y_ref, carry):
        @pl.when(pl.program_id(0) == 0)
        def _():
            carry[...] = jnp.zeros_like(carry)

        xv = x_ref[...]
        hb = ((xv * _rs(xv)) * t_ref[R_W1:R_W1 + 1, :] + t_ref[R_SH1:R_SH1 + 1, :]).astype(BF16)
        h_ref[...] = hb
        for j in range(nsh):
            bcx_ref[:, j * cw:(j + 1) * cw] = jnp.dot(hb, wci_ref[j], preferred_element_type=F32)
        bg, cg, xi = bcx_ref[:, 0:d], bcx_ref[:, d:2 * d], bcx_ref[:, 2 * d:3 * d]
        z = cg * xi
        row = lax.broadcasted_iota(jnp.int32, z.shape, 0)
        c1, c2 = carry[7:8, :], carry[6:7, :]
        z1 = jnp.where(row == 0, c1, pltpu.roll(z, 1, 0))
        z2 = jnp.where(row == 0, c2, jnp.where(row == 1, c1, pltpu.roll(z, 2, 0)))
        carry[...] = z[tm - 8:tm, :]
        conv = ck_ref[0:1, :] * z2 + ck_ref[1:2, :] * z1 + ck_ref[2:3, :] * z
        ub = (bg * conv).astype(BF16)
        u_ref[...] = ub
        yv = jnp.dot(ub, wco_ref[...], preferred_element_type=F32)
        y_ref[...] = yv
        x1_ref[...] = xv + (yv * _rs(yv)) * t_ref[R_P1:R_P1 + 1, :]

    return pl.pallas_call(
        body, name="conv_fwd", grid=(s // tm,),
        in_specs=[_rows(tm, d), _const(tab.shape), _const(ck.shape), _const(wci.shape), _const(wco.shape)],
        out_specs=(_rows(tm, d), _rows(tm, d), _rows(tm, 3 * d), _rows(tm, d), _rows(tm, d)),
        out_shape=(jax.ShapeDtypeStruct((s, d), F32), jax.ShapeDtypeStruct((s, d), BF16),
                   jax.ShapeDtypeStruct((s, 3 * d), F32), jax.ShapeDtypeStruct((s, d), BF16),
                   jax.ShapeDtypeStruct((s, d), F32)),
        scratch_shapes=[pltpu.VMEM((8, d), F32)],
        compiler_params=_params(VMEM_BIG),
    )(x, tab, ck, wci, wco)


def _ffn_fwd(x, tab, base, wfi, wfo, tgt, tm, name):
    s, d = x.shape
    nsh, fw, _ = wfi.shape
    nh = nsh // 2
    with_loss = tgt is not None

    def body(*refs):
        if with_loss:
            x_ref, t_ref, wfi_ref, wfo_ref, tgt_ref, xo_ref, h_ref, gu_ref, y_ref, loss_ref = refs
        else:
            x_ref, t_ref, wfi_ref, wfo_ref, xo_ref, h_ref, gu_ref, y_ref = refs
        xv = x_ref[...]
        hb = ((xv * _rs(xv)) * t_ref[base + R_W2:base + R_W2 + 1, :]
              + t_ref[base + R_SH2:base + R_SH2 + 1, :]).astype(BF16)
        h_ref[...] = hb
        acc = jnp.zeros((tm, d), F32)
        for j in range(nh):
            g = lax.dot_general(hb, wfi_ref[j], NT, preferred_element_type=F32)
            u = lax.dot_general(hb, wfi_ref[j + nh], NT, preferred_element_type=F32)
            gu_ref[j] = g.astype(BF16)
            gu_ref[j + nh] = u.astype(BF16)
            ab = ((g * _sigmoid(g)) * u).astype(BF16)
            acc = acc + jnp.dot(ab, wfo_ref[j], preferred_element_type=F32)
        y_ref[...] = acc
        xo = xv + (acc * _rs(acc)) * t_ref[base + R_P2:base + R_P2 + 1, :]
        if with_loss:
            @pl.when(pl.program_id(0) == 0)
            def _():
                loss_ref[...] = jnp.zeros_like(loss_ref)

            err = xo - tgt_ref[...]
            xo_ref[...] = err * (1.0 / d)
            loss_ref[...] += jnp.sum(err * err)
        else:
            xo_ref[...] = xo

    in_specs = [_rows(tm, d), _const(tab.shape), _const(wfi.shape), _const(wfo.shape)]
    args = [x, tab, wfi, wfo]
    out_specs = [_rows(tm, d), _rows(tm, d), pl.BlockSpec((nsh, tm, fw), lambda i: (0, i, 0)), _rows(tm, d)]
    out_shape = [jax.ShapeDtypeStruct((s, d), F32), jax.ShapeDtypeStruct((s, d), BF16),
                 jax.ShapeDtypeStruct((nsh, s, fw), BF16), jax.ShapeDtypeStruct((s, d), F32)]
    if with_loss:
        in_specs.append(_rows(tm, d))
        args.append(tgt)
        out_specs.append(pl.BlockSpec((8, LANES), lambda i: (0, 0)))
        out_shape.append(jax.ShapeDtypeStruct((8, LANES), F32))
    return pl.pallas_call(
        body, name=name, grid=(s // tm,), in_specs=in_specs, out_specs=tuple(out_specs),
        out_shape=tuple(out_shape), compiler_params=_params(VMEM_BIG),
    )(*args)


def _qkv_fwd(x, tab, wq, wkv, tm):
    s, d = x.shape
    nsh, _, kw = wkv.shape
    nh = nsh // 2
    base = 6

    def body(x_ref, t_ref, wq_ref, wkv_ref, hkv_ref, h1_ref, q_ref, k_ref, v_ref):
        xv = x_ref[...]
        n = xv * _rs(xv)
        hkv = (n * t_ref[R_KV:R_KV + 1, :] + t_ref[R_KV + 1:R_KV + 2, :]).astype(BF16)
        h1 = (n * t_ref[base + R_W1:base + R_W1 + 1, :] + t_ref[base + R_SH1:base + R_SH1 + 1, :]).astype(BF16)
        hkv_ref[...] = hkv
        h1_ref[...] = h1
        q_ref[...] = (jnp.dot(h1, wq_ref[...], preferred_element_type=F32) * (HEAD_DIM ** -0.5)).astype(BF16)
        for j in range(nh):
            k_ref[:, j * kw:(j + 1) * kw] = jnp.dot(hkv, wkv_ref[j], preferred_element_type=F32).astype(BF16)
            v_ref[:, j * kw:(j + 1) * kw] = jnp.dot(hkv, wkv_ref[j + nh], preferred_element_type=F32).astype(BF16)

    act = jax.ShapeDtypeStruct((s, d), BF16)
    return pl.pallas_call(
        body, name="qkv_fwd", grid=(s // tm,),
        in_specs=[_rows(tm, d), _const(tab.shape), _const(wq.shape), _const(wkv.shape)],
        out_specs=tuple([_rows(tm, d)] * 5), out_shape=(act,) * 5,
        compiler_params=_params(VMEM_BIG),
    )(x, tab, wq, wkv)


def _window_specs():
    return [pl.BlockSpec((QB, LANES), (lambda p, b, w=w: (jnp.maximum(b - 2 + w, 0), p))) for w in range(3)]


def _key_valid(b):
    col = lax.broadcasted_iota(jnp.int32, (QB, KW), 1) // CHUNK
    return (b * (QB // CHUNK) - N_LEFT + col) >= 0


def _bias_spec():
    return pl.BlockSpec((None, LANES // HEAD_DIM, QB, KW),
                        lambda p, b: (jnp.minimum(b, BIAS_VARIANTS - 1), p, 0, 0))


def _head_masks():
    lane = lax.broadcasted_iota(jnp.int32, (1, LANES), 1)
    return [(lane // HEAD_DIM == hh) for hh in range(LANES // HEAD_DIM)]


def _attn_fwd(q, k, v, bias):
    s, d = q.shape
    npair, nb = d // LANES, s // QB
    hpp = LANES // HEAD_DIM

    def body(q_ref, k0, k1, k2, v0, v1, v2, bias_ref, o_ref, lse_ref):
        qv = q_ref[...]
        kwin = jnp.concatenate([k0[...], k1[...], k2[...]], axis=0)
        vwin = jnp.concatenate([v0[...], v1[...], v2[...]], axis=0)
        masks = _head_masks()
        o = jnp.zeros((QB, LANES), F32)
        lse = jnp.zeros((QB, LANES), F32)
        scs = [lax.dot_general(jnp.where(masks[hh], qv, jnp.zeros_like(qv)), kwin, NT, preferred_element_type=F32)
               + bias_ref[hh] for hh in range(hpp)]
        for hh in range(hpp):
            vm = jnp.where(masks[hh], vwin, jnp.zeros_like(vwin))
            sc = scs[hh]
            m = jnp.max(sc, axis=-1, keepdims=True)
            p = jnp.exp(sc - m)
            l = jnp.sum(p, axis=-1, keepdims=True)
            o = o + jnp.dot(p.astype(BF16), vm, preferred_element_type=F32) * (1.0 / l)
            lse = jnp.where(masks[hh], m + jnp.log(l), lse)
        o_ref[...] = o.astype(BF16)
        lse_ref[...] = lse

    blk = pl.BlockSpec((QB, LANES), lambda p, b: (b, p))
    return pl.pallas_call(
        body, name="attn_fwd", grid=(npair, nb),
        in_specs=[blk] + _window_specs() + _window_specs() + [_bias_spec()],
        out_specs=(blk, blk),
        out_shape=(jax.ShapeDtypeStruct((s, d), BF16), jax.ShapeDtypeStruct((s, d), F32)),
        compiler_params=_params(VMEM_BIG),
    )(q, k, k, k, v, v, v, bias)


def _attn_out_fwd(o, x, tab, wo, tm):
    s, d = x.shape
    base = 6

    def body(o_ref, x_ref, t_ref, wo_ref, x3_ref, y_ref):
        yv = jnp.dot(o_ref[...], wo_ref[...], preferred_element_type=F32)
        y_ref[...] = yv
        x3_ref[...] = x_ref[...] + (yv * _rs(yv)) * t_ref[base + R_P1:base + R_P1 + 1, :]

    return pl.pallas_call(
        body, name="attn_out_fwd", grid=(s // tm,),
        in_specs=[_rows(tm, d), _rows(tm, d), _const(tab.shape), _const(wo.shape)],
        out_specs=(_rows(tm, d), _rows(tm, d)),
        out_shape=(jax.ShapeDtypeStruct((s, d), F32), jax.ShapeDtypeStruct((s, d), F32)),
        compiler_params=_params(VMEM_BIG),
    )(o, x, tab, wo)


def _ffn_bwd(dxo, x, y, gu, tab, base, wfi, wfo, tm, name):
    s, d = x.shape
    nsh, fw, _ = wfi.shape
    nh = nsh // 2

    def body(dxo_ref, x_ref, y_ref, gu_ref, t_ref, wfi_ref, wfo_ref, dx_ref, dyb_ref, dgu_ref, a_ref, sums_ref):
        @pl.when(pl.program_id(0) == 0)
        def _():
            sums_ref[...] = jnp.zeros_like(sums_ref)

        dxo_v = dxo_ref[...]
        yv = y_ref[...]
        ry = _rs(yv)
        ny = yv * ry
        sums_ref[R_P2:R_P2 + 1, :] += _colsum(dxo_v * ny)
        dyb = _norm_bwd(dxo_v * t_ref[base + R_P2:base + R_P2 + 1, :], ny, ry).astype(BF16)
        dyb_ref[...] = dyb
        dh = jnp.zeros((tm, d), F32)
        for j in range(nh):
            da = lax.dot_general(dyb, wfo_ref[j], NT, preferred_element_type=F32)
            g, u = gu_ref[j].astype(F32), gu_ref[j + nh].astype(F32)
            sg = _sigmoid(g)
            gs = g * sg
            a_ref[j] = (gs * u).astype(BF16)
            dg = (da * u * sg * (1.0 + g * (1.0 - sg))).astype(BF16)
            du = (da * gs).astype(BF16)
            dgu_ref[j] = dg
            dgu_ref[j + nh] = du
            dh = dh + jnp.dot(dg, wfi_ref[j], preferred_element_type=F32)
            dh = dh + jnp.dot(du, wfi_ref[j + nh], preferred_element_type=F32)
        xv = x_ref[...]
        r = _rs(xv)
        n = xv * r
        sums_ref[R_SH2:R_SH2 + 1, :] += _colsum(dh)
        sums_ref[R_W2:R_W2 + 1, :] += _colsum(dh * n)
        dx_ref[...] = dxo_v + _norm_bwd(dh * t_ref[base + R_W2:base + R_W2 + 1, :], n, r)

    stack = lambda n: pl.BlockSpec((n, tm, fw), lambda i: (0, i, 0))
    return pl.pallas_call(
        body, name=name, grid=(s // tm,),
        in_specs=[_rows(tm, d), _rows(tm, d), _rows(tm, d), stack(nsh),
                  _const(tab.shape), _const(wfi.shape), _const(wfo.shape)],
        out_specs=(_rows(tm, d), _rows(tm, d), stack(nsh), stack(nh), pl.BlockSpec((8, d), lambda i: (0, 0))),
        out_shape=(jax.ShapeDtypeStruct((s, d), F32), jax.ShapeDtypeStruct((s, d), BF16),
                   jax.ShapeDtypeStruct((nsh, s, fw), BF16), jax.ShapeDtypeStruct((nh, s, fw), BF16),
                   jax.ShapeDtypeStruct((8, d), F32)),
        compiler_params=_params(VMEM_BIG),
    )(dxo, x, y, gu, tab, wfi, wfo)


def _attn_out_bwd(dx, y, tab, wo, tm):
    s, d = y.shape
    base = 6

    def body(dx_ref, y_ref, t_ref, wo_ref, dyb_ref, do_ref, sums_ref):
        @pl.when(pl.program_id(0) == 0)
        def _():
            sums_ref[...] = jnp.zeros_like(sums_ref)

        dxv = dx_ref[...]
        yv = y_ref[...]
        ry = _rs(yv)
        ny = yv * ry
        sums_ref[R_P1:R_P1 + 1, :] += _colsum(dxv * ny)
        dyb = _norm_bwd(dxv * t_ref[base + R_P1:base + R_P1 + 1, :], ny, ry).astype(BF16)
        dyb_ref[...] = dyb
        do_ref[...] = lax.dot_general(dyb, wo_ref[...], NT, preferred_element_type=F32).astype(BF16)

    return pl.pallas_call(
        body, name="attn_out_bwd", grid=(s // tm,),
        in_specs=[_rows(tm, d), _rows(tm, d), _const(tab.shape), _const(wo.shape)],
        out_specs=(_rows(tm, d), _rows(tm, d), pl.BlockSpec((8, d), lambda i: (0, 0))),
        out_shape=(jax.ShapeDtypeStruct((s, d), BF16), jax.ShapeDtypeStruct((s, d), BF16),
                   jax.ShapeDtypeStruct((8, d), F32)),
        compiler_params=_params(VMEM_BIG),
    )(dx, y, tab, wo)


def _attn_bwd(q, k, v, o, do, lse, bias):
    s, d = q.shape
    npair, nb = d // LANES, s // QB
    hpp = LANES // HEAD_DIM

    def body(q_ref, k0, k1, k2, v0, v1, v2, o_ref, do_ref, lse_ref, bias_ref, dq_ref, dk_ref, dv_ref, db_ref):
        b = pl.program_id(1)

        @pl.when(b == 0)
        def _():
            dk_ref[...] = jnp.zeros_like(dk_ref)
            dv_ref[...] = jnp.zeros_like(dv_ref)
            db_ref[...] = jnp.zeros_like(db_ref)

        qv = q_ref[...]
        dov = do_ref[...]
        lsev = lse_ref[...]
        doo = dov.astype(F32) * o_ref[...].astype(F32)
        kwin = jnp.concatenate([k0[...], k1[...], k2[...]], axis=0)
        vwin = jnp.concatenate([v0[...], v1[...], v2[...]], axis=0)
        masks = _head_masks()
        dq = jnp.zeros((QB, LANES), F32)
        dkw = jnp.zeros((KW, LANES), F32)
        dvw = jnp.zeros((KW, LANES), F32)
        for hh in range(hpp):
            qm = jnp.where(masks[hh], qv, jnp.zeros_like(qv))
            dom = jnp.where(masks[hh], dov, jnp.zeros_like(dov))
            km = jnp.where(masks[hh], kwin, jnp.zeros_like(kwin))
            lse_h = jnp.max(jnp.where(masks[hh], lsev, NEG), axis=-1, keepdims=True)
            delta = jnp.sum(jnp.where(masks[hh], doo, 0.0), axis=-1, keepdims=True)
            sc = lax.dot_general(qm, kwin, NT, preferred_element_type=F32) + bias_ref[hh]
            p = jnp.exp(sc - lse_h)
            dp = lax.dot_general(dom, vwin, NT, preferred_element_type=F32)
            ds = p * (dp - delta)
            db_ref[hh] += ds
            dsb = ds.astype(BF16)
            dq = dq + jnp.dot(dsb, km, preferred_element_type=F32)
            dkw = dkw + lax.dot_general(dsb, qm, TN, preferred_element_type=F32)
            dvw = dvw + lax.dot_general(p.astype(BF16), dom, TN, preferred_element_type=F32)
        dq_ref[...] = (dq * (HEAD_DIM ** -0.5)).astype(BF16)
        for w in range(3):
            start = pl.multiple_of(jnp.maximum(b - 2 + w, 0) * QB, QB)
            dk_ref[pl.ds(start, QB), :] += dkw[w * QB:(w + 1) * QB, :]
            dv_ref[pl.ds(start, QB), :] += dvw[w * QB:(w + 1) * QB, :]

    blk = pl.BlockSpec((QB, LANES), lambda p, b: (b, p))
    col = pl.BlockSpec((s, LANES), lambda p, b: (0, p))
    pair = pl.BlockSpec((hpp, QB, KW), lambda p, b: (p, 0, 0))
    return pl.pallas_call(
        body, name="attn_bwd", grid=(npair, nb),
        in_specs=[blk] + _window_specs() + _window_specs() + [blk, blk, blk, _bias_spec()],
        out_specs=(blk, col, col, pair),
        out_shape=(jax.ShapeDtypeStruct((s, d), BF16), jax.ShapeDtypeStruct((s, d), F32),
                   jax.ShapeDtypeStruct((s, d), F32), jax.ShapeDtypeStruct(bias.shape[1:], F32)),
        compiler_params=_params(VMEM_BIG),
    )(q, k, k, k, v, v, v, o, do, lse, bias)


def _qkv_bwd(dres, dq, dk, dv, x, tab, wq, wkv, tm):
    s, d = x.shape
    nsh, _, kw = wkv.shape
    nh = nsh // 2
    base = 6

    def body(dres_ref, dq_ref, dk_ref, dv_ref, x_ref, t_ref, wq_ref, wkv_ref, dx_ref, dkv_ref, sums_ref):
        @pl.when(pl.program_id(0) == 0)
        def _():
            sums_ref[...] = jnp.zeros_like(sums_ref)

        dh1 = lax.dot_general(dq_ref[...], wq_ref[...], NT, preferred_element_type=F32)
        dkv_ref[:, 0:d] = dk_ref[...].astype(BF16)
        dkv_ref[:, d:2 * d] = dv_ref[...].astype(BF16)
        dhkv = jnp.zeros((tm, d), F32)
        for j in range(nsh):
            dhkv = dhkv + lax.dot_general(dkv_ref[:, j * kw:(j + 1) * kw], wkv_ref[j], NT,
                                          preferred_element_type=F32)
        xv = x_ref[...]
        r = _rs(xv)
        n = xv * r
        sums_ref[0:1, :] += _colsum(dh1 * n)
        sums_ref[1:2, :] += _colsum(dh1)
        sums_ref[2:3, :] += _colsum(dhkv * n)
        sums_ref[3:4, :] += _colsum(dhkv)
        dn = dh1 * t_ref[base + R_W1:base + R_W1 + 1, :] + dhkv * t_ref[R_KV:R_KV + 1, :]
        dx_ref[...] = dres_ref[...] + _norm_bwd(dn, n, r)

    return pl.pallas_call(
        body, name="qkv_bwd", grid=(s // tm,),
        in_specs=[_rows(tm, d)] * 5 + [_const(tab.shape), _const(wq.shape), _const(wkv.shape)],
        out_specs=(_rows(tm, d), _rows(tm, 2 * d), pl.BlockSpec((8, d), lambda i: (0, 0))),
        out_shape=(jax.ShapeDtypeStruct((s, d), F32), jax.ShapeDtypeStruct((s, 2 * d), BF16),
                   jax.ShapeDtypeStruct((8, d), F32)),
        compiler_params=_params(VMEM_BIG),
    )(dres, dq, dk, dv, x, tab, wq, wkv)


def _conv_bwd(dx1, x, y, bcx, tab, ck, wci, wco, tm):
    s, d = x.shape
    nsh, _, cw = wci.shape
    nt = s // tm

    def rev(i):
        return (nt - 1 - i, 0)

    def halo(i):
        return (jnp.maximum((nt - 1 - i) * (tm // 8) - 1, 0), 0)

    def body(dx_ref, x_ref, y_ref, bcx_ref, halo_ref, t_ref, ck_ref, wci_ref, wco_ref,
             dx0_ref, dyb_ref, dbcx_ref, sums_ref, dck_ref, carry):
        i = pl.program_id(0)

        @pl.when(i == 0)
        def _():
            sums_ref[...] = jnp.zeros_like(sums_ref)
            dck_ref[...] = jnp.zeros_like(dck_ref)
            carry[...] = jnp.zeros_like(carry)

        dxv = dx_ref[...]
        yv = y_ref[...]
        ry = _rs(yv)
        ny = yv * ry
        sums_ref[R_P1:R_P1 + 1, :] += _colsum(dxv * ny)
        dyb = _norm_bwd(dxv * t_ref[R_P1:R_P1 + 1, :], ny, ry).astype(BF16)
        dyb_ref[...] = dyb
        du = lax.dot_general(dyb, wco_ref[...], NT, preferred_element_type=F32)
        bg, cg, xi = bcx_ref[:, 0:d], bcx_ref[:, d:2 * d], bcx_ref[:, 2 * d:3 * d]
        z = cg * xi
        zp = halo_ref[:, d:2 * d] * halo_ref[:, 2 * d:3 * d]
        zp = jnp.where(i == nt - 1, jnp.zeros_like(zp), zp)
        row = lax.broadcasted_iota(jnp.int32, z.shape, 0)
        c1, c2 = zp[7:8, :], zp[6:7, :]
        z1 = jnp.where(row == 0, c1, pltpu.roll(z, 1, 0))
        z2 = jnp.where(row == 0, c2, jnp.where(row == 1, c1, pltpu.roll(z, 2, 0)))
        k0, k1, k2 = ck_ref[0:1, :], ck_ref[1:2, :], ck_ref[2:3, :]
        conv = k0 * z2 + k1 * z1 + k2 * z
        dconv = du * bg
        dck_ref[0:1, :] += _colsum(dconv * z2)
        dck_ref[1:2, :] += _colsum(dconv * z1)
        dck_ref[2:3, :] += _colsum(dconv * z)
        n1, n2 = carry[0:1, :], carry[1:2, :]
        d1 = jnp.where(row == tm - 1, n1, pltpu.roll(dconv, tm - 1, 0))
        d2 = jnp.where(row == tm - 1, n2, jnp.where(row == tm - 2, n1, pltpu.roll(dconv, tm - 2, 0)))
        carry[...] = dconv[0:8, :]
        dz = k2 * dconv + k1 * d1 + k0 * d2
        dbcx_ref[:, 0:d] = (du * conv).astype(BF16)
        dbcx_ref[:, d:2 * d] = (dz * xi).astype(BF16)
        dbcx_ref[:, 2 * d:3 * d] = (dz * cg).astype(BF16)
        dh = jnp.zeros((tm, d), F32)
        for j in range(nsh):
            dh = dh + lax.dot_general(dbcx_ref[:, j * cw:(j + 1) * cw], wci_ref[j], NT,
                                      preferred_element_type=F32)
        xv = x_ref[...]
        r = _rs(xv)
        n = xv * r
        sums_ref[R_W1:R_W1 + 1, :] += _colsum(dh * n)
        sums_ref[R_SH1:R_SH1 + 1, :] += _colsum(dh)
        dx0_ref[...] = dxv + _norm_bwd(dh * t_ref[R_W1:R_W1 + 1, :], n, r)

    rrow = lambda cols: pl.BlockSpec((tm, cols), rev)
    acc = pl.BlockSpec((8, d), lambda i: (0, 0))
    return pl.pallas_call(
        body, name="conv_bwd", grid=(nt,),
        in_specs=[rrow(d), rrow(d), rrow(d), rrow(3 * d), pl.BlockSpec((8, 3 * d), halo),
                  _const(tab.shape), _const(ck.shape), _const(wci.shape), _const(wco.shape)],
        out_specs=(rrow(d), rrow(d), rrow(3 * d), acc, acc),
        out_shape=(jax.ShapeDtypeStruct((s, d), F32), jax.ShapeDtypeStruct((s, d), BF16),
                   jax.ShapeDtypeStruct((s, 3 * d), BF16), jax.ShapeDtypeStruct((8, d), F32),
                   jax.ShapeDtypeStruct((8, d), F32)),
        scratch_shapes=[pltpu.VMEM((8, d), F32)],
        compiler_params=_params(VMEM_BIG),
    )(dx1, x, y, bcx, bcx, tab, ck, wci, wco)


def _wgrad(a, b, nblk, a_spec, b_spec, m, n, tk, name):
    s = a.shape[-2]
    nk = s // tk

    def body(a_ref, b_ref, o_ref, acc):
        kk = pl.program_id(1)

        @pl.when(kk == 0)
        def _():
            acc[...] = jnp.zeros_like(acc)

        acc[...] += lax.dot_general(a_ref[...], b_ref[...], TN, preferred_element_type=F32)

        @pl.when(kk == nk - 1)
        def _():
            o_ref[...] = acc[...].astype(BF16)

    return pl.pallas_call(
        body, name=name, grid=(nblk, nk),
        in_specs=[a_spec, b_spec],
        out_specs=pl.BlockSpec((None, m, n), lambda j, kk: (j, 0, 0)),
        out_shape=jax.ShapeDtypeStruct((nblk, m, n), BF16),
        scratch_shapes=[pltpu.VMEM((m, n), F32)],
        compiler_params=_params(VMEM_BIG),
    )(a, b)


def _wgrad_wide(a, b, nblk, tk, name):
    s, m = a.shape
    n = b.shape[1] // nblk
    nk = s // tk

    def body(a_ref, b_ref, o_ref, acc):
        kk = pl.program_id(0)

        @pl.when(kk == 0)
        def _():
            acc[...] = jnp.zeros_like(acc)

        acc[...] += jnp.dot(a_ref[...].T, b_ref[...], preferred_element_type=F32)

        @pl.when(kk == nk - 1)
        def _():
            for j in range(nblk):
                o_ref[j] = acc[:, j * n:(j + 1) * n].astype(BF16)

    return pl.pallas_call(
        body, name=name, grid=(nk,),
        in_specs=[pl.BlockSpec((tk, m), lambda kk: (kk, 0)), pl.BlockSpec((tk, nblk * n), lambda kk: (kk, 0))],
        out_specs=pl.BlockSpec((nblk, m, n), lambda kk: (0, 0, 0)),
        out_shape=jax.ShapeDtypeStruct((nblk, m, n), BF16),
        scratch_shapes=[pltpu.VMEM((m, nblk * n), F32)],
        compiler_params=_params(VMEM_BIG),
    )(a, b)


def _wgrad_astack(a, b, tk, name):
    nblk, _, m = a.shape
    n = b.shape[1]
    return _wgrad(a, b, nblk, pl.BlockSpec((None, tk, m), lambda j, kk: (j, kk, 0)),
                  pl.BlockSpec((tk, n), lambda j, kk: (kk, 0)), m, n, tk, name)


def _adamw_math(w, g, m, v):
    m = ADAM_B1 * m + (1.0 - ADAM_B1) * g
    v = ADAM_B2 * v + (1.0 - ADAM_B2) * (g * g)
    m_hat = m / (1.0 - ADAM_B1 ** ADAM_STEP)
    v_hat = v / (1.0 - ADAM_B2 ** ADAM_STEP)
    delta = -ADAM_LR * (m_hat / (jnp.sqrt(v_hat) + ADAM_EPS) + ADAM_WD * w)
    return delta, m, v


def _adamw_reduce(parts, w, m, v, tr, name):
    nl, r, c = w.shape
    tr = _row_tile(r, tr)

    def body(*refs):
        p_refs = refs[:nl]
        w_ref, m_ref, v_ref, g_ref, d_ref, mo_ref, vo_ref = refs[nl:]
        layer = pl.program_id(0)

        def partial(i):
            val = p_refs[0][i].astype(F32)
            for q in range(1, nl):
                val = jnp.where(layer == q, p_refs[q][i].astype(F32), val)
            return val

        g = partial(0)
        for i in range(1, N_DEV):
            g = g + partial(i)
        g_ref[...] = g
        d_ref[...], mo_ref[...], vo_ref[...] = _adamw_math(w_ref[...], g, m_ref[...], v_ref[...])

    blk = pl.BlockSpec((None, tr, c), lambda l, i: (l, i, 0))
    out = jax.ShapeDtypeStruct((nl, r, c), F32)
    p_specs = [pl.BlockSpec((N_DEV, tr, c), (lambda l, i, q=q: (0, jnp.where(l == q, i, 0), 0))) for q in range(nl)]
    return pl.pallas_call(
        body, name=name, grid=(nl, r // tr),
        in_specs=p_specs + [blk, blk, blk],
        out_specs=(blk,) * 4, out_shape=(out,) * 4,
        compiler_params=_params(VMEM_BIG),
    )(*parts, w, m, v)


def _adamw_outer(sct, dm, w, m, v, tr, name):
    nl, d, c = w.shape

    def body(s_ref, dm_ref, w_ref, m_ref, v_ref, g_ref, d_ref, mo_ref, vo_ref):
        g = jnp.dot(s_ref[...], dm_ref[...], preferred_element_type=F32)
        g_ref[...] = g
        d_ref[...], mo_ref[...], vo_ref[...] = _adamw_math(w_ref[...], g, m_ref[...], v_ref[...])

    blk = pl.BlockSpec((None, tr, c), lambda l, i: (l, i, 0))
    out = jax.ShapeDtypeStruct((nl, d, c), F32)
    return pl.pallas_call(
        body, name=name, grid=(nl, d // tr),
        in_specs=[pl.BlockSpec((tr, N_DEV), lambda l, i: (i, 0)),
                  pl.BlockSpec((None, N_DEV, c), lambda l, i: (l, 0, 0)), blk, blk, blk],
        out_specs=(blk,) * 4, out_shape=(out,) * 4,
        compiler_params=_params(VMEM_BIG),
    )(sct, dm, w, m, v)


def _pad_rows(a, rows):
    return jnp.concatenate([a, jnp.zeros((rows - a.shape[0],) + a.shape[1:], a.dtype)], axis=0)


def kernel(x, c, mod_w, mod_b, norm_g, ffn_w_in, ffn_w_out, conv_w_in, conv_k, conv_w_out, kv_mod_w, kv_mod_b, kv_norm_g, w_kv, attn_w_q, attn_w_o, rel_bias, loss_target, m_mod_w, m_mod_b, m_norm_g, m_ffn_w_in, m_ffn_w_out, m_conv_w_in, m_conv_k, m_conv_w_out, m_kv_mod_w, m_kv_mod_b, m_kv_norm_g, m_w_kv, m_attn_w_q, m_attn_w_o, m_rel_bias, v_mod_w, v_mod_b, v_norm_g, v_ffn_w_in, v_ffn_w_out, v_conv_w_in, v_conv_k, v_conv_w_out, v_kv_mod_w, v_kv_mod_b, v_kv_norm_g, v_w_kv, v_attn_w_q, v_attn_w_o, v_rel_bias):
    s, d = x.shape[1], x.shape[2]
    dq = d // LANES
    dsh = d // N_DEV
    nl = mod_w.shape[0]
    mw = mod_w.shape[2]
    kmw = kv_mod_w.shape[1]
    fw = ffn_w_in.shape[2]
    nh, nrel = rel_bias.shape[1], rel_bias.shape[2]
    tm = min(256, s)
    tm2 = min(512, s)
    tk = min(1024, s)
    tk2 = min(2048, s)
    me = 4 * lax.axis_index("x") + 2 * lax.axis_index("y") + lax.axis_index("c")

    x0 = x[0]
    tgt = loss_target[0]

    small1 = jnp.concatenate([c.reshape(dq, LANES), norm_g.reshape(dq, LANES),
                              _pad_rows(conv_k[0], 8).reshape(dq, LANES)], axis=0)
    (sm,) = _exchange([small1], ["gather"], "gather_small")
    c_all = sm[:, 0:dq].reshape(N_DEV, d)
    ng_full = jnp.transpose(sm[:, dq:2 * dq].reshape(N_DEV, 8, dsh), (1, 0, 2)).reshape(8, d)
    ck_full = jnp.transpose(sm[:, 2 * dq:3 * dq].reshape(N_DEV, 8, dsh), (1, 0, 2)).reshape(8, d)

    modcols, silu_c = _mod_fwd(c_all, mod_w, kv_mod_w)
    (modall,) = _exchange([modcols], ["gather"], "gather_mod")

    cast = lambda *ws: [a.astype(BF16) for a in ws]
    gath = lambda ws: (ws, ["gather"] * len(ws))
    (h_conv, h_ffn0, h_attn, h_ffn1), token = _xstart(
        [gath(cast(conv_w_in[0], conv_w_out[0])), gath(cast(jnp.swapaxes(ffn_w_in[0], 0, 1), ffn_w_out[0])),
         gath(cast(w_kv, attn_w_q[0], attn_w_o[0])), gath(cast(jnp.swapaxes(ffn_w_in[1], 0, 1), ffn_w_out[1]))],
        modall, "gather_start")
    modall = modall + token[0, 0]
    mine = lax.dynamic_index_in_dim(modall, me, axis=1, keepdims=False)
    modrow = jnp.stack([mine[:, l * mw:(l + 1) * mw].reshape(6, d) for l in range(nl)])
    kvrow = mine[:, nl * mw:nl * mw + kmw].reshape(2, d)
    tab, modval = _vec_prep(modrow, mod_b.reshape(nl, 6, d), kvrow, kv_mod_b.reshape(2, d), ng_full,
                            kv_norm_g.reshape(1, d))
    bias = _bias_fwd(rel_bias[0])

    wci, wco = _xwait(h_conv, [bias], "gather_wait_conv")
    wco = wco.reshape(d, d)
    x1, h1a, bcx, ua, ya = _conv_fwd(x0, tab, ck_full, wci, wco, tm2)
    wfi0, wfo0 = _xwait(h_ffn0, [x1], "gather_wait_ffn0")
    wfo0 = wfo0.reshape(N_DEV // 2, -1, d)
    x2, h2a, gua, y2a = _ffn_fwd(x1, tab, 0, wfi0, wfo0, None, tm2, "ffn_fwd0")
    wkv, wq, wo = _xwait(h_attn, [x2], "gather_wait_attn")
    wq, wo = wq.reshape(d, d), wo.reshape(d, d)
    hkv, h1b, q, k, v = _qkv_fwd(x2, tab, wq, wkv, tm2)
    o, lse = _attn_fwd(q, k, v, bias)
    x3, yb = _attn_out_fwd(o, x2, tab, wo, tm2)
    wfi1, wfo1 = _xwait(h_ffn1, [x3], "gather_wait_ffn1")
    wfo1 = wfo1.reshape(N_DEV // 2, -1, d)
    dx4, h2b, gub, y2b, loss_acc = _ffn_fwd(x3, tab, 6, wfi1, wfo1, tgt, tm2, "ffn_fwd1")

    scat = lambda ws: [(ws, ["scatter"] * len(ws))]
    dx3, dy2b, dgub, ab, sums_f1 = _ffn_bwd(dx4, x3, y2b, gub, tab, 6, wfi1, wfo1, tm, "ffn_bwd1")
    g_wfi1 = _wgrad_astack(dgub, h2b, tk2, "wgrad_ffn_in1")
    g_wfo1 = _wgrad_astack(ab, dy2b, tk2, "wgrad_ffn_out1").reshape(N_DEV, -1, d)
    (h_g1,), token = _xstart(scat([g_wfi1, g_wfo1]), dx3, "grads_start_ffn1")
    tab = tab + token[0, 0]
    dyb, do, sums_o = _attn_out_bwd(dx3, yb, tab, wo, tm2)
    g_wo = _wgrad_wide(o, dyb, 1, tk, "wgrad_o").reshape(N_DEV, dsh, d)
    dqb, dk, dv, dbias = _attn_bwd(q, k, v, o, do, lse, bias)
    g_wq = _wgrad_wide(h1b, dqb, 1, tk, "wgrad_q").reshape(N_DEV, dsh, d)
    dx2, dkvb, sums_q = _qkv_bwd(dx3, dqb, dk, dv, x2, tab, wq, wkv, tm2)
    g_wkv = _wgrad_wide(hkv, dkvb, N_DEV, tk, "wgrad_kv")
    (h_g2,), token = _xstart(scat([g_wkv, g_wq, g_wo]), dx2, "grads_start_attn")
    tab = tab + token[0, 0]
    dx1, dy2a, dgua, aa, sums_f0 = _ffn_bwd(dx2, x1, y2a, gua, tab, 0, wfi0, wfo0, tm, "ffn_bwd0")
    g_wfi0 = _wgrad_astack(dgua, h2a, tk2, "wgrad_ffn_in0")
    g_wfo0 = _wgrad_astack(aa, dy2a, tk2, "wgrad_ffn_out0").reshape(N_DEV, -1, d)
    (h_g3,), token = _xstart(scat([g_wfi0, g_wfo0]), dx1, "grads_start_ffn0")
    tab = tab + token[0, 0]
    dx0, dya, dbcx, sums_c, dck = _conv_bwd(dx1, x0, ya, bcx, tab, ck_full, wci, wco, tm2)
    drel = _bias_bwd(dbias, nrel)
    dmod, dng, dkvg = _vec_bwd(sums_c, sums_f0, sums_q, sums_o, sums_f1, modval, ng_full, kv_norm_g.reshape(1, d))

    relw = -(-nrel // LANES) * LANES
    drel_p = jnp.concatenate([drel, jnp.zeros((nh, relw - nrel), F32)], axis=1)
    small3 = jnp.concatenate([dmod.reshape(16 * dq, LANES), dng.reshape(8 * dq, LANES), dkvg.reshape(8 * dq, LANES),
                              dck.reshape(8 * dq, LANES), loss_acc,
                              drel_p.reshape(nh * relw // LANES, LANES)], axis=0)
    (sm,) = _exchange([small3], ["gather"], "gather_small_grads")
    g_wci = _wgrad_wide(h1a, dbcx, N_DEV, tk, "wgrad_conv_in")
    g_wco = _wgrad_wide(ua, dya, 1, tk, "wgrad_conv_out").reshape(N_DEV, dsh, d)
    (h_g4,), token = _xstart(scat([g_wci, g_wco]), sm, "grads_start_conv")
    sm = sm + token[0, 0]
    o1, o2, o3, o4, o5 = 16 * dq, 24 * dq, 32 * dq, 40 * dq, 40 * dq + 8
    loss = jnp.sum(sm[:, o4, 0]) * (0.5 / d)
    dmod_all = sm[:, 0:o1].reshape(N_DEV, 16, d)
    mine_cols = lambda a: lax.dynamic_slice_in_dim(a, me * dsh, dsh, axis=2)
    dng_parts = mine_cols(sm[:, o1:o2].reshape(N_DEV, 8, d))
    dkvg_parts = sm[:, o2:o3].reshape(N_DEV, 8, d)[:, 0:1]
    dck_parts = mine_cols(sm[:, o3:o4].reshape(N_DEV, 8, d))[:, 0:3]
    drel_parts = sm[:, o5:].reshape(N_DEV, nh, relw)[:, :, 0:nrel]

    def update(parts, w, m, v, name, layers=1):
        shp = w.shape
        w3, m3, v3 = (a.reshape(layers, -1, shp[-1]) for a in (w, m, v))
        outs = _adamw_reduce([p.reshape(N_DEV, -1, shp[-1]) for p in parts], w3, m3, v3, 256, name)
        return [a.reshape(shp) for a in outs]

    p_wfi1, p_wfo1 = _xwait(h_g1, [sm], "grads_wait_ffn1")
    p_wfi0, p_wfo0 = _xwait(h_g3, [p_wfi1], "grads_wait_ffn0")
    tr = lambda a: jnp.swapaxes(a, 1, 2)
    u_ffn_in = [tr(a) for a in update([p_wfi0, p_wfi1], tr(ffn_w_in), tr(m_ffn_w_in), tr(v_ffn_w_in),
                                      "adamw_ffn_in", 2)]
    u_ffn_out = update([p_wfo0, p_wfo1], ffn_w_out, m_ffn_w_out, v_ffn_w_out, "adamw_ffn_out", 2)
    p_wkv, p_wq, p_wo = _xwait(h_g2, [u_ffn_out[0]], "grads_wait_attn")
    u_w_kv = update([p_wkv], w_kv, m_w_kv, v_w_kv, "adamw_w_kv")
    u_w_q = update([p_wq], attn_w_q, m_attn_w_q, v_attn_w_q, "adamw_w_q")
    u_w_o = update([p_wo], attn_w_o, m_attn_w_o, v_attn_w_o, "adamw_w_o")

    sct = jnp.transpose(silu_c)
    dm_mod = jnp.stack([lax.dynamic_slice_in_dim(dmod_all[:, 6 * l:6 * l + 6].reshape(N_DEV, 6 * d), me * mw, mw, axis=1)
                        for l in range(nl)]).astype(BF16)
    dm_kv = lax.dynamic_slice_in_dim(dmod_all[:, R_KV:R_KV + 2].reshape(N_DEV, 2 * d), me * kmw, kmw, axis=1)
    u_mod_w = _adamw_outer(sct, dm_mod, mod_w, m_mod_w, v_mod_w, min(256, d), "adamw_mod_w")
    u_kv_mod_w = [a[0] for a in _adamw_outer(sct, dm_kv.astype(BF16)[None], kv_mod_w[None], m_kv_mod_w[None],
                                             v_kv_mod_w[None], min(256, d), "adamw_kv_mod_w")]

    modb_parts = jnp.stack([dmod_all[:, 6 * l:6 * l + 6].reshape(N_DEV, 6 * d) for l in range(nl)], axis=1)
    u_mod_b = update([modb_parts], mod_b, m_mod_b, v_mod_b, "adamw_mod_b")
    u_norm_g = update([dng_parts], norm_g.reshape(8, dsh), m_norm_g.reshape(8, dsh), v_norm_g.reshape(8, dsh), "adamw_norm_g")
    u_norm_g = [a.reshape(norm_g.shape) for a in u_norm_g]
    u_conv_k = update([dck_parts], conv_k, m_conv_k, v_conv_k, "adamw_conv_k")
    kvb_parts = dmod_all[:, R_KV:R_KV + 2].reshape(N_DEV, 1, 2 * d)
    u_kv_mod_b = [a.reshape(kv_mod_b.shape) for a in update([kvb_parts], kv_mod_b.reshape(1, -1), m_kv_mod_b.reshape(1, -1),
                                                            v_kv_mod_b.reshape(1, -1), "adamw_kv_mod_b")]
    u_kv_norm_g = [a.reshape(kv_norm_g.shape) for a in update([dkvg_parts], kv_norm_g.reshape(1, -1), m_kv_norm_g.reshape(1, -1),
                                                              v_kv_norm_g.reshape(1, -1), "adamw_kv_norm_g")]
    u_rel = update([drel_parts], rel_bias, m_rel_bias, v_rel_bias, "adamw_rel_bias")

    others = [u_ffn_in, u_ffn_out, u_w_kv, u_w_q, u_w_o, u_mod_w, u_kv_mod_w, u_mod_b, u_norm_g, u_conv_k, u_kv_mod_b,
              u_kv_norm_g, u_rel]
    p_wci, p_wco = _xwait(h_g4, [u[3] for u in others], "grads_wait_conv")
    u_conv_in = update([p_wci], conv_w_in, m_conv_w_in, v_conv_w_in, "adamw_conv_in")
    u_conv_out = update([p_wco], conv_w_out, m_conv_w_out, v_conv_w_out, "adamw_conv_out")

    ups = [u_mod_w, u_mod_b, u_norm_g, u_ffn_in, u_ffn_out, u_conv_in, u_conv_k, u_conv_out, u_kv_mod_w, u_kv_mod_b,
           u_kv_norm_g, u_w_kv, u_w_q, u_w_o, u_rel]
    return (loss, dx0[None], *[u[0] for u in ups], *[u[1] for u in ups], *[u[2] for u in ups], *[u[3] for u in ups])
```

```python
import jax
import jax.numpy as jnp
from jax import lax
from jax.experimental import pallas as pl
from jax.experimental.pallas import tpu as pltpu

F32 = jnp.float32
BF16 = jnp.bfloat16

EPS = 1e-6
CHUNK = 64
HEAD_DIM = 64
N_LEFT = 8
LANES = 128
QB = 4 * CHUNK
KW = QB + N_LEFT * CHUNK
BIAS_VARIANTS = N_LEFT * CHUNK // QB + 1
ATTN_PER = 8
NEG = -1e30
N_DEV = 8

ADAM_LR = 0.001
ADAM_B1 = 0.9
ADAM_B2 = 0.999
ADAM_EPS = 1e-08
ADAM_WD = 0.01
ADAM_STEP = 10

VMEM_BIG = 56 * 1024 * 1024

NT = (((1,), (1,)), ((), ()))
TN = (((0,), (0,)), ((), ()))

R_W1, R_SH1, R_P1, R_W2, R_SH2, R_P2 = range(6)
R_KV = 12


def _params(vmem):
    return pltpu.CompilerParams(vmem_limit_bytes=vmem)


def _row_tile(rows, cap):
    for t in range(min(cap, rows) // 16 * 16, 0, -16):
        if rows % t == 0:
            return t
    return rows


def _rows(tm, cols):
    return pl.BlockSpec((tm, cols), lambda i: (i, 0))


def _const(shape):
    nd = len(shape)
    return pl.BlockSpec(shape, lambda *_: (0,) * nd, pipeline_mode=pl.Buffered(1))


def _rs(x):
    return lax.rsqrt(jnp.mean(x * x, axis=-1, keepdims=True) + EPS)


def _norm_bwd(d, n, r):
    return r * (d - n * jnp.mean(d * n, axis=-1, keepdims=True))


def _colsum(a):
    return jnp.sum(a, axis=0, keepdims=True)


def _sigmoid(g):
    return 1.0 / (1.0 + jnp.exp(-g))


def _exchange(arrays, modes, name):
    n = len(arrays)
    out_shape = []
    for a, mode in zip(arrays, modes):
        shp = (N_DEV,) + a.shape if mode == "gather" else a.shape
        out_shape.append(jax.ShapeDtypeStruct(shp, a.dtype))

    def body(*refs):
        ins, outs = refs[:n], refs[n:2 * n]
        send_sems, recv_sems, local_sems = refs[2 * n:]
        x, y, c = lax.axis_index("x"), lax.axis_index("y"), lax.axis_index("c")
        me = 4 * x + 2 * y + c
        local, sends, recvs = [], [], []
        for a in range(n):
            own = ins[a] if modes[a] == "gather" else ins[a].at[me]
            cp = pltpu.make_async_copy(own, outs[a].at[me], local_sems.at[a])
            cp.start()
            local.append(cp)
        for k in range(1, N_DEV):
            px = 1 - x if k & 4 else x
            py = 1 - y if k & 2 else y
            pc = 1 - c if k & 1 else c
            peer = 4 * px + 2 * py + pc
            for a in range(n):
                src = ins[a] if modes[a] == "gather" else ins[a].at[peer]
                sem = a * (N_DEV - 1) + k - 1
                cp = pltpu.make_async_remote_copy(
                    src_ref=src, dst_ref=outs[a].at[me],
                    send_sem=send_sems.at[sem], recv_sem=recv_sems.at[sem],
                    device_id=(px, py, pc), device_id_type=pl.DeviceIdType.MESH)
                cp.start()
                sends.append(cp)
                recvs.append(pltpu.make_async_remote_copy(
                    src_ref=src, dst_ref=outs[a].at[peer],
                    send_sem=send_sems.at[sem], recv_sem=recv_sems.at[sem],
                    device_id=(px, py, pc), device_id_type=pl.DeviceIdType.MESH))
        for cp in recvs:
            cp.wait_recv()
        for cp in sends:
            cp.wait_send()
        for cp in local:
            cp.wait()

    any_spec = pl.BlockSpec(memory_space=pl.ANY)
    return pl.pallas_call(
        body, name=name,
        out_shape=tuple(out_shape),
        in_specs=[any_spec] * n,
        out_specs=tuple([any_spec] * n),
        scratch_shapes=[
            pltpu.SemaphoreType.DMA((n * (N_DEV - 1),)),
            pltpu.SemaphoreType.DMA((n * (N_DEV - 1),)),
            pltpu.SemaphoreType.DMA((n,)),
        ],
    )(*arrays)


def _peers(x, y, c):
    out = []
    for k in range(1, N_DEV):
        px = 1 - x if k & 4 else x
        py = 1 - y if k & 2 else y
        pc = 1 - c if k & 1 else c
        out.append((k - 1, (px, py, pc), 4 * px + 2 * py + pc))
    return out


def _land_shape(a, mode):
    return (N_DEV,) + a.shape if mode == "gather" else a.shape


_HBM = pl.BlockSpec(memory_space=pltpu.HBM)
_SEM = pl.BlockSpec(memory_space=pltpu.SEMAPHORE)
_EFFECT = pltpu.SideEffectType.DATAFLOW_SIDE_EFFECTING


def _xstart(groups, after, name):
    flat = [(a, m) for arrays, modes in groups for a, m in zip(arrays, modes)]
    n, ngr = len(flat), len(groups)
    sizes = [len(arrays) for arrays, _ in groups]
    npeer = N_DEV - 1

    def body(*refs):
        ins, lands = refs[:n], refs[n:2 * n]
        outs = refs[2 * n + 1:]
        sems = outs[:2 * ngr]
        token = outs[2 * ngr + 2 * n]
        local_sems = outs[2 * ngr + 2 * n + 1]
        stage = outs[2 * ngr + 2 * n + 2:]
        x, y, c = lax.axis_index("x"), lax.axis_index("y"), lax.axis_index("c")
        me = 4 * x + 2 * y + c
        loads, stores = [], []
        for a in range(n):
            own = ins[a] if flat[a][1] == "gather" else ins[a].at[me]
            loads.append(pltpu.make_async_copy(own, stage[a], local_sems.at[a]))
            stores.append(pltpu.make_async_copy(stage[a], lands[a].at[me], local_sems.at[a]))
            loads[a].start()
        for a in range(n):
            loads[a].wait()
            stores[a].start()
        a = 0
        for g in range(ngr):
            for j in range(sizes[g]):
                mode = flat[a][1]
                for slot, peer, pidx in _peers(x, y, c):
                    pltpu.make_async_remote_copy(
                        src_ref=ins[a] if mode == "gather" else ins[a].at[pidx], dst_ref=lands[a].at[me],
                        send_sem=sems[2 * g].at[j * npeer + slot], recv_sem=sems[2 * g + 1].at[j * npeer + slot],
                        device_id=peer, device_id_type=pl.DeviceIdType.MESH).start()
                a += 1
        for cp in stores:
            cp.wait()
        token[...] = jnp.zeros_like(token)

    out_shape, out_specs = [], []
    for sz in sizes:
        out_shape += [pltpu.SemaphoreType.DMA((sz * npeer,)), pltpu.SemaphoreType.DMA((sz * npeer,))]
        out_specs += [_SEM, _SEM]
    out_shape += [pltpu.HBM(a.shape, a.dtype) for a, _ in flat]
    out_shape += [pltpu.HBM(_land_shape(a, m), a.dtype) for a, m in flat]
    out_specs += [_HBM] * (2 * n)
    out_shape.append(jax.ShapeDtypeStruct((8, LANES), F32))
    out_specs.append(pl.BlockSpec(memory_space=pltpu.VMEM))
    args = [pltpu.with_memory_space_constraint(a, pltpu.HBM) for a, _ in flat]
    args += [pltpu.with_memory_space_constraint(lax.empty(_land_shape(a, m), a.dtype), pltpu.HBM) for a, m in flat]
    res = pl.pallas_call(
        body, name=name, out_shape=tuple(out_shape),
        in_specs=[_HBM] * (2 * n) + [pl.BlockSpec(memory_space=pl.ANY)], out_specs=tuple(out_specs),
        input_output_aliases={i: 2 * ngr + i for i in range(2 * n)},
        scratch_shapes=[pltpu.SemaphoreType.DMA((n,))]
                       + [pltpu.VMEM(a.shape if m == "gather" else a.shape[1:], a.dtype) for a, m in flat],
        compiler_params=pltpu.CompilerParams(has_side_effects=_EFFECT, vmem_limit_bytes=VMEM_BIG),
    )(*args, after)
    handles, a = [], 0
    for g, sz in enumerate(sizes):
        handles.append((res[2 * g], res[2 * g + 1], list(res[2 * ngr + a:2 * ngr + a + sz]),
                        list(res[2 * ngr + n + a:2 * ngr + n + a + sz]), list(groups[g][1])))
        a += sz
    return handles, res[-1]


def _xwait(handle, after, name):
    send_sems, recv_sems, srcs, lands, modes = handle
    m = len(srcs)
    npeer = N_DEV - 1
    after = list(after)

    def body(*refs):
        ins, lnd = refs[:m], refs[m:2 * m]
        ssem, rsem = refs[2 * m], refs[2 * m + 1]
        x, y, c = lax.axis_index("x"), lax.axis_index("y"), lax.axis_index("c")
        for j in range(m):
            for slot, peer, pidx in _peers(x, y, c):
                cp = pltpu.make_async_remote_copy(
                    src_ref=ins[j] if modes[j] == "gather" else ins[j].at[pidx], dst_ref=lnd[j].at[pidx],
                    send_sem=ssem.at[j * npeer + slot], recv_sem=rsem.at[j * npeer + slot],
                    device_id=peer, device_id_type=pl.DeviceIdType.MESH)
                cp.wait_send()
                cp.wait_recv()

    res = pl.pallas_call(
        body, name=name,
        out_shape=tuple([pltpu.HBM(a.shape, a.dtype) for a in srcs] + [pltpu.HBM(a.shape, a.dtype) for a in lands]),
        in_specs=[_HBM] * (2 * m) + [_SEM, _SEM] + [pl.BlockSpec(memory_space=pl.ANY)] * len(after),
        out_specs=tuple([_HBM] * (2 * m)),
        input_output_aliases={i: i for i in range(2 * m)},
        compiler_params=pltpu.CompilerParams(has_side_effects=_EFFECT),
    )(*srcs, *lands, send_sems, recv_sems, *after)
    return list(res[m:])


def _mod_fwd(c_all, mod_w, kv_mod_w):
    nl, d, mw = mod_w.shape
    kw = kv_mod_w.shape[1]

    def body(c_ref, mw_ref, kw_ref, o_ref, sc_ref):
        cc = c_ref[...]
        sc = (cc * _sigmoid(cc)).astype(BF16)
        sc_ref[...] = sc
        for l in range(nl):
            o_ref[:, l * mw:(l + 1) * mw] = jnp.dot(sc, mw_ref[l].astype(BF16), preferred_element_type=F32)
        o_ref[:, nl * mw:nl * mw + kw] = jnp.dot(sc, kw_ref[...].astype(BF16), preferred_element_type=F32)

    return pl.pallas_call(
        body, name="mod_fwd",
        out_shape=(jax.ShapeDtypeStruct((c_all.shape[0], nl * mw + kw), F32),
                   jax.ShapeDtypeStruct(c_all.shape, BF16)),
        compiler_params=_params(VMEM_BIG),
    )(c_all, mod_w, kv_mod_w)


def _vec_prep(modrow, modb, kvrow, kvb, ng, kvg):
    d = ng.shape[1]

    def body(mr_ref, mb_ref, kr_ref, kb_ref, ng_ref, kvg_ref, t_ref, m_ref):
        t_ref[...] = jnp.zeros_like(t_ref)
        m_ref[...] = jnp.zeros_like(m_ref)
        for l in range(2):
            mod = mr_ref[l] + mb_ref[l]
            m_ref[6 * l:6 * l + 6, :] = mod
            g = ng_ref[4 * l:4 * l + 4, :]
            t_ref[6 * l + R_W1:6 * l + R_W1 + 1, :] = g[0:1] * (1.0 + mod[1:2])
            t_ref[6 * l + R_SH1:6 * l + R_SH1 + 1, :] = mod[0:1]
            t_ref[6 * l + R_P1:6 * l + R_P1 + 1, :] = mod[2:3] * g[1:2]
            t_ref[6 * l + R_W2:6 * l + R_W2 + 1, :] = g[2:3] * (1.0 + mod[4:5])
            t_ref[6 * l + R_SH2:6 * l + R_SH2 + 1, :] = mod[3:4]
            t_ref[6 * l + R_P2:6 * l + R_P2 + 1, :] = mod[5:6] * g[3:4]
        kv = kr_ref[...] + kb_ref[...]
        m_ref[R_KV:R_KV + 2, :] = kv
        t_ref[R_KV:R_KV + 1, :] = kvg_ref[...] * (1.0 + kv[1:2])
        t_ref[R_KV + 1:R_KV + 2, :] = kv[0:1]

    return pl.pallas_call(
        body, name="vec_prep",
        out_shape=(jax.ShapeDtypeStruct((16, d), F32), jax.ShapeDtypeStruct((16, d), F32)),
    )(modrow, modb, kvrow, kvb, ng, kvg)


def _vec_bwd(sums_c, sums_f0, sums_q, sums_o, sums_f1, mt, ng, kvg):
    d = ng.shape[1]

    def body(sc_ref, sf0_ref, sq_ref, so_ref, sf1_ref, m_ref, ng_ref, kvg_ref, dm_ref, dng_ref, dkvg_ref, g_ref):
        g_ref[...] = jnp.zeros_like(g_ref)
        g_ref[0:3, :] = sc_ref[0:3, :]
        g_ref[3:6, :] = sf0_ref[3:6, :]
        g_ref[6:8, :] = sq_ref[0:2, :]
        g_ref[8:9, :] = so_ref[2:3, :]
        g_ref[9:12, :] = sf1_ref[3:6, :]
        g_ref[R_KV:R_KV + 2, :] = sq_ref[2:4, :]
        dm_ref[...] = jnp.zeros_like(dm_ref)
        dkvg_ref[...] = jnp.zeros_like(dkvg_ref)
        for l in range(2):
            g = ng_ref[4 * l:4 * l + 4, :]
            mod = m_ref[6 * l:6 * l + 6, :]
            s = g_ref[6 * l:6 * l + 6, :]
            dm_ref[6 * l + 0:6 * l + 1, :] = s[1:2]
            dm_ref[6 * l + 1:6 * l + 2, :] = s[0:1] * g[0:1]
            dm_ref[6 * l + 2:6 * l + 3, :] = s[2:3] * g[1:2]
            dm_ref[6 * l + 3:6 * l + 4, :] = s[4:5]
            dm_ref[6 * l + 4:6 * l + 5, :] = s[3:4] * g[2:3]
            dm_ref[6 * l + 5:6 * l + 6, :] = s[5:6] * g[3:4]
            dng_ref[4 * l + 0:4 * l + 1, :] = s[0:1] * (1.0 + mod[1:2])
            dng_ref[4 * l + 1:4 * l + 2, :] = s[2:3] * mod[2:3]
            dng_ref[4 * l + 2:4 * l + 3, :] = s[3:4] * (1.0 + mod[4:5])
            dng_ref[4 * l + 3:4 * l + 4, :] = s[5:6] * mod[5:6]
        dm_ref[R_KV:R_KV + 1, :] = g_ref[R_KV + 1:R_KV + 2, :]
        dm_ref[R_KV + 1:R_KV + 2, :] = g_ref[R_KV:R_KV + 1, :] * kvg_ref[...]
        dkvg_ref[0:1, :] = g_ref[R_KV:R_KV + 1, :] * (1.0 + m_ref[R_KV + 1:R_KV + 2, :])

    return pl.pallas_call(
        body, name="vec_bwd",
        out_shape=(jax.ShapeDtypeStruct((16, d), F32), jax.ShapeDtypeStruct((8, d), F32),
                   jax.ShapeDtypeStruct((8, d), F32)),
        scratch_shapes=[pltpu.VMEM((16, d), F32)],
    )(sums_c, sums_f0, sums_q, sums_o, sums_f1, mt, ng, kvg)


def _rel_index(nrel):
    width = KW + QB
    e = lax.broadcasted_iota(jnp.int32, (nrel, width), 1)
    r = lax.broadcasted_iota(jnp.int32, (nrel, width), 0)
    max_rel = (nrel - 1) // 2
    idx = jnp.clip(KW - e, -max_rel, max_rel) + max_rel
    return (idx == r).astype(F32)


def _band_valid():
    row = lax.broadcasted_iota(jnp.int32, (QB, KW), 0) // CHUNK
    col = lax.broadcasted_iota(jnp.int32, (QB, KW), 1) // CHUNK
    j = col - row
    return (j >= 0) & (j <= N_LEFT)


def _bias_fwd(rel_bias):
    nh, nrel = rel_bias.shape
    width = KW + QB

    def body(rb_ref, o_ref):
        onehot = _rel_index(nrel)
        gr = jnp.dot(rb_ref[...], onehot, preferred_element_type=F32, precision=lax.Precision.HIGHEST)
        valid = _band_valid() & _key_valid(pl.program_id(0))
        for h in range(nh):
            xrow = jnp.broadcast_to(gr[h:h + 1, :], (QB, width))
            rolled = pltpu.roll(xrow, 0, 1, stride=1, stride_axis=0)
            o_ref[h] = jnp.where(valid, rolled[:, QB:], NEG)

    return pl.pallas_call(
        body, name="bias_fwd", grid=(BIAS_VARIANTS,),
        in_specs=[pl.BlockSpec(rel_bias.shape, lambda v: (0, 0))],
        out_specs=pl.BlockSpec((None, nh, QB, KW), lambda v: (v, 0, 0, 0)),
        out_shape=jax.ShapeDtypeStruct((BIAS_VARIANTS, nh, QB, KW), F32),
        compiler_params=_params(VMEM_BIG),
    )(rel_bias)


def _bias_bwd(dbias, nrel):
    nh = dbias.shape[0]
    width = KW + QB

    def body(db_ref, o_ref, diag_ref):
        onehot = _rel_index(nrel)
        valid = _band_valid()
        rr = lax.broadcasted_iota(jnp.int32, (QB, QB), 0)
        cc = lax.broadcasted_iota(jnp.int32, (QB, QB), 1)
        flip = (rr + cc == QB - 1).astype(F32)
        for h in range(nh):
            rev = jnp.dot(flip, jnp.where(valid, db_ref[h], 0.0), preferred_element_type=F32,
                          precision=lax.Precision.HIGHEST)
            w = jnp.concatenate([jnp.zeros((QB, QB), F32), rev], axis=1)
            back = pltpu.roll(w, width - (QB - 1), 1, stride=1, stride_axis=0)
            diag_ref[h:h + 1, :] = _colsum(back)
        o_ref[...] = lax.dot_general(diag_ref[...], onehot, NT, preferred_element_type=F32,
                                     precision=lax.Precision.HIGHEST)

    return pl.pallas_call(
        body, name="bias_bwd",
        out_shape=jax.ShapeDtypeStruct((nh, nrel), F32),
        scratch_shapes=[pltpu.VMEM((nh, width), F32)],
        compiler_params=_params(VMEM_BIG),
    )(dbias)


def _conv_fwd(x, tab, ck, wci, wco, tm):
    s, d = x.shape
    nsh, _, cw = wci.shape

    def body(x_ref, t_ref, ck_ref, wci_ref, wco_ref, x1_ref, h_ref, bcx_ref, u_ref, y_ref, carry):
        @pl.when(pl.program_id(0) == 0)
        def _():
            carry[...] = jnp.zeros_like(carry)

        xv = x_ref[...]
        hb = ((xv * _rs(xv)) * t_ref[R_W1:R_W1 + 1, :] + t_ref[R_SH1:R_SH1 + 1, :]).astype(BF16)
        h_ref[...] = hb
        for j in range(nsh):
            bcx_ref[:, j * cw:(j + 1) * cw] = jnp.dot(hb, wci_ref[j], preferred_element_type=F32)
        bg, cg, xi = bcx_ref[:, 0:d], bcx_ref[:, d:2 * d], bcx_ref[:, 2 * d:3 * d]
        z = cg * xi
        row = lax.broadcasted_iota(jnp.int32, z.shape, 0)
        c1, c2 = carry[7:8, :], carry[6:7, :]
        z1 = jnp.where(row == 0, c1, pltpu.roll(z, 1, 0))
        z2 = jnp.where(row == 0, c2, jnp.where(row == 1, c1, pltpu.roll(z, 2, 0)))
        carry[...] = z[tm - 8:tm, :]
        conv = ck_ref[0:1, :] * z2 + ck_ref[1:2, :] * z1 + ck_ref[2:3, :] * z
        ub = (bg * conv).astype(BF16)
        u_ref[...] = ub
        yv = jnp.dot(ub, wco_ref[...], preferred_element_type=F32)
        y_ref[...] = yv
        x1_ref[...] = xv + (yv * _rs(yv)) * t_ref[R_P1:R_P1 + 1, :]

    return pl.pallas_call(
        body, name="conv_fwd", grid=(s // tm,),
        in_specs=[_rows(tm, d), _const(tab.shape), _const(ck.shape), _const(wci.shape), _const(wco.shape)],
        out_specs=(_rows(tm, d), _rows(tm, d), _rows(tm, 3 * d), _rows(tm, d), _rows(tm, d)),
        out_shape=(jax.ShapeDtypeStruct((s, d), F32), jax.ShapeDtypeStruct((s, d), BF16),
                   jax.ShapeDtypeStruct((s, 3 * d), F32), jax.ShapeDtypeStruct((s, d), BF16),
                   jax.ShapeDtypeStruct((s, d), F32)),
        scratch_shapes=[pltpu.VMEM((8, d), F32)],
        compiler_params=_params(VMEM_BIG),
    )(x, tab, ck, wci, wco)


def _ffn_fwd(x, tab, base, wfi, wfo, tgt, tm, name):
    s, d = x.shape
    nsh, fw, _ = wfi.shape
    nh = nsh // 2
    with_loss = tgt is not None

    def body(*refs):
        if with_loss:
            x_ref, t_ref, wfi_ref, wfo_ref, tgt_ref, xo_ref, h_ref, gu_ref, y_ref, loss_ref = refs
        else:
            x_ref, t_ref, wfi_ref, wfo_ref, xo_ref, h_ref, gu_ref, y_ref = refs
        xv = x_ref[...]
        hb = ((xv * _rs(xv)) * t_ref[base + R_W2:base + R_W2 + 1, :]
              + t_ref[base + R_SH2:base + R_SH2 + 1, :]).astype(BF16)
        h_ref[...] = hb
        acc = jnp.zeros((tm, d), F32)
        for j in range(nh):
            g = lax.dot_general(hb, wfi_ref[j], NT, preferred_element_type=F32)
            u = lax.dot_general(hb, wfi_ref[j + nh], NT, preferred_element_type=F32)
            gu_ref[j] = g.astype(BF16)
            gu_ref[j + nh] = u.astype(BF16)
            ab = ((g * _sigmoid(g)) * u).astype(BF16)
            acc = acc + jnp.dot(ab, wfo_ref[j], preferred_element_type=F32)
        y_ref[...] = acc
        xo = xv + (acc * _rs(acc)) * t_ref[base + R_P2:base + R_P2 + 1, :]
        if with_loss:
            @pl.when(pl.program_id(0) == 0)
            def _():
                loss_ref[...] = jnp.zeros_like(loss_ref)

            err = xo - tgt_ref[...]
            xo_ref[...] = err * (1.0 / d)
            loss_ref[...] += jnp.sum(err * err)
        else:
            xo_ref[...] = xo

    in_specs = [_rows(tm, d), _const(tab.shape), _const(wfi.shape), _const(wfo.shape)]
    args = [x, tab, wfi, wfo]
    out_specs = [_rows(tm, d), _rows(tm, d), pl.BlockSpec((nsh, tm, fw), lambda i: (0, i, 0)), _rows(tm, d)]
    out_shape = [jax.ShapeDtypeStruct((s, d), F32), jax.ShapeDtypeStruct((s, d), BF16),
                 jax.ShapeDtypeStruct((nsh, s, fw), BF16), jax.ShapeDtypeStruct((s, d), F32)]
    if with_loss:
        in_specs.append(_rows(tm, d))
        args.append(tgt)
        out_specs.append(pl.BlockSpec((8, LANES), lambda i: (0, 0)))
        out_shape.append(jax.ShapeDtypeStruct((8, LANES), F32))
    return pl.pallas_call(
        body, name=name, grid=(s // tm,), in_specs=in_specs, out_specs=tuple(out_specs),
        out_shape=tuple(out_shape), compiler_params=_params(VMEM_BIG),
    )(*args)


def _qkv_fwd(x, tab, wq, wkv, tm):
    s, d = x.shape
    nsh, _, kw = wkv.shape
    nh = nsh // 2
    base = 6

    def body(x_ref, t_ref, wq_ref, wkv_ref, hkv_ref, h1_ref, q_ref, k_ref, v_ref):
        xv = x_ref[...]
        n = xv * _rs(xv)
        hkv = (n * t_ref[R_KV:R_KV + 1, :] + t_ref[R_KV + 1:R_KV + 2, :]).astype(BF16)
        h1 = (n * t_ref[base + R_W1:base + R_W1 + 1, :] + t_ref[base + R_SH1:base + R_SH1 + 1, :]).astype(BF16)
        hkv_ref[...] = hkv
        h1_ref[...] = h1
        q_ref[...] = (jnp.dot(h1, wq_ref[...], preferred_element_type=F32) * (HEAD_DIM ** -0.5)).astype(BF16)
        for j in range(nh):
            k_ref[:, j * kw:(j + 1) * kw] = jnp.dot(hkv, wkv_ref[j], preferred_element_type=F32).astype(BF16)
            v_ref[:, j * kw:(j + 1) * kw] = jnp.dot(hkv, wkv_ref[j + nh], preferred_element_type=F32).astype(BF16)

    act = jax.ShapeDtypeStruct((s, d), BF16)
    return pl.pallas_call(
        body, name="qkv_fwd", grid=(s // tm,),
        in_specs=[_rows(tm, d), _const(tab.shape), _const(wq.shape), _const(wkv.shape)],
        out_specs=tuple([_rows(tm, d)] * 5), out_shape=(act,) * 5,
        compiler_params=_params(VMEM_BIG),
    )(x, tab, wq, wkv)


def _window_specs(per=1):
    return [pl.BlockSpec((QB, LANES), (lambda p, b, w=w: (jnp.maximum(per * b - 2 + w, 0), p)))
            for w in range(2 + per)]


def _key_valid(b):
    col = lax.broadcasted_iota(jnp.int32, (QB, KW), 1) // CHUNK
    return (b * (QB // CHUNK) - N_LEFT + col) >= 0


def _bias_spec(per=1, sub=0):
    return pl.BlockSpec((None, LANES // HEAD_DIM, QB, KW),
                        lambda p, b: (jnp.minimum(per * b + sub, BIAS_VARIANTS - 1), p, 0, 0))


def _head_masks():
    lane = lax.broadcasted_iota(jnp.int32, (1, LANES), 1)
    return [(lane // HEAD_DIM == hh) for hh in range(LANES // HEAD_DIM)]


def _attn_fwd(q, k, v, bias):
    s, d = q.shape
    per = ATTN_PER
    npair, nb = d // LANES, s // (per * QB)
    hpp = LANES // HEAD_DIM
    nwin = 2 + per
    nbias = min(per, BIAS_VARIANTS)

    def body(*refs):
        q_ref, k_refs, v_refs = refs[0], refs[1:1 + nwin], refs[1 + nwin:1 + 2 * nwin]
        bias_refs = refs[1 + 2 * nwin:1 + 2 * nwin + nbias]
        o_ref, lse_ref = refs[1 + 2 * nwin + nbias:]
        ks = [r[...] for r in k_refs]
        vs = [r[...] for r in v_refs]
        masks = _head_masks()
        for sub in range(per):
            rows = slice(sub * QB, (sub + 1) * QB)
            qv = q_ref[rows, :]
            kwin = jnp.concatenate(ks[sub:sub + 3], axis=0)
            vwin = jnp.concatenate(vs[sub:sub + 3], axis=0)
            o = jnp.zeros((QB, LANES), F32)
            lse = jnp.zeros((QB, LANES), F32)
            scs = [lax.dot_general(jnp.where(masks[hh], qv, jnp.zeros_like(qv)), kwin, NT,
                                   preferred_element_type=F32) + bias_refs[min(sub, nbias - 1)][hh]
                   for hh in range(hpp)]
            for hh in range(hpp):
                vm = jnp.where(masks[hh], vwin, jnp.zeros_like(vwin))
                sc = scs[hh]
                m = jnp.max(sc, axis=-1, keepdims=True)
                p = jnp.exp(sc - m)
                l = jnp.sum(p, axis=-1, keepdims=True)
                o = o + jnp.dot(p.astype(BF16), vm, preferred_element_type=F32) * (1.0 / l)
                lse = jnp.where(masks[hh], m + jnp.log(l), lse)
            o_ref[rows, :] = o.astype(BF16)
            lse_ref[rows, :] = lse

    blk = pl.BlockSpec((per * QB, LANES), lambda p, b: (b, p))
    return pl.pallas_call(
        body, name="attn_fwd", grid=(npair, nb),
        in_specs=[blk] + _window_specs(per) + _window_specs(per) + [_bias_spec(per, sub) for sub in range(nbias)],
        out_specs=(blk, blk),
        out_shape=(jax.ShapeDtypeStruct((s, d), BF16), jax.ShapeDtypeStruct((s, d), F32)),
        compiler_params=_params(VMEM_BIG),
    )(q, *([k] * nwin), *([v] * nwin), *([bias] * nbias))


def _attn_out_fwd(o, x, tab, wo, tm):
    s, d = x.shape
    base = 6

    def body(o_ref, x_ref, t_ref, wo_ref, x3_ref, y_ref):
        yv = jnp.dot(o_ref[...], wo_ref[...], preferred_element_type=F32)
        y_ref[...] = yv
        x3_ref[...] = x_ref[...] + (yv * _rs(yv)) * t_ref[base + R_P1:base + R_P1 + 1, :]

    return pl.pallas_call(
        body, name="attn_out_fwd", grid=(s // tm,),
        in_specs=[_rows(tm, d), _rows(tm, d), _const(tab.shape), _const(wo.shape)],
        out_specs=(_rows(tm, d), _rows(tm, d)),
        out_shape=(jax.ShapeDtypeStruct((s, d), F32), jax.ShapeDtypeStruct((s, d), F32)),
        compiler_params=_params(VMEM_BIG),
    )(o, x, tab, wo)


def _ffn_bwd(dxo, x, y, gu, tab, base, wfi, wfo, tm, name):
    s, d = x.shape
    nsh, fw, _ = wfi.shape
    nh = nsh // 2

    def body(dxo_ref, x_ref, y_ref, gu_ref, t_ref, wfi_ref, wfo_ref, dx_ref, dyb_ref, dgu_ref, a_ref, sums_ref):
        @pl.when(pl.program_id(0) == 0)
        def _():
            sums_ref[...] = jnp.zeros_like(sums_ref)

        dxo_v = dxo_ref[...]
        yv = y_ref[...]
        ry = _rs(yv)
        ny = yv * ry
        sums_ref[R_P2:R_P2 + 1, :] += _colsum(dxo_v * ny)
        dyb = _norm_bwd(dxo_v * t_ref[base + R_P2:base + R_P2 + 1, :], ny, ry).astype(BF16)
        dyb_ref[...] = dyb
        dh = jnp.zeros((tm, d), F32)
        for j in range(nh):
            da = lax.dot_general(dyb, wfo_ref[j], NT, preferred_element_type=F32)
            g, u = gu_ref[j].astype(F32), gu_ref[j + nh].astype(F32)
            sg = _sigmoid(g)
            gs = g * sg
            a_ref[j] = (gs * u).astype(BF16)
            dg = (da * u * sg * (1.0 + g * (1.0 - sg))).astype(BF16)
            du = (da * gs).astype(BF16)
            dgu_ref[j] = dg
            dgu_ref[j + nh] = du
            dh = dh + jnp.dot(dg, wfi_ref[j], preferred_element_type=F32)
            dh = dh + jnp.dot(du, wfi_ref[j + nh], preferred_element_type=F32)
        xv = x_ref[...]
        r = _rs(xv)
        n = xv * r
        sums_ref[R_SH2:R_SH2 + 1, :] += _colsum(dh)
        sums_ref[R_W2:R_W2 + 1, :] += _colsum(dh * n)
        dx_ref[...] = dxo_v + _norm_bwd(dh * t_ref[base + R_W2:base + R_W2 + 1, :], n, r)

    stack = lambda n: pl.BlockSpec((n, tm, fw), lambda i: (0, i, 0))
    return pl.pallas_call(
        body, name=name, grid=(s // tm,),
        in_specs=[_rows(tm, d), _rows(tm, d), _rows(tm, d), stack(nsh),
                  _const(tab.shape), _const(wfi.shape), _const(wfo.shape)],
        out_specs=(_rows(tm, d), _rows(tm, d), stack(nsh), stack(nh), pl.BlockSpec((8, d), lambda i: (0, 0))),
        out_shape=(jax.ShapeDtypeStruct((s, d), F32), jax.ShapeDtypeStruct((s, d), BF16),
                   jax.ShapeDtypeStruct((nsh, s, fw), BF16), jax.ShapeDtypeStruct((nh, s, fw), BF16),
                   jax.ShapeDtypeStruct((8, d), F32)),
        compiler_params=_params(VMEM_BIG),
    )(dxo, x, y, gu, tab, wfi, wfo)


def _attn_out_bwd(dx, y, tab, wo, tm):
    s, d = y.shape
    base = 6

    def body(dx_ref, y_ref, t_ref, wo_ref, dyb_ref, do_ref, sums_ref):
        @pl.when(pl.program_id(0) == 0)
        def _():
            sums_ref[...] = jnp.zeros_like(sums_ref)

        dxv = dx_ref[...]
        yv = y_ref[...]
        ry = _rs(yv)
        ny = yv * ry
        sums_ref[R_P1:R_P1 + 1, :] += _colsum(dxv * ny)
        dyb = _norm_bwd(dxv * t_ref[base + R_P1:base + R_P1 + 1, :], ny, ry).astype(BF16)
        dyb_ref[...] = dyb
        do_ref[...] = lax.dot_general(dyb, wo_ref[...], NT, preferred_element_type=F32).astype(BF16)

    return pl.pallas_call(
        body, name="attn_out_bwd", grid=(s // tm,),
        in_specs=[_rows(tm, d), _rows(tm, d), _const(tab.shape), _const(wo.shape)],
        out_specs=(_rows(tm, d), _rows(tm, d), pl.BlockSpec((8, d), lambda i: (0, 0))),
        out_shape=(jax.ShapeDtypeStruct((s, d), BF16), jax.ShapeDtypeStruct((s, d), BF16),
                   jax.ShapeDtypeStruct((8, d), F32)),
        compiler_params=_params(VMEM_BIG),
    )(dx, y, tab, wo)


def _attn_bwd(q, k, v, o, do, lse, bias):
    s, d = q.shape
    per = ATTN_PER
    npair, nb = d // LANES, s // (per * QB)
    hpp = LANES // HEAD_DIM
    nwin = 2 + per
    nbias = min(per, BIAS_VARIANTS)

    def body(*refs):
        q_ref, k_refs, v_refs = refs[0], refs[1:1 + nwin], refs[1 + nwin:1 + 2 * nwin]
        o_ref, do_ref, lse_ref = refs[1 + 2 * nwin:4 + 2 * nwin]
        bias_refs = refs[4 + 2 * nwin:4 + 2 * nwin + nbias]
        dq_ref, dk_ref, dv_ref, db_ref = refs[4 + 2 * nwin + nbias:]
        b = pl.program_id(1)

        @pl.when(b == 0)
        def _():
            dk_ref[...] = jnp.zeros_like(dk_ref)
            dv_ref[...] = jnp.zeros_like(dv_ref)
            db_ref[...] = jnp.zeros_like(db_ref)

        ks = [r[...] for r in k_refs]
        vs = [r[...] for r in v_refs]
        masks = _head_masks()
        for sub in range(per):
            rows = slice(sub * QB, (sub + 1) * QB)
            qv = q_ref[rows, :]
            dov = do_ref[rows, :]
            lsev = lse_ref[rows, :]
            doo = dov.astype(F32) * o_ref[rows, :].astype(F32)
            kwin = jnp.concatenate(ks[sub:sub + 3], axis=0)
            vwin = jnp.concatenate(vs[sub:sub + 3], axis=0)
            dq = jnp.zeros((QB, LANES), F32)
            dkw = jnp.zeros((KW, LANES), F32)
            dvw = jnp.zeros((KW, LANES), F32)
            for hh in range(hpp):
                qm = jnp.where(masks[hh], qv, jnp.zeros_like(qv))
                dom = jnp.where(masks[hh], dov, jnp.zeros_like(dov))
                km = jnp.where(masks[hh], kwin, jnp.zeros_like(kwin))
                lse_h = jnp.max(jnp.where(masks[hh], lsev, NEG), axis=-1, keepdims=True)
                delta = jnp.sum(jnp.where(masks[hh], doo, 0.0), axis=-1, keepdims=True)
                sc = lax.dot_general(qm, kwin, NT, preferred_element_type=F32) + bias_refs[min(sub, nbias - 1)][hh]
                p = jnp.exp(sc - lse_h)
                dp = lax.dot_general(dom, vwin, NT, preferred_element_type=F32)
                ds = p * (dp - delta)
                db_ref[hh] += ds
                dsb = ds.astype(BF16)
                dq = dq + jnp.dot(dsb, km, preferred_element_type=F32)
                dkw = dkw + lax.dot_general(dsb, qm, TN, preferred_element_type=F32)
                dvw = dvw + lax.dot_general(p.astype(BF16), dom, TN, preferred_element_type=F32)
            dq_ref[rows, :] = (dq * (HEAD_DIM ** -0.5)).astype(BF16)
            for w in range(3):
                start = pl.multiple_of(jnp.maximum(per * b + sub - 2 + w, 0) * QB, QB)
                dk_ref[pl.ds(start, QB), :] += dkw[w * QB:(w + 1) * QB, :]
                dv_ref[pl.ds(start, QB), :] += dvw[w * QB:(w + 1) * QB, :]

    blk = pl.BlockSpec((per * QB, LANES), lambda p, b: (b, p))
    col = pl.BlockSpec((s, LANES), lambda p, b: (0, p))
    pair = pl.BlockSpec((hpp, QB, KW), lambda p, b: (p, 0, 0))
    return pl.pallas_call(
        body, name="attn_bwd", grid=(npair, nb),
        in_specs=[blk] + _window_specs(per) + _window_specs(per) + [blk, blk, blk]
                 + [_bias_spec(per, sub) for sub in range(nbias)],
        out_specs=(blk, col, col, pair),
        out_shape=(jax.ShapeDtypeStruct((s, d), BF16), jax.ShapeDtypeStruct((s, d), F32),
                   jax.ShapeDtypeStruct((s, d), F32), jax.ShapeDtypeStruct(bias.shape[1:], F32)),
        compiler_params=_params(VMEM_BIG),
    )(q, *([k] * nwin), *([v] * nwin), o, do, lse, *([bias] * nbias))


def _qkv_bwd(dres, dq, dk, dv, x, tab, wq, wkv, tm):
    s, d = x.shape
    nsh, _, kw = wkv.shape
    nh = nsh // 2
    base = 6

    def body(dres_ref, dq_ref, dk_ref, dv_ref, x_ref, t_ref, wq_ref, wkv_ref, dx_ref, dkv_ref, sums_ref):
        @pl.when(pl.program_id(0) == 0)
        def _():
            sums_ref[...] = jnp.zeros_like(sums_ref)

        dh1 = lax.dot_general(dq_ref[...], wq_ref[...], NT, preferred_element_type=F32)
        dkv_ref[:, 0:d] = dk_ref[...].astype(BF16)
        dkv_ref[:, d:2 * d] = dv_ref[...].astype(BF16)
        dhkv = jnp.zeros((tm, d), F32)
        for j in range(nsh):
            dhkv = dhkv + lax.dot_general(dkv_ref[:, j * kw:(j + 1) * kw], wkv_ref[j], NT,
                                          preferred_element_type=F32)
        xv = x_ref[...]
        r = _rs(xv)
        n = xv * r
        sums_ref[0:1, :] += _colsum(dh1 * n)
        sums_ref[1:2, :] += _colsum(dh1)
        sums_ref[2:3, :] += _colsum(dhkv * n)
        sums_ref[3:4, :] += _colsum(dhkv)
        dn = dh1 * t_ref[base + R_W1:base + R_W1 + 1, :] + dhkv * t_ref[R_KV:R_KV + 1, :]
        dx_ref[...] = dres_ref[...] + _norm_bwd(dn, n, r)

    return pl.pallas_call(
        body, name="qkv_bwd", grid=(s // tm,),
        in_specs=[_rows(tm, d)] * 5 + [_const(tab.shape), _const(wq.shape), _const(wkv.shape)],
        out_specs=(_rows(tm, d), _rows(tm, 2 * d), pl.BlockSpec((8, d), lambda i: (0, 0))),
        out_shape=(jax.ShapeDtypeStruct((s, d), F32), jax.ShapeDtypeStruct((s, 2 * d), BF16),
                   jax.ShapeDtypeStruct((8, d), F32)),
        compiler_params=_params(VMEM_BIG),
    )(dres, dq, dk, dv, x, tab, wq, wkv)


def _conv_bwd(dx1, x, y, bcx, tab, ck, wci, wco, tm):
    s, d = x.shape
    nsh, _, cw = wci.shape
    nt = s // tm

    def rev(i):
        return (nt - 1 - i, 0)

    def halo(i):
        return (jnp.maximum((nt - 1 - i) * (tm // 8) - 1, 0), 0)

    def body(dx_ref, x_ref, y_ref, bcx_ref, halo_ref, t_ref, ck_ref, wci_ref, wco_ref,
             dx0_ref, dyb_ref, dbcx_ref, sums_ref, dck_ref, carry):
        i = pl.program_id(0)

        @pl.when(i == 0)
        def _():
            sums_ref[...] = jnp.zeros_like(sums_ref)
            dck_ref[...] = jnp.zeros_like(dck_ref)
            carry[...] = jnp.zeros_like(carry)

        dxv = dx_ref[...]
        yv = y_ref[...]
        ry = _rs(yv)
        ny = yv * ry
        sums_ref[R_P1:R_P1 + 1, :] += _colsum(dxv * ny)
        dyb = _norm_bwd(dxv * t_ref[R_P1:R_P1 + 1, :], ny, ry).astype(BF16)
        dyb_ref[...] = dyb
        du = lax.dot_general(dyb, wco_ref[...], NT, preferred_element_type=F32)
        bg, cg, xi = bcx_ref[:, 0:d], bcx_ref[:, d:2 * d], bcx_ref[:, 2 * d:3 * d]
        z = cg * xi
        zp = halo_ref[:, d:2 * d] * halo_ref[:, 2 * d:3 * d]
        zp = jnp.where(i == nt - 1, jnp.zeros_like(zp), zp)
        row = lax.broadcasted_iota(jnp.int32, z.shape, 0)
        c1, c2 = zp[7:8, :], zp[6:7, :]
        z1 = jnp.where(row == 0, c1, pltpu.roll(z, 1, 0))
        z2 = jnp.where(row == 0, c2, jnp.where(row == 1, c1, pltpu.roll(z, 2, 0)))
        k0, k1, k2 = ck_ref[0:1, :], ck_ref[1:2, :], ck_ref[2:3, :]
        conv = k0 * z2 + k1 * z1 + k2 * z
        dconv = du * bg
        dck_ref[0:1, :] += _colsum(dconv * z2)
        dck_ref[1:2, :] += _colsum(dconv * z1)
        dck_ref[2:3, :] += _colsum(dconv * z)
        n1, n2 = carry[0:1, :], carry[1:2, :]
        d1 = jnp.where(row == tm - 1, n1, pltpu.roll(dconv, tm - 1, 0))
        d2 = jnp.where(row == tm - 1, n2, jnp.where(row == tm - 2, n1, pltpu.roll(dconv, tm - 2, 0)))
        carry[...] = dconv[0:8, :]
        dz = k2 * dconv + k1 * d1 + k0 * d2
        dbcx_ref[:, 0:d] = (du * conv).astype(BF16)
        dbcx_ref[:, d:2 * d] = (dz * xi).astype(BF16)
        dbcx_ref[:, 2 * d:3 * d] = (dz * cg).astype(BF16)
        dh = jnp.zeros((tm, d), F32)
        for j in range(nsh):
            dh = dh + lax.dot_general(dbcx_ref[:, j * cw:(j + 1) * cw], wci_ref[j], NT,
                                      preferred_element_type=F32)
        xv = x_ref[...]
        r = _rs(xv)
        n = xv * r
        sums_ref[R_W1:R_W1 + 1, :] += _colsum(dh * n)
        sums_ref[R_SH1:R_SH1 + 1, :] += _colsum(dh)
        dx0_ref[...] = dxv + _norm_bwd(dh * t_ref[R_W1:R_W1 + 1, :], n, r)

    rrow = lambda cols: pl.BlockSpec((tm, cols), rev)
    acc = pl.BlockSpec((8, d), lambda i: (0, 0))
    return pl.pallas_call(
        body, name="conv_bwd", grid=(nt,),
        in_specs=[rrow(d), rrow(d), rrow(d), rrow(3 * d), pl.BlockSpec((8, 3 * d), halo),
                  _const(tab.shape), _const(ck.shape), _const(wci.shape), _const(wco.shape)],
        out_specs=(rrow(d), rrow(d), rrow(3 * d), acc, acc),
        out_shape=(jax.ShapeDtypeStruct((s, d), F32), jax.ShapeDtypeStruct((s, d), BF16),
                   jax.ShapeDtypeStruct((s, 3 * d), BF16), jax.ShapeDtypeStruct((8, d), F32),
                   jax.ShapeDtypeStruct((8, d), F32)),
        scratch_shapes=[pltpu.VMEM((8, d), F32)],
        compiler_params=_params(VMEM_BIG),
    )(dx1, x, y, bcx, bcx, tab, ck, wci, wco)


def _wgrad(a, b, nblk, a_spec, b_spec, m, n, tk, name):
    s = a.shape[-2]
    nk = s // tk

    def body(a_ref, b_ref, o_ref, acc):
        kk = pl.program_id(1)

        @pl.when(kk == 0)
        def _():
            acc[...] = jnp.zeros_like(acc)

        acc[...] += lax.dot_general(a_ref[...], b_ref[...], TN, preferred_element_type=F32)

        @pl.when(kk == nk - 1)
        def _():
            o_ref[...] = acc[...].astype(BF16)

    return pl.pallas_call(
        body, name=name, grid=(nblk, nk),
        in_specs=[a_spec, b_spec],
        out_specs=pl.BlockSpec((None, m, n), lambda j, kk: (j, 0, 0)),
        out_shape=jax.ShapeDtypeStruct((nblk, m, n), BF16),
        scratch_shapes=[pltpu.VMEM((m, n), F32)],
        compiler_params=_params(VMEM_BIG),
    )(a, b)


def _wgrad_wide(a, b, nblk, tk, name):
    s, m = a.shape
    n = b.shape[1] // nblk
    nk = s // tk

    def body(a_ref, b_ref, o_ref, acc):
        kk = pl.program_id(0)

        @pl.when(kk == 0)
        def _():
            acc[...] = jnp.zeros_like(acc)

        acc[...] += jnp.dot(a_ref[...].T, b_ref[...], preferred_element_type=F32)

        @pl.when(kk == nk - 1)
        def _():
            for j in range(nblk):
                o_ref[j] = acc[:, j * n:(j + 1) * n].astype(BF16)

    return pl.pallas_call(
        body, name=name, grid=(nk,),
        in_specs=[pl.BlockSpec((tk, m), lambda kk: (kk, 0)), pl.BlockSpec((tk, nblk * n), lambda kk: (kk, 0))],
        out_specs=pl.BlockSpec((nblk, m, n), lambda kk: (0, 0, 0)),
        out_shape=jax.ShapeDtypeStruct((nblk, m, n), BF16),
        scratch_shapes=[pltpu.VMEM((m, nblk * n), F32)],
        compiler_params=_params(VMEM_BIG),
    )(a, b)


def _wgrad_astack(a, b, tk, name):
    nblk, _, m = a.shape
    n = b.shape[1]
    return _wgrad(a, b, nblk, pl.BlockSpec((None, tk, m), lambda j, kk: (j, kk, 0)),
                  pl.BlockSpec((tk, n), lambda j, kk: (kk, 0)), m, n, tk, name)


def _adamw_math(w, g, m, v):
    m = ADAM_B1 * m + (1.0 - ADAM_B1) * g
    v = ADAM_B2 * v + (1.0 - ADAM_B2) * (g * g)
    m_hat = m / (1.0 - ADAM_B1 ** ADAM_STEP)
    v_hat = v / (1.0 - ADAM_B2 ** ADAM_STEP)
    delta = -ADAM_LR * (m_hat / (jnp.sqrt(v_hat) + ADAM_EPS) + ADAM_WD * w)
    return delta, m, v


def _adamw_reduce(parts, w, m, v, tr, name):
    nl, r, c = w.shape
    tr = _row_tile(r, tr)

    def body(*refs):
        p_refs = refs[:nl]
        w_ref, m_ref, v_ref, g_ref, d_ref, mo_ref, vo_ref = refs[nl:]
        layer = pl.program_id(0)

        def partial(i):
            val = p_refs[0][i].astype(F32)
            for q in range(1, nl):
                val = jnp.where(layer == q, p_refs[q][i].astype(F32), val)
            return val

        g = partial(0)
        for i in range(1, N_DEV):
            g = g + partial(i)
        g_ref[...] = g
        d_ref[...], mo_ref[...], vo_ref[...] = _adamw_math(w_ref[...], g, m_ref[...], v_ref[...])

    blk = pl.BlockSpec((None, tr, c), lambda l, i: (l, i, 0))
    out = jax.ShapeDtypeStruct((nl, r, c), F32)
    p_specs = [pl.BlockSpec((N_DEV, tr, c), (lambda l, i, q=q: (0, jnp.where(l == q, i, 0), 0))) for q in range(nl)]
    return pl.pallas_call(
        body, name=name, grid=(nl, r // tr),
        in_specs=p_specs + [blk, blk, blk],
        out_specs=(blk,) * 4, out_shape=(out,) * 4,
        compiler_params=_params(VMEM_BIG),
    )(*parts, w, m, v)


def _adamw_outer(sct, dm, w, m, v, tr, name):
    nl, d, c = w.shape

    def body(s_ref, dm_ref, w_ref, m_ref, v_ref, g_ref, d_ref, mo_ref, vo_ref):
        g = jnp.dot(s_ref[...], dm_ref[...], preferred_element_type=F32)
        g_ref[...] = g
        d_ref[...], mo_ref[...], vo_ref[...] = _adamw_math(w_ref[...], g, m_ref[...], v_ref[...])

    blk = pl.BlockSpec((None, tr, c), lambda l, i: (l, i, 0))
    out = jax.ShapeDtypeStruct((nl, d, c), F32)
    return pl.pallas_call(
        body, name=name, grid=(nl, d // tr),
        in_specs=[pl.BlockSpec((tr, N_DEV), lambda l, i: (i, 0)),
                  pl.BlockSpec((None, N_DEV, c), lambda l, i: (l, 0, 0)), blk, blk, blk],
        out_specs=(blk,) * 4, out_shape=(out,) * 4,
        compiler_params=_params(VMEM_BIG),
    )(sct, dm, w, m, v)


def _pad_rows(a, rows):
    return jnp.concatenate([a, jnp.zeros((rows - a.shape[0],) + a.shape[1:], a.dtype)], axis=0)


def kernel(x, c, mod_w, mod_b, norm_g, ffn_w_in, ffn_w_out, conv_w_in, conv_k, conv_w_out, kv_mod_w, kv_mod_b, kv_norm_g, w_kv, attn_w_q, attn_w_o, rel_bias, loss_target, m_mod_w, m_mod_b, m_norm_g, m_ffn_w_in, m_ffn_w_out, m_conv_w_in, m_conv_k, m_conv_w_out, m_kv_mod_w, m_kv_mod_b, m_kv_norm_g, m_w_kv, m_attn_w_q, m_attn_w_o, m_rel_bias, v_mod_w, v_mod_b, v_norm_g, v_ffn_w_in, v_ffn_w_out, v_conv_w_in, v_conv_k, v_conv_w_out, v_kv_mod_w, v_kv_mod_b, v_kv_norm_g, v_w_kv, v_attn_w_q, v_attn_w_o, v_rel_bias):
    s, d = x.shape[1], x.shape[2]
    dq = d // LANES
    dsh = d // N_DEV
    nl = mod_w.shape[0]
    mw = mod_w.shape[2]
    kmw = kv_mod_w.shape[1]
    fw = ffn_w_in.shape[2]
    nh, nrel = rel_bias.shape[1], rel_bias.shape[2]
    tm = min(256, s)
    tm2 = min(512, s)
    tk = min(1024, s)
    tk2 = min(2048, s)
    me = 4 * lax.axis_index("x") + 2 * lax.axis_index("y") + lax.axis_index("c")

    x0 = x[0]
    tgt = loss_target[0]

    small1 = jnp.concatenate([c.reshape(dq, LANES), norm_g.reshape(dq, LANES),
                              _pad_rows(conv_k[0], 8).reshape(dq, LANES)], axis=0)
    (sm,) = _exchange([small1], ["gather"], "gather_small")
    c_all = sm[:, 0:dq].reshape(N_DEV, d)
    ng_full = jnp.transpose(sm[:, dq:2 * dq].reshape(N_DEV, 8, dsh), (1, 0, 2)).reshape(8, d)
    ck_full = jnp.transpose(sm[:, 2 * dq:3 * dq].reshape(N_DEV, 8, dsh), (1, 0, 2)).reshape(8, d)

    modcols, silu_c = _mod_fwd(c_all, mod_w, kv_mod_w)
    (modall,) = _exchange([modcols], ["gather"], "gather_mod")

    cast = lambda *ws: [a.astype(BF16) for a in ws]
    gath = lambda ws: (ws, ["gather"] * len(ws))
    (h_conv, h_ffn0, h_attn, h_ffn1), token = _xstart(
        [gath(cast(conv_w_in[0], conv_w_out[0])), gath(cast(jnp.swapaxes(ffn_w_in[0], 0, 1), ffn_w_out[0])),
         gath(cast(w_kv, attn_w_q[0], attn_w_o[0])), gath(cast(jnp.swapaxes(ffn_w_in[1], 0, 1), ffn_w_out[1]))],
        modall, "gather_start")
    modall = modall + token[0, 0]
    mine = lax.dynamic_index_in_dim(modall, me, axis=1, keepdims=False)
    modrow = jnp.stack([mine[:, l * mw:(l + 1) * mw].reshape(6, d) for l in range(nl)])
    kvrow = mine[:, nl * mw:nl * mw + kmw].reshape(2, d)
    tab, modval = _vec_prep(modrow, mod_b.reshape(nl, 6, d), kvrow, kv_mod_b.reshape(2, d), ng_full,
                            kv_norm_g.reshape(1, d))
    bias = _bias_fwd(rel_bias[0])

    wci, wco = _xwait(h_conv, [bias], "gather_wait_conv")
    wco = wco.reshape(d, d)
    x1, h1a, bcx, ua, ya = _conv_fwd(x0, tab, ck_full, wci, wco, tm2)
    wfi0, wfo0 = _xwait(h_ffn0, [x1], "gather_wait_ffn0")
    wfo0 = wfo0.reshape(N_DEV // 2, -1, d)
    x2, h2a, gua, y2a = _ffn_fwd(x1, tab, 0, wfi0, wfo0, None, tm2, "ffn_fwd0")
    wkv, wq, wo = _xwait(h_attn, [x2], "gather_wait_attn")
    wq, wo = wq.reshape(d, d), wo.reshape(d, d)
    hkv, h1b, q, k, v = _qkv_fwd(x2, tab, wq, wkv, tm2)
    o, lse = _attn_fwd(q, k, v, bias)
    x3, yb = _attn_out_fwd(o, x2, tab, wo, tm2)
    wfi1, wfo1 = _xwait(h_ffn1, [x3], "gather_wait_ffn1")
    wfo1 = wfo1.reshape(N_DEV // 2, -1, d)
    dx4, h2b, gub, y2b, loss_acc = _ffn_fwd(x3, tab, 6, wfi1, wfo1, tgt, tm2, "ffn_fwd1")

    scat = lambda ws: [(ws, ["scatter"] * len(ws))]
    dx3, dy2b, dgub, ab, sums_f1 = _ffn_bwd(dx4, x3, y2b, gub, tab, 6, wfi1, wfo1, tm, "ffn_bwd1")
    g_wfi1 = _wgrad_astack(dgub, h2b, tk2, "wgrad_ffn_in1")
    g_wfo1 = _wgrad_astack(ab, dy2b, tk2, "wgrad_ffn_out1").reshape(N_DEV, -1, d)
    (h_g1,), token = _xstart(scat([g_wfi1, g_wfo1]), dx3, "grads_start_ffn1")
    tab = tab + token[0, 0]
    dyb, do, sums_o = _attn_out_bwd(dx3, yb, tab, wo, tm2)
    g_wo = _wgrad_wide(o, dyb, 1, tk, "wgrad_o").reshape(N_DEV, dsh, d)
    dqb, dk, dv, dbias = _attn_bwd(q, k, v, o, do, lse, bias)
    g_wq = _wgrad_wide(h1b, dqb, 1, tk, "wgrad_q").reshape(N_DEV, dsh, d)
    dx2, dkvb, sums_q = _qkv_bwd(dx3, dqb, dk, dv, x2, tab, wq, wkv, tm2)
    g_wkv = _wgrad_wide(hkv, dkvb, N_DEV, tk, "wgrad_kv")
    (h_g2,), token = _xstart(scat([g_wkv, g_wq, g_wo]), dx2, "grads_start_attn")
    tab = tab + token[0, 0]
    dx1, dy2a, dgua, aa, sums_f0 = _ffn_bwd(dx2, x1, y2a, gua, tab, 0, wfi0, wfo0, tm, "ffn_bwd0")
    g_wfi0 = _wgrad_astack(dgua, h2a, tk2, "wgrad_ffn_in0")
    g_wfo0 = _wgrad_astack(aa, dy2a, tk2, "wgrad_ffn_out0").reshape(N_DEV, -1, d)
    (h_g3,), token = _xstart(scat([g_wfi0, g_wfo0]), dx1, "grads_start_ffn0")
    tab = tab + token[0, 0]
    dx0, dya, dbcx, sums_c, dck = _conv_bwd(dx1, x0, ya, bcx, tab, ck_full, wci, wco, tm2)
    drel = _bias_bwd(dbias, nrel)
    dmod, dng, dkvg = _vec_bwd(sums_c, sums_f0, sums_q, sums_o, sums_f1, modval, ng_full, kv_norm_g.reshape(1, d))

    relw = -(-nrel // LANES) * LANES
    drel_p = jnp.concatenate([drel, jnp.zeros((nh, relw - nrel), F32)], axis=1)
    small3 = jnp.concatenate([dmod.reshape(16 * dq, LANES), dng.reshape(8 * dq, LANES), dkvg.reshape(8 * dq, LANES),
                              dck.reshape(8 * dq, LANES), loss_acc,
                              drel_p.reshape(nh * relw // LANES, LANES)], axis=0)
    (sm,) = _exchange([small3], ["gather"], "gather_small_grads")
    g_wci = _wgrad_wide(h1a, dbcx, N_DEV, tk, "wgrad_conv_in")
    g_wco = _wgrad_wide(ua, dya, 1, tk, "wgrad_conv_out").reshape(N_DEV, dsh, d)
    (h_g4,), token = _xstart(scat([g_wci, g_wco]), sm, "grads_start_conv")
    sm = sm + token[0, 0]
    o1, o2, o3, o4, o5 = 16 * dq, 24 * dq, 32 * dq, 40 * dq, 40 * dq + 8
    loss = jnp.sum(sm[:, o4, 0]) * (0.5 / d)
    dmod_all = sm[:, 0:o1].reshape(N_DEV, 16, d)
    mine_cols = lambda a: lax.dynamic_slice_in_dim(a, me * dsh, dsh, axis=2)
    dng_parts = mine_cols(sm[:, o1:o2].reshape(N_DEV, 8, d))
    dkvg_parts = sm[:, o2:o3].reshape(N_DEV, 8, d)[:, 0:1]
    dck_parts = mine_cols(sm[:, o3:o4].reshape(N_DEV, 8, d))[:, 0:3]
    drel_parts = sm[:, o5:].reshape(N_DEV, nh, relw)[:, :, 0:nrel]

    def update(parts, w, m, v, name, layers=1):
        shp = w.shape
        w3, m3, v3 = (a.reshape(layers, -1, shp[-1]) for a in (w, m, v))
        outs = _adamw_reduce([p.reshape(N_DEV, -1, shp[-1]) for p in parts], w3, m3, v3, 256, name)
        return [a.reshape(shp) for a in outs]

    p_wfi1, p_wfo1 = _xwait(h_g1, [sm], "grads_wait_ffn1")
    p_wfi0, p_wfo0 = _xwait(h_g3, [p_wfi1], "grads_wait_ffn0")
    tr = lambda a: jnp.swapaxes(a, 1, 2)
    u_ffn_in = [tr(a) for a in update([p_wfi0, p_wfi1], tr(ffn_w_in), tr(m_ffn_w_in), tr(v_ffn_w_in),
                                      "adamw_ffn_in", 2)]
    u_ffn_out = update([p_wfo0, p_wfo1], ffn_w_out, m_ffn_w_out, v_ffn_w_out, "adamw_ffn_out", 2)
    p_wkv, p_wq, p_wo = _xwait(h_g2, [u_ffn_out[0]], "grads_wait_attn")
    u_w_kv = update([p_wkv], w_kv, m_w_kv, v_w_kv, "adamw_w_kv")
    u_w_q = update([p_wq], attn_w_q, m_attn_w_q, v_attn_w_q, "adamw_w_q")
    u_w_o = update([p_wo], attn_w_o, m_attn_w_o, v_attn_w_o, "adamw_w_o")

    sct = jnp.transpose(silu_c)
    dm_mod = jnp.stack([lax.dynamic_slice_in_dim(dmod_all[:, 6 * l:6 * l + 6].reshape(N_DEV, 6 * d), me * mw, mw, axis=1)
                        for l in range(nl)]).astype(BF16)
    dm_kv = lax.dynamic_slice_in_dim(dmod_all[:, R_KV:R_KV + 2].reshape(N_DEV, 2 * d), me * kmw, kmw, axis=1)
    u_mod_w = _adamw_outer(sct, dm_mod, mod_w, m_mod_w, v_mod_w, min(256, d), "adamw_mod_w")
    u_kv_mod_w = [a[0] for a in _adamw_outer(sct, dm_kv.astype(BF16)[None], kv_mod_w[None], m_kv_mod_w[None],
                                             v_kv_mod_w[None], min(256, d), "adamw_kv_mod_w")]

    modb_parts = jnp.stack([dmod_all[:, 6 * l:6 * l + 6].reshape(N_DEV, 6 * d) for l in range(nl)], axis=1)
    u_mod_b = update([modb_parts], mod_b, m_mod_b, v_mod_b, "adamw_mod_b")
    u_norm_g = update([dng_parts], norm_g.reshape(8, dsh), m_norm_g.reshape(8, dsh), v_norm_g.reshape(8, dsh), "adamw_norm_g")
    u_norm_g = [a.reshape(norm_g.shape) for a in u_norm_g]
    u_conv_k = update([dck_parts], conv_k, m_conv_k, v_conv_k, "adamw_conv_k")
    kvb_parts = dmod_all[:, R_KV:R_KV + 2].reshape(N_DEV, 1, 2 * d)
    u_kv_mod_b = [a.reshape(kv_mod_b.shape) for a in update([kvb_parts], kv_mod_b.reshape(1, -1), m_kv_mod_b.reshape(1, -1),
                                                            v_kv_mod_b.reshape(1, -1), "adamw_kv_mod_b")]
    u_kv_norm_g = [a.reshape(kv_norm_g.shape) for a in update([dkvg_parts], kv_norm_g.reshape(1, -1), m_kv_norm_g.reshape(1, -1),
                                                              v_kv_norm_g.reshape(1, -1), "adamw_kv_norm_g")]
    u_rel = update([drel_parts], rel_bias, m_rel_bias, v_rel_bias, "adamw_rel_bias")

    others = [u_ffn_in, u_ffn_out, u_w_kv, u_w_q, u_w_o, u_mod_w, u_kv_mod_w, u_mod_b, u_norm_g, u_conv_k, u_kv_mod_b,
              u_kv_norm_g, u_rel]
    p_wci, p_wco = _xwait(h_g4, [u[3] for u in others], "grads_wait_conv")
    u_conv_in = update([p_wci], conv_w_in, m_conv_w_in, v_conv_w_in, "adamw_conv_in")
    u_conv_out = update([p_wco], conv_w_out, m_conv_w_out, v_conv_w_out, "adamw_conv_out")

    ups = [u_mod_w, u_mod_b, u_norm_g, u_ffn_in, u_ffn_out, u_conv_in, u_conv_k, u_conv_out, u_kv_mod_w, u_kv_mod_b,
           u_kv_norm_g, u_w_kv, u_w_q, u_w_o, u_rel]
    return (loss, dx0[None], *[u[0] for u in ups], *[u[1] for u in ups], *[u[2] for u in ups], *[u[3] for u in ups])
```

```python
import jax
import jax.numpy as jnp
from jax import lax
from jax.experimental import pallas as pl
from jax.experimental.pallas import tpu as pltpu

F32 = jnp.float32
BF16 = jnp.bfloat16

EPS = 1e-6
CHUNK = 64
HEAD_DIM = 64
N_LEFT = 8
LANES = 128
MXU = 256
FFN_CHUNK = 4
QB = 4 * CHUNK
KW = QB + N_LEFT * CHUNK
BIAS_VARIANTS = N_LEFT * CHUNK // QB + 1
ATTN_PER = 8
NEG = -1e30
N_DEV = 8

ADAM_LR = 0.001
ADAM_B1 = 0.9
ADAM_B2 = 0.999
ADAM_EPS = 1e-08
ADAM_WD = 0.01
ADAM_STEP = 10

VMEM_BIG = 56 * 1024 * 1024

NT = (((1,), (1,)), ((), ()))
TN = (((0,), (0,)), ((), ()))

R_W1, R_SH1, R_P1, R_W2, R_SH2, R_P2 = range(6)
R_KV = 12


def _params(vmem):
    return pltpu.CompilerParams(vmem_limit_bytes=vmem)


def _row_tile(rows, cap):
    for t in range(min(cap, rows) // 16 * 16, 0, -16):
        if rows % t == 0:
            return t
    return rows


def _rows(tm, cols):
    return pl.BlockSpec((tm, cols), lambda i: (i, 0))


def _const(shape):
    nd = len(shape)
    return pl.BlockSpec(shape, lambda *_: (0,) * nd, pipeline_mode=pl.Buffered(1))


def _rs(x):
    return lax.rsqrt(jnp.mean(x * x, axis=-1, keepdims=True) + EPS)


def _norm_bwd(d, n, r):
    return r * (d - n * jnp.mean(d * n, axis=-1, keepdims=True))


def _colsum(a):
    return jnp.sum(a, axis=0, keepdims=True)


def _sigmoid(g):
    return 1.0 / (1.0 + jnp.exp(-g))


def _exchange(arrays, modes, name):
    n = len(arrays)
    out_shape = []
    for a, mode in zip(arrays, modes):
        shp = (N_DEV,) + a.shape if mode == "gather" else a.shape
        out_shape.append(jax.ShapeDtypeStruct(shp, a.dtype))

    def body(*refs):
        ins, outs = refs[:n], refs[n:2 * n]
        send_sems, recv_sems, local_sems = refs[2 * n:]
        x, y, c = lax.axis_index("x"), lax.axis_index("y"), lax.axis_index("c")
        me = 4 * x + 2 * y + c
        local, sends, recvs = [], [], []
        for a in range(n):
            own = ins[a] if modes[a] == "gather" else ins[a].at[me]
            cp = pltpu.make_async_copy(own, outs[a].at[me], local_sems.at[a])
            cp.start()
            local.append(cp)
        for k in range(1, N_DEV):
            px = 1 - x if k & 4 else x
            py = 1 - y if k & 2 else y
            pc = 1 - c if k & 1 else c
            peer = 4 * px + 2 * py + pc
            for a in range(n):
                src = ins[a] if modes[a] == "gather" else ins[a].at[peer]
                sem = a * (N_DEV - 1) + k - 1
                cp = pltpu.make_async_remote_copy(
                    src_ref=src, dst_ref=outs[a].at[me],
                    send_sem=send_sems.at[sem], recv_sem=recv_sems.at[sem],
                    device_id=(px, py, pc), device_id_type=pl.DeviceIdType.MESH)
                cp.start()
                sends.append(cp)
                recvs.append(pltpu.make_async_remote_copy(
                    src_ref=src, dst_ref=outs[a].at[peer],
                    send_sem=send_sems.at[sem], recv_sem=recv_sems.at[sem],
                    device_id=(px, py, pc), device_id_type=pl.DeviceIdType.MESH))
        for cp in recvs:
            cp.wait_recv()
        for cp in sends:
            cp.wait_send()
        for cp in local:
            cp.wait()

    any_spec = pl.BlockSpec(memory_space=pl.ANY)
    return pl.pallas_call(
        body, name=name,
        out_shape=tuple(out_shape),
        in_specs=[any_spec] * n,
        out_specs=tuple([any_spec] * n),
        scratch_shapes=[
            pltpu.SemaphoreType.DMA((n * (N_DEV - 1),)),
            pltpu.SemaphoreType.DMA((n * (N_DEV - 1),)),
            pltpu.SemaphoreType.DMA((n,)),
        ],
    )(*arrays)


def _peers(x, y, c):
    out = []
    for k in range(1, N_DEV):
        px = 1 - x if k & 4 else x
        py = 1 - y if k & 2 else y
        pc = 1 - c if k & 1 else c
        out.append((k - 1, (px, py, pc), 4 * px + 2 * py + pc))
    return out


def _land_shape(a, mode):
    return (N_DEV,) + a.shape if mode == "gather" else a.shape


_HBM = pl.BlockSpec(memory_space=pltpu.HBM)
_SEM = pl.BlockSpec(memory_space=pltpu.SEMAPHORE)
_EFFECT = pltpu.SideEffectType.DATAFLOW_SIDE_EFFECTING


def _xstart(groups, after, name):
    flat = [(a, m) for arrays, modes in groups for a, m in zip(arrays, modes)]
    n, ngr = len(flat), len(groups)
    sizes = [len(arrays) for arrays, _ in groups]
    npeer = N_DEV - 1

    def body(*refs):
        ins, lands = refs[:n], refs[n:2 * n]
        outs = refs[2 * n + 1:]
        sems = outs[:2 * ngr]
        token = outs[2 * ngr + 2 * n]
        local_sems = outs[2 * ngr + 2 * n + 1]
        stage = outs[2 * ngr + 2 * n + 2:]
        x, y, c = lax.axis_index("x"), lax.axis_index("y"), lax.axis_index("c")
        me = 4 * x + 2 * y + c
        loads, stores = [], []
        for a in range(n):
            own = ins[a] if flat[a][1] == "gather" else ins[a].at[me]
            loads.append(pltpu.make_async_copy(own, stage[a], local_sems.at[a]))
            stores.append(pltpu.make_async_copy(stage[a], lands[a].at[me], local_sems.at[a]))
            loads[a].start()
        for a in range(n):
            loads[a].wait()
            stores[a].start()
        a = 0
        for g in range(ngr):
            for j in range(sizes[g]):
                mode = flat[a][1]
                for slot, peer, pidx in _peers(x, y, c):
                    pltpu.make_async_remote_copy(
                        src_ref=ins[a] if mode == "gather" else ins[a].at[pidx], dst_ref=lands[a].at[me],
                        send_sem=sems[2 * g].at[j * npeer + slot], recv_sem=sems[2 * g + 1].at[j * npeer + slot],
                        device_id=peer, device_id_type=pl.DeviceIdType.MESH).start()
                a += 1
        for cp in stores:
            cp.wait()
        token[...] = jnp.zeros_like(token)

    out_shape, out_specs = [], []
    for sz in sizes:
        out_shape += [pltpu.SemaphoreType.DMA((sz * npeer,)), pltpu.SemaphoreType.DMA((sz * npeer,))]
        out_specs += [_SEM, _SEM]
    out_shape += [pltpu.HBM(a.shape, a.dtype) for a, _ in flat]
    out_shape += [pltpu.HBM(_land_shape(a, m), a.dtype) for a, m in flat]
    out_specs += [_HBM] * (2 * n)
    out_shape.append(jax.ShapeDtypeStruct((8, LANES), F32))
    out_specs.append(pl.BlockSpec(memory_space=pltpu.VMEM))
    args = [pltpu.with_memory_space_constraint(a, pltpu.HBM) for a, _ in flat]
    args += [pltpu.with_memory_space_constraint(lax.empty(_land_shape(a, m), a.dtype), pltpu.HBM) for a, m in flat]
    res = pl.pallas_call(
        body, name=name, out_shape=tuple(out_shape),
        in_specs=[_HBM] * (2 * n) + [pl.BlockSpec(memory_space=pl.ANY)], out_specs=tuple(out_specs),
        input_output_aliases={i: 2 * ngr + i for i in range(2 * n)},
        scratch_shapes=[pltpu.SemaphoreType.DMA((n,))]
                       + [pltpu.VMEM(a.shape if m == "gather" else a.shape[1:], a.dtype) for a, m in flat],
        compiler_params=pltpu.CompilerParams(has_side_effects=_EFFECT, vmem_limit_bytes=VMEM_BIG),
    )(*args, after)
    handles, a = [], 0
    for g, sz in enumerate(sizes):
        handles.append((res[2 * g], res[2 * g + 1], list(res[2 * ngr + a:2 * ngr + a + sz]),
                        list(res[2 * ngr + n + a:2 * ngr + n + a + sz]), list(groups[g][1])))
        a += sz
    return handles, res[-1]


def _xwait(handle, after, name):
    send_sems, recv_sems, srcs, lands, modes = handle
    m = len(srcs)
    npeer = N_DEV - 1
    after = list(after)

    def body(*refs):
        ins, lnd = refs[:m], refs[m:2 * m]
        ssem, rsem = refs[2 * m], refs[2 * m + 1]
        x, y, c = lax.axis_index("x"), lax.axis_index("y"), lax.axis_index("c")
        for j in range(m):
            for slot, peer, pidx in _peers(x, y, c):
                cp = pltpu.make_async_remote_copy(
                    src_ref=ins[j] if modes[j] == "gather" else ins[j].at[pidx], dst_ref=lnd[j].at[pidx],
                    send_sem=ssem.at[j * npeer + slot], recv_sem=rsem.at[j * npeer + slot],
                    device_id=peer, device_id_type=pl.DeviceIdType.MESH)
                cp.wait_send()
                cp.wait_recv()

    res = pl.pallas_call(
        body, name=name,
        out_shape=tuple([pltpu.HBM(a.shape, a.dtype) for a in srcs] + [pltpu.HBM(a.shape, a.dtype) for a in lands]),
        in_specs=[_HBM] * (2 * m) + [_SEM, _SEM] + [pl.BlockSpec(memory_space=pl.ANY)] * len(after),
        out_specs=tuple([_HBM] * (2 * m)),
        input_output_aliases={i: i for i in range(2 * m)},
        compiler_params=pltpu.CompilerParams(has_side_effects=_EFFECT),
    )(*srcs, *lands, send_sems, recv_sems, *after)
    return list(res[m:])


def _mod_fwd(c_all, mod_w, kv_mod_w):
    nl, d, mw = mod_w.shape
    kw = kv_mod_w.shape[1]

    def body(c_ref, mw_ref, kw_ref, o_ref, sc_ref):
        cc = c_ref[...]
        sc = (cc * _sigmoid(cc)).astype(BF16)
        sc_ref[...] = sc
        for l in range(nl):
            o_ref[:, l * mw:(l + 1) * mw] = jnp.dot(sc, mw_ref[l].astype(BF16), preferred_element_type=F32)
        o_ref[:, nl * mw:nl * mw + kw] = jnp.dot(sc, kw_ref[...].astype(BF16), preferred_element_type=F32)

    return pl.pallas_call(
        body, name="mod_fwd",
        out_shape=(jax.ShapeDtypeStruct((c_all.shape[0], nl * mw + kw), F32),
                   jax.ShapeDtypeStruct(c_all.shape, BF16)),
        compiler_params=_params(VMEM_BIG),
    )(c_all, mod_w, kv_mod_w)


def _vec_prep(modrow, modb, kvrow, kvb, ng, kvg):
    d = ng.shape[1]

    def body(mr_ref, mb_ref, kr_ref, kb_ref, ng_ref, kvg_ref, t_ref, m_ref):
        t_ref[...] = jnp.zeros_like(t_ref)
        m_ref[...] = jnp.zeros_like(m_ref)
        for l in range(2):
            mod = mr_ref[l] + mb_ref[l]
            m_ref[6 * l:6 * l + 6, :] = mod
            g = ng_ref[4 * l:4 * l + 4, :]
            t_ref[6 * l + R_W1:6 * l + R_W1 + 1, :] = g[0:1] * (1.0 + mod[1:2])
            t_ref[6 * l + R_SH1:6 * l + R_SH1 + 1, :] = mod[0:1]
            t_ref[6 * l + R_P1:6 * l + R_P1 + 1, :] = mod[2:3] * g[1:2]
            t_ref[6 * l + R_W2:6 * l + R_W2 + 1, :] = g[2:3] * (1.0 + mod[4:5])
            t_ref[6 * l + R_SH2:6 * l + R_SH2 + 1, :] = mod[3:4]
            t_ref[6 * l + R_P2:6 * l + R_P2 + 1, :] = mod[5:6] * g[3:4]
        kv = kr_ref[...] + kb_ref[...]
        m_ref[R_KV:R_KV + 2, :] = kv
        t_ref[R_KV:R_KV + 1, :] = kvg_ref[...] * (1.0 + kv[1:2])
        t_ref[R_KV + 1:R_KV + 2, :] = kv[0:1]

    return pl.pallas_call(
        body, name="vec_prep",
        out_shape=(jax.ShapeDtypeStruct((16, d), F32), jax.ShapeDtypeStruct((16, d), F32)),
    )(modrow, modb, kvrow, kvb, ng, kvg)


def _vec_bwd(sums_c, sums_f0, sums_q, sums_o, sums_f1, mt, ng, kvg):
    d = ng.shape[1]

    def body(sc_ref, sf0_ref, sq_ref, so_ref, sf1_ref, m_ref, ng_ref, kvg_ref, dm_ref, dng_ref, dkvg_ref, g_ref):
        g_ref[...] = jnp.zeros_like(g_ref)
        g_ref[0:3, :] = sc_ref[0:3, :]
        g_ref[3:6, :] = sf0_ref[3:6, :]
        g_ref[6:8, :] = sq_ref[0:2, :]
        g_ref[8:9, :] = so_ref[2:3, :]
        g_ref[9:12, :] = sf1_ref[3:6, :]
        g_ref[R_KV:R_KV + 2, :] = sq_ref[2:4, :]
        dm_ref[...] = jnp.zeros_like(dm_ref)
        dkvg_ref[...] = jnp.zeros_like(dkvg_ref)
        for l in range(2):
            g = ng_ref[4 * l:4 * l + 4, :]
            mod = m_ref[6 * l:6 * l + 6, :]
            s = g_ref[6 * l:6 * l + 6, :]
            dm_ref[6 * l + 0:6 * l + 1, :] = s[1:2]
            dm_ref[6 * l + 1:6 * l + 2, :] = s[0:1] * g[0:1]
            dm_ref[6 * l + 2:6 * l + 3, :] = s[2:3] * g[1:2]
            dm_ref[6 * l + 3:6 * l + 4, :] = s[4:5]
            dm_ref[6 * l + 4:6 * l + 5, :] = s[3:4] * g[2:3]
            dm_ref[6 * l + 5:6 * l + 6, :] = s[5:6] * g[3:4]
            dng_ref[4 * l + 0:4 * l + 1, :] = s[0:1] * (1.0 + mod[1:2])
            dng_ref[4 * l + 1:4 * l + 2, :] = s[2:3] * mod[2:3]
            dng_ref[4 * l + 2:4 * l + 3, :] = s[3:4] * (1.0 + mod[4:5])
            dng_ref[4 * l + 3:4 * l + 4, :] = s[5:6] * mod[5:6]
        dm_ref[R_KV:R_KV + 1, :] = g_ref[R_KV + 1:R_KV + 2, :]
        dm_ref[R_KV + 1:R_KV + 2, :] = g_ref[R_KV:R_KV + 1, :] * kvg_ref[...]
        dkvg_ref[0:1, :] = g_ref[R_KV:R_KV + 1, :] * (1.0 + m_ref[R_KV + 1:R_KV + 2, :])

    return pl.pallas_call(
        body, name="vec_bwd",
        out_shape=(jax.ShapeDtypeStruct((16, d), F32), jax.ShapeDtypeStruct((8, d), F32),
                   jax.ShapeDtypeStruct((8, d), F32)),
        scratch_shapes=[pltpu.VMEM((16, d), F32)],
    )(sums_c, sums_f0, sums_q, sums_o, sums_f1, mt, ng, kvg)


def _rel_index(nrel):
    width = KW + QB
    e = lax.broadcasted_iota(jnp.int32, (nrel, width), 1)
    r = lax.broadcasted_iota(jnp.int32, (nrel, width), 0)
    max_rel = (nrel - 1) // 2
    idx = jnp.clip(KW - e, -max_rel, max_rel) + max_rel
    return (idx == r).astype(F32)


def _band_valid():
    row = lax.broadcasted_iota(jnp.int32, (QB, KW), 0) // CHUNK
    col = lax.broadcasted_iota(jnp.int32, (QB, KW), 1) // CHUNK
    j = col - row
    return (j >= 0) & (j <= N_LEFT)


def _bias_fwd(rel_bias):
    nh, nrel = rel_bias.shape
    width = KW + QB

    def body(rb_ref, o_ref):
        onehot = _rel_index(nrel)
        gr = jnp.dot(rb_ref[...], onehot, preferred_element_type=F32, precision=lax.Precision.HIGHEST)
        valid = _band_valid() & _key_valid(pl.program_id(0))
        for h in range(nh):
            xrow = jnp.broadcast_to(gr[h:h + 1, :], (QB, width))
            rolled = pltpu.roll(xrow, 0, 1, stride=1, stride_axis=0)
            o_ref[h] = jnp.where(valid, rolled[:, QB:], NEG)

    return pl.pallas_call(
        body, name="bias_fwd", grid=(BIAS_VARIANTS,),
        in_specs=[pl.BlockSpec(rel_bias.shape, lambda v: (0, 0))],
        out_specs=pl.BlockSpec((None, nh, QB, KW), lambda v: (v, 0, 0, 0)),
        out_shape=jax.ShapeDtypeStruct((BIAS_VARIANTS, nh, QB, KW), F32),
        compiler_params=_params(VMEM_BIG),
    )(rel_bias)


def _bias_bwd(dbias, nrel):
    nh = dbias.shape[0]
    width = KW + QB

    def body(db_ref, o_ref, diag_ref):
        onehot = _rel_index(nrel)
        valid = _band_valid()
        rr = lax.broadcasted_iota(jnp.int32, (QB, QB), 0)
        cc = lax.broadcasted_iota(jnp.int32, (QB, QB), 1)
        flip = (rr + cc == QB - 1).astype(F32)
        for h in range(nh):
            rev = jnp.dot(flip, jnp.where(valid, db_ref[h], 0.0), preferred_element_type=F32,
                          precision=lax.Precision.HIGHEST)
            w = jnp.concatenate([jnp.zeros((QB, QB), F32), rev], axis=1)
            back = pltpu.roll(w, width - (QB - 1), 1, stride=1, stride_axis=0)
            diag_ref[h:h + 1, :] = _colsum(back)
        o_ref[...] = lax.dot_general(diag_ref[...], onehot, NT, preferred_element_type=F32,
                                     precision=lax.Precision.HIGHEST)

    return pl.pallas_call(
        body, name="bias_bwd",
        out_shape=jax.ShapeDtypeStruct((nh, nrel), F32),
        scratch_shapes=[pltpu.VMEM((nh, width), F32)],
        compiler_params=_params(VMEM_BIG),
    )(dbias)


def _conv_fwd(x, tab, ck, wci, wco, tm):
    s, d = x.shape

    def body(x_ref, t_ref, ck_ref, wci_ref, wco_ref, x1_ref, h_ref, bcx_ref, u_ref, y_ref, carry):
        @pl.when(pl.program_id(0) == 0)
        def _():
            carry[...] = jnp.zeros_like(carry)

        xv = x_ref[...]
        hb = ((xv * _rs(xv)) * t_ref[R_W1:R_W1 + 1, :] + t_ref[R_SH1:R_SH1 + 1, :]).astype(BF16)
        h_ref[...] = hb
        for j in range(3 * d // MXU):
            bcx_ref[:, j * MXU:(j + 1) * MXU] = lax.dot_general(hb, wci_ref[j * MXU:(j + 1) * MXU, :], NT,
                                                                preferred_element_type=F32)
        bg, cg, xi = bcx_ref[:, 0:d], bcx_ref[:, d:2 * d], bcx_ref[:, 2 * d:3 * d]
        z = cg * xi
        row = lax.broadcasted_iota(jnp.int32, z.shape, 0)
        c1, c2 = carry[7:8, :], carry[6:7, :]
        z1 = jnp.where(row == 0, c1, pltpu.roll(z, 1, 0))
        z2 = jnp.where(row == 0, c2, jnp.where(row == 1, c1, pltpu.roll(z, 2, 0)))
        carry[...] = z[tm - 8:tm, :]
        conv = ck_ref[0:1, :] * z2 + ck_ref[1:2, :] * z1 + ck_ref[2:3, :] * z
        ub = (bg * conv).astype(BF16)
        u_ref[...] = ub
        yv = jnp.dot(ub, wco_ref[...], preferred_element_type=F32)
        y_ref[...] = yv
        x1_ref[...] = xv + (yv * _rs(yv)) * t_ref[R_P1:R_P1 + 1, :]

    return pl.pallas_call(
        body, name="conv_fwd", grid=(s // tm,),
        in_specs=[_rows(tm, d), _const(tab.shape), _const(ck.shape), _const(wci.shape), _const(wco.shape)],
        out_specs=(_rows(tm, d), _rows(tm, d), _rows(tm, 3 * d), _rows(tm, d), _rows(tm, d)),
        out_shape=(jax.ShapeDtypeStruct((s, d), F32), jax.ShapeDtypeStruct((s, d), BF16),
                   jax.ShapeDtypeStruct((s, 3 * d), F32), jax.ShapeDtypeStruct((s, d), BF16),
                   jax.ShapeDtypeStruct((s, d), F32)),
        scratch_shapes=[pltpu.VMEM((8, d), F32)],
        compiler_params=_params(VMEM_BIG),
    )(x, tab, ck, wci, wco)


def _ffn_fwd(x, tab, base, wfi, wfo, tgt, tm, name):
    s, d = x.shape
    hid = wfo.shape[0]
    nblk = hid // MXU
    with_loss = tgt is not None

    def body(*refs):
        if with_loss:
            x_ref, t_ref, wfi_ref, wfo_ref, tgt_ref, xo_ref, h_ref, gu_ref, y_ref, loss_ref, a_scr = refs
        else:
            x_ref, t_ref, wfi_ref, wfo_ref, xo_ref, h_ref, gu_ref, y_ref, a_scr = refs
        xv = x_ref[...]
        hb = ((xv * _rs(xv)) * t_ref[base + R_W2:base + R_W2 + 1, :]
              + t_ref[base + R_SH2:base + R_SH2 + 1, :]).astype(BF16)
        h_ref[...] = hb
        acc = jnp.zeros((tm, d), F32)
        for c0 in range(0, nblk, FFN_CHUNK):
            for j in range(c0, min(c0 + FFN_CHUNK, nblk)):
                lo, hi = j * MXU, (j + 1) * MXU
                g = lax.dot_general(hb, wfi_ref[lo:hi, :], NT, preferred_element_type=F32)
                u = lax.dot_general(hb, wfi_ref[hid + lo:hid + hi, :], NT, preferred_element_type=F32)
                gu_ref[:, lo:hi] = g.astype(BF16)
                gu_ref[:, hid + lo:hid + hi] = u.astype(BF16)
                a_scr[:, lo:hi] = ((g * _sigmoid(g)) * u).astype(BF16)
            lo, hi = c0 * MXU, min(c0 + FFN_CHUNK, nblk) * MXU
            acc = acc + jnp.dot(a_scr[:, lo:hi], wfo_ref[lo:hi, :], preferred_element_type=F32)
        y_ref[...] = acc
        xo = xv + (acc * _rs(acc)) * t_ref[base + R_P2:base + R_P2 + 1, :]
        if with_loss:
            @pl.when(pl.program_id(0) == 0)
            def _():
                loss_ref[...] = jnp.zeros_like(loss_ref)

            err = xo - tgt_ref[...]
            xo_ref[...] = err * (1.0 / d)
            loss_ref[...] += jnp.sum(err * err)
        else:
            xo_ref[...] = xo

    in_specs = [_rows(tm, d), _const(tab.shape), _const(wfi.shape), _const(wfo.shape)]
    args = [x, tab, wfi, wfo]
    out_specs = [_rows(tm, d), _rows(tm, d), _rows(tm, 2 * hid), _rows(tm, d)]
    out_shape = [jax.ShapeDtypeStruct((s, d), F32), jax.ShapeDtypeStruct((s, d), BF16),
                 jax.ShapeDtypeStruct((s, 2 * hid), BF16), jax.ShapeDtypeStruct((s, d), F32)]
    if with_loss:
        in_specs.append(_rows(tm, d))
        args.append(tgt)
        out_specs.append(pl.BlockSpec((8, LANES), lambda i: (0, 0)))
        out_shape.append(jax.ShapeDtypeStruct((8, LANES), F32))
    return pl.pallas_call(
        body, name=name, grid=(s // tm,), in_specs=in_specs, out_specs=tuple(out_specs),
        out_shape=tuple(out_shape), scratch_shapes=[pltpu.VMEM((tm, hid), BF16)],
        compiler_params=_params(VMEM_BIG),
    )(*args)


def _qkv_fwd(x, tab, wq, wkv, tm):
    s, d = x.shape
    base = 6

    def body(x_ref, t_ref, wq_ref, wkv_ref, hkv_ref, h1_ref, q_ref, k_ref, v_ref):
        xv = x_ref[...]
        n = xv * _rs(xv)
        hkv = (n * t_ref[R_KV:R_KV + 1, :] + t_ref[R_KV + 1:R_KV + 2, :]).astype(BF16)
        h1 = (n * t_ref[base + R_W1:base + R_W1 + 1, :] + t_ref[base + R_SH1:base + R_SH1 + 1, :]).astype(BF16)
        hkv_ref[...] = hkv
        h1_ref[...] = h1
        q_ref[...] = (jnp.dot(h1, wq_ref[...], preferred_element_type=F32) * (HEAD_DIM ** -0.5)).astype(BF16)
        for j in range(d // MXU):
            lo, hi = j * MXU, (j + 1) * MXU
            k_ref[:, lo:hi] = lax.dot_general(hkv, wkv_ref[lo:hi, :], NT, preferred_element_type=F32).astype(BF16)
            v_ref[:, lo:hi] = lax.dot_general(hkv, wkv_ref[d + lo:d + hi, :], NT,
                                              preferred_element_type=F32).astype(BF16)

    act = jax.ShapeDtypeStruct((s, d), BF16)
    return pl.pallas_call(
        body, name="qkv_fwd", grid=(s // tm,),
        in_specs=[_rows(tm, d), _const(tab.shape), _const(wq.shape), _const(wkv.shape)],
        out_specs=tuple([_rows(tm, d)] * 5), out_shape=(act,) * 5,
        compiler_params=_params(VMEM_BIG),
    )(x, tab, wq, wkv)


def _window_specs(per=1):
    return [pl.BlockSpec((QB, LANES), (lambda p, b, w=w: (jnp.maximum(per * b - 2 + w, 0), p)))
            for w in range(2 + per)]


def _key_valid(b):
    col = lax.broadcasted_iota(jnp.int32, (QB, KW), 1) // CHUNK
    return (b * (QB // CHUNK) - N_LEFT + col) >= 0


def _bias_spec(per=1, sub=0):
    return pl.BlockSpec((None, LANES // HEAD_DIM, QB, KW),
                        lambda p, b: (jnp.minimum(per * b + sub, BIAS_VARIANTS - 1), p, 0, 0))


def _head_masks():
    lane = lax.broadcasted_iota(jnp.int32, (1, LANES), 1)
    return [(lane // HEAD_DIM == hh) for hh in range(LANES // HEAD_DIM)]


def _attn_fwd(q, k, v, bias):
    s, d = q.shape
    per = ATTN_PER
    npair, nb = d // LANES, s // (per * QB)
    hpp = LANES // HEAD_DIM
    nwin = 2 + per
    nbias = min(per, BIAS_VARIANTS)

    def body(*refs):
        q_ref, k_refs, v_refs = refs[0], refs[1:1 + nwin], refs[1 + nwin:1 + 2 * nwin]
        bias_refs = refs[1 + 2 * nwin:1 + 2 * nwin + nbias]
        o_ref, lse_ref = refs[1 + 2 * nwin + nbias:]
        ks = [r[...] for r in k_refs]
        vs = [r[...] for r in v_refs]
        masks = _head_masks()
        for sub in range(per):
            rows = slice(sub * QB, (sub + 1) * QB)
            qv = q_ref[rows, :]
            kwin = jnp.concatenate(ks[sub:sub + 3], axis=0)
            vwin = jnp.concatenate(vs[sub:sub + 3], axis=0)
            o = jnp.zeros((QB, LANES), F32)
            lse = jnp.zeros((QB, LANES), F32)
            scs = [lax.dot_general(jnp.where(masks[hh], qv, jnp.zeros_like(qv)), kwin, NT,
                                   preferred_element_type=F32) + bias_refs[min(sub, nbias - 1)][hh]
                   for hh in range(hpp)]
            for hh in range(hpp):
                vm = jnp.where(masks[hh], vwin, jnp.zeros_like(vwin))
                sc = scs[hh]
                m = jnp.max(sc, axis=-1, keepdims=True)
                p = jnp.exp(sc - m)
                l = jnp.sum(p, axis=-1, keepdims=True)
                o = o + jnp.dot(p.astype(BF16), vm, preferred_element_type=F32) * (1.0 / l)
                lse = jnp.where(masks[hh], m + jnp.log(l), lse)
            o_ref[rows, :] = o.astype(BF16)
            lse_ref[rows, :] = lse

    blk = pl.BlockSpec((per * QB, LANES), lambda p, b: (b, p))
    return pl.pallas_call(
        body, name="attn_fwd", grid=(npair, nb),
        in_specs=[blk] + _window_specs(per) + _window_specs(per) + [_bias_spec(per, sub) for sub in range(nbias)],
        out_specs=(blk, blk),
        out_shape=(jax.ShapeDtypeStruct((s, d), BF16), jax.ShapeDtypeStruct((s, d), F32)),
        compiler_params=_params(VMEM_BIG),
    )(q, *([k] * nwin), *([v] * nwin), *([bias] * nbias))


def _attn_out_fwd(o, x, tab, wo, tm):
    s, d = x.shape
    base = 6

    def body(o_ref, x_ref, t_ref, wo_ref, x3_ref, y_ref):
        yv = jnp.dot(o_ref[...], wo_ref[...], preferred_element_type=F32)
        y_ref[...] = yv
        x3_ref[...] = x_ref[...] + (yv * _rs(yv)) * t_ref[base + R_P1:base + R_P1 + 1, :]

    return pl.pallas_call(
        body, name="attn_out_fwd", grid=(s // tm,),
        in_specs=[_rows(tm, d), _rows(tm, d), _const(tab.shape), _const(wo.shape)],
        out_specs=(_rows(tm, d), _rows(tm, d)),
        out_shape=(jax.ShapeDtypeStruct((s, d), F32), jax.ShapeDtypeStruct((s, d), F32)),
        compiler_params=_params(VMEM_BIG),
    )(o, x, tab, wo)


def _ffn_bwd(dxo, x, y, gu, tab, base, wfi, wfo, tm, name):
    s, d = x.shape
    hid = wfo.shape[0]
    nblk = hid // MXU

    def body(dxo_ref, x_ref, y_ref, gu_ref, t_ref, wfi_ref, wfo_ref, dx_ref, dyb_ref, dgu_ref, a_ref, sums_ref):
        @pl.when(pl.program_id(0) == 0)
        def _():
            sums_ref[...] = jnp.zeros_like(sums_ref)

        dxo_v = dxo_ref[...]
        yv = y_ref[...]
        ry = _rs(yv)
        ny = yv * ry
        sums_ref[R_P2:R_P2 + 1, :] += _colsum(dxo_v * ny)
        dyb = _norm_bwd(dxo_v * t_ref[base + R_P2:base + R_P2 + 1, :], ny, ry).astype(BF16)
        dyb_ref[...] = dyb
        dh = jnp.zeros((tm, d), F32)
        for c0 in range(0, nblk, FFN_CHUNK):
            for j in range(c0, min(c0 + FFN_CHUNK, nblk)):
                lo, hi = j * MXU, (j + 1) * MXU
                da = lax.dot_general(dyb, wfo_ref[lo:hi, :], NT, preferred_element_type=F32)
                g, u = gu_ref[:, lo:hi].astype(F32), gu_ref[:, hid + lo:hid + hi].astype(F32)
                sg = _sigmoid(g)
                gs = g * sg
                a_ref[:, lo:hi] = (gs * u).astype(BF16)
                dgu_ref[:, lo:hi] = (da * u * sg * (1.0 + g * (1.0 - sg))).astype(BF16)
                dgu_ref[:, hid + lo:hid + hi] = (da * gs).astype(BF16)
            lo, hi = c0 * MXU, min(c0 + FFN_CHUNK, nblk) * MXU
            dh = dh + jnp.dot(dgu_ref[:, lo:hi], wfi_ref[lo:hi, :], preferred_element_type=F32)
            dh = dh + jnp.dot(dgu_ref[:, hid + lo:hid + hi], wfi_ref[hid + lo:hid + hi, :],
                              preferred_element_type=F32)
        xv = x_ref[...]
        r = _rs(xv)
        n = xv * r
        sums_ref[R_SH2:R_SH2 + 1, :] += _colsum(dh)
        sums_ref[R_W2:R_W2 + 1, :] += _colsum(dh * n)
        dx_ref[...] = dxo_v + _norm_bwd(dh * t_ref[base + R_W2:base + R_W2 + 1, :], n, r)

    return pl.pallas_call(
        body, name=name, grid=(s // tm,),
        in_specs=[_rows(tm, d), _rows(tm, d), _rows(tm, d), _rows(tm, 2 * hid),
                  _const(tab.shape), _const(wfi.shape), _const(wfo.shape)],
        out_specs=(_rows(tm, d), _rows(tm, d), _rows(tm, 2 * hid), _rows(tm, hid),
                   pl.BlockSpec((8, d), lambda i: (0, 0))),
        out_shape=(jax.ShapeDtypeStruct((s, d), F32), jax.ShapeDtypeStruct((s, d), BF16),
                   jax.ShapeDtypeStruct((s, 2 * hid), BF16), jax.ShapeDtypeStruct((s, hid), BF16),
                   jax.ShapeDtypeStruct((8, d), F32)),
        compiler_params=_params(VMEM_BIG),
    )(dxo, x, y, gu, tab, wfi, wfo)


def _attn_out_bwd(dx, y, tab, wo, tm):
    s, d = y.shape
    base = 6

    def body(dx_ref, y_ref, t_ref, wo_ref, dyb_ref, do_ref, sums_ref):
        @pl.when(pl.program_id(0) == 0)
        def _():
            sums_ref[...] = jnp.zeros_like(sums_ref)

        dxv = dx_ref[...]
        yv = y_ref[...]
        ry = _rs(yv)
        ny = yv * ry
        sums_ref[R_P1:R_P1 + 1, :] += _colsum(dxv * ny)
        dyb = _norm_bwd(dxv * t_ref[base + R_P1:base + R_P1 + 1, :], ny, ry).astype(BF16)
        dyb_ref[...] = dyb
        do_ref[...] = lax.dot_general(dyb, wo_ref[...], NT, preferred_element_type=F32).astype(BF16)

    return pl.pallas_call(
        body, name="attn_out_bwd", grid=(s // tm,),
        in_specs=[_rows(tm, d), _rows(tm, d), _const(tab.shape), _const(wo.shape)],
        out_specs=(_rows(tm, d), _rows(tm, d), pl.BlockSpec((8, d), lambda i: (0, 0))),
        out_shape=(jax.ShapeDtypeStruct((s, d), BF16), jax.ShapeDtypeStruct((s, d), BF16),
                   jax.ShapeDtypeStruct((8, d), F32)),
        compiler_params=_params(VMEM_BIG),
    )(dx, y, tab, wo)


def _attn_bwd(q, k, v, o, do, lse, bias):
    s, d = q.shape
    per = ATTN_PER
    npair, nb = d // LANES, s // (per * QB)
    hpp = LANES // HEAD_DIM
    nwin = 2 + per
    nbias = min(per, BIAS_VARIANTS)

    def body(*refs):
        q_ref, k_refs, v_refs = refs[0], refs[1:1 + nwin], refs[1 + nwin:1 + 2 * nwin]
        o_ref, do_ref, lse_ref = refs[1 + 2 * nwin:4 + 2 * nwin]
        bias_refs = refs[4 + 2 * nwin:4 + 2 * nwin + nbias]
        dq_ref, dk_ref, dv_ref, db_ref = refs[4 + 2 * nwin + nbias:]
        b = pl.program_id(1)

        @pl.when(b == 0)
        def _():
            dk_ref[...] = jnp.zeros_like(dk_ref)
            dv_ref[...] = jnp.zeros_like(dv_ref)
            db_ref[...] = jnp.zeros_like(db_ref)

        ks = [r[...] for r in k_refs]
        vs = [r[...] for r in v_refs]
        masks = _head_masks()
        for sub in range(per):
            rows = slice(sub * QB, (sub + 1) * QB)
            qv = q_ref[rows, :]
            dov = do_ref[rows, :]
            lsev = lse_ref[rows, :]
            doo = dov.astype(F32) * o_ref[rows, :].astype(F32)
            kwin = jnp.concatenate(ks[sub:sub + 3], axis=0)
            vwin = jnp.concatenate(vs[sub:sub + 3], axis=0)
            dq = jnp.zeros((QB, LANES), F32)
            dkw = jnp.zeros((KW, LANES), F32)
            dvw = jnp.zeros((KW, LANES), F32)
            for hh in range(hpp):
                qm = jnp.where(masks[hh], qv, jnp.zeros_like(qv))
                dom = jnp.where(masks[hh], dov, jnp.zeros_like(dov))
                km = jnp.where(masks[hh], kwin, jnp.zeros_like(kwin))
                lse_h = jnp.max(jnp.where(masks[hh], lsev, NEG), axis=-1, keepdims=True)
                delta = jnp.sum(jnp.where(masks[hh], doo, 0.0), axis=-1, keepdims=True)
                sc = lax.dot_general(qm, kwin, NT, preferred_element_type=F32) + bias_refs[min(sub, nbias - 1)][hh]
                p = jnp.exp(sc - lse_h)
                dp = lax.dot_general(dom, vwin, NT, preferred_element_type=F32)
                ds = p * (dp - delta)
                db_ref[hh] += ds
                dsb = ds.astype(BF16)
                dq = dq + jnp.dot(dsb, km, preferred_element_type=F32)
                dkw = dkw + lax.dot_general(dsb, qm, TN, preferred_element_type=F32)
                dvw = dvw + lax.dot_general(p.astype(BF16), dom, TN, preferred_element_type=F32)
            dq_ref[rows, :] = (dq * (HEAD_DIM ** -0.5)).astype(BF16)
            for w in range(3):
                start = pl.multiple_of(jnp.maximum(per * b + sub - 2 + w, 0) * QB, QB)
                dk_ref[pl.ds(start, QB), :] += dkw[w * QB:(w + 1) * QB, :]
                dv_ref[pl.ds(start, QB), :] += dvw[w * QB:(w + 1) * QB, :]

    blk = pl.BlockSpec((per * QB, LANES), lambda p, b: (b, p))
    col = pl.BlockSpec((s, LANES), lambda p, b: (0, p))
    pair = pl.BlockSpec((hpp, QB, KW), lambda p, b: (p, 0, 0))
    return pl.pallas_call(
        body, name="attn_bwd", grid=(npair, nb),
        in_specs=[blk] + _window_specs(per) + _window_specs(per) + [blk, blk, blk]
                 + [_bias_spec(per, sub) for sub in range(nbias)],
        out_specs=(blk, col, col, pair),
        out_shape=(jax.ShapeDtypeStruct((s, d), BF16), jax.ShapeDtypeStruct((s, d), F32),
                   jax.ShapeDtypeStruct((s, d), F32), jax.ShapeDtypeStruct(bias.shape[1:], F32)),
        compiler_params=_params(VMEM_BIG),
    )(q, *([k] * nwin), *([v] * nwin), o, do, lse, *([bias] * nbias))


def _qkv_bwd(dres, dq, dk, dv, x, tab, wq, wkv, tm):
    s, d = x.shape
    base = 6

    def body(dres_ref, dq_ref, dk_ref, dv_ref, x_ref, t_ref, wq_ref, wkv_ref, dx_ref, dkv_ref, sums_ref):
        @pl.when(pl.program_id(0) == 0)
        def _():
            sums_ref[...] = jnp.zeros_like(sums_ref)

        dh1 = lax.dot_general(dq_ref[...], wq_ref[...], NT, preferred_element_type=F32)
        dkv_ref[:, 0:d] = dk_ref[...].astype(BF16)
        dkv_ref[:, d:2 * d] = dv_ref[...].astype(BF16)
        dhkv = jnp.dot(dkv_ref[...], wkv_ref[...], preferred_element_type=F32)
        xv = x_ref[...]
        r = _rs(xv)
        n = xv * r
        sums_ref[0:1, :] += _colsum(dh1 * n)
        sums_ref[1:2, :] += _colsum(dh1)
        sums_ref[2:3, :] += _colsum(dhkv * n)
        sums_ref[3:4, :] += _colsum(dhkv)
        dn = dh1 * t_ref[base + R_W1:base + R_W1 + 1, :] + dhkv * t_ref[R_KV:R_KV + 1, :]
        dx_ref[...] = dres_ref[...] + _norm_bwd(dn, n, r)

    return pl.pallas_call(
        body, name="qkv_bwd", grid=(s // tm,),
        in_specs=[_rows(tm, d)] * 5 + [_const(tab.shape), _const(wq.shape), _const(wkv.shape)],
        out_specs=(_rows(tm, d), _rows(tm, 2 * d), pl.BlockSpec((8, d), lambda i: (0, 0))),
        out_shape=(jax.ShapeDtypeStruct((s, d), F32), jax.ShapeDtypeStruct((s, 2 * d), BF16),
                   jax.ShapeDtypeStruct((8, d), F32)),
        compiler_params=_params(VMEM_BIG),
    )(dres, dq, dk, dv, x, tab, wq, wkv)


def _conv_bwd(dx1, x, y, bcx, tab, ck, wci, wco, tm):
    s, d = x.shape
    nt = s // tm

    def rev(i):
        return (nt - 1 - i, 0)

    def halo(i):
        return (jnp.maximum((nt - 1 - i) * (tm // 8) - 1, 0), 0)

    def body(dx_ref, x_ref, y_ref, bcx_ref, halo_ref, t_ref, ck_ref, wci_ref, wco_ref,
             dx0_ref, dyb_ref, dbcx_ref, sums_ref, dck_ref, carry):
        i = pl.program_id(0)

        @pl.when(i == 0)
        def _():
            sums_ref[...] = jnp.zeros_like(sums_ref)
            dck_ref[...] = jnp.zeros_like(dck_ref)
            carry[...] = jnp.zeros_like(carry)

        dxv = dx_ref[...]
        yv = y_ref[...]
        ry = _rs(yv)
        ny = yv * ry
        sums_ref[R_P1:R_P1 + 1, :] += _colsum(dxv * ny)
        dyb = _norm_bwd(dxv * t_ref[R_P1:R_P1 + 1, :], ny, ry).astype(BF16)
        dyb_ref[...] = dyb
        du = lax.dot_general(dyb, wco_ref[...], NT, preferred_element_type=F32)
        bg, cg, xi = bcx_ref[:, 0:d], bcx_ref[:, d:2 * d], bcx_ref[:, 2 * d:3 * d]
        z = cg * xi
        zp = halo_ref[:, d:2 * d] * halo_ref[:, 2 * d:3 * d]
        zp = jnp.where(i == nt - 1, jnp.zeros_like(zp), zp)
        row = lax.broadcasted_iota(jnp.int32, z.shape, 0)
        c1, c2 = zp[7:8, :], zp[6:7, :]
        z1 = jnp.where(row == 0, c1, pltpu.roll(z, 1, 0))
        z2 = jnp.where(row == 0, c2, jnp.where(row == 1, c1, pltpu.roll(z, 2, 0)))
        k0, k1, k2 = ck_ref[0:1, :], ck_ref[1:2, :], ck_ref[2:3, :]
        conv = k0 * z2 + k1 * z1 + k2 * z
        dconv = du * bg
        dck_ref[0:1, :] += _colsum(dconv * z2)
        dck_ref[1:2, :] += _colsum(dconv * z1)
        dck_ref[2:3, :] += _colsum(dconv * z)
        n1, n2 = carry[0:1, :], carry[1:2, :]
        d1 = jnp.where(row == tm - 1, n1, pltpu.roll(dconv, tm - 1, 0))
        d2 = jnp.where(row == tm - 1, n2, jnp.where(row == tm - 2, n1, pltpu.roll(dconv, tm - 2, 0)))
        carry[...] = dconv[0:8, :]
        dz = k2 * dconv + k1 * d1 + k0 * d2
        dbcx_ref[:, 0:d] = (du * conv).astype(BF16)
        dbcx_ref[:, d:2 * d] = (dz * xi).astype(BF16)
        dbcx_ref[:, 2 * d:3 * d] = (dz * cg).astype(BF16)
        dh = jnp.dot(dbcx_ref[...], wci_ref[...], preferred_element_type=F32)
        xv = x_ref[...]
        r = _rs(xv)
        n = xv * r
        sums_ref[R_W1:R_W1 + 1, :] += _colsum(dh * n)
        sums_ref[R_SH1:R_SH1 + 1, :] += _colsum(dh)
        dx0_ref[...] = dxv + _norm_bwd(dh * t_ref[R_W1:R_W1 + 1, :], n, r)

    rrow = lambda cols: pl.BlockSpec((tm, cols), rev)
    acc = pl.BlockSpec((8, d), lambda i: (0, 0))
    return pl.pallas_call(
        body, name="conv_bwd", grid=(nt,),
        in_specs=[rrow(d), rrow(d), rrow(d), rrow(3 * d), pl.BlockSpec((8, 3 * d), halo),
                  _const(tab.shape), _const(ck.shape), _const(wci.shape), _const(wco.shape)],
        out_specs=(rrow(d), rrow(d), rrow(3 * d), acc, acc),
        out_shape=(jax.ShapeDtypeStruct((s, d), F32), jax.ShapeDtypeStruct((s, d), BF16),
                   jax.ShapeDtypeStruct((s, 3 * d), BF16), jax.ShapeDtypeStruct((8, d), F32),
                   jax.ShapeDtypeStruct((8, d), F32)),
        scratch_shapes=[pltpu.VMEM((8, d), F32)],
        compiler_params=_params(VMEM_BIG),
    )(dx1, x, y, bcx, bcx, tab, ck, wci, wco)


def _wgrad_wide(a, b, nblk, tk, name):
    s, m = a.shape
    n = b.shape[1] // nblk
    nk = s // tk

    def body(a_ref, b_ref, o_ref, acc):
        kk = pl.program_id(0)

        @pl.when(kk == 0)
        def _():
            acc[...] = jnp.zeros_like(acc)

        acc[...] += jnp.dot(a_ref[...].T, b_ref[...], preferred_element_type=F32)

        @pl.when(kk == nk - 1)
        def _():
            for j in range(nblk):
                o_ref[j] = acc[:, j * n:(j + 1) * n].astype(BF16)

    return pl.pallas_call(
        body, name=name, grid=(nk,),
        in_specs=[pl.BlockSpec((tk, m), lambda kk: (kk, 0)), pl.BlockSpec((tk, nblk * n), lambda kk: (kk, 0))],
        out_specs=pl.BlockSpec((nblk, m, n), lambda kk: (0, 0, 0)),
        out_shape=jax.ShapeDtypeStruct((nblk, m, n), BF16),
        scratch_shapes=[pltpu.VMEM((m, nblk * n), F32)],
        compiler_params=_params(VMEM_BIG),
    )(a, b)


def _wgrad_rows(a, b, ncb, tk, name):
    s, m = a.shape
    n = b.shape[1]
    mb = m // ncb
    nk = s // tk

    def body(a_ref, b_ref, o_ref, acc):
        kk = pl.program_id(1)

        @pl.when(kk == 0)
        def _():
            acc[...] = jnp.zeros_like(acc)

        acc[...] += jnp.dot(a_ref[...].T, b_ref[...], preferred_element_type=F32)

        @pl.when(kk == nk - 1)
        def _():
            o_ref[...] = acc[...].astype(BF16)

    return pl.pallas_call(
        body, name=name, grid=(ncb, nk),
        in_specs=[pl.BlockSpec((tk, mb), lambda j, kk: (kk, j)), pl.BlockSpec((tk, n), lambda j, kk: (kk, 0))],
        out_specs=pl.BlockSpec((mb, n), lambda j, kk: (j, 0)),
        out_shape=jax.ShapeDtypeStruct((m, n), BF16),
        scratch_shapes=[pltpu.VMEM((mb, n), F32)],
        compiler_params=_params(VMEM_BIG),
    )(a, b)


def _adamw_math(w, g, m, v):
    m = ADAM_B1 * m + (1.0 - ADAM_B1) * g
    v = ADAM_B2 * v + (1.0 - ADAM_B2) * (g * g)
    m_hat = m / (1.0 - ADAM_B1 ** ADAM_STEP)
    v_hat = v / (1.0 - ADAM_B2 ** ADAM_STEP)
    delta = -ADAM_LR * (m_hat / (jnp.sqrt(v_hat) + ADAM_EPS) + ADAM_WD * w)
    return delta, m, v


def _adamw_reduce(parts, w, m, v, tr, name, parts_t=False):
    nl, r, c = w.shape
    tr = r if (parts_t and r % LANES) else (LANES if parts_t else _row_tile(r, tr))

    def body(*refs):
        p_refs = refs[:nl]
        w_ref, m_ref, v_ref, g_ref, d_ref, mo_ref, vo_ref = refs[nl:]
        layer = pl.program_id(0)

        def partial(i):
            val = p_refs[0][i].astype(F32)
            for q in range(1, nl):
                val = jnp.where(layer == q, p_refs[q][i].astype(F32), val)
            return val

        g = partial(0)
        for i in range(1, N_DEV):
            g = g + partial(i)
        if parts_t:
            g = g.T
        g_ref[...] = g
        d_ref[...], mo_ref[...], vo_ref[...] = _adamw_math(w_ref[...], g, m_ref[...], v_ref[...])

    blk = pl.BlockSpec((None, tr, c), lambda l, i: (l, i, 0))
    out = jax.ShapeDtypeStruct((nl, r, c), F32)
    if parts_t:
        p_specs = [pl.BlockSpec((N_DEV, c, tr), lambda l, i: (0, 0, i))]
    else:
        p_specs = [pl.BlockSpec((N_DEV, tr, c), (lambda l, i, q=q: (0, jnp.where(l == q, i, 0), 0)))
                   for q in range(nl)]
    return pl.pallas_call(
        body, name=name, grid=(nl, r // tr),
        in_specs=p_specs + [blk, blk, blk],
        out_specs=(blk,) * 4, out_shape=(out,) * 4,
        compiler_params=_params(VMEM_BIG),
    )(*parts, w, m, v)


def _adamw_outer(sct, dm, w, m, v, tr, name):
    nl, d, c = w.shape

    def body(s_ref, dm_ref, w_ref, m_ref, v_ref, g_ref, d_ref, mo_ref, vo_ref):
        g = jnp.dot(s_ref[...], dm_ref[...], preferred_element_type=F32)
        g_ref[...] = g
        d_ref[...], mo_ref[...], vo_ref[...] = _adamw_math(w_ref[...], g, m_ref[...], v_ref[...])

    blk = pl.BlockSpec((None, tr, c), lambda l, i: (l, i, 0))
    out = jax.ShapeDtypeStruct((nl, d, c), F32)
    return pl.pallas_call(
        body, name=name, grid=(nl, d // tr),
        in_specs=[pl.BlockSpec((tr, N_DEV), lambda l, i: (i, 0)),
                  pl.BlockSpec((None, N_DEV, c), lambda l, i: (l, 0, 0)), blk, blk, blk],
        out_specs=(blk,) * 4, out_shape=(out,) * 4,
        compiler_params=_params(VMEM_BIG),
    )(sct, dm, w, m, v)


def _pad_rows(a, rows):
    return jnp.concatenate([a, jnp.zeros((rows - a.shape[0],) + a.shape[1:], a.dtype)], axis=0)


def kernel(x, c, mod_w, mod_b, norm_g, ffn_w_in, ffn_w_out, conv_w_in, conv_k, conv_w_out, kv_mod_w, kv_mod_b, kv_norm_g, w_kv, attn_w_q, attn_w_o, rel_bias, loss_target, m_mod_w, m_mod_b, m_norm_g, m_ffn_w_in, m_ffn_w_out, m_conv_w_in, m_conv_k, m_conv_w_out, m_kv_mod_w, m_kv_mod_b, m_kv_norm_g, m_w_kv, m_attn_w_q, m_attn_w_o, m_rel_bias, v_mod_w, v_mod_b, v_norm_g, v_ffn_w_in, v_ffn_w_out, v_conv_w_in, v_conv_k, v_conv_w_out, v_kv_mod_w, v_kv_mod_b, v_kv_norm_g, v_w_kv, v_attn_w_q, v_attn_w_o, v_rel_bias):
    s, d = x.shape[1], x.shape[2]
    dq = d // LANES
    dsh = d // N_DEV
    nl = mod_w.shape[0]
    mw = mod_w.shape[2]
    kmw = kv_mod_w.shape[1]
    fw = ffn_w_in.shape[2]
    nh, nrel = rel_bias.shape[1], rel_bias.shape[2]
    tm = min(256, s)
    tm2 = min(512, s)
    tk = min(1024, s)
    me = 4 * lax.axis_index("x") + 2 * lax.axis_index("y") + lax.axis_index("c")

    x0 = x[0]
    tgt = loss_target[0]

    small1 = jnp.concatenate([c.reshape(dq, LANES), norm_g.reshape(dq, LANES),
                              _pad_rows(conv_k[0], 8).reshape(dq, LANES)], axis=0)
    (sm,) = _exchange([small1], ["gather"], "gather_small")
    c_all = sm[:, 0:dq].reshape(N_DEV, d)
    ng_full = jnp.transpose(sm[:, dq:2 * dq].reshape(N_DEV, 8, dsh), (1, 0, 2)).reshape(8, d)
    ck_full = jnp.transpose(sm[:, 2 * dq:3 * dq].reshape(N_DEV, 8, dsh), (1, 0, 2)).reshape(8, d)

    modcols, silu_c = _mod_fwd(c_all, mod_w, kv_mod_w)
    (modall,) = _exchange([modcols], ["gather"], "gather_mod")

    cast = lambda *ws: [a.astype(BF16) for a in ws]
    gath = lambda ws: (ws, ["gather"] * len(ws))
    (h_conv, h_ffn0, h_attn, h_ffn1), token = _xstart(
        [gath(cast(conv_w_in[0].T, conv_w_out[0])), gath(cast(jnp.swapaxes(ffn_w_in[0], 0, 1), ffn_w_out[0])),
         gath(cast(w_kv.T, attn_w_q[0], attn_w_o[0])), gath(cast(jnp.swapaxes(ffn_w_in[1], 0, 1), ffn_w_out[1]))],
        modall, "gather_start")
    modall = modall + token[0, 0]
    mine = lax.dynamic_index_in_dim(modall, me, axis=1, keepdims=False)
    modrow = jnp.stack([mine[:, l * mw:(l + 1) * mw].reshape(6, d) for l in range(nl)])
    kvrow = mine[:, nl * mw:nl * mw + kmw].reshape(2, d)
    tab, modval = _vec_prep(modrow, mod_b.reshape(nl, 6, d), kvrow, kv_mod_b.reshape(2, d), ng_full,
                            kv_norm_g.reshape(1, d))
    bias = _bias_fwd(rel_bias[0])

    wci, wco = _xwait(h_conv, [bias], "gather_wait_conv")
    wci, wco = wci.reshape(3 * d, d), wco.reshape(d, d)
    x1, h1a, bcx, ua, ya = _conv_fwd(x0, tab, ck_full, wci, wco, tm2)
    wfi0, wfo0 = _xwait(h_ffn0, [x1], "gather_wait_ffn0")
    wfi0, wfo0 = wfi0.reshape(-1, d), wfo0.reshape(-1, d)
    x2, h2a, gua, y2a = _ffn_fwd(x1, tab, 0, wfi0, wfo0, None, tm2, "ffn_fwd0")
    wkv, wq, wo = _xwait(h_attn, [x2], "gather_wait_attn")
    wkv, wq, wo = wkv.reshape(2 * d, d), wq.reshape(d, d), wo.reshape(d, d)
    hkv, h1b, q, k, v = _qkv_fwd(x2, tab, wq, wkv, tm2)
    o, lse = _attn_fwd(q, k, v, bias)
    x3, yb = _attn_out_fwd(o, x2, tab, wo, tm2)
    wfi1, wfo1 = _xwait(h_ffn1, [x3], "gather_wait_ffn1")
    wfi1, wfo1 = wfi1.reshape(-1, d), wfo1.reshape(-1, d)
    dx4, h2b, gub, y2b, loss_acc = _ffn_fwd(x3, tab, 6, wfi1, wfo1, tgt, tm2, "ffn_fwd1")

    scat = lambda ws: [(ws, ["scatter"] * len(ws))]
    dx3, dy2b, dgub, ab, sums_f1 = _ffn_bwd(dx4, x3, y2b, gub, tab, 6, wfi1, wfo1, tm, "ffn_bwd1")
    g_wfi1 = _wgrad_rows(dgub, h2b, 4, tk, "wgrad_ffn_in1").reshape(N_DEV, -1, d)
    g_wfo1 = _wgrad_rows(ab, dy2b, 2, tk, "wgrad_ffn_out1").reshape(N_DEV, -1, d)
    (h_g1,), token = _xstart(scat([g_wfi1, g_wfo1]), dx3, "grads_start_ffn1")
    tab = tab + token[0, 0]
    dyb, do, sums_o = _attn_out_bwd(dx3, yb, tab, wo, tm2)
    g_wo = _wgrad_wide(o, dyb, 1, tk, "wgrad_o").reshape(N_DEV, dsh, d)
    dqb, dk, dv, dbias = _attn_bwd(q, k, v, o, do, lse, bias)
    g_wq = _wgrad_wide(h1b, dqb, 1, tk, "wgrad_q").reshape(N_DEV, dsh, d)
    dx2, dkvb, sums_q = _qkv_bwd(dx3, dqb, dk, dv, x2, tab, wq, wkv, tm2)
    g_wkv = _wgrad_rows(dkvb, hkv, 2, tk, "wgrad_kv").reshape(N_DEV, -1, d)
    (h_g2,), token = _xstart(scat([g_wkv, g_wq, g_wo]), dx2, "grads_start_attn")
    tab = tab + token[0, 0]
    dx1, dy2a, dgua, aa, sums_f0 = _ffn_bwd(dx2, x1, y2a, gua, tab, 0, wfi0, wfo0, tm, "ffn_bwd0")
    g_wfi0 = _wgrad_rows(dgua, h2a, 4, tk, "wgrad_ffn_in0").reshape(N_DEV, -1, d)
    g_wfo0 = _wgrad_rows(aa, dy2a, 2, tk, "wgrad_ffn_out0").reshape(N_DEV, -1, d)
    (h_g3,), token = _xstart(scat([g_wfi0, g_wfo0]), dx1, "grads_start_ffn0")
    tab = tab + token[0, 0]
    dx0, dya, dbcx, sums_c, dck = _conv_bwd(dx1, x0, ya, bcx, tab, ck_full, wci, wco, tm2)
    drel = _bias_bwd(dbias, nrel)
    dmod, dng, dkvg = _vec_bwd(sums_c, sums_f0, sums_q, sums_o, sums_f1, modval, ng_full, kv_norm_g.reshape(1, d))

    relw = -(-nrel // LANES) * LANES
    drel_p = jnp.concatenate([drel, jnp.zeros((nh, relw - nrel), F32)], axis=1)
    small3 = jnp.concatenate([dmod.reshape(16 * dq, LANES), dng.reshape(8 * dq, LANES), dkvg.reshape(8 * dq, LANES),
                              dck.reshape(8 * dq, LANES), loss_acc,
                              drel_p.reshape(nh * relw // LANES, LANES)], axis=0)
    (sm,) = _exchange([small3], ["gather"], "gather_small_grads")
    g_wci = _wgrad_rows(dbcx, h1a, 3, tk, "wgrad_conv_in").reshape(N_DEV, -1, d)
    g_wco = _wgrad_wide(ua, dya, 1, tk, "wgrad_conv_out").reshape(N_DEV, dsh, d)
    (h_g4,), token = _xstart(scat([g_wci, g_wco]), sm, "grads_start_conv")
    sm = sm + token[0, 0]
    o1, o2, o3, o4, o5 = 16 * dq, 24 * dq, 32 * dq, 40 * dq, 40 * dq + 8
    loss = jnp.sum(sm[:, o4, 0]) * (0.5 / d)
    dmod_all = sm[:, 0:o1].reshape(N_DEV, 16, d)
    mine_cols = lambda a: lax.dynamic_slice_in_dim(a, me * dsh, dsh, axis=2)
    dng_parts = mine_cols(sm[:, o1:o2].reshape(N_DEV, 8, d))
    dkvg_parts = sm[:, o2:o3].reshape(N_DEV, 8, d)[:, 0:1]
    dck_parts = mine_cols(sm[:, o3:o4].reshape(N_DEV, 8, d))[:, 0:3]
    drel_parts = sm[:, o5:].reshape(N_DEV, nh, relw)[:, :, 0:nrel]

    def update(parts, w, m, v, name, layers=1, parts_t=False):
        shp = w.shape
        w3, m3, v3 = (a.reshape(layers, -1, shp[-1]) for a in (w, m, v))
        if not parts_t:
            parts = [p.reshape(N_DEV, -1, shp[-1]) for p in parts]
        outs = _adamw_reduce(parts, w3, m3, v3, 256, name, parts_t)
        return [a.reshape(shp) for a in outs]

    p_wfi1, p_wfo1 = _xwait(h_g1, [sm], "grads_wait_ffn1")
    p_wfi0, p_wfo0 = _xwait(h_g3, [p_wfi1], "grads_wait_ffn0")
    tr = lambda a: jnp.swapaxes(a, 1, 2)
    u_ffn_in = [tr(a) for a in update([p_wfi0, p_wfi1], tr(ffn_w_in), tr(m_ffn_w_in), tr(v_ffn_w_in),
                                      "adamw_ffn_in", 2)]
    u_ffn_out = update([p_wfo0, p_wfo1], ffn_w_out, m_ffn_w_out, v_ffn_w_out, "adamw_ffn_out", 2)
    p_wkv, p_wq, p_wo = _xwait(h_g2, [u_ffn_out[0]], "grads_wait_attn")
    u_w_kv = update([p_wkv], w_kv, m_w_kv, v_w_kv, "adamw_w_kv", parts_t=True)
    u_w_q = update([p_wq], attn_w_q, m_attn_w_q, v_attn_w_q, "adamw_w_q")
    u_w_o = update([p_wo], attn_w_o, m_attn_w_o, v_attn_w_o, "adamw_w_o")

    sct = jnp.transpose(silu_c)
    dm_mod = jnp.stack([lax.dynamic_slice_in_dim(dmod_all[:, 6 * l:6 * l + 6].reshape(N_DEV, 6 * d), me * mw, mw, axis=1)
                        for l in range(nl)]).astype(BF16)
    dm_kv = lax.dynamic_slice_in_dim(dmod_all[:, R_KV:R_KV + 2].reshape(N_DEV, 2 * d), me * kmw, kmw, axis=1)
    u_mod_w = _adamw_outer(sct, dm_mod, mod_w, m_mod_w, v_mod_w, min(256, d), "adamw_mod_w")
    u_kv_mod_w = [a[0] for a in _adamw_outer(sct, dm_kv.astype(BF16)[None], kv_mod_w[None], m_kv_mod_w[None],
                                             v_kv_mod_w[None], min(256, d), "adamw_kv_mod_w")]

    modb_parts = jnp.stack([dmod_all[:, 6 * l:6 * l + 6].reshape(N_DEV, 6 * d) for l in range(nl)], axis=1)
    u_mod_b = update([modb_parts], mod_b, m_mod_b, v_mod_b, "adamw_mod_b")
    u_norm_g = update([dng_parts], norm_g.reshape(8, dsh), m_norm_g.reshape(8, dsh), v_norm_g.reshape(8, dsh), "adamw_norm_g")
    u_norm_g = [a.reshape(norm_g.shape) for a in u_norm_g]
    u_conv_k = update([dck_parts], conv_k, m_conv_k, v_conv_k, "adamw_conv_k")
    kvb_parts = dmod_all[:, R_KV:R_KV + 2].reshape(N_DEV, 1, 2 * d)
    u_kv_mod_b = [a.reshape(kv_mod_b.shape) for a in update([kvb_parts], kv_mod_b.reshape(1, -1), m_kv_mod_b.reshape(1, -1),
                                                            v_kv_mod_b.reshape(1, -1), "adamw_kv_mod_b")]
    u_kv_norm_g = [a.reshape(kv_norm_g.shape) for a in update([dkvg_parts], kv_norm_g.reshape(1, -1), m_kv_norm_g.reshape(1, -1),
                                                              v_kv_norm_g.reshape(1, -1), "adamw_kv_norm_g")]
    u_rel = update([drel_parts], rel_bias, m_rel_bias, v_rel_bias, "adamw_rel_bias")

    others = [u_ffn_in, u_ffn_out, u_w_kv, u_w_q, u_w_o, u_mod_w, u_kv_mod_w, u_mod_b, u_norm_g, u_conv_k, u_kv_mod_b,
              u_kv_norm_g, u_rel]
    p_wci, p_wco = _xwait(h_g4, [u[3] for u in others], "grads_wait_conv")
    u_conv_in = update([p_wci], conv_w_in, m_conv_w_in, v_conv_w_in, "adamw_conv_in", parts_t=True)
    u_conv_out = update([p_wco], conv_w_out, m_conv_w_out, v_conv_w_out, "adamw_conv_out")

    ups = [u_mod_w, u_mod_b, u_norm_g, u_ffn_in, u_ffn_out, u_conv_in, u_conv_k, u_conv_out, u_kv_mod_w, u_kv_mod_b,
           u_kv_norm_g, u_w_kv, u_w_q, u_w_o, u_rel]
    return (loss, dx0[None], *[u[0] for u in ups], *[u[1] for u in ups], *[u[2] for u in ups], *[u[3] for u in ups])
```

```python
import jax
import jax.numpy as jnp
from jax import lax
from jax.experimental import pallas as pl
from jax.experimental.pallas import tpu as pltpu

F32 = jnp.float32
BF16 = jnp.bfloat16

EPS = 1e-6
CHUNK = 64
HEAD_DIM = 64
N_LEFT = 8
LANES = 128
MXU = 256
FFN_CHUNK = 4
QB = 4 * CHUNK
KW = QB + N_LEFT * CHUNK
BIAS_VARIANTS = N_LEFT * CHUNK // QB + 1
ATTN_PER = 8
NEG = -1e30
N_DEV = 8

ADAM_LR = 0.001
ADAM_B1 = 0.9
ADAM_B2 = 0.999
ADAM_EPS = 1e-08
ADAM_WD = 0.01
ADAM_STEP = 10

VMEM_BIG = 56 * 1024 * 1024

NT = (((1,), (1,)), ((), ()))
TN = (((0,), (0,)), ((), ()))

R_W1, R_SH1, R_P1, R_W2, R_SH2, R_P2 = range(6)
R_KV = 12


def _params(vmem):
    return pltpu.CompilerParams(vmem_limit_bytes=vmem)


def _row_tile(rows, cap):
    for t in range(min(cap, rows) // 16 * 16, 0, -16):
        if rows % t == 0:
            return t
    return rows


def _rows(tm, cols):
    return pl.BlockSpec((tm, cols), lambda i: (i, 0))


def _const(shape):
    nd = len(shape)
    return pl.BlockSpec(shape, lambda *_: (0,) * nd, pipeline_mode=pl.Buffered(1))


def _rs(x):
    return lax.rsqrt(jnp.mean(x * x, axis=-1, keepdims=True) + EPS)


def _norm_bwd(d, n, r):
    return r * (d - n * jnp.mean(d * n, axis=-1, keepdims=True))


def _colsum(a):
    return jnp.sum(a, axis=0, keepdims=True)


def _sigmoid(g):
    return 1.0 / (1.0 + jnp.exp(-g))


def _exchange(arrays, modes, name):
    n = len(arrays)
    out_shape = []
    for a, mode in zip(arrays, modes):
        shp = (N_DEV,) + a.shape if mode == "gather" else a.shape
        out_shape.append(jax.ShapeDtypeStruct(shp, a.dtype))

    def body(*refs):
        ins, outs = refs[:n], refs[n:2 * n]
        send_sems, recv_sems, local_sems = refs[2 * n:]
        x, y, c = lax.axis_index("x"), lax.axis_index("y"), lax.axis_index("c")
        me = 4 * x + 2 * y + c
        local, sends, recvs = [], [], []
        for a in range(n):
            own = ins[a] if modes[a] == "gather" else ins[a].at[me]
            cp = pltpu.make_async_copy(own, outs[a].at[me], local_sems.at[a])
            cp.start()
            local.append(cp)
        for k in range(1, N_DEV):
            px = 1 - x if k & 4 else x
            py = 1 - y if k & 2 else y
            pc = 1 - c if k & 1 else c
            peer = 4 * px + 2 * py + pc
            for a in range(n):
                src = ins[a] if modes[a] == "gather" else ins[a].at[peer]
                sem = a * (N_DEV - 1) + k - 1
                cp = pltpu.make_async_remote_copy(
                    src_ref=src, dst_ref=outs[a].at[me],
                    send_sem=send_sems.at[sem], recv_sem=recv_sems.at[sem],
                    device_id=(px, py, pc), device_id_type=pl.DeviceIdType.MESH)
                cp.start()
                sends.append(cp)
                recvs.append(pltpu.make_async_remote_copy(
                    src_ref=src, dst_ref=outs[a].at[peer],
                    send_sem=send_sems.at[sem], recv_sem=recv_sems.at[sem],
                    device_id=(px, py, pc), device_id_type=pl.DeviceIdType.MESH))
        for cp in recvs:
            cp.wait_recv()
        for cp in sends:
            cp.wait_send()
        for cp in local:
            cp.wait()

    any_spec = pl.BlockSpec(memory_space=pl.ANY)
    return pl.pallas_call(
        body, name=name,
        out_shape=tuple(out_shape),
        in_specs=[any_spec] * n,
        out_specs=tuple([any_spec] * n),
        scratch_shapes=[
            pltpu.SemaphoreType.DMA((n * (N_DEV - 1),)),
            pltpu.SemaphoreType.DMA((n * (N_DEV - 1),)),
            pltpu.SemaphoreType.DMA((n,)),
        ],
    )(*arrays)


def _peers(x, y, c):
    out = []
    for k in range(1, N_DEV):
        px = 1 - x if k & 4 else x
        py = 1 - y if k & 2 else y
        pc = 1 - c if k & 1 else c
        out.append((k - 1, (px, py, pc), 4 * px + 2 * py + pc))
    return out


_OTHER_CORE_SLOTS = (2, 4, 6)


def _land_shape(a, mode):
    return a.shape if mode == "scatter" else (N_DEV,) + a.shape


_HBM = pl.BlockSpec(memory_space=pltpu.HBM)
_SEM = pl.BlockSpec(memory_space=pltpu.SEMAPHORE)
_EFFECT = pltpu.SideEffectType.DATAFLOW_SIDE_EFFECTING


def _xstart(groups, after, name):
    flat = [(a, m) for arrays, modes in groups for a, m in zip(arrays, modes)]
    n, ngr = len(flat), len(groups)
    sizes = [len(arrays) for arrays, _ in groups]
    npeer = N_DEV - 1

    def body(*refs):
        ins, lands = refs[:n], refs[n:2 * n]
        outs = refs[2 * n + 1:]
        sems = outs[:2 * ngr]
        token = outs[2 * ngr + 2 * n]
        local_sems = outs[2 * ngr + 2 * n + 1]
        stage = outs[2 * ngr + 2 * n + 2:]
        x, y, c = lax.axis_index("x"), lax.axis_index("y"), lax.axis_index("c")
        me = 4 * x + 2 * y + c
        loads, stores = [], []
        for a in range(n):
            own = ins[a].at[me] if flat[a][1] == "scatter" else ins[a]
            loads.append(pltpu.make_async_copy(own, stage[a], local_sems.at[a]))
            stores.append(pltpu.make_async_copy(stage[a], lands[a].at[me], local_sems.at[a]))
            loads[a].start()
        for a in range(n):
            loads[a].wait()
            stores[a].start()
        a = 0
        for g in range(ngr):
            for j in range(sizes[g]):
                mode = flat[a][1]
                for slot, peer, pidx in _peers(x, y, c):
                    if mode == "gather_half" and slot in _OTHER_CORE_SLOTS:
                        continue
                    pltpu.make_async_remote_copy(
                        src_ref=ins[a].at[pidx] if mode == "scatter" else ins[a], dst_ref=lands[a].at[me],
                        send_sem=sems[2 * g].at[j * npeer + slot], recv_sem=sems[2 * g + 1].at[j * npeer + slot],
                        device_id=peer, device_id_type=pl.DeviceIdType.MESH).start()
                a += 1
        for cp in stores:
            cp.wait()
        token[...] = jnp.zeros_like(token)

    out_shape, out_specs = [], []
    for sz in sizes:
        out_shape += [pltpu.SemaphoreType.DMA((sz * npeer,)), pltpu.SemaphoreType.DMA((sz * npeer,))]
        out_specs += [_SEM, _SEM]
    out_shape += [pltpu.HBM(a.shape, a.dtype) for a, _ in flat]
    out_shape += [pltpu.HBM(_land_shape(a, m), a.dtype) for a, m in flat]
    out_specs += [_HBM] * (2 * n)
    out_shape.append(jax.ShapeDtypeStruct((8, LANES), F32))
    out_specs.append(pl.BlockSpec(memory_space=pltpu.VMEM))
    args = [pltpu.with_memory_space_constraint(a, pltpu.HBM) for a, _ in flat]
    args += [pltpu.with_memory_space_constraint(lax.empty(_land_shape(a, m), a.dtype), pltpu.HBM) for a, m in flat]
    res = pl.pallas_call(
        body, name=name, out_shape=tuple(out_shape),
        in_specs=[_HBM] * (2 * n) + [pl.BlockSpec(memory_space=pl.ANY)], out_specs=tuple(out_specs),
        input_output_aliases={i: 2 * ngr + i for i in range(2 * n)},
        scratch_shapes=[pltpu.SemaphoreType.DMA((n,))]
                       + [pltpu.VMEM(a.shape[1:] if m == "scatter" else a.shape, a.dtype) for a, m in flat],
        compiler_params=pltpu.CompilerParams(has_side_effects=_EFFECT, vmem_limit_bytes=VMEM_BIG),
    )(*args, after)
    handles, a = [], 0
    for g, sz in enumerate(sizes):
        handles.append((res[2 * g], res[2 * g + 1], list(res[2 * ngr + a:2 * ngr + a + sz]),
                        list(res[2 * ngr + n + a:2 * ngr + n + a + sz]), list(groups[g][1])))
        a += sz
    return handles, res[-1]


def _xwait(handle, after, name):
    send_sems, recv_sems, srcs, lands, modes = handle
    m = len(srcs)
    npeer = N_DEV - 1
    after = list(after)

    def body(*refs):
        ins, lnd = refs[:m], refs[m:2 * m]
        ssem, rsem = refs[2 * m], refs[2 * m + 1]
        x, y, c = lax.axis_index("x"), lax.axis_index("y"), lax.axis_index("c")
        for j in range(m):
            for slot, peer, pidx in _peers(x, y, c):
                if modes[j] == "gather_half" and slot in _OTHER_CORE_SLOTS:
                    continue
                cp = pltpu.make_async_remote_copy(
                    src_ref=ins[j].at[pidx] if modes[j] == "scatter" else ins[j], dst_ref=lnd[j].at[pidx],
                    send_sem=ssem.at[j * npeer + slot], recv_sem=rsem.at[j * npeer + slot],
                    device_id=peer, device_id_type=pl.DeviceIdType.MESH)
                cp.wait_send()
                cp.wait_recv()

    res = pl.pallas_call(
        body, name=name,
        out_shape=tuple([pltpu.HBM(a.shape, a.dtype) for a in srcs] + [pltpu.HBM(a.shape, a.dtype) for a in lands]),
        in_specs=[_HBM] * (2 * m) + [_SEM, _SEM] + [pl.BlockSpec(memory_space=pl.ANY)] * len(after),
        out_specs=tuple([_HBM] * (2 * m)),
        input_output_aliases={i: i for i in range(2 * m)},
        compiler_params=pltpu.CompilerParams(has_side_effects=_EFFECT),
    )(*srcs, *lands, send_sems, recv_sems, *after)
    return list(res[m:])


def _forward_to_sibling(lands, name):
    n = len(lands)

    def body(*refs):
        outs = refs[n:2 * n]
        send_sems, recv_sems = refs[2 * n:]
        x, y, c = lax.axis_index("x"), lax.axis_index("y"), lax.axis_index("c")
        sibling = (x, y, 1 - c)
        sends, recvs = [], []
        for a in range(n):
            for j, k in enumerate((2, 4, 6)):
                px = 1 - x if k & 4 else x
                py = 1 - y if k & 2 else y
                got = 4 * px + 2 * py + c
                missing = 4 * px + 2 * py + (1 - c)
                sem = a * 3 + j
                cp = pltpu.make_async_remote_copy(
                    src_ref=outs[a].at[got], dst_ref=outs[a].at[got],
                    send_sem=send_sems.at[sem], recv_sem=recv_sems.at[sem],
                    device_id=sibling, device_id_type=pl.DeviceIdType.MESH)
                cp.start()
                sends.append(cp)
                recvs.append(pltpu.make_async_remote_copy(
                    src_ref=outs[a].at[missing], dst_ref=outs[a].at[missing],
                    send_sem=send_sems.at[sem], recv_sem=recv_sems.at[sem],
                    device_id=sibling, device_id_type=pl.DeviceIdType.MESH))
        for cp in recvs:
            cp.wait_recv()
        for cp in sends:
            cp.wait_send()

    any_spec = pl.BlockSpec(memory_space=pl.ANY)
    return list(pl.pallas_call(
        body, name=name,
        out_shape=tuple(jax.ShapeDtypeStruct(a.shape, a.dtype) for a in lands),
        in_specs=[any_spec] * n, out_specs=tuple([any_spec] * n),
        input_output_aliases={i: i for i in range(n)},
        scratch_shapes=[pltpu.SemaphoreType.DMA((3 * n,)), pltpu.SemaphoreType.DMA((3 * n,))],
    )(*lands))


def _mod_fwd(c_all, mod_w, kv_mod_w):
    nl, d, mw = mod_w.shape
    kw = kv_mod_w.shape[1]

    def body(c_ref, mw_ref, kw_ref, o_ref, sc_ref):
        cc = c_ref[...]
        sc = (cc * _sigmoid(cc)).astype(BF16)
        sc_ref[...] = sc
        for l in range(nl):
            o_ref[:, l * mw:(l + 1) * mw] = jnp.dot(sc, mw_ref[l].astype(BF16), preferred_element_type=F32)
        o_ref[:, nl * mw:nl * mw + kw] = jnp.dot(sc, kw_ref[...].astype(BF16), preferred_element_type=F32)

    return pl.pallas_call(
        body, name="mod_fwd",
        out_shape=(jax.ShapeDtypeStruct((c_all.shape[0], nl * mw + kw), F32),
                   jax.ShapeDtypeStruct(c_all.shape, BF16)),
        compiler_params=_params(VMEM_BIG),
    )(c_all, mod_w, kv_mod_w)


def _vec_prep(modrow, modb, kvrow, kvb, ng, kvg):
    d = ng.shape[1]

    def body(mr_ref, mb_ref, kr_ref, kb_ref, ng_ref, kvg_ref, t_ref, m_ref):
        t_ref[...] = jnp.zeros_like(t_ref)
        m_ref[...] = jnp.zeros_like(m_ref)
        for l in range(2):
            mod = mr_ref[l] + mb_ref[l]
            m_ref[6 * l:6 * l + 6, :] = mod
            g = ng_ref[4 * l:4 * l + 4, :]
            t_ref[6 * l + R_W1:6 * l + R_W1 + 1, :] = g[0:1] * (1.0 + mod[1:2])
            t_ref[6 * l + R_SH1:6 * l + R_SH1 + 1, :] = mod[0:1]
            t_ref[6 * l + R_P1:6 * l + R_P1 + 1, :] = mod[2:3] * g[1:2]
            t_ref[6 * l + R_W2:6 * l + R_W2 + 1, :] = g[2:3] * (1.0 + mod[4:5])
            t_ref[6 * l + R_SH2:6 * l + R_SH2 + 1, :] = mod[3:4]
            t_ref[6 * l + R_P2:6 * l + R_P2 + 1, :] = mod[5:6] * g[3:4]
        kv = kr_ref[...] + kb_ref[...]
        m_ref[R_KV:R_KV + 2, :] = kv
        t_ref[R_KV:R_KV + 1, :] = kvg_ref[...] * (1.0 + kv[1:2])
        t_ref[R_KV + 1:R_KV + 2, :] = kv[0:1]

    return pl.pallas_call(
        body, name="vec_prep",
        out_shape=(jax.ShapeDtypeStruct((16, d), F32), jax.ShapeDtypeStruct((16, d), F32)),
    )(modrow, modb, kvrow, kvb, ng, kvg)


def _vec_bwd(sums_c, sums_f0, sums_q, sums_o, sums_f1, mt, ng, kvg):
    d = ng.shape[1]

    def body(sc_ref, sf0_ref, sq_ref, so_ref, sf1_ref, m_ref, ng_ref, kvg_ref, dm_ref, dng_ref, dkvg_ref, g_ref):
        g_ref[...] = jnp.zeros_like(g_ref)
        g_ref[0:3, :] = sc_ref[0:3, :]
        g_ref[3:6, :] = sf0_ref[3:6, :]
        g_ref[6:8, :] = sq_ref[0:2, :]
        g_ref[8:9, :] = so_ref[2:3, :]
        g_ref[9:12, :] = sf1_ref[3:6, :]
        g_ref[R_KV:R_KV + 2, :] = sq_ref[2:4, :]
        dm_ref[...] = jnp.zeros_like(dm_ref)
        dkvg_ref[...] = jnp.zeros_like(dkvg_ref)
        for l in range(2):
            g = ng_ref[4 * l:4 * l + 4, :]
            mod = m_ref[6 * l:6 * l + 6, :]
            s = g_ref[6 * l:6 * l + 6, :]
            dm_ref[6 * l + 0:6 * l + 1, :] = s[1:2]
            dm_ref[6 * l + 1:6 * l + 2, :] = s[0:1] * g[0:1]
            dm_ref[6 * l + 2:6 * l + 3, :] = s[2:3] * g[1:2]
            dm_ref[6 * l + 3:6 * l + 4, :] = s[4:5]
            dm_ref[6 * l + 4:6 * l + 5, :] = s[3:4] * g[2:3]
            dm_ref[6 * l + 5:6 * l + 6, :] = s[5:6] * g[3:4]
            dng_ref[4 * l + 0:4 * l + 1, :] = s[0:1] * (1.0 + mod[1:2])
            dng_ref[4 * l + 1:4 * l + 2, :] = s[2:3] * mod[2:3]
            dng_ref[4 * l + 2:4 * l + 3, :] = s[3:4] * (1.0 + mod[4:5])
            dng_ref[4 * l + 3:4 * l + 4, :] = s[5:6] * mod[5:6]
        dm_ref[R_KV:R_KV + 1, :] = g_ref[R_KV + 1:R_KV + 2, :]
        dm_ref[R_KV + 1:R_KV + 2, :] = g_ref[R_KV:R_KV + 1, :] * kvg_ref[...]
        dkvg_ref[0:1, :] = g_ref[R_KV:R_KV + 1, :] * (1.0 + m_ref[R_KV + 1:R_KV + 2, :])

    return pl.pallas_call(
        body, name="vec_bwd",
        out_shape=(jax.ShapeDtypeStruct((16, d), F32), jax.ShapeDtypeStruct((8, d), F32),
                   jax.ShapeDtypeStruct((8, d), F32)),
        scratch_shapes=[pltpu.VMEM((16, d), F32)],
    )(sums_c, sums_f0, sums_q, sums_o, sums_f1, mt, ng, kvg)


def _rel_index(nrel):
    width = KW + QB
    e = lax.broadcasted_iota(jnp.int32, (nrel, width), 1)
    r = lax.broadcasted_iota(jnp.int32, (nrel, width), 0)
    max_rel = (nrel - 1) // 2
    idx = jnp.clip(KW - e, -max_rel, max_rel) + max_rel
    return (idx == r).astype(F32)


def _band_valid():
    row = lax.broadcasted_iota(jnp.int32, (QB, KW), 0) // CHUNK
    col = lax.broadcasted_iota(jnp.int32, (QB, KW), 1) // CHUNK
    j = col - row
    return (j >= 0) & (j <= N_LEFT)


def _bias_fwd(rel_bias):
    nh, nrel = rel_bias.shape
    width = KW + QB

    def body(rb_ref, o_ref):
        onehot = _rel_index(nrel)
        gr = jnp.dot(rb_ref[...], onehot, preferred_element_type=F32, precision=lax.Precision.HIGHEST)
        valid = _band_valid() & _key_valid(pl.program_id(0))
        for h in range(nh):
            xrow = jnp.broadcast_to(gr[h:h + 1, :], (QB, width))
            rolled = pltpu.roll(xrow, 0, 1, stride=1, stride_axis=0)
            o_ref[h] = jnp.where(valid, rolled[:, QB:], NEG)

    return pl.pallas_call(
        body, name="bias_fwd", grid=(BIAS_VARIANTS,),
        in_specs=[pl.BlockSpec(rel_bias.shape, lambda v: (0, 0))],
        out_specs=pl.BlockSpec((None, nh, QB, KW), lambda v: (v, 0, 0, 0)),
        out_shape=jax.ShapeDtypeStruct((BIAS_VARIANTS, nh, QB, KW), F32),
        compiler_params=_params(VMEM_BIG),
    )(rel_bias)


def _bias_bwd(dbias, nrel):
    nh = dbias.shape[0]
    width = KW + QB

    def body(db_ref, o_ref, diag_ref):
        onehot = _rel_index(nrel)
        valid = _band_valid()
        rr = lax.broadcasted_iota(jnp.int32, (QB, QB), 0)
        cc = lax.broadcasted_iota(jnp.int32, (QB, QB), 1)
        flip = (rr + cc == QB - 1).astype(F32)
        for h in range(nh):
            rev = jnp.dot(flip, jnp.where(valid, db_ref[h], 0.0), preferred_element_type=F32,
                          precision=lax.Precision.HIGHEST)
            w = jnp.concatenate([jnp.zeros((QB, QB), F32), rev], axis=1)
            back = pltpu.roll(w, width - (QB - 1), 1, stride=1, stride_axis=0)
            diag_ref[h:h + 1, :] = _colsum(back)
        o_ref[...] = lax.dot_general(diag_ref[...], onehot, NT, preferred_element_type=F32,
                                     precision=lax.Precision.HIGHEST)

    return pl.pallas_call(
        body, name="bias_bwd",
        out_shape=jax.ShapeDtypeStruct((nh, nrel), F32),
        scratch_shapes=[pltpu.VMEM((nh, width), F32)],
        compiler_params=_params(VMEM_BIG),
    )(dbias)


def _conv_fwd(x, tab, ck, wci, wco, tm):
    s, d = x.shape

    def body(x_ref, t_ref, ck_ref, wci_ref, wco_ref, x1_ref, h_ref, bcx_ref, u_ref, y_ref, carry):
        @pl.when(pl.program_id(0) == 0)
        def _():
            carry[...] = jnp.zeros_like(carry)

        xv = x_ref[...]
        hb = ((xv * _rs(xv)) * t_ref[R_W1:R_W1 + 1, :] + t_ref[R_SH1:R_SH1 + 1, :]).astype(BF16)
        h_ref[...] = hb
        for j in range(3 * d // MXU):
            bcx_ref[:, j * MXU:(j + 1) * MXU] = lax.dot_general(hb, wci_ref[j * MXU:(j + 1) * MXU, :], NT,
                                                                preferred_element_type=F32)
        bg, cg, xi = bcx_ref[:, 0:d], bcx_ref[:, d:2 * d], bcx_ref[:, 2 * d:3 * d]
        z = cg * xi
        row = lax.broadcasted_iota(jnp.int32, z.shape, 0)
        c1, c2 = carry[7:8, :], carry[6:7, :]
        z1 = jnp.where(row == 0, c1, pltpu.roll(z, 1, 0))
        z2 = jnp.where(row == 0, c2, jnp.where(row == 1, c1, pltpu.roll(z, 2, 0)))
        carry[...] = z[tm - 8:tm, :]
        conv = ck_ref[0:1, :] * z2 + ck_ref[1:2, :] * z1 + ck_ref[2:3, :] * z
        ub = (bg * conv).astype(BF16)
        u_ref[...] = ub
        yv = jnp.dot(ub, wco_ref[...], preferred_element_type=F32)
        y_ref[...] = yv
        x1_ref[...] = xv + (yv * _rs(yv)) * t_ref[R_P1:R_P1 + 1, :]

    return pl.pallas_call(
        body, name="conv_fwd", grid=(s // tm,),
        in_specs=[_rows(tm, d), _const(tab.shape), _const(ck.shape), _const(wci.shape), _const(wco.shape)],
        out_specs=(_rows(tm, d), _rows(tm, d), _rows(tm, 3 * d), _rows(tm, d), _rows(tm, d)),
        out_shape=(jax.ShapeDtypeStruct((s, d), F32), jax.ShapeDtypeStruct((s, d), BF16),
                   jax.ShapeDtypeStruct((s, 3 * d), F32), jax.ShapeDtypeStruct((s, d), BF16),
                   jax.ShapeDtypeStruct((s, d), F32)),
        scratch_shapes=[pltpu.VMEM((8, d), F32)],
        compiler_params=_params(VMEM_BIG),
    )(x, tab, ck, wci, wco)


def _ffn_fwd(x, tab, base, wfi, wfo, tgt, tm, name):
    s, d = x.shape
    hid = wfo.shape[0]
    nblk = hid // MXU
    with_loss = tgt is not None

    def body(*refs):
        if with_loss:
            x_ref, t_ref, wfi_ref, wfo_ref, tgt_ref, xo_ref, h_ref, gu_ref, y_ref, loss_ref, a_scr = refs
        else:
            x_ref, t_ref, wfi_ref, wfo_ref, xo_ref, h_ref, gu_ref, y_ref, a_scr = refs
        xv = x_ref[...]
        hb = ((xv * _rs(xv)) * t_ref[base + R_W2:base + R_W2 + 1, :]
              + t_ref[base + R_SH2:base + R_SH2 + 1, :]).astype(BF16)
        h_ref[...] = hb
        acc = jnp.zeros((tm, d), F32)
        for c0 in range(0, nblk, FFN_CHUNK):
            for j in range(c0, min(c0 + FFN_CHUNK, nblk)):
                lo, hi = j * MXU, (j + 1) * MXU
                g = lax.dot_general(hb, wfi_ref[lo:hi, :], NT, preferred_element_type=F32)
                u = lax.dot_general(hb, wfi_ref[hid + lo:hid + hi, :], NT, preferred_element_type=F32)
                gu_ref[:, lo:hi] = g.astype(BF16)
                gu_ref[:, hid + lo:hid + hi] = u.astype(BF16)
                a_scr[:, lo:hi] = ((g * _sigmoid(g)) * u).astype(BF16)
            lo, hi = c0 * MXU, min(c0 + FFN_CHUNK, nblk) * MXU
            acc = acc + jnp.dot(a_scr[:, lo:hi], wfo_ref[lo:hi, :], preferred_element_type=F32)
        y_ref[...] = acc
        xo = xv + (acc * _rs(acc)) * t_ref[base + R_P2:base + R_P2 + 1, :]
        if with_loss:
            @pl.when(pl.program_id(0) == 0)
            def _():
                loss_ref[...] = jnp.zeros_like(loss_ref)

            err = xo - tgt_ref[...]
            xo_ref[...] = err * (1.0 / d)
            loss_ref[...] += jnp.sum(err * err)
        else:
            xo_ref[...] = xo

    in_specs = [_rows(tm, d), _const(tab.shape), _const(wfi.shape), _const(wfo.shape)]
    args = [x, tab, wfi, wfo]
    out_specs = [_rows(tm, d), _rows(tm, d), _rows(tm, 2 * hid), _rows(tm, d)]
    out_shape = [jax.ShapeDtypeStruct((s, d), F32), jax.ShapeDtypeStruct((s, d), BF16),
                 jax.ShapeDtypeStruct((s, 2 * hid), BF16), jax.ShapeDtypeStruct((s, d), F32)]
    if with_loss:
        in_specs.append(_rows(tm, d))
        args.append(tgt)
        out_specs.append(pl.BlockSpec((8, LANES), lambda i: (0, 0)))
        out_shape.append(jax.ShapeDtypeStruct((8, LANES), F32))
    return pl.pallas_call(
        body, name=name, grid=(s // tm,), in_specs=in_specs, out_specs=tuple(out_specs),
        out_shape=tuple(out_shape), scratch_shapes=[pltpu.VMEM((tm, hid), BF16)],
        compiler_params=_params(VMEM_BIG),
    )(*args)


def _qkv_fwd(x, tab, wq, wkv, tm):
    s, d = x.shape
    base = 6

    def body(x_ref, t_ref, wq_ref, wkv_ref, hkv_ref, h1_ref, q_ref, k_ref, v_ref):
        xv = x_ref[...]
        n = xv * _rs(xv)
        hkv = (n * t_ref[R_KV:R_KV + 1, :] + t_ref[R_KV + 1:R_KV + 2, :]).astype(BF16)
        h1 = (n * t_ref[base + R_W1:base + R_W1 + 1, :] + t_ref[base + R_SH1:base + R_SH1 + 1, :]).astype(BF16)
        hkv_ref[...] = hkv
        h1_ref[...] = h1
        q_ref[...] = (jnp.dot(h1, wq_ref[...], preferred_element_type=F32) * (HEAD_DIM ** -0.5)).astype(BF16)
        for j in range(d // MXU):
            lo, hi = j * MXU, (j + 1) * MXU
            k_ref[:, lo:hi] = lax.dot_general(hkv, wkv_ref[lo:hi, :], NT, preferred_element_type=F32).astype(BF16)
            v_ref[:, lo:hi] = lax.dot_general(hkv, wkv_ref[d + lo:d + hi, :], NT,
                                              preferred_element_type=F32).astype(BF16)

    act = jax.ShapeDtypeStruct((s, d), BF16)
    return pl.pallas_call(
        body, name="qkv_fwd", grid=(s // tm,),
        in_specs=[_rows(tm, d), _const(tab.shape), _const(wq.shape), _const(wkv.shape)],
        out_specs=tuple([_rows(tm, d)] * 5), out_shape=(act,) * 5,
        compiler_params=_params(VMEM_BIG),
    )(x, tab, wq, wkv)


def _window_specs(per=1):
    return [pl.BlockSpec((QB, LANES), (lambda p, b, w=w: (jnp.maximum(per * b - 2 + w, 0), p)))
            for w in range(2 + per)]


def _key_valid(b):
    col = lax.broadcasted_iota(jnp.int32, (QB, KW), 1) // CHUNK
    return (b * (QB // CHUNK) - N_LEFT + col) >= 0


def _bias_spec(per=1, sub=0):
    return pl.BlockSpec((None, LANES // HEAD_DIM, QB, KW),
                        lambda p, b: (jnp.minimum(per * b + sub, BIAS_VARIANTS - 1), p, 0, 0))


def _head_masks():
    lane = lax.broadcasted_iota(jnp.int32, (1, LANES), 1)
    return [(lane // HEAD_DIM == hh) for hh in range(LANES // HEAD_DIM)]


def _attn_fwd(q, k, v, bias):
    s, d = q.shape
    per = ATTN_PER
    npair, nb = d // LANES, s // (per * QB)
    hpp = LANES // HEAD_DIM
    nwin = 2 + per
    nbias = min(per, BIAS_VARIANTS)

    def body(*refs):
        q_ref, k_refs, v_refs = refs[0], refs[1:1 + nwin], refs[1 + nwin:1 + 2 * nwin]
        bias_refs = refs[1 + 2 * nwin:1 + 2 * nwin + nbias]
        o_ref, lse_ref = refs[1 + 2 * nwin + nbias:]
        ks = [r[...] for r in k_refs]
        vs = [r[...] for r in v_refs]
        masks = _head_masks()
        for sub in range(per):
            rows = slice(sub * QB, (sub + 1) * QB)
            qv = q_ref[rows, :]
            kwin = jnp.concatenate(ks[sub:sub + 3], axis=0)
            vwin = jnp.concatenate(vs[sub:sub + 3], axis=0)
            o = jnp.zeros((QB, LANES), F32)
            lse = jnp.zeros((QB, LANES), F32)
            scs = [lax.dot_general(jnp.where(masks[hh], qv, jnp.zeros_like(qv)), kwin, NT,
                                   preferred_element_type=F32) + bias_refs[min(sub, nbias - 1)][hh]
                   for hh in range(hpp)]
            for hh in range(hpp):
                vm = jnp.where(masks[hh], vwin, jnp.zeros_like(vwin))
                sc = scs[hh]
                m = jnp.max(sc, axis=-1, keepdims=True)
                p = jnp.exp(sc - m)
                l = jnp.sum(p, axis=-1, keepdims=True)
                o = o + jnp.dot(p.astype(BF16), vm, preferred_element_type=F32) * (1.0 / l)
                lse = jnp.where(masks[hh], m + jnp.log(l), lse)
            o_ref[rows, :] = o.astype(BF16)
            lse_ref[rows, :] = lse

    blk = pl.BlockSpec((per * QB, LANES), lambda p, b: (b, p))
    return pl.pallas_call(
        body, name="attn_fwd", grid=(npair, nb),
        in_specs=[blk] + _window_specs(per) + _window_specs(per) + [_bias_spec(per, sub) for sub in range(nbias)],
        out_specs=(blk, blk),
        out_shape=(jax.ShapeDtypeStruct((s, d), BF16), jax.ShapeDtypeStruct((s, d), F32)),
        compiler_params=_params(VMEM_BIG),
    )(q, *([k] * nwin), *([v] * nwin), *([bias] * nbias))


def _attn_out_fwd(o, x, tab, wo, tm):
    s, d = x.shape
    base = 6

    def body(o_ref, x_ref, t_ref, wo_ref, x3_ref, y_ref):
        yv = jnp.dot(o_ref[...], wo_ref[...], preferred_element_type=F32)
        y_ref[...] = yv
        x3_ref[...] = x_ref[...] + (yv * _rs(yv)) * t_ref[base + R_P1:base + R_P1 + 1, :]

    return pl.pallas_call(
        body, name="attn_out_fwd", grid=(s // tm,),
        in_specs=[_rows(tm, d), _rows(tm, d), _const(tab.shape), _const(wo.shape)],
        out_specs=(_rows(tm, d), _rows(tm, d)),
        out_shape=(jax.ShapeDtypeStruct((s, d), F32), jax.ShapeDtypeStruct((s, d), F32)),
        compiler_params=_params(VMEM_BIG),
    )(o, x, tab, wo)


def _ffn_bwd(dxo, x, y, gu, tab, base, wfi, wfo, tm, name):
    s, d = x.shape
    hid = wfo.shape[0]
    nblk = hid // MXU

    def body(dxo_ref, x_ref, y_ref, gu_ref, t_ref, wfi_ref, wfo_ref, dx_ref, dyb_ref, dgu_ref, a_ref, sums_ref):
        @pl.when(pl.program_id(0) == 0)
        def _():
            sums_ref[...] = jnp.zeros_like(sums_ref)

        dxo_v = dxo_ref[...]
        yv = y_ref[...]
        ry = _rs(yv)
        ny = yv * ry
        sums_ref[R_P2:R_P2 + 1, :] += _colsum(dxo_v * ny)
        dyb = _norm_bwd(dxo_v * t_ref[base + R_P2:base + R_P2 + 1, :], ny, ry).astype(BF16)
        dyb_ref[...] = dyb
        dh = jnp.zeros((tm, d), F32)
        for c0 in range(0, nblk, FFN_CHUNK):
            for j in range(c0, min(c0 + FFN_CHUNK, nblk)):
                lo, hi = j * MXU, (j + 1) * MXU
                da = lax.dot_general(dyb, wfo_ref[lo:hi, :], NT, preferred_element_type=F32)
                g, u = gu_ref[:, lo:hi].astype(F32), gu_ref[:, hid + lo:hid + hi].astype(F32)
                sg = _sigmoid(g)
                gs = g * sg
                a_ref[:, lo:hi] = (gs * u).astype(BF16)
                dgu_ref[:, lo:hi] = (da * u * sg * (1.0 + g * (1.0 - sg))).astype(BF16)
                dgu_ref[:, hid + lo:hid + hi] = (da * gs).astype(BF16)
            lo, hi = c0 * MXU, min(c0 + FFN_CHUNK, nblk) * MXU
            dh = dh + jnp.dot(dgu_ref[:, lo:hi], wfi_ref[lo:hi, :], preferred_element_type=F32)
            dh = dh + jnp.dot(dgu_ref[:, hid + lo:hid + hi], wfi_ref[hid + lo:hid + hi, :],
                              preferred_element_type=F32)
        xv = x_ref[...]
        r = _rs(xv)
        n = xv * r
        sums_ref[R_SH2:R_SH2 + 1, :] += _colsum(dh)
        sums_ref[R_W2:R_W2 + 1, :] += _colsum(dh * n)
        dx_ref[...] = dxo_v + _norm_bwd(dh * t_ref[base + R_W2:base + R_W2 + 1, :], n, r)

    return pl.pallas_call(
        body, name=name, grid=(s // tm,),
        in_specs=[_rows(tm, d), _rows(tm, d), _rows(tm, d), _rows(tm, 2 * hid),
                  _const(tab.shape), _const(wfi.shape), _const(wfo.shape)],
        out_specs=(_rows(tm, d), _rows(tm, d), _rows(tm, 2 * hid), _rows(tm, hid),
                   pl.BlockSpec((8, d), lambda i: (0, 0))),
        out_shape=(jax.ShapeDtypeStruct((s, d), F32), jax.ShapeDtypeStruct((s, d), BF16),
                   jax.ShapeDtypeStruct((s, 2 * hid), BF16), jax.ShapeDtypeStruct((s, hid), BF16),
                   jax.ShapeDtypeStruct((8, d), F32)),
        compiler_params=_params(VMEM_BIG),
    )(dxo, x, y, gu, tab, wfi, wfo)


def _attn_out_bwd(dx, y, tab, wo, tm):
    s, d = y.shape
    base = 6

    def body(dx_ref, y_ref, t_ref, wo_ref, dyb_ref, do_ref, sums_ref):
        @pl.when(pl.program_id(0) == 0)
        def _():
            sums_ref[...] = jnp.zeros_like(sums_ref)

        dxv = dx_ref[...]
        yv = y_ref[...]
        ry = _rs(yv)
        ny = yv * ry
        sums_ref[R_P1:R_P1 + 1, :] += _colsum(dxv * ny)
        dyb = _norm_bwd(dxv * t_ref[base + R_P1:base + R_P1 + 1, :], ny, ry).astype(BF16)
        dyb_ref[...] = dyb
        do_ref[...] = lax.dot_general(dyb, wo_ref[...], NT, preferred_element_type=F32).astype(BF16)

    return pl.pallas_call(
        body, name="attn_out_bwd", grid=(s // tm,),
        in_specs=[_rows(tm, d), _rows(tm, d), _const(tab.shape), _const(wo.shape)],
        out_specs=(_rows(tm, d), _rows(tm, d), pl.BlockSpec((8, d), lambda i: (0, 0))),
        out_shape=(jax.ShapeDtypeStruct((s, d), BF16), jax.ShapeDtypeStruct((s, d), BF16),
                   jax.ShapeDtypeStruct((8, d), F32)),
        compiler_params=_params(VMEM_BIG),
    )(dx, y, tab, wo)


def _attn_bwd(q, k, v, o, do, lse, bias):
    s, d = q.shape
    per = ATTN_PER
    npair, nb = d // LANES, s // (per * QB)
    hpp = LANES // HEAD_DIM
    nwin = 2 + per
    nbias = min(per, BIAS_VARIANTS)

    def body(*refs):
        q_ref, k_refs, v_refs = refs[0], refs[1:1 + nwin], refs[1 + nwin:1 + 2 * nwin]
        o_ref, do_ref, lse_ref = refs[1 + 2 * nwin:4 + 2 * nwin]
        bias_refs = refs[4 + 2 * nwin:4 + 2 * nwin + nbias]
        dq_ref, dk_ref, dv_ref, db_ref = refs[4 + 2 * nwin + nbias:]
        b = pl.program_id(1)

        @pl.when(b == 0)
        def _():
            dk_ref[...] = jnp.zeros_like(dk_ref)
            dv_ref[...] = jnp.zeros_like(dv_ref)
            db_ref[...] = jnp.zeros_like(db_ref)

        ks = [r[...] for r in k_refs]
        vs = [r[...] for r in v_refs]
        masks = _head_masks()
        for sub in range(per):
            rows = slice(sub * QB, (sub + 1) * QB)
            qv = q_ref[rows, :]
            dov = do_ref[rows, :]
            lsev = lse_ref[rows, :]
            doo = dov.astype(F32) * o_ref[rows, :].astype(F32)
            kwin = jnp.concatenate(ks[sub:sub + 3], axis=0)
            vwin = jnp.concatenate(vs[sub:sub + 3], axis=0)
            dq = jnp.zeros((QB, LANES), F32)
            dkw = jnp.zeros((KW, LANES), F32)
            dvw = jnp.zeros((KW, LANES), F32)
            for hh in range(hpp):
                qm = jnp.where(masks[hh], qv, jnp.zeros_like(qv))
                dom = jnp.where(masks[hh], dov, jnp.zeros_like(dov))
                km = jnp.where(masks[hh], kwin, jnp.zeros_like(kwin))
                lse_h = jnp.max(jnp.where(masks[hh], lsev, NEG), axis=-1, keepdims=True)
                delta = jnp.sum(jnp.where(masks[hh], doo, 0.0), axis=-1, keepdims=True)
                sc = lax.dot_general(qm, kwin, NT, preferred_element_type=F32) + bias_refs[min(sub, nbias - 1)][hh]
                p = jnp.exp(sc - lse_h)
                dp = lax.dot_general(dom, vwin, NT, preferred_element_type=F32)
                ds = p * (dp - delta)
                db_ref[hh] += ds
                dsb = ds.astype(BF16)
                dq = dq + jnp.dot(dsb, km, preferred_element_type=F32)
                dkw = dkw + lax.dot_general(dsb, qm, TN, preferred_element_type=F32)
                dvw = dvw + lax.dot_general(p.astype(BF16), dom, TN, preferred_element_type=F32)
            dq_ref[rows, :] = (dq * (HEAD_DIM ** -0.5)).astype(BF16)
            for w in range(3):
                start = pl.multiple_of(jnp.maximum(per * b + sub - 2 + w, 0) * QB, QB)
                dk_ref[pl.ds(start, QB), :] += dkw[w * QB:(w + 1) * QB, :]
                dv_ref[pl.ds(start, QB), :] += dvw[w * QB:(w + 1) * QB, :]

    blk = pl.BlockSpec((per * QB, LANES), lambda p, b: (b, p))
    col = pl.BlockSpec((s, LANES), lambda p, b: (0, p))
    pair = pl.BlockSpec((hpp, QB, KW), lambda p, b: (p, 0, 0))
    return pl.pallas_call(
        body, name="attn_bwd", grid=(npair, nb),
        in_specs=[blk] + _window_specs(per) + _window_specs(per) + [blk, blk, blk]
                 + [_bias_spec(per, sub) for sub in range(nbias)],
        out_specs=(blk, col, col, pair),
        out_shape=(jax.ShapeDtypeStruct((s, d), BF16), jax.ShapeDtypeStruct((s, d), F32),
                   jax.ShapeDtypeStruct((s, d), F32), jax.ShapeDtypeStruct(bias.shape[1:], F32)),
        compiler_params=_params(VMEM_BIG),
    )(q, *([k] * nwin), *([v] * nwin), o, do, lse, *([bias] * nbias))


def _qkv_bwd(dres, dq, dk, dv, x, tab, wq, wkv, tm):
    s, d = x.shape
    base = 6

    def body(dres_ref, dq_ref, dk_ref, dv_ref, x_ref, t_ref, wq_ref, wkv_ref, dx_ref, dkv_ref, sums_ref):
        @pl.when(pl.program_id(0) == 0)
        def _():
            sums_ref[...] = jnp.zeros_like(sums_ref)

        dh1 = lax.dot_general(dq_ref[...], wq_ref[...], NT, preferred_element_type=F32)
        dkv_ref[:, 0:d] = dk_ref[...].astype(BF16)
        dkv_ref[:, d:2 * d] = dv_ref[...].astype(BF16)
        dhkv = jnp.dot(dkv_ref[...], wkv_ref[...], preferred_element_type=F32)
        xv = x_ref[...]
        r = _rs(xv)
        n = xv * r
        sums_ref[0:1, :] += _colsum(dh1 * n)
        sums_ref[1:2, :] += _colsum(dh1)
        sums_ref[2:3, :] += _colsum(dhkv * n)
        sums_ref[3:4, :] += _colsum(dhkv)
        dn = dh1 * t_ref[base + R_W1:base + R_W1 + 1, :] + dhkv * t_ref[R_KV:R_KV + 1, :]
        dx_ref[...] = dres_ref[...] + _norm_bwd(dn, n, r)

    return pl.pallas_call(
        body, name="qkv_bwd", grid=(s // tm,),
        in_specs=[_rows(tm, d)] * 5 + [_const(tab.shape), _const(wq.shape), _const(wkv.shape)],
        out_specs=(_rows(tm, d), _rows(tm, 2 * d), pl.BlockSpec((8, d), lambda i: (0, 0))),
        out_shape=(jax.ShapeDtypeStruct((s, d), F32), jax.ShapeDtypeStruct((s, 2 * d), BF16),
                   jax.ShapeDtypeStruct((8, d), F32)),
        compiler_params=_params(VMEM_BIG),
    )(dres, dq, dk, dv, x, tab, wq, wkv)


def _conv_bwd(dx1, x, y, bcx, tab, ck, wci, wco, tm):
    s, d = x.shape
    nt = s // tm

    def rev(i):
        return (nt - 1 - i, 0)

    def halo(i):
        return (jnp.maximum((nt - 1 - i) * (tm // 8) - 1, 0), 0)

    def body(dx_ref, x_ref, y_ref, bcx_ref, halo_ref, t_ref, ck_ref, wci_ref, wco_ref,
             dx0_ref, dyb_ref, dbcx_ref, sums_ref, dck_ref, carry):
        i = pl.program_id(0)

        @pl.when(i == 0)
        def _():
            sums_ref[...] = jnp.zeros_like(sums_ref)
            dck_ref[...] = jnp.zeros_like(dck_ref)
            carry[...] = jnp.zeros_like(carry)

        dxv = dx_ref[...]
        yv = y_ref[...]
        ry = _rs(yv)
        ny = yv * ry
        sums_ref[R_P1:R_P1 + 1, :] += _colsum(dxv * ny)
        dyb = _norm_bwd(dxv * t_ref[R_P1:R_P1 + 1, :], ny, ry).astype(BF16)
        dyb_ref[...] = dyb
        du = lax.dot_general(dyb, wco_ref[...], NT, preferred_element_type=F32)
        bg, cg, xi = bcx_ref[:, 0:d], bcx_ref[:, d:2 * d], bcx_ref[:, 2 * d:3 * d]
        z = cg * xi
        zp = halo_ref[:, d:2 * d] * halo_ref[:, 2 * d:3 * d]
        zp = jnp.where(i == nt - 1, jnp.zeros_like(zp), zp)
        row = lax.broadcasted_iota(jnp.int32, z.shape, 0)
        c1, c2 = zp[7:8, :], zp[6:7, :]
        z1 = jnp.where(row == 0, c1, pltpu.roll(z, 1, 0))
        z2 = jnp.where(row == 0, c2, jnp.where(row == 1, c1, pltpu.roll(z, 2, 0)))
        k0, k1, k2 = ck_ref[0:1, :], ck_ref[1:2, :], ck_ref[2:3, :]
        conv = k0 * z2 + k1 * z1 + k2 * z
        dconv = du * bg
        dck_ref[0:1, :] += _colsum(dconv * z2)
        dck_ref[1:2, :] += _colsum(dconv * z1)
        dck_ref[2:3, :] += _colsum(dconv * z)
        n1, n2 = carry[0:1, :], carry[1:2, :]
        d1 = jnp.where(row == tm - 1, n1, pltpu.roll(dconv, tm - 1, 0))
        d2 = jnp.where(row == tm - 1, n2, jnp.where(row == tm - 2, n1, pltpu.roll(dconv, tm - 2, 0)))
        carry[...] = dconv[0:8, :]
        dz = k2 * dconv + k1 * d1 + k0 * d2
        dbcx_ref[:, 0:d] = (du * conv).astype(BF16)
        dbcx_ref[:, d:2 * d] = (dz * xi).astype(BF16)
        dbcx_ref[:, 2 * d:3 * d] = (dz * cg).astype(BF16)
        dh = jnp.dot(dbcx_ref[...], wci_ref[...], preferred_element_type=F32)
        xv = x_ref[...]
        r = _rs(xv)
        n = xv * r
        sums_ref[R_W1:R_W1 + 1, :] += _colsum(dh * n)
        sums_ref[R_SH1:R_SH1 + 1, :] += _colsum(dh)
        dx0_ref[...] = dxv + _norm_bwd(dh * t_ref[R_W1:R_W1 + 1, :], n, r)

    rrow = lambda cols: pl.BlockSpec((tm, cols), rev)
    acc = pl.BlockSpec((8, d), lambda i: (0, 0))
    return pl.pallas_call(
        body, name="conv_bwd", grid=(nt,),
        in_specs=[rrow(d), rrow(d), rrow(d), rrow(3 * d), pl.BlockSpec((8, 3 * d), halo),
                  _const(tab.shape), _const(ck.shape), _const(wci.shape), _const(wco.shape)],
        out_specs=(rrow(d), rrow(d), rrow(3 * d), acc, acc),
        out_shape=(jax.ShapeDtypeStruct((s, d), F32), jax.ShapeDtypeStruct((s, d), BF16),
                   jax.ShapeDtypeStruct((s, 3 * d), BF16), jax.ShapeDtypeStruct((8, d), F32),
                   jax.ShapeDtypeStruct((8, d), F32)),
        scratch_shapes=[pltpu.VMEM((8, d), F32)],
        compiler_params=_params(VMEM_BIG),
    )(dx1, x, y, bcx, bcx, tab, ck, wci, wco)


def _wgrad_wide(a, b, nblk, tk, name):
    s, m = a.shape
    n = b.shape[1] // nblk
    nk = s // tk

    def body(a_ref, b_ref, o_ref, acc):
        kk = pl.program_id(0)

        @pl.when(kk == 0)
        def _():
            acc[...] = jnp.zeros_like(acc)

        acc[...] += jnp.dot(a_ref[...].T, b_ref[...], preferred_element_type=F32)

        @pl.when(kk == nk - 1)
        def _():
            for j in range(nblk):
                o_ref[j] = acc[:, j * n:(j + 1) * n].astype(BF16)

    return pl.pallas_call(
        body, name=name, grid=(nk,),
        in_specs=[pl.BlockSpec((tk, m), lambda kk: (kk, 0)), pl.BlockSpec((tk, nblk * n), lambda kk: (kk, 0))],
        out_specs=pl.BlockSpec((nblk, m, n), lambda kk: (0, 0, 0)),
        out_shape=jax.ShapeDtypeStruct((nblk, m, n), BF16),
        scratch_shapes=[pltpu.VMEM((m, nblk * n), F32)],
        compiler_params=_params(VMEM_BIG),
    )(a, b)


def _wgrad_rows(a, b, ncb, tk, name):
    s, m = a.shape
    n = b.shape[1]
    mb = m // ncb
    nk = s // tk

    def body(a_ref, b_ref, o_ref, acc):
        kk = pl.program_id(1)

        @pl.when(kk == 0)
        def _():
            acc[...] = jnp.zeros_like(acc)

        acc[...] += jnp.dot(a_ref[...].T, b_ref[...], preferred_element_type=F32)

        @pl.when(kk == nk - 1)
        def _():
            o_ref[...] = acc[...].astype(BF16)

    return pl.pallas_call(
        body, name=name, grid=(ncb, nk),
        in_specs=[pl.BlockSpec((tk, mb), lambda j, kk: (kk, j)), pl.BlockSpec((tk, n), lambda j, kk: (kk, 0))],
        out_specs=pl.BlockSpec((mb, n), lambda j, kk: (j, 0)),
        out_shape=jax.ShapeDtypeStruct((m, n), BF16),
        scratch_shapes=[pltpu.VMEM((mb, n), F32)],
        compiler_params=_params(VMEM_BIG),
    )(a, b)


def _adamw_math(w, g, m, v):
    m = ADAM_B1 * m + (1.0 - ADAM_B1) * g
    v = ADAM_B2 * v + (1.0 - ADAM_B2) * (g * g)
    m_hat = m / (1.0 - ADAM_B1 ** ADAM_STEP)
    v_hat = v / (1.0 - ADAM_B2 ** ADAM_STEP)
    delta = -ADAM_LR * (m_hat / (jnp.sqrt(v_hat) + ADAM_EPS) + ADAM_WD * w)
    return delta, m, v


def _adamw_reduce(parts, w, m, v, tr, name, parts_t=False):
    nl, r, c = w.shape
    tr = r if (parts_t and r % LANES) else (LANES if parts_t else _row_tile(r, tr))

    def body(*refs):
        p_refs = refs[:nl]
        w_ref, m_ref, v_ref, g_ref, d_ref, mo_ref, vo_ref = refs[nl:]
        layer = pl.program_id(0)

        def partial(i):
            val = p_refs[0][i].astype(F32)
            for q in range(1, nl):
                val = jnp.where(layer == q, p_refs[q][i].astype(F32), val)
            return val

        g = partial(0)
        for i in range(1, N_DEV):
            g = g + partial(i)
        if parts_t:
            g = g.T
        g_ref[...] = g
        d_ref[...], mo_ref[...], vo_ref[...] = _adamw_math(w_ref[...], g, m_ref[...], v_ref[...])

    blk = pl.BlockSpec((None, tr, c), lambda l, i: (l, i, 0))
    out = jax.ShapeDtypeStruct((nl, r, c), F32)
    if parts_t:
        p_specs = [pl.BlockSpec((N_DEV, c, tr), lambda l, i: (0, 0, i))]
    else:
        p_specs = [pl.BlockSpec((N_DEV, tr, c), (lambda l, i, q=q: (0, jnp.where(l == q, i, 0), 0)))
                   for q in range(nl)]
    return pl.pallas_call(
        body, name=name, grid=(nl, r // tr),
        in_specs=p_specs + [blk, blk, blk],
        out_specs=(blk,) * 4, out_shape=(out,) * 4,
        compiler_params=_params(VMEM_BIG),
    )(*parts, w, m, v)


def _adamw_outer(sct, dm, w, m, v, tr, name):
    nl, d, c = w.shape

    def body(s_ref, dm_ref, w_ref, m_ref, v_ref, g_ref, d_ref, mo_ref, vo_ref):
        g = jnp.dot(s_ref[...], dm_ref[...], preferred_element_type=F32)
        g_ref[...] = g
        d_ref[...], mo_ref[...], vo_ref[...] = _adamw_math(w_ref[...], g, m_ref[...], v_ref[...])

    blk = pl.BlockSpec((None, tr, c), lambda l, i: (l, i, 0))
    out = jax.ShapeDtypeStruct((nl, d, c), F32)
    return pl.pallas_call(
        body, name=name, grid=(nl, d // tr),
        in_specs=[pl.BlockSpec((tr, N_DEV), lambda l, i: (i, 0)),
                  pl.BlockSpec((None, N_DEV, c), lambda l, i: (l, 0, 0)), blk, blk, blk],
        out_specs=(blk,) * 4, out_shape=(out,) * 4,
        compiler_params=_params(VMEM_BIG),
    )(sct, dm, w, m, v)


def _pad_rows(a, rows):
    return jnp.concatenate([a, jnp.zeros((rows - a.shape[0],) + a.shape[1:], a.dtype)], axis=0)


def kernel(x, c, mod_w, mod_b, norm_g, ffn_w_in, ffn_w_out, conv_w_in, conv_k, conv_w_out, kv_mod_w, kv_mod_b, kv_norm_g, w_kv, attn_w_q, attn_w_o, rel_bias, loss_target, m_mod_w, m_mod_b, m_norm_g, m_ffn_w_in, m_ffn_w_out, m_conv_w_in, m_conv_k, m_conv_w_out, m_kv_mod_w, m_kv_mod_b, m_kv_norm_g, m_w_kv, m_attn_w_q, m_attn_w_o, m_rel_bias, v_mod_w, v_mod_b, v_norm_g, v_ffn_w_in, v_ffn_w_out, v_conv_w_in, v_conv_k, v_conv_w_out, v_kv_mod_w, v_kv_mod_b, v_kv_norm_g, v_w_kv, v_attn_w_q, v_attn_w_o, v_rel_bias):
    s, d = x.shape[1], x.shape[2]
    dq = d // LANES
    dsh = d // N_DEV
    nl = mod_w.shape[0]
    mw = mod_w.shape[2]
    kmw = kv_mod_w.shape[1]
    fw = ffn_w_in.shape[2]
    nh, nrel = rel_bias.shape[1], rel_bias.shape[2]
    tm = min(256, s)
    tm2 = min(512, s)
    tk = min(1024, s)
    me = 4 * lax.axis_index("x") + 2 * lax.axis_index("y") + lax.axis_index("c")

    x0 = x[0]
    tgt = loss_target[0]

    small1 = jnp.concatenate([c.reshape(dq, LANES), norm_g.reshape(dq, LANES),
                              _pad_rows(conv_k[0], 8).reshape(dq, LANES)], axis=0)
    (sm,) = _exchange([small1], ["gather"], "gather_small")
    c_all = sm[:, 0:dq].reshape(N_DEV, d)
    ng_full = jnp.transpose(sm[:, dq:2 * dq].reshape(N_DEV, 8, dsh), (1, 0, 2)).reshape(8, d)
    ck_full = jnp.transpose(sm[:, 2 * dq:3 * dq].reshape(N_DEV, 8, dsh), (1, 0, 2)).reshape(8, d)

    modcols, silu_c = _mod_fwd(c_all, mod_w, kv_mod_w)
    (modall,) = _exchange([modcols], ["gather"], "gather_mod")

    cast = lambda *ws: [a.astype(BF16) for a in ws]
    gath = lambda ws: (ws, ["gather"] * len(ws))
    half = lambda ws: (ws, ["gather_half"] * len(ws))
    (h_conv, h_ffn0, h_attn, h_ffn1), token = _xstart(
        [half(cast(conv_w_in[0].T, conv_w_out[0])), half(cast(jnp.swapaxes(ffn_w_in[0], 0, 1), ffn_w_out[0])),
         gath(cast(w_kv.T, attn_w_q[0], attn_w_o[0])), gath(cast(jnp.swapaxes(ffn_w_in[1], 0, 1), ffn_w_out[1]))],
        modall, "gather_start")
    modall = modall + token[0, 0]
    mine = lax.dynamic_index_in_dim(modall, me, axis=1, keepdims=False)
    modrow = jnp.stack([mine[:, l * mw:(l + 1) * mw].reshape(6, d) for l in range(nl)])
    kvrow = mine[:, nl * mw:nl * mw + kmw].reshape(2, d)
    tab, modval = _vec_prep(modrow, mod_b.reshape(nl, 6, d), kvrow, kv_mod_b.reshape(2, d), ng_full,
                            kv_norm_g.reshape(1, d))
    bias = _bias_fwd(rel_bias[0])

    wci, wco = _xwait(h_conv, [bias], "gather_wait_conv")
    wci, wco = _forward_to_sibling([wci, wco], "gather_forward_conv")
    wci, wco = wci.reshape(3 * d, d), wco.reshape(d, d)
    x1, h1a, bcx, ua, ya = _conv_fwd(x0, tab, ck_full, wci, wco, tm2)
    wfi0, wfo0 = _xwait(h_ffn0, [x1], "gather_wait_ffn0")
    wfi0, wfo0 = _forward_to_sibling([wfi0, wfo0], "gather_forward_ffn0")
    wfi0, wfo0 = wfi0.reshape(-1, d), wfo0.reshape(-1, d)
    x2, h2a, gua, y2a = _ffn_fwd(x1, tab, 0, wfi0, wfo0, None, tm2, "ffn_fwd0")
    wkv, wq, wo = _xwait(h_attn, [x2], "gather_wait_attn")
    wkv, wq, wo = wkv.reshape(2 * d, d), wq.reshape(d, d), wo.reshape(d, d)
    hkv, h1b, q, k, v = _qkv_fwd(x2, tab, wq, wkv, tm2)
    o, lse = _attn_fwd(q, k, v, bias)
    x3, yb = _attn_out_fwd(o, x2, tab, wo, tm2)
    wfi1, wfo1 = _xwait(h_ffn1, [x3], "gather_wait_ffn1")
    wfi1, wfo1 = wfi1.reshape(-1, d), wfo1.reshape(-1, d)
    dx4, h2b, gub, y2b, loss_acc = _ffn_fwd(x3, tab, 6, wfi1, wfo1, tgt, tm2, "ffn_fwd1")

    scat = lambda ws: [(ws, ["scatter"] * len(ws))]
    dx3, dy2b, dgub, ab, sums_f1 = _ffn_bwd(dx4, x3, y2b, gub, tab, 6, wfi1, wfo1, tm, "ffn_bwd1")
    g_wfi1 = _wgrad_rows(dgub, h2b, 4, tk, "wgrad_ffn_in1").reshape(N_DEV, -1, d)
    g_wfo1 = _wgrad_rows(ab, dy2b, 2, tk, "wgrad_ffn_out1").reshape(N_DEV, -1, d)
    (h_g1,), token = _xstart(scat([g_wfi1, g_wfo1]), dx3, "grads_start_ffn1")
    tab = tab + token[0, 0]
    dyb, do, sums_o = _attn_out_bwd(dx3, yb, tab, wo, tm2)
    g_wo = _wgrad_wide(o, dyb, 1, tk, "wgrad_o").reshape(N_DEV, dsh, d)
    dqb, dk, dv, dbias = _attn_bwd(q, k, v, o, do, lse, bias)
    g_wq = _wgrad_wide(h1b, dqb, 1, tk, "wgrad_q").reshape(N_DEV, dsh, d)
    dx2, dkvb, sums_q = _qkv_bwd(dx3, dqb, dk, dv, x2, tab, wq, wkv, tm2)
    g_wkv = _wgrad_rows(dkvb, hkv, 2, tk, "wgrad_kv").reshape(N_DEV, -1, d)
    (h_g2,), token = _xstart(scat([g_wkv, g_wq, g_wo]), dx2, "grads_start_attn")
    tab = tab + token[0, 0]
    dx1, dy2a, dgua, aa, sums_f0 = _ffn_bwd(dx2, x1, y2a, gua, tab, 0, wfi0, wfo0, tm, "ffn_bwd0")
    g_wfi0 = _wgrad_rows(dgua, h2a, 4, tk, "wgrad_ffn_in0").reshape(N_DEV, -1, d)
    g_wfo0 = _wgrad_rows(aa, dy2a, 2, tk, "wgrad_ffn_out0").reshape(N_DEV, -1, d)
    (h_g3,), token = _xstart(scat([g_wfi0, g_wfo0]), dx1, "grads_start_ffn0")
    tab = tab + token[0, 0]
    dx0, dya, dbcx, sums_c, dck = _conv_bwd(dx1, x0, ya, bcx, tab, ck_full, wci, wco, tm2)
    drel = _bias_bwd(dbias, nrel)
    dmod, dng, dkvg = _vec_bwd(sums_c, sums_f0, sums_q, sums_o, sums_f1, modval, ng_full, kv_norm_g.reshape(1, d))

    relw = -(-nrel // LANES) * LANES
    drel_p = jnp.concatenate([drel, jnp.zeros((nh, relw - nrel), F32)], axis=1)
    small3 = jnp.concatenate([dmod.reshape(16 * dq, LANES), dng.reshape(8 * dq, LANES), dkvg.reshape(8 * dq, LANES),
                              dck.reshape(8 * dq, LANES), loss_acc,
                              drel_p.reshape(nh * relw // LANES, LANES)], axis=0)
    (sm,) = _exchange([small3], ["gather"], "gather_small_grads")
    g_wci = _wgrad_rows(dbcx, h1a, 3, tk, "wgrad_conv_in").reshape(N_DEV, -1, d)
    g_wco = _wgrad_wide(ua, dya, 1, tk, "wgrad_conv_out").reshape(N_DEV, dsh, d)
    (h_g4,), token = _xstart(scat([g_wci, g_wco]), sm, "grads_start_conv")
    sm = sm + token[0, 0]
    o1, o2, o3, o4, o5 = 16 * dq, 24 * dq, 32 * dq, 40 * dq, 40 * dq + 8
    loss = jnp.sum(sm[:, o4, 0]) * (0.5 / d)
    dmod_all = sm[:, 0:o1].reshape(N_DEV, 16, d)
    mine_cols = lambda a: lax.dynamic_slice_in_dim(a, me * dsh, dsh, axis=2)
    dng_parts = mine_cols(sm[:, o1:o2].reshape(N_DEV, 8, d))
    dkvg_parts = sm[:, o2:o3].reshape(N_DEV, 8, d)[:, 0:1]
    dck_parts = mine_cols(sm[:, o3:o4].reshape(N_DEV, 8, d))[:, 0:3]
    drel_parts = sm[:, o5:].reshape(N_DEV, nh, relw)[:, :, 0:nrel]

    def update(parts, w, m, v, name, layers=1, parts_t=False):
        shp = w.shape
        w3, m3, v3 = (a.reshape(layers, -1, shp[-1]) for a in (w, m, v))
        if not parts_t:
            parts = [p.reshape(N_DEV, -1, shp[-1]) for p in parts]
        outs = _adamw_reduce(parts, w3, m3, v3, 256, name, parts_t)
        return [a.reshape(shp) for a in outs]

    p_wfi1, p_wfo1 = _xwait(h_g1, [sm], "grads_wait_ffn1")
    p_wfi0, p_wfo0 = _xwait(h_g3, [p_wfi1], "grads_wait_ffn0")
    tr = lambda a: jnp.swapaxes(a, 1, 2)
    u_ffn_in = [tr(a) for a in update([p_wfi0, p_wfi1], tr(ffn_w_in), tr(m_ffn_w_in), tr(v_ffn_w_in),
                                      "adamw_ffn_in", 2)]
    u_ffn_out = update([p_wfo0, p_wfo1], ffn_w_out, m_ffn_w_out, v_ffn_w_out, "adamw_ffn_out", 2)
    p_wkv, p_wq, p_wo = _xwait(h_g2, [u_ffn_out[0]], "grads_wait_attn")
    u_w_kv = update([p_wkv], w_kv, m_w_kv, v_w_kv, "adamw_w_kv", parts_t=True)
    u_w_q = update([p_wq], attn_w_q, m_attn_w_q, v_attn_w_q, "adamw_w_q")
    u_w_o = update([p_wo], attn_w_o, m_attn_w_o, v_attn_w_o, "adamw_w_o")

    sct = jnp.transpose(silu_c)
    dm_mod = jnp.stack([lax.dynamic_slice_in_dim(dmod_all[:, 6 * l:6 * l + 6].reshape(N_DEV, 6 * d), me * mw, mw, axis=1)
                        for l in range(nl)]).astype(BF16)
    dm_kv = lax.dynamic_slice_in_dim(dmod_all[:, R_KV:R_KV + 2].reshape(N_DEV, 2 * d), me * kmw, kmw, axis=1)
    u_mod_w = _adamw_outer(sct, dm_mod, mod_w, m_mod_w, v_mod_w, min(256, d), "adamw_mod_w")
    u_kv_mod_w = [a[0] for a in _adamw_outer(sct, dm_kv.astype(BF16)[None], kv_mod_w[None], m_kv_mod_w[None],
                                             v_kv_mod_w[None], min(256, d), "adamw_kv_mod_w")]

    modb_parts = jnp.stack([dmod_all[:, 6 * l:6 * l + 6].reshape(N_DEV, 6 * d) for l in range(nl)], axis=1)
    u_mod_b = update([modb_parts], mod_b, m_mod_b, v_mod_b, "adamw_mod_b")
    u_norm_g = update([dng_parts], norm_g.reshape(8, dsh), m_norm_g.reshape(8, dsh), v_norm_g.reshape(8, dsh), "adamw_norm_g")
    u_norm_g = [a.reshape(norm_g.shape) for a in u_norm_g]
    u_conv_k = update([dck_parts], conv_k, m_conv_k, v_conv_k, "adamw_conv_k")
    kvb_parts = dmod_all[:, R_KV:R_KV + 2].reshape(N_DEV, 1, 2 * d)
    u_kv_mod_b = [a.reshape(kv_mod_b.shape) for a in update([kvb_parts], kv_mod_b.reshape(1, -1), m_kv_mod_b.reshape(1, -1),
                                                            v_kv_mod_b.reshape(1, -1), "adamw_kv_mod_b")]
    u_kv_norm_g = [a.reshape(kv_norm_g.shape) for a in update([dkvg_parts], kv_norm_g.reshape(1, -1), m_kv_norm_g.reshape(1, -1),
                                                              v_kv_norm_g.reshape(1, -1), "adamw_kv_norm_g")]
    u_rel = update([drel_parts], rel_bias, m_rel_bias, v_rel_bias, "adamw_rel_bias")

    others = [u_ffn_in, u_ffn_out, u_w_kv, u_w_q, u_w_o, u_mod_w, u_kv_mod_w, u_mod_b, u_norm_g, u_conv_k, u_kv_mod_b,
              u_kv_norm_g, u_rel]
    p_wci, p_wco = _xwait(h_g4, [u[3] for u in others], "grads_wait_conv")
    u_conv_in = update([p_wci], conv_w_in, m_conv_w_in, v_conv_w_in, "adamw_conv_in", parts_t=True)
    u_conv_out = update([p_wco], conv_w_out, m_conv_w_out, v_conv_w_out, "adamw_conv_out")

    ups = [u_mod_w, u_mod_b, u_norm_g, u_ffn_in, u_ffn_out, u_conv_in, u_conv_k, u_conv_out, u_kv_mod_w, u_kv_mod_b,
           u_kv_norm_g, u_w_kv, u_w_q, u_w_o, u_rel]
    return (loss, dx0[None], *[u[0] for u in ups], *[u[1] for u in ups], *[u[2] for u in ups], *[u[3] for u in ups])
```

```python
import jax
import jax.numpy as jnp
from jax import lax
from jax.experimental import pallas as pl
from jax.experimental.pallas import tpu as pltpu

F32 = jnp.float32
BF16 = jnp.bfloat16

EPS = 1e-6
CHUNK = 64
HEAD_DIM = 64
N_LEFT = 8
LANES = 128
MXU = 256
FFN_CHUNK = 4
QB = 4 * CHUNK
KW = QB + N_LEFT * CHUNK
BIAS_VARIANTS = N_LEFT * CHUNK // QB + 1
ATTN_PER = 8
NEG = -1e30
N_DEV = 8

ADAM_LR = 0.001
ADAM_B1 = 0.9
ADAM_B2 = 0.999
ADAM_EPS = 1e-08
ADAM_WD = 0.01
ADAM_STEP = 10

VMEM_BIG = 56 * 1024 * 1024

NT = (((1,), (1,)), ((), ()))
TN = (((0,), (0,)), ((), ()))

R_W1, R_SH1, R_P1, R_W2, R_SH2, R_P2 = range(6)
R_KV = 12


def _params(vmem):
    return pltpu.CompilerParams(vmem_limit_bytes=vmem)


def _row_tile(rows, cap):
    for t in range(min(cap, rows) // 16 * 16, 0, -16):
        if rows % t == 0:
            return t
    return rows


def _rows(tm, cols):
    return pl.BlockSpec((tm, cols), lambda i: (i, 0))


def _const(shape):
    nd = len(shape)
    return pl.BlockSpec(shape, lambda *_: (0,) * nd, pipeline_mode=pl.Buffered(1))


def _rs(x):
    return lax.rsqrt(jnp.mean(x * x, axis=-1, keepdims=True) + EPS)


def _norm_bwd(d, n, r):
    return r * (d - n * jnp.mean(d * n, axis=-1, keepdims=True))


def _colsum(a):
    return jnp.sum(a, axis=0, keepdims=True)


def _sigmoid(g):
    return 1.0 / (1.0 + jnp.exp(-g))


def _exchange(arrays, modes, name):
    n = len(arrays)
    out_shape = []
    for a, mode in zip(arrays, modes):
        shp = (N_DEV,) + a.shape if mode == "gather" else a.shape
        out_shape.append(jax.ShapeDtypeStruct(shp, a.dtype))

    def body(*refs):
        ins, outs = refs[:n], refs[n:2 * n]
        send_sems, recv_sems, local_sems = refs[2 * n:]
        x, y, c = lax.axis_index("x"), lax.axis_index("y"), lax.axis_index("c")
        me = 4 * x + 2 * y + c
        local, sends, recvs = [], [], []
        for a in range(n):
            own = ins[a] if modes[a] == "gather" else ins[a].at[me]
            cp = pltpu.make_async_copy(own, outs[a].at[me], local_sems.at[a])
            cp.start()
            local.append(cp)
        for k in range(1, N_DEV):
            px = 1 - x if k & 4 else x
            py = 1 - y if k & 2 else y
            pc = 1 - c if k & 1 else c
            peer = 4 * px + 2 * py + pc
            for a in range(n):
                src = ins[a] if modes[a] == "gather" else ins[a].at[peer]
                sem = a * (N_DEV - 1) + k - 1
                cp = pltpu.make_async_remote_copy(
                    src_ref=src, dst_ref=outs[a].at[me],
                    send_sem=send_sems.at[sem], recv_sem=recv_sems.at[sem],
                    device_id=(px, py, pc), device_id_type=pl.DeviceIdType.MESH)
                cp.start()
                sends.append(cp)
                recvs.append(pltpu.make_async_remote_copy(
                    src_ref=src, dst_ref=outs[a].at[peer],
                    send_sem=send_sems.at[sem], recv_sem=recv_sems.at[sem],
                    device_id=(px, py, pc), device_id_type=pl.DeviceIdType.MESH))
        for cp in recvs:
            cp.wait_recv()
        for cp in sends:
            cp.wait_send()
        for cp in local:
            cp.wait()

    any_spec = pl.BlockSpec(memory_space=pl.ANY)
    return pl.pallas_call(
        body, name=name,
        out_shape=tuple(out_shape),
        in_specs=[any_spec] * n,
        out_specs=tuple([any_spec] * n),
        scratch_shapes=[
            pltpu.SemaphoreType.DMA((n * (N_DEV - 1),)),
            pltpu.SemaphoreType.DMA((n * (N_DEV - 1),)),
            pltpu.SemaphoreType.DMA((n,)),
        ],
    )(*arrays)


def _peers(x, y, c):
    out = []
    for k in range(1, N_DEV):
        px = 1 - x if k & 4 else x
        py = 1 - y if k & 2 else y
        pc = 1 - c if k & 1 else c
        out.append((k - 1, (px, py, pc), 4 * px + 2 * py + pc))
    return out


_OTHER_CORE_SLOTS = (2, 4, 6)


def _land_shape(a, mode):
    return a.shape if mode == "scatter" else (N_DEV,) + a.shape


_HBM = pl.BlockSpec(memory_space=pltpu.HBM)
_SEM = pl.BlockSpec(memory_space=pltpu.SEMAPHORE)
_EFFECT = pltpu.SideEffectType.DATAFLOW_SIDE_EFFECTING


def _xstart(groups, after, name):
    flat = [(a, m) for arrays, modes in groups for a, m in zip(arrays, modes)]
    n, ngr = len(flat), len(groups)
    sizes = [len(arrays) for arrays, _ in groups]
    npeer = N_DEV - 1

    def body(*refs):
        ins, lands = refs[:n], refs[n:2 * n]
        outs = refs[2 * n + 1:]
        sems = outs[:2 * ngr]
        token = outs[2 * ngr + 2 * n]
        local_sems = outs[2 * ngr + 2 * n + 1]
        stage = outs[2 * ngr + 2 * n + 2:]
        x, y, c = lax.axis_index("x"), lax.axis_index("y"), lax.axis_index("c")
        me = 4 * x + 2 * y + c
        loads, stores = [], []
        for a in range(n):
            own = ins[a].at[me] if flat[a][1] == "scatter" else ins[a]
            loads.append(pltpu.make_async_copy(own, stage[a], local_sems.at[a]))
            stores.append(pltpu.make_async_copy(stage[a], lands[a].at[me], local_sems.at[a]))
            loads[a].start()
        for a in range(n):
            loads[a].wait()
            stores[a].start()
        a = 0
        for g in range(ngr):
            for j in range(sizes[g]):
                mode = flat[a][1]
                for slot, peer, pidx in _peers(x, y, c):
                    if mode == "gather_half" and slot in _OTHER_CORE_SLOTS:
                        continue
                    pltpu.make_async_remote_copy(
                        src_ref=ins[a].at[pidx] if mode == "scatter" else ins[a], dst_ref=lands[a].at[me],
                        send_sem=sems[2 * g].at[j * npeer + slot], recv_sem=sems[2 * g + 1].at[j * npeer + slot],
                        device_id=peer, device_id_type=pl.DeviceIdType.MESH).start()
                a += 1
        for cp in stores:
            cp.wait()
        token[...] = jnp.zeros_like(token)

    out_shape, out_specs = [], []
    for sz in sizes:
        out_shape += [pltpu.SemaphoreType.DMA((sz * npeer,)), pltpu.SemaphoreType.DMA((sz * npeer,))]
        out_specs += [_SEM, _SEM]
    out_shape += [pltpu.HBM(a.shape, a.dtype) for a, _ in flat]
    out_shape += [pltpu.HBM(_land_shape(a, m), a.dtype) for a, m in flat]
    out_specs += [_HBM] * (2 * n)
    out_shape.append(jax.ShapeDtypeStruct((8, LANES), F32))
    out_specs.append(pl.BlockSpec(memory_space=pltpu.VMEM))
    args = [pltpu.with_memory_space_constraint(a, pltpu.HBM) for a, _ in flat]
    args += [pltpu.with_memory_space_constraint(lax.empty(_land_shape(a, m), a.dtype), pltpu.HBM) for a, m in flat]
    res = pl.pallas_call(
        body, name=name, out_shape=tuple(out_shape),
        in_specs=[_HBM] * (2 * n) + [pl.BlockSpec(memory_space=pl.ANY)], out_specs=tuple(out_specs),
        input_output_aliases={i: 2 * ngr + i for i in range(2 * n)},
        scratch_shapes=[pltpu.SemaphoreType.DMA((n,))]
                       + [pltpu.VMEM(a.shape[1:] if m == "scatter" else a.shape, a.dtype) for a, m in flat],
        compiler_params=pltpu.CompilerParams(has_side_effects=_EFFECT, vmem_limit_bytes=VMEM_BIG),
    )(*args, after)
    handles, a = [], 0
    for g, sz in enumerate(sizes):
        handles.append((res[2 * g], res[2 * g + 1], list(res[2 * ngr + a:2 * ngr + a + sz]),
                        list(res[2 * ngr + n + a:2 * ngr + n + a + sz]), list(groups[g][1])))
        a += sz
    return handles, res[-1]


def _xwait(handle, after, name):
    send_sems, recv_sems, srcs, lands, modes = handle
    m = len(srcs)
    npeer = N_DEV - 1
    after = list(after)

    def body(*refs):
        ins, lnd = refs[:m], refs[m:2 * m]
        ssem, rsem = refs[2 * m], refs[2 * m + 1]
        x, y, c = lax.axis_index("x"), lax.axis_index("y"), lax.axis_index("c")
        for j in range(m):
            for slot, peer, pidx in _peers(x, y, c):
                if modes[j] == "gather_half" and slot in _OTHER_CORE_SLOTS:
                    continue
                cp = pltpu.make_async_remote_copy(
                    src_ref=ins[j].at[pidx] if modes[j] == "scatter" else ins[j], dst_ref=lnd[j].at[pidx],
                    send_sem=ssem.at[j * npeer + slot], recv_sem=rsem.at[j * npeer + slot],
                    device_id=peer, device_id_type=pl.DeviceIdType.MESH)
                cp.wait_send()
                cp.wait_recv()

    res = pl.pallas_call(
        body, name=name,
        out_shape=tuple([pltpu.HBM(a.shape, a.dtype) for a in srcs] + [pltpu.HBM(a.shape, a.dtype) for a in lands]),
        in_specs=[_HBM] * (2 * m) + [_SEM, _SEM] + [pl.BlockSpec(memory_space=pl.ANY)] * len(after),
        out_specs=tuple([_HBM] * (2 * m)),
        input_output_aliases={i: i for i in range(2 * m)},
        compiler_params=pltpu.CompilerParams(has_side_effects=_EFFECT),
    )(*srcs, *lands, send_sems, recv_sems, *after)
    return list(res[m:])


def _forward_to_sibling(lands, name):
    n = len(lands)

    def body(*refs):
        outs = refs[n:2 * n]
        send_sems, recv_sems = refs[2 * n:]
        x, y, c = lax.axis_index("x"), lax.axis_index("y"), lax.axis_index("c")
        sibling = (x, y, 1 - c)
        sends, recvs = [], []
        for a in range(n):
            for j, k in enumerate((2, 4, 6)):
                px = 1 - x if k & 4 else x
                py = 1 - y if k & 2 else y
                got = 4 * px + 2 * py + c
                missing = 4 * px + 2 * py + (1 - c)
                sem = a * 3 + j
                cp = pltpu.make_async_remote_copy(
                    src_ref=outs[a].at[got], dst_ref=outs[a].at[got],
                    send_sem=send_sems.at[sem], recv_sem=recv_sems.at[sem],
                    device_id=sibling, device_id_type=pl.DeviceIdType.MESH)
                cp.start()
                sends.append(cp)
                recvs.append(pltpu.make_async_remote_copy(
                    src_ref=outs[a].at[missing], dst_ref=outs[a].at[missing],
                    send_sem=send_sems.at[sem], recv_sem=recv_sems.at[sem],
                    device_id=sibling, device_id_type=pl.DeviceIdType.MESH))
        for cp in recvs:
            cp.wait_recv()
        for cp in sends:
            cp.wait_send()

    any_spec = pl.BlockSpec(memory_space=pl.ANY)
    return list(pl.pallas_call(
        body, name=name,
        out_shape=tuple(jax.ShapeDtypeStruct(a.shape, a.dtype) for a in lands),
        in_specs=[any_spec] * n, out_specs=tuple([any_spec] * n),
        input_output_aliases={i: i for i in range(n)},
        scratch_shapes=[pltpu.SemaphoreType.DMA((3 * n,)), pltpu.SemaphoreType.DMA((3 * n,))],
    )(*lands))


def _mod_fwd(c_all, mod_w, kv_mod_w):
    nl, d, mw = mod_w.shape
    kw = kv_mod_w.shape[1]

    def body(c_ref, mw_ref, kw_ref, o_ref, sc_ref):
        cc = c_ref[...]
        sc = (cc * _sigmoid(cc)).astype(BF16)
        sc_ref[...] = sc
        for l in range(nl):
            o_ref[:, l * mw:(l + 1) * mw] = jnp.dot(sc, mw_ref[l].astype(BF16), preferred_element_type=F32)
        o_ref[:, nl * mw:nl * mw + kw] = jnp.dot(sc, kw_ref[...].astype(BF16), preferred_element_type=F32)

    return pl.pallas_call(
        body, name="mod_fwd",
        out_shape=(jax.ShapeDtypeStruct((c_all.shape[0], nl * mw + kw), F32),
                   jax.ShapeDtypeStruct(c_all.shape, BF16)),
        compiler_params=_params(VMEM_BIG),
    )(c_all, mod_w, kv_mod_w)


def _vec_prep(modrow, modb, kvrow, kvb, ng, kvg):
    d = ng.shape[1]

    def body(mr_ref, mb_ref, kr_ref, kb_ref, ng_ref, kvg_ref, t_ref, m_ref):
        t_ref[...] = jnp.zeros_like(t_ref)
        m_ref[...] = jnp.zeros_like(m_ref)
        for l in range(2):
            mod = mr_ref[l] + mb_ref[l]
            m_ref[6 * l:6 * l + 6, :] = mod
            g = ng_ref[4 * l:4 * l + 4, :]
            t_ref[6 * l + R_W1:6 * l + R_W1 + 1, :] = g[0:1] * (1.0 + mod[1:2])
            t_ref[6 * l + R_SH1:6 * l + R_SH1 + 1, :] = mod[0:1]
            t_ref[6 * l + R_P1:6 * l + R_P1 + 1, :] = mod[2:3] * g[1:2]
            t_ref[6 * l + R_W2:6 * l + R_W2 + 1, :] = g[2:3] * (1.0 + mod[4:5])
            t_ref[6 * l + R_SH2:6 * l + R_SH2 + 1, :] = mod[3:4]
            t_ref[6 * l + R_P2:6 * l + R_P2 + 1, :] = mod[5:6] * g[3:4]
        kv = kr_ref[...] + kb_ref[...]
        m_ref[R_KV:R_KV + 2, :] = kv
        t_ref[R_KV:R_KV + 1, :] = kvg_ref[...] * (1.0 + kv[1:2])
        t_ref[R_KV + 1:R_KV + 2, :] = kv[0:1]

    return pl.pallas_call(
        body, name="vec_prep",
        out_shape=(jax.ShapeDtypeStruct((16, d), F32), jax.ShapeDtypeStruct((16, d), F32)),
    )(modrow, modb, kvrow, kvb, ng, kvg)


def _vec_bwd(sums_c, sums_f0, sums_q, sums_o, sums_f1, mt, ng, kvg):
    d = ng.shape[1]

    def body(sc_ref, sf0_ref, sq_ref, so_ref, sf1_ref, m_ref, ng_ref, kvg_ref, dm_ref, dng_ref, dkvg_ref, g_ref):
        g_ref[...] = jnp.zeros_like(g_ref)
        g_ref[0:3, :] = sc_ref[0:3, :]
        g_ref[3:6, :] = sf0_ref[3:6, :]
        g_ref[6:8, :] = sq_ref[0:2, :]
        g_ref[8:9, :] = so_ref[2:3, :]
        g_ref[9:12, :] = sf1_ref[3:6, :]
        g_ref[R_KV:R_KV + 2, :] = sq_ref[2:4, :]
        dm_ref[...] = jnp.zeros_like(dm_ref)
        dkvg_ref[...] = jnp.zeros_like(dkvg_ref)
        for l in range(2):
            g = ng_ref[4 * l:4 * l + 4, :]
            mod = m_ref[6 * l:6 * l + 6, :]
            s = g_ref[6 * l:6 * l + 6, :]
            dm_ref[6 * l + 0:6 * l + 1, :] = s[1:2]
            dm_ref[6 * l + 1:6 * l + 2, :] = s[0:1] * g[0:1]
            dm_ref[6 * l + 2:6 * l + 3, :] = s[2:3] * g[1:2]
            dm_ref[6 * l + 3:6 * l + 4, :] = s[4:5]
            dm_ref[6 * l + 4:6 * l + 5, :] = s[3:4] * g[2:3]
            dm_ref[6 * l + 5:6 * l + 6, :] = s[5:6] * g[3:4]
            dng_ref[4 * l + 0:4 * l + 1, :] = s[0:1] * (1.0 + mod[1:2])
            dng_ref[4 * l + 1:4 * l + 2, :] = s[2:3] * mod[2:3]
            dng_ref[4 * l + 2:4 * l + 3, :] = s[3:4] * (1.0 + mod[4:5])
            dng_ref[4 * l + 3:4 * l + 4, :] = s[5:6] * mod[5:6]
        dm_ref[R_KV:R_KV + 1, :] = g_ref[R_KV + 1:R_KV + 2, :]
        dm_ref[R_KV + 1:R_KV + 2, :] = g_ref[R_KV:R_KV + 1, :] * kvg_ref[...]
        dkvg_ref[0:1, :] = g_ref[R_KV:R_KV + 1, :] * (1.0 + m_ref[R_KV + 1:R_KV + 2, :])

    return pl.pallas_call(
        body, name="vec_bwd",
        out_shape=(jax.ShapeDtypeStruct((16, d), F32), jax.ShapeDtypeStruct((8, d), F32),
                   jax.ShapeDtypeStruct((8, d), F32)),
        scratch_shapes=[pltpu.VMEM((16, d), F32)],
    )(sums_c, sums_f0, sums_q, sums_o, sums_f1, mt, ng, kvg)


def _rel_index(nrel):
    width = KW + QB
    e = lax.broadcasted_iota(jnp.int32, (nrel, width), 1)
    r = lax.broadcasted_iota(jnp.int32, (nrel, width), 0)
    max_rel = (nrel - 1) // 2
    idx = jnp.clip(KW - e, -max_rel, max_rel) + max_rel
    return (idx == r).astype(F32)


def _band_valid():
    row = lax.broadcasted_iota(jnp.int32, (QB, KW), 0) // CHUNK
    col = lax.broadcasted_iota(jnp.int32, (QB, KW), 1) // CHUNK
    j = col - row
    return (j >= 0) & (j <= N_LEFT)


def _bias_fwd(rel_bias):
    nh, nrel = rel_bias.shape
    width = KW + QB

    def body(rb_ref, o_ref):
        onehot = _rel_index(nrel)
        gr = jnp.dot(rb_ref[...], onehot, preferred_element_type=F32, precision=lax.Precision.HIGHEST)
        valid = _band_valid() & _key_valid(pl.program_id(0))
        for h in range(nh):
            xrow = jnp.broadcast_to(gr[h:h + 1, :], (QB, width))
            rolled = pltpu.roll(xrow, 0, 1, stride=1, stride_axis=0)
            o_ref[h] = jnp.where(valid, rolled[:, QB:], NEG)

    return pl.pallas_call(
        body, name="bias_fwd", grid=(BIAS_VARIANTS,),
        in_specs=[pl.BlockSpec(rel_bias.shape, lambda v: (0, 0))],
        out_specs=pl.BlockSpec((None, nh, QB, KW), lambda v: (v, 0, 0, 0)),
        out_shape=jax.ShapeDtypeStruct((BIAS_VARIANTS, nh, QB, KW), F32),
        compiler_params=_params(VMEM_BIG),
    )(rel_bias)


def _bias_bwd(dbias, nrel):
    nh = dbias.shape[0]
    width = KW + QB

    def body(db_ref, o_ref, diag_ref):
        onehot = _rel_index(nrel)
        valid = _band_valid()
        rr = lax.broadcasted_iota(jnp.int32, (QB, QB), 0)
        cc = lax.broadcasted_iota(jnp.int32, (QB, QB), 1)
        flip = (rr + cc == QB - 1).astype(F32)
        for h in range(nh):
            rev = jnp.dot(flip, jnp.where(valid, db_ref[h], 0.0), preferred_element_type=F32,
                          precision=lax.Precision.HIGHEST)
            w = jnp.concatenate([jnp.zeros((QB, QB), F32), rev], axis=1)
            back = pltpu.roll(w, width - (QB - 1), 1, stride=1, stride_axis=0)
            diag_ref[h:h + 1, :] = _colsum(back)
        o_ref[...] = lax.dot_general(diag_ref[...], onehot, NT, preferred_element_type=F32,
                                     precision=lax.Precision.HIGHEST)

    return pl.pallas_call(
        body, name="bias_bwd",
        out_shape=jax.ShapeDtypeStruct((nh, nrel), F32),
        scratch_shapes=[pltpu.VMEM((nh, width), F32)],
        compiler_params=_params(VMEM_BIG),
    )(dbias)


def _conv_fwd(x, tab, ck, wci, wco, tm):
    s, d = x.shape

    def body(x_ref, t_ref, ck_ref, wci_ref, wco_ref, x1_ref, h_ref, bcx_ref, u_ref, y_ref, carry):
        @pl.when(pl.program_id(0) == 0)
        def _():
            carry[...] = jnp.zeros_like(carry)

        xv = x_ref[...]
        hb = ((xv * _rs(xv)) * t_ref[R_W1:R_W1 + 1, :] + t_ref[R_SH1:R_SH1 + 1, :]).astype(BF16)
        h_ref[...] = hb
        for j in range(3 * d // MXU):
            bcx_ref[:, j * MXU:(j + 1) * MXU] = lax.dot_general(hb, wci_ref[j * MXU:(j + 1) * MXU, :], NT,
                                                                preferred_element_type=F32)
        bg, cg, xi = bcx_ref[:, 0:d], bcx_ref[:, d:2 * d], bcx_ref[:, 2 * d:3 * d]
        z = cg * xi
        row = lax.broadcasted_iota(jnp.int32, z.shape, 0)
        c1, c2 = carry[7:8, :], carry[6:7, :]
        z1 = jnp.where(row == 0, c1, pltpu.roll(z, 1, 0))
        z2 = jnp.where(row == 0, c2, jnp.where(row == 1, c1, pltpu.roll(z, 2, 0)))
        carry[...] = z[tm - 8:tm, :]
        conv = ck_ref[0:1, :] * z2 + ck_ref[1:2, :] * z1 + ck_ref[2:3, :] * z
        ub = (bg * conv).astype(BF16)
        u_ref[...] = ub
        yv = jnp.dot(ub, wco_ref[...], preferred_element_type=F32)
        y_ref[...] = yv
        x1_ref[...] = xv + (yv * _rs(yv)) * t_ref[R_P1:R_P1 + 1, :]

    return pl.pallas_call(
        body, name="conv_fwd", grid=(s // tm,),
        in_specs=[_rows(tm, d), _const(tab.shape), _const(ck.shape), _const(wci.shape), _const(wco.shape)],
        out_specs=(_rows(tm, d), _rows(tm, d), _rows(tm, 3 * d), _rows(tm, d), _rows(tm, d)),
        out_shape=(jax.ShapeDtypeStruct((s, d), F32), jax.ShapeDtypeStruct((s, d), BF16),
                   jax.ShapeDtypeStruct((s, 3 * d), F32), jax.ShapeDtypeStruct((s, d), BF16),
                   jax.ShapeDtypeStruct((s, d), F32)),
        scratch_shapes=[pltpu.VMEM((8, d), F32)],
        compiler_params=_params(VMEM_BIG),
    )(x, tab, ck, wci, wco)


def _ffn_fwd(x, tab, base, wfi, wfo, tgt, tm, name):
    s, d = x.shape
    hid = wfo.shape[0]
    nblk = hid // MXU
    with_loss = tgt is not None

    def body(*refs):
        if with_loss:
            x_ref, t_ref, wfi_ref, wfo_ref, tgt_ref, xo_ref, h_ref, gu_ref, y_ref, loss_ref, a_scr = refs
        else:
            x_ref, t_ref, wfi_ref, wfo_ref, xo_ref, h_ref, gu_ref, y_ref, a_scr = refs
        xv = x_ref[...]
        hb = ((xv * _rs(xv)) * t_ref[base + R_W2:base + R_W2 + 1, :]
              + t_ref[base + R_SH2:base + R_SH2 + 1, :]).astype(BF16)
        h_ref[...] = hb
        acc = jnp.zeros((tm, d), F32)
        for c0 in range(0, nblk, FFN_CHUNK):
            for j in range(c0, min(c0 + FFN_CHUNK, nblk)):
                lo, hi = j * MXU, (j + 1) * MXU
                g = lax.dot_general(hb, wfi_ref[lo:hi, :], NT, preferred_element_type=F32)
                u = lax.dot_general(hb, wfi_ref[hid + lo:hid + hi, :], NT, preferred_element_type=F32)
                gu_ref[:, lo:hi] = g.astype(BF16)
                gu_ref[:, hid + lo:hid + hi] = u.astype(BF16)
                a_scr[:, lo:hi] = ((g * _sigmoid(g)) * u).astype(BF16)
            lo, hi = c0 * MXU, min(c0 + FFN_CHUNK, nblk) * MXU
            acc = acc + jnp.dot(a_scr[:, lo:hi], wfo_ref[lo:hi, :], preferred_element_type=F32)
        y_ref[...] = acc
        xo = xv + (acc * _rs(acc)) * t_ref[base + R_P2:base + R_P2 + 1, :]
        if with_loss:
            @pl.when(pl.program_id(0) == 0)
            def _():
                loss_ref[...] = jnp.zeros_like(loss_ref)

            err = xo - tgt_ref[...]
            xo_ref[...] = err * (1.0 / d)
            e2 = jnp.sum((err * err).reshape(tm // 8, 8, d), axis=0)
            for q in range(d // LANES):
                loss_ref[...] += e2[:, q * LANES:(q + 1) * LANES]
        else:
            xo_ref[...] = xo

    in_specs = [_rows(tm, d), _const(tab.shape), _const(wfi.shape), _const(wfo.shape)]
    args = [x, tab, wfi, wfo]
    out_specs = [_rows(tm, d), _rows(tm, d), _rows(tm, 2 * hid), _rows(tm, d)]
    out_shape = [jax.ShapeDtypeStruct((s, d), F32), jax.ShapeDtypeStruct((s, d), BF16),
                 jax.ShapeDtypeStruct((s, 2 * hid), BF16), jax.ShapeDtypeStruct((s, d), F32)]
    if with_loss:
        in_specs.append(_rows(tm, d))
        args.append(tgt)
        out_specs.append(pl.BlockSpec((8, LANES), lambda i: (0, 0)))
        out_shape.append(jax.ShapeDtypeStruct((8, LANES), F32))
    return pl.pallas_call(
        body, name=name, grid=(s // tm,), in_specs=in_specs, out_specs=tuple(out_specs),
        out_shape=tuple(out_shape), scratch_shapes=[pltpu.VMEM((tm, hid), BF16)],
        compiler_params=_params(VMEM_BIG),
    )(*args)


def _qkv_fwd(x, tab, wq, wkv, tm):
    s, d = x.shape
    base = 6

    def body(x_ref, t_ref, wq_ref, wkv_ref, hkv_ref, h1_ref, q_ref, k_ref, v_ref):
        xv = x_ref[...]
        n = xv * _rs(xv)
        hkv = (n * t_ref[R_KV:R_KV + 1, :] + t_ref[R_KV + 1:R_KV + 2, :]).astype(BF16)
        h1 = (n * t_ref[base + R_W1:base + R_W1 + 1, :] + t_ref[base + R_SH1:base + R_SH1 + 1, :]).astype(BF16)
        hkv_ref[...] = hkv
        h1_ref[...] = h1
        q_ref[...] = (jnp.dot(h1, wq_ref[...], preferred_element_type=F32) * (HEAD_DIM ** -0.5)).astype(BF16)
        for j in range(d // MXU):
            lo, hi = j * MXU, (j + 1) * MXU
            k_ref[:, lo:hi] = lax.dot_general(hkv, wkv_ref[lo:hi, :], NT, preferred_element_type=F32).astype(BF16)
            v_ref[:, lo:hi] = lax.dot_general(hkv, wkv_ref[d + lo:d + hi, :], NT,
                                              preferred_element_type=F32).astype(BF16)

    act = jax.ShapeDtypeStruct((s, d), BF16)
    return pl.pallas_call(
        body, name="qkv_fwd", grid=(s // tm,),
        in_specs=[_rows(tm, d), _const(tab.shape), _const(wq.shape), _const(wkv.shape)],
        out_specs=tuple([_rows(tm, d)] * 5), out_shape=(act,) * 5,
        compiler_params=_params(VMEM_BIG),
    )(x, tab, wq, wkv)


def _window_specs(per=1):
    return [pl.BlockSpec((QB, LANES), (lambda p, b, w=w: (jnp.maximum(per * b - 2 + w, 0), p)))
            for w in range(2 + per)]


def _key_valid(b):
    col = lax.broadcasted_iota(jnp.int32, (QB, KW), 1) // CHUNK
    return (b * (QB // CHUNK) - N_LEFT + col) >= 0


def _bias_spec(per=1, sub=0):
    return pl.BlockSpec((None, LANES // HEAD_DIM, QB, KW),
                        lambda p, b: (jnp.minimum(per * b + sub, BIAS_VARIANTS - 1), p, 0, 0))


def _head_masks():
    lane = lax.broadcasted_iota(jnp.int32, (1, LANES), 1)
    return [(lane // HEAD_DIM == hh) for hh in range(LANES // HEAD_DIM)]


def _attn_fwd(q, k, v, bias):
    s, d = q.shape
    per = ATTN_PER
    npair, nb = d // LANES, s // (per * QB)
    hpp = LANES // HEAD_DIM
    nwin = 2 + per
    nbias = min(per, BIAS_VARIANTS)

    def body(*refs):
        q_ref, k_refs, v_refs = refs[0], refs[1:1 + nwin], refs[1 + nwin:1 + 2 * nwin]
        bias_refs = refs[1 + 2 * nwin:1 + 2 * nwin + nbias]
        o_ref, lse_ref = refs[1 + 2 * nwin + nbias:]
        ks = [r[...] for r in k_refs]
        vs = [r[...] for r in v_refs]
        masks = _head_masks()
        for sub in range(per):
            rows = slice(sub * QB, (sub + 1) * QB)
            qv = q_ref[rows, :]
            kwin = jnp.concatenate(ks[sub:sub + 3], axis=0)
            vwin = jnp.concatenate(vs[sub:sub + 3], axis=0)
            o = jnp.zeros((QB, LANES), F32)
            lse = jnp.zeros((QB, LANES), F32)
            scs = [lax.dot_general(jnp.where(masks[hh], qv, jnp.zeros_like(qv)), kwin, NT,
                                   preferred_element_type=F32) + bias_refs[min(sub, nbias - 1)][hh]
                   for hh in range(hpp)]
            for hh in range(hpp):
                vm = jnp.where(masks[hh], vwin, jnp.zeros_like(vwin))
                sc = scs[hh]
                m = jnp.max(sc, axis=-1, keepdims=True)
                p = jnp.exp(sc - m)
                l = jnp.sum(p, axis=-1, keepdims=True)
                o = o + jnp.dot(p.astype(BF16), vm, preferred_element_type=F32) * (1.0 / l)
                lse = jnp.where(masks[hh], m + jnp.log(l), lse)
            o_ref[rows, :] = o.astype(BF16)
            lse_ref[rows, :] = lse

    blk = pl.BlockSpec((per * QB, LANES), lambda p, b: (b, p))
    return pl.pallas_call(
        body, name="attn_fwd", grid=(npair, nb),
        in_specs=[blk] + _window_specs(per) + _window_specs(per) + [_bias_spec(per, sub) for sub in range(nbias)],
        out_specs=(blk, blk),
        out_shape=(jax.ShapeDtypeStruct((s, d), BF16), jax.ShapeDtypeStruct((s, d), F32)),
        compiler_params=_params(VMEM_BIG),
    )(q, *([k] * nwin), *([v] * nwin), *([bias] * nbias))


def _attn_out_fwd(o, x, tab, wo, tm):
    s, d = x.shape
    base = 6

    def body(o_ref, x_ref, t_ref, wo_ref, x3_ref, y_ref):
        yv = jnp.dot(o_ref[...], wo_ref[...], preferred_element_type=F32)
        y_ref[...] = yv
        x3_ref[...] = x_ref[...] + (yv * _rs(yv)) * t_ref[base + R_P1:base + R_P1 + 1, :]

    return pl.pallas_call(
        body, name="attn_out_fwd", grid=(s // tm,),
        in_specs=[_rows(tm, d), _rows(tm, d), _const(tab.shape), _const(wo.shape)],
        out_specs=(_rows(tm, d), _rows(tm, d)),
        out_shape=(jax.ShapeDtypeStruct((s, d), F32), jax.ShapeDtypeStruct((s, d), F32)),
        compiler_params=_params(VMEM_BIG),
    )(o, x, tab, wo)


def _ffn_bwd(dxo, x, y, gu, tab, base, wfi, wfo, tm, name):
    s, d = x.shape
    hid = wfo.shape[0]
    nblk = hid // MXU

    def body(dxo_ref, x_ref, y_ref, gu_ref, t_ref, wfi_ref, wfo_ref, dx_ref, dyb_ref, dgu_ref, a_ref, sums_ref):
        @pl.when(pl.program_id(0) == 0)
        def _():
            sums_ref[...] = jnp.zeros_like(sums_ref)

        dxo_v = dxo_ref[...]
        yv = y_ref[...]
        ry = _rs(yv)
        ny = yv * ry
        sums_ref[R_P2:R_P2 + 1, :] += _colsum(dxo_v * ny)
        dyb = _norm_bwd(dxo_v * t_ref[base + R_P2:base + R_P2 + 1, :], ny, ry).astype(BF16)
        dyb_ref[...] = dyb
        dh = jnp.zeros((tm, d), F32)
        for c0 in range(0, nblk, FFN_CHUNK):
            for j in range(c0, min(c0 + FFN_CHUNK, nblk)):
                lo, hi = j * MXU, (j + 1) * MXU
                da = lax.dot_general(dyb, wfo_ref[lo:hi, :], NT, preferred_element_type=F32)
                g, u = gu_ref[:, lo:hi].astype(F32), gu_ref[:, hid + lo:hid + hi].astype(F32)
                sg = _sigmoid(g)
                gs = g * sg
                a_ref[:, lo:hi] = (gs * u).astype(BF16)
                dgu_ref[:, lo:hi] = (da * u * sg * (1.0 + g * (1.0 - sg))).astype(BF16)
                dgu_ref[:, hid + lo:hid + hi] = (da * gs).astype(BF16)
            lo, hi = c0 * MXU, min(c0 + FFN_CHUNK, nblk) * MXU
            dh = dh + jnp.dot(dgu_ref[:, lo:hi], wfi_ref[lo:hi, :], preferred_element_type=F32)
            dh = dh + jnp.dot(dgu_ref[:, hid + lo:hid + hi], wfi_ref[hid + lo:hid + hi, :],
                              preferred_element_type=F32)
        xv = x_ref[...]
        r = _rs(xv)
        n = xv * r
        sums_ref[R_SH2:R_SH2 + 1, :] += _colsum(dh)
        sums_ref[R_W2:R_W2 + 1, :] += _colsum(dh * n)
        dx_ref[...] = dxo_v + _norm_bwd(dh * t_ref[base + R_W2:base + R_W2 + 1, :], n, r)

    return pl.pallas_call(
        body, name=name, grid=(s // tm,),
        in_specs=[_rows(tm, d), _rows(tm, d), _rows(tm, d), _rows(tm, 2 * hid),
                  _const(tab.shape), _const(wfi.shape), _const(wfo.shape)],
        out_specs=(_rows(tm, d), _rows(tm, d), _rows(tm, 2 * hid), _rows(tm, hid),
                   pl.BlockSpec((8, d), lambda i: (0, 0))),
        out_shape=(jax.ShapeDtypeStruct((s, d), F32), jax.ShapeDtypeStruct((s, d), BF16),
                   jax.ShapeDtypeStruct((s, 2 * hid), BF16), jax.ShapeDtypeStruct((s, hid), BF16),
                   jax.ShapeDtypeStruct((8, d), F32)),
        compiler_params=_params(VMEM_BIG),
    )(dxo, x, y, gu, tab, wfi, wfo)


def _attn_out_bwd(dx, y, tab, wo, tm):
    s, d = y.shape
    base = 6

    def body(dx_ref, y_ref, t_ref, wo_ref, dyb_ref, do_ref, sums_ref):
        @pl.when(pl.program_id(0) == 0)
        def _():
            sums_ref[...] = jnp.zeros_like(sums_ref)

        dxv = dx_ref[...]
        yv = y_ref[...]
        ry = _rs(yv)
        ny = yv * ry
        sums_ref[R_P1:R_P1 + 1, :] += _colsum(dxv * ny)
        dyb = _norm_bwd(dxv * t_ref[base + R_P1:base + R_P1 + 1, :], ny, ry).astype(BF16)
        dyb_ref[...] = dyb
        do_ref[...] = lax.dot_general(dyb, wo_ref[...], NT, preferred_element_type=F32).astype(BF16)

    return pl.pallas_call(
        body, name="attn_out_bwd", grid=(s // tm,),
        in_specs=[_rows(tm, d), _rows(tm, d), _const(tab.shape), _const(wo.shape)],
        out_specs=(_rows(tm, d), _rows(tm, d), pl.BlockSpec((8, d), lambda i: (0, 0))),
        out_shape=(jax.ShapeDtypeStruct((s, d), BF16), jax.ShapeDtypeStruct((s, d), BF16),
                   jax.ShapeDtypeStruct((8, d), F32)),
        compiler_params=_params(VMEM_BIG),
    )(dx, y, tab, wo)


def _attn_bwd(q, k, v, o, do, lse, bias):
    s, d = q.shape
    per = ATTN_PER
    npair, nb = d // LANES, s // (per * QB)
    hpp = LANES // HEAD_DIM
    nwin = 2 + per
    nbias = min(per, BIAS_VARIANTS)

    def body(*refs):
        q_ref, k_refs, v_refs = refs[0], refs[1:1 + nwin], refs[1 + nwin:1 + 2 * nwin]
        o_ref, do_ref, lse_ref = refs[1 + 2 * nwin:4 + 2 * nwin]
        bias_refs = refs[4 + 2 * nwin:4 + 2 * nwin + nbias]
        dq_ref, dk_ref, dv_ref, db_ref = refs[4 + 2 * nwin + nbias:]
        b = pl.program_id(1)

        @pl.when(b == 0)
        def _():
            dk_ref[...] = jnp.zeros_like(dk_ref)
            dv_ref[...] = jnp.zeros_like(dv_ref)
            db_ref[...] = jnp.zeros_like(db_ref)

        ks = [r[...] for r in k_refs]
        vs = [r[...] for r in v_refs]
        masks = _head_masks()
        for sub in range(per):
            rows = slice(sub * QB, (sub + 1) * QB)
            qv = q_ref[rows, :]
            dov = do_ref[rows, :]
            lsev = lse_ref[rows, :]
            doo = dov.astype(F32) * o_ref[rows, :].astype(F32)
            kwin = jnp.concatenate(ks[sub:sub + 3], axis=0)
            vwin = jnp.concatenate(vs[sub:sub + 3], axis=0)
            dq = jnp.zeros((QB, LANES), F32)
            dkw = jnp.zeros((KW, LANES), F32)
            dvw = jnp.zeros((KW, LANES), F32)
            for hh in range(hpp):
                qm = jnp.where(masks[hh], qv, jnp.zeros_like(qv))
                dom = jnp.where(masks[hh], dov, jnp.zeros_like(dov))
                km = jnp.where(masks[hh], kwin, jnp.zeros_like(kwin))
                lse_h = jnp.max(jnp.where(masks[hh], lsev, NEG), axis=-1, keepdims=True)
                delta = jnp.sum(jnp.where(masks[hh], doo, 0.0), axis=-1, keepdims=True)
                sc = lax.dot_general(qm, kwin, NT, preferred_element_type=F32) + bias_refs[min(sub, nbias - 1)][hh]
                p = jnp.exp(sc - lse_h)
                dp = lax.dot_general(dom, vwin, NT, preferred_element_type=F32)
                ds = p * (dp - delta)
                db_ref[hh] += ds
                dsb = ds.astype(BF16)
                dq = dq + jnp.dot(dsb, km, preferred_element_type=F32)
                dkw = dkw + lax.dot_general(dsb, qm, TN, preferred_element_type=F32)
                dvw = dvw + lax.dot_general(p.astype(BF16), dom, TN, preferred_element_type=F32)
            dq_ref[rows, :] = (dq * (HEAD_DIM ** -0.5)).astype(BF16)
            for w in range(3):
                start = pl.multiple_of(jnp.maximum(per * b + sub - 2 + w, 0) * QB, QB)
                dk_ref[pl.ds(start, QB), :] += dkw[w * QB:(w + 1) * QB, :]
                dv_ref[pl.ds(start, QB), :] += dvw[w * QB:(w + 1) * QB, :]

    blk = pl.BlockSpec((per * QB, LANES), lambda p, b: (b, p))
    col = pl.BlockSpec((s, LANES), lambda p, b: (0, p))
    pair = pl.BlockSpec((hpp, QB, KW), lambda p, b: (p, 0, 0))
    return pl.pallas_call(
        body, name="attn_bwd", grid=(npair, nb),
        in_specs=[blk] + _window_specs(per) + _window_specs(per) + [blk, blk, blk]
                 + [_bias_spec(per, sub) for sub in range(nbias)],
        out_specs=(blk, col, col, pair),
        out_shape=(jax.ShapeDtypeStruct((s, d), BF16), jax.ShapeDtypeStruct((s, d), F32),
                   jax.ShapeDtypeStruct((s, d), F32), jax.ShapeDtypeStruct(bias.shape[1:], F32)),
        compiler_params=_params(VMEM_BIG),
    )(q, *([k] * nwin), *([v] * nwin), o, do, lse, *([bias] * nbias))


def _qkv_bwd(dres, dq, dk, dv, x, tab, wq, wkv, tm):
    s, d = x.shape
    base = 6

    def body(dres_ref, dq_ref, dk_ref, dv_ref, x_ref, t_ref, wq_ref, wkv_ref, dx_ref, dkv_ref, sums_ref):
        @pl.when(pl.program_id(0) == 0)
        def _():
            sums_ref[...] = jnp.zeros_like(sums_ref)

        dh1 = lax.dot_general(dq_ref[...], wq_ref[...], NT, preferred_element_type=F32)
        dkv_ref[:, 0:d] = dk_ref[...].astype(BF16)
        dkv_ref[:, d:2 * d] = dv_ref[...].astype(BF16)
        dhkv = jnp.dot(dkv_ref[...], wkv_ref[...], preferred_element_type=F32)
        xv = x_ref[...]
        r = _rs(xv)
        n = xv * r
        sums_ref[0:1, :] += _colsum(dh1 * n)
        sums_ref[1:2, :] += _colsum(dh1)
        sums_ref[2:3, :] += _colsum(dhkv * n)
        sums_ref[3:4, :] += _colsum(dhkv)
        dn = dh1 * t_ref[base + R_W1:base + R_W1 + 1, :] + dhkv * t_ref[R_KV:R_KV + 1, :]
        dx_ref[...] = dres_ref[...] + _norm_bwd(dn, n, r)

    return pl.pallas_call(
        body, name="qkv_bwd", grid=(s // tm,),
        in_specs=[_rows(tm, d)] * 5 + [_const(tab.shape), _const(wq.shape), _const(wkv.shape)],
        out_specs=(_rows(tm, d), _rows(tm, 2 * d), pl.BlockSpec((8, d), lambda i: (0, 0))),
        out_shape=(jax.ShapeDtypeStruct((s, d), F32), jax.ShapeDtypeStruct((s, 2 * d), BF16),
                   jax.ShapeDtypeStruct((8, d), F32)),
        compiler_params=_params(VMEM_BIG),
    )(dres, dq, dk, dv, x, tab, wq, wkv)


def _conv_bwd(dx1, x, y, bcx, tab, ck, wci, wco, tm):
    s, d = x.shape
    nt = s // tm

    def rev(i):
        return (nt - 1 - i, 0)

    def halo(i):
        return (jnp.maximum((nt - 1 - i) * (tm // 8) - 1, 0), 0)

    def body(dx_ref, x_ref, y_ref, bcx_ref, halo_ref, t_ref, ck_ref, wci_ref, wco_ref,
             dx0_ref, dyb_ref, dbcx_ref, sums_ref, dck_ref, carry):
        i = pl.program_id(0)

        @pl.when(i == 0)
        def _():
            sums_ref[...] = jnp.zeros_like(sums_ref)
            dck_ref[...] = jnp.zeros_like(dck_ref)
            carry[...] = jnp.zeros_like(carry)

        dxv = dx_ref[...]
        yv = y_ref[...]
        ry = _rs(yv)
        ny = yv * ry
        sums_ref[R_P1:R_P1 + 1, :] += _colsum(dxv * ny)
        dyb = _norm_bwd(dxv * t_ref[R_P1:R_P1 + 1, :], ny, ry).astype(BF16)
        dyb_ref[...] = dyb
        du = lax.dot_general(dyb, wco_ref[...], NT, preferred_element_type=F32)
        bg, cg, xi = bcx_ref[:, 0:d], bcx_ref[:, d:2 * d], bcx_ref[:, 2 * d:3 * d]
        z = cg * xi
        zp = halo_ref[:, d:2 * d] * halo_ref[:, 2 * d:3 * d]
        zp = jnp.where(i == nt - 1, jnp.zeros_like(zp), zp)
        row = lax.broadcasted_iota(jnp.int32, z.shape, 0)
        c1, c2 = zp[7:8, :], zp[6:7, :]
        z1 = jnp.where(row == 0, c1, pltpu.roll(z, 1, 0))
        z2 = jnp.where(row == 0, c2, jnp.where(row == 1, c1, pltpu.roll(z, 2, 0)))
        k0, k1, k2 = ck_ref[0:1, :], ck_ref[1:2, :], ck_ref[2:3, :]
        conv = k0 * z2 + k1 * z1 + k2 * z
        dconv = du * bg
        dck_ref[0:1, :] += _colsum(dconv * z2)
        dck_ref[1:2, :] += _colsum(dconv * z1)
        dck_ref[2:3, :] += _colsum(dconv * z)
        n1, n2 = carry[0:1, :], carry[1:2, :]
        d1 = jnp.where(row == tm - 1, n1, pltpu.roll(dconv, tm - 1, 0))
        d2 = jnp.where(row == tm - 1, n2, jnp.where(row == tm - 2, n1, pltpu.roll(dconv, tm - 2, 0)))
        carry[...] = dconv[0:8, :]
        dz = k2 * dconv + k1 * d1 + k0 * d2
        dbcx_ref[:, 0:d] = (du * conv).astype(BF16)
        dbcx_ref[:, d:2 * d] = (dz * xi).astype(BF16)
        dbcx_ref[:, 2 * d:3 * d] = (dz * cg).astype(BF16)
        dh = jnp.dot(dbcx_ref[...], wci_ref[...], preferred_element_type=F32)
        xv = x_ref[...]
        r = _rs(xv)
        n = xv * r
        sums_ref[R_W1:R_W1 + 1, :] += _colsum(dh * n)
        sums_ref[R_SH1:R_SH1 + 1, :] += _colsum(dh)
        dx0_ref[...] = dxv + _norm_bwd(dh * t_ref[R_W1:R_W1 + 1, :], n, r)

    rrow = lambda cols: pl.BlockSpec((tm, cols), rev)
    acc = pl.BlockSpec((8, d), lambda i: (0, 0))
    return pl.pallas_call(
        body, name="conv_bwd", grid=(nt,),
        in_specs=[rrow(d), rrow(d), rrow(d), rrow(3 * d), pl.BlockSpec((8, 3 * d), halo),
                  _const(tab.shape), _const(ck.shape), _const(wci.shape), _const(wco.shape)],
        out_specs=(rrow(d), rrow(d), rrow(3 * d), acc, acc),
        out_shape=(jax.ShapeDtypeStruct((s, d), F32), jax.ShapeDtypeStruct((s, d), BF16),
                   jax.ShapeDtypeStruct((s, 3 * d), BF16), jax.ShapeDtypeStruct((8, d), F32),
                   jax.ShapeDtypeStruct((8, d), F32)),
        scratch_shapes=[pltpu.VMEM((8, d), F32)],
        compiler_params=_params(VMEM_BIG),
    )(dx1, x, y, bcx, bcx, tab, ck, wci, wco)


def _wgrad_wide(a, b, nblk, tk, name):
    s, m = a.shape
    n = b.shape[1] // nblk
    nk = s // tk

    def body(a_ref, b_ref, o_ref, acc):
        kk = pl.program_id(0)

        @pl.when(kk == 0)
        def _():
            acc[...] = jnp.zeros_like(acc)

        acc[...] += jnp.dot(a_ref[...].T, b_ref[...], preferred_element_type=F32)

        @pl.when(kk == nk - 1)
        def _():
            for j in range(nblk):
                o_ref[j] = acc[:, j * n:(j + 1) * n].astype(BF16)

    return pl.pallas_call(
        body, name=name, grid=(nk,),
        in_specs=[pl.BlockSpec((tk, m), lambda kk: (kk, 0)), pl.BlockSpec((tk, nblk * n), lambda kk: (kk, 0))],
        out_specs=pl.BlockSpec((nblk, m, n), lambda kk: (0, 0, 0)),
        out_shape=jax.ShapeDtypeStruct((nblk, m, n), BF16),
        scratch_shapes=[pltpu.VMEM((m, nblk * n), F32)],
        compiler_params=_params(VMEM_BIG),
    )(a, b)


def _wgrad_rows(a, b, ncb, tk, name):
    s, m = a.shape
    n = b.shape[1]
    mb = m // ncb
    nk = s // tk

    def body(a_ref, b_ref, o_ref, acc):
        kk = pl.program_id(1)

        @pl.when(kk == 0)
        def _():
            acc[...] = jnp.zeros_like(acc)

        acc[...] += jnp.dot(a_ref[...].T, b_ref[...], preferred_element_type=F32)

        @pl.when(kk == nk - 1)
        def _():
            o_ref[...] = acc[...].astype(BF16)

    return pl.pallas_call(
        body, name=name, grid=(ncb, nk),
        in_specs=[pl.BlockSpec((tk, mb), lambda j, kk: (kk, j)), pl.BlockSpec((tk, n), lambda j, kk: (kk, 0))],
        out_specs=pl.BlockSpec((mb, n), lambda j, kk: (j, 0)),
        out_shape=jax.ShapeDtypeStruct((m, n), BF16),
        scratch_shapes=[pltpu.VMEM((mb, n), F32)],
        compiler_params=_params(VMEM_BIG),
    )(a, b)


def _adamw_math(w, g, m, v):
    m = ADAM_B1 * m + (1.0 - ADAM_B1) * g
    v = ADAM_B2 * v + (1.0 - ADAM_B2) * (g * g)
    m_hat = m / (1.0 - ADAM_B1 ** ADAM_STEP)
    v_hat = v / (1.0 - ADAM_B2 ** ADAM_STEP)
    delta = -ADAM_LR * (m_hat / (jnp.sqrt(v_hat) + ADAM_EPS) + ADAM_WD * w)
    return delta, m, v


def _adamw_reduce(parts, w, m, v, tr, name, parts_t=False):
    nl, r, c = w.shape
    tr = r if (parts_t and r % LANES) else (LANES if parts_t else _row_tile(r, tr))

    def body(*refs):
        p_refs = refs[:nl]
        w_ref, m_ref, v_ref, g_ref, d_ref, mo_ref, vo_ref = refs[nl:]
        layer = pl.program_id(0)

        def partial(i):
            val = p_refs[0][i].astype(F32)
            for q in range(1, nl):
                val = jnp.where(layer == q, p_refs[q][i].astype(F32), val)
            return val

        g = partial(0)
        for i in range(1, N_DEV):
            g = g + partial(i)
        if parts_t:
            g = g.T
        g_ref[...] = g
        d_ref[...], mo_ref[...], vo_ref[...] = _adamw_math(w_ref[...], g, m_ref[...], v_ref[...])

    blk = pl.BlockSpec((None, tr, c), lambda l, i: (l, i, 0))
    out = jax.ShapeDtypeStruct((nl, r, c), F32)
    if parts_t:
        p_specs = [pl.BlockSpec((N_DEV, c, tr), lambda l, i: (0, 0, i))]
    else:
        p_specs = [pl.BlockSpec((N_DEV, tr, c), (lambda l, i, q=q: (0, jnp.where(l == q, i, 0), 0)))
                   for q in range(nl)]
    return pl.pallas_call(
        body, name=name, grid=(nl, r // tr),
        in_specs=p_specs + [blk, blk, blk],
        out_specs=(blk,) * 4, out_shape=(out,) * 4,
        compiler_params=_params(VMEM_BIG),
    )(*parts, w, m, v)


def _adamw_outer(sct, dm, w, m, v, tr, name):
    nl, d, c = w.shape

    def body(s_ref, dm_ref, w_ref, m_ref, v_ref, g_ref, d_ref, mo_ref, vo_ref):
        g = jnp.dot(s_ref[...], dm_ref[...], preferred_element_type=F32)
        g_ref[...] = g
        d_ref[...], mo_ref[...], vo_ref[...] = _adamw_math(w_ref[...], g, m_ref[...], v_ref[...])

    blk = pl.BlockSpec((None, tr, c), lambda l, i: (l, i, 0))
    out = jax.ShapeDtypeStruct((nl, d, c), F32)
    return pl.pallas_call(
        body, name=name, grid=(nl, d // tr),
        in_specs=[pl.BlockSpec((tr, N_DEV), lambda l, i: (i, 0)),
                  pl.BlockSpec((None, N_DEV, c), lambda l, i: (l, 0, 0)), blk, blk, blk],
        out_specs=(blk,) * 4, out_shape=(out,) * 4,
        compiler_params=_params(VMEM_BIG),
    )(sct, dm, w, m, v)


def _pad_rows(a, rows):
    return jnp.concatenate([a, jnp.zeros((rows - a.shape[0],) + a.shape[1:], a.dtype)], axis=0)


def kernel(x, c, mod_w, mod_b, norm_g, ffn_w_in, ffn_w_out, conv_w_in, conv_k, conv_w_out, kv_mod_w, kv_mod_b, kv_norm_g, w_kv, attn_w_q, attn_w_o, rel_bias, loss_target, m_mod_w, m_mod_b, m_norm_g, m_ffn_w_in, m_ffn_w_out, m_conv_w_in, m_conv_k, m_conv_w_out, m_kv_mod_w, m_kv_mod_b, m_kv_norm_g, m_w_kv, m_attn_w_q, m_attn_w_o, m_rel_bias, v_mod_w, v_mod_b, v_norm_g, v_ffn_w_in, v_ffn_w_out, v_conv_w_in, v_conv_k, v_conv_w_out, v_kv_mod_w, v_kv_mod_b, v_kv_norm_g, v_w_kv, v_attn_w_q, v_attn_w_o, v_rel_bias):
    s, d = x.shape[1], x.shape[2]
    dq = d // LANES
    dsh = d // N_DEV
    nl = mod_w.shape[0]
    mw = mod_w.shape[2]
    kmw = kv_mod_w.shape[1]
    fw = ffn_w_in.shape[2]
    nh, nrel = rel_bias.shape[1], rel_bias.shape[2]
    tm = min(256, s)
    tm2 = min(512, s)
    tk = min(2048, s)
    me = 4 * lax.axis_index("x") + 2 * lax.axis_index("y") + lax.axis_index("c")

    x0 = x[0]
    tgt = loss_target[0]

    small1 = jnp.concatenate([c.reshape(dq, LANES), norm_g.reshape(dq, LANES),
                              _pad_rows(conv_k[0], 8).reshape(dq, LANES)], axis=0)
    (sm,) = _exchange([small1], ["gather"], "gather_small")
    c_all = sm[:, 0:dq].reshape(N_DEV, d)
    ng_full = jnp.transpose(sm[:, dq:2 * dq].reshape(N_DEV, 8, dsh), (1, 0, 2)).reshape(8, d)
    ck_full = jnp.transpose(sm[:, 2 * dq:3 * dq].reshape(N_DEV, 8, dsh), (1, 0, 2)).reshape(8, d)

    modcols, silu_c = _mod_fwd(c_all, mod_w, kv_mod_w)
    (modall,) = _exchange([modcols], ["gather"], "gather_mod")

    cast = lambda *ws: [a.astype(BF16) for a in ws]
    gath = lambda ws: (ws, ["gather"] * len(ws))
    half = lambda ws: (ws, ["gather_half"] * len(ws))
    (h_conv, h_ffn0, h_attn, h_ffn1), token = _xstart(
        [half(cast(conv_w_in[0].T, conv_w_out[0])), half(cast(jnp.swapaxes(ffn_w_in[0], 0, 1), ffn_w_out[0])),
         gath(cast(w_kv.T, attn_w_q[0], attn_w_o[0])), gath(cast(jnp.swapaxes(ffn_w_in[1], 0, 1), ffn_w_out[1]))],
        modall, "gather_start")
    modall = modall + token[0, 0]
    mine = lax.dynamic_index_in_dim(modall, me, axis=1, keepdims=False)
    modrow = jnp.stack([mine[:, l * mw:(l + 1) * mw].reshape(6, d) for l in range(nl)])
    kvrow = mine[:, nl * mw:nl * mw + kmw].reshape(2, d)
    tab, modval = _vec_prep(modrow, mod_b.reshape(nl, 6, d), kvrow, kv_mod_b.reshape(2, d), ng_full,
                            kv_norm_g.reshape(1, d))
    bias = _bias_fwd(rel_bias[0])

    wci, wco = _xwait(h_conv, [bias], "gather_wait_conv")
    wci, wco = _forward_to_sibling([wci, wco], "gather_forward_conv")
    wci, wco = wci.reshape(3 * d, d), wco.reshape(d, d)
    x1, h1a, bcx, ua, ya = _conv_fwd(x0, tab, ck_full, wci, wco, tm2)
    wfi0, wfo0 = _xwait(h_ffn0, [x1], "gather_wait_ffn0")
    wfi0, wfo0 = _forward_to_sibling([wfi0, wfo0], "gather_forward_ffn0")
    wfi0, wfo0 = wfi0.reshape(-1, d), wfo0.reshape(-1, d)
    x2, h2a, gua, y2a = _ffn_fwd(x1, tab, 0, wfi0, wfo0, None, tm2, "ffn_fwd0")
    wkv, wq, wo = _xwait(h_attn, [x2], "gather_wait_attn")
    wkv, wq, wo = wkv.reshape(2 * d, d), wq.reshape(d, d), wo.reshape(d, d)
    hkv, h1b, q, k, v = _qkv_fwd(x2, tab, wq, wkv, tm2)
    o, lse = _attn_fwd(q, k, v, bias)
    x3, yb = _attn_out_fwd(o, x2, tab, wo, tm2)
    wfi1, wfo1 = _xwait(h_ffn1, [x3], "gather_wait_ffn1")
    wfi1, wfo1 = wfi1.reshape(-1, d), wfo1.reshape(-1, d)
    dx4, h2b, gub, y2b, loss_acc = _ffn_fwd(x3, tab, 6, wfi1, wfo1, tgt, tm2, "ffn_fwd1")

    scat = lambda ws: [(ws, ["scatter"] * len(ws))]
    dx3, dy2b, dgub, ab, sums_f1 = _ffn_bwd(dx4, x3, y2b, gub, tab, 6, wfi1, wfo1, tm, "ffn_bwd1")
    g_wfi1 = _wgrad_rows(dgub, h2b, 4, tk, "wgrad_ffn_in1").reshape(N_DEV, -1, d)
    g_wfo1 = _wgrad_rows(ab, dy2b, 2, tk, "wgrad_ffn_out1").reshape(N_DEV, -1, d)
    (h_g1,), token = _xstart(scat([g_wfi1, g_wfo1]), dx3, "grads_start_ffn1")
    tab = tab + token[0, 0]
    dyb, do, sums_o = _attn_out_bwd(dx3, yb, tab, wo, tm2)
    g_wo = _wgrad_wide(o, dyb, 1, tk, "wgrad_o").reshape(N_DEV, dsh, d)
    dqb, dk, dv, dbias = _attn_bwd(q, k, v, o, do, lse, bias)
    g_wq = _wgrad_wide(h1b, dqb, 1, tk, "wgrad_q").reshape(N_DEV, dsh, d)
    dx2, dkvb, sums_q = _qkv_bwd(dx3, dqb, dk, dv, x2, tab, wq, wkv, tm2)
    g_wkv = _wgrad_rows(dkvb, hkv, 2, tk, "wgrad_kv").reshape(N_DEV, -1, d)
    (h_g2,), token = _xstart(scat([g_wkv, g_wq, g_wo]), dx2, "grads_start_attn")
    tab = tab + token[0, 0]
    dx1, dy2a, dgua, aa, sums_f0 = _ffn_bwd(dx2, x1, y2a, gua, tab, 0, wfi0, wfo0, tm, "ffn_bwd0")
    g_wfi0 = _wgrad_rows(dgua, h2a, 4, tk, "wgrad_ffn_in0").reshape(N_DEV, -1, d)
    g_wfo0 = _wgrad_rows(aa, dy2a, 2, tk, "wgrad_ffn_out0").reshape(N_DEV, -1, d)
    (h_g3,), token = _xstart(scat([g_wfi0, g_wfo0]), dx1, "grads_start_ffn0")
    tab = tab + token[0, 0]
    dx0, dya, dbcx, sums_c, dck = _conv_bwd(dx1, x0, ya, bcx, tab, ck_full, wci, wco, tm2)
    drel = _bias_bwd(dbias, nrel)
    dmod, dng, dkvg = _vec_bwd(sums_c, sums_f0, sums_q, sums_o, sums_f1, modval, ng_full, kv_norm_g.reshape(1, d))

    relw = -(-nrel // LANES) * LANES
    drel_p = jnp.concatenate([drel, jnp.zeros((nh, relw - nrel), F32)], axis=1)
    small3 = jnp.concatenate([dmod.reshape(16 * dq, LANES), dng.reshape(8 * dq, LANES), dkvg.reshape(8 * dq, LANES),
                              dck.reshape(8 * dq, LANES), loss_acc,
                              drel_p.reshape(nh * relw // LANES, LANES)], axis=0)
    (sm,) = _exchange([small3], ["gather"], "gather_small_grads")
    g_wci = _wgrad_rows(dbcx, h1a, 3, tk, "wgrad_conv_in").reshape(N_DEV, -1, d)
    g_wco = _wgrad_wide(ua, dya, 1, tk, "wgrad_conv_out").reshape(N_DEV, dsh, d)
    (h_g4,), token = _xstart(scat([g_wci, g_wco]), sm, "grads_start_conv")
    sm = sm + token[0, 0]
    o1, o2, o3, o4, o5 = 16 * dq, 24 * dq, 32 * dq, 40 * dq, 40 * dq + 8
    loss = jnp.sum(sm[:, o4:o5, :]) * (0.5 / d)
    dmod_all = sm[:, 0:o1].reshape(N_DEV, 16, d)
    mine_cols = lambda a: lax.dynamic_slice_in_dim(a, me * dsh, dsh, axis=2)
    dng_parts = mine_cols(sm[:, o1:o2].reshape(N_DEV, 8, d))
    dkvg_parts = sm[:, o2:o3].reshape(N_DEV, 8, d)[:, 0:1]
    dck_parts = mine_cols(sm[:, o3:o4].reshape(N_DEV, 8, d))[:, 0:3]
    drel_parts = sm[:, o5:].reshape(N_DEV, nh, relw)[:, :, 0:nrel]

    def update(parts, w, m, v, name, layers=1, parts_t=False):
        shp = w.shape
        w3, m3, v3 = (a.reshape(layers, -1, shp[-1]) for a in (w, m, v))
        if not parts_t:
            parts = [p.reshape(N_DEV, -1, shp[-1]) for p in parts]
        outs = _adamw_reduce(parts, w3, m3, v3, 256, name, parts_t)
        return [a.reshape(shp) for a in outs]

    p_wfi1, p_wfo1 = _xwait(h_g1, [sm], "grads_wait_ffn1")
    p_wfi0, p_wfo0 = _xwait(h_g3, [p_wfi1], "grads_wait_ffn0")
    tr = lambda a: jnp.swapaxes(a, 1, 2)
    u_ffn_in = [tr(a) for a in update([p_wfi0, p_wfi1], tr(ffn_w_in), tr(m_ffn_w_in), tr(v_ffn_w_in),
                                      "adamw_ffn_in", 2)]
    u_ffn_out = update([p_wfo0, p_wfo1], ffn_w_out, m_ffn_w_out, v_ffn_w_out, "adamw_ffn_out", 2)
    p_wkv, p_wq, p_wo = _xwait(h_g2, [u_ffn_out[0]], "grads_wait_attn")
    u_w_kv = update([p_wkv], w_kv, m_w_kv, v_w_kv, "adamw_w_kv", parts_t=True)
    u_w_q = update([p_wq], attn_w_q, m_attn_w_q, v_attn_w_q, "adamw_w_q")
    u_w_o = update([p_wo], attn_w_o, m_attn_w_o, v_attn_w_o, "adamw_w_o")

    sct = jnp.transpose(silu_c)
    dm_mod = jnp.stack([lax.dynamic_slice_in_dim(dmod_all[:, 6 * l:6 * l + 6].reshape(N_DEV, 6 * d), me * mw, mw, axis=1)
                        for l in range(nl)]).astype(BF16)
    dm_kv = lax.dynamic_slice_in_dim(dmod_all[:, R_KV:R_KV + 2].reshape(N_DEV, 2 * d), me * kmw, kmw, axis=1)
    u_mod_w = _adamw_outer(sct, dm_mod, mod_w, m_mod_w, v_mod_w, min(256, d), "adamw_mod_w")
    u_kv_mod_w = [a[0] for a in _adamw_outer(sct, dm_kv.astype(BF16)[None], kv_mod_w[None], m_kv_mod_w[None],
                                             v_kv_mod_w[None], min(256, d), "adamw_kv_mod_w")]

    modb_parts = jnp.stack([dmod_all[:, 6 * l:6 * l + 6].reshape(N_DEV, 6 * d) for l in range(nl)], axis=1)
    u_mod_b = update([modb_parts], mod_b, m_mod_b, v_mod_b, "adamw_mod_b")
    u_norm_g = update([dng_parts], norm_g.reshape(8, dsh), m_norm_g.reshape(8, dsh), v_norm_g.reshape(8, dsh), "adamw_norm_g")
    u_norm_g = [a.reshape(norm_g.shape) for a in u_norm_g]
    u_conv_k = update([dck_parts], conv_k, m_conv_k, v_conv_k, "adamw_conv_k")
    kvb_parts = dmod_all[:, R_KV:R_KV + 2].reshape(N_DEV, 1, 2 * d)
    u_kv_mod_b = [a.reshape(kv_mod_b.shape) for a in update([kvb_parts], kv_mod_b.reshape(1, -1), m_kv_mod_b.reshape(1, -1),
                                                            v_kv_mod_b.reshape(1, -1), "adamw_kv_mod_b")]
    u_kv_norm_g = [a.reshape(kv_norm_g.shape) for a in update([dkvg_parts], kv_norm_g.reshape(1, -1), m_kv_norm_g.reshape(1, -1),
                                                              v_kv_norm_g.reshape(1, -1), "adamw_kv_norm_g")]
    u_rel = update([drel_parts], rel_bias, m_rel_bias, v_rel_bias, "adamw_rel_bias")

    others = [u_ffn_in, u_ffn_out, u_w_kv, u_w_q, u_w_o, u_mod_w, u_kv_mod_w, u_mod_b, u_norm_g, u_conv_k, u_kv_mod_b,
              u_kv_norm_g, u_rel]
    p_wci, p_wco = _xwait(h_g4, [u[3] for u in others], "grads_wait_conv")
    u_conv_in = update([p_wci], conv_w_in, m_conv_w_in, v_conv_w_in, "adamw_conv_in", parts_t=True)
    u_conv_out = update([p_wco], conv_w_out, m_conv_w_out, v_conv_w_out, "adamw_conv_out")

    ups = [u_mod_w, u_mod_b, u_norm_g, u_ffn_in, u_ffn_out, u_conv_in, u_conv_k, u_conv_out, u_kv_mod_w, u_kv_mod_b,
           u_kv_norm_g, u_w_kv, u_w_q, u_w_o, u_rel]
    return (loss, dx0[None], *[u[0] for u in ups], *[u[1] for u in ups], *[u[2] for u in ups], *[u[3] for u in ups])
```

```python
import jax
import jax.numpy as jnp
from jax import lax
from jax.experimental import pallas as pl
from jax.experimental.pallas import tpu as pltpu

F32 = jnp.float32
BF16 = jnp.bfloat16

EPS = 1e-6
CHUNK = 64
HEAD_DIM = 64
N_LEFT = 8
LANES = 128
MXU = 256
FFN_CHUNK = 4
QB = 4 * CHUNK
KW = QB + N_LEFT * CHUNK
BIAS_VARIANTS = N_LEFT * CHUNK // QB + 1
ATTN_PER = 8
NEG = -1e30
N_DEV = 8

ADAM_LR = 0.001
ADAM_B1 = 0.9
ADAM_B2 = 0.999
ADAM_EPS = 1e-08
ADAM_WD = 0.01
ADAM_STEP = 10

VMEM_BIG = 56 * 1024 * 1024

NT = (((1,), (1,)), ((), ()))
TN = (((0,), (0,)), ((), ()))

R_W1, R_SH1, R_P1, R_W2, R_SH2, R_P2 = range(6)
R_KV = 12


def _params(vmem):
    return pltpu.CompilerParams(vmem_limit_bytes=vmem)


def _row_tile(rows, cap):
    for t in range(min(cap, rows) // 16 * 16, 0, -16):
        if rows % t == 0:
            return t
    return rows


def _rows(tm, cols):
    return pl.BlockSpec((tm, cols), lambda i: (i, 0))


def _const(shape):
    nd = len(shape)
    return pl.BlockSpec(shape, lambda *_: (0,) * nd, pipeline_mode=pl.Buffered(1))


def _rs(x):
    return lax.rsqrt(jnp.mean(x * x, axis=-1, keepdims=True) + EPS)


def _norm_bwd(d, n, r):
    return r * (d - n * jnp.mean(d * n, axis=-1, keepdims=True))


def _colsum(a):
    return jnp.sum(a, axis=0, keepdims=True)


def _sigmoid(g):
    return 1.0 / (1.0 + jnp.exp(-g))


def _exchange(arrays, modes, name):
    n = len(arrays)
    out_shape = []
    for a, mode in zip(arrays, modes):
        shp = (N_DEV,) + a.shape if mode == "gather" else a.shape
        out_shape.append(jax.ShapeDtypeStruct(shp, a.dtype))

    def body(*refs):
        ins, outs = refs[:n], refs[n:2 * n]
        send_sems, recv_sems, local_sems = refs[2 * n:]
        x, y, c = lax.axis_index("x"), lax.axis_index("y"), lax.axis_index("c")
        me = 4 * x + 2 * y + c
        local, sends, recvs = [], [], []
        for a in range(n):
            own = ins[a] if modes[a] == "gather" else ins[a].at[me]
            cp = pltpu.make_async_copy(own, outs[a].at[me], local_sems.at[a])
            cp.start()
            local.append(cp)
        for k in range(1, N_DEV):
            px = 1 - x if k & 4 else x
            py = 1 - y if k & 2 else y
            pc = 1 - c if k & 1 else c
            peer = 4 * px + 2 * py + pc
            for a in range(n):
                src = ins[a] if modes[a] == "gather" else ins[a].at[peer]
                sem = a * (N_DEV - 1) + k - 1
                cp = pltpu.make_async_remote_copy(
                    src_ref=src, dst_ref=outs[a].at[me],
                    send_sem=send_sems.at[sem], recv_sem=recv_sems.at[sem],
                    device_id=(px, py, pc), device_id_type=pl.DeviceIdType.MESH)
                cp.start()
                sends.append(cp)
                recvs.append(pltpu.make_async_remote_copy(
                    src_ref=src, dst_ref=outs[a].at[peer],
                    send_sem=send_sems.at[sem], recv_sem=recv_sems.at[sem],
                    device_id=(px, py, pc), device_id_type=pl.DeviceIdType.MESH))
        for cp in recvs:
            cp.wait_recv()
        for cp in sends:
            cp.wait_send()
        for cp in local:
            cp.wait()

    any_spec = pl.BlockSpec(memory_space=pl.ANY)
    return pl.pallas_call(
        body, name=name,
        out_shape=tuple(out_shape),
        in_specs=[any_spec] * n,
        out_specs=tuple([any_spec] * n),
        scratch_shapes=[
            pltpu.SemaphoreType.DMA((n * (N_DEV - 1),)),
            pltpu.SemaphoreType.DMA((n * (N_DEV - 1),)),
            pltpu.SemaphoreType.DMA((n,)),
        ],
    )(*arrays)


def _peers(x, y, c):
    out = []
    for k in range(1, N_DEV):
        px = 1 - x if k & 4 else x
        py = 1 - y if k & 2 else y
        pc = 1 - c if k & 1 else c
        out.append((k - 1, (px, py, pc), 4 * px + 2 * py + pc))
    return out


_OTHER_CORE_SLOTS = (2, 4, 6)


def _land_shape(a, mode):
    return a.shape if mode == "scatter" else (N_DEV,) + a.shape


_HBM = pl.BlockSpec(memory_space=pltpu.HBM)
_SEM = pl.BlockSpec(memory_space=pltpu.SEMAPHORE)
_EFFECT = pltpu.SideEffectType.DATAFLOW_SIDE_EFFECTING


def _xstart(groups, after, name):
    flat = [(a, m) for arrays, modes in groups for a, m in zip(arrays, modes)]
    n, ngr = len(flat), len(groups)
    sizes = [len(arrays) for arrays, _ in groups]
    npeer = N_DEV - 1

    def body(*refs):
        ins, lands = refs[:n], refs[n:2 * n]
        outs = refs[2 * n + 1:]
        sems = outs[:2 * ngr]
        token = outs[2 * ngr + 2 * n]
        local_sems = outs[2 * ngr + 2 * n + 1]
        stage = outs[2 * ngr + 2 * n + 2:]
        x, y, c = lax.axis_index("x"), lax.axis_index("y"), lax.axis_index("c")
        me = 4 * x + 2 * y + c
        loads, stores = [], []
        for a in range(n):
            own = ins[a].at[me] if flat[a][1] == "scatter" else ins[a]
            loads.append(pltpu.make_async_copy(own, stage[a], local_sems.at[a]))
            stores.append(pltpu.make_async_copy(stage[a], lands[a].at[me], local_sems.at[a]))
            loads[a].start()
        for a in range(n):
            loads[a].wait()
            stores[a].start()
        a = 0
        for g in range(ngr):
            for j in range(sizes[g]):
                mode = flat[a][1]
                for slot, peer, pidx in _peers(x, y, c):
                    if mode == "gather_half" and slot in _OTHER_CORE_SLOTS:
                        continue
                    pltpu.make_async_remote_copy(
                        src_ref=ins[a].at[pidx] if mode == "scatter" else ins[a], dst_ref=lands[a].at[me],
                        send_sem=sems[2 * g].at[j * npeer + slot], recv_sem=sems[2 * g + 1].at[j * npeer + slot],
                        device_id=peer, device_id_type=pl.DeviceIdType.MESH).start()
                a += 1
        for cp in stores:
            cp.wait()
        token[...] = jnp.zeros_like(token)

    out_shape, out_specs = [], []
    for sz in sizes:
        out_shape += [pltpu.SemaphoreType.DMA((sz * npeer,)), pltpu.SemaphoreType.DMA((sz * npeer,))]
        out_specs += [_SEM, _SEM]
    out_shape += [pltpu.HBM(a.shape, a.dtype) for a, _ in flat]
    out_shape += [pltpu.HBM(_land_shape(a, m), a.dtype) for a, m in flat]
    out_specs += [_HBM] * (2 * n)
    out_shape.append(jax.ShapeDtypeStruct((8, LANES), F32))
    out_specs.append(pl.BlockSpec(memory_space=pltpu.VMEM))
    args = [pltpu.with_memory_space_constraint(a, pltpu.HBM) for a, _ in flat]
    args += [pltpu.with_memory_space_constraint(lax.empty(_land_shape(a, m), a.dtype), pltpu.HBM) for a, m in flat]
    res = pl.pallas_call(
        body, name=name, out_shape=tuple(out_shape),
        in_specs=[_HBM] * (2 * n) + [pl.BlockSpec(memory_space=pl.ANY)], out_specs=tuple(out_specs),
        input_output_aliases={i: 2 * ngr + i for i in range(2 * n)},
        scratch_shapes=[pltpu.SemaphoreType.DMA((n,))]
                       + [pltpu.VMEM(a.shape[1:] if m == "scatter" else a.shape, a.dtype) for a, m in flat],
        compiler_params=pltpu.CompilerParams(has_side_effects=_EFFECT, vmem_limit_bytes=VMEM_BIG),
    )(*args, after)
    handles, a = [], 0
    for g, sz in enumerate(sizes):
        handles.append((res[2 * g], res[2 * g + 1], list(res[2 * ngr + a:2 * ngr + a + sz]),
                        list(res[2 * ngr + n + a:2 * ngr + n + a + sz]), list(groups[g][1])))
        a += sz
    return handles, res[-1]


def _xwait(handle, after, name):
    send_sems, recv_sems, srcs, lands, modes = handle
    m = len(srcs)
    npeer = N_DEV - 1
    after = list(after)

    def body(*refs):
        ins, lnd = refs[:m], refs[m:2 * m]
        ssem, rsem = refs[2 * m], refs[2 * m + 1]
        x, y, c = lax.axis_index("x"), lax.axis_index("y"), lax.axis_index("c")
        for j in range(m):
            for slot, peer, pidx in _peers(x, y, c):
                if modes[j] == "gather_half" and slot in _OTHER_CORE_SLOTS:
                    continue
                cp = pltpu.make_async_remote_copy(
                    src_ref=ins[j].at[pidx] if modes[j] == "scatter" else ins[j], dst_ref=lnd[j].at[pidx],
                    send_sem=ssem.at[j * npeer + slot], recv_sem=rsem.at[j * npeer + slot],
                    device_id=peer, device_id_type=pl.DeviceIdType.MESH)
                cp.wait_send()
                cp.wait_recv()

    res = pl.pallas_call(
        body, name=name,
        out_shape=tuple([pltpu.HBM(a.shape, a.dtype) for a in srcs] + [pltpu.HBM(a.shape, a.dtype) for a in lands]),
        in_specs=[_HBM] * (2 * m) + [_SEM, _SEM] + [pl.BlockSpec(memory_space=pl.ANY)] * len(after),
        out_specs=tuple([_HBM] * (2 * m)),
        input_output_aliases={i: i for i in range(2 * m)},
        compiler_params=pltpu.CompilerParams(has_side_effects=_EFFECT),
    )(*srcs, *lands, send_sems, recv_sems, *after)
    return list(res[m:])


def _forward_to_sibling(lands, name):
    n = len(lands)

    def body(*refs):
        outs = refs[n:2 * n]
        send_sems, recv_sems = refs[2 * n:]
        x, y, c = lax.axis_index("x"), lax.axis_index("y"), lax.axis_index("c")
        sibling = (x, y, 1 - c)
        sends, recvs = [], []
        for a in range(n):
            for j, k in enumerate((2, 4, 6)):
                px = 1 - x if k & 4 else x
                py = 1 - y if k & 2 else y
                got = 4 * px + 2 * py + c
                missing = 4 * px + 2 * py + (1 - c)
                sem = a * 3 + j
                cp = pltpu.make_async_remote_copy(
                    src_ref=outs[a].at[got], dst_ref=outs[a].at[got],
                    send_sem=send_sems.at[sem], recv_sem=recv_sems.at[sem],
                    device_id=sibling, device_id_type=pl.DeviceIdType.MESH)
                cp.start()
                sends.append(cp)
                recvs.append(pltpu.make_async_remote_copy(
                    src_ref=outs[a].at[missing], dst_ref=outs[a].at[missing],
                    send_sem=send_sems.at[sem], recv_sem=recv_sems.at[sem],
                    device_id=sibling, device_id_type=pl.DeviceIdType.MESH))
        for cp in recvs:
            cp.wait_recv()
        for cp in sends:
            cp.wait_send()

    any_spec = pl.BlockSpec(memory_space=pl.ANY)
    return list(pl.pallas_call(
        body, name=name,
        out_shape=tuple(jax.ShapeDtypeStruct(a.shape, a.dtype) for a in lands),
        in_specs=[any_spec] * n, out_specs=tuple([any_spec] * n),
        input_output_aliases={i: i for i in range(n)},
        scratch_shapes=[pltpu.SemaphoreType.DMA((3 * n,)), pltpu.SemaphoreType.DMA((3 * n,))],
    )(*lands))


def _mod_fwd(c_all, mod_w, kv_mod_w):
    nl, d, mw = mod_w.shape
    kw = kv_mod_w.shape[1]

    def body(c_ref, mw_ref, kw_ref, o_ref, sc_ref):
        cc = c_ref[...]
        sc = (cc * _sigmoid(cc)).astype(BF16)
        sc_ref[...] = sc
        for l in range(nl):
            o_ref[:, l * mw:(l + 1) * mw] = jnp.dot(sc, mw_ref[l].astype(BF16), preferred_element_type=F32)
        o_ref[:, nl * mw:nl * mw + kw] = jnp.dot(sc, kw_ref[...].astype(BF16), preferred_element_type=F32)

    return pl.pallas_call(
        body, name="mod_fwd",
        out_shape=(jax.ShapeDtypeStruct((c_all.shape[0], nl * mw + kw), F32),
                   jax.ShapeDtypeStruct(c_all.shape, BF16)),
        compiler_params=_params(VMEM_BIG),
    )(c_all, mod_w, kv_mod_w)


def _vec_prep(modrow, modb, kvrow, kvb, ng, kvg):
    d = ng.shape[1]

    def body(mr_ref, mb_ref, kr_ref, kb_ref, ng_ref, kvg_ref, t_ref, m_ref):
        t_ref[...] = jnp.zeros_like(t_ref)
        m_ref[...] = jnp.zeros_like(m_ref)
        for l in range(2):
            mod = mr_ref[l] + mb_ref[l]
            m_ref[6 * l:6 * l + 6, :] = mod
            g = ng_ref[4 * l:4 * l + 4, :]
            t_ref[6 * l + R_W1:6 * l + R_W1 + 1, :] = g[0:1] * (1.0 + mod[1:2])
            t_ref[6 * l + R_SH1:6 * l + R_SH1 + 1, :] = mod[0:1]
            t_ref[6 * l + R_P1:6 * l + R_P1 + 1, :] = mod[2:3] * g[1:2]
            t_ref[6 * l + R_W2:6 * l + R_W2 + 1, :] = g[2:3] * (1.0 + mod[4:5])
            t_ref[6 * l + R_SH2:6 * l + R_SH2 + 1, :] = mod[3:4]
            t_ref[6 * l + R_P2:6 * l + R_P2 + 1, :] = mod[5:6] * g[3:4]
        kv = kr_ref[...] + kb_ref[...]
        m_ref[R_KV:R_KV + 2, :] = kv
        t_ref[R_KV:R_KV + 1, :] = kvg_ref[...] * (1.0 + kv[1:2])
        t_ref[R_KV + 1:R_KV + 2, :] = kv[0:1]

    return pl.pallas_call(
        body, name="vec_prep",
        out_shape=(jax.ShapeDtypeStruct((16, d), F32), jax.ShapeDtypeStruct((16, d), F32)),
    )(modrow, modb, kvrow, kvb, ng, kvg)


def _vec_bwd(sums_c, sums_f0, sums_q, sums_o, sums_f1, mt, ng, kvg):
    d = ng.shape[1]

    def body(sc_ref, sf0_ref, sq_ref, so_ref, sf1_ref, m_ref, ng_ref, kvg_ref, dm_ref, dng_ref, dkvg_ref, g_ref):
        g_ref[...] = jnp.zeros_like(g_ref)
        g_ref[0:3, :] = sc_ref[0:3, :]
        g_ref[3:6, :] = sf0_ref[3:6, :]
        g_ref[6:8, :] = sq_ref[0:2, :]
        g_ref[8:9, :] = so_ref[2:3, :]
        g_ref[9:12, :] = sf1_ref[3:6, :]
        g_ref[R_KV:R_KV + 2, :] = sq_ref[2:4, :]
        dm_ref[...] = jnp.zeros_like(dm_ref)
        dkvg_ref[...] = jnp.zeros_like(dkvg_ref)
        for l in range(2):
            g = ng_ref[4 * l:4 * l + 4, :]
            mod = m_ref[6 * l:6 * l + 6, :]
            s = g_ref[6 * l:6 * l + 6, :]
            dm_ref[6 * l + 0:6 * l + 1, :] = s[1:2]
            dm_ref[6 * l + 1:6 * l + 2, :] = s[0:1] * g[0:1]
            dm_ref[6 * l + 2:6 * l + 3, :] = s[2:3] * g[1:2]
            dm_ref[6 * l + 3:6 * l + 4, :] = s[4:5]
            dm_ref[6 * l + 4:6 * l + 5, :] = s[3:4] * g[2:3]
            dm_ref[6 * l + 5:6 * l + 6, :] = s[5:6] * g[3:4]
            dng_ref[4 * l + 0:4 * l + 1, :] = s[0:1] * (1.0 + mod[1:2])
            dng_ref[4 * l + 1:4 * l + 2, :] = s[2:3] * mod[2:3]
            dng_ref[4 * l + 2:4 * l + 3, :] = s[3:4] * (1.0 + mod[4:5])
            dng_ref[4 * l + 3:4 * l + 4, :] = s[5:6] * mod[5:6]
        dm_ref[R_KV:R_KV + 1, :] = g_ref[R_KV + 1:R_KV + 2, :]
        dm_ref[R_KV + 1:R_KV + 2, :] = g_ref[R_KV:R_KV + 1, :] * kvg_ref[...]
        dkvg_ref[0:1, :] = g_ref[R_KV:R_KV + 1, :] * (1.0 + m_ref[R_KV + 1:R_KV + 2, :])

    return pl.pallas_call(
        body, name="vec_bwd",
        out_shape=(jax.ShapeDtypeStruct((16, d), F32), jax.ShapeDtypeStruct((8, d), F32),
                   jax.ShapeDtypeStruct((8, d), F32)),
        scratch_shapes=[pltpu.VMEM((16, d), F32)],
    )(sums_c, sums_f0, sums_q, sums_o, sums_f1, mt, ng, kvg)


def _rel_index(nrel):
    width = KW + QB
    e = lax.broadcasted_iota(jnp.int32, (nrel, width), 1)
    r = lax.broadcasted_iota(jnp.int32, (nrel, width), 0)
    max_rel = (nrel - 1) // 2
    idx = jnp.clip(KW - e, -max_rel, max_rel) + max_rel
    return (idx == r).astype(F32)


def _band_valid():
    row = lax.broadcasted_iota(jnp.int32, (QB, KW), 0) // CHUNK
    col = lax.broadcasted_iota(jnp.int32, (QB, KW), 1) // CHUNK
    j = col - row
    return (j >= 0) & (j <= N_LEFT)


def _bias_fwd(rel_bias):
    nh, nrel = rel_bias.shape
    width = KW + QB

    def body(rb_ref, o_ref):
        onehot = _rel_index(nrel)
        gr = jnp.dot(rb_ref[...], onehot, preferred_element_type=F32, precision=lax.Precision.HIGHEST)
        valid = _band_valid() & _key_valid(pl.program_id(0))
        for h in range(nh):
            xrow = jnp.broadcast_to(gr[h:h + 1, :], (QB, width))
            rolled = pltpu.roll(xrow, 0, 1, stride=1, stride_axis=0)
            o_ref[h] = jnp.where(valid, rolled[:, QB:], NEG)

    return pl.pallas_call(
        body, name="bias_fwd", grid=(BIAS_VARIANTS,),
        in_specs=[pl.BlockSpec(rel_bias.shape, lambda v: (0, 0))],
        out_specs=pl.BlockSpec((None, nh, QB, KW), lambda v: (v, 0, 0, 0)),
        out_shape=jax.ShapeDtypeStruct((BIAS_VARIANTS, nh, QB, KW), F32),
        compiler_params=_params(VMEM_BIG),
    )(rel_bias)


def _bias_bwd(dbias, nrel):
    nh = dbias.shape[0]
    width = KW + QB

    def body(db_ref, o_ref, diag_ref):
        onehot = _rel_index(nrel)
        valid = _band_valid()
        rr = lax.broadcasted_iota(jnp.int32, (QB, QB), 0)
        cc = lax.broadcasted_iota(jnp.int32, (QB, QB), 1)
        flip = (rr + cc == QB - 1).astype(F32)
        for h in range(nh):
            rev = jnp.dot(flip, jnp.where(valid, db_ref[h], 0.0), preferred_element_type=F32,
                          precision=lax.Precision.HIGHEST)
            w = jnp.concatenate([jnp.zeros((QB, QB), F32), rev], axis=1)
            back = pltpu.roll(w, width - (QB - 1), 1, stride=1, stride_axis=0)
            diag_ref[h:h + 1, :] = _colsum(back)
        o_ref[...] = lax.dot_general(diag_ref[...], onehot, NT, preferred_element_type=F32,
                                     precision=lax.Precision.HIGHEST)

    return pl.pallas_call(
        body, name="bias_bwd",
        out_shape=jax.ShapeDtypeStruct((nh, nrel), F32),
        scratch_shapes=[pltpu.VMEM((nh, width), F32)],
        compiler_params=_params(VMEM_BIG),
    )(dbias)


def _conv_fwd(x, tab, ck, wci, wco, tm):
    s, d = x.shape

    def body(x_ref, t_ref, ck_ref, wci_ref, wco_ref, x1_ref, h_ref, bcx_ref, u_ref, y_ref, carry):
        @pl.when(pl.program_id(0) == 0)
        def _():
            carry[...] = jnp.zeros_like(carry)

        xv = x_ref[...]
        hb = ((xv * _rs(xv)) * t_ref[R_W1:R_W1 + 1, :] + t_ref[R_SH1:R_SH1 + 1, :]).astype(BF16)
        h_ref[...] = hb
        for j in range(3 * d // MXU):
            bcx_ref[:, j * MXU:(j + 1) * MXU] = lax.dot_general(hb, wci_ref[j * MXU:(j + 1) * MXU, :], NT,
                                                                preferred_element_type=F32)
        bg, cg, xi = bcx_ref[:, 0:d], bcx_ref[:, d:2 * d], bcx_ref[:, 2 * d:3 * d]
        z = cg * xi
        row = lax.broadcasted_iota(jnp.int32, z.shape, 0)
        c1, c2 = carry[7:8, :], carry[6:7, :]
        z1 = jnp.where(row == 0, c1, pltpu.roll(z, 1, 0))
        z2 = jnp.where(row == 0, c2, jnp.where(row == 1, c1, pltpu.roll(z, 2, 0)))
        carry[...] = z[tm - 8:tm, :]
        conv = ck_ref[0:1, :] * z2 + ck_ref[1:2, :] * z1 + ck_ref[2:3, :] * z
        ub = (bg * conv).astype(BF16)
        u_ref[...] = ub
        yv = jnp.dot(ub, wco_ref[...], preferred_element_type=F32)
        y_ref[...] = yv
        x1_ref[...] = xv + (yv * _rs(yv)) * t_ref[R_P1:R_P1 + 1, :]

    return pl.pallas_call(
        body, name="conv_fwd", grid=(s // tm,),
        in_specs=[_rows(tm, d), _const(tab.shape), _const(ck.shape), _const(wci.shape), _const(wco.shape)],
        out_specs=(_rows(tm, d), _rows(tm, d), _rows(tm, 3 * d), _rows(tm, d), _rows(tm, d)),
        out_shape=(jax.ShapeDtypeStruct((s, d), F32), jax.ShapeDtypeStruct((s, d), BF16),
                   jax.ShapeDtypeStruct((s, 3 * d), F32), jax.ShapeDtypeStruct((s, d), BF16),
                   jax.ShapeDtypeStruct((s, d), F32)),
        scratch_shapes=[pltpu.VMEM((8, d), F32)],
        compiler_params=_params(VMEM_BIG),
    )(x, tab, ck, wci, wco)


def _ffn_fwd(x, tab, base, wfi, wfo, tgt, tm, name):
    s, d = x.shape
    hid = wfo.shape[0]
    nblk = hid // MXU
    with_loss = tgt is not None

    def body(*refs):
        if with_loss:
            x_ref, t_ref, wfi_ref, wfo_ref, tgt_ref, xo_ref, h_ref, gu_ref, y_ref, loss_ref, a_scr = refs
        else:
            x_ref, t_ref, wfi_ref, wfo_ref, xo_ref, h_ref, gu_ref, y_ref, a_scr = refs
        xv = x_ref[...]
        hb = ((xv * _rs(xv)) * t_ref[base + R_W2:base + R_W2 + 1, :]
              + t_ref[base + R_SH2:base + R_SH2 + 1, :]).astype(BF16)
        h_ref[...] = hb
        acc = jnp.zeros((tm, d), F32)
        for c0 in range(0, nblk, FFN_CHUNK):
            for j in range(c0, min(c0 + FFN_CHUNK, nblk)):
                lo, hi = j * MXU, (j + 1) * MXU
                g = lax.dot_general(hb, wfi_ref[lo:hi, :], NT, preferred_element_type=F32)
                u = lax.dot_general(hb, wfi_ref[hid + lo:hid + hi, :], NT, preferred_element_type=F32)
                gu_ref[:, lo:hi] = g.astype(BF16)
                gu_ref[:, hid + lo:hid + hi] = u.astype(BF16)
                a_scr[:, lo:hi] = ((g * _sigmoid(g)) * u).astype(BF16)
            lo, hi = c0 * MXU, min(c0 + FFN_CHUNK, nblk) * MXU
            acc = acc + jnp.dot(a_scr[:, lo:hi], wfo_ref[lo:hi, :], preferred_element_type=F32)
        y_ref[...] = acc
        xo = xv + (acc * _rs(acc)) * t_ref[base + R_P2:base + R_P2 + 1, :]
        if with_loss:
            @pl.when(pl.program_id(0) == 0)
            def _():
                loss_ref[...] = jnp.zeros_like(loss_ref)

            err = xo - tgt_ref[...]
            xo_ref[...] = err * (1.0 / d)
            e2 = jnp.sum((err * err).reshape(tm // 8, 8, d), axis=0)
            for q in range(d // LANES):
                loss_ref[...] += e2[:, q * LANES:(q + 1) * LANES]
        else:
            xo_ref[...] = xo

    in_specs = [_rows(tm, d), _const(tab.shape), _const(wfi.shape), _const(wfo.shape)]
    args = [x, tab, wfi, wfo]
    out_specs = [_rows(tm, d), _rows(tm, d), _rows(tm, 2 * hid), _rows(tm, d)]
    out_shape = [jax.ShapeDtypeStruct((s, d), F32), jax.ShapeDtypeStruct((s, d), BF16),
                 jax.ShapeDtypeStruct((s, 2 * hid), BF16), jax.ShapeDtypeStruct((s, d), F32)]
    if with_loss:
        in_specs.append(_rows(tm, d))
        args.append(tgt)
        out_specs.append(pl.BlockSpec((8, LANES), lambda i: (0, 0)))
        out_shape.append(jax.ShapeDtypeStruct((8, LANES), F32))
    return pl.pallas_call(
        body, name=name, grid=(s // tm,), in_specs=in_specs, out_specs=tuple(out_specs),
        out_shape=tuple(out_shape), scratch_shapes=[pltpu.VMEM((tm, hid), BF16)],
        compiler_params=_params(VMEM_BIG),
    )(*args)


def _qkv_fwd(x, tab, wq, wkv, tm):
    s, d = x.shape
    base = 6

    def body(x_ref, t_ref, wq_ref, wkv_ref, hkv_ref, h1_ref, q_ref, k_ref, v_ref):
        xv = x_ref[...]
        n = xv * _rs(xv)
        hkv = (n * t_ref[R_KV:R_KV + 1, :] + t_ref[R_KV + 1:R_KV + 2, :]).astype(BF16)
        h1 = (n * t_ref[base + R_W1:base + R_W1 + 1, :] + t_ref[base + R_SH1:base + R_SH1 + 1, :]).astype(BF16)
        hkv_ref[...] = hkv
        h1_ref[...] = h1
        q_ref[...] = (jnp.dot(h1, wq_ref[...], preferred_element_type=F32) * (HEAD_DIM ** -0.5)).astype(BF16)
        for j in range(d // MXU):
            lo, hi = j * MXU, (j + 1) * MXU
            k_ref[:, lo:hi] = lax.dot_general(hkv, wkv_ref[lo:hi, :], NT, preferred_element_type=F32).astype(BF16)
            v_ref[:, lo:hi] = lax.dot_general(hkv, wkv_ref[d + lo:d + hi, :], NT,
                                              preferred_element_type=F32).astype(BF16)

    act = jax.ShapeDtypeStruct((s, d), BF16)
    return pl.pallas_call(
        body, name="qkv_fwd", grid=(s // tm,),
        in_specs=[_rows(tm, d), _const(tab.shape), _const(wq.shape), _const(wkv.shape)],
        out_specs=tuple([_rows(tm, d)] * 5), out_shape=(act,) * 5,
        compiler_params=_params(VMEM_BIG),
    )(x, tab, wq, wkv)


def _window_specs(per=1):
    return [pl.BlockSpec((QB, LANES), (lambda p, b, w=w: (jnp.maximum(per * b - 2 + w, 0), p)))
            for w in range(2 + per)]


def _key_valid(b):
    col = lax.broadcasted_iota(jnp.int32, (QB, KW), 1) // CHUNK
    return (b * (QB // CHUNK) - N_LEFT + col) >= 0


def _bias_spec(per=1, sub=0):
    return pl.BlockSpec((None, LANES // HEAD_DIM, QB, KW),
                        lambda p, b: (jnp.minimum(per * b + sub, BIAS_VARIANTS - 1), p, 0, 0))


def _head_masks():
    lane = lax.broadcasted_iota(jnp.int32, (1, LANES), 1)
    return [(lane // HEAD_DIM == hh) for hh in range(LANES // HEAD_DIM)]


def _attn_fwd(q, k, v, bias):
    s, d = q.shape
    per = ATTN_PER
    npair, nb = d // LANES, s // (per * QB)
    hpp = LANES // HEAD_DIM
    nwin = 2 + per
    nbias = min(per, BIAS_VARIANTS)

    def body(*refs):
        q_ref, k_refs, v_refs = refs[0], refs[1:1 + nwin], refs[1 + nwin:1 + 2 * nwin]
        bias_refs = refs[1 + 2 * nwin:1 + 2 * nwin + nbias]
        o_ref, lse_ref = refs[1 + 2 * nwin + nbias:]
        ks = [r[...] for r in k_refs]
        vs = [r[...] for r in v_refs]
        masks = _head_masks()
        for sub in range(per):
            rows = slice(sub * QB, (sub + 1) * QB)
            qv = q_ref[rows, :]
            kwin = jnp.concatenate(ks[sub:sub + 3], axis=0)
            vwin = jnp.concatenate(vs[sub:sub + 3], axis=0)
            o = jnp.zeros((QB, LANES), F32)
            lse = jnp.zeros((QB, LANES), F32)
            scs = [lax.dot_general(jnp.where(masks[hh], qv, jnp.zeros_like(qv)), kwin, NT,
                                   preferred_element_type=F32) + bias_refs[min(sub, nbias - 1)][hh]
                   for hh in range(hpp)]
            for hh in range(hpp):
                vm = jnp.where(masks[hh], vwin, jnp.zeros_like(vwin))
                sc = scs[hh]
                m = jnp.max(sc, axis=-1, keepdims=True)
                p = jnp.exp(sc - m)
                l = jnp.sum(p, axis=-1, keepdims=True)
                o = o + jnp.dot(p.astype(BF16), vm, preferred_element_type=F32) * (1.0 / l)
                lse = jnp.where(masks[hh], m + jnp.log(l), lse)
            o_ref[rows, :] = o.astype(BF16)
            lse_ref[rows, :] = lse

    blk = pl.BlockSpec((per * QB, LANES), lambda p, b: (b, p))
    return pl.pallas_call(
        body, name="attn_fwd", grid=(npair, nb),
        in_specs=[blk] + _window_specs(per) + _window_specs(per) + [_bias_spec(per, sub) for sub in range(nbias)],
        out_specs=(blk, blk),
        out_shape=(jax.ShapeDtypeStruct((s, d), BF16), jax.ShapeDtypeStruct((s, d), F32)),
        compiler_params=_params(VMEM_BIG),
    )(q, *([k] * nwin), *([v] * nwin), *([bias] * nbias))


def _attn_out_fwd(o, x, tab, wo, tm):
    s, d = x.shape
    base = 6

    def body(o_ref, x_ref, t_ref, wo_ref, x3_ref, y_ref):
        yv = jnp.dot(o_ref[...], wo_ref[...], preferred_element_type=F32)
        y_ref[...] = yv
        x3_ref[...] = x_ref[...] + (yv * _rs(yv)) * t_ref[base + R_P1:base + R_P1 + 1, :]

    return pl.pallas_call(
        body, name="attn_out_fwd", grid=(s // tm,),
        in_specs=[_rows(tm, d), _rows(tm, d), _const(tab.shape), _const(wo.shape)],
        out_specs=(_rows(tm, d), _rows(tm, d)),
        out_shape=(jax.ShapeDtypeStruct((s, d), F32), jax.ShapeDtypeStruct((s, d), F32)),
        compiler_params=_params(VMEM_BIG),
    )(o, x, tab, wo)


def _ffn_bwd(dxo, x, y, gu, tab, base, wfi, wfo, tm, name):
    s, d = x.shape
    hid = wfo.shape[0]
    nblk = hid // MXU

    def body(dxo_ref, x_ref, y_ref, gu_ref, t_ref, wfi_ref, wfo_ref, dx_ref, dyb_ref, dgu_ref, a_ref, sums_ref):
        @pl.when(pl.program_id(0) == 0)
        def _():
            sums_ref[...] = jnp.zeros_like(sums_ref)

        dxo_v = dxo_ref[...]
        yv = y_ref[...]
        ry = _rs(yv)
        ny = yv * ry
        sums_ref[R_P2:R_P2 + 1, :] += _colsum(dxo_v * ny)
        dyb = _norm_bwd(dxo_v * t_ref[base + R_P2:base + R_P2 + 1, :], ny, ry).astype(BF16)
        dyb_ref[...] = dyb
        dh = jnp.zeros((tm, d), F32)
        for c0 in range(0, nblk, FFN_CHUNK):
            for j in range(c0, min(c0 + FFN_CHUNK, nblk)):
                lo, hi = j * MXU, (j + 1) * MXU
                da = lax.dot_general(dyb, wfo_ref[lo:hi, :], NT, preferred_element_type=F32)
                g, u = gu_ref[:, lo:hi].astype(F32), gu_ref[:, hid + lo:hid + hi].astype(F32)
                sg = _sigmoid(g)
                gs = g * sg
                a_ref[:, lo:hi] = (gs * u).astype(BF16)
                dgu_ref[:, lo:hi] = (da * u * sg * (1.0 + g * (1.0 - sg))).astype(BF16)
                dgu_ref[:, hid + lo:hid + hi] = (da * gs).astype(BF16)
            lo, hi = c0 * MXU, min(c0 + FFN_CHUNK, nblk) * MXU
            dh = dh + jnp.dot(dgu_ref[:, lo:hi], wfi_ref[lo:hi, :], preferred_element_type=F32)
            dh = dh + jnp.dot(dgu_ref[:, hid + lo:hid + hi], wfi_ref[hid + lo:hid + hi, :],
                              preferred_element_type=F32)
        xv = x_ref[...]
        r = _rs(xv)
        n = xv * r
        sums_ref[R_SH2:R_SH2 + 1, :] += _colsum(dh)
        sums_ref[R_W2:R_W2 + 1, :] += _colsum(dh * n)
        dx_ref[...] = dxo_v + _norm_bwd(dh * t_ref[base + R_W2:base + R_W2 + 1, :], n, r)

    return pl.pallas_call(
        body, name=name, grid=(s // tm,),
        in_specs=[_rows(tm, d), _rows(tm, d), _rows(tm, d), _rows(tm, 2 * hid),
                  _const(tab.shape), _const(wfi.shape), _const(wfo.shape)],
        out_specs=(_rows(tm, d), _rows(tm, d), _rows(tm, 2 * hid), _rows(tm, hid),
                   pl.BlockSpec((8, d), lambda i: (0, 0))),
        out_shape=(jax.ShapeDtypeStruct((s, d), F32), jax.ShapeDtypeStruct((s, d), BF16),
                   jax.ShapeDtypeStruct((s, 2 * hid), BF16), jax.ShapeDtypeStruct((s, hid), BF16),
                   jax.ShapeDtypeStruct((8, d), F32)),
        compiler_params=_params(VMEM_BIG),
    )(dxo, x, y, gu, tab, wfi, wfo)


def _attn_out_bwd(dx, y, tab, wo, tm):
    s, d = y.shape
    base = 6

    def body(dx_ref, y_ref, t_ref, wo_ref, dyb_ref, do_ref, sums_ref):
        @pl.when(pl.program_id(0) == 0)
        def _():
            sums_ref[...] = jnp.zeros_like(sums_ref)

        dxv = dx_ref[...]
        yv = y_ref[...]
        ry = _rs(yv)
        ny = yv * ry
        sums_ref[R_P1:R_P1 + 1, :] += _colsum(dxv * ny)
        dyb = _norm_bwd(dxv * t_ref[base + R_P1:base + R_P1 + 1, :], ny, ry).astype(BF16)
        dyb_ref[...] = dyb
        do_ref[...] = lax.dot_general(dyb, wo_ref[...], NT, preferred_element_type=F32).astype(BF16)

    return pl.pallas_call(
        body, name="attn_out_bwd", grid=(s // tm,),
        in_specs=[_rows(tm, d), _rows(tm, d), _const(tab.shape), _const(wo.shape)],
        out_specs=(_rows(tm, d), _rows(tm, d), pl.BlockSpec((8, d), lambda i: (0, 0))),
        out_shape=(jax.ShapeDtypeStruct((s, d), BF16), jax.ShapeDtypeStruct((s, d), BF16),
                   jax.ShapeDtypeStruct((8, d), F32)),
        compiler_params=_params(VMEM_BIG),
    )(dx, y, tab, wo)


def _attn_bwd(q, k, v, o, do, lse, bias):
    s, d = q.shape
    per = ATTN_PER
    npair, nb = d // LANES, s // (per * QB)
    hpp = LANES // HEAD_DIM
    nwin = 2 + per
    nbias = min(per, BIAS_VARIANTS)

    def body(*refs):
        q_ref, k_refs, v_refs = refs[0], refs[1:1 + nwin], refs[1 + nwin:1 + 2 * nwin]
        o_ref, do_ref, lse_ref = refs[1 + 2 * nwin:4 + 2 * nwin]
        bias_refs = refs[4 + 2 * nwin:4 + 2 * nwin + nbias]
        dq_ref, dk_ref, dv_ref, db_ref = refs[4 + 2 * nwin + nbias:]
        b = pl.program_id(1)

        @pl.when(b == 0)
        def _():
            dk_ref[...] = jnp.zeros_like(dk_ref)
            dv_ref[...] = jnp.zeros_like(dv_ref)
            db_ref[...] = jnp.zeros_like(db_ref)

        ks = [r[...] for r in k_refs]
        vs = [r[...] for r in v_refs]
        masks = _head_masks()
        for sub in range(per):
            rows = slice(sub * QB, (sub + 1) * QB)
            qv = q_ref[rows, :]
            dov = do_ref[rows, :]
            lsev = lse_ref[rows, :]
            doo = dov.astype(F32) * o_ref[rows, :].astype(F32)
            kwin = jnp.concatenate(ks[sub:sub + 3], axis=0)
            vwin = jnp.concatenate(vs[sub:sub + 3], axis=0)
            dq = jnp.zeros((QB, LANES), F32)
            dkt = jnp.zeros((LANES, KW), F32)
            dvt = jnp.zeros((LANES, KW), F32)
            for hh in range(hpp):
                qm = jnp.where(masks[hh], qv, jnp.zeros_like(qv))
                dom = jnp.where(masks[hh], dov, jnp.zeros_like(dov))
                km = jnp.where(masks[hh], kwin, jnp.zeros_like(kwin))
                lse_h = jnp.max(jnp.where(masks[hh], lsev, NEG), axis=-1, keepdims=True)
                delta = jnp.sum(jnp.where(masks[hh], doo, 0.0), axis=-1, keepdims=True)
                sc = lax.dot_general(qm, kwin, NT, preferred_element_type=F32) + bias_refs[min(sub, nbias - 1)][hh]
                p = jnp.exp(sc - lse_h)
                dp = lax.dot_general(dom, vwin, NT, preferred_element_type=F32)
                ds = p * (dp - delta)
                db_ref[hh] += ds
                dsb = ds.astype(BF16)
                dq = dq + jnp.dot(dsb, km, preferred_element_type=F32)
                dkt = dkt + jnp.dot(qm.T, dsb, preferred_element_type=F32)
                dvt = dvt + jnp.dot(dom.T, p.astype(BF16), preferred_element_type=F32)
            dkw, dvw = dkt.T, dvt.T
            dq_ref[rows, :] = (dq * (HEAD_DIM ** -0.5)).astype(BF16)
            for w in range(3):
                start = pl.multiple_of(jnp.maximum(per * b + sub - 2 + w, 0) * QB, QB)
                dk_ref[pl.ds(start, QB), :] += dkw[w * QB:(w + 1) * QB, :]
                dv_ref[pl.ds(start, QB), :] += dvw[w * QB:(w + 1) * QB, :]

    blk = pl.BlockSpec((per * QB, LANES), lambda p, b: (b, p))
    col = pl.BlockSpec((s, LANES), lambda p, b: (0, p))
    pair = pl.BlockSpec((hpp, QB, KW), lambda p, b: (p, 0, 0))
    return pl.pallas_call(
        body, name="attn_bwd", grid=(npair, nb),
        in_specs=[blk] + _window_specs(per) + _window_specs(per) + [blk, blk, blk]
                 + [_bias_spec(per, sub) for sub in range(nbias)],
        out_specs=(blk, col, col, pair),
        out_shape=(jax.ShapeDtypeStruct((s, d), BF16), jax.ShapeDtypeStruct((s, d), F32),
                   jax.ShapeDtypeStruct((s, d), F32), jax.ShapeDtypeStruct(bias.shape[1:], F32)),
        compiler_params=_params(VMEM_BIG),
    )(q, *([k] * nwin), *([v] * nwin), o, do, lse, *([bias] * nbias))


def _qkv_bwd(dres, dq, dk, dv, x, tab, wq, wkv, tm):
    s, d = x.shape
    base = 6

    def body(dres_ref, dq_ref, dk_ref, dv_ref, x_ref, t_ref, wq_ref, wkv_ref, dx_ref, dkv_ref, sums_ref):
        @pl.when(pl.program_id(0) == 0)
        def _():
            sums_ref[...] = jnp.zeros_like(sums_ref)

        dh1 = lax.dot_general(dq_ref[...], wq_ref[...], NT, preferred_element_type=F32)
        dkv_ref[:, 0:d] = dk_ref[...].astype(BF16)
        dkv_ref[:, d:2 * d] = dv_ref[...].astype(BF16)
        dhkv = jnp.dot(dkv_ref[...], wkv_ref[...], preferred_element_type=F32)
        xv = x_ref[...]
        r = _rs(xv)
        n = xv * r
        sums_ref[0:1, :] += _colsum(dh1 * n)
        sums_ref[1:2, :] += _colsum(dh1)
        sums_ref[2:3, :] += _colsum(dhkv * n)
        sums_ref[3:4, :] += _colsum(dhkv)
        dn = dh1 * t_ref[base + R_W1:base + R_W1 + 1, :] + dhkv * t_ref[R_KV:R_KV + 1, :]
        dx_ref[...] = dres_ref[...] + _norm_bwd(dn, n, r)

    return pl.pallas_call(
        body, name="qkv_bwd", grid=(s // tm,),
        in_specs=[_rows(tm, d)] * 5 + [_const(tab.shape), _const(wq.shape), _const(wkv.shape)],
        out_specs=(_rows(tm, d), _rows(tm, 2 * d), pl.BlockSpec((8, d), lambda i: (0, 0))),
        out_shape=(jax.ShapeDtypeStruct((s, d), F32), jax.ShapeDtypeStruct((s, 2 * d), BF16),
                   jax.ShapeDtypeStruct((8, d), F32)),
        compiler_params=_params(VMEM_BIG),
    )(dres, dq, dk, dv, x, tab, wq, wkv)


def _conv_bwd(dx1, x, y, bcx, tab, ck, wci, wco, tm):
    s, d = x.shape
    nt = s // tm

    def rev(i):
        return (nt - 1 - i, 0)

    def halo(i):
        return (jnp.maximum((nt - 1 - i) * (tm // 8) - 1, 0), 0)

    def body(dx_ref, x_ref, y_ref, bcx_ref, halo_ref, t_ref, ck_ref, wci_ref, wco_ref,
             dx0_ref, dyb_ref, dbcx_ref, sums_ref, dck_ref, carry):
        i = pl.program_id(0)

        @pl.when(i == 0)
        def _():
            sums_ref[...] = jnp.zeros_like(sums_ref)
            dck_ref[...] = jnp.zeros_like(dck_ref)
            carry[...] = jnp.zeros_like(carry)

        dxv = dx_ref[...]
        yv = y_ref[...]
        ry = _rs(yv)
        ny = yv * ry
        sums_ref[R_P1:R_P1 + 1, :] += _colsum(dxv * ny)
        dyb = _norm_bwd(dxv * t_ref[R_P1:R_P1 + 1, :], ny, ry).astype(BF16)
        dyb_ref[...] = dyb
        du = lax.dot_general(dyb, wco_ref[...], NT, preferred_element_type=F32)
        bg, cg, xi = bcx_ref[:, 0:d], bcx_ref[:, d:2 * d], bcx_ref[:, 2 * d:3 * d]
        z = cg * xi
        zp = halo_ref[:, d:2 * d] * halo_ref[:, 2 * d:3 * d]
        zp = jnp.where(i == nt - 1, jnp.zeros_like(zp), zp)
        row = lax.broadcasted_iota(jnp.int32, z.shape, 0)
        c1, c2 = zp[7:8, :], zp[6:7, :]
        z1 = jnp.where(row == 0, c1, pltpu.roll(z, 1, 0))
        z2 = jnp.where(row == 0, c2, jnp.where(row == 1, c1, pltpu.roll(z, 2, 0)))
        k0, k1, k2 = ck_ref[0:1, :], ck_ref[1:2, :], ck_ref[2:3, :]
        conv = k0 * z2 + k1 * z1 + k2 * z
        dconv = du * bg
        dck_ref[0:1, :] += _colsum(dconv * z2)
        dck_ref[1:2, :] += _colsum(dconv * z1)
        dck_ref[2:3, :] += _colsum(dconv * z)
        n1, n2 = carry[0:1, :], carry[1:2, :]
        d1 = jnp.where(row == tm - 1, n1, pltpu.roll(dconv, tm - 1, 0))
        d2 = jnp.where(row == tm - 1, n2, jnp.where(row == tm - 2, n1, pltpu.roll(dconv, tm - 2, 0)))
        carry[...] = dconv[0:8, :]
        dz = k2 * dconv + k1 * d1 + k0 * d2
        dbcx_ref[:, 0:d] = (du * conv).astype(BF16)
        dbcx_ref[:, d:2 * d] = (dz * xi).astype(BF16)
        dbcx_ref[:, 2 * d:3 * d] = (dz * cg).astype(BF16)
        dh = jnp.dot(dbcx_ref[...], wci_ref[...], preferred_element_type=F32)
        xv = x_ref[...]
        r = _rs(xv)
        n = xv * r
        sums_ref[R_W1:R_W1 + 1, :] += _colsum(dh * n)
        sums_ref[R_SH1:R_SH1 + 1, :] += _colsum(dh)
        dx0_ref[...] = dxv + _norm_bwd(dh * t_ref[R_W1:R_W1 + 1, :], n, r)

    rrow = lambda cols: pl.BlockSpec((tm, cols), rev)
    acc = pl.BlockSpec((8, d), lambda i: (0, 0))
    return pl.pallas_call(
        body, name="conv_bwd", grid=(nt,),
        in_specs=[rrow(d), rrow(d), rrow(d), rrow(3 * d), pl.BlockSpec((8, 3 * d), halo),
                  _const(tab.shape), _const(ck.shape), _const(wci.shape), _const(wco.shape)],
        out_specs=(rrow(d), rrow(d), rrow(3 * d), acc, acc),
        out_shape=(jax.ShapeDtypeStruct((s, d), F32), jax.ShapeDtypeStruct((s, d), BF16),
                   jax.ShapeDtypeStruct((s, 3 * d), BF16), jax.ShapeDtypeStruct((8, d), F32),
                   jax.ShapeDtypeStruct((8, d), F32)),
        scratch_shapes=[pltpu.VMEM((8, d), F32)],
        compiler_params=_params(VMEM_BIG),
    )(dx1, x, y, bcx, bcx, tab, ck, wci, wco)


def _wgrad_wide(a, b, nblk, tk, name):
    s, m = a.shape
    n = b.shape[1] // nblk
    nk = s // tk

    def body(a_ref, b_ref, o_ref, acc):
        kk = pl.program_id(0)

        @pl.when(kk == 0)
        def _():
            acc[...] = jnp.zeros_like(acc)

        acc[...] += jnp.dot(a_ref[...].T, b_ref[...], preferred_element_type=F32)

        @pl.when(kk == nk - 1)
        def _():
            for j in range(nblk):
                o_ref[j] = acc[:, j * n:(j + 1) * n].astype(BF16)

    return pl.pallas_call(
        body, name=name, grid=(nk,),
        in_specs=[pl.BlockSpec((tk, m), lambda kk: (kk, 0)), pl.BlockSpec((tk, nblk * n), lambda kk: (kk, 0))],
        out_specs=pl.BlockSpec((nblk, m, n), lambda kk: (0, 0, 0)),
        out_shape=jax.ShapeDtypeStruct((nblk, m, n), BF16),
        scratch_shapes=[pltpu.VMEM((m, nblk * n), F32)],
        compiler_params=_params(VMEM_BIG),
    )(a, b)


def _wgrad_rows(a, b, ncb, tk, name):
    s, m = a.shape
    n = b.shape[1]
    mb = m // ncb
    nk = s // tk

    def body(a_ref, b_ref, o_ref, acc):
        kk = pl.program_id(1)

        @pl.when(kk == 0)
        def _():
            acc[...] = jnp.zeros_like(acc)

        acc[...] += jnp.dot(a_ref[...].T, b_ref[...], preferred_element_type=F32)

        @pl.when(kk == nk - 1)
        def _():
            o_ref[...] = acc[...].astype(BF16)

    return pl.pallas_call(
        body, name=name, grid=(ncb, nk),
        in_specs=[pl.BlockSpec((tk, mb), lambda j, kk: (kk, j)), pl.BlockSpec((tk, n), lambda j, kk: (kk, 0))],
        out_specs=pl.BlockSpec((mb, n), lambda j, kk: (j, 0)),
        out_shape=jax.ShapeDtypeStruct((m, n), BF16),
        scratch_shapes=[pltpu.VMEM((mb, n), F32)],
        compiler_params=_params(VMEM_BIG),
    )(a, b)


def _adamw_math(w, g, m, v):
    m = ADAM_B1 * m + (1.0 - ADAM_B1) * g
    v = ADAM_B2 * v + (1.0 - ADAM_B2) * (g * g)
    m_hat = m / (1.0 - ADAM_B1 ** ADAM_STEP)
    v_hat = v / (1.0 - ADAM_B2 ** ADAM_STEP)
    delta = -ADAM_LR * (m_hat / (jnp.sqrt(v_hat) + ADAM_EPS) + ADAM_WD * w)
    return delta, m, v


def _adamw_reduce(parts, w, m, v, tr, name, parts_t=False):
    nl, r, c = w.shape
    tr = r if (parts_t and r % LANES) else (LANES if parts_t else _row_tile(r, tr))

    def body(*refs):
        p_refs = refs[:nl]
        w_ref, m_ref, v_ref, g_ref, d_ref, mo_ref, vo_ref = refs[nl:]
        layer = pl.program_id(0)

        def partial(i):
            val = p_refs[0][i].astype(F32)
            for q in range(1, nl):
                val = jnp.where(layer == q, p_refs[q][i].astype(F32), val)
            return val

        g = partial(0)
        for i in range(1, N_DEV):
            g = g + partial(i)
        if parts_t:
            g = g.T
        g_ref[...] = g
        d_ref[...], mo_ref[...], vo_ref[...] = _adamw_math(w_ref[...], g, m_ref[...], v_ref[...])

    blk = pl.BlockSpec((None, tr, c), lambda l, i: (l, i, 0))
    out = jax.ShapeDtypeStruct((nl, r, c), F32)
    if parts_t:
        p_specs = [pl.BlockSpec((N_DEV, c, tr), lambda l, i: (0, 0, i))]
    else:
        p_specs = [pl.BlockSpec((N_DEV, tr, c), (lambda l, i, q=q: (0, jnp.where(l == q, i, 0), 0)))
                   for q in range(nl)]
    return pl.pallas_call(
        body, name=name, grid=(nl, r // tr),
        in_specs=p_specs + [blk, blk, blk],
        out_specs=(blk,) * 4, out_shape=(out,) * 4,
        compiler_params=_params(VMEM_BIG),
    )(*parts, w, m, v)


def _adamw_outer(sct, dm, w, m, v, tr, name):
    nl, d, c = w.shape

    def body(s_ref, dm_ref, w_ref, m_ref, v_ref, g_ref, d_ref, mo_ref, vo_ref):
        g = jnp.dot(s_ref[...], dm_ref[...], preferred_element_type=F32)
        g_ref[...] = g
        d_ref[...], mo_ref[...], vo_ref[...] = _adamw_math(w_ref[...], g, m_ref[...], v_ref[...])

    blk = pl.BlockSpec((None, tr, c), lambda l, i: (l, i, 0))
    out = jax.ShapeDtypeStruct((nl, d, c), F32)
    return pl.pallas_call(
        body, name=name, grid=(nl, d // tr),
        in_specs=[pl.BlockSpec((tr, N_DEV), lambda l, i: (i, 0)),
                  pl.BlockSpec((None, N_DEV, c), lambda l, i: (l, 0, 0)), blk, blk, blk],
        out_specs=(blk,) * 4, out_shape=(out,) * 4,
        compiler_params=_params(VMEM_BIG),
    )(sct, dm, w, m, v)


def _pad_rows(a, rows):
    return jnp.concatenate([a, jnp.zeros((rows - a.shape[0],) + a.shape[1:], a.dtype)], axis=0)


def kernel(x, c, mod_w, mod_b, norm_g, ffn_w_in, ffn_w_out, conv_w_in, conv_k, conv_w_out, kv_mod_w, kv_mod_b, kv_norm_g, w_kv, attn_w_q, attn_w_o, rel_bias, loss_target, m_mod_w, m_mod_b, m_norm_g, m_ffn_w_in, m_ffn_w_out, m_conv_w_in, m_conv_k, m_conv_w_out, m_kv_mod_w, m_kv_mod_b, m_kv_norm_g, m_w_kv, m_attn_w_q, m_attn_w_o, m_rel_bias, v_mod_w, v_mod_b, v_norm_g, v_ffn_w_in, v_ffn_w_out, v_conv_w_in, v_conv_k, v_conv_w_out, v_kv_mod_w, v_kv_mod_b, v_kv_norm_g, v_w_kv, v_attn_w_q, v_attn_w_o, v_rel_bias):
    s, d = x.shape[1], x.shape[2]
    dq = d // LANES
    dsh = d // N_DEV
    nl = mod_w.shape[0]
    mw = mod_w.shape[2]
    kmw = kv_mod_w.shape[1]
    fw = ffn_w_in.shape[2]
    nh, nrel = rel_bias.shape[1], rel_bias.shape[2]
    tm = min(256, s)
    tm2 = min(512, s)
    tk = min(2048, s)
    me = 4 * lax.axis_index("x") + 2 * lax.axis_index("y") + lax.axis_index("c")

    x0 = x[0]
    tgt = loss_target[0]

    small1 = jnp.concatenate([c.reshape(dq, LANES), norm_g.reshape(dq, LANES),
                              _pad_rows(conv_k[0], 8).reshape(dq, LANES)], axis=0)
    (sm,) = _exchange([small1], ["gather"], "gather_small")
    c_all = sm[:, 0:dq].reshape(N_DEV, d)
    ng_full = jnp.transpose(sm[:, dq:2 * dq].reshape(N_DEV, 8, dsh), (1, 0, 2)).reshape(8, d)
    ck_full = jnp.transpose(sm[:, 2 * dq:3 * dq].reshape(N_DEV, 8, dsh), (1, 0, 2)).reshape(8, d)

    modcols, silu_c = _mod_fwd(c_all, mod_w, kv_mod_w)
    (modall,) = _exchange([modcols], ["gather"], "gather_mod")

    cast = lambda *ws: [a.astype(BF16) for a in ws]
    gath = lambda ws: (ws, ["gather"] * len(ws))
    half = lambda ws: (ws, ["gather_half"] * len(ws))
    (h_conv, h_ffn0, h_attn, h_ffn1), token = _xstart(
        [half(cast(conv_w_in[0].T, conv_w_out[0])), half(cast(jnp.swapaxes(ffn_w_in[0], 0, 1), ffn_w_out[0])),
         gath(cast(w_kv.T, attn_w_q[0], attn_w_o[0])), gath(cast(jnp.swapaxes(ffn_w_in[1], 0, 1), ffn_w_out[1]))],
        modall, "gather_start")
    modall = modall + token[0, 0]
    mine = lax.dynamic_index_in_dim(modall, me, axis=1, keepdims=False)
    modrow = jnp.stack([mine[:, l * mw:(l + 1) * mw].reshape(6, d) for l in range(nl)])
    kvrow = mine[:, nl * mw:nl * mw + kmw].reshape(2, d)
    tab, modval = _vec_prep(modrow, mod_b.reshape(nl, 6, d), kvrow, kv_mod_b.reshape(2, d), ng_full,
                            kv_norm_g.reshape(1, d))
    bias = _bias_fwd(rel_bias[0])

    wci, wco = _xwait(h_conv, [bias], "gather_wait_conv")
    wci, wco = _forward_to_sibling([wci, wco], "gather_forward_conv")
    wci, wco = wci.reshape(3 * d, d), wco.reshape(d, d)
    x1, h1a, bcx, ua, ya = _conv_fwd(x0, tab, ck_full, wci, wco, tm2)
    wfi0, wfo0 = _xwait(h_ffn0, [x1], "gather_wait_ffn0")
    wfi0, wfo0 = _forward_to_sibling([wfi0, wfo0], "gather_forward_ffn0")
    wfi0, wfo0 = wfi0.reshape(-1, d), wfo0.reshape(-1, d)
    x2, h2a, gua, y2a = _ffn_fwd(x1, tab, 0, wfi0, wfo0, None, tm2, "ffn_fwd0")
    wkv, wq, wo = _xwait(h_attn, [x2], "gather_wait_attn")
    wkv, wq, wo = wkv.reshape(2 * d, d), wq.reshape(d, d), wo.reshape(d, d)
    hkv, h1b, q, k, v = _qkv_fwd(x2, tab, wq, wkv, tm2)
    o, lse = _attn_fwd(q, k, v, bias)
    x3, yb = _attn_out_fwd(o, x2, tab, wo, tm2)
    wfi1, wfo1 = _xwait(h_ffn1, [x3], "gather_wait_ffn1")
    wfi1, wfo1 = wfi1.reshape(-1, d), wfo1.reshape(-1, d)
    dx4, h2b, gub, y2b, loss_acc = _ffn_fwd(x3, tab, 6, wfi1, wfo1, tgt, tm2, "ffn_fwd1")

    scat = lambda ws: [(ws, ["scatter"] * len(ws))]
    dx3, dy2b, dgub, ab, sums_f1 = _ffn_bwd(dx4, x3, y2b, gub, tab, 6, wfi1, wfo1, tm, "ffn_bwd1")
    g_wfi1 = _wgrad_rows(dgub, h2b, 4, tk, "wgrad_ffn_in1").reshape(N_DEV, -1, d)
    g_wfo1 = _wgrad_rows(ab, dy2b, 2, tk, "wgrad_ffn_out1").reshape(N_DEV, -1, d)
    (h_g1,), token = _xstart(scat([g_wfi1, g_wfo1]), dx3, "grads_start_ffn1")
    tab = tab + token[0, 0]
    dyb, do, sums_o = _attn_out_bwd(dx3, yb, tab, wo, tm2)
    g_wo = _wgrad_wide(o, dyb, 1, tk, "wgrad_o").reshape(N_DEV, dsh, d)
    dqb, dk, dv, dbias = _attn_bwd(q, k, v, o, do, lse, bias)
    g_wq = _wgrad_wide(h1b, dqb, 1, tk, "wgrad_q").reshape(N_DEV, dsh, d)
    dx2, dkvb, sums_q = _qkv_bwd(dx3, dqb, dk, dv, x2, tab, wq, wkv, tm2)
    g_wkv = _wgrad_rows(dkvb, hkv, 2, tk, "wgrad_kv").reshape(N_DEV, -1, d)
    (h_g2,), token = _xstart(scat([g_wkv, g_wq, g_wo]), dx2, "grads_start_attn")
    tab = tab + token[0, 0]
    dx1, dy2a, dgua, aa, sums_f0 = _ffn_bwd(dx2, x1, y2a, gua, tab, 0, wfi0, wfo0, tm, "ffn_bwd0")
    g_wfi0 = _wgrad_rows(dgua, h2a, 4, tk, "wgrad_ffn_in0").reshape(N_DEV, -1, d)
    g_wfo0 = _wgrad_rows(aa, dy2a, 2, tk, "wgrad_ffn_out0").reshape(N_DEV, -1, d)
    (h_g3,), token = _xstart(scat([g_wfi0, g_wfo0]), dx1, "grads_start_ffn0")
    tab = tab + token[0, 0]
    dx0, dya, dbcx, sums_c, dck = _conv_bwd(dx1, x0, ya, bcx, tab, ck_full, wci, wco, tm2)
    drel = _bias_bwd(dbias, nrel)
    dmod, dng, dkvg = _vec_bwd(sums_c, sums_f0, sums_q, sums_o, sums_f1, modval, ng_full, kv_norm_g.reshape(1, d))

    relw = -(-nrel // LANES) * LANES
    drel_p = jnp.concatenate([drel, jnp.zeros((nh, relw - nrel), F32)], axis=1)
    small3 = jnp.concatenate([dmod.reshape(16 * dq, LANES), dng.reshape(8 * dq, LANES), dkvg.reshape(8 * dq, LANES),
                              dck.reshape(8 * dq, LANES), loss_acc,
                              drel_p.reshape(nh * relw // LANES, LANES)], axis=0)
    (sm,) = _exchange([small3], ["gather"], "gather_small_grads")
    g_wci = _wgrad_rows(dbcx, h1a, 3, tk, "wgrad_conv_in").reshape(N_DEV, -1, d)
    g_wco = _wgrad_wide(ua, dya, 1, tk, "wgrad_conv_out").reshape(N_DEV, dsh, d)
    (h_g4,), token = _xstart(scat([g_wci, g_wco]), sm, "grads_start_conv")
    sm = sm + token[0, 0]
    o1, o2, o3, o4, o5 = 16 * dq, 24 * dq, 32 * dq, 40 * dq, 40 * dq + 8
    loss = jnp.sum(sm[:, o4:o5, :]) * (0.5 / d)
    dmod_all = sm[:, 0:o1].reshape(N_DEV, 16, d)
    mine_cols = lambda a: lax.dynamic_slice_in_dim(a, me * dsh, dsh, axis=2)
    dng_parts = mine_cols(sm[:, o1:o2].reshape(N_DEV, 8, d))
    dkvg_parts = sm[:, o2:o3].reshape(N_DEV, 8, d)[:, 0:1]
    dck_parts = mine_cols(sm[:, o3:o4].reshape(N_DEV, 8, d))[:, 0:3]
    drel_parts = sm[:, o5:].reshape(N_DEV, nh, relw)[:, :, 0:nrel]

    def update(parts, w, m, v, name, layers=1, parts_t=False):
        shp = w.shape
        w3, m3, v3 = (a.reshape(layers, -1, shp[-1]) for a in (w, m, v))
        if not parts_t:
            parts = [p.reshape(N_DEV, -1, shp[-1]) for p in parts]
        outs = _adamw_reduce(parts, w3, m3, v3, 256, name, parts_t)
        return [a.reshape(shp) for a in outs]

    p_wfi1, p_wfo1 = _xwait(h_g1, [sm], "grads_wait_ffn1")
    p_wfi0, p_wfo0 = _xwait(h_g3, [p_wfi1], "grads_wait_ffn0")
    tr = lambda a: jnp.swapaxes(a, 1, 2)
    u_ffn_in = [tr(a) for a in update([p_wfi0, p_wfi1], tr(ffn_w_in), tr(m_ffn_w_in), tr(v_ffn_w_in),
                                      "adamw_ffn_in", 2)]
    u_ffn_out = update([p_wfo0, p_wfo1], ffn_w_out, m_ffn_w_out, v_ffn_w_out, "adamw_ffn_out", 2)
    p_wkv, p_wq, p_wo = _xwait(h_g2, [u_ffn_out[0]], "grads_wait_attn")
    u_w_kv = update([p_wkv], w_kv, m_w_kv, v_w_kv, "adamw_w_kv", parts_t=True)
    u_w_q = update([p_wq], attn_w_q, m_attn_w_q, v_attn_w_q, "adamw_w_q")
    u_w_o = update([p_wo], attn_w_o, m_attn_w_o, v_attn_w_o, "adamw_w_o")

    sct = jnp.transpose(silu_c)
    dm_mod = jnp.stack([lax.dynamic_slice_in_dim(dmod_all[:, 6 * l:6 * l + 6].reshape(N_DEV, 6 * d), me * mw, mw, axis=1)
                        for l in range(nl)]).astype(BF16)
    dm_kv = lax.dynamic_slice_in_dim(dmod_all[:, R_KV:R_KV + 2].reshape(N_DEV, 2 * d), me * kmw, kmw, axis=1)
    u_mod_w = _adamw_outer(sct, dm_mod, mod_w, m_mod_w, v_mod_w, min(256, d), "adamw_mod_w")
    u_kv_mod_w = [a[0] for a in _adamw_outer(sct, dm_kv.astype(BF16)[None], kv_mod_w[None], m_kv_mod_w[None],
                                             v_kv_mod_w[None], min(256, d), "adamw_kv_mod_w")]

    modb_parts = jnp.stack([dmod_all[:, 6 * l:6 * l + 6].reshape(N_DEV, 6 * d) for l in range(nl)], axis=1)
    u_mod_b = update([modb_parts], mod_b, m_mod_b, v_mod_b, "adamw_mod_b")
    u_norm_g = update([dng_parts], norm_g.reshape(8, dsh), m_norm_g.reshape(8, dsh), v_norm_g.reshape(8, dsh), "adamw_norm_g")
    u_norm_g = [a.reshape(norm_g.shape) for a in u_norm_g]
    u_conv_k = update([dck_parts], conv_k, m_conv_k, v_conv_k, "adamw_conv_k")
    kvb_parts = dmod_all[:, R_KV:R_KV + 2].reshape(N_DEV, 1, 2 * d)
    u_kv_mod_b = [a.reshape(kv_mod_b.shape) for a in update([kvb_parts], kv_mod_b.reshape(1, -1), m_kv_mod_b.reshape(1, -1),
                                                            v_kv_mod_b.reshape(1, -1), "adamw_kv_mod_b")]
    u_kv_norm_g = [a.reshape(kv_norm_g.shape) for a in update([dkvg_parts], kv_norm_g.reshape(1, -1), m_kv_norm_g.reshape(1, -1),
                                                              v_kv_norm_g.reshape(1, -1), "adamw_kv_norm_g")]
    u_rel = update([drel_parts], rel_bias, m_rel_bias, v_rel_bias, "adamw_rel_bias")

    others = [u_ffn_in, u_ffn_out, u_w_kv, u_w_q, u_w_o, u_mod_w, u_kv_mod_w, u_mod_b, u_norm_g, u_conv_k, u_kv_mod_b,
              u_kv_norm_g, u_rel]
    p_wci, p_wco = _xwait(h_g4, [u[3] for u in others], "grads_wait_conv")
    u_conv_in = update([p_wci], conv_w_in, m_conv_w_in, v_conv_w_in, "adamw_conv_in", parts_t=True)
    u_conv_out = update([p_wco], conv_w_out, m_conv_w_out, v_conv_w_out, "adamw_conv_out")

    ups = [u_mod_w, u_mod_b, u_norm_g, u_ffn_in, u_ffn_out, u_conv_in, u_conv_k, u_conv_out, u_kv_mod_w, u_kv_mod_b,
           u_kv_norm_g, u_w_kv, u_w_q, u_w_o, u_rel]
    return (loss, dx0[None], *[u[0] for u in ups], *[u[1] for u in ups], *[u[2] for u in ups], *[u[3] for u in ups])
```

```python
import jax
import jax.numpy as jnp
from jax import lax
from jax.experimental import pallas as pl
from jax.experimental.pallas import tpu as pltpu

F32 = jnp.float32
BF16 = jnp.bfloat16

EPS = 1e-6
CHUNK = 64
HEAD_DIM = 64
N_LEFT = 8
LANES = 128
MXU = 256
FFN_CHUNK = 4
QB = 4 * CHUNK
KW = QB + N_LEFT * CHUNK
BIAS_VARIANTS = N_LEFT * CHUNK // QB + 1
ATTN_PER = 8
NEG = -1e30
N_DEV = 8

ADAM_LR = 0.001
ADAM_B1 = 0.9
ADAM_B2 = 0.999
ADAM_EPS = 1e-08
ADAM_WD = 0.01
ADAM_STEP = 10

VMEM_BIG = 56 * 1024 * 1024

NT = (((1,), (1,)), ((), ()))
TN = (((0,), (0,)), ((), ()))

R_W1, R_SH1, R_P1, R_W2, R_SH2, R_P2 = range(6)
R_KV = 12


def _params(vmem):
    return pltpu.CompilerParams(vmem_limit_bytes=vmem)


def _row_tile(rows, cap):
    for t in range(min(cap, rows) // 16 * 16, 0, -16):
        if rows % t == 0:
            return t
    return rows


def _rows(tm, cols):
    return pl.BlockSpec((tm, cols), lambda i: (i, 0))


def _const(shape):
    nd = len(shape)
    return pl.BlockSpec(shape, lambda *_: (0,) * nd, pipeline_mode=pl.Buffered(1))


def _rs(x):
    return lax.rsqrt(jnp.mean(x * x, axis=-1, keepdims=True) + EPS)


def _norm_bwd(d, n, r):
    return r * (d - n * jnp.mean(d * n, axis=-1, keepdims=True))


def _colsum(a):
    return jnp.sum(a, axis=0, keepdims=True)


def _sigmoid(g):
    return 1.0 / (1.0 + jnp.exp(-g))


def _exchange(arrays, modes, name):
    n = len(arrays)
    out_shape = []
    for a, mode in zip(arrays, modes):
        shp = (N_DEV,) + a.shape if mode == "gather" else a.shape
        out_shape.append(jax.ShapeDtypeStruct(shp, a.dtype))

    def body(*refs):
        ins, outs = refs[:n], refs[n:2 * n]
        send_sems, recv_sems, local_sems = refs[2 * n:]
        x, y, c = lax.axis_index("x"), lax.axis_index("y"), lax.axis_index("c")
        me = 4 * x + 2 * y + c
        local, sends, recvs = [], [], []
        for a in range(n):
            own = ins[a] if modes[a] == "gather" else ins[a].at[me]
            cp = pltpu.make_async_copy(own, outs[a].at[me], local_sems.at[a])
            cp.start()
            local.append(cp)
        for k in range(1, N_DEV):
            px = 1 - x if k & 4 else x
            py = 1 - y if k & 2 else y
            pc = 1 - c if k & 1 else c
            peer = 4 * px + 2 * py + pc
            for a in range(n):
                src = ins[a] if modes[a] == "gather" else ins[a].at[peer]
                sem = a * (N_DEV - 1) + k - 1
                cp = pltpu.make_async_remote_copy(
                    src_ref=src, dst_ref=outs[a].at[me],
                    send_sem=send_sems.at[sem], recv_sem=recv_sems.at[sem],
                    device_id=(px, py, pc), device_id_type=pl.DeviceIdType.MESH)
                cp.start()
                sends.append(cp)
                recvs.append(pltpu.make_async_remote_copy(
                    src_ref=src, dst_ref=outs[a].at[peer],
                    send_sem=send_sems.at[sem], recv_sem=recv_sems.at[sem],
                    device_id=(px, py, pc), device_id_type=pl.DeviceIdType.MESH))
        for cp in recvs:
            cp.wait_recv()
        for cp in sends:
            cp.wait_send()
        for cp in local:
            cp.wait()

    any_spec = pl.BlockSpec(memory_space=pl.ANY)
    return pl.pallas_call(
        body, name=name,
        out_shape=tuple(out_shape),
        in_specs=[any_spec] * n,
        out_specs=tuple([any_spec] * n),
        scratch_shapes=[
            pltpu.SemaphoreType.DMA((n * (N_DEV - 1),)),
            pltpu.SemaphoreType.DMA((n * (N_DEV - 1),)),
            pltpu.SemaphoreType.DMA((n,)),
        ],
    )(*arrays)


def _peers(x, y, c):
    out = []
    for k in range(1, N_DEV):
        px = 1 - x if k & 4 else x
        py = 1 - y if k & 2 else y
        pc = 1 - c if k & 1 else c
        out.append((k - 1, (px, py, pc), 4 * px + 2 * py + pc))
    return out


_OTHER_CORE_SLOTS = (2, 4, 6)


def _land_shape(a, mode):
    return a.shape if mode == "scatter" else (N_DEV,) + a.shape


_HBM = pl.BlockSpec(memory_space=pltpu.HBM)
_SEM = pl.BlockSpec(memory_space=pltpu.SEMAPHORE)
_EFFECT = pltpu.SideEffectType.DATAFLOW_SIDE_EFFECTING


def _xstart(groups, after, name):
    flat = [(a, m) for arrays, modes in groups for a, m in zip(arrays, modes)]
    n, ngr = len(flat), len(groups)
    sizes = [len(arrays) for arrays, _ in groups]
    npeer = N_DEV - 1

    def body(*refs):
        ins, lands = refs[:n], refs[n:2 * n]
        outs = refs[2 * n + 1:]
        sems = outs[:2 * ngr]
        token = outs[2 * ngr + 2 * n]
        local_sems = outs[2 * ngr + 2 * n + 1]
        stage = outs[2 * ngr + 2 * n + 2:]
        x, y, c = lax.axis_index("x"), lax.axis_index("y"), lax.axis_index("c")
        me = 4 * x + 2 * y + c
        loads, stores = [], []
        for a in range(n):
            own = ins[a].at[me] if flat[a][1] == "scatter" else ins[a]
            loads.append(pltpu.make_async_copy(own, stage[a], local_sems.at[a]))
            stores.append(pltpu.make_async_copy(stage[a], lands[a].at[me], local_sems.at[a]))
            loads[a].start()
        for a in range(n):
            loads[a].wait()
            stores[a].start()
        a = 0
        for g in range(ngr):
            for j in range(sizes[g]):
                mode = flat[a][1]
                for slot, peer, pidx in _peers(x, y, c):
                    if mode == "gather_half" and slot in _OTHER_CORE_SLOTS:
                        continue
                    pltpu.make_async_remote_copy(
                        src_ref=ins[a].at[pidx] if mode == "scatter" else ins[a], dst_ref=lands[a].at[me],
                        send_sem=sems[2 * g].at[j * npeer + slot], recv_sem=sems[2 * g + 1].at[j * npeer + slot],
                        device_id=peer, device_id_type=pl.DeviceIdType.MESH).start()
                a += 1
        for cp in stores:
            cp.wait()
        token[...] = jnp.zeros_like(token)

    out_shape, out_specs = [], []
    for sz in sizes:
        out_shape += [pltpu.SemaphoreType.DMA((sz * npeer,)), pltpu.SemaphoreType.DMA((sz * npeer,))]
        out_specs += [_SEM, _SEM]
    out_shape += [pltpu.HBM(a.shape, a.dtype) for a, _ in flat]
    out_shape += [pltpu.HBM(_land_shape(a, m), a.dtype) for a, m in flat]
    out_specs += [_HBM] * (2 * n)
    out_shape.append(jax.ShapeDtypeStruct((8, LANES), F32))
    out_specs.append(pl.BlockSpec(memory_space=pltpu.VMEM))
    args = [pltpu.with_memory_space_constraint(a, pltpu.HBM) for a, _ in flat]
    args += [pltpu.with_memory_space_constraint(lax.empty(_land_shape(a, m), a.dtype), pltpu.HBM) for a, m in flat]
    res = pl.pallas_call(
        body, name=name, out_shape=tuple(out_shape),
        in_specs=[_HBM] * (2 * n) + [pl.BlockSpec(memory_space=pl.ANY)], out_specs=tuple(out_specs),
        input_output_aliases={i: 2 * ngr + i for i in range(2 * n)},
        scratch_shapes=[pltpu.SemaphoreType.DMA((n,))]
                       + [pltpu.VMEM(a.shape[1:] if m == "scatter" else a.shape, a.dtype) for a, m in flat],
        compiler_params=pltpu.CompilerParams(has_side_effects=_EFFECT, vmem_limit_bytes=VMEM_BIG),
    )(*args, after)
    handles, a = [], 0
    for g, sz in enumerate(sizes):
        handles.append((res[2 * g], res[2 * g + 1], list(res[2 * ngr + a:2 * ngr + a + sz]),
                        list(res[2 * ngr + n + a:2 * ngr + n + a + sz]), list(groups[g][1])))
        a += sz
    return handles, res[-1]


def _xwait(handle, after, name):
    send_sems, recv_sems, srcs, lands, modes = handle
    m = len(srcs)
    npeer = N_DEV - 1
    after = list(after)

    def body(*refs):
        ins, lnd = refs[:m], refs[m:2 * m]
        ssem, rsem = refs[2 * m], refs[2 * m + 1]
        x, y, c = lax.axis_index("x"), lax.axis_index("y"), lax.axis_index("c")
        for j in range(m):
            for slot, peer, pidx in _peers(x, y, c):
                if modes[j] == "gather_half" and slot in _OTHER_CORE_SLOTS:
                    continue
                cp = pltpu.make_async_remote_copy(
                    src_ref=ins[j].at[pidx] if modes[j] == "scatter" else ins[j], dst_ref=lnd[j].at[pidx],
                    send_sem=ssem.at[j * npeer + slot], recv_sem=rsem.at[j * npeer + slot],
                    device_id=peer, device_id_type=pl.DeviceIdType.MESH)
                cp.wait_send()
                cp.wait_recv()

    res = pl.pallas_call(
        body, name=name,
        out_shape=tuple([pltpu.HBM(a.shape, a.dtype) for a in srcs] + [pltpu.HBM(a.shape, a.dtype) for a in lands]),
        in_specs=[_HBM] * (2 * m) + [_SEM, _SEM] + [pl.BlockSpec(memory_space=pl.ANY)] * len(after),
        out_specs=tuple([_HBM] * (2 * m)),
        input_output_aliases={i: i for i in range(2 * m)},
        compiler_params=pltpu.CompilerParams(has_side_effects=_EFFECT),
    )(*srcs, *lands, send_sems, recv_sems, *after)
    return list(res[m:])


def _forward_to_sibling(lands, name):
    n = len(lands)

    def body(*refs):
        outs = refs[n:2 * n]
        send_sems, recv_sems = refs[2 * n:]
        x, y, c = lax.axis_index("x"), lax.axis_index("y"), lax.axis_index("c")
        sibling = (x, y, 1 - c)
        sends, recvs = [], []
        for a in range(n):
            for j, k in enumerate((2, 4, 6)):
                px = 1 - x if k & 4 else x
                py = 1 - y if k & 2 else y
                got = 4 * px + 2 * py + c
                missing = 4 * px + 2 * py + (1 - c)
                sem = a * 3 + j
                cp = pltpu.make_async_remote_copy(
                    src_ref=outs[a].at[got], dst_ref=outs[a].at[got],
                    send_sem=send_sems.at[sem], recv_sem=recv_sems.at[sem],
                    device_id=sibling, device_id_type=pl.DeviceIdType.MESH)
                cp.start()
                sends.append(cp)
                recvs.append(pltpu.make_async_remote_copy(
                    src_ref=outs[a].at[missing], dst_ref=outs[a].at[missing],
                    send_sem=send_sems.at[sem], recv_sem=recv_sems.at[sem],
                    device_id=sibling, device_id_type=pl.DeviceIdType.MESH))
        for cp in recvs:
            cp.wait_recv()
        for cp in sends:
            cp.wait_send()

    any_spec = pl.BlockSpec(memory_space=pl.ANY)
    return list(pl.pallas_call(
        body, name=name,
        out_shape=tuple(jax.ShapeDtypeStruct(a.shape, a.dtype) for a in lands),
        in_specs=[any_spec] * n, out_specs=tuple([any_spec] * n),
        input_output_aliases={i: i for i in range(n)},
        scratch_shapes=[pltpu.SemaphoreType.DMA((3 * n,)), pltpu.SemaphoreType.DMA((3 * n,))],
    )(*lands))


def _mod_fwd(c_all, mod_w, kv_mod_w):
    nl, d, mw = mod_w.shape
    kw = kv_mod_w.shape[1]

    def body(c_ref, mw_ref, kw_ref, o_ref, sc_ref):
        cc = c_ref[...]
        sc = (cc * _sigmoid(cc)).astype(BF16)
        sc_ref[...] = sc
        for l in range(nl):
            o_ref[:, l * mw:(l + 1) * mw] = jnp.dot(sc, mw_ref[l].astype(BF16), preferred_element_type=F32)
        o_ref[:, nl * mw:nl * mw + kw] = jnp.dot(sc, kw_ref[...].astype(BF16), preferred_element_type=F32)

    return pl.pallas_call(
        body, name="mod_fwd",
        out_shape=(jax.ShapeDtypeStruct((c_all.shape[0], nl * mw + kw), F32),
                   jax.ShapeDtypeStruct(c_all.shape, BF16)),
        compiler_params=_params(VMEM_BIG),
    )(c_all, mod_w, kv_mod_w)


def _vec_prep(modrow, modb, kvrow, kvb, ng, kvg):
    d = ng.shape[1]

    def body(mr_ref, mb_ref, kr_ref, kb_ref, ng_ref, kvg_ref, t_ref, m_ref):
        t_ref[...] = jnp.zeros_like(t_ref)
        m_ref[...] = jnp.zeros_like(m_ref)
        for l in range(2):
            mod = mr_ref[l] + mb_ref[l]
            m_ref[6 * l:6 * l + 6, :] = mod
            g = ng_ref[4 * l:4 * l + 4, :]
            t_ref[6 * l + R_W1:6 * l + R_W1 + 1, :] = g[0:1] * (1.0 + mod[1:2])
            t_ref[6 * l + R_SH1:6 * l + R_SH1 + 1, :] = mod[0:1]
            t_ref[6 * l + R_P1:6 * l + R_P1 + 1, :] = mod[2:3] * g[1:2]
            t_ref[6 * l + R_W2:6 * l + R_W2 + 1, :] = g[2:3] * (1.0 + mod[4:5])
            t_ref[6 * l + R_SH2:6 * l + R_SH2 + 1, :] = mod[3:4]
            t_ref[6 * l + R_P2:6 * l + R_P2 + 1, :] = mod[5:6] * g[3:4]
        kv = kr_ref[...] + kb_ref[...]
        m_ref[R_KV:R_KV + 2, :] = kv
        t_ref[R_KV:R_KV + 1, :] = kvg_ref[...] * (1.0 + kv[1:2])
        t_ref[R_KV + 1:R_KV + 2, :] = kv[0:1]

    return pl.pallas_call(
        body, name="vec_prep",
        out_shape=(jax.ShapeDtypeStruct((16, d), F32), jax.ShapeDtypeStruct((16, d), F32)),
    )(modrow, modb, kvrow, kvb, ng, kvg)


def _vec_bwd(sums_c, sums_f0, sums_q, sums_o, sums_f1, mt, ng, kvg):
    d = ng.shape[1]

    def body(sc_ref, sf0_ref, sq_ref, so_ref, sf1_ref, m_ref, ng_ref, kvg_ref, dm_ref, dng_ref, dkvg_ref, g_ref):
        g_ref[...] = jnp.zeros_like(g_ref)
        g_ref[0:3, :] = sc_ref[0:3, :]
        g_ref[3:6, :] = sf0_ref[3:6, :]
        g_ref[6:8, :] = sq_ref[0:2, :]
        g_ref[8:9, :] = so_ref[2:3, :]
        g_ref[9:12, :] = sf1_ref[3:6, :]
        g_ref[R_KV:R_KV + 2, :] = sq_ref[2:4, :]
        dm_ref[...] = jnp.zeros_like(dm_ref)
        dkvg_ref[...] = jnp.zeros_like(dkvg_ref)
        for l in range(2):
            g = ng_ref[4 * l:4 * l + 4, :]
            mod = m_ref[6 * l:6 * l + 6, :]
            s = g_ref[6 * l:6 * l + 6, :]
            dm_ref[6 * l + 0:6 * l + 1, :] = s[1:2]
            dm_ref[6 * l + 1:6 * l + 2, :] = s[0:1] * g[0:1]
            dm_ref[6 * l + 2:6 * l + 3, :] = s[2:3] * g[1:2]
            dm_ref[6 * l + 3:6 * l + 4, :] = s[4:5]
            dm_ref[6 * l + 4:6 * l + 5, :] = s[3:4] * g[2:3]
            dm_ref[6 * l + 5:6 * l + 6, :] = s[5:6] * g[3:4]
            dng_ref[4 * l + 0:4 * l + 1, :] = s[0:1] * (1.0 + mod[1:2])
            dng_ref[4 * l + 1:4 * l + 2, :] = s[2:3] * mod[2:3]
            dng_ref[4 * l + 2:4 * l + 3, :] = s[3:4] * (1.0 + mod[4:5])
            dng_ref[4 * l + 3:4 * l + 4, :] = s[5:6] * mod[5:6]
        dm_ref[R_KV:R_KV + 1, :] = g_ref[R_KV + 1:R_KV + 2, :]
        dm_ref[R_KV + 1:R_KV + 2, :] = g_ref[R_KV:R_KV + 1, :] * kvg_ref[...]
        dkvg_ref[0:1, :] = g_ref[R_KV:R_KV + 1, :] * (1.0 + m_ref[R_KV + 1:R_KV + 2, :])

    return pl.pallas_call(
        body, name="vec_bwd",
        out_shape=(jax.ShapeDtypeStruct((16, d), F32), jax.ShapeDtypeStruct((8, d), F32),
                   jax.ShapeDtypeStruct((8, d), F32)),
        scratch_shapes=[pltpu.VMEM((16, d), F32)],
    )(sums_c, sums_f0, sums_q, sums_o, sums_f1, mt, ng, kvg)


def _rel_index(nrel):
    width = KW + QB
    e = lax.broadcasted_iota(jnp.int32, (nrel, width), 1)
    r = lax.broadcasted_iota(jnp.int32, (nrel, width), 0)
    max_rel = (nrel - 1) // 2
    idx = jnp.clip(KW - e, -max_rel, max_rel) + max_rel
    return (idx == r).astype(F32)


def _band_valid():
    row = lax.broadcasted_iota(jnp.int32, (QB, KW), 0) // CHUNK
    col = lax.broadcasted_iota(jnp.int32, (QB, KW), 1) // CHUNK
    j = col - row
    return (j >= 0) & (j <= N_LEFT)


def _bias_fwd(rel_bias):
    nh, nrel = rel_bias.shape
    width = KW + QB

    def body(rb_ref, o_ref):
        onehot = _rel_index(nrel)
        gr = jnp.dot(rb_ref[...], onehot, preferred_element_type=F32, precision=lax.Precision.HIGHEST)
        valid = _band_valid() & _key_valid(pl.program_id(0))
        for h in range(nh):
            xrow = jnp.broadcast_to(gr[h:h + 1, :], (QB, width))
            rolled = pltpu.roll(xrow, 0, 1, stride=1, stride_axis=0)
            o_ref[h] = jnp.where(valid, rolled[:, QB:], NEG)

    return pl.pallas_call(
        body, name="bias_fwd", grid=(BIAS_VARIANTS,),
        in_specs=[pl.BlockSpec(rel_bias.shape, lambda v: (0, 0))],
        out_specs=pl.BlockSpec((None, nh, QB, KW), lambda v: (v, 0, 0, 0)),
        out_shape=jax.ShapeDtypeStruct((BIAS_VARIANTS, nh, QB, KW), F32),
        compiler_params=_params(VMEM_BIG),
    )(rel_bias)


def _bias_bwd(dbias, nrel):
    nh = dbias.shape[0]
    width = KW + QB

    def body(db_ref, o_ref, diag_ref):
        onehot = _rel_index(nrel)
        valid = _band_valid()
        rr = lax.broadcasted_iota(jnp.int32, (QB, QB), 0)
        cc = lax.broadcasted_iota(jnp.int32, (QB, QB), 1)
        flip = (rr + cc == QB - 1).astype(F32)
        for h in range(nh):
            rev = jnp.dot(flip, jnp.where(valid, db_ref[h], 0.0), preferred_element_type=F32,
                          precision=lax.Precision.HIGHEST)
            w = jnp.concatenate([jnp.zeros((QB, QB), F32), rev], axis=1)
            back = pltpu.roll(w, width - (QB - 1), 1, stride=1, stride_axis=0)
            diag_ref[h:h + 1, :] = _colsum(back)
        o_ref[...] = lax.dot_general(diag_ref[...], onehot, NT, preferred_element_type=F32,
                                     precision=lax.Precision.HIGHEST)

    return pl.pallas_call(
        body, name="bias_bwd",
        out_shape=jax.ShapeDtypeStruct((nh, nrel), F32),
        scratch_shapes=[pltpu.VMEM((nh, width), F32)],
        compiler_params=_params(VMEM_BIG),
    )(dbias)


def _conv_fwd(x, tab, ck, wci, wco, tm):
    s, d = x.shape

    def body(x_ref, t_ref, ck_ref, wci_ref, wco_ref, x1_ref, h_ref, bcx_ref, u_ref, y_ref, carry):
        @pl.when(pl.program_id(0) == 0)
        def _():
            carry[...] = jnp.zeros_like(carry)

        xv = x_ref[...]
        hb = ((xv * _rs(xv)) * t_ref[R_W1:R_W1 + 1, :] + t_ref[R_SH1:R_SH1 + 1, :]).astype(BF16)
        h_ref[...] = hb
        for j in range(3 * d // MXU):
            bcx_ref[:, j * MXU:(j + 1) * MXU] = lax.dot_general(hb, wci_ref[j * MXU:(j + 1) * MXU, :], NT,
                                                                preferred_element_type=F32)
        bg, cg, xi = bcx_ref[:, 0:d], bcx_ref[:, d:2 * d], bcx_ref[:, 2 * d:3 * d]
        z = cg * xi
        row = lax.broadcasted_iota(jnp.int32, z.shape, 0)
        c1, c2 = carry[7:8, :], carry[6:7, :]
        z1 = jnp.where(row == 0, c1, pltpu.roll(z, 1, 0))
        z2 = jnp.where(row == 0, c2, jnp.where(row == 1, c1, pltpu.roll(z, 2, 0)))
        carry[...] = z[tm - 8:tm, :]
        conv = ck_ref[0:1, :] * z2 + ck_ref[1:2, :] * z1 + ck_ref[2:3, :] * z
        ub = (bg * conv).astype(BF16)
        u_ref[...] = ub
        yv = jnp.dot(ub, wco_ref[...], preferred_element_type=F32)
        y_ref[...] = yv
        x1_ref[...] = xv + (yv * _rs(yv)) * t_ref[R_P1:R_P1 + 1, :]

    return pl.pallas_call(
        body, name="conv_fwd", grid=(s // tm,),
        in_specs=[_rows(tm, d), _const(tab.shape), _const(ck.shape), _const(wci.shape), _const(wco.shape)],
        out_specs=(_rows(tm, d), _rows(tm, d), _rows(tm, 3 * d), _rows(tm, d), _rows(tm, d)),
        out_shape=(jax.ShapeDtypeStruct((s, d), F32), jax.ShapeDtypeStruct((s, d), BF16),
                   jax.ShapeDtypeStruct((s, 3 * d), F32), jax.ShapeDtypeStruct((s, d), BF16),
                   jax.ShapeDtypeStruct((s, d), F32)),
        scratch_shapes=[pltpu.VMEM((8, d), F32)],
        compiler_params=_params(VMEM_BIG),
    )(x, tab, ck, wci, wco)


def _ffn_fwd(x, tab, base, wfi, wfo, tgt, tm, name):
    s, d = x.shape
    hid = wfo.shape[0]
    nblk = hid // MXU
    with_loss = tgt is not None

    def body(*refs):
        if with_loss:
            x_ref, t_ref, wfi_ref, wfo_ref, tgt_ref, xo_ref, h_ref, gu_ref, y_ref, loss_ref, a_scr = refs
        else:
            x_ref, t_ref, wfi_ref, wfo_ref, xo_ref, h_ref, gu_ref, y_ref, a_scr = refs
        xv = x_ref[...]
        hb = ((xv * _rs(xv)) * t_ref[base + R_W2:base + R_W2 + 1, :]
              + t_ref[base + R_SH2:base + R_SH2 + 1, :]).astype(BF16)
        h_ref[...] = hb
        acc = jnp.zeros((tm, d), F32)
        for c0 in range(0, nblk, FFN_CHUNK):
            for j in range(c0, min(c0 + FFN_CHUNK, nblk)):
                lo, hi = j * MXU, (j + 1) * MXU
                g = lax.dot_general(hb, wfi_ref[lo:hi, :], NT, preferred_element_type=F32)
                u = lax.dot_general(hb, wfi_ref[hid + lo:hid + hi, :], NT, preferred_element_type=F32)
                gu_ref[:, lo:hi] = g.astype(BF16)
                gu_ref[:, hid + lo:hid + hi] = u.astype(BF16)
                a_scr[:, lo:hi] = ((g * _sigmoid(g)) * u).astype(BF16)
            lo, hi = c0 * MXU, min(c0 + FFN_CHUNK, nblk) * MXU
            acc = acc + jnp.dot(a_scr[:, lo:hi], wfo_ref[lo:hi, :], preferred_element_type=F32)
        y_ref[...] = acc
        xo = xv + (acc * _rs(acc)) * t_ref[base + R_P2:base + R_P2 + 1, :]
        if with_loss:
            @pl.when(pl.program_id(0) == 0)
            def _():
                loss_ref[...] = jnp.zeros_like(loss_ref)

            err = xo - tgt_ref[...]
            xo_ref[...] = err * (1.0 / d)
            e2 = jnp.sum((err * err).reshape(tm // 8, 8, d), axis=0)
            for q in range(d // LANES):
                loss_ref[...] += e2[:, q * LANES:(q + 1) * LANES]
        else:
            xo_ref[...] = xo

    in_specs = [_rows(tm, d), _const(tab.shape), _const(wfi.shape), _const(wfo.shape)]
    args = [x, tab, wfi, wfo]
    out_specs = [_rows(tm, d), _rows(tm, d), _rows(tm, 2 * hid), _rows(tm, d)]
    out_shape = [jax.ShapeDtypeStruct((s, d), F32), jax.ShapeDtypeStruct((s, d), BF16),
                 jax.ShapeDtypeStruct((s, 2 * hid), BF16), jax.ShapeDtypeStruct((s, d), F32)]
    if with_loss:
        in_specs.append(_rows(tm, d))
        args.append(tgt)
        out_specs.append(pl.BlockSpec((8, LANES), lambda i: (0, 0)))
        out_shape.append(jax.ShapeDtypeStruct((8, LANES), F32))
    return pl.pallas_call(
        body, name=name, grid=(s // tm,), in_specs=in_specs, out_specs=tuple(out_specs),
        out_shape=tuple(out_shape), scratch_shapes=[pltpu.VMEM((tm, hid), BF16)],
        compiler_params=_params(VMEM_BIG),
    )(*args)


def _qkv_fwd(x, tab, wq, wkv, tm):
    s, d = x.shape
    base = 6

    def body(x_ref, t_ref, wq_ref, wkv_ref, hkv_ref, h1_ref, q_ref, k_ref, v_ref):
        xv = x_ref[...]
        n = xv * _rs(xv)
        hkv = (n * t_ref[R_KV:R_KV + 1, :] + t_ref[R_KV + 1:R_KV + 2, :]).astype(BF16)
        h1 = (n * t_ref[base + R_W1:base + R_W1 + 1, :] + t_ref[base + R_SH1:base + R_SH1 + 1, :]).astype(BF16)
        hkv_ref[...] = hkv
        h1_ref[...] = h1
        q_ref[...] = (jnp.dot(h1, wq_ref[...], preferred_element_type=F32) * (HEAD_DIM ** -0.5)).astype(BF16)
        for j in range(d // MXU):
            lo, hi = j * MXU, (j + 1) * MXU
            k_ref[:, lo:hi] = lax.dot_general(hkv, wkv_ref[lo:hi, :], NT, preferred_element_type=F32).astype(BF16)
            v_ref[:, lo:hi] = lax.dot_general(hkv, wkv_ref[d + lo:d + hi, :], NT,
                                              preferred_element_type=F32).astype(BF16)

    act = jax.ShapeDtypeStruct((s, d), BF16)
    return pl.pallas_call(
        body, name="qkv_fwd", grid=(s // tm,),
        in_specs=[_rows(tm, d), _const(tab.shape), _const(wq.shape), _const(wkv.shape)],
        out_specs=tuple([_rows(tm, d)] * 5), out_shape=(act,) * 5,
        compiler_params=_params(VMEM_BIG),
    )(x, tab, wq, wkv)


def _window_specs(per=1):
    return [pl.BlockSpec((QB, LANES), (lambda p, b, w=w: (jnp.maximum(per * b - 2 + w, 0), p)))
            for w in range(2 + per)]


def _key_valid(b):
    col = lax.broadcasted_iota(jnp.int32, (QB, KW), 1) // CHUNK
    return (b * (QB // CHUNK) - N_LEFT + col) >= 0


def _bias_spec(per=1, sub=0):
    return pl.BlockSpec((None, LANES // HEAD_DIM, QB, KW),
                        lambda p, b: (jnp.minimum(per * b + sub, BIAS_VARIANTS - 1), p, 0, 0))


def _head_masks():
    lane = lax.broadcasted_iota(jnp.int32, (1, LANES), 1)
    return [(lane // HEAD_DIM == hh) for hh in range(LANES // HEAD_DIM)]


def _attn_fwd(q, k, v, bias):
    s, d = q.shape
    per = ATTN_PER
    npair, nb = d // LANES, s // (per * QB)
    hpp = LANES // HEAD_DIM
    nwin = 2 + per
    nbias = min(per, BIAS_VARIANTS)

    def body(*refs):
        q_ref, k_refs, v_refs = refs[0], refs[1:1 + nwin], refs[1 + nwin:1 + 2 * nwin]
        bias_refs = refs[1 + 2 * nwin:1 + 2 * nwin + nbias]
        o_ref, lse_ref = refs[1 + 2 * nwin + nbias:]
        ks = [r[...] for r in k_refs]
        vs = [r[...] for r in v_refs]
        masks = _head_masks()
        for sub in range(per):
            rows = slice(sub * QB, (sub + 1) * QB)
            qv = q_ref[rows, :]
            kwin = jnp.concatenate(ks[sub:sub + 3], axis=0)
            vwin = jnp.concatenate(vs[sub:sub + 3], axis=0)
            o = jnp.zeros((QB, LANES), F32)
            lse = jnp.zeros((QB, LANES), F32)
            scs = [lax.dot_general(jnp.where(masks[hh], qv, jnp.zeros_like(qv)), kwin, NT,
                                   preferred_element_type=F32) + bias_refs[min(sub, nbias - 1)][hh]
                   for hh in range(hpp)]
            for hh in range(hpp):
                vm = jnp.where(masks[hh], vwin, jnp.zeros_like(vwin))
                sc = scs[hh]
                m = jnp.max(sc, axis=-1, keepdims=True)
                p = jnp.exp(sc - m)
                l = jnp.sum(p, axis=-1, keepdims=True)
                o = o + jnp.dot(p.astype(BF16), vm, preferred_element_type=F32) * (1.0 / l)
                lse = jnp.where(masks[hh], m + jnp.log(l), lse)
            o_ref[rows, :] = o.astype(BF16)
            lse_ref[rows, :] = lse

    blk = pl.BlockSpec((per * QB, LANES), lambda p, b: (b, p))
    return pl.pallas_call(
        body, name="attn_fwd", grid=(npair, nb),
        in_specs=[blk] + _window_specs(per) + _window_specs(per) + [_bias_spec(per, sub) for sub in range(nbias)],
        out_specs=(blk, blk),
        out_shape=(jax.ShapeDtypeStruct((s, d), BF16), jax.ShapeDtypeStruct((s, d), F32)),
        compiler_params=_params(VMEM_BIG),
    )(q, *([k] * nwin), *([v] * nwin), *([bias] * nbias))


def _attn_out_fwd(o, x, tab, wo, tm):
    s, d = x.shape
    base = 6

    def body(o_ref, x_ref, t_ref, wo_ref, x3_ref, y_ref):
        yv = jnp.dot(o_ref[...], wo_ref[...], preferred_element_type=F32)
        y_ref[...] = yv
        x3_ref[...] = x_ref[...] + (yv * _rs(yv)) * t_ref[base + R_P1:base + R_P1 + 1, :]

    return pl.pallas_call(
        body, name="attn_out_fwd", grid=(s // tm,),
        in_specs=[_rows(tm, d), _rows(tm, d), _const(tab.shape), _const(wo.shape)],
        out_specs=(_rows(tm, d), _rows(tm, d)),
        out_shape=(jax.ShapeDtypeStruct((s, d), F32), jax.ShapeDtypeStruct((s, d), F32)),
        compiler_params=_params(VMEM_BIG),
    )(o, x, tab, wo)


def _ffn_bwd(dxo, x, y, gu, tab, base, wfi, wfo, tm, name):
    s, d = x.shape
    hid = wfo.shape[0]
    nblk = hid // MXU

    def body(dxo_ref, x_ref, y_ref, gu_ref, t_ref, wfi_ref, wfo_ref, dx_ref, dyb_ref, dgu_ref, a_ref, sums_ref):
        @pl.when(pl.program_id(0) == 0)
        def _():
            sums_ref[...] = jnp.zeros_like(sums_ref)

        dxo_v = dxo_ref[...]
        yv = y_ref[...]
        ry = _rs(yv)
        ny = yv * ry
        sums_ref[R_P2:R_P2 + 1, :] += _colsum(dxo_v * ny)
        dyb = _norm_bwd(dxo_v * t_ref[base + R_P2:base + R_P2 + 1, :], ny, ry).astype(BF16)
        dyb_ref[...] = dyb
        dh = jnp.zeros((tm, d), F32)
        for c0 in range(0, nblk, FFN_CHUNK):
            for j in range(c0, min(c0 + FFN_CHUNK, nblk)):
                lo, hi = j * MXU, (j + 1) * MXU
                da = lax.dot_general(dyb, wfo_ref[lo:hi, :], NT, preferred_element_type=F32)
                g, u = gu_ref[:, lo:hi].astype(F32), gu_ref[:, hid + lo:hid + hi].astype(F32)
                sg = _sigmoid(g)
                gs = g * sg
                a_ref[:, lo:hi] = (gs * u).astype(BF16)
                dgu_ref[:, lo:hi] = (da * u * sg * (1.0 + g * (1.0 - sg))).astype(BF16)
                dgu_ref[:, hid + lo:hid + hi] = (da * gs).astype(BF16)
            lo, hi = c0 * MXU, min(c0 + FFN_CHUNK, nblk) * MXU
            dh = dh + jnp.dot(dgu_ref[:, lo:hi], wfi_ref[lo:hi, :], preferred_element_type=F32)
            dh = dh + jnp.dot(dgu_ref[:, hid + lo:hid + hi], wfi_ref[hid + lo:hid + hi, :],
                              preferred_element_type=F32)
        xv = x_ref[...]
        r = _rs(xv)
        n = xv * r
        sums_ref[R_SH2:R_SH2 + 1, :] += _colsum(dh)
        sums_ref[R_W2:R_W2 + 1, :] += _colsum(dh * n)
        dx_ref[...] = dxo_v + _norm_bwd(dh * t_ref[base + R_W2:base + R_W2 + 1, :], n, r)

    return pl.pallas_call(
        body, name=name, grid=(s // tm,),
        in_specs=[_rows(tm, d), _rows(tm, d), _rows(tm, d), _rows(tm, 2 * hid),
                  _const(tab.shape), _const(wfi.shape), _const(wfo.shape)],
        out_specs=(_rows(tm, d), _rows(tm, d), _rows(tm, 2 * hid), _rows(tm, hid),
                   pl.BlockSpec((8, d), lambda i: (0, 0))),
        out_shape=(jax.ShapeDtypeStruct((s, d), F32), jax.ShapeDtypeStruct((s, d), BF16),
                   jax.ShapeDtypeStruct((s, 2 * hid), BF16), jax.ShapeDtypeStruct((s, hid), BF16),
                   jax.ShapeDtypeStruct((8, d), F32)),
        compiler_params=_params(VMEM_BIG),
    )(dxo, x, y, gu, tab, wfi, wfo)


def _attn_out_bwd(dx, y, tab, wo, tm):
    s, d = y.shape
    base = 6

    def body(dx_ref, y_ref, t_ref, wo_ref, dyb_ref, do_ref, sums_ref):
        @pl.when(pl.program_id(0) == 0)
        def _():
            sums_ref[...] = jnp.zeros_like(sums_ref)

        dxv = dx_ref[...]
        yv = y_ref[...]
        ry = _rs(yv)
        ny = yv * ry
        sums_ref[R_P1:R_P1 + 1, :] += _colsum(dxv * ny)
        dyb = _norm_bwd(dxv * t_ref[base + R_P1:base + R_P1 + 1, :], ny, ry).astype(BF16)
        dyb_ref[...] = dyb
        do_ref[...] = lax.dot_general(dyb, wo_ref[...], NT, preferred_element_type=F32).astype(BF16)

    return pl.pallas_call(
        body, name="attn_out_bwd", grid=(s // tm,),
        in_specs=[_rows(tm, d), _rows(tm, d), _const(tab.shape), _const(wo.shape)],
        out_specs=(_rows(tm, d), _rows(tm, d), pl.BlockSpec((8, d), lambda i: (0, 0))),
        out_shape=(jax.ShapeDtypeStruct((s, d), BF16), jax.ShapeDtypeStruct((s, d), BF16),
                   jax.ShapeDtypeStruct((8, d), F32)),
        compiler_params=_params(VMEM_BIG),
    )(dx, y, tab, wo)


def _attn_bwd(q, k, v, o, do, lse, bias):
    s, d = q.shape
    per = ATTN_PER
    npair, nb = d // LANES, s // (per * QB)
    hpp = LANES // HEAD_DIM
    nwin = 2 + per
    nbias = min(per, BIAS_VARIANTS)

    def body(*refs):
        q_ref, k_refs, v_refs = refs[0], refs[1:1 + nwin], refs[1 + nwin:1 + 2 * nwin]
        o_ref, do_ref, lse_ref = refs[1 + 2 * nwin:4 + 2 * nwin]
        bias_refs = refs[4 + 2 * nwin:4 + 2 * nwin + nbias]
        dq_ref, dk_ref, dv_ref, db_ref = refs[4 + 2 * nwin + nbias:]
        b = pl.program_id(1)

        @pl.when(b == 0)
        def _():
            dk_ref[...] = jnp.zeros_like(dk_ref)
            dv_ref[...] = jnp.zeros_like(dv_ref)
            db_ref[...] = jnp.zeros_like(db_ref)

        ks = [r[...] for r in k_refs]
        vs = [r[...] for r in v_refs]
        masks = _head_masks()
        for sub in range(per):
            rows = slice(sub * QB, (sub + 1) * QB)
            qv = q_ref[rows, :]
            dov = do_ref[rows, :]
            lsev = lse_ref[rows, :]
            doo = dov.astype(F32) * o_ref[rows, :].astype(F32)
            kwin = jnp.concatenate(ks[sub:sub + 3], axis=0)
            vwin = jnp.concatenate(vs[sub:sub + 3], axis=0)
            dq = jnp.zeros((QB, LANES), F32)
            dkt = jnp.zeros((LANES, KW), F32)
            dvt = jnp.zeros((LANES, KW), F32)
            for hh in range(hpp):
                qm = jnp.where(masks[hh], qv, jnp.zeros_like(qv))
                dom = jnp.where(masks[hh], dov, jnp.zeros_like(dov))
                km = jnp.where(masks[hh], kwin, jnp.zeros_like(kwin))
                lse_h = jnp.max(jnp.where(masks[hh], lsev, NEG), axis=-1, keepdims=True)
                delta = jnp.sum(jnp.where(masks[hh], doo, 0.0), axis=-1, keepdims=True)
                sc = lax.dot_general(qm, kwin, NT, preferred_element_type=F32) + bias_refs[min(sub, nbias - 1)][hh]
                p = jnp.exp(sc - lse_h)
                dp = lax.dot_general(dom, vwin, NT, preferred_element_type=F32)
                ds = p * (dp - delta)
                db_ref[hh] += ds
                dsb = ds.astype(BF16)
                dq = dq + jnp.dot(dsb, km, preferred_element_type=F32)
                dkt = dkt + jnp.dot(qm.T, dsb, preferred_element_type=F32)
                dvt = dvt + jnp.dot(dom.T, p.astype(BF16), preferred_element_type=F32)
            dkw, dvw = dkt.T, dvt.T
            dq_ref[rows, :] = (dq * (HEAD_DIM ** -0.5)).astype(BF16)
            for w in range(3):
                start = pl.multiple_of(jnp.maximum(per * b + sub - 2 + w, 0) * QB, QB)
                dk_ref[pl.ds(start, QB), :] += dkw[w * QB:(w + 1) * QB, :]
                dv_ref[pl.ds(start, QB), :] += dvw[w * QB:(w + 1) * QB, :]

    blk = pl.BlockSpec((per * QB, LANES), lambda p, b: (b, p))
    col = pl.BlockSpec((s, LANES), lambda p, b: (0, p))
    pair = pl.BlockSpec((hpp, QB, KW), lambda p, b: (p, 0, 0))
    return pl.pallas_call(
        body, name="attn_bwd", grid=(npair, nb),
        in_specs=[blk] + _window_specs(per) + _window_specs(per) + [blk, blk, blk]
                 + [_bias_spec(per, sub) for sub in range(nbias)],
        out_specs=(blk, col, col, pair),
        out_shape=(jax.ShapeDtypeStruct((s, d), BF16), jax.ShapeDtypeStruct((s, d), F32),
                   jax.ShapeDtypeStruct((s, d), F32), jax.ShapeDtypeStruct(bias.shape[1:], F32)),
        compiler_params=_params(VMEM_BIG),
    )(q, *([k] * nwin), *([v] * nwin), o, do, lse, *([bias] * nbias))


def _qkv_bwd(dres, dq, dk, dv, x, tab, wq, wkv, tm):
    s, d = x.shape
    base = 6

    def body(dres_ref, dq_ref, dk_ref, dv_ref, x_ref, t_ref, wq_ref, wkv_ref, dx_ref, dkv_ref, sums_ref):
        @pl.when(pl.program_id(0) == 0)
        def _():
            sums_ref[...] = jnp.zeros_like(sums_ref)

        dh1 = lax.dot_general(dq_ref[...], wq_ref[...], NT, preferred_element_type=F32)
        dkv_ref[:, 0:d] = dk_ref[...].astype(BF16)
        dkv_ref[:, d:2 * d] = dv_ref[...].astype(BF16)
        dhkv = jnp.dot(dkv_ref[...], wkv_ref[...], preferred_element_type=F32)
        xv = x_ref[...]
        r = _rs(xv)
        n = xv * r
        sums_ref[0:1, :] += _colsum(dh1 * n)
        sums_ref[1:2, :] += _colsum(dh1)
        sums_ref[2:3, :] += _colsum(dhkv * n)
        sums_ref[3:4, :] += _colsum(dhkv)
        dn = dh1 * t_ref[base + R_W1:base + R_W1 + 1, :] + dhkv * t_ref[R_KV:R_KV + 1, :]
        dx_ref[...] = dres_ref[...] + _norm_bwd(dn, n, r)

    return pl.pallas_call(
        body, name="qkv_bwd", grid=(s // tm,),
        in_specs=[_rows(tm, d)] * 5 + [_const(tab.shape), _const(wq.shape), _const(wkv.shape)],
        out_specs=(_rows(tm, d), _rows(tm, 2 * d), pl.BlockSpec((8, d), lambda i: (0, 0))),
        out_shape=(jax.ShapeDtypeStruct((s, d), F32), jax.ShapeDtypeStruct((s, 2 * d), BF16),
                   jax.ShapeDtypeStruct((8, d), F32)),
        compiler_params=_params(VMEM_BIG),
    )(dres, dq, dk, dv, x, tab, wq, wkv)


def _conv_bwd(dx1, x, y, bcx, tab, ck, wci, wco, tm):
    s, d = x.shape
    nt = s // tm

    def rev(i):
        return (nt - 1 - i, 0)

    def halo(i):
        return (jnp.maximum((nt - 1 - i) * (tm // 8) - 1, 0), 0)

    def body(dx_ref, x_ref, y_ref, bcx_ref, halo_ref, t_ref, ck_ref, wci_ref, wco_ref,
             dx0_ref, dyb_ref, dbcx_ref, sums_ref, dck_ref, carry):
        i = pl.program_id(0)

        @pl.when(i == 0)
        def _():
            sums_ref[...] = jnp.zeros_like(sums_ref)
            dck_ref[...] = jnp.zeros_like(dck_ref)
            carry[...] = jnp.zeros_like(carry)

        dxv = dx_ref[...]
        yv = y_ref[...]
        ry = _rs(yv)
        ny = yv * ry
        sums_ref[R_P1:R_P1 + 1, :] += _colsum(dxv * ny)
        dyb = _norm_bwd(dxv * t_ref[R_P1:R_P1 + 1, :], ny, ry).astype(BF16)
        dyb_ref[...] = dyb
        du = lax.dot_general(dyb, wco_ref[...], NT, preferred_element_type=F32)
        bg, cg, xi = bcx_ref[:, 0:d], bcx_ref[:, d:2 * d], bcx_ref[:, 2 * d:3 * d]
        z = cg * xi
        zp = halo_ref[:, d:2 * d] * halo_ref[:, 2 * d:3 * d]
        zp = jnp.where(i == nt - 1, jnp.zeros_like(zp), zp)
        row = lax.broadcasted_iota(jnp.int32, z.shape, 0)
        c1, c2 = zp[7:8, :], zp[6:7, :]
        z1 = jnp.where(row == 0, c1, pltpu.roll(z, 1, 0))
        z2 = jnp.where(row == 0, c2, jnp.where(row == 1, c1, pltpu.roll(z, 2, 0)))
        k0, k1, k2 = ck_ref[0:1, :], ck_ref[1:2, :], ck_ref[2:3, :]
        conv = k0 * z2 + k1 * z1 + k2 * z
        dconv = du * bg
        dck_ref[0:1, :] += _colsum(dconv * z2)
        dck_ref[1:2, :] += _colsum(dconv * z1)
        dck_ref[2:3, :] += _colsum(dconv * z)
        n1, n2 = carry[0:1, :], carry[1:2, :]
        d1 = jnp.where(row == tm - 1, n1, pltpu.roll(dconv, tm - 1, 0))
        d2 = jnp.where(row == tm - 1, n2, jnp.where(row == tm - 2, n1, pltpu.roll(dconv, tm - 2, 0)))
        carry[...] = dconv[0:8, :]
        dz = k2 * dconv + k1 * d1 + k0 * d2
        dbcx_ref[:, 0:d] = (du * conv).astype(BF16)
        dbcx_ref[:, d:2 * d] = (dz * xi).astype(BF16)
        dbcx_ref[:, 2 * d:3 * d] = (dz * cg).astype(BF16)
        dh = jnp.dot(dbcx_ref[...], wci_ref[...], preferred_element_type=F32)
        xv = x_ref[...]
        r = _rs(xv)
        n = xv * r
        sums_ref[R_W1:R_W1 + 1, :] += _colsum(dh * n)
        sums_ref[R_SH1:R_SH1 + 1, :] += _colsum(dh)
        dx0_ref[...] = dxv + _norm_bwd(dh * t_ref[R_W1:R_W1 + 1, :], n, r)

    rrow = lambda cols: pl.BlockSpec((tm, cols), rev)
    acc = pl.BlockSpec((8, d), lambda i: (0, 0))
    return pl.pallas_call(
        body, name="conv_bwd", grid=(nt,),
        in_specs=[rrow(d), rrow(d), rrow(d), rrow(3 * d), pl.BlockSpec((8, 3 * d), halo),
                  _const(tab.shape), _const(ck.shape), _const(wci.shape), _const(wco.shape)],
        out_specs=(rrow(d), rrow(d), rrow(3 * d), acc, acc),
        out_shape=(jax.ShapeDtypeStruct((s, d), F32), jax.ShapeDtypeStruct((s, d), BF16),
                   jax.ShapeDtypeStruct((s, 3 * d), BF16), jax.ShapeDtypeStruct((8, d), F32),
                   jax.ShapeDtypeStruct((8, d), F32)),
        scratch_shapes=[pltpu.VMEM((8, d), F32)],
        compiler_params=_params(VMEM_BIG),
    )(dx1, x, y, bcx, bcx, tab, ck, wci, wco)


def _wgrad_wide(a, b, nblk, tk, name):
    s, m = a.shape
    n = b.shape[1] // nblk
    nk = s // tk

    def body(a_ref, b_ref, o_ref, acc):
        kk = pl.program_id(0)

        @pl.when(kk == 0)
        def _():
            acc[...] = jnp.zeros_like(acc)

        acc[...] += jnp.dot(a_ref[...].T, b_ref[...], preferred_element_type=F32)

        @pl.when(kk == nk - 1)
        def _():
            for j in range(nblk):
                o_ref[j] = acc[:, j * n:(j + 1) * n].astype(BF16)

    return pl.pallas_call(
        body, name=name, grid=(nk,),
        in_specs=[pl.BlockSpec((tk, m), lambda kk: (kk, 0)), pl.BlockSpec((tk, nblk * n), lambda kk: (kk, 0))],
        out_specs=pl.BlockSpec((nblk, m, n), lambda kk: (0, 0, 0)),
        out_shape=jax.ShapeDtypeStruct((nblk, m, n), BF16),
        scratch_shapes=[pltpu.VMEM((m, nblk * n), F32)],
        compiler_params=_params(VMEM_BIG),
    )(a, b)


def _wgrad_rows(a, b, ncb, tk, name):
    s, m = a.shape
    n = b.shape[1]
    mb = m // ncb
    nk = s // tk

    def body(a_ref, b_ref, o_ref, acc):
        kk = pl.program_id(1)

        @pl.when(kk == 0)
        def _():
            acc[...] = jnp.zeros_like(acc)

        acc[...] += jnp.dot(a_ref[...].T, b_ref[...], preferred_element_type=F32)

        @pl.when(kk == nk - 1)
        def _():
            o_ref[...] = acc[...].astype(BF16)

    return pl.pallas_call(
        body, name=name, grid=(ncb, nk),
        in_specs=[pl.BlockSpec((tk, mb), lambda j, kk: (kk, j)), pl.BlockSpec((tk, n), lambda j, kk: (kk, 0))],
        out_specs=pl.BlockSpec((mb, n), lambda j, kk: (j, 0)),
        out_shape=jax.ShapeDtypeStruct((m, n), BF16),
        scratch_shapes=[pltpu.VMEM((mb, n), F32)],
        compiler_params=_params(VMEM_BIG),
    )(a, b)


def _adamw_math(w, g, m, v):
    m = ADAM_B1 * m + (1.0 - ADAM_B1) * g
    v = ADAM_B2 * v + (1.0 - ADAM_B2) * (g * g)
    m_hat = m / (1.0 - ADAM_B1 ** ADAM_STEP)
    v_hat = v / (1.0 - ADAM_B2 ** ADAM_STEP)
    delta = -ADAM_LR * (m_hat / (jnp.sqrt(v_hat) + ADAM_EPS) + ADAM_WD * w)
    return delta, m, v


def _adamw_reduce(parts, w, m, v, tr, name, parts_t=False):
    nl, r, c = w.shape
    tr = r if (parts_t and r % LANES) else (LANES if parts_t else _row_tile(r, tr))

    def body(*refs):
        p_refs = refs[:nl]
        w_ref, m_ref, v_ref, g_ref, d_ref, mo_ref, vo_ref = refs[nl:]
        layer = pl.program_id(0)

        def partial(i):
            val = p_refs[0][i].astype(F32)
            for q in range(1, nl):
                val = jnp.where(layer == q, p_refs[q][i].astype(F32), val)
            return val

        g = partial(0)
        for i in range(1, N_DEV):
            g = g + partial(i)
        if parts_t:
            g = g.T
        g_ref[...] = g
        d_ref[...], mo_ref[...], vo_ref[...] = _adamw_math(w_ref[...], g, m_ref[...], v_ref[...])

    blk = pl.BlockSpec((None, tr, c), lambda l, i: (l, i, 0))
    out = jax.ShapeDtypeStruct((nl, r, c), F32)
    if parts_t:
        p_specs = [pl.BlockSpec((N_DEV, c, tr), lambda l, i: (0, 0, i))]
    else:
        p_specs = [pl.BlockSpec((N_DEV, tr, c), (lambda l, i, q=q: (0, jnp.where(l == q, i, 0), 0)))
                   for q in range(nl)]
    return pl.pallas_call(
        body, name=name, grid=(nl, r // tr),
        in_specs=p_specs + [blk, blk, blk],
        out_specs=(blk,) * 4, out_shape=(out,) * 4,
        compiler_params=_params(VMEM_BIG),
    )(*parts, w, m, v)


def _adamw_outer(sct, dm, w, m, v, tr, name):
    nl, d, c = w.shape

    def body(s_ref, dm_ref, w_ref, m_ref, v_ref, g_ref, d_ref, mo_ref, vo_ref):
        g = jnp.dot(s_ref[...], dm_ref[...], preferred_element_type=F32)
        g_ref[...] = g
        d_ref[...], mo_ref[...], vo_ref[...] = _adamw_math(w_ref[...], g, m_ref[...], v_ref[...])

    blk = pl.BlockSpec((None, tr, c), lambda l, i: (l, i, 0))
    out = jax.ShapeDtypeStruct((nl, d, c), F32)
    return pl.pallas_call(
        body, name=name, grid=(nl, d // tr),
        in_specs=[pl.BlockSpec((tr, N_DEV), lambda l, i: (i, 0)),
                  pl.BlockSpec((None, N_DEV, c), lambda l, i: (l, 0, 0)), blk, blk, blk],
        out_specs=(blk,) * 4, out_shape=(out,) * 4,
        compiler_params=_params(VMEM_BIG),
    )(sct, dm, w, m, v)


def _pad_rows(a, rows):
    return jnp.concatenate([a, jnp.zeros((rows - a.shape[0],) + a.shape[1:], a.dtype)], axis=0)


def kernel(x, c, mod_w, mod_b, norm_g, ffn_w_in, ffn_w_out, conv_w_in, conv_k, conv_w_out, kv_mod_w, kv_mod_b, kv_norm_g, w_kv, attn_w_q, attn_w_o, rel_bias, loss_target, m_mod_w, m_mod_b, m_norm_g, m_ffn_w_in, m_ffn_w_out, m_conv_w_in, m_conv_k, m_conv_w_out, m_kv_mod_w, m_kv_mod_b, m_kv_norm_g, m_w_kv, m_attn_w_q, m_attn_w_o, m_rel_bias, v_mod_w, v_mod_b, v_norm_g, v_ffn_w_in, v_ffn_w_out, v_conv_w_in, v_conv_k, v_conv_w_out, v_kv_mod_w, v_kv_mod_b, v_kv_norm_g, v_w_kv, v_attn_w_q, v_attn_w_o, v_rel_bias):
    s, d = x.shape[1], x.shape[2]
    dq = d // LANES
    dsh = d // N_DEV
    nl = mod_w.shape[0]
    mw = mod_w.shape[2]
    kmw = kv_mod_w.shape[1]
    nh, nrel = rel_bias.shape[1], rel_bias.shape[2]
    tm = min(256, s)
    tm2 = min(512, s)
    tk = min(2048, s)
    tu = 512
    me = 4 * lax.axis_index("x") + 2 * lax.axis_index("y") + lax.axis_index("c")

    x0 = x[0]
    tgt = loss_target[0]

    small1 = jnp.concatenate([c.reshape(dq, LANES), norm_g.reshape(dq, LANES),
                              _pad_rows(conv_k[0], 8).reshape(dq, LANES)], axis=0)
    (sm,) = _exchange([small1], ["gather"], "gather_small")
    c_all = sm[:, 0:dq].reshape(N_DEV, d)
    ng_full = jnp.transpose(sm[:, dq:2 * dq].reshape(N_DEV, 8, dsh), (1, 0, 2)).reshape(8, d)
    ck_full = jnp.transpose(sm[:, 2 * dq:3 * dq].reshape(N_DEV, 8, dsh), (1, 0, 2)).reshape(8, d)

    modcols, silu_c = _mod_fwd(c_all, mod_w, kv_mod_w)
    (modall,) = _exchange([modcols], ["gather"], "gather_mod")

    cast = lambda *ws: [a.astype(BF16) for a in ws]
    gath = lambda ws: (ws, ["gather"] * len(ws))
    half = lambda ws: (ws, ["gather_half"] * len(ws))
    (h_conv, h_ffn0, h_attn, h_ffn1), token = _xstart(
        [half(cast(conv_w_in[0].T, conv_w_out[0])), half(cast(jnp.swapaxes(ffn_w_in[0], 0, 1), ffn_w_out[0])),
         gath(cast(w_kv.T, attn_w_q[0], attn_w_o[0])), gath(cast(jnp.swapaxes(ffn_w_in[1], 0, 1), ffn_w_out[1]))],
        modall, "gather_start")
    modall = modall + token[0, 0]
    mine = lax.dynamic_index_in_dim(modall, me, axis=1, keepdims=False)
    modrow = jnp.stack([mine[:, l * mw:(l + 1) * mw].reshape(6, d) for l in range(nl)])
    kvrow = mine[:, nl * mw:nl * mw + kmw].reshape(2, d)
    tab, modval = _vec_prep(modrow, mod_b.reshape(nl, 6, d), kvrow, kv_mod_b.reshape(2, d), ng_full,
                            kv_norm_g.reshape(1, d))
    bias = _bias_fwd(rel_bias[0])

    wci, wco = _xwait(h_conv, [bias], "gather_wait_conv")
    wci, wco = _forward_to_sibling([wci, wco], "gather_forward_conv")
    wci, wco = wci.reshape(3 * d, d), wco.reshape(d, d)
    x1, h1a, bcx, ua, ya = _conv_fwd(x0, tab, ck_full, wci, wco, tm2)
    wfi0, wfo0 = _xwait(h_ffn0, [x1], "gather_wait_ffn0")
    wfi0, wfo0 = _forward_to_sibling([wfi0, wfo0], "gather_forward_ffn0")
    wfi0, wfo0 = wfi0.reshape(-1, d), wfo0.reshape(-1, d)
    x2, h2a, gua, y2a = _ffn_fwd(x1, tab, 0, wfi0, wfo0, None, tm2, "ffn_fwd0")
    wkv, wq, wo = _xwait(h_attn, [x2], "gather_wait_attn")
    wkv, wq, wo = wkv.reshape(2 * d, d), wq.reshape(d, d), wo.reshape(d, d)
    hkv, h1b, q, k, v = _qkv_fwd(x2, tab, wq, wkv, tm2)
    o, lse = _attn_fwd(q, k, v, bias)
    x3, yb = _attn_out_fwd(o, x2, tab, wo, tm2)
    wfi1, wfo1 = _xwait(h_ffn1, [x3], "gather_wait_ffn1")
    wfi1, wfo1 = wfi1.reshape(-1, d), wfo1.reshape(-1, d)
    dx4, h2b, gub, y2b, loss_acc = _ffn_fwd(x3, tab, 6, wfi1, wfo1, tgt, tm2, "ffn_fwd1")

    scat = lambda ws: [(ws, ["scatter"] * len(ws))]
    dx3, dy2b, dgub, ab, sums_f1 = _ffn_bwd(dx4, x3, y2b, gub, tab, 6, wfi1, wfo1, tm, "ffn_bwd1")
    g_wfi1 = _wgrad_rows(dgub, h2b, 4, tk, "wgrad_ffn_in1").reshape(N_DEV, -1, d)
    g_wfo1 = _wgrad_rows(ab, dy2b, 2, tk, "wgrad_ffn_out1").reshape(N_DEV, -1, d)
    (h_g1,), token = _xstart(scat([g_wfi1, g_wfo1]), dx3, "grads_start_ffn1")
    tab = tab + token[0, 0]
    dyb, do, sums_o = _attn_out_bwd(dx3, yb, tab, wo, tm2)
    g_wo = _wgrad_wide(o, dyb, 1, tk, "wgrad_o").reshape(N_DEV, dsh, d)
    dqb, dk, dv, dbias = _attn_bwd(q, k, v, o, do, lse, bias)
    g_wq = _wgrad_wide(h1b, dqb, 1, tk, "wgrad_q").reshape(N_DEV, dsh, d)
    dx2, dkvb, sums_q = _qkv_bwd(dx3, dqb, dk, dv, x2, tab, wq, wkv, tm2)
    g_wkv = _wgrad_rows(dkvb, hkv, 2, tk, "wgrad_kv").reshape(N_DEV, -1, d)
    (h_g2,), token = _xstart(scat([g_wkv, g_wq, g_wo]), dx2, "grads_start_attn")
    tab = tab + token[0, 0]
    dx1, dy2a, dgua, aa, sums_f0 = _ffn_bwd(dx2, x1, y2a, gua, tab, 0, wfi0, wfo0, tm, "ffn_bwd0")
    g_wfi0 = _wgrad_rows(dgua, h2a, 4, tk, "wgrad_ffn_in0").reshape(N_DEV, -1, d)
    g_wfo0 = _wgrad_rows(aa, dy2a, 2, tk, "wgrad_ffn_out0").reshape(N_DEV, -1, d)
    (h_g3,), token = _xstart(scat([g_wfi0, g_wfo0]), dx1, "grads_start_ffn0")
    tab = tab + token[0, 0]
    dx0, dya, dbcx, sums_c, dck = _conv_bwd(dx1, x0, ya, bcx, tab, ck_full, wci, wco, tm2)
    drel = _bias_bwd(dbias, nrel)
    dmod, dng, dkvg = _vec_bwd(sums_c, sums_f0, sums_q, sums_o, sums_f1, modval, ng_full, kv_norm_g.reshape(1, d))

    relw = -(-nrel // LANES) * LANES
    drel_p = jnp.concatenate([drel, jnp.zeros((nh, relw - nrel), F32)], axis=1)
    small3 = jnp.concatenate([dmod.reshape(16 * dq, LANES), dng.reshape(8 * dq, LANES), dkvg.reshape(8 * dq, LANES),
                              dck.reshape(8 * dq, LANES), loss_acc,
                              drel_p.reshape(nh * relw // LANES, LANES)], axis=0)
    (sm,) = _exchange([small3], ["gather"], "gather_small_grads")
    g_wci = _wgrad_rows(dbcx, h1a, 3, tk, "wgrad_conv_in").reshape(N_DEV, -1, d)
    g_wco = _wgrad_wide(ua, dya, 1, tk, "wgrad_conv_out").reshape(N_DEV, dsh, d)
    (h_g4,), token = _xstart(scat([g_wci, g_wco]), sm, "grads_start_conv")
    sm = sm + token[0, 0]
    o1, o2, o3, o4, o5 = 16 * dq, 24 * dq, 32 * dq, 40 * dq, 40 * dq + 8
    loss = jnp.sum(sm[:, o4:o5, :]) * (0.5 / d)
    dmod_all = sm[:, 0:o1].reshape(N_DEV, 16, d)
    mine_cols = lambda a: lax.dynamic_slice_in_dim(a, me * dsh, dsh, axis=2)
    dng_parts = mine_cols(sm[:, o1:o2].reshape(N_DEV, 8, d))
    dkvg_parts = sm[:, o2:o3].reshape(N_DEV, 8, d)[:, 0:1]
    dck_parts = mine_cols(sm[:, o3:o4].reshape(N_DEV, 8, d))[:, 0:3]
    drel_parts = sm[:, o5:].reshape(N_DEV, nh, relw)[:, :, 0:nrel]

    def update(parts, w, m, v, name, layers=1, parts_t=False):
        shp = w.shape
        w3, m3, v3 = (a.reshape(layers, -1, shp[-1]) for a in (w, m, v))
        if not parts_t:
            parts = [p.reshape(N_DEV, -1, shp[-1]) for p in parts]
        outs = _adamw_reduce(parts, w3, m3, v3, tu, name, parts_t)
        return [a.reshape(shp) for a in outs]

    p_wfi1, p_wfo1 = _xwait(h_g1, [sm], "grads_wait_ffn1")
    p_wfi0, p_wfo0 = _xwait(h_g3, [p_wfi1], "grads_wait_ffn0")
    tr = lambda a: jnp.swapaxes(a, 1, 2)
    u_ffn_in = [tr(a) for a in update([p_wfi0, p_wfi1], tr(ffn_w_in), tr(m_ffn_w_in), tr(v_ffn_w_in),
                                      "adamw_ffn_in", 2)]
    u_ffn_out = update([p_wfo0, p_wfo1], ffn_w_out, m_ffn_w_out, v_ffn_w_out, "adamw_ffn_out", 2)
    p_wkv, p_wq, p_wo = _xwait(h_g2, [u_ffn_out[0]], "grads_wait_attn")
    u_w_kv = update([p_wkv], w_kv, m_w_kv, v_w_kv, "adamw_w_kv", parts_t=True)
    u_w_q = update([p_wq], attn_w_q, m_attn_w_q, v_attn_w_q, "adamw_w_q")
    u_w_o = update([p_wo], attn_w_o, m_attn_w_o, v_attn_w_o, "adamw_w_o")

    sct = jnp.transpose(silu_c)
    dm_mod = jnp.stack([lax.dynamic_slice_in_dim(dmod_all[:, 6 * l:6 * l + 6].reshape(N_DEV, 6 * d), me * mw, mw, axis=1)
                        for l in range(nl)]).astype(BF16)
    dm_kv = lax.dynamic_slice_in_dim(dmod_all[:, R_KV:R_KV + 2].reshape(N_DEV, 2 * d), me * kmw, kmw, axis=1)
    u_mod_w = _adamw_outer(sct, dm_mod, mod_w, m_mod_w, v_mod_w, min(tu, d), "adamw_mod_w")
    u_kv_mod_w = [a[0] for a in _adamw_outer(sct, dm_kv.astype(BF16)[None], kv_mod_w[None], m_kv_mod_w[None],
                                             v_kv_mod_w[None], min(tu, d), "adamw_kv_mod_w")]

    modb_parts = jnp.stack([dmod_all[:, 6 * l:6 * l + 6].reshape(N_DEV, 6 * d) for l in range(nl)], axis=1)
    u_mod_b = update([modb_parts], mod_b, m_mod_b, v_mod_b, "adamw_mod_b")
    u_norm_g = update([dng_parts], norm_g.reshape(8, dsh), m_norm_g.reshape(8, dsh), v_norm_g.reshape(8, dsh), "adamw_norm_g")
    u_norm_g = [a.reshape(norm_g.shape) for a in u_norm_g]
    u_conv_k = update([dck_parts], conv_k, m_conv_k, v_conv_k, "adamw_conv_k")
    kvb_parts = dmod_all[:, R_KV:R_KV + 2].reshape(N_DEV, 1, 2 * d)
    u_kv_mod_b = [a.reshape(kv_mod_b.shape) for a in update([kvb_parts], kv_mod_b.reshape(1, -1), m_kv_mod_b.reshape(1, -1),
                                                            v_kv_mod_b.reshape(1, -1), "adamw_kv_mod_b")]
    u_kv_norm_g = [a.reshape(kv_norm_g.shape) for a in update([dkvg_parts], kv_norm_g.reshape(1, -1), m_kv_norm_g.reshape(1, -1),
                                                              v_kv_norm_g.reshape(1, -1), "adamw_kv_norm_g")]
    u_rel = update([drel_parts], rel_bias, m_rel_bias, v_rel_bias, "adamw_rel_bias")

    others = [u_ffn_in, u_ffn_out, u_w_kv, u_w_q, u_w_o, u_mod_w, u_kv_mod_w, u_mod_b, u_norm_g, u_conv_k, u_kv_mod_b,
              u_kv_norm_g, u_rel]
    p_wci, p_wco = _xwait(h_g4, [u[3] for u in others], "grads_wait_conv")
    u_conv_in = update([p_wci], conv_w_in, m_conv_w_in, v_conv_w_in, "adamw_conv_in", parts_t=True)
    u_conv_out = update([p_wco], conv_w_out, m_conv_w_out, v_conv_w_out, "adamw_conv_out")

    ups = [u_mod_w, u_mod_b, u_norm_g, u_ffn_in, u_ffn_out, u_conv_in, u_conv_k, u_conv_out, u_kv_mod_w, u_kv_mod_b,
           u_kv_norm_g, u_w_kv, u_w_q, u_w_o, u_rel]
    return (loss, dx0[None], *[u[0] for u in ups], *[u[1] for u in ups], *[u[2] for u in ups], *[u[3] for u in ups])
```

```python
import jax
import jax.numpy as jnp
from jax import lax
from jax.experimental import pallas as pl
from jax.experimental.pallas import tpu as pltpu

F32 = jnp.float32
BF16 = jnp.bfloat16

EPS = 1e-6
CHUNK = 64
HEAD_DIM = 64
N_LEFT = 8
LANES = 128
MXU = 256
FFN_CHUNK = 4
QB = 4 * CHUNK
KW = QB + N_LEFT * CHUNK
BIAS_VARIANTS = N_LEFT * CHUNK // QB + 1
ATTN_PER = 8
NEG = -1e30
N_DEV = 8

ADAM_LR = 0.001
ADAM_B1 = 0.9
ADAM_B2 = 0.999
ADAM_EPS = 1e-08
ADAM_WD = 0.01
ADAM_STEP = 10

VMEM_BIG = 56 * 1024 * 1024

NT = (((1,), (1,)), ((), ()))
TN = (((0,), (0,)), ((), ()))

R_W1, R_SH1, R_P1, R_W2, R_SH2, R_P2 = range(6)
R_KV = 12


def _params(vmem):
    return pltpu.CompilerParams(vmem_limit_bytes=vmem)


def _row_tile(rows, cap):
    for t in range(min(cap, rows) // 16 * 16, 0, -16):
        if rows % t == 0:
            return t
    return rows


def _rows(tm, cols):
    return pl.BlockSpec((tm, cols), lambda i: (i, 0))


def _const(shape):
    nd = len(shape)
    return pl.BlockSpec(shape, lambda *_: (0,) * nd, pipeline_mode=pl.Buffered(1))


def _rs(x):
    return lax.rsqrt(jnp.mean(x * x, axis=-1, keepdims=True) + EPS)


def _norm_bwd(d, n, r):
    return r * (d - n * jnp.mean(d * n, axis=-1, keepdims=True))


def _colsum(a):
    return jnp.sum(a, axis=0, keepdims=True)


def _sigmoid(g):
    return 1.0 / (1.0 + jnp.exp(-g))


def _exchange(arrays, modes, name):
    n = len(arrays)
    out_shape = []
    for a, mode in zip(arrays, modes):
        shp = (N_DEV,) + a.shape if mode == "gather" else a.shape
        out_shape.append(jax.ShapeDtypeStruct(shp, a.dtype))

    def body(*refs):
        ins, outs = refs[:n], refs[n:2 * n]
        send_sems, recv_sems, local_sems = refs[2 * n:]
        x, y, c = lax.axis_index("x"), lax.axis_index("y"), lax.axis_index("c")
        me = 4 * x + 2 * y + c
        local, sends, recvs = [], [], []
        for a in range(n):
            own = ins[a] if modes[a] == "gather" else ins[a].at[me]
            cp = pltpu.make_async_copy(own, outs[a].at[me], local_sems.at[a])
            cp.start()
            local.append(cp)
        for k in range(1, N_DEV):
            px = 1 - x if k & 4 else x
            py = 1 - y if k & 2 else y
            pc = 1 - c if k & 1 else c
            peer = 4 * px + 2 * py + pc
            for a in range(n):
                src = ins[a] if modes[a] == "gather" else ins[a].at[peer]
                sem = a * (N_DEV - 1) + k - 1
                cp = pltpu.make_async_remote_copy(
                    src_ref=src, dst_ref=outs[a].at[me],
                    send_sem=send_sems.at[sem], recv_sem=recv_sems.at[sem],
                    device_id=(px, py, pc), device_id_type=pl.DeviceIdType.MESH)
                cp.start()
                sends.append(cp)
                recvs.append(pltpu.make_async_remote_copy(
                    src_ref=src, dst_ref=outs[a].at[peer],
                    send_sem=send_sems.at[sem], recv_sem=recv_sems.at[sem],
                    device_id=(px, py, pc), device_id_type=pl.DeviceIdType.MESH))
        for cp in recvs:
            cp.wait_recv()
        for cp in sends:
            cp.wait_send()
        for cp in local:
            cp.wait()

    any_spec = pl.BlockSpec(memory_space=pl.ANY)
    return pl.pallas_call(
        body, name=name,
        out_shape=tuple(out_shape),
        in_specs=[any_spec] * n,
        out_specs=tuple([any_spec] * n),
        scratch_shapes=[
            pltpu.SemaphoreType.DMA((n * (N_DEV - 1),)),
            pltpu.SemaphoreType.DMA((n * (N_DEV - 1),)),
            pltpu.SemaphoreType.DMA((n,)),
        ],
    )(*arrays)


def _peers(x, y, c):
    out = []
    for k in range(1, N_DEV):
        px = 1 - x if k & 4 else x
        py = 1 - y if k & 2 else y
        pc = 1 - c if k & 1 else c
        out.append((k - 1, (px, py, pc), 4 * px + 2 * py + pc))
    return out


_OTHER_CORE_SLOTS = (2, 4, 6)


def _land_shape(a, mode):
    return a.shape if mode == "scatter" else (N_DEV,) + a.shape


_HBM = pl.BlockSpec(memory_space=pltpu.HBM)
_SEM = pl.BlockSpec(memory_space=pltpu.SEMAPHORE)
_EFFECT = pltpu.SideEffectType.DATAFLOW_SIDE_EFFECTING


def _xstart(groups, after, name):
    flat = [(a, m) for arrays, modes in groups for a, m in zip(arrays, modes)]
    n, ngr = len(flat), len(groups)
    sizes = [len(arrays) for arrays, _ in groups]
    npeer = N_DEV - 1

    def body(*refs):
        ins, lands = refs[:n], refs[n:2 * n]
        outs = refs[2 * n + 1:]
        sems = outs[:2 * ngr]
        token = outs[2 * ngr + 2 * n]
        local_sems = outs[2 * ngr + 2 * n + 1]
        stage = outs[2 * ngr + 2 * n + 2:]
        x, y, c = lax.axis_index("x"), lax.axis_index("y"), lax.axis_index("c")
        me = 4 * x + 2 * y + c
        loads, stores = [], []
        for a in range(n):
            own = ins[a].at[me] if flat[a][1] == "scatter" else ins[a]
            loads.append(pltpu.make_async_copy(own, stage[a], local_sems.at[a]))
            stores.append(pltpu.make_async_copy(stage[a], lands[a].at[me], local_sems.at[a]))
            loads[a].start()
        for a in range(n):
            loads[a].wait()
            stores[a].start()
        a = 0
        for g in range(ngr):
            for j in range(sizes[g]):
                mode = flat[a][1]
                for slot, peer, pidx in _peers(x, y, c):
                    if mode == "gather_half" and slot in _OTHER_CORE_SLOTS:
                        continue
                    pltpu.make_async_remote_copy(
                        src_ref=ins[a].at[pidx] if mode == "scatter" else ins[a], dst_ref=lands[a].at[me],
                        send_sem=sems[2 * g].at[j * npeer + slot], recv_sem=sems[2 * g + 1].at[j * npeer + slot],
                        device_id=peer, device_id_type=pl.DeviceIdType.MESH).start()
                a += 1
        for cp in stores:
            cp.wait()
        token[...] = jnp.zeros_like(token)

    out_shape, out_specs = [], []
    for sz in sizes:
        out_shape += [pltpu.SemaphoreType.DMA((sz * npeer,)), pltpu.SemaphoreType.DMA((sz * npeer,))]
        out_specs += [_SEM, _SEM]
    out_shape += [pltpu.HBM(a.shape, a.dtype) for a, _ in flat]
    out_shape += [pltpu.HBM(_land_shape(a, m), a.dtype) for a, m in flat]
    out_specs += [_HBM] * (2 * n)
    out_shape.append(jax.ShapeDtypeStruct((8, LANES), F32))
    out_specs.append(pl.BlockSpec(memory_space=pltpu.VMEM))
    args = [pltpu.with_memory_space_constraint(a, pltpu.HBM) for a, _ in flat]
    args += [pltpu.with_memory_space_constraint(lax.empty(_land_shape(a, m), a.dtype), pltpu.HBM) for a, m in flat]
    res = pl.pallas_call(
        body, name=name, out_shape=tuple(out_shape),
        in_specs=[_HBM] * (2 * n) + [pl.BlockSpec(memory_space=pl.ANY)], out_specs=tuple(out_specs),
        input_output_aliases={i: 2 * ngr + i for i in range(2 * n)},
        scratch_shapes=[pltpu.SemaphoreType.DMA((n,))]
                       + [pltpu.VMEM(a.shape[1:] if m == "scatter" else a.shape, a.dtype) for a, m in flat],
        compiler_params=pltpu.CompilerParams(has_side_effects=_EFFECT, vmem_limit_bytes=VMEM_BIG),
    )(*args, after)
    handles, a = [], 0
    for g, sz in enumerate(sizes):
        handles.append((res[2 * g], res[2 * g + 1], list(res[2 * ngr + a:2 * ngr + a + sz]),
                        list(res[2 * ngr + n + a:2 * ngr + n + a + sz]), list(groups[g][1])))
        a += sz
    return handles, res[-1]


def _xwait(handle, after, name):
    send_sems, recv_sems, srcs, lands, modes = handle
    m = len(srcs)
    npeer = N_DEV - 1
    after = list(after)

    def body(*refs):
        ins, lnd = refs[:m], refs[m:2 * m]
        ssem, rsem = refs[2 * m], refs[2 * m + 1]
        x, y, c = lax.axis_index("x"), lax.axis_index("y"), lax.axis_index("c")
        for j in range(m):
            for slot, peer, pidx in _peers(x, y, c):
                if modes[j] == "gather_half" and slot in _OTHER_CORE_SLOTS:
                    continue
                cp = pltpu.make_async_remote_copy(
                    src_ref=ins[j].at[pidx] if modes[j] == "scatter" else ins[j], dst_ref=lnd[j].at[pidx],
                    send_sem=ssem.at[j * npeer + slot], recv_sem=rsem.at[j * npeer + slot],
                    device_id=peer, device_id_type=pl.DeviceIdType.MESH)
                cp.wait_send()
                cp.wait_recv()

    res = pl.pallas_call(
        body, name=name,
        out_shape=tuple([pltpu.HBM(a.shape, a.dtype) for a in srcs] + [pltpu.HBM(a.shape, a.dtype) for a in lands]),
        in_specs=[_HBM] * (2 * m) + [_SEM, _SEM] + [pl.BlockSpec(memory_space=pl.ANY)] * len(after),
        out_specs=tuple([_HBM] * (2 * m)),
        input_output_aliases={i: i for i in range(2 * m)},
        compiler_params=pltpu.CompilerParams(has_side_effects=_EFFECT),
    )(*srcs, *lands, send_sems, recv_sems, *after)
    return list(res[m:])


def _forward_to_sibling(lands, name):
    n = len(lands)

    def body(*refs):
        outs = refs[n:2 * n]
        send_sems, recv_sems = refs[2 * n:]
        x, y, c = lax.axis_index("x"), lax.axis_index("y"), lax.axis_index("c")
        sibling = (x, y, 1 - c)
        sends, recvs = [], []
        for a in range(n):
            for j, k in enumerate((2, 4, 6)):
                px = 1 - x if k & 4 else x
                py = 1 - y if k & 2 else y
                got = 4 * px + 2 * py + c
                missing = 4 * px + 2 * py + (1 - c)
                sem = a * 3 + j
                cp = pltpu.make_async_remote_copy(
                    src_ref=outs[a].at[got], dst_ref=outs[a].at[got],
                    send_sem=send_sems.at[sem], recv_sem=recv_sems.at[sem],
                    device_id=sibling, device_id_type=pl.DeviceIdType.MESH)
                cp.start()
                sends.append(cp)
                recvs.append(pltpu.make_async_remote_copy(
                    src_ref=outs[a].at[missing], dst_ref=outs[a].at[missing],
                    send_sem=send_sems.at[sem], recv_sem=recv_sems.at[sem],
                    device_id=sibling, device_id_type=pl.DeviceIdType.MESH))
        for cp in recvs:
            cp.wait_recv()
        for cp in sends:
            cp.wait_send()

    any_spec = pl.BlockSpec(memory_space=pl.ANY)
    return list(pl.pallas_call(
        body, name=name,
        out_shape=tuple(jax.ShapeDtypeStruct(a.shape, a.dtype) for a in lands),
        in_specs=[any_spec] * n, out_specs=tuple([any_spec] * n),
        input_output_aliases={i: i for i in range(n)},
        scratch_shapes=[pltpu.SemaphoreType.DMA((3 * n,)), pltpu.SemaphoreType.DMA((3 * n,))],
    )(*lands))


def _mod_fwd(c_all, mod_w, kv_mod_w):
    nl, d, mw = mod_w.shape
    kw = kv_mod_w.shape[1]

    def body(c_ref, mw_ref, kw_ref, o_ref, sc_ref):
        cc = c_ref[...]
        sc = (cc * _sigmoid(cc)).astype(BF16)
        sc_ref[...] = sc
        for l in range(nl):
            o_ref[:, l * mw:(l + 1) * mw] = jnp.dot(sc, mw_ref[l].astype(BF16), preferred_element_type=F32)
        o_ref[:, nl * mw:nl * mw + kw] = jnp.dot(sc, kw_ref[...].astype(BF16), preferred_element_type=F32)

    return pl.pallas_call(
        body, name="mod_fwd",
        out_shape=(jax.ShapeDtypeStruct((c_all.shape[0], nl * mw + kw), F32),
                   jax.ShapeDtypeStruct(c_all.shape, BF16)),
        compiler_params=_params(VMEM_BIG),
    )(c_all, mod_w, kv_mod_w)


def _vec_prep(modrow, modb, kvrow, kvb, ng, kvg):
    d = ng.shape[1]

    def body(mr_ref, mb_ref, kr_ref, kb_ref, ng_ref, kvg_ref, t_ref, m_ref):
        t_ref[...] = jnp.zeros_like(t_ref)
        m_ref[...] = jnp.zeros_like(m_ref)
        for l in range(2):
            mod = mr_ref[l] + mb_ref[l]
            m_ref[6 * l:6 * l + 6, :] = mod
            g = ng_ref[4 * l:4 * l + 4, :]
            t_ref[6 * l + R_W1:6 * l + R_W1 + 1, :] = g[0:1] * (1.0 + mod[1:2])
            t_ref[6 * l + R_SH1:6 * l + R_SH1 + 1, :] = mod[0:1]
            t_ref[6 * l + R_P1:6 * l + R_P1 + 1, :] = mod[2:3] * g[1:2]
            t_ref[6 * l + R_W2:6 * l + R_W2 + 1, :] = g[2:3] * (1.0 + mod[4:5])
            t_ref[6 * l + R_SH2:6 * l + R_SH2 + 1, :] = mod[3:4]
            t_ref[6 * l + R_P2:6 * l + R_P2 + 1, :] = mod[5:6] * g[3:4]
        kv = kr_ref[...] + kb_ref[...]
        m_ref[R_KV:R_KV + 2, :] = kv
        t_ref[R_KV:R_KV + 1, :] = kvg_ref[...] * (1.0 + kv[1:2])
        t_ref[R_KV + 1:R_KV + 2, :] = kv[0:1]

    return pl.pallas_call(
        body, name="vec_prep",
        out_shape=(jax.ShapeDtypeStruct((16, d), F32), jax.ShapeDtypeStruct((16, d), F32)),
    )(modrow, modb, kvrow, kvb, ng, kvg)


def _vec_bwd(sums_c, sums_f0, sums_q, sums_o, sums_f1, mt, ng, kvg):
    d = ng.shape[1]

    def body(sc_ref, sf0_ref, sq_ref, so_ref, sf1_ref, m_ref, ng_ref, kvg_ref, dm_ref, dng_ref, dkvg_ref, g_ref):
        g_ref[...] = jnp.zeros_like(g_ref)
        g_ref[0:3, :] = sc_ref[0:3, :]
        g_ref[3:6, :] = sf0_ref[3:6, :]
        g_ref[6:8, :] = sq_ref[0:2, :]
        g_ref[8:9, :] = so_ref[2:3, :]
        g_ref[9:12, :] = sf1_ref[3:6, :]
        g_ref[R_KV:R_KV + 2, :] = sq_ref[2:4, :]
        dm_ref[...] = jnp.zeros_like(dm_ref)
        dkvg_ref[...] = jnp.zeros_like(dkvg_ref)
        for l in range(2):
            g = ng_ref[4 * l:4 * l + 4, :]
            mod = m_ref[6 * l:6 * l + 6, :]
            s = g_ref[6 * l:6 * l + 6, :]
            dm_ref[6 * l + 0:6 * l + 1, :] = s[1:2]
            dm_ref[6 * l + 1:6 * l + 2, :] = s[0:1] * g[0:1]
            dm_ref[6 * l + 2:6 * l + 3, :] = s[2:3] * g[1:2]
            dm_ref[6 * l + 3:6 * l + 4, :] = s[4:5]
            dm_ref[6 * l + 4:6 * l + 5, :] = s[3:4] * g[2:3]
            dm_ref[6 * l + 5:6 * l + 6, :] = s[5:6] * g[3:4]
            dng_ref[4 * l + 0:4 * l + 1, :] = s[0:1] * (1.0 + mod[1:2])
            dng_ref[4 * l + 1:4 * l + 2, :] = s[2:3] * mod[2:3]
            dng_ref[4 * l + 2:4 * l + 3, :] = s[3:4] * (1.0 + mod[4:5])
            dng_ref[4 * l + 3:4 * l + 4, :] = s[5:6] * mod[5:6]
        dm_ref[R_KV:R_KV + 1, :] = g_ref[R_KV + 1:R_KV + 2, :]
        dm_ref[R_KV + 1:R_KV + 2, :] = g_ref[R_KV:R_KV + 1, :] * kvg_ref[...]
        dkvg_ref[0:1, :] = g_ref[R_KV:R_KV + 1, :] * (1.0 + m_ref[R_KV + 1:R_KV + 2, :])

    return pl.pallas_call(
        body, name="vec_bwd",
        out_shape=(jax.ShapeDtypeStruct((16, d), F32), jax.ShapeDtypeStruct((8, d), F32),
                   jax.ShapeDtypeStruct((8, d), F32)),
        scratch_shapes=[pltpu.VMEM((16, d), F32)],
    )(sums_c, sums_f0, sums_q, sums_o, sums_f1, mt, ng, kvg)


def _rel_index(nrel):
    width = KW + QB
    e = lax.broadcasted_iota(jnp.int32, (nrel, width), 1)
    r = lax.broadcasted_iota(jnp.int32, (nrel, width), 0)
    max_rel = (nrel - 1) // 2
    idx = jnp.clip(KW - e, -max_rel, max_rel) + max_rel
    return (idx == r).astype(F32)


def _band_valid():
    row = lax.broadcasted_iota(jnp.int32, (QB, KW), 0) // CHUNK
    col = lax.broadcasted_iota(jnp.int32, (QB, KW), 1) // CHUNK
    j = col - row
    return (j >= 0) & (j <= N_LEFT)


def _bias_fwd(rel_bias):
    nh, nrel = rel_bias.shape
    width = KW + QB

    def body(rb_ref, o_ref):
        onehot = _rel_index(nrel)
        gr = jnp.dot(rb_ref[...], onehot, preferred_element_type=F32, precision=lax.Precision.HIGHEST)
        valid = _band_valid() & _key_valid(pl.program_id(0))
        for h in range(nh):
            xrow = jnp.broadcast_to(gr[h:h + 1, :], (QB, width))
            rolled = pltpu.roll(xrow, 0, 1, stride=1, stride_axis=0)
            o_ref[h] = jnp.where(valid, rolled[:, QB:], NEG)

    return pl.pallas_call(
        body, name="bias_fwd", grid=(BIAS_VARIANTS,),
        in_specs=[pl.BlockSpec(rel_bias.shape, lambda v: (0, 0))],
        out_specs=pl.BlockSpec((None, nh, QB, KW), lambda v: (v, 0, 0, 0)),
        out_shape=jax.ShapeDtypeStruct((BIAS_VARIANTS, nh, QB, KW), F32),
        compiler_params=_params(VMEM_BIG),
    )(rel_bias)


def _bias_bwd(dbias, nrel):
    nh = dbias.shape[0]
    width = KW + QB

    def body(db_ref, o_ref, diag_ref):
        onehot = _rel_index(nrel)
        valid = _band_valid()
        rr = lax.broadcasted_iota(jnp.int32, (QB, QB), 0)
        cc = lax.broadcasted_iota(jnp.int32, (QB, QB), 1)
        flip = (rr + cc == QB - 1).astype(F32)
        for h in range(nh):
            rev = jnp.dot(flip, jnp.where(valid, db_ref[h], 0.0), preferred_element_type=F32,
                          precision=lax.Precision.HIGHEST)
            w = jnp.concatenate([jnp.zeros((QB, QB), F32), rev], axis=1)
            back = pltpu.roll(w, width - (QB - 1), 1, stride=1, stride_axis=0)
            diag_ref[h:h + 1, :] = _colsum(back)
        o_ref[...] = lax.dot_general(diag_ref[...], onehot, NT, preferred_element_type=F32,
                                     precision=lax.Precision.HIGHEST)

    return pl.pallas_call(
        body, name="bias_bwd",
        out_shape=jax.ShapeDtypeStruct((nh, nrel), F32),
        scratch_shapes=[pltpu.VMEM((nh, width), F32)],
        compiler_params=_params(VMEM_BIG),
    )(dbias)


def _conv_fwd(x, tab, ck, wci, wco, tm):
    s, d = x.shape

    def body(x_ref, t_ref, ck_ref, wci_ref, wco_ref, x1_ref, h_ref, bcx_ref, u_ref, y_ref, carry):
        @pl.when(pl.program_id(0) == 0)
        def _():
            carry[...] = jnp.zeros_like(carry)

        xv = x_ref[...]
        hb = ((xv * _rs(xv)) * t_ref[R_W1:R_W1 + 1, :] + t_ref[R_SH1:R_SH1 + 1, :]).astype(BF16)
        h_ref[...] = hb
        for j in range(3 * d // MXU):
            bcx_ref[:, j * MXU:(j + 1) * MXU] = lax.dot_general(hb, wci_ref[j * MXU:(j + 1) * MXU, :], NT,
                                                                preferred_element_type=F32)
        bg, cg, xi = bcx_ref[:, 0:d], bcx_ref[:, d:2 * d], bcx_ref[:, 2 * d:3 * d]
        z = cg * xi
        row = lax.broadcasted_iota(jnp.int32, z.shape, 0)
        c1, c2 = carry[7:8, :], carry[6:7, :]
        z1 = jnp.where(row == 0, c1, pltpu.roll(z, 1, 0))
        z2 = jnp.where(row == 0, c2, jnp.where(row == 1, c1, pltpu.roll(z, 2, 0)))
        carry[...] = z[tm - 8:tm, :]
        conv = ck_ref[0:1, :] * z2 + ck_ref[1:2, :] * z1 + ck_ref[2:3, :] * z
        ub = (bg * conv).astype(BF16)
        u_ref[...] = ub
        yv = jnp.dot(ub, wco_ref[...], preferred_element_type=F32)
        y_ref[...] = yv
        x1_ref[...] = xv + (yv * _rs(yv)) * t_ref[R_P1:R_P1 + 1, :]

    return pl.pallas_call(
        body, name="conv_fwd", grid=(s // tm,),
        in_specs=[_rows(tm, d), _const(tab.shape), _const(ck.shape), _const(wci.shape), _const(wco.shape)],
        out_specs=(_rows(tm, d), _rows(tm, d), _rows(tm, 3 * d), _rows(tm, d), _rows(tm, d)),
        out_shape=(jax.ShapeDtypeStruct((s, d), F32), jax.ShapeDtypeStruct((s, d), BF16),
                   jax.ShapeDtypeStruct((s, 3 * d), F32), jax.ShapeDtypeStruct((s, d), BF16),
                   jax.ShapeDtypeStruct((s, d), F32)),
        scratch_shapes=[pltpu.VMEM((8, d), F32)],
        compiler_params=_params(VMEM_BIG),
    )(x, tab, ck, wci, wco)


def _ffn_fwd(x, tab, base, wfi, wfo, tgt, tm, name):
    s, d = x.shape
    hid = wfo.shape[0]
    nblk = hid // MXU
    with_loss = tgt is not None

    def body(*refs):
        if with_loss:
            x_ref, t_ref, wfi_ref, wfo_ref, tgt_ref, xo_ref, h_ref, gu_ref, a_scr, y_ref, loss_ref = refs
        else:
            x_ref, t_ref, wfi_ref, wfo_ref, xo_ref, h_ref, gu_ref, a_scr, y_ref = refs
        xv = x_ref[...]
        hb = ((xv * _rs(xv)) * t_ref[base + R_W2:base + R_W2 + 1, :]
              + t_ref[base + R_SH2:base + R_SH2 + 1, :]).astype(BF16)
        h_ref[...] = hb
        acc = jnp.zeros((tm, d), F32)
        for c0 in range(0, nblk, FFN_CHUNK):
            for j in range(c0, min(c0 + FFN_CHUNK, nblk)):
                lo, hi = j * MXU, (j + 1) * MXU
                g = lax.dot_general(hb, wfi_ref[lo:hi, :], NT, preferred_element_type=F32)
                u = lax.dot_general(hb, wfi_ref[hid + lo:hid + hi, :], NT, preferred_element_type=F32)
                gu_ref[:, lo:hi] = g.astype(BF16)
                gu_ref[:, hid + lo:hid + hi] = u.astype(BF16)
                a_scr[:, lo:hi] = ((g * _sigmoid(g)) * u).astype(BF16)
            lo, hi = c0 * MXU, min(c0 + FFN_CHUNK, nblk) * MXU
            acc = acc + jnp.dot(a_scr[:, lo:hi], wfo_ref[lo:hi, :], preferred_element_type=F32)
        y_ref[...] = acc
        xo = xv + (acc * _rs(acc)) * t_ref[base + R_P2:base + R_P2 + 1, :]
        if with_loss:
            @pl.when(pl.program_id(0) == 0)
            def _():
                loss_ref[...] = jnp.zeros_like(loss_ref)

            err = xo - tgt_ref[...]
            xo_ref[...] = err * (1.0 / d)
            e2 = jnp.sum((err * err).reshape(tm // 8, 8, d), axis=0)
            for q in range(d // LANES):
                loss_ref[...] += e2[:, q * LANES:(q + 1) * LANES]
        else:
            xo_ref[...] = xo

    in_specs = [_rows(tm, d), _const(tab.shape), _const(wfi.shape), _const(wfo.shape)]
    args = [x, tab, wfi, wfo]
    out_specs = [_rows(tm, d), _rows(tm, d), _rows(tm, 2 * hid), _rows(tm, hid), _rows(tm, d)]
    out_shape = [jax.ShapeDtypeStruct((s, d), F32), jax.ShapeDtypeStruct((s, d), BF16),
                 jax.ShapeDtypeStruct((s, 2 * hid), BF16), jax.ShapeDtypeStruct((s, hid), BF16),
                 jax.ShapeDtypeStruct((s, d), F32)]
    if with_loss:
        in_specs.append(_rows(tm, d))
        args.append(tgt)
        out_specs.append(pl.BlockSpec((8, LANES), lambda i: (0, 0)))
        out_shape.append(jax.ShapeDtypeStruct((8, LANES), F32))
    return pl.pallas_call(
        body, name=name, grid=(s // tm,), in_specs=in_specs, out_specs=tuple(out_specs),
        out_shape=tuple(out_shape), compiler_params=_params(VMEM_BIG),
    )(*args)


def _qkv_fwd(x, tab, wq, wkv, tm):
    s, d = x.shape
    base = 6

    def body(x_ref, t_ref, wq_ref, wkv_ref, hkv_ref, h1_ref, q_ref, k_ref, v_ref):
        xv = x_ref[...]
        n = xv * _rs(xv)
        hkv = (n * t_ref[R_KV:R_KV + 1, :] + t_ref[R_KV + 1:R_KV + 2, :]).astype(BF16)
        h1 = (n * t_ref[base + R_W1:base + R_W1 + 1, :] + t_ref[base + R_SH1:base + R_SH1 + 1, :]).astype(BF16)
        hkv_ref[...] = hkv
        h1_ref[...] = h1
        q_ref[...] = (jnp.dot(h1, wq_ref[...], preferred_element_type=F32) * (HEAD_DIM ** -0.5)).astype(BF16)
        for j in range(d // MXU):
            lo, hi = j * MXU, (j + 1) * MXU
            k_ref[:, lo:hi] = lax.dot_general(hkv, wkv_ref[lo:hi, :], NT, preferred_element_type=F32).astype(BF16)
            v_ref[:, lo:hi] = lax.dot_general(hkv, wkv_ref[d + lo:d + hi, :], NT,
                                              preferred_element_type=F32).astype(BF16)

    act = jax.ShapeDtypeStruct((s, d), BF16)
    return pl.pallas_call(
        body, name="qkv_fwd", grid=(s // tm,),
        in_specs=[_rows(tm, d), _const(tab.shape), _const(wq.shape), _const(wkv.shape)],
        out_specs=tuple([_rows(tm, d)] * 5), out_shape=(act,) * 5,
        compiler_params=_params(VMEM_BIG),
    )(x, tab, wq, wkv)


def _window_specs(per=1):
    return [pl.BlockSpec((QB, LANES), (lambda p, b, w=w: (jnp.maximum(per * b - 2 + w, 0), p)))
            for w in range(2 + per)]


def _key_valid(b):
    col = lax.broadcasted_iota(jnp.int32, (QB, KW), 1) // CHUNK
    return (b * (QB // CHUNK) - N_LEFT + col) >= 0


def _bias_spec(per=1, sub=0):
    return pl.BlockSpec((None, LANES // HEAD_DIM, QB, KW),
                        lambda p, b: (jnp.minimum(per * b + sub, BIAS_VARIANTS - 1), p, 0, 0))


def _head_masks():
    lane = lax.broadcasted_iota(jnp.int32, (1, LANES), 1)
    return [(lane // HEAD_DIM == hh) for hh in range(LANES // HEAD_DIM)]


def _attn_fwd(q, k, v, bias):
    s, d = q.shape
    per = ATTN_PER
    npair, nb = d // LANES, s // (per * QB)
    hpp = LANES // HEAD_DIM
    nwin = 2 + per
    nbias = min(per, BIAS_VARIANTS)

    def body(*refs):
        q_ref, k_refs, v_refs = refs[0], refs[1:1 + nwin], refs[1 + nwin:1 + 2 * nwin]
        bias_refs = refs[1 + 2 * nwin:1 + 2 * nwin + nbias]
        o_ref, lse_ref = refs[1 + 2 * nwin + nbias:]
        ks = [r[...] for r in k_refs]
        vs = [r[...] for r in v_refs]
        masks = _head_masks()
        for sub in range(per):
            rows = slice(sub * QB, (sub + 1) * QB)
            qv = q_ref[rows, :]
            kwin = jnp.concatenate(ks[sub:sub + 3], axis=0)
            vwin = jnp.concatenate(vs[sub:sub + 3], axis=0)
            o = jnp.zeros((QB, LANES), F32)
            lse = jnp.zeros((QB, LANES), F32)
            scs = [lax.dot_general(jnp.where(masks[hh], qv, jnp.zeros_like(qv)), kwin, NT,
                                   preferred_element_type=F32) + bias_refs[min(sub, nbias - 1)][hh]
                   for hh in range(hpp)]
            for hh in range(hpp):
                vm = jnp.where(masks[hh], vwin, jnp.zeros_like(vwin))
                sc = scs[hh]
                m = jnp.max(sc, axis=-1, keepdims=True)
                p = jnp.exp(sc - m)
                l = jnp.sum(p, axis=-1, keepdims=True)
                o = o + jnp.dot(p.astype(BF16), vm, preferred_element_type=F32) * (1.0 / l)
                lse = jnp.where(masks[hh], m + jnp.log(l), lse)
            o_ref[rows, :] = o.astype(BF16)
            lse_ref[rows, :] = lse

    blk = pl.BlockSpec((per * QB, LANES), lambda p, b: (b, p))
    return pl.pallas_call(
        body, name="attn_fwd", grid=(npair, nb),
        in_specs=[blk] + _window_specs(per) + _window_specs(per) + [_bias_spec(per, sub) for sub in range(nbias)],
        out_specs=(blk, blk),
        out_shape=(jax.ShapeDtypeStruct((s, d), BF16), jax.ShapeDtypeStruct((s, d), F32)),
        compiler_params=_params(VMEM_BIG),
    )(q, *([k] * nwin), *([v] * nwin), *([bias] * nbias))


def _attn_out_fwd(o, x, tab, wo, tm):
    s, d = x.shape
    base = 6

    def body(o_ref, x_ref, t_ref, wo_ref, x3_ref, y_ref):
        yv = jnp.dot(o_ref[...], wo_ref[...], preferred_element_type=F32)
        y_ref[...] = yv
        x3_ref[...] = x_ref[...] + (yv * _rs(yv)) * t_ref[base + R_P1:base + R_P1 + 1, :]

    return pl.pallas_call(
        body, name="attn_out_fwd", grid=(s // tm,),
        in_specs=[_rows(tm, d), _rows(tm, d), _const(tab.shape), _const(wo.shape)],
        out_specs=(_rows(tm, d), _rows(tm, d)),
        out_shape=(jax.ShapeDtypeStruct((s, d), F32), jax.ShapeDtypeStruct((s, d), F32)),
        compiler_params=_params(VMEM_BIG),
    )(o, x, tab, wo)


def _ffn_bwd(dxo, x, y, gu, tab, base, wfi, wfo, tm, name):
    s, d = x.shape
    hid = wfo.shape[0]
    nblk = hid // MXU

    def body(dxo_ref, x_ref, y_ref, gu_ref, t_ref, wfi_ref, wfo_ref, dx_ref, dyb_ref, dgu_ref, sums_ref):
        @pl.when(pl.program_id(0) == 0)
        def _():
            sums_ref[...] = jnp.zeros_like(sums_ref)

        dxo_v = dxo_ref[...]
        yv = y_ref[...]
        ry = _rs(yv)
        ny = yv * ry
        sums_ref[R_P2:R_P2 + 1, :] += _colsum(dxo_v * ny)
        dyb = _norm_bwd(dxo_v * t_ref[base + R_P2:base + R_P2 + 1, :], ny, ry).astype(BF16)
        dyb_ref[...] = dyb
        dh = jnp.zeros((tm, d), F32)
        for c0 in range(0, nblk, FFN_CHUNK):
            for j in range(c0, min(c0 + FFN_CHUNK, nblk)):
                lo, hi = j * MXU, (j + 1) * MXU
                da = lax.dot_general(dyb, wfo_ref[lo:hi, :], NT, preferred_element_type=F32)
                g, u = gu_ref[:, lo:hi].astype(F32), gu_ref[:, hid + lo:hid + hi].astype(F32)
                sg = _sigmoid(g)
                gs = g * sg
                dgu_ref[:, lo:hi] = (da * u * sg * (1.0 + g * (1.0 - sg))).astype(BF16)
                dgu_ref[:, hid + lo:hid + hi] = (da * gs).astype(BF16)
            lo, hi = c0 * MXU, min(c0 + FFN_CHUNK, nblk) * MXU
            dh = dh + jnp.dot(dgu_ref[:, lo:hi], wfi_ref[lo:hi, :], preferred_element_type=F32)
            dh = dh + jnp.dot(dgu_ref[:, hid + lo:hid + hi], wfi_ref[hid + lo:hid + hi, :],
                              preferred_element_type=F32)
        xv = x_ref[...]
        r = _rs(xv)
        n = xv * r
        sums_ref[R_SH2:R_SH2 + 1, :] += _colsum(dh)
        sums_ref[R_W2:R_W2 + 1, :] += _colsum(dh * n)
        dx_ref[...] = dxo_v + _norm_bwd(dh * t_ref[base + R_W2:base + R_W2 + 1, :], n, r)

    return pl.pallas_call(
        body, name=name, grid=(s // tm,),
        in_specs=[_rows(tm, d), _rows(tm, d), _rows(tm, d), _rows(tm, 2 * hid),
                  _const(tab.shape), _const(wfi.shape), _const(wfo.shape)],
        out_specs=(_rows(tm, d), _rows(tm, d), _rows(tm, 2 * hid), pl.BlockSpec((8, d), lambda i: (0, 0))),
        out_shape=(jax.ShapeDtypeStruct((s, d), F32), jax.ShapeDtypeStruct((s, d), BF16),
                   jax.ShapeDtypeStruct((s, 2 * hid), BF16), jax.ShapeDtypeStruct((8, d), F32)),
        compiler_params=_params(VMEM_BIG),
    )(dxo, x, y, gu, tab, wfi, wfo)


def _attn_out_bwd(dx, y, tab, wo, tm):
    s, d = y.shape
    base = 6

    def body(dx_ref, y_ref, t_ref, wo_ref, dyb_ref, do_ref, sums_ref):
        @pl.when(pl.program_id(0) == 0)
        def _():
            sums_ref[...] = jnp.zeros_like(sums_ref)

        dxv = dx_ref[...]
        yv = y_ref[...]
        ry = _rs(yv)
        ny = yv * ry
        sums_ref[R_P1:R_P1 + 1, :] += _colsum(dxv * ny)
        dyb = _norm_bwd(dxv * t_ref[base + R_P1:base + R_P1 + 1, :], ny, ry).astype(BF16)
        dyb_ref[...] = dyb
        do_ref[...] = lax.dot_general(dyb, wo_ref[...], NT, preferred_element_type=F32).astype(BF16)

    return pl.pallas_call(
        body, name="attn_out_bwd", grid=(s // tm,),
        in_specs=[_rows(tm, d), _rows(tm, d), _const(tab.shape), _const(wo.shape)],
        out_specs=(_rows(tm, d), _rows(tm, d), pl.BlockSpec((8, d), lambda i: (0, 0))),
        out_shape=(jax.ShapeDtypeStruct((s, d), BF16), jax.ShapeDtypeStruct((s, d), BF16),
                   jax.ShapeDtypeStruct((8, d), F32)),
        compiler_params=_params(VMEM_BIG),
    )(dx, y, tab, wo)


def _attn_bwd(q, k, v, o, do, lse, bias):
    s, d = q.shape
    per = ATTN_PER
    npair, nb = d // LANES, s // (per * QB)
    hpp = LANES // HEAD_DIM
    nwin = 2 + per
    nbias = min(per, BIAS_VARIANTS)

    def body(*refs):
        q_ref, k_refs, v_refs = refs[0], refs[1:1 + nwin], refs[1 + nwin:1 + 2 * nwin]
        o_ref, do_ref, lse_ref = refs[1 + 2 * nwin:4 + 2 * nwin]
        bias_refs = refs[4 + 2 * nwin:4 + 2 * nwin + nbias]
        dq_ref, dk_ref, dv_ref, db_ref = refs[4 + 2 * nwin + nbias:]
        b = pl.program_id(1)

        @pl.when(b == 0)
        def _():
            dk_ref[...] = jnp.zeros_like(dk_ref)
            dv_ref[...] = jnp.zeros_like(dv_ref)
            db_ref[...] = jnp.zeros_like(db_ref)

        ks = [r[...] for r in k_refs]
        vs = [r[...] for r in v_refs]
        masks = _head_masks()
        for sub in range(per):
            rows = slice(sub * QB, (sub + 1) * QB)
            qv = q_ref[rows, :]
            dov = do_ref[rows, :]
            lsev = lse_ref[rows, :]
            doo = dov.astype(F32) * o_ref[rows, :].astype(F32)
            kwin = jnp.concatenate(ks[sub:sub + 3], axis=0)
            vwin = jnp.concatenate(vs[sub:sub + 3], axis=0)
            dq = jnp.zeros((QB, LANES), F32)
            dkt = jnp.zeros((LANES, KW), F32)
            dvt = jnp.zeros((LANES, KW), F32)
            for hh in range(hpp):
                qm = jnp.where(masks[hh], qv, jnp.zeros_like(qv))
                dom = jnp.where(masks[hh], dov, jnp.zeros_like(dov))
                km = jnp.where(masks[hh], kwin, jnp.zeros_like(kwin))
                lse_h = jnp.max(jnp.where(masks[hh], lsev, NEG), axis=-1, keepdims=True)
                delta = jnp.sum(jnp.where(masks[hh], doo, 0.0), axis=-1, keepdims=True)
                sc = lax.dot_general(qm, kwin, NT, preferred_element_type=F32) + bias_refs[min(sub, nbias - 1)][hh]
                p = jnp.exp(sc - lse_h)
                dp = lax.dot_general(dom, vwin, NT, preferred_element_type=F32)
                ds = p * (dp - delta)
                db_ref[hh] += ds
                dsb = ds.astype(BF16)
                dq = dq + jnp.dot(dsb, km, preferred_element_type=F32)
                dkt = dkt + jnp.dot(qm.T, dsb, preferred_element_type=F32)
                dvt = dvt + jnp.dot(dom.T, p.astype(BF16), preferred_element_type=F32)
            dkw, dvw = dkt.T, dvt.T
            dq_ref[rows, :] = (dq * (HEAD_DIM ** -0.5)).astype(BF16)
            for w in range(3):
                start = pl.multiple_of(jnp.maximum(per * b + sub - 2 + w, 0) * QB, QB)
                dk_ref[pl.ds(start, QB), :] += dkw[w * QB:(w + 1) * QB, :]
                dv_ref[pl.ds(start, QB), :] += dvw[w * QB:(w + 1) * QB, :]

    blk = pl.BlockSpec((per * QB, LANES), lambda p, b: (b, p))
    col = pl.BlockSpec((s, LANES), lambda p, b: (0, p))
    pair = pl.BlockSpec((hpp, QB, KW), lambda p, b: (p, 0, 0))
    return pl.pallas_call(
        body, name="attn_bwd", grid=(npair, nb),
        in_specs=[blk] + _window_specs(per) + _window_specs(per) + [blk, blk, blk]
                 + [_bias_spec(per, sub) for sub in range(nbias)],
        out_specs=(blk, col, col, pair),
        out_shape=(jax.ShapeDtypeStruct((s, d), BF16), jax.ShapeDtypeStruct((s, d), F32),
                   jax.ShapeDtypeStruct((s, d), F32), jax.ShapeDtypeStruct(bias.shape[1:], F32)),
        compiler_params=_params(VMEM_BIG),
    )(q, *([k] * nwin), *([v] * nwin), o, do, lse, *([bias] * nbias))


def _qkv_bwd(dres, dq, dk, dv, x, tab, wq, wkv, tm):
    s, d = x.shape
    base = 6

    def body(dres_ref, dq_ref, dk_ref, dv_ref, x_ref, t_ref, wq_ref, wkv_ref, dx_ref, dkv_ref, sums_ref):
        @pl.when(pl.program_id(0) == 0)
        def _():
            sums_ref[...] = jnp.zeros_like(sums_ref)

        dh1 = lax.dot_general(dq_ref[...], wq_ref[...], NT, preferred_element_type=F32)
        dkv_ref[:, 0:d] = dk_ref[...].astype(BF16)
        dkv_ref[:, d:2 * d] = dv_ref[...].astype(BF16)
        dhkv = jnp.dot(dkv_ref[...], wkv_ref[...], preferred_element_type=F32)
        xv = x_ref[...]
        r = _rs(xv)
        n = xv * r
        sums_ref[0:1, :] += _colsum(dh1 * n)
        sums_ref[1:2, :] += _colsum(dh1)
        sums_ref[2:3, :] += _colsum(dhkv * n)
        sums_ref[3:4, :] += _colsum(dhkv)
        dn = dh1 * t_ref[base + R_W1:base + R_W1 + 1, :] + dhkv * t_ref[R_KV:R_KV + 1, :]
        dx_ref[...] = dres_ref[...] + _norm_bwd(dn, n, r)

    return pl.pallas_call(
        body, name="qkv_bwd", grid=(s // tm,),
        in_specs=[_rows(tm, d)] * 5 + [_const(tab.shape), _const(wq.shape), _const(wkv.shape)],
        out_specs=(_rows(tm, d), _rows(tm, 2 * d), pl.BlockSpec((8, d), lambda i: (0, 0))),
        out_shape=(jax.ShapeDtypeStruct((s, d), F32), jax.ShapeDtypeStruct((s, 2 * d), BF16),
                   jax.ShapeDtypeStruct((8, d), F32)),
        compiler_params=_params(VMEM_BIG),
    )(dres, dq, dk, dv, x, tab, wq, wkv)


def _conv_bwd(dx1, x, y, bcx, tab, ck, wci, wco, tm):
    s, d = x.shape
    nt = s // tm

    def rev(i):
        return (nt - 1 - i, 0)

    def halo(i):
        return (jnp.maximum((nt - 1 - i) * (tm // 8) - 1, 0), 0)

    def body(dx_ref, x_ref, y_ref, bcx_ref, halo_ref, t_ref, ck_ref, wci_ref, wco_ref,
             dx0_ref, dyb_ref, dbcx_ref, sums_ref, dck_ref, carry):
        i = pl.program_id(0)

        @pl.when(i == 0)
        def _():
            sums_ref[...] = jnp.zeros_like(sums_ref)
            dck_ref[...] = jnp.zeros_like(dck_ref)
            carry[...] = jnp.zeros_like(carry)

        dxv = dx_ref[...]
        yv = y_ref[...]
        ry = _rs(yv)
        ny = yv * ry
        sums_ref[R_P1:R_P1 + 1, :] += _colsum(dxv * ny)
        dyb = _norm_bwd(dxv * t_ref[R_P1:R_P1 + 1, :], ny, ry).astype(BF16)
        dyb_ref[...] = dyb
        du = lax.dot_general(dyb, wco_ref[...], NT, preferred_element_type=F32)
        bg, cg, xi = bcx_ref[:, 0:d], bcx_ref[:, d:2 * d], bcx_ref[:, 2 * d:3 * d]
        z = cg * xi
        zp = halo_ref[:, d:2 * d] * halo_ref[:, 2 * d:3 * d]
        zp = jnp.where(i == nt - 1, jnp.zeros_like(zp), zp)
        row = lax.broadcasted_iota(jnp.int32, z.shape, 0)
        c1, c2 = zp[7:8, :], zp[6:7, :]
        z1 = jnp.where(row == 0, c1, pltpu.roll(z, 1, 0))
        z2 = jnp.where(row == 0, c2, jnp.where(row == 1, c1, pltpu.roll(z, 2, 0)))
        k0, k1, k2 = ck_ref[0:1, :], ck_ref[1:2, :], ck_ref[2:3, :]
        conv = k0 * z2 + k1 * z1 + k2 * z
        dconv = du * bg
        dck_ref[0:1, :] += _colsum(dconv * z2)
        dck_ref[1:2, :] += _colsum(dconv * z1)
        dck_ref[2:3, :] += _colsum(dconv * z)
        n1, n2 = carry[0:1, :], carry[1:2, :]
        d1 = jnp.where(row == tm - 1, n1, pltpu.roll(dconv, tm - 1, 0))
        d2 = jnp.where(row == tm - 1, n2, jnp.where(row == tm - 2, n1, pltpu.roll(dconv, tm - 2, 0)))
        carry[...] = dconv[0:8, :]
        dz = k2 * dconv + k1 * d1 + k0 * d2
        dbcx_ref[:, 0:d] = (du * conv).astype(BF16)
        dbcx_ref[:, d:2 * d] = (dz * xi).astype(BF16)
        dbcx_ref[:, 2 * d:3 * d] = (dz * cg).astype(BF16)
        dh = jnp.dot(dbcx_ref[...], wci_ref[...], preferred_element_type=F32)
        xv = x_ref[...]
        r = _rs(xv)
        n = xv * r
        sums_ref[R_W1:R_W1 + 1, :] += _colsum(dh * n)
        sums_ref[R_SH1:R_SH1 + 1, :] += _colsum(dh)
        dx0_ref[...] = dxv + _norm_bwd(dh * t_ref[R_W1:R_W1 + 1, :], n, r)

    rrow = lambda cols: pl.BlockSpec((tm, cols), rev)
    acc = pl.BlockSpec((8, d), lambda i: (0, 0))
    return pl.pallas_call(
        body, name="conv_bwd", grid=(nt,),
        in_specs=[rrow(d), rrow(d), rrow(d), rrow(3 * d), pl.BlockSpec((8, 3 * d), halo),
                  _const(tab.shape), _const(ck.shape), _const(wci.shape), _const(wco.shape)],
        out_specs=(rrow(d), rrow(d), rrow(3 * d), acc, acc),
        out_shape=(jax.ShapeDtypeStruct((s, d), F32), jax.ShapeDtypeStruct((s, d), BF16),
                   jax.ShapeDtypeStruct((s, 3 * d), BF16), jax.ShapeDtypeStruct((8, d), F32),
                   jax.ShapeDtypeStruct((8, d), F32)),
        scratch_shapes=[pltpu.VMEM((8, d), F32)],
        compiler_params=_params(VMEM_BIG),
    )(dx1, x, y, bcx, bcx, tab, ck, wci, wco)


def _wgrad_wide(a, b, nblk, tk, name):
    s, m = a.shape
    n = b.shape[1] // nblk
    nk = s // tk

    def body(a_ref, b_ref, o_ref, acc):
        kk = pl.program_id(0)

        @pl.when(kk == 0)
        def _():
            acc[...] = jnp.zeros_like(acc)

        acc[...] += jnp.dot(a_ref[...].T, b_ref[...], preferred_element_type=F32)

        @pl.when(kk == nk - 1)
        def _():
            for j in range(nblk):
                o_ref[j] = acc[:, j * n:(j + 1) * n].astype(BF16)

    return pl.pallas_call(
        body, name=name, grid=(nk,),
        in_specs=[pl.BlockSpec((tk, m), lambda kk: (kk, 0)), pl.BlockSpec((tk, nblk * n), lambda kk: (kk, 0))],
        out_specs=pl.BlockSpec((nblk, m, n), lambda kk: (0, 0, 0)),
        out_shape=jax.ShapeDtypeStruct((nblk, m, n), BF16),
        scratch_shapes=[pltpu.VMEM((m, nblk * n), F32)],
        compiler_params=_params(VMEM_BIG),
    )(a, b)


def _wgrad_rows(a, b, ncb, tk, name):
    s, m = a.shape
    n = b.shape[1]
    mb = m // ncb
    nk = s // tk

    def body(a_ref, b_ref, o_ref, acc):
        kk = pl.program_id(1)

        @pl.when(kk == 0)
        def _():
            acc[...] = jnp.zeros_like(acc)

        acc[...] += jnp.dot(a_ref[...].T, b_ref[...], preferred_element_type=F32)

        @pl.when(kk == nk - 1)
        def _():
            o_ref[...] = acc[...].astype(BF16)

    return pl.pallas_call(
        body, name=name, grid=(ncb, nk),
        in_specs=[pl.BlockSpec((tk, mb), lambda j, kk: (kk, j)), pl.BlockSpec((tk, n), lambda j, kk: (kk, 0))],
        out_specs=pl.BlockSpec((mb, n), lambda j, kk: (j, 0)),
        out_shape=jax.ShapeDtypeStruct((m, n), BF16),
        scratch_shapes=[pltpu.VMEM((mb, n), F32)],
        compiler_params=_params(VMEM_BIG),
    )(a, b)


def _adamw_math(w, g, m, v):
    m = ADAM_B1 * m + (1.0 - ADAM_B1) * g
    v = ADAM_B2 * v + (1.0 - ADAM_B2) * (g * g)
    m_hat = m / (1.0 - ADAM_B1 ** ADAM_STEP)
    v_hat = v / (1.0 - ADAM_B2 ** ADAM_STEP)
    delta = -ADAM_LR * (m_hat / (jnp.sqrt(v_hat) + ADAM_EPS) + ADAM_WD * w)
    return delta, m, v


def _adamw_reduce(parts, w, m, v, tr, name, parts_t=False):
    nl, r, c = w.shape
    tr = r if (parts_t and r % LANES) else (LANES if parts_t else _row_tile(r, tr))

    def body(*refs):
        p_refs = refs[:nl]
        w_ref, m_ref, v_ref, g_ref, d_ref, mo_ref, vo_ref = refs[nl:]
        layer = pl.program_id(0)

        def partial(i):
            val = p_refs[0][i].astype(F32)
            for q in range(1, nl):
                val = jnp.where(layer == q, p_refs[q][i].astype(F32), val)
            return val

        g = partial(0)
        for i in range(1, N_DEV):
            g = g + partial(i)
        if parts_t:
            g = g.T
        g_ref[...] = g
        d_ref[...], mo_ref[...], vo_ref[...] = _adamw_math(w_ref[...], g, m_ref[...], v_ref[...])

    blk = pl.BlockSpec((None, tr, c), lambda l, i: (l, i, 0))
    out = jax.ShapeDtypeStruct((nl, r, c), F32)
    if parts_t:
        p_specs = [pl.BlockSpec((N_DEV, c, tr), lambda l, i: (0, 0, i))]
    else:
        p_specs = [pl.BlockSpec((N_DEV, tr, c), (lambda l, i, q=q: (0, jnp.where(l == q, i, 0), 0)))
                   for q in range(nl)]
    return pl.pallas_call(
        body, name=name, grid=(nl, r // tr),
        in_specs=p_specs + [blk, blk, blk],
        out_specs=(blk,) * 4, out_shape=(out,) * 4,
        compiler_params=_params(VMEM_BIG),
    )(*parts, w, m, v)


def _adamw_outer(sct, dm, w, m, v, tr, name):
    nl, d, c = w.shape

    def body(s_ref, dm_ref, w_ref, m_ref, v_ref, g_ref, d_ref, mo_ref, vo_ref):
        g = jnp.dot(s_ref[...], dm_ref[...], preferred_element_type=F32)
        g_ref[...] = g
        d_ref[...], mo_ref[...], vo_ref[...] = _adamw_math(w_ref[...], g, m_ref[...], v_ref[...])

    blk = pl.BlockSpec((None, tr, c), lambda l, i: (l, i, 0))
    out = jax.ShapeDtypeStruct((nl, d, c), F32)
    return pl.pallas_call(
        body, name=name, grid=(nl, d // tr),
        in_specs=[pl.BlockSpec((tr, N_DEV), lambda l, i: (i, 0)),
                  pl.BlockSpec((None, N_DEV, c), lambda l, i: (l, 0, 0)), blk, blk, blk],
        out_specs=(blk,) * 4, out_shape=(out,) * 4,
        compiler_params=_params(VMEM_BIG),
    )(sct, dm, w, m, v)


def _pad_rows(a, rows):
    return jnp.concatenate([a, jnp.zeros((rows - a.shape[0],) + a.shape[1:], a.dtype)], axis=0)


def kernel(x, c, mod_w, mod_b, norm_g, ffn_w_in, ffn_w_out, conv_w_in, conv_k, conv_w_out, kv_mod_w, kv_mod_b, kv_norm_g, w_kv, attn_w_q, attn_w_o, rel_bias, loss_target, m_mod_w, m_mod_b, m_norm_g, m_ffn_w_in, m_ffn_w_out, m_conv_w_in, m_conv_k, m_conv_w_out, m_kv_mod_w, m_kv_mod_b, m_kv_norm_g, m_w_kv, m_attn_w_q, m_attn_w_o, m_rel_bias, v_mod_w, v_mod_b, v_norm_g, v_ffn_w_in, v_ffn_w_out, v_conv_w_in, v_conv_k, v_conv_w_out, v_kv_mod_w, v_kv_mod_b, v_kv_norm_g, v_w_kv, v_attn_w_q, v_attn_w_o, v_rel_bias):
    s, d = x.shape[1], x.shape[2]
    dq = d // LANES
    dsh = d // N_DEV
    nl = mod_w.shape[0]
    mw = mod_w.shape[2]
    kmw = kv_mod_w.shape[1]
    nh, nrel = rel_bias.shape[1], rel_bias.shape[2]
    tm = min(256, s)
    tm2 = min(512, s)
    tk = min(2048, s)
    tu = 512
    me = 4 * lax.axis_index("x") + 2 * lax.axis_index("y") + lax.axis_index("c")

    x0 = x[0]
    tgt = loss_target[0]

    small1 = jnp.concatenate([c.reshape(dq, LANES), norm_g.reshape(dq, LANES),
                              _pad_rows(conv_k[0], 8).reshape(dq, LANES)], axis=0)
    (sm,) = _exchange([small1], ["gather"], "gather_small")
    c_all = sm[:, 0:dq].reshape(N_DEV, d)
    ng_full = jnp.transpose(sm[:, dq:2 * dq].reshape(N_DEV, 8, dsh), (1, 0, 2)).reshape(8, d)
    ck_full = jnp.transpose(sm[:, 2 * dq:3 * dq].reshape(N_DEV, 8, dsh), (1, 0, 2)).reshape(8, d)

    modcols, silu_c = _mod_fwd(c_all, mod_w, kv_mod_w)
    (modall,) = _exchange([modcols], ["gather"], "gather_mod")

    cast = lambda *ws: [a.astype(BF16) for a in ws]
    gath = lambda ws: (ws, ["gather"] * len(ws))
    half = lambda ws: (ws, ["gather_half"] * len(ws))
    (h_conv, h_ffn0, h_attn, h_ffn1), token = _xstart(
        [half(cast(conv_w_in[0].T, conv_w_out[0])), half(cast(jnp.swapaxes(ffn_w_in[0], 0, 1), ffn_w_out[0])),
         gath(cast(w_kv.T, attn_w_q[0], attn_w_o[0])), gath(cast(jnp.swapaxes(ffn_w_in[1], 0, 1), ffn_w_out[1]))],
        modall, "gather_start")
    modall = modall + token[0, 0]
    mine = lax.dynamic_index_in_dim(modall, me, axis=1, keepdims=False)
    modrow = jnp.stack([mine[:, l * mw:(l + 1) * mw].reshape(6, d) for l in range(nl)])
    kvrow = mine[:, nl * mw:nl * mw + kmw].reshape(2, d)
    tab, modval = _vec_prep(modrow, mod_b.reshape(nl, 6, d), kvrow, kv_mod_b.reshape(2, d), ng_full,
                            kv_norm_g.reshape(1, d))
    bias = _bias_fwd(rel_bias[0])

    wci, wco = _xwait(h_conv, [bias], "gather_wait_conv")
    wci, wco = _forward_to_sibling([wci, wco], "gather_forward_conv")
    wci, wco = wci.reshape(3 * d, d), wco.reshape(d, d)
    x1, h1a, bcx, ua, ya = _conv_fwd(x0, tab, ck_full, wci, wco, tm2)
    wfi0, wfo0 = _xwait(h_ffn0, [x1], "gather_wait_ffn0")
    wfi0, wfo0 = _forward_to_sibling([wfi0, wfo0], "gather_forward_ffn0")
    wfi0, wfo0 = wfi0.reshape(-1, d), wfo0.reshape(-1, d)
    x2, h2a, gua, aa, y2a = _ffn_fwd(x1, tab, 0, wfi0, wfo0, None, tm2, "ffn_fwd0")
    wkv, wq, wo = _xwait(h_attn, [x2], "gather_wait_attn")
    wkv, wq, wo = wkv.reshape(2 * d, d), wq.reshape(d, d), wo.reshape(d, d)
    hkv, h1b, q, k, v = _qkv_fwd(x2, tab, wq, wkv, tm2)
    o, lse = _attn_fwd(q, k, v, bias)
    x3, yb = _attn_out_fwd(o, x2, tab, wo, tm2)
    wfi1, wfo1 = _xwait(h_ffn1, [x3], "gather_wait_ffn1")
    wfi1, wfo1 = wfi1.reshape(-1, d), wfo1.reshape(-1, d)
    dx4, h2b, gub, ab, y2b, loss_acc = _ffn_fwd(x3, tab, 6, wfi1, wfo1, tgt, tm2, "ffn_fwd1")

    scat = lambda ws: [(ws, ["scatter"] * len(ws))]
    dx3, dy2b, dgub, sums_f1 = _ffn_bwd(dx4, x3, y2b, gub, tab, 6, wfi1, wfo1, tm, "ffn_bwd1")
    g_wfi1 = _wgrad_rows(dgub, h2b, 4, tk, "wgrad_ffn_in1").reshape(N_DEV, -1, d)
    g_wfo1 = _wgrad_rows(ab, dy2b, 2, tk, "wgrad_ffn_out1").reshape(N_DEV, -1, d)
    (h_g1,), token = _xstart(scat([g_wfi1, g_wfo1]), dx3, "grads_start_ffn1")
    tab = tab + token[0, 0]
    dyb, do, sums_o = _attn_out_bwd(dx3, yb, tab, wo, tm2)
    g_wo = _wgrad_wide(o, dyb, 1, tk, "wgrad_o").reshape(N_DEV, dsh, d)
    dqb, dk, dv, dbias = _attn_bwd(q, k, v, o, do, lse, bias)
    g_wq = _wgrad_wide(h1b, dqb, 1, tk, "wgrad_q").reshape(N_DEV, dsh, d)
    dx2, dkvb, sums_q = _qkv_bwd(dx3, dqb, dk, dv, x2, tab, wq, wkv, tm2)
    g_wkv = _wgrad_rows(dkvb, hkv, 2, tk, "wgrad_kv").reshape(N_DEV, -1, d)
    (h_g2,), token = _xstart(scat([g_wkv, g_wq, g_wo]), dx2, "grads_start_attn")
    tab = tab + token[0, 0]
    dx1, dy2a, dgua, sums_f0 = _ffn_bwd(dx2, x1, y2a, gua, tab, 0, wfi0, wfo0, tm, "ffn_bwd0")
    g_wfi0 = _wgrad_rows(dgua, h2a, 4, tk, "wgrad_ffn_in0").reshape(N_DEV, -1, d)
    g_wfo0 = _wgrad_rows(aa, dy2a, 2, tk, "wgrad_ffn_out0").reshape(N_DEV, -1, d)
    (h_g3,), token = _xstart(scat([g_wfi0, g_wfo0]), dx1, "grads_start_ffn0")
    tab = tab + token[0, 0]
    dx0, dya, dbcx, sums_c, dck = _conv_bwd(dx1, x0, ya, bcx, tab, ck_full, wci, wco, tm2)
    drel = _bias_bwd(dbias, nrel)
    dmod, dng, dkvg = _vec_bwd(sums_c, sums_f0, sums_q, sums_o, sums_f1, modval, ng_full, kv_norm_g.reshape(1, d))

    relw = -(-nrel // LANES) * LANES
    drel_p = jnp.concatenate([drel, jnp.zeros((nh, relw - nrel), F32)], axis=1)
    small3 = jnp.concatenate([dmod.reshape(16 * dq, LANES), dng.reshape(8 * dq, LANES), dkvg.reshape(8 * dq, LANES),
                              dck.reshape(8 * dq, LANES), loss_acc,
                              drel_p.reshape(nh * relw // LANES, LANES)], axis=0)
    (sm,) = _exchange([small3], ["gather"], "gather_small_grads")
    g_wci = _wgrad_rows(dbcx, h1a, 3, tk, "wgrad_conv_in").reshape(N_DEV, -1, d)
    g_wco = _wgrad_wide(ua, dya, 1, tk, "wgrad_conv_out").reshape(N_DEV, dsh, d)
    (h_g4,), token = _xstart(scat([g_wci, g_wco]), sm, "grads_start_conv")
    sm = sm + token[0, 0]
    o1, o2, o3, o4, o5 = 16 * dq, 24 * dq, 32 * dq, 40 * dq, 40 * dq + 8
    loss = jnp.sum(sm[:, o4:o5, :]) * (0.5 / d)
    dmod_all = sm[:, 0:o1].reshape(N_DEV, 16, d)
    mine_cols = lambda a: lax.dynamic_slice_in_dim(a, me * dsh, dsh, axis=2)
    dng_parts = mine_cols(sm[:, o1:o2].reshape(N_DEV, 8, d))
    dkvg_parts = sm[:, o2:o3].reshape(N_DEV, 8, d)[:, 0:1]
    dck_parts = mine_cols(sm[:, o3:o4].reshape(N_DEV, 8, d))[:, 0:3]
    drel_parts = sm[:, o5:].reshape(N_DEV, nh, relw)[:, :, 0:nrel]

    def update(parts, w, m, v, name, layers=1, parts_t=False):
        shp = w.shape
        w3, m3, v3 = (a.reshape(layers, -1, shp[-1]) for a in (w, m, v))
        if not parts_t:
            parts = [p.reshape(N_DEV, -1, shp[-1]) for p in parts]
        outs = _adamw_reduce(parts, w3, m3, v3, tu, name, parts_t)
        return [a.reshape(shp) for a in outs]

    p_wfi1, p_wfo1 = _xwait(h_g1, [sm], "grads_wait_ffn1")
    p_wfi0, p_wfo0 = _xwait(h_g3, [p_wfi1], "grads_wait_ffn0")
    tr = lambda a: jnp.swapaxes(a, 1, 2)
    u_ffn_in = [tr(a) for a in update([p_wfi0, p_wfi1], tr(ffn_w_in), tr(m_ffn_w_in), tr(v_ffn_w_in),
                                      "adamw_ffn_in", 2)]
    u_ffn_out = update([p_wfo0, p_wfo1], ffn_w_out, m_ffn_w_out, v_ffn_w_out, "adamw_ffn_out", 2)
    p_wkv, p_wq, p_wo = _xwait(h_g2, [u_ffn_out[0]], "grads_wait_attn")
    u_w_kv = update([p_wkv], w_kv, m_w_kv, v_w_kv, "adamw_w_kv", parts_t=True)
    u_w_q = update([p_wq], attn_w_q, m_attn_w_q, v_attn_w_q, "adamw_w_q")
    u_w_o = update([p_wo], attn_w_o, m_attn_w_o, v_attn_w_o, "adamw_w_o")

    sct = jnp.transpose(silu_c)
    dm_mod = jnp.stack([lax.dynamic_slice_in_dim(dmod_all[:, 6 * l:6 * l + 6].reshape(N_DEV, 6 * d), me * mw, mw, axis=1)
                        for l in range(nl)]).astype(BF16)
    dm_kv = lax.dynamic_slice_in_dim(dmod_all[:, R_KV:R_KV + 2].reshape(N_DEV, 2 * d), me * kmw, kmw, axis=1)
    u_mod_w = _adamw_outer(sct, dm_mod, mod_w, m_mod_w, v_mod_w, min(tu, d), "adamw_mod_w")
    u_kv_mod_w = [a[0] for a in _adamw_outer(sct, dm_kv.astype(BF16)[None], kv_mod_w[None], m_kv_mod_w[None],
                                             v_kv_mod_w[None], min(tu, d), "adamw_kv_mod_w")]

    modb_parts = jnp.stack([dmod_all[:, 6 * l:6 * l + 6].reshape(N_DEV, 6 * d) for l in range(nl)], axis=1)
    u_mod_b = update([modb_parts], mod_b, m_mod_b, v_mod_b, "adamw_mod_b")
    u_norm_g = update([dng_parts], norm_g.reshape(8, dsh), m_norm_g.reshape(8, dsh), v_norm_g.reshape(8, dsh), "adamw_norm_g")
    u_norm_g = [a.reshape(norm_g.shape) for a in u_norm_g]
    u_conv_k = update([dck_parts], conv_k, m_conv_k, v_conv_k, "adamw_conv_k")
    kvb_parts = dmod_all[:, R_KV:R_KV + 2].reshape(N_DEV, 1, 2 * d)
    u_kv_mod_b = [a.reshape(kv_mod_b.shape) for a in update([kvb_parts], kv_mod_b.reshape(1, -1), m_kv_mod_b.reshape(1, -1),
                                                            v_kv_mod_b.reshape(1, -1), "adamw_kv_mod_b")]
    u_kv_norm_g = [a.reshape(kv_norm_g.shape) for a in update([dkvg_parts], kv_norm_g.reshape(1, -1), m_kv_norm_g.reshape(1, -1),
                                                              v_kv_norm_g.reshape(1, -1), "adamw_kv_norm_g")]
    u_rel = update([drel_parts], rel_bias, m_rel_bias, v_rel_bias, "adamw_rel_bias")

    others = [u_ffn_in, u_ffn_out, u_w_kv, u_w_q, u_w_o, u_mod_w, u_kv_mod_w, u_mod_b, u_norm_g, u_conv_k, u_kv_mod_b,
              u_kv_norm_g, u_rel]
    p_wci, p_wco = _xwait(h_g4, [u[3] for u in others], "grads_wait_conv")
    u_conv_in = update([p_wci], conv_w_in, m_conv_w_in, v_conv_w_in, "adamw_conv_in", parts_t=True)
    u_conv_out = update([p_wco], conv_w_out, m_conv_w_out, v_conv_w_out, "adamw_conv_out")

    ups = [u_mod_w, u_mod_b, u_norm_g, u_ffn_in, u_ffn_out, u_conv_in, u_conv_k, u_conv_out, u_kv_mod_w, u_kv_mod_b,
           u_kv_norm_g, u_w_kv, u_w_q, u_w_o, u_rel]
    return (loss, dx0[None], *[u[0] for u in ups], *[u[1] for u in ups], *[u[2] for u in ups], *[u[3] for u in ups])
```

```python
import jax
import jax.numpy as jnp
from jax import lax
from jax.experimental import pallas as pl
from jax.experimental.pallas import tpu as pltpu

F32 = jnp.float32
BF16 = jnp.bfloat16

EPS = 1e-6
CHUNK = 64
HEAD_DIM = 64
N_LEFT = 8
LANES = 128
MXU = 256
FFN_CHUNK = 4
QB = 4 * CHUNK
KW = QB + N_LEFT * CHUNK
BIAS_VARIANTS = N_LEFT * CHUNK // QB + 1
ATTN_PER = 8
NEG = -1e30
N_DEV = 8

ADAM_LR = 0.001
ADAM_B1 = 0.9
ADAM_B2 = 0.999
ADAM_EPS = 1e-08
ADAM_WD = 0.01
ADAM_STEP = 10

VMEM_BIG = 56 * 1024 * 1024

NT = (((1,), (1,)), ((), ()))
TN = (((0,), (0,)), ((), ()))

R_W1, R_SH1, R_P1, R_W2, R_SH2, R_P2 = range(6)
R_KV = 12


def _params(vmem):
    return pltpu.CompilerParams(vmem_limit_bytes=vmem)


def _row_tile(rows, cap):
    for t in range(min(cap, rows) // 16 * 16, 0, -16):
        if rows % t == 0:
            return t
    return rows


def _rows(tm, cols):
    return pl.BlockSpec((tm, cols), lambda i: (i, 0))


def _const(shape):
    nd = len(shape)
    return pl.BlockSpec(shape, lambda *_: (0,) * nd, pipeline_mode=pl.Buffered(1))


def _rs(x):
    return lax.rsqrt(jnp.mean(x * x, axis=-1, keepdims=True) + EPS)


def _norm_bwd(d, n, r):
    return r * (d - n * jnp.mean(d * n, axis=-1, keepdims=True))


def _colsum(a):
    return jnp.sum(a, axis=0, keepdims=True)


def _sigmoid(g):
    return 1.0 / (1.0 + jnp.exp(-g))


def _exchange(arrays, modes, name):
    n = len(arrays)
    out_shape = []
    for a, mode in zip(arrays, modes):
        shp = (N_DEV,) + a.shape if mode == "gather" else a.shape
        out_shape.append(jax.ShapeDtypeStruct(shp, a.dtype))

    def body(*refs):
        ins, outs = refs[:n], refs[n:2 * n]
        send_sems, recv_sems, local_sems = refs[2 * n:]
        x, y, c = lax.axis_index("x"), lax.axis_index("y"), lax.axis_index("c")
        me = 4 * x + 2 * y + c
        local, sends, recvs = [], [], []
        for a in range(n):
            own = ins[a] if modes[a] == "gather" else ins[a].at[me]
            cp = pltpu.make_async_copy(own, outs[a].at[me], local_sems.at[a])
            cp.start()
            local.append(cp)
        for k in range(1, N_DEV):
            px = 1 - x if k & 4 else x
            py = 1 - y if k & 2 else y
            pc = 1 - c if k & 1 else c
            peer = 4 * px + 2 * py + pc
            for a in range(n):
                src = ins[a] if modes[a] == "gather" else ins[a].at[peer]
                sem = a * (N_DEV - 1) + k - 1
                cp = pltpu.make_async_remote_copy(
                    src_ref=src, dst_ref=outs[a].at[me],
                    send_sem=send_sems.at[sem], recv_sem=recv_sems.at[sem],
                    device_id=(px, py, pc), device_id_type=pl.DeviceIdType.MESH)
                cp.start()
                sends.append(cp)
                recvs.append(pltpu.make_async_remote_copy(
                    src_ref=src, dst_ref=outs[a].at[peer],
                    send_sem=send_sems.at[sem], recv_sem=recv_sems.at[sem],
                    device_id=(px, py, pc), device_id_type=pl.DeviceIdType.MESH))
        for cp in recvs:
            cp.wait_recv()
        for cp in sends:
            cp.wait_send()
        for cp in local:
            cp.wait()

    any_spec = pl.BlockSpec(memory_space=pl.ANY)
    return pl.pallas_call(
        body, name=name,
        out_shape=tuple(out_shape),
        in_specs=[any_spec] * n,
        out_specs=tuple([any_spec] * n),
        scratch_shapes=[
            pltpu.SemaphoreType.DMA((n * (N_DEV - 1),)),
            pltpu.SemaphoreType.DMA((n * (N_DEV - 1),)),
            pltpu.SemaphoreType.DMA((n,)),
        ],
    )(*arrays)


def _peers(x, y, c):
    out = []
    for k in range(1, N_DEV):
        px = 1 - x if k & 4 else x
        py = 1 - y if k & 2 else y
        pc = 1 - c if k & 1 else c
        out.append((k - 1, (px, py, pc), 4 * px + 2 * py + pc))
    return out


_OTHER_CORE_SLOTS = (2, 4, 6)


def _land_shape(a, mode):
    return a.shape if mode == "scatter" else (N_DEV,) + a.shape


_HBM = pl.BlockSpec(memory_space=pltpu.HBM)
_SEM = pl.BlockSpec(memory_space=pltpu.SEMAPHORE)
_EFFECT = pltpu.SideEffectType.DATAFLOW_SIDE_EFFECTING


def _xstart(groups, after, name):
    flat = [(a, m) for arrays, modes in groups for a, m in zip(arrays, modes)]
    n, ngr = len(flat), len(groups)
    sizes = [len(arrays) for arrays, _ in groups]
    npeer = N_DEV - 1

    def body(*refs):
        ins, lands = refs[:n], refs[n:2 * n]
        outs = refs[2 * n + 1:]
        sems = outs[:2 * ngr]
        token = outs[2 * ngr + 2 * n]
        local_sems = outs[2 * ngr + 2 * n + 1]
        stage = outs[2 * ngr + 2 * n + 2:]
        x, y, c = lax.axis_index("x"), lax.axis_index("y"), lax.axis_index("c")
        me = 4 * x + 2 * y + c
        loads, stores = [], []
        for a in range(n):
            own = ins[a].at[me] if flat[a][1] == "scatter" else ins[a]
            loads.append(pltpu.make_async_copy(own, stage[a], local_sems.at[a]))
            stores.append(pltpu.make_async_copy(stage[a], lands[a].at[me], local_sems.at[a]))
            loads[a].start()
        for a in range(n):
            loads[a].wait()
            stores[a].start()
        a = 0
        for g in range(ngr):
            for j in range(sizes[g]):
                mode = flat[a][1]
                for slot, peer, pidx in _peers(x, y, c):
                    if mode == "gather_half" and slot in _OTHER_CORE_SLOTS:
                        continue
                    pltpu.make_async_remote_copy(
                        src_ref=ins[a].at[pidx] if mode == "scatter" else ins[a], dst_ref=lands[a].at[me],
                        send_sem=sems[2 * g].at[j * npeer + slot], recv_sem=sems[2 * g + 1].at[j * npeer + slot],
                        device_id=peer, device_id_type=pl.DeviceIdType.MESH).start()
                a += 1
        for cp in stores:
            cp.wait()
        token[...] = jnp.zeros_like(token)

    out_shape, out_specs = [], []
    for sz in sizes:
        out_shape += [pltpu.SemaphoreType.DMA((sz * npeer,)), pltpu.SemaphoreType.DMA((sz * npeer,))]
        out_specs += [_SEM, _SEM]
    out_shape += [pltpu.HBM(a.shape, a.dtype) for a, _ in flat]
    out_shape += [pltpu.HBM(_land_shape(a, m), a.dtype) for a, m in flat]
    out_specs += [_HBM] * (2 * n)
    out_shape.append(jax.ShapeDtypeStruct((8, LANES), F32))
    out_specs.append(pl.BlockSpec(memory_space=pltpu.VMEM))
    args = [pltpu.with_memory_space_constraint(a, pltpu.HBM) for a, _ in flat]
    args += [pltpu.with_memory_space_constraint(lax.empty(_land_shape(a, m), a.dtype), pltpu.HBM) for a, m in flat]
    res = pl.pallas_call(
        body, name=name, out_shape=tuple(out_shape),
        in_specs=[_HBM] * (2 * n) + [pl.BlockSpec(memory_space=pl.ANY)], out_specs=tuple(out_specs),
        input_output_aliases={i: 2 * ngr + i for i in range(2 * n)},
        scratch_shapes=[pltpu.SemaphoreType.DMA((n,))]
                       + [pltpu.VMEM(a.shape[1:] if m == "scatter" else a.shape, a.dtype) for a, m in flat],
        compiler_params=pltpu.CompilerParams(has_side_effects=_EFFECT, vmem_limit_bytes=VMEM_BIG),
    )(*args, after)
    handles, a = [], 0
    for g, sz in enumerate(sizes):
        handles.append((res[2 * g], res[2 * g + 1], list(res[2 * ngr + a:2 * ngr + a + sz]),
                        list(res[2 * ngr + n + a:2 * ngr + n + a + sz]), list(groups[g][1])))
        a += sz
    return handles, res[-1]


def _xwait(handle, after, name):
    send_sems, recv_sems, srcs, lands, modes = handle
    m = len(srcs)
    npeer = N_DEV - 1
    after = list(after)

    def body(*refs):
        ins, lnd = refs[:m], refs[m:2 * m]
        ssem, rsem = refs[2 * m], refs[2 * m + 1]
        x, y, c = lax.axis_index("x"), lax.axis_index("y"), lax.axis_index("c")
        for j in range(m):
            for slot, peer, pidx in _peers(x, y, c):
                if modes[j] == "gather_half" and slot in _OTHER_CORE_SLOTS:
                    continue
                cp = pltpu.make_async_remote_copy(
                    src_ref=ins[j].at[pidx] if modes[j] == "scatter" else ins[j], dst_ref=lnd[j].at[pidx],
                    send_sem=ssem.at[j * npeer + slot], recv_sem=rsem.at[j * npeer + slot],
                    device_id=peer, device_id_type=pl.DeviceIdType.MESH)
                cp.wait_send()
                cp.wait_recv()

    res = pl.pallas_call(
        body, name=name,
        out_shape=tuple([pltpu.HBM(a.shape, a.dtype) for a in srcs] + [pltpu.HBM(a.shape, a.dtype) for a in lands]),
        in_specs=[_HBM] * (2 * m) + [_SEM, _SEM] + [pl.BlockSpec(memory_space=pl.ANY)] * len(after),
        out_specs=tuple([_HBM] * (2 * m)),
        input_output_aliases={i: i for i in range(2 * m)},
        compiler_params=pltpu.CompilerParams(has_side_effects=_EFFECT),
    )(*srcs, *lands, send_sems, recv_sems, *after)
    return list(res[m:])


def _forward_to_sibling(lands, name):
    n = len(lands)

    def body(*refs):
        outs = refs[n:2 * n]
        send_sems, recv_sems = refs[2 * n:]
        x, y, c = lax.axis_index("x"), lax.axis_index("y"), lax.axis_index("c")
        sibling = (x, y, 1 - c)
        sends, recvs = [], []
        for a in range(n):
            for j, k in enumerate((2, 4, 6)):
                px = 1 - x if k & 4 else x
                py = 1 - y if k & 2 else y
                got = 4 * px + 2 * py + c
                missing = 4 * px + 2 * py + (1 - c)
                sem = a * 3 + j
                cp = pltpu.make_async_remote_copy(
                    src_ref=outs[a].at[got], dst_ref=outs[a].at[got],
                    send_sem=send_sems.at[sem], recv_sem=recv_sems.at[sem],
                    device_id=sibling, device_id_type=pl.DeviceIdType.MESH)
                cp.start()
                sends.append(cp)
                recvs.append(pltpu.make_async_remote_copy(
                    src_ref=outs[a].at[missing], dst_ref=outs[a].at[missing],
                    send_sem=send_sems.at[sem], recv_sem=recv_sems.at[sem],
                    device_id=sibling, device_id_type=pl.DeviceIdType.MESH))
        for cp in recvs:
            cp.wait_recv()
        for cp in sends:
            cp.wait_send()

    any_spec = pl.BlockSpec(memory_space=pl.ANY)
    return list(pl.pallas_call(
        body, name=name,
        out_shape=tuple(jax.ShapeDtypeStruct(a.shape, a.dtype) for a in lands),
        in_specs=[any_spec] * n, out_specs=tuple([any_spec] * n),
        input_output_aliases={i: i for i in range(n)},
        scratch_shapes=[pltpu.SemaphoreType.DMA((3 * n,)), pltpu.SemaphoreType.DMA((3 * n,))],
    )(*lands))


def _mod_fwd(c_all, mod_w, kv_mod_w):
    nl, d, mw = mod_w.shape
    kw = kv_mod_w.shape[1]

    def body(c_ref, mw_ref, kw_ref, o_ref, sc_ref):
        cc = c_ref[...]
        sc = (cc * _sigmoid(cc)).astype(BF16)
        sc_ref[...] = sc
        for l in range(nl):
            o_ref[:, l * mw:(l + 1) * mw] = jnp.dot(sc, mw_ref[l].astype(BF16), preferred_element_type=F32)
        o_ref[:, nl * mw:nl * mw + kw] = jnp.dot(sc, kw_ref[...].astype(BF16), preferred_element_type=F32)

    return pl.pallas_call(
        body, name="mod_fwd",
        out_shape=(jax.ShapeDtypeStruct((c_all.shape[0], nl * mw + kw), F32),
                   jax.ShapeDtypeStruct(c_all.shape, BF16)),
        compiler_params=_params(VMEM_BIG),
    )(c_all, mod_w, kv_mod_w)


def _vec_prep(modrow, modb, kvrow, kvb, ng, kvg):
    d = ng.shape[1]

    def body(mr_ref, mb_ref, kr_ref, kb_ref, ng_ref, kvg_ref, t_ref, m_ref):
        t_ref[...] = jnp.zeros_like(t_ref)
        m_ref[...] = jnp.zeros_like(m_ref)
        for l in range(2):
            mod = mr_ref[l] + mb_ref[l]
            m_ref[6 * l:6 * l + 6, :] = mod
            g = ng_ref[4 * l:4 * l + 4, :]
            t_ref[6 * l + R_W1:6 * l + R_W1 + 1, :] = g[0:1] * (1.0 + mod[1:2])
            t_ref[6 * l + R_SH1:6 * l + R_SH1 + 1, :] = mod[0:1]
            t_ref[6 * l + R_P1:6 * l + R_P1 + 1, :] = mod[2:3] * g[1:2]
            t_ref[6 * l + R_W2:6 * l + R_W2 + 1, :] = g[2:3] * (1.0 + mod[4:5])
            t_ref[6 * l + R_SH2:6 * l + R_SH2 + 1, :] = mod[3:4]
            t_ref[6 * l + R_P2:6 * l + R_P2 + 1, :] = mod[5:6] * g[3:4]
        kv = kr_ref[...] + kb_ref[...]
        m_ref[R_KV:R_KV + 2, :] = kv
        t_ref[R_KV:R_KV + 1, :] = kvg_ref[...] * (1.0 + kv[1:2])
        t_ref[R_KV + 1:R_KV + 2, :] = kv[0:1]

    return pl.pallas_call(
        body, name="vec_prep",
        out_shape=(jax.ShapeDtypeStruct((16, d), F32), jax.ShapeDtypeStruct((16, d), F32)),
    )(modrow, modb, kvrow, kvb, ng, kvg)


def _vec_bwd(sums_c, sums_f0, sums_q, sums_o, sums_f1, mt, ng, kvg):
    d = ng.shape[1]

    def body(sc_ref, sf0_ref, sq_ref, so_ref, sf1_ref, m_ref, ng_ref, kvg_ref, dm_ref, dng_ref, dkvg_ref, g_ref):
        g_ref[...] = jnp.zeros_like(g_ref)
        g_ref[0:3, :] = sc_ref[0:3, :]
        g_ref[3:6, :] = sf0_ref[3:6, :]
        g_ref[6:8, :] = sq_ref[0:2, :]
        g_ref[8:9, :] = so_ref[2:3, :]
        g_ref[9:12, :] = sf1_ref[3:6, :]
        g_ref[R_KV:R_KV + 2, :] = sq_ref[2:4, :]
        dm_ref[...] = jnp.zeros_like(dm_ref)
        dkvg_ref[...] = jnp.zeros_like(dkvg_ref)
        for l in range(2):
            g = ng_ref[4 * l:4 * l + 4, :]
            mod = m_ref[6 * l:6 * l + 6, :]
            s = g_ref[6 * l:6 * l + 6, :]
            dm_ref[6 * l + 0:6 * l + 1, :] = s[1:2]
            dm_ref[6 * l + 1:6 * l + 2, :] = s[0:1] * g[0:1]
            dm_ref[6 * l + 2:6 * l + 3, :] = s[2:3] * g[1:2]
            dm_ref[6 * l + 3:6 * l + 4, :] = s[4:5]
            dm_ref[6 * l + 4:6 * l + 5, :] = s[3:4] * g[2:3]
            dm_ref[6 * l + 5:6 * l + 6, :] = s[5:6] * g[3:4]
            dng_ref[4 * l + 0:4 * l + 1, :] = s[0:1] * (1.0 + mod[1:2])
            dng_ref[4 * l + 1:4 * l + 2, :] = s[2:3] * mod[2:3]
            dng_ref[4 * l + 2:4 * l + 3, :] = s[3:4] * (1.0 + mod[4:5])
            dng_ref[4 * l + 3:4 * l + 4, :] = s[5:6] * mod[5:6]
        dm_ref[R_KV:R_KV + 1, :] = g_ref[R_KV + 1:R_KV + 2, :]
        dm_ref[R_KV + 1:R_KV + 2, :] = g_ref[R_KV:R_KV + 1, :] * kvg_ref[...]
        dkvg_ref[0:1, :] = g_ref[R_KV:R_KV + 1, :] * (1.0 + m_ref[R_KV + 1:R_KV + 2, :])

    return pl.pallas_call(
        body, name="vec_bwd",
        out_shape=(jax.ShapeDtypeStruct((16, d), F32), jax.ShapeDtypeStruct((8, d), F32),
                   jax.ShapeDtypeStruct((8, d), F32)),
        scratch_shapes=[pltpu.VMEM((16, d), F32)],
    )(sums_c, sums_f0, sums_q, sums_o, sums_f1, mt, ng, kvg)


def _rel_index(nrel):
    width = KW + QB
    e = lax.broadcasted_iota(jnp.int32, (nrel, width), 1)
    r = lax.broadcasted_iota(jnp.int32, (nrel, width), 0)
    max_rel = (nrel - 1) // 2
    idx = jnp.clip(KW - e, -max_rel, max_rel) + max_rel
    return (idx == r).astype(F32)


def _band_valid():
    row = lax.broadcasted_iota(jnp.int32, (QB, KW), 0) // CHUNK
    col = lax.broadcasted_iota(jnp.int32, (QB, KW), 1) // CHUNK
    j = col - row
    return (j >= 0) & (j <= N_LEFT)


def _bias_fwd(rel_bias):
    nh, nrel = rel_bias.shape
    width = KW + QB

    def body(rb_ref, o_ref):
        onehot = _rel_index(nrel)
        gr = jnp.dot(rb_ref[...], onehot, preferred_element_type=F32, precision=lax.Precision.HIGHEST)
        valid = _band_valid() & _key_valid(pl.program_id(0))
        for h in range(nh):
            xrow = jnp.broadcast_to(gr[h:h + 1, :], (QB, width))
            rolled = pltpu.roll(xrow, 0, 1, stride=1, stride_axis=0)
            o_ref[h] = jnp.where(valid, rolled[:, QB:], NEG)

    return pl.pallas_call(
        body, name="bias_fwd", grid=(BIAS_VARIANTS,),
        in_specs=[pl.BlockSpec(rel_bias.shape, lambda v: (0, 0))],
        out_specs=pl.BlockSpec((None, nh, QB, KW), lambda v: (v, 0, 0, 0)),
        out_shape=jax.ShapeDtypeStruct((BIAS_VARIANTS, nh, QB, KW), F32),
        compiler_params=_params(VMEM_BIG),
    )(rel_bias)


def _bias_bwd(dbias, nrel):
    nh = dbias.shape[0]
    width = KW + QB

    def body(db_ref, o_ref, diag_ref):
        onehot = _rel_index(nrel)
        valid = _band_valid()
        rr = lax.broadcasted_iota(jnp.int32, (QB, QB), 0)
        cc = lax.broadcasted_iota(jnp.int32, (QB, QB), 1)
        flip = (rr + cc == QB - 1).astype(F32)
        for h in range(nh):
            rev = jnp.dot(flip, jnp.where(valid, db_ref[h], 0.0), preferred_element_type=F32,
                          precision=lax.Precision.HIGHEST)
            w = jnp.concatenate([jnp.zeros((QB, QB), F32), rev], axis=1)
            back = pltpu.roll(w, width - (QB - 1), 1, stride=1, stride_axis=0)
            diag_ref[h:h + 1, :] = _colsum(back)
        o_ref[...] = lax.dot_general(diag_ref[...], onehot, NT, preferred_element_type=F32,
                                     precision=lax.Precision.HIGHEST)

    return pl.pallas_call(
        body, name="bias_bwd",
        out_shape=jax.ShapeDtypeStruct((nh, nrel), F32),
        scratch_shapes=[pltpu.VMEM((nh, width), F32)],
        compiler_params=_params(VMEM_BIG),
    )(dbias)


def _conv_fwd(x, tab, ck, wci, wco, tm):
    s, d = x.shape

    def body(x_ref, t_ref, ck_ref, wci_ref, wco_ref, x1_ref, h_ref, bcx_ref, u_ref, y_ref, carry):
        @pl.when(pl.program_id(0) == 0)
        def _():
            carry[...] = jnp.zeros_like(carry)

        xv = x_ref[...]
        hb = ((xv * _rs(xv)) * t_ref[R_W1:R_W1 + 1, :] + t_ref[R_SH1:R_SH1 + 1, :]).astype(BF16)
        h_ref[...] = hb
        for j in range(3 * d // MXU):
            bcx_ref[:, j * MXU:(j + 1) * MXU] = lax.dot_general(hb, wci_ref[j * MXU:(j + 1) * MXU, :], NT,
                                                                preferred_element_type=F32)
        bg, cg, xi = bcx_ref[:, 0:d], bcx_ref[:, d:2 * d], bcx_ref[:, 2 * d:3 * d]
        z = cg * xi
        row = lax.broadcasted_iota(jnp.int32, z.shape, 0)
        c1, c2 = carry[7:8, :], carry[6:7, :]
        z1 = jnp.where(row == 0, c1, pltpu.roll(z, 1, 0))
        z2 = jnp.where(row == 0, c2, jnp.where(row == 1, c1, pltpu.roll(z, 2, 0)))
        carry[...] = z[tm - 8:tm, :]
        conv = ck_ref[0:1, :] * z2 + ck_ref[1:2, :] * z1 + ck_ref[2:3, :] * z
        ub = (bg * conv).astype(BF16)
        u_ref[...] = ub
        yv = jnp.dot(ub, wco_ref[...], preferred_element_type=F32)
        y_ref[...] = yv
        x1_ref[...] = xv + (yv * _rs(yv)) * t_ref[R_P1:R_P1 + 1, :]

    return pl.pallas_call(
        body, name="conv_fwd", grid=(s // tm,),
        in_specs=[_rows(tm, d), _const(tab.shape), _const(ck.shape), _const(wci.shape), _const(wco.shape)],
        out_specs=(_rows(tm, d), _rows(tm, d), _rows(tm, 3 * d), _rows(tm, d), _rows(tm, d)),
        out_shape=(jax.ShapeDtypeStruct((s, d), F32), jax.ShapeDtypeStruct((s, d), BF16),
                   jax.ShapeDtypeStruct((s, 3 * d), F32), jax.ShapeDtypeStruct((s, d), BF16),
                   jax.ShapeDtypeStruct((s, d), F32)),
        scratch_shapes=[pltpu.VMEM((8, d), F32)],
        compiler_params=_params(VMEM_BIG),
    )(x, tab, ck, wci, wco)


def _ffn_fwd(x, tab, base, wfi, wfo, tgt, tm, name):
    s, d = x.shape
    hid = wfo.shape[0]
    nblk = hid // MXU
    with_loss = tgt is not None

    def body(*refs):
        if with_loss:
            x_ref, t_ref, wfi_ref, wfo_ref, tgt_ref, xo_ref, h_ref, gu_ref, a_scr, y_ref, loss_ref = refs
        else:
            x_ref, t_ref, wfi_ref, wfo_ref, xo_ref, h_ref, gu_ref, a_scr, y_ref = refs
        xv = x_ref[...]
        hb = ((xv * _rs(xv)) * t_ref[base + R_W2:base + R_W2 + 1, :]
              + t_ref[base + R_SH2:base + R_SH2 + 1, :]).astype(BF16)
        h_ref[...] = hb
        acc = jnp.zeros((tm, d), F32)
        for c0 in range(0, nblk, FFN_CHUNK):
            for j in range(c0, min(c0 + FFN_CHUNK, nblk)):
                lo, hi = j * MXU, (j + 1) * MXU
                g = lax.dot_general(hb, wfi_ref[lo:hi, :], NT, preferred_element_type=F32)
                u = lax.dot_general(hb, wfi_ref[hid + lo:hid + hi, :], NT, preferred_element_type=F32)
                gu_ref[:, lo:hi] = g.astype(BF16)
                gu_ref[:, hid + lo:hid + hi] = u.astype(BF16)
                a_scr[:, lo:hi] = ((g * _sigmoid(g)) * u).astype(BF16)
            lo, hi = c0 * MXU, min(c0 + FFN_CHUNK, nblk) * MXU
            acc = acc + jnp.dot(a_scr[:, lo:hi], wfo_ref[lo:hi, :], preferred_element_type=F32)
        y_ref[...] = acc
        xo = xv + (acc * _rs(acc)) * t_ref[base + R_P2:base + R_P2 + 1, :]
        if with_loss:
            @pl.when(pl.program_id(0) == 0)
            def _():
                loss_ref[...] = jnp.zeros_like(loss_ref)

            err = xo - tgt_ref[...]
            xo_ref[...] = err * (1.0 / d)
            e2 = jnp.sum((err * err).reshape(tm // 8, 8, d), axis=0)
            for q in range(d // LANES):
                loss_ref[...] += e2[:, q * LANES:(q + 1) * LANES]
        else:
            xo_ref[...] = xo

    in_specs = [_rows(tm, d), _const(tab.shape), _const(wfi.shape), _const(wfo.shape)]
    args = [x, tab, wfi, wfo]
    out_specs = [_rows(tm, d), _rows(tm, d), _rows(tm, 2 * hid), _rows(tm, hid), _rows(tm, d)]
    out_shape = [jax.ShapeDtypeStruct((s, d), F32), jax.ShapeDtypeStruct((s, d), BF16),
                 jax.ShapeDtypeStruct((s, 2 * hid), BF16), jax.ShapeDtypeStruct((s, hid), BF16),
                 jax.ShapeDtypeStruct((s, d), F32)]
    if with_loss:
        in_specs.append(_rows(tm, d))
        args.append(tgt)
        out_specs.append(pl.BlockSpec((8, LANES), lambda i: (0, 0)))
        out_shape.append(jax.ShapeDtypeStruct((8, LANES), F32))
    return pl.pallas_call(
        body, name=name, grid=(s // tm,), in_specs=in_specs, out_specs=tuple(out_specs),
        out_shape=tuple(out_shape), compiler_params=_params(VMEM_BIG),
    )(*args)


def _qkv_fwd(x, tab, wq, wkv, tm):
    s, d = x.shape
    base = 6

    def body(x_ref, t_ref, wq_ref, wkv_ref, hkv_ref, h1_ref, q_ref, k_ref, v_ref):
        xv = x_ref[...]
        n = xv * _rs(xv)
        hkv = (n * t_ref[R_KV:R_KV + 1, :] + t_ref[R_KV + 1:R_KV + 2, :]).astype(BF16)
        h1 = (n * t_ref[base + R_W1:base + R_W1 + 1, :] + t_ref[base + R_SH1:base + R_SH1 + 1, :]).astype(BF16)
        hkv_ref[...] = hkv
        h1_ref[...] = h1
        q_ref[...] = (jnp.dot(h1, wq_ref[...], preferred_element_type=F32) * (HEAD_DIM ** -0.5)).astype(BF16)
        for j in range(d // MXU):
            lo, hi = j * MXU, (j + 1) * MXU
            k_ref[:, lo:hi] = lax.dot_general(hkv, wkv_ref[lo:hi, :], NT, preferred_element_type=F32).astype(BF16)
            v_ref[:, lo:hi] = lax.dot_general(hkv, wkv_ref[d + lo:d + hi, :], NT,
                                              preferred_element_type=F32).astype(BF16)

    act = jax.ShapeDtypeStruct((s, d), BF16)
    return pl.pallas_call(
        body, name="qkv_fwd", grid=(s // tm,),
        in_specs=[_rows(tm, d), _const(tab.shape), _const(wq.shape), _const(wkv.shape)],
        out_specs=tuple([_rows(tm, d)] * 5), out_shape=(act,) * 5,
        compiler_params=_params(VMEM_BIG),
    )(x, tab, wq, wkv)


def _window_specs(per=1):
    return [pl.BlockSpec((QB, LANES), (lambda p, b, w=w: (jnp.maximum(per * b - 2 + w, 0), p)))
            for w in range(2 + per)]


def _key_valid(b):
    col = lax.broadcasted_iota(jnp.int32, (QB, KW), 1) // CHUNK
    return (b * (QB // CHUNK) - N_LEFT + col) >= 0


def _bias_spec(per=1, sub=0):
    return pl.BlockSpec((None, LANES // HEAD_DIM, QB, KW),
                        lambda p, b: (jnp.minimum(per * b + sub, BIAS_VARIANTS - 1), p, 0, 0))


def _head_masks():
    lane = lax.broadcasted_iota(jnp.int32, (1, LANES), 1)
    return [(lane // HEAD_DIM == hh) for hh in range(LANES // HEAD_DIM)]


def _attn_fwd(q, k, v, bias):
    s, d = q.shape
    per = ATTN_PER
    npair, nb = d // LANES, s // (per * QB)
    hpp = LANES // HEAD_DIM
    nwin = 2 + per
    nbias = min(per, BIAS_VARIANTS)

    def body(*refs):
        q_ref, k_refs, v_refs = refs[0], refs[1:1 + nwin], refs[1 + nwin:1 + 2 * nwin]
        bias_refs = refs[1 + 2 * nwin:1 + 2 * nwin + nbias]
        o_ref, lse_ref = refs[1 + 2 * nwin + nbias:]
        ks = [r[...] for r in k_refs]
        vs = [r[...] for r in v_refs]
        masks = _head_masks()
        for sub in range(per):
            rows = slice(sub * QB, (sub + 1) * QB)
            qv = q_ref[rows, :]
            kwin = jnp.concatenate(ks[sub:sub + 3], axis=0)
            vwin = jnp.concatenate(vs[sub:sub + 3], axis=0)
            o = jnp.zeros((QB, LANES), F32)
            lse = jnp.zeros((QB, LANES), F32)
            scs = [lax.dot_general(jnp.where(masks[hh], qv, jnp.zeros_like(qv)), kwin, NT,
                                   preferred_element_type=F32) + bias_refs[min(sub, nbias - 1)][hh]
                   for hh in range(hpp)]
            for hh in range(hpp):
                vm = jnp.where(masks[hh], vwin, jnp.zeros_like(vwin))
                sc = scs[hh]
                m = jnp.max(sc, axis=-1, keepdims=True)
                p = jnp.exp(sc - m)
                l = jnp.sum(p, axis=-1, keepdims=True)
                o = o + jnp.dot(p.astype(BF16), vm, preferred_element_type=F32) * (1.0 / l)
                lse = jnp.where(masks[hh], m + jnp.log(l), lse)
            o_ref[rows, :] = o.astype(BF16)
            lse_ref[rows, :] = lse

    blk = pl.BlockSpec((per * QB, LANES), lambda p, b: (b, p))
    return pl.pallas_call(
        body, name="attn_fwd", grid=(npair, nb),
        in_specs=[blk] + _window_specs(per) + _window_specs(per) + [_bias_spec(per, sub) for sub in range(nbias)],
        out_specs=(blk, blk),
        out_shape=(jax.ShapeDtypeStruct((s, d), BF16), jax.ShapeDtypeStruct((s, d), F32)),
        compiler_params=_params(VMEM_BIG),
    )(q, *([k] * nwin), *([v] * nwin), *([bias] * nbias))


def _attn_out_fwd(o, x, tab, wo, tm):
    s, d = x.shape
    base = 6

    def body(o_ref, x_ref, t_ref, wo_ref, x3_ref, y_ref):
        yv = jnp.dot(o_ref[...], wo_ref[...], preferred_element_type=F32)
        y_ref[...] = yv
        x3_ref[...] = x_ref[...] + (yv * _rs(yv)) * t_ref[base + R_P1:base + R_P1 + 1, :]

    return pl.pallas_call(
        body, name="attn_out_fwd", grid=(s // tm,),
        in_specs=[_rows(tm, d), _rows(tm, d), _const(tab.shape), _const(wo.shape)],
        out_specs=(_rows(tm, d), _rows(tm, d)),
        out_shape=(jax.ShapeDtypeStruct((s, d), F32), jax.ShapeDtypeStruct((s, d), F32)),
        compiler_params=_params(VMEM_BIG),
    )(o, x, tab, wo)


def _ffn_bwd(dxo, x, y, gu, tab, base, wfi, wfo, tm, name):
    s, d = x.shape
    hid = wfo.shape[0]
    nblk = hid // MXU

    def body(dxo_ref, x_ref, y_ref, gu_ref, t_ref, wfi_ref, wfo_ref, dx_ref, dyb_ref, dgu_ref, sums_ref):
        @pl.when(pl.program_id(0) == 0)
        def _():
            sums_ref[...] = jnp.zeros_like(sums_ref)

        dxo_v = dxo_ref[...]
        yv = y_ref[...]
        ry = _rs(yv)
        ny = yv * ry
        sums_ref[R_P2:R_P2 + 1, :] += _colsum(dxo_v * ny)
        dyb = _norm_bwd(dxo_v * t_ref[base + R_P2:base + R_P2 + 1, :], ny, ry).astype(BF16)
        dyb_ref[...] = dyb
        dh = jnp.zeros((tm, d), F32)
        for c0 in range(0, nblk, FFN_CHUNK):
            for j in range(c0, min(c0 + FFN_CHUNK, nblk)):
                lo, hi = j * MXU, (j + 1) * MXU
                da = lax.dot_general(dyb, wfo_ref[lo:hi, :], NT, preferred_element_type=F32)
                g, u = gu_ref[:, lo:hi].astype(F32), gu_ref[:, hid + lo:hid + hi].astype(F32)
                sg = _sigmoid(g)
                gs = g * sg
                dgu_ref[:, lo:hi] = (da * u * sg * (1.0 + g * (1.0 - sg))).astype(BF16)
                dgu_ref[:, hid + lo:hid + hi] = (da * gs).astype(BF16)
            lo, hi = c0 * MXU, min(c0 + FFN_CHUNK, nblk) * MXU
            dh = dh + jnp.dot(dgu_ref[:, lo:hi], wfi_ref[lo:hi, :], preferred_element_type=F32)
            dh = dh + jnp.dot(dgu_ref[:, hid + lo:hid + hi], wfi_ref[hid + lo:hid + hi, :],
                              preferred_element_type=F32)
        xv = x_ref[...]
        r = _rs(xv)
        n = xv * r
        sums_ref[R_SH2:R_SH2 + 1, :] += _colsum(dh)
        sums_ref[R_W2:R_W2 + 1, :] += _colsum(dh * n)
        dx_ref[...] = dxo_v + _norm_bwd(dh * t_ref[base + R_W2:base + R_W2 + 1, :], n, r)

    return pl.pallas_call(
        body, name=name, grid=(s // tm,),
        in_specs=[_rows(tm, d), _rows(tm, d), _rows(tm, d), _rows(tm, 2 * hid),
                  _const(tab.shape), _const(wfi.shape), _const(wfo.shape)],
        out_specs=(_rows(tm, d), _rows(tm, d), _rows(tm, 2 * hid), pl.BlockSpec((8, d), lambda i: (0, 0))),
        out_shape=(jax.ShapeDtypeStruct((s, d), F32), jax.ShapeDtypeStruct((s, d), BF16),
                   jax.ShapeDtypeStruct((s, 2 * hid), BF16), jax.ShapeDtypeStruct((8, d), F32)),
        compiler_params=_params(VMEM_BIG),
    )(dxo, x, y, gu, tab, wfi, wfo)


def _attn_out_bwd(dx, y, tab, wo, tm):
    s, d = y.shape
    base = 6

    def body(dx_ref, y_ref, t_ref, wo_ref, dyb_ref, do_ref, sums_ref):
        @pl.when(pl.program_id(0) == 0)
        def _():
            sums_ref[...] = jnp.zeros_like(sums_ref)

        dxv = dx_ref[...]
        yv = y_ref[...]
        ry = _rs(yv)
        ny = yv * ry
        sums_ref[R_P1:R_P1 + 1, :] += _colsum(dxv * ny)
        dyb = _norm_bwd(dxv * t_ref[base + R_P1:base + R_P1 + 1, :], ny, ry).astype(BF16)
        dyb_ref[...] = dyb
        do_ref[...] = lax.dot_general(dyb, wo_ref[...], NT, preferred_element_type=F32).astype(BF16)

    return pl.pallas_call(
        body, name="attn_out_bwd", grid=(s // tm,),
        in_specs=[_rows(tm, d), _rows(tm, d), _const(tab.shape), _const(wo.shape)],
        out_specs=(_rows(tm, d), _rows(tm, d), pl.BlockSpec((8, d), lambda i: (0, 0))),
        out_shape=(jax.ShapeDtypeStruct((s, d), BF16), jax.ShapeDtypeStruct((s, d), BF16),
                   jax.ShapeDtypeStruct((8, d), F32)),
        compiler_params=_params(VMEM_BIG),
    )(dx, y, tab, wo)


def _attn_bwd(q, k, v, o, do, lse, bias):
    s, d = q.shape
    per = ATTN_PER
    npair, nb = d // LANES, s // (per * QB)
    hpp = LANES // HEAD_DIM
    nwin = 2 + per
    nbias = min(per, BIAS_VARIANTS)

    def body(*refs):
        q_ref, k_refs, v_refs = refs[0], refs[1:1 + nwin], refs[1 + nwin:1 + 2 * nwin]
        o_ref, do_ref, lse_ref = refs[1 + 2 * nwin:4 + 2 * nwin]
        bias_refs = refs[4 + 2 * nwin:4 + 2 * nwin + nbias]
        dq_ref, dkb_ref, dvb_ref, db_ref, dk_ref, dv_ref = refs[4 + 2 * nwin + nbias:]
        b = pl.program_id(1)

        @pl.when(b == 0)
        def _():
            dk_ref[...] = jnp.zeros_like(dk_ref)
            dv_ref[...] = jnp.zeros_like(dv_ref)
            db_ref[...] = jnp.zeros_like(db_ref)

        ks = [r[...] for r in k_refs]
        vs = [r[...] for r in v_refs]
        masks = _head_masks()
        for sub in range(per):
            rows = slice(sub * QB, (sub + 1) * QB)
            qv = q_ref[rows, :]
            dov = do_ref[rows, :]
            lsev = lse_ref[rows, :]
            doo = dov.astype(F32) * o_ref[rows, :].astype(F32)
            kwin = jnp.concatenate(ks[sub:sub + 3], axis=0)
            vwin = jnp.concatenate(vs[sub:sub + 3], axis=0)
            dq = jnp.zeros((QB, LANES), F32)
            dkt = jnp.zeros((LANES, KW), F32)
            dvt = jnp.zeros((LANES, KW), F32)
            for hh in range(hpp):
                qm = jnp.where(masks[hh], qv, jnp.zeros_like(qv))
                dom = jnp.where(masks[hh], dov, jnp.zeros_like(dov))
                km = jnp.where(masks[hh], kwin, jnp.zeros_like(kwin))
                lse_h = jnp.max(jnp.where(masks[hh], lsev, NEG), axis=-1, keepdims=True)
                delta = jnp.sum(jnp.where(masks[hh], doo, 0.0), axis=-1, keepdims=True)
                sc = lax.dot_general(qm, kwin, NT, preferred_element_type=F32) + bias_refs[min(sub, nbias - 1)][hh]
                p = jnp.exp(sc - lse_h)
                dp = lax.dot_general(dom, vwin, NT, preferred_element_type=F32)
                ds = p * (dp - delta)
                db_ref[hh] += ds
                dsb = ds.astype(BF16)
                dq = dq + jnp.dot(dsb, km, preferred_element_type=F32)
                dkt = dkt + jnp.dot(qm.T, dsb, preferred_element_type=F32)
                dvt = dvt + jnp.dot(dom.T, p.astype(BF16), preferred_element_type=F32)
            dkw, dvw = dkt.T, dvt.T
            dq_ref[rows, :] = (dq * (HEAD_DIM ** -0.5)).astype(BF16)
            for w in range(3):
                start = pl.multiple_of(jnp.maximum(per * b + sub - 2 + w, 0) * QB, QB)
                dk_ref[pl.ds(start, QB), :] += dkw[w * QB:(w + 1) * QB, :]
                dv_ref[pl.ds(start, QB), :] += dvw[w * QB:(w + 1) * QB, :]

        @pl.when(b == nb - 1)
        def _():
            dkb_ref[...] = dk_ref[...].astype(BF16)
            dvb_ref[...] = dv_ref[...].astype(BF16)

    blk = pl.BlockSpec((per * QB, LANES), lambda p, b: (b, p))
    col = pl.BlockSpec((s, LANES), lambda p, b: (0, p))
    pair = pl.BlockSpec((hpp, QB, KW), lambda p, b: (p, 0, 0))
    return pl.pallas_call(
        body, name="attn_bwd", grid=(npair, nb),
        in_specs=[blk] + _window_specs(per) + _window_specs(per) + [blk, blk, blk]
                 + [_bias_spec(per, sub) for sub in range(nbias)],
        out_specs=(blk, col, col, pair),
        out_shape=(jax.ShapeDtypeStruct((s, d), BF16), jax.ShapeDtypeStruct((s, d), BF16),
                   jax.ShapeDtypeStruct((s, d), BF16), jax.ShapeDtypeStruct(bias.shape[1:], F32)),
        scratch_shapes=[pltpu.VMEM((s, LANES), F32), pltpu.VMEM((s, LANES), F32)],
        compiler_params=_params(VMEM_BIG),
    )(q, *([k] * nwin), *([v] * nwin), o, do, lse, *([bias] * nbias))


def _qkv_bwd(dres, dq, dk, dv, x, tab, wq, wkv, tm):
    s, d = x.shape
    base = 6

    def body(dres_ref, dq_ref, dk_ref, dv_ref, x_ref, t_ref, wq_ref, wkv_ref, dx_ref, sums_ref):
        @pl.when(pl.program_id(0) == 0)
        def _():
            sums_ref[...] = jnp.zeros_like(sums_ref)

        dh1 = lax.dot_general(dq_ref[...], wq_ref[...], NT, preferred_element_type=F32)
        dkv = jnp.concatenate([dk_ref[...], dv_ref[...]], axis=1)
        dhkv = jnp.dot(dkv, wkv_ref[...], preferred_element_type=F32)
        xv = x_ref[...]
        r = _rs(xv)
        n = xv * r
        sums_ref[0:1, :] += _colsum(dh1 * n)
        sums_ref[1:2, :] += _colsum(dh1)
        sums_ref[2:3, :] += _colsum(dhkv * n)
        sums_ref[3:4, :] += _colsum(dhkv)
        dn = dh1 * t_ref[base + R_W1:base + R_W1 + 1, :] + dhkv * t_ref[R_KV:R_KV + 1, :]
        dx_ref[...] = dres_ref[...] + _norm_bwd(dn, n, r)

    return pl.pallas_call(
        body, name="qkv_bwd", grid=(s // tm,),
        in_specs=[_rows(tm, d)] * 5 + [_const(tab.shape), _const(wq.shape), _const(wkv.shape)],
        out_specs=(_rows(tm, d), pl.BlockSpec((8, d), lambda i: (0, 0))),
        out_shape=(jax.ShapeDtypeStruct((s, d), F32), jax.ShapeDtypeStruct((8, d), F32)),
        compiler_params=_params(VMEM_BIG),
    )(dres, dq, dk, dv, x, tab, wq, wkv)


def _conv_bwd(dx1, x, y, bcx, tab, ck, wci, wco, tm):
    s, d = x.shape
    nt = s // tm

    def rev(i):
        return (nt - 1 - i, 0)

    def halo(i):
        return (jnp.maximum((nt - 1 - i) * (tm // 8) - 1, 0), 0)

    def body(dx_ref, x_ref, y_ref, bcx_ref, halo_ref, t_ref, ck_ref, wci_ref, wco_ref,
             dx0_ref, dyb_ref, dbcx_ref, sums_ref, dck_ref, carry):
        i = pl.program_id(0)

        @pl.when(i == 0)
        def _():
            sums_ref[...] = jnp.zeros_like(sums_ref)
            dck_ref[...] = jnp.zeros_like(dck_ref)
            carry[...] = jnp.zeros_like(carry)

        dxv = dx_ref[...]
        yv = y_ref[...]
        ry = _rs(yv)
        ny = yv * ry
        sums_ref[R_P1:R_P1 + 1, :] += _colsum(dxv * ny)
        dyb = _norm_bwd(dxv * t_ref[R_P1:R_P1 + 1, :], ny, ry).astype(BF16)
        dyb_ref[...] = dyb
        du = lax.dot_general(dyb, wco_ref[...], NT, preferred_element_type=F32)
        bg, cg, xi = bcx_ref[:, 0:d], bcx_ref[:, d:2 * d], bcx_ref[:, 2 * d:3 * d]
        z = cg * xi
        zp = halo_ref[:, d:2 * d] * halo_ref[:, 2 * d:3 * d]
        zp = jnp.where(i == nt - 1, jnp.zeros_like(zp), zp)
        row = lax.broadcasted_iota(jnp.int32, z.shape, 0)
        c1, c2 = zp[7:8, :], zp[6:7, :]
        z1 = jnp.where(row == 0, c1, pltpu.roll(z, 1, 0))
        z2 = jnp.where(row == 0, c2, jnp.where(row == 1, c1, pltpu.roll(z, 2, 0)))
        k0, k1, k2 = ck_ref[0:1, :], ck_ref[1:2, :], ck_ref[2:3, :]
        conv = k0 * z2 + k1 * z1 + k2 * z
        dconv = du * bg
        dck_ref[0:1, :] += _colsum(dconv * z2)
        dck_ref[1:2, :] += _colsum(dconv * z1)
        dck_ref[2:3, :] += _colsum(dconv * z)
        n1, n2 = carry[0:1, :], carry[1:2, :]
        d1 = jnp.where(row == tm - 1, n1, pltpu.roll(dconv, tm - 1, 0))
        d2 = jnp.where(row == tm - 1, n2, jnp.where(row == tm - 2, n1, pltpu.roll(dconv, tm - 2, 0)))
        carry[...] = dconv[0:8, :]
        dz = k2 * dconv + k1 * d1 + k0 * d2
        dbcx_ref[:, 0:d] = (du * conv).astype(BF16)
        dbcx_ref[:, d:2 * d] = (dz * xi).astype(BF16)
        dbcx_ref[:, 2 * d:3 * d] = (dz * cg).astype(BF16)
        dh = jnp.dot(dbcx_ref[...], wci_ref[...], preferred_element_type=F32)
        xv = x_ref[...]
        r = _rs(xv)
        n = xv * r
        sums_ref[R_W1:R_W1 + 1, :] += _colsum(dh * n)
        sums_ref[R_SH1:R_SH1 + 1, :] += _colsum(dh)
        dx0_ref[...] = dxv + _norm_bwd(dh * t_ref[R_W1:R_W1 + 1, :], n, r)

    rrow = lambda cols: pl.BlockSpec((tm, cols), rev)
    acc = pl.BlockSpec((8, d), lambda i: (0, 0))
    return pl.pallas_call(
        body, name="conv_bwd", grid=(nt,),
        in_specs=[rrow(d), rrow(d), rrow(d), rrow(3 * d), pl.BlockSpec((8, 3 * d), halo),
                  _const(tab.shape), _const(ck.shape), _const(wci.shape), _const(wco.shape)],
        out_specs=(rrow(d), rrow(d), rrow(3 * d), acc, acc),
        out_shape=(jax.ShapeDtypeStruct((s, d), F32), jax.ShapeDtypeStruct((s, d), BF16),
                   jax.ShapeDtypeStruct((s, 3 * d), BF16), jax.ShapeDtypeStruct((8, d), F32),
                   jax.ShapeDtypeStruct((8, d), F32)),
        scratch_shapes=[pltpu.VMEM((8, d), F32)],
        compiler_params=_params(VMEM_BIG),
    )(dx1, x, y, bcx, bcx, tab, ck, wci, wco)


def _wgrad_wide(a, b, nblk, tk, name):
    s, m = a.shape
    n = b.shape[1] // nblk
    nk = s // tk

    def body(a_ref, b_ref, o_ref, acc):
        kk = pl.program_id(0)

        @pl.when(kk == 0)
        def _():
            acc[...] = jnp.zeros_like(acc)

        acc[...] += jnp.dot(a_ref[...].T, b_ref[...], preferred_element_type=F32)

        @pl.when(kk == nk - 1)
        def _():
            for j in range(nblk):
                o_ref[j] = acc[:, j * n:(j + 1) * n].astype(BF16)

    return pl.pallas_call(
        body, name=name, grid=(nk,),
        in_specs=[pl.BlockSpec((tk, m), lambda kk: (kk, 0)), pl.BlockSpec((tk, nblk * n), lambda kk: (kk, 0))],
        out_specs=pl.BlockSpec((nblk, m, n), lambda kk: (0, 0, 0)),
        out_shape=jax.ShapeDtypeStruct((nblk, m, n), BF16),
        scratch_shapes=[pltpu.VMEM((m, nblk * n), F32)],
        compiler_params=_params(VMEM_BIG),
    )(a, b)


def _wgrad_rows(a, b, ncb, tk, name):
    s, m = a.shape
    n = b.shape[1]
    mb = m // ncb
    nk = s // tk

    def body(a_ref, b_ref, o_ref, acc):
        kk = pl.program_id(1)

        @pl.when(kk == 0)
        def _():
            acc[...] = jnp.zeros_like(acc)

        acc[...] += jnp.dot(a_ref[...].T, b_ref[...], preferred_element_type=F32)

        @pl.when(kk == nk - 1)
        def _():
            o_ref[...] = acc[...].astype(BF16)

    return pl.pallas_call(
        body, name=name, grid=(ncb, nk),
        in_specs=[pl.BlockSpec((tk, mb), lambda j, kk: (kk, j)), pl.BlockSpec((tk, n), lambda j, kk: (kk, 0))],
        out_specs=pl.BlockSpec((mb, n), lambda j, kk: (j, 0)),
        out_shape=jax.ShapeDtypeStruct((m, n), BF16),
        scratch_shapes=[pltpu.VMEM((mb, n), F32)],
        compiler_params=_params(VMEM_BIG),
    )(a, b)


def _wgrad_rows2(a1, a2, b, ncb, tk, name):
    s, m = a1.shape
    n = b.shape[1]
    mb = m // ncb
    nk = s // tk

    def body(a1_ref, a2_ref, b_ref, o_ref, acc):
        j, kk = pl.program_id(0), pl.program_id(1)

        @pl.when(kk == 0)
        def _():
            acc[...] = jnp.zeros_like(acc)

        @pl.when(j < ncb)
        def _():
            acc[...] += jnp.dot(a1_ref[...].T, b_ref[...], preferred_element_type=F32)

        @pl.when(j >= ncb)
        def _():
            acc[...] += jnp.dot(a2_ref[...].T, b_ref[...], preferred_element_type=F32)

        @pl.when(kk == nk - 1)
        def _():
            o_ref[...] = acc[...].astype(BF16)

    return pl.pallas_call(
        body, name=name, grid=(2 * ncb, nk),
        in_specs=[pl.BlockSpec((tk, mb), lambda j, kk: (jnp.where(j < ncb, kk, nk - 1), jnp.minimum(j, ncb - 1))),
                  pl.BlockSpec((tk, mb), lambda j, kk: (jnp.where(j >= ncb, kk, 0), jnp.maximum(j - ncb, 0))),
                  pl.BlockSpec((tk, n), lambda j, kk: (kk, 0))],
        out_specs=pl.BlockSpec((mb, n), lambda j, kk: (j, 0)),
        out_shape=jax.ShapeDtypeStruct((2 * m, n), BF16),
        scratch_shapes=[pltpu.VMEM((mb, n), F32)],
        compiler_params=_params(VMEM_BIG),
    )(a1, a2, b)


def _adamw_math(w, g, m, v):
    m = ADAM_B1 * m + (1.0 - ADAM_B1) * g
    v = ADAM_B2 * v + (1.0 - ADAM_B2) * (g * g)
    m_hat = m / (1.0 - ADAM_B1 ** ADAM_STEP)
    v_hat = v / (1.0 - ADAM_B2 ** ADAM_STEP)
    delta = -ADAM_LR * (m_hat / (jnp.sqrt(v_hat) + ADAM_EPS) + ADAM_WD * w)
    return delta, m, v


def _adamw_reduce(parts, w, m, v, tr, name, parts_t=False):
    nl, r, c = w.shape
    tr = r if (parts_t and r % LANES) else (LANES if parts_t else _row_tile(r, tr))

    def body(*refs):
        p_refs = refs[:nl]
        w_ref, m_ref, v_ref, g_ref, d_ref, mo_ref, vo_ref = refs[nl:]
        layer = pl.program_id(0)

        def partial(i):
            val = p_refs[0][i].astype(F32)
            for q in range(1, nl):
                val = jnp.where(layer == q, p_refs[q][i].astype(F32), val)
            return val

        g = partial(0)
        for i in range(1, N_DEV):
            g = g + partial(i)
        if parts_t:
            g = g.T
        g_ref[...] = g
        d_ref[...], mo_ref[...], vo_ref[...] = _adamw_math(w_ref[...], g, m_ref[...], v_ref[...])

    blk = pl.BlockSpec((None, tr, c), lambda l, i: (l, i, 0))
    out = jax.ShapeDtypeStruct((nl, r, c), F32)
    if parts_t:
        p_specs = [pl.BlockSpec((N_DEV, c, tr), lambda l, i: (0, 0, i))]
    else:
        p_specs = [pl.BlockSpec((N_DEV, tr, c), (lambda l, i, q=q: (0, jnp.where(l == q, i, 0), 0)))
                   for q in range(nl)]
    return pl.pallas_call(
        body, name=name, grid=(nl, r // tr),
        in_specs=p_specs + [blk, blk, blk],
        out_specs=(blk,) * 4, out_shape=(out,) * 4,
        compiler_params=_params(VMEM_BIG),
    )(*parts, w, m, v)


def _adamw_outer(sct, dm, w, m, v, tr, name):
    nl, d, c = w.shape

    def body(s_ref, dm_ref, w_ref, m_ref, v_ref, g_ref, d_ref, mo_ref, vo_ref):
        g = jnp.dot(s_ref[...], dm_ref[...], preferred_element_type=F32)
        g_ref[...] = g
        d_ref[...], mo_ref[...], vo_ref[...] = _adamw_math(w_ref[...], g, m_ref[...], v_ref[...])

    blk = pl.BlockSpec((None, tr, c), lambda l, i: (l, i, 0))
    out = jax.ShapeDtypeStruct((nl, d, c), F32)
    return pl.pallas_call(
        body, name=name, grid=(nl, d // tr),
        in_specs=[pl.BlockSpec((tr, N_DEV), lambda l, i: (i, 0)),
                  pl.BlockSpec((None, N_DEV, c), lambda l, i: (l, 0, 0)), blk, blk, blk],
        out_specs=(blk,) * 4, out_shape=(out,) * 4,
        compiler_params=_params(VMEM_BIG),
    )(sct, dm, w, m, v)


def _pad_rows(a, rows):
    return jnp.concatenate([a, jnp.zeros((rows - a.shape[0],) + a.shape[1:], a.dtype)], axis=0)


def kernel(x, c, mod_w, mod_b, norm_g, ffn_w_in, ffn_w_out, conv_w_in, conv_k, conv_w_out, kv_mod_w, kv_mod_b, kv_norm_g, w_kv, attn_w_q, attn_w_o, rel_bias, loss_target, m_mod_w, m_mod_b, m_norm_g, m_ffn_w_in, m_ffn_w_out, m_conv_w_in, m_conv_k, m_conv_w_out, m_kv_mod_w, m_kv_mod_b, m_kv_norm_g, m_w_kv, m_attn_w_q, m_attn_w_o, m_rel_bias, v_mod_w, v_mod_b, v_norm_g, v_ffn_w_in, v_ffn_w_out, v_conv_w_in, v_conv_k, v_conv_w_out, v_kv_mod_w, v_kv_mod_b, v_kv_norm_g, v_w_kv, v_attn_w_q, v_attn_w_o, v_rel_bias):
    s, d = x.shape[1], x.shape[2]
    dq = d // LANES
    dsh = d // N_DEV
    nl = mod_w.shape[0]
    mw = mod_w.shape[2]
    kmw = kv_mod_w.shape[1]
    nh, nrel = rel_bias.shape[1], rel_bias.shape[2]
    tm = min(256, s)
    tm2 = min(512, s)
    tk = min(2048, s)
    tu = 512
    me = 4 * lax.axis_index("x") + 2 * lax.axis_index("y") + lax.axis_index("c")

    x0 = x[0]
    tgt = loss_target[0]

    small1 = jnp.concatenate([c.reshape(dq, LANES), norm_g.reshape(dq, LANES),
                              _pad_rows(conv_k[0], 8).reshape(dq, LANES)], axis=0)
    (sm,) = _exchange([small1], ["gather"], "gather_small")
    c_all = sm[:, 0:dq].reshape(N_DEV, d)
    ng_full = jnp.transpose(sm[:, dq:2 * dq].reshape(N_DEV, 8, dsh), (1, 0, 2)).reshape(8, d)
    ck_full = jnp.transpose(sm[:, 2 * dq:3 * dq].reshape(N_DEV, 8, dsh), (1, 0, 2)).reshape(8, d)

    modcols, silu_c = _mod_fwd(c_all, mod_w, kv_mod_w)
    (modall,) = _exchange([modcols], ["gather"], "gather_mod")

    cast = lambda *ws: [a.astype(BF16) for a in ws]
    gath = lambda ws: (ws, ["gather"] * len(ws))
    half = lambda ws: (ws, ["gather_half"] * len(ws))
    (h_conv, h_ffn0, h_attn, h_ffn1), token = _xstart(
        [half(cast(conv_w_in[0].T, conv_w_out[0])), half(cast(jnp.swapaxes(ffn_w_in[0], 0, 1), ffn_w_out[0])),
         gath(cast(w_kv.T, attn_w_q[0], attn_w_o[0])), gath(cast(jnp.swapaxes(ffn_w_in[1], 0, 1), ffn_w_out[1]))],
        modall, "gather_start")
    modall = modall + token[0, 0]
    mine = lax.dynamic_index_in_dim(modall, me, axis=1, keepdims=False)
    modrow = jnp.stack([mine[:, l * mw:(l + 1) * mw].reshape(6, d) for l in range(nl)])
    kvrow = mine[:, nl * mw:nl * mw + kmw].reshape(2, d)
    tab, modval = _vec_prep(modrow, mod_b.reshape(nl, 6, d), kvrow, kv_mod_b.reshape(2, d), ng_full,
                            kv_norm_g.reshape(1, d))
    bias = _bias_fwd(rel_bias[0])

    wci, wco = _xwait(h_conv, [bias], "gather_wait_conv")
    wci, wco = _forward_to_sibling([wci, wco], "gather_forward_conv")
    wci, wco = wci.reshape(3 * d, d), wco.reshape(d, d)
    x1, h1a, bcx, ua, ya = _conv_fwd(x0, tab, ck_full, wci, wco, tm2)
    wfi0, wfo0 = _xwait(h_ffn0, [x1], "gather_wait_ffn0")
    wfi0, wfo0 = _forward_to_sibling([wfi0, wfo0], "gather_forward_ffn0")
    wfi0, wfo0 = wfi0.reshape(-1, d), wfo0.reshape(-1, d)
    x2, h2a, gua, aa, y2a = _ffn_fwd(x1, tab, 0, wfi0, wfo0, None, tm2, "ffn_fwd0")
    wkv, wq, wo = _xwait(h_attn, [x2], "gather_wait_attn")
    wkv, wq, wo = wkv.reshape(2 * d, d), wq.reshape(d, d), wo.reshape(d, d)
    hkv, h1b, q, k, v = _qkv_fwd(x2, tab, wq, wkv, tm2)
    o, lse = _attn_fwd(q, k, v, bias)
    x3, yb = _attn_out_fwd(o, x2, tab, wo, tm2)
    wfi1, wfo1 = _xwait(h_ffn1, [x3], "gather_wait_ffn1")
    wfi1, wfo1 = wfi1.reshape(-1, d), wfo1.reshape(-1, d)
    dx4, h2b, gub, ab, y2b, loss_acc = _ffn_fwd(x3, tab, 6, wfi1, wfo1, tgt, tm2, "ffn_fwd1")

    scat = lambda ws: [(ws, ["scatter"] * len(ws))]
    dx3, dy2b, dgub, sums_f1 = _ffn_bwd(dx4, x3, y2b, gub, tab, 6, wfi1, wfo1, tm, "ffn_bwd1")
    g_wfi1 = _wgrad_rows(dgub, h2b, 4, tk, "wgrad_ffn_in1").reshape(N_DEV, -1, d)
    g_wfo1 = _wgrad_rows(ab, dy2b, 2, tk, "wgrad_ffn_out1").reshape(N_DEV, -1, d)
    (h_g1,), token = _xstart(scat([g_wfi1, g_wfo1]), dx3, "grads_start_ffn1")
    tab = tab + token[0, 0]
    dyb, do, sums_o = _attn_out_bwd(dx3, yb, tab, wo, tm2)
    g_wo = _wgrad_wide(o, dyb, 1, tk, "wgrad_o").reshape(N_DEV, dsh, d)
    dqb, dk, dv, dbias = _attn_bwd(q, k, v, o, do, lse, bias)
    g_wq = _wgrad_wide(h1b, dqb, 1, tk, "wgrad_q").reshape(N_DEV, dsh, d)
    dx2, sums_q = _qkv_bwd(dx3, dqb, dk, dv, x2, tab, wq, wkv, tm2)
    g_wkv = _wgrad_rows2(dk, dv, hkv, 1, tk, "wgrad_kv").reshape(N_DEV, -1, d)
    (h_g2,), token = _xstart(scat([g_wkv, g_wq, g_wo]), dx2, "grads_start_attn")
    tab = tab + token[0, 0]
    dx1, dy2a, dgua, sums_f0 = _ffn_bwd(dx2, x1, y2a, gua, tab, 0, wfi0, wfo0, tm, "ffn_bwd0")
    g_wfi0 = _wgrad_rows(dgua, h2a, 4, tk, "wgrad_ffn_in0").reshape(N_DEV, -1, d)
    g_wfo0 = _wgrad_rows(aa, dy2a, 2, tk, "wgrad_ffn_out0").reshape(N_DEV, -1, d)
    (h_g3,), token = _xstart(scat([g_wfi0, g_wfo0]), dx1, "grads_start_ffn0")
    tab = tab + token[0, 0]
    dx0, dya, dbcx, sums_c, dck = _conv_bwd(dx1, x0, ya, bcx, tab, ck_full, wci, wco, tm2)
    drel = _bias_bwd(dbias, nrel)
    dmod, dng, dkvg = _vec_bwd(sums_c, sums_f0, sums_q, sums_o, sums_f1, modval, ng_full, kv_norm_g.reshape(1, d))

    relw = -(-nrel // LANES) * LANES
    drel_p = jnp.concatenate([drel, jnp.zeros((nh, relw - nrel), F32)], axis=1)
    small3 = jnp.concatenate([dmod.reshape(16 * dq, LANES), dng.reshape(8 * dq, LANES), dkvg.reshape(8 * dq, LANES),
                              dck.reshape(8 * dq, LANES), loss_acc,
                              drel_p.reshape(nh * relw // LANES, LANES)], axis=0)
    (sm,) = _exchange([small3], ["gather"], "gather_small_grads")
    g_wci = _wgrad_rows(dbcx, h1a, 3, tk, "wgrad_conv_in").reshape(N_DEV, -1, d)
    g_wco = _wgrad_wide(ua, dya, 1, tk, "wgrad_conv_out").reshape(N_DEV, dsh, d)
    (h_g4,), token = _xstart(scat([g_wci, g_wco]), sm, "grads_start_conv")
    sm = sm + token[0, 0]
    o1, o2, o3, o4, o5 = 16 * dq, 24 * dq, 32 * dq, 40 * dq, 40 * dq + 8
    loss = jnp.sum(sm[:, o4:o5, :]) * (0.5 / d)
    dmod_all = sm[:, 0:o1].reshape(N_DEV, 16, d)
    mine_cols = lambda a: lax.dynamic_slice_in_dim(a, me * dsh, dsh, axis=2)
    dng_parts = mine_cols(sm[:, o1:o2].reshape(N_DEV, 8, d))
    dkvg_parts = sm[:, o2:o3].reshape(N_DEV, 8, d)[:, 0:1]
    dck_parts = mine_cols(sm[:, o3:o4].reshape(N_DEV, 8, d))[:, 0:3]
    drel_parts = sm[:, o5:].reshape(N_DEV, nh, relw)[:, :, 0:nrel]

    def update(parts, w, m, v, name, layers=1, parts_t=False):
        shp = w.shape
        w3, m3, v3 = (a.reshape(layers, -1, shp[-1]) for a in (w, m, v))
        if not parts_t:
            parts = [p.reshape(N_DEV, -1, shp[-1]) for p in parts]
        outs = _adamw_reduce(parts, w3, m3, v3, tu, name, parts_t)
        return [a.reshape(shp) for a in outs]

    p_wfi1, p_wfo1 = _xwait(h_g1, [sm], "grads_wait_ffn1")
    p_wfi0, p_wfo0 = _xwait(h_g3, [p_wfi1], "grads_wait_ffn0")
    tr = lambda a: jnp.swapaxes(a, 1, 2)
    u_ffn_in = [tr(a) for a in update([p_wfi0, p_wfi1], tr(ffn_w_in), tr(m_ffn_w_in), tr(v_ffn_w_in),
                                      "adamw_ffn_in", 2)]
    u_ffn_out = update([p_wfo0, p_wfo1], ffn_w_out, m_ffn_w_out, v_ffn_w_out, "adamw_ffn_out", 2)
    p_wkv, p_wq, p_wo = _xwait(h_g2, [u_ffn_out[0]], "grads_wait_attn")
    u_w_kv = update([p_wkv], w_kv, m_w_kv, v_w_kv, "adamw_w_kv", parts_t=True)
    u_w_q = update([p_wq], attn_w_q, m_attn_w_q, v_attn_w_q, "adamw_w_q")
    u_w_o = update([p_wo], attn_w_o, m_attn_w_o, v_attn_w_o, "adamw_w_o")

    sct = jnp.transpose(silu_c)
    dm_mod = jnp.stack([lax.dynamic_slice_in_dim(dmod_all[:, 6 * l:6 * l + 6].reshape(N_DEV, 6 * d), me * mw, mw, axis=1)
                        for l in range(nl)]).astype(BF16)
    dm_kv = lax.dynamic_slice_in_dim(dmod_all[:, R_KV:R_KV + 2].reshape(N_DEV, 2 * d), me * kmw, kmw, axis=1)
    u_mod_w = _adamw_outer(sct, dm_mod, mod_w, m_mod_w, v_mod_w, min(tu, d), "adamw_mod_w")
    u_kv_mod_w = [a[0] for a in _adamw_outer(sct, dm_kv.astype(BF16)[None], kv_mod_w[None], m_kv_mod_w[None],
                                             v_kv_mod_w[None], min(tu, d), "adamw_kv_mod_w")]

    modb_parts = jnp.stack([dmod_all[:, 6 * l:6 * l + 6].reshape(N_DEV, 6 * d) for l in range(nl)], axis=1)
    u_mod_b = update([modb_parts], mod_b, m_mod_b, v_mod_b, "adamw_mod_b")
    u_norm_g = update([dng_parts], norm_g.reshape(8, dsh), m_norm_g.reshape(8, dsh), v_norm_g.reshape(8, dsh), "adamw_norm_g")
    u_norm_g = [a.reshape(norm_g.shape) for a in u_norm_g]
    u_conv_k = update([dck_parts], conv_k, m_conv_k, v_conv_k, "adamw_conv_k")
    kvb_parts = dmod_all[:, R_KV:R_KV + 2].reshape(N_DEV, 1, 2 * d)
    u_kv_mod_b = [a.reshape(kv_mod_b.shape) for a in update([kvb_parts], kv_mod_b.reshape(1, -1), m_kv_mod_b.reshape(1, -1),
                                                            v_kv_mod_b.reshape(1, -1), "adamw_kv_mod_b")]
    u_kv_norm_g = [a.reshape(kv_norm_g.shape) for a in update([dkvg_parts], kv_norm_g.reshape(1, -1), m_kv_norm_g.reshape(1, -1),
                                                              v_kv_norm_g.reshape(1, -1), "adamw_kv_norm_g")]
    u_rel = update([drel_parts], rel_bias, m_rel_bias, v_rel_bias, "adamw_rel_bias")

    others = [u_ffn_in, u_ffn_out, u_w_kv, u_w_q, u_w_o, u_mod_w, u_kv_mod_w, u_mod_b, u_norm_g, u_conv_k, u_kv_mod_b,
              u_kv_norm_g, u_rel]
    p_wci, p_wco = _xwait(h_g4, [u[3] for u in others], "grads_wait_conv")
    u_conv_in = update([p_wci], conv_w_in, m_conv_w_in, v_conv_w_in, "adamw_conv_in", parts_t=True)
    u_conv_out = update([p_wco], conv_w_out, m_conv_w_out, v_conv_w_out, "adamw_conv_out")

    ups = [u_mod_w, u_mod_b, u_norm_g, u_ffn_in, u_ffn_out, u_conv_in, u_conv_k, u_conv_out, u_kv_mod_w, u_kv_mod_b,
           u_kv_norm_g, u_w_kv, u_w_q, u_w_o, u_rel]
    return (loss, dx0[None], *[u[0] for u in ups], *[u[1] for u in ups], *[u[2] for u in ups], *[u[3] for u in ups])
```
